```python
import jax, jax.numpy as jnp
from jax import lax
import numpy as np

D_MODEL = 1024
BATCH = 16
SEQ = 256
DEPTH = 2
DEC_BATCH = 2
DEC_SEQ = 2048
PAST_LEN = 512

GRID_W = 64
ROPE_BASE = 10000.0
Q_BLOCK = 128
WINDOW = 128
NEG_INF = -1e30

A_HEADS = 4
A_KV_HEADS = 2
A_GROUP = A_HEADS // A_KV_HEADS
A_HD = 64
B_HEADS = 4
B_NOPE = 64
B_ROPE = 32
B_VD = 64
B_Q_LORA = 192
B_KV_LORA = 128
C_HEADS = 4
C_DK = 32
C_DV = 64
C_GATE_RANK = 16
C_GATE_TEMP = 16.0
C_CHUNK = 64
D_HEADS = 4
D_N = 64
D_DECAY_RANK = 64
D_AAA_RANK = 64
D_GATE_RANK = 128
D_GN_EPS = 64e-5

BRANCH_W = 256
N_BRANCH = 4
A_COLS = A_HEADS * A_HD + 2 * A_KV_HEADS * A_HD
B_COLS = B_Q_LORA + B_KV_LORA + B_ROPE
C_COLS = 2 * C_HEADS * C_DK + 2 * C_HEADS * C_DV + 2 * C_GATE_RANK
D_COLS = 3 * D_HEADS * D_N + 2 * D_DECAY_RANK + 2 * D_AAA_RANK + D_GATE_RANK
G_COLS = N_BRANCH * D_MODEL
IN_COLS = A_COLS + B_COLS + C_COLS + D_COLS + G_COLS

N_EXPERTS = 32
TOP_K = 4
D_FF = D_MODEL
SWIGLU_LIMIT = 7.0
SWIGLU_ALPHA = 1.702

ALPHA = (2 * DEPTH) ** 0.25
BETA = (8 * DEPTH) ** -0.25

kernel_name = "hybrid_diffusion_prefix_step"

F32 = jnp.float32


def split_cols(z, sizes):
    idx = np.cumsum(np.array(sizes))[:-1].tolist()
    return jnp.split(z, idx, axis=-1)


def layer_norm(x, g, b, eps=1e-5):
    xf = x.astype(F32)
    mu = jnp.mean(xf, -1, keepdims=True)
    var = jnp.mean(jnp.square(xf - mu), -1, keepdims=True)
    return ((xf - mu) * lax.rsqrt(var + eps)).astype(x.dtype) * g + b


def rms_norm(x, g, eps=1e-6):
    xf = x.astype(F32)
    return (xf * lax.rsqrt(jnp.mean(jnp.square(xf), -1, keepdims=True) + eps)).astype(x.dtype) * g


def to_heads(x, n):
    b, t, _ = x.shape
    return x.reshape(b, t, n, -1).transpose(0, 2, 1, 3)


def from_heads(x):
    b, n, t, d = x.shape
    return x.transpose(0, 2, 1, 3).reshape(b, t, n * d)


def axial_rope(n_tok, rot_dim):
    rows = n_tok // GRID_W
    row = jnp.repeat(jnp.arange(rows, dtype=F32), GRID_W)
    col = jnp.tile(jnp.arange(GRID_W, dtype=F32), rows)
    quarter = rot_dim // 4
    inv = ROPE_BASE ** (-jnp.arange(quarter, dtype=F32) / quarter)
    ang = jnp.concatenate([row[:, None] * inv, col[:, None] * inv], axis=-1)
    return jnp.cos(ang), jnp.sin(ang)


def apply_axial_rope(x, cos, sin):
    q = x.shape[-1] // 4
    cos = cos.astype(x.dtype)
    sin = sin.astype(x.dtype)
    cr, cc, sr, sc = cos[:, :q], cos[:, q:], sin[:, :q], sin[:, q:]
    p0, p1, p2, p3 = x[..., :q], x[..., q:2 * q], x[..., 2 * q:3 * q], x[..., 3 * q:]
    return jnp.concatenate([p0 * cr - p1 * sr, p1 * cr + p0 * sr,
                            p2 * cc - p3 * sc, p3 * cc + p2 * sc], axis=-1)


def attn_probs(s, sink):
    if sink is None:
        return jax.nn.softmax(s, axis=-1)
    sk = sink.astype(F32)[None, :, :, None, None]
    m = jnp.maximum(jnp.max(s, -1, keepdims=True), sk)
    e = jnp.exp(s - m)
    return e / (jnp.sum(e, -1, keepdims=True) + jnp.exp(sk - m))


def blocked_attention(q, k, v, scale, sink):
    b, kvh, g, t, d = q.shape
    nb = t // Q_BLOCK
    qb = jnp.moveaxis(q.reshape(b, kvh, g, nb, Q_BLOCK, d), 3, 0)

    def one_block(qi):
        s = jnp.einsum('bkgqd,bksd->bkgqs', qi, k).astype(F32) * scale
        p = attn_probs(s, sink)
        return jnp.einsum('bkgqs,bksv->bkgqv', p.astype(v.dtype), v)

    o = lax.map(one_block, qb)
    return jnp.moveaxis(o, 0, 3).reshape(b, kvh, g, t, v.shape[-1])


def window_context_attention(q, k, v, k_ctx, v_ctx, scale, sink):
    b, kvh, g, t, d = q.shape
    nb = t // Q_BLOCK
    pad = ((0, 0), (0, 0), (Q_BLOCK, Q_BLOCK), (0, 0))

    def bands(x):
        xb = jnp.pad(x, pad).reshape(b, kvh, nb + 2, Q_BLOCK, x.shape[-1])
        return jnp.moveaxis(jnp.concatenate([xb[:, :, :-2], xb[:, :, 1:-1], xb[:, :, 2:]], axis=3), 2, 0)

    kw, vw = bands(k), bands(v)
    qb = jnp.moveaxis(q.reshape(b, kvh, g, nb, Q_BLOCK, d), 3, 0)
    qpos = jnp.arange(nb)[:, None] * Q_BLOCK + jnp.arange(Q_BLOCK)[None]
    kpos = jnp.arange(nb)[:, None] * Q_BLOCK - Q_BLOCK + jnp.arange(3 * Q_BLOCK)[None]
    mask = ((jnp.abs(qpos[:, :, None] - kpos[:, None, :]) <= WINDOW)
            & (kpos[:, None, :] >= 0) & (kpos[:, None, :] < t))

    def one_block(xs):
        qi, ki, vi, mi = xs
        s_win = jnp.where(mi, jnp.einsum('bkgqd,bksd->bkgqs', qi, ki).astype(F32) * scale, NEG_INF)
        s_ctx = jnp.einsum('bkgqd,bksd->bkgqs', qi, k_ctx).astype(F32) * scale
        p = attn_probs(jnp.concatenate([s_win, s_ctx], axis=-1), sink)
        return jnp.einsum('bkgqs,bksv->bkgqv', p.astype(v.dtype), jnp.concatenate([vi, v_ctx], axis=2))

    o = lax.map(one_block, (qb, kw, vw, mask))
    return jnp.moveaxis(o, 0, 3).reshape(b, kvh, g, t, v.shape[-1])


def gla_chunked(q, k, v, log_a, s0):
    b, h, t, dk = q.shape
    dv = v.shape[-1]
    nc = t // C_CHUNK

    def chunks(x):
        return jnp.moveaxis(x.astype(F32).reshape(b, h, nc, C_CHUNK, x.shape[-1]), 2, 0)

    tril = jnp.tril(jnp.ones((C_CHUNK, C_CHUNK), dtype=bool))

    def step(S, xs):
        qc, kc, vc, lac = xs
        cb = jnp.cumsum(lac, axis=-2)
        inter = jnp.einsum('bhtd,bhdv->bhtv', qc * jnp.exp(cb), S)
        diff = jnp.where(tril[:, :, None], cb[..., :, None, :] - cb[..., None, :, :], -jnp.inf)
        att = jnp.einsum('bhtd,bhsd,bhtsd->bhts', qc, kc, jnp.exp(diff))
        intra = jnp.einsum('bhts,bhsv->bhtv', att, vc)
        cend = cb[..., -1:, :]
        S = jnp.exp(cend[..., 0, :])[..., None] * S + jnp.einsum('bhsd,bhsv->bhdv', kc * jnp.exp(cend - cb), vc)
        return S, inter + intra

    S, o = lax.scan(step, s0.astype(F32), (chunks(q), chunks(k), chunks(v), chunks(log_a)))
    return jnp.moveaxis(o, 0, 2).reshape(b, h, t, dv).astype(v.dtype), S


def rwkv7_scan(r, w, k, v, kk, a, s0):
    def tm(x):
        return jnp.moveaxis(x.astype(F32), 2, 0)

    def step(S, xs):
        rt, wt, kt, vt, kkt, at = xs
        sa = jnp.einsum('bhvk,bhk->bhv', S, -kkt)
        S = S * wt[:, :, None, :] + sa[..., None] * (kkt * at)[:, :, None, :] + vt[..., None] * kt[:, :, None, :]
        return S, jnp.einsum('bhvk,bhk->bhv', S, rt)

    S, y = lax.scan(step, s0.astype(F32), tuple(tm(x) for x in (r, w, k, v, kk, a)))
    return jnp.moveaxis(y, 0, 2).astype(v.dtype), S


def centred_shift(x):
    xp = jnp.pad(x, ((0, 0), (1, 1), (0, 0)))
    return 0.5 * (xp[:, :-2] + xp[:, 2:])


def mla_keys_values(lp, ckv, kpe):
    bn, s, _ = ckv.shape
    kv = to_heads(ckv @ lp['b_w_ukv'], B_HEADS)
    k_pe = jnp.broadcast_to(kpe[:, None], (bn, B_HEADS, s, B_ROPE))
    return jnp.concatenate([kv[..., :B_NOPE], k_pe], axis=-1), kv[..., B_NOPE:]


def token_mixers(lp, h, ctx):
    latent = ctx is not None
    bn, t, _ = h.shape
    z = jnp.einsum('btd,de->bte', h, lp['w_in'])
    za, zb, zc, zd, zg = split_cols(z, (A_COLS, B_COLS, C_COLS, D_COLS, G_COLS))

    a_q, a_k, a_v = split_cols(za, (A_HEADS * A_HD, A_KV_HEADS * A_HD, A_KV_HEADS * A_HD))
    a_q = a_q.reshape(bn, t, A_KV_HEADS, A_GROUP, A_HD).transpose(0, 2, 3, 1, 4)
    a_k = to_heads(a_k, A_KV_HEADS)
    a_v = to_heads(a_v, A_KV_HEADS)
    a_sink = lp['a_sink'].reshape(A_KV_HEADS, A_GROUP)
    if latent:
        cos, sin = axial_rope(t, A_HD)
        o_a = window_context_attention(apply_axial_rope(a_q, cos, sin), apply_axial_rope(a_k, cos, sin),
                                       a_v, ctx[0], ctx[1], A_HD ** -0.5, a_sink)
    else:
        o_a = blocked_attention(a_q, a_k, a_v, A_HD ** -0.5, a_sink)
    o_a = o_a.transpose(0, 3, 1, 2, 4).reshape(bn, t, BRANCH_W)

    b_cq, b_ckv, b_kpe = split_cols(zb, (B_Q_LORA, B_KV_LORA, B_ROPE))
    b_q = to_heads(rms_norm(b_cq, lp['b_q_norm']) @ lp['b_w_uq'], B_HEADS)
    b_ckv = rms_norm(b_ckv, lp['b_kv_norm'])
    if latent:
        cos, sin = axial_rope(t, B_ROPE)
        b_q = jnp.concatenate([b_q[..., :B_NOPE], apply_axial_rope(b_q[..., B_NOPE:], cos, sin)], axis=-1)
        k_l, v_l = mla_keys_values(lp, b_ckv, apply_axial_rope(b_kpe, cos, sin))
        k_c, v_c = mla_keys_values(lp, ctx[2], ctx[3])
        b_k = jnp.concatenate([k_c, k_l], axis=2)
        b_v = jnp.concatenate([v_c, v_l], axis=2)
    else:
        b_k, b_v = mla_keys_values(lp, b_ckv, b_kpe)
    o_b = blocked_attention(b_q[:, :, None], b_k, b_v, (B_NOPE + B_ROPE) ** -0.5, None)
    o_b = from_heads(o_b[:, :, 0])

    c_q, c_k, c_v, c_og, c_af, c_ab = split_cols(
        zc, (C_HEADS * C_DK, C_HEADS * C_DK, C_HEADS * C_DV, C_HEADS * C_DV, C_GATE_RANK, C_GATE_RANK))
    cq = to_heads(c_q, C_HEADS) * (C_DK ** -0.5)
    ck = to_heads(c_k, C_HEADS)
    cv = to_heads(c_v, C_HEADS)
    c_s0 = ctx[4] if latent else jnp.zeros((bn, 2, C_HEADS, C_DK, C_DV), F32)
    c_outs, c_states = [], []
    for direction, lr in enumerate((c_af, c_ab)):
        log_a = to_heads(jax.nn.log_sigmoid(
            (lr @ lp['c_w_gate'][direction] + lp['c_b_gate'][direction]).astype(F32)) / C_GATE_TEMP, C_HEADS)
        xs = (cq, ck, cv, log_a)
        if direction == 1:
            xs = tuple(jnp.flip(x, axis=2) for x in xs)
        o, s_fin = gla_chunked(*xs, c_s0[:, direction])
        c_outs.append(jnp.flip(o, axis=2) if direction == 1 else o)
        c_states.append(s_fin)
    o_c = from_heads(rms_norm(c_outs[0] + c_outs[1], lp['c_norm'])) * jax.nn.silu(c_og)

    zd = zd + (centred_shift(zd) - zd) * lp['d_mu']
    hn = D_HEADS * D_N
    d_r, d_k, d_v, d_wf, d_wb, d_af, d_ab, d_g = split_cols(
        zd, (hn, hn, hn, D_DECAY_RANK, D_DECAY_RANK, D_AAA_RANK, D_AAA_RANK, D_GATE_RANK))
    kk = to_heads(d_k * lp['d_k_k'], D_HEADS).astype(F32)
    kk = kk / jnp.maximum(jnp.sqrt(jnp.sum(jnp.square(kk), -1, keepdims=True)), 1e-12)
    r_h = to_heads(d_r, D_HEADS)
    v_h = to_heads(d_v, D_HEADS)
    d_s0 = ctx[5] if latent else jnp.zeros((bn, 2, D_HEADS, D_N, D_N), F32)
    d_y, d_states, bonus = 0.0, [], 0.0
    for direction, (lw, la) in enumerate(((d_wf, d_af), (d_wb, d_ab))):
        w_log = -jax.nn.softplus(-(lp['d_w0'][direction] + jnp.tanh(lw) @ lp['d_w2'][direction]).astype(F32)) - 0.5
        decay = to_heads(jnp.exp(-jnp.exp(w_log)), D_HEADS)
        a = jax.nn.sigmoid(lp['d_a0'][direction] + la @ lp['d_a2'][direction])
        k_dir = to_heads(d_k * (1.0 + (a - 1.0) * lp['d_k_a']), D_HEADS)
        a_h = to_heads(a, D_HEADS)
        xs = (r_h, decay, k_dir, v_h, kk, a_h)
        if direction == 1:
            xs = tuple(jnp.flip(x, axis=2) for x in xs)
        y, s_fin = rwkv7_scan(*xs, d_s0[:, direction])
        d_y = d_y + (jnp.flip(y, axis=2) if direction == 1 else y)
        d_states.append(s_fin)
        bonus = bonus + jnp.sum(r_h * k_dir * lp['d_r_k'][None, :, None, :], -1, keepdims=True) * v_h
    yf = d_y.astype(F32)
    mu = jnp.mean(yf, -1, keepdims=True)
    var = jnp.mean(jnp.square(yf - mu), -1, keepdims=True)
    y_gn = from_heads(((yf - mu) * lax.rsqrt(var + D_GN_EPS)).astype(d_y.dtype)) * lp['d_ln_g'] + lp['d_ln_b']
    o_d = (y_gn + from_heads(bonus)) * (jax.nn.sigmoid(d_g) @ lp['d_g2'])

    gates = jax.nn.sigmoid(zg).reshape(bn, t, N_BRANCH, D_MODEL)
    branches = jnp.stack([o_a, o_b, o_c, o_d], axis=2)
    proj = jnp.einsum('btnw,nwd->btnd', branches, lp['w_br'])
    out = jnp.sum(gates * proj, axis=2) @ lp['w_out']
    if latent:
        return out, None
    return out, (a_k, a_v, b_ckv, b_kpe, jnp.stack(c_states, axis=1), jnp.stack(d_states, axis=1))


def routed_experts(lp, h):
    bn, t, d = h.shape
    xt = h.reshape(bn * t, d)
    logits = (xt @ lp['w_router'] + lp['b_router']).astype(F32)
    top_v, top_i = lax.top_k(logits, TOP_K)
    top_w = jax.nn.softmax(top_v, axis=-1)
    combine = jnp.einsum('nk,nke->en', top_w, jax.nn.one_hot(top_i, N_EXPERTS, dtype=F32))

    def add_expert(acc, xs):
        w1, b1, w2, b2, cw = xs
        u = xt @ w1 + b1
        glu = jnp.minimum(u[:, 0::2], SWIGLU_LIMIT)
        lin = jnp.clip(u[:, 1::2], -SWIGLU_LIMIT, SWIGLU_LIMIT)
        y = (glu * jax.nn.sigmoid(SWIGLU_ALPHA * glu) * (lin + 1.0)) @ w2 + b2
        return acc + cw[:, None].astype(y.dtype) * y, None

    acc, _ = lax.scan(add_expert, jnp.zeros_like(xt),
                      (lp['w_mlp1'], lp['b_mlp1'], lp['w_mlp2'], lp['b_mlp2'], combine))
    return acc.reshape(bn, t, d)


def trunk_layer(lp, x, cond, ctx):
    mod = (jax.nn.silu(cond) @ lp['w_mod'] + lp['b_mod'])[:, None, :]
    sh1, sc1, g1, sh2, sc2, g2 = jnp.split(mod, 6, axis=-1)
    mix, ctx_out = token_mixers(lp, x * (1.0 + sc1) + sh1, ctx)
    x = layer_norm(ALPHA * x + g1 * mix, lp['ln_g'][0], lp['ln_b'][0])
    x = layer_norm(ALPHA * x + g2 * routed_experts(lp, x * (1.0 + sc2) + sh2), lp['ln_g'][1], lp['ln_b'][1])
    return x, ctx_out


def setup_inputs(seed: int = 0) -> dict:
    key = jax.random.key(seed)
    ks = iter(jax.random.split(key, 48))
    L, D, HN, E, F = DEPTH, D_MODEL, D_HEADS * D_N, N_EXPERTS, D_FF

    def nrm(shape, scale):
        return scale * jax.random.normal(next(ks), shape, F32)

    def gain(shape):
        return 1.0 + nrm(shape, 0.01)

    return {
        "x_prompt": nrm((BATCH, SEQ, D), 1.0),
        "x_sample": nrm((DEC_BATCH, DEC_SEQ, D), 1.0),
        "cache_a_k": nrm((DEC_BATCH, L, A_KV_HEADS, PAST_LEN, A_HD), 1.0),
        "cache_a_v": nrm((DEC_BATCH, L, A_KV_HEADS, PAST_LEN, A_HD), 1.0),
        "cache_b_ckv": nrm((DEC_BATCH, L, PAST_LEN, B_KV_LORA), 1.0),
        "cache_b_kpe": nrm((DEC_BATCH, L, PAST_LEN, B_ROPE), 1.0),
        "state_c": nrm((DEC_BATCH, L, 2, C_HEADS, C_DK, C_DV), 0.5),
        "state_d": nrm((DEC_BATCH, L, 2, D_HEADS, D_N, D_N), 0.5),
        "c": nrm((DEC_BATCH, D), 1.0),
        "c_ctx": nrm((D,), 1.0),
        "w_mod": nrm((L, D, 6 * D), 0.5 * D ** -0.5),
        "b_mod": nrm((L, 6 * D), 0.01),
        "w_in": nrm((L, D, IN_COLS), D ** -0.5),
        "a_sink": nrm((L, A_HEADS), 0.5),
        "b_q_norm": gain((L, B_Q_LORA)),
        "b_w_uq": nrm((L, B_Q_LORA, B_HEADS * (B_NOPE + B_ROPE)), B_Q_LORA ** -0.5),
        "b_kv_norm": gain((L, B_KV_LORA)),
        "b_w_ukv": nrm((L, B_KV_LORA, B_HEADS * (B_NOPE + B_VD)), B_KV_LORA ** -0.5),
        "c_w_gate": nrm((L, 2, C_GATE_RANK, C_HEADS * C_DK), C_GATE_RANK ** -0.5),
        "c_b_gate": nrm((L, 2, C_HEADS * C_DK), 0.5),
        "c_norm": gain((L, C_DV)),
        "d_mu": jax.random.uniform(next(ks), (L, D_COLS), F32),
        "d_w0": nrm((L, 2, HN), 0.5),
        "d_w2": nrm((L, 2, D_DECAY_RANK, HN), 0.1),
        "d_a0": nrm((L, 2, HN), 0.5),
        "d_a2": nrm((L, 2, D_AAA_RANK, HN), 0.1),
        "d_g2": nrm((L, D_GATE_RANK, HN), D_GATE_RANK ** -0.5),
        "d_k_k": 0.85 + nrm((L, HN), 0.02),
        "d_k_a": 1.0 + nrm((L, HN), 0.02),
        "d_r_k": nrm((L, D_HEADS, D_N), 0.1),
        "d_ln_g": gain((L, HN)),
        "d_ln_b": nrm((L, HN), 0.01),
        "w_br": nrm((L, N_BRANCH, BRANCH_W, D), BETA * BRANCH_W ** -0.5),
        "w_out": nrm((L, D, D), BETA * D ** -0.5),
        "ln_g": gain((L, 2, D)),
        "ln_b": nrm((L, 2, D), 0.01),
        "w_router": nrm((L, D, E), D ** -0.5),
        "b_router": nrm((L, E), 0.01),
        "w_mlp1": nrm((L, E, D, 2 * F), D ** -0.5),
        "b_mlp1": nrm((L, E, 2 * F), 0.01),
        "w_mlp2": nrm((L, E, F, D), BETA * F ** -0.5),
        "b_mlp2": nrm((L, E, D), 0.01),
    }


def reference(x_prompt, x_sample, cache_a_k, cache_a_v, cache_b_ckv, cache_b_kpe, state_c, state_d, c,
              c_ctx, w_mod, b_mod, w_in, a_sink, b_q_norm, b_w_uq, b_kv_norm, b_w_ukv, c_w_gate, c_b_gate,
              c_norm, d_mu, d_w0, d_w2, d_a0, d_a2, d_g2, d_k_k, d_k_a, d_r_k, d_ln_g, d_ln_b, w_br, w_out,
              ln_g, ln_b, w_router, b_router, w_mlp1, b_mlp1, w_mlp2, b_mlp2):
    layers = [dict(w_mod=w_mod[l], b_mod=b_mod[l], w_in=w_in[l], a_sink=a_sink[l], b_q_norm=b_q_norm[l],
                   b_w_uq=b_w_uq[l], b_kv_norm=b_kv_norm[l], b_w_ukv=b_w_ukv[l], c_w_gate=c_w_gate[l],
                   c_b_gate=c_b_gate[l], c_norm=c_norm[l], d_mu=d_mu[l], d_w0=d_w0[l], d_w2=d_w2[l],
                   d_a0=d_a0[l], d_a2=d_a2[l], d_g2=d_g2[l], d_k_k=d_k_k[l], d_k_a=d_k_a[l], d_r_k=d_r_k[l],
                   d_ln_g=d_ln_g[l], d_ln_b=d_ln_b[l], w_br=w_br[l], w_out=w_out[l], ln_g=ln_g[l], ln_b=ln_b[l],
                   w_router=w_router[l], b_router=b_router[l], w_mlp1=w_mlp1[l], b_mlp1=b_mlp1[l],
                   w_mlp2=w_mlp2[l], b_mlp2=b_mlp2[l])
              for l in range(DEPTH)]

    y_prompt = x_prompt
    ctx_layers = []
    for l in range(DEPTH):
        y_prompt, ctx_out = trunk_layer(layers[l], y_prompt, c_ctx[None], None)
        ctx_layers.append(ctx_out)
    new_a_k = jnp.stack([t[0] for t in ctx_layers], axis=1)
    new_a_v = jnp.stack([t[1] for t in ctx_layers], axis=1)
    new_b_ckv = jnp.stack([t[2] for t in ctx_layers], axis=1)
    new_b_kpe = jnp.stack([t[3] for t in ctx_layers], axis=1)
    new_state_c = jnp.stack([t[4] for t in ctx_layers], axis=1)
    new_state_d = jnp.stack([t[5] for t in ctx_layers], axis=1)

    y_sample = x_sample
    for l in range(DEPTH):
        ctx = (cache_a_k[:, l], cache_a_v[:, l], cache_b_ckv[:, l], cache_b_kpe[:, l], state_c[:, l], state_d[:, l])
        y_sample, _ = trunk_layer(layers[l], y_sample, c, ctx)

    return (y_prompt, y_sample, new_a_k, new_a_v, new_b_ckv, new_b_kpe, new_state_c, new_state_d)
```

```python
import functools

import jax
import jax.numpy as jnp
import numpy as np
from jax import lax
from jax.experimental import pallas as pl
from jax.experimental.pallas import tpu as pltpu

F32 = jnp.float32
BF16 = jnp.bfloat16

V7X_VMEM_BYTES = 64 * 1024 * 1024
LANES = 128


def _split3(x):
    hi = x.astype(BF16)
    r1 = x - hi.astype(F32)
    mid = r1.astype(BF16)
    lo = (r1 - mid.astype(F32)).astype(BF16)
    return hi, mid, lo


def _split2(x):
    hi = x.astype(BF16)
    lo = (x - hi.astype(F32)).astype(BF16)
    return hi, lo


def _bdot(a, b, dims):
    return lax.dot_general(a, b, dims, preferred_element_type=F32)


def _dot1(a, b, dims):
    return _bdot(a.astype(BF16), b.astype(BF16), dims)


def _dot3(a, b, dims):
    ah, al = _split2(a)
    bh, bl = _split2(b)
    return _bdot(ah, bh, dims) + (_bdot(ah, bl, dims) + _bdot(al, bh, dims))


def _dot_exact_lhs(a01, b, dims):
    a = a01.astype(BF16)
    h, m, l = _split3(b)
    return _bdot(a, h, dims) + (_bdot(a, m, dims) + _bdot(a, l, dims))


_NN = (((2,), (1,)), ((0,), (0,)))
_NT = (((2,), (2,)), ((0,), (0,)))
_TN = (((1,), (1,)), ((0,), (0,)))


RWKV_CHUNK = 64


def _rwkv_kernel(r_ref, lw_ref, k_ref, v_ref, kk_ref, a_ref, s0_ref, y_ref, sfin_ref, s_scr, *, dot):
    c_idx = pl.program_id(1)

    @pl.when(c_idx == 0)
    def _():
        s_scr[...] = s0_ref[...]

    r = r_ref[...]
    lw = lw_ref[...]
    k = k_ref[...]
    v = v_ref[...]
    kk = kk_ref[...]
    a = a_ref[...]
    g, L, n = r.shape
    S = s_scr[...]

    row = lax.broadcasted_iota(jnp.int32, (L, L), 0)
    col = lax.broadcasted_iota(jnp.int32, (L, L), 1)
    incl = row >= col
    strict = row > col
    tri = jnp.broadcast_to(jnp.where(incl, 1.0, 0.0).astype(F32)[None], (g, L, L))
    ci = _dot_exact_lhs(tri, lw, _NN)
    ce = ci - lw
    cl = ci[:, L - 1:L, :]
    e_neg = jnp.exp(-ci)
    b = a * kk
    alpha = kk * jnp.exp(ce)
    rho = r * jnp.exp(ci)
    beta = b * e_neg
    kappa = k * e_neg
    e_end = jnp.exp(cl - ci)
    ar = jnp.concatenate([alpha, rho], axis=1)
    bk = jnp.concatenate([beta, kappa], axis=1)
    w = dot(ar, bk, _NT)
    nmat = jnp.where(strict[None], w[:, :L, :L], 0.0)
    mmat = jnp.where(strict[None], w[:, :L, L:], 0.0)
    p1 = jnp.where(incl[None], w[:, L:, :L], 0.0)
    p2 = jnp.where(incl[None], w[:, L:, L:], 0.0)

    eye = jnp.where(row == col, 1.0, 0.0).astype(F32)[None]
    x = eye - nmat
    p = dot(nmat, nmat, _NN)
    span = 2
    while True:
        x = x + dot(x, p, _NN)
        span *= 2
        if span >= L:
            break
        p = dot(p, p, _NN)

    us = dot(ar, S, _NT)
    rhs = us[:, :L] + dot(mmat, v, _NN)
    d = -dot(x, rhs, _NN)
    dv = jnp.concatenate([d, v], axis=1)
    pp = jnp.concatenate([p1, p2], axis=2)
    y_ref[...] = us[:, L:] + dot(pp, dv, _NN)
    bk_end = jnp.concatenate([b * e_end, k * e_end], axis=1)
    s_new = S * jnp.exp(cl) + dot(dv, bk_end, _TN)
    s_scr[...] = s_new

    @pl.when(c_idx == pl.num_programs(1) - 1)
    def _():
        sfin_ref[...] = s_new


def rwkv7_chunked(r, lw, k, v, kk, a, s0, *, group=8, dot=_dot3):
    C, T, n = r.shape
    L = RWKV_CHUNK
    g = min(group, C)
    assert C % g == 0 and T % L == 0
    seq = pl.BlockSpec((g, L, n), lambda i, c: (i, c, 0))
    st = pl.BlockSpec((g, n, n), lambda i, c: (i, 0, 0))
    return pl.pallas_call(
        functools.partial(_rwkv_kernel, dot=dot),
        grid=(C // g, T // L),
        in_specs=[seq] * 6 + [st],
        out_specs=[seq, st],
        out_shape=[jax.ShapeDtypeStruct((C, T, n), F32), jax.ShapeDtypeStruct((C, n, n), F32)],
        scratch_shapes=[pltpu.VMEM((g, n, n), F32)],
        compiler_params=pltpu.CompilerParams(dimension_semantics=("parallel", "arbitrary")),
        name="rwkv7_chunked",
    )(r, lw, k, v, kk, a, s0)


N_EXPERTS = 32
TOP_K = 4
SWIGLU_LIMIT = 7.0
SWIGLU_ALPHA = 1.702
MOE_ROW_TILE = 256
MXU_WIDTH = 256


def _moe_kernel(te_ref, tv_ref, x_ref, cw_ref, w1_ref, b1_ref, w2_ref, b2_ref, perm_ref, y_ref, w1s, w2s, hs):
    t = pl.program_id(0)
    e = te_ref[t]
    prev = te_ref[jnp.maximum(t - 1, 0)]
    new_expert = jnp.logical_or(t == 0, e != prev)
    valid = tv_ref[t] != 0
    d_model, two_f = w1s.shape
    n_blk = two_f // MXU_WIDTH
    half = MXU_WIDTH // 2

    @pl.when(new_expert)
    def _():
        for blk in range(n_blk):
            sl = slice(blk * MXU_WIDTH, (blk + 1) * MXU_WIDTH)
            wb = w1_ref[0, :, sl].astype(BF16)
            w1s[:, sl] = jnp.dot(wb, perm_ref[...], preferred_element_type=F32).astype(BF16)
        w2s[...] = w2_ref[0].astype(BF16)

    @pl.when(valid)
    def _():
        x = x_ref[...]
        for blk in range(n_blk):
            sl = slice(blk * MXU_WIDTH, (blk + 1) * MXU_WIDTH)
            u = jnp.dot(x, w1s[:, sl], preferred_element_type=F32) + b1_ref[0, :, sl]
            glu = jnp.minimum(u[:, :half], SWIGLU_LIMIT)
            lin = jnp.clip(u[:, half:], -SWIGLU_LIMIT, SWIGLU_LIMIT)
            act = glu * (1.0 / (1.0 + jnp.exp(-SWIGLU_ALPHA * glu))) * (lin + 1.0)
            hs[:, blk * half:(blk + 1) * half] = act.astype(BF16)
        y = jnp.dot(hs[...], w2s[...], preferred_element_type=F32) + b2_ref[0]
        y_ref[...] = cw_ref[...] * y

    @pl.when(jnp.logical_not(valid))
    def _():
        y_ref[...] = jnp.zeros_like(y_ref)


def _deinterleave_perm():
    half = MXU_WIDTH // 2
    src = np.arange(MXU_WIDTH)
    dst = np.where(src % 2 == 0, src // 2, half + src // 2)
    p = np.zeros((MXU_WIDTH, MXU_WIDTH), np.float32)
    p[src, dst] = 1.0
    return jnp.asarray(p, BF16)


def _moe_dispatch(top_i, top_w):
    n, k = top_i.shape
    tm = MOE_ROW_TILE
    p_rows = n * k + N_EXPERTS * tm
    sel = jnp.zeros((n, N_EXPERTS), jnp.int32).at[jnp.arange(n)[:, None], top_i].add(1)
    before = jnp.cumsum(sel, axis=0) - sel
    rank = jnp.take_along_axis(before, top_i, axis=1)
    counts = jnp.sum(sel, axis=0)
    padded = ((counts + tm - 1) // tm) * tm
    ends = jnp.cumsum(padded)
    pos = (ends - padded)[top_i] + rank
    flat_pos = pos.reshape(-1)
    src_tok = jnp.zeros((p_rows,), jnp.int32).at[flat_pos].set(jnp.repeat(jnp.arange(n, dtype=jnp.int32), k))
    cw = jnp.zeros((p_rows,), F32).at[flat_pos].set(top_w.reshape(-1))
    n_tiles = p_rows // tm
    tile_start = jnp.arange(n_tiles, dtype=jnp.int32) * tm
    tile_valid = (tile_start < ends[-1]).astype(jnp.int32)
    last_tile = ends[-1] // tm - 1
    tile_expert = jnp.sum(ends[None, :] <= jnp.minimum(tile_start, last_tile * tm)[:, None], axis=1).astype(jnp.int32)
    return src_tok, cw[:, None], pos, tile_expert, tile_valid


def moe_experts(xt, top_i, top_w, w1, b1, w2, b2):
    n, d = xt.shape
    e, _, two_f = w1.shape
    f = two_f // 2
    tm = MOE_ROW_TILE
    src_tok, cw, pos, tile_expert, tile_valid = _moe_dispatch(top_i, top_w)
    p_rows = src_tok.shape[0]
    xs = xt.astype(BF16)[src_tok]
    b1p = b1.reshape(e, two_f // MXU_WIDTH, MXU_WIDTH // 2, 2).transpose(0, 1, 3, 2).reshape(e, 1, two_f)
    grid_spec = pltpu.PrefetchScalarGridSpec(
        num_scalar_prefetch=2,
        grid=(p_rows // tm,),
        in_specs=[
            pl.BlockSpec((tm, d), lambda t, te, tv: (t, 0)),
            pl.BlockSpec((tm, 1), lambda t, te, tv: (t, 0)),
            pl.BlockSpec((1, d, two_f), lambda t, te, tv: (te[t], 0, 0)),
            pl.BlockSpec((1, 1, two_f), lambda t, te, tv: (te[t], 0, 0)),
            pl.BlockSpec((1, f, d), lambda t, te, tv: (te[t], 0, 0)),
            pl.BlockSpec((1, 1, d), lambda t, te, tv: (te[t], 0, 0)),
            pl.BlockSpec((MXU_WIDTH, MXU_WIDTH), lambda t, te, tv: (0, 0)),
        ],
        out_specs=pl.BlockSpec((tm, d), lambda t, te, tv: (t, 0)),
        scratch_shapes=[pltpu.VMEM((d, two_f), BF16), pltpu.VMEM((f, d), BF16), pltpu.VMEM((tm, f), BF16)],
    )
    ys = pl.pallas_call(
        _moe_kernel,
        grid_spec=grid_spec,
        out_shape=jax.ShapeDtypeStruct((p_rows, d), F32),
        compiler_params=pltpu.CompilerParams(dimension_semantics=("arbitrary",),
                                             vmem_limit_bytes=48 * 1024 * 1024),
        name="moe_experts",
    )(tile_expert, tile_valid, xs, cw, w1, b1p, w2, b2.reshape(e, 1, d), _deinterleave_perm())
    return jnp.sum(ys[pos], axis=1)


D_MODEL = 1024
DEPTH = 2
GRID_W = 64
ROPE_BASE = 10000.0
Q_BLOCK = 128
WINDOW = 128
NEG_INF = -1e30
A_HEADS, A_KV_HEADS, A_HD = 4, 2, 64
A_GROUP = A_HEADS // A_KV_HEADS
B_HEADS, B_NOPE, B_ROPE, B_VD, B_Q_LORA, B_KV_LORA = 4, 64, 32, 64, 192, 128
C_HEADS, C_DK, C_DV, C_GATE_RANK, C_GATE_TEMP, C_CHUNK = 4, 32, 64, 16, 16.0, 64
D_HEADS, D_N, D_DECAY_RANK, D_AAA_RANK, D_GATE_RANK, D_GN_EPS = 4, 64, 64, 64, 128, 64e-5
BRANCH_W = 256
N_BRANCH = 4
A_COLS = A_HEADS * A_HD + 2 * A_KV_HEADS * A_HD
B_COLS = B_Q_LORA + B_KV_LORA + B_ROPE
C_COLS = 2 * C_HEADS * C_DK + 2 * C_HEADS * C_DV + 2 * C_GATE_RANK
D_COLS = 3 * D_HEADS * D_N + 2 * D_DECAY_RANK + 2 * D_AAA_RANK + D_GATE_RANK
G_COLS = N_BRANCH * D_MODEL
ALPHA = (2 * DEPTH) ** 0.25


def split_cols(z, sizes):
    idx = np.cumsum(np.array(sizes))[:-1].tolist()
    return jnp.split(z, idx, axis=-1)


def layer_norm(x, g, b, eps=1e-5):
    mu = jnp.mean(x, -1, keepdims=True)
    var = jnp.mean(jnp.square(x - mu), -1, keepdims=True)
    return ((x - mu) * lax.rsqrt(var + eps)) * g + b


def rms_norm(x, g, eps=1e-6):
    return (x * lax.rsqrt(jnp.mean(jnp.square(x), -1, keepdims=True) + eps)) * g


def to_heads(x, n):
    b, t, _ = x.shape
    return x.reshape(b, t, n, -1).transpose(0, 2, 1, 3)


def from_heads(x):
    b, n, t, d = x.shape
    return x.transpose(0, 2, 1, 3).reshape(b, t, n * d)


def axial_rope(n_tok, rot_dim):
    rows = n_tok // GRID_W
    row = jnp.repeat(jnp.arange(rows, dtype=F32), GRID_W)
    col = jnp.tile(jnp.arange(GRID_W, dtype=F32), rows)
    quarter = rot_dim // 4
    inv = ROPE_BASE ** (-jnp.arange(quarter, dtype=F32) / quarter)
    ang = jnp.concatenate([row[:, None] * inv, col[:, None] * inv], axis=-1)
    return jnp.cos(ang), jnp.sin(ang)


def apply_axial_rope(x, cos, sin):
    q = x.shape[-1] // 4
    cr, cc, sr, sc = cos[:, :q], cos[:, q:], sin[:, :q], sin[:, q:]
    p0, p1, p2, p3 = x[..., :q], x[..., q:2 * q], x[..., 2 * q:3 * q], x[..., 3 * q:]
    return jnp.concatenate([p0 * cr - p1 * sr, p1 * cr + p0 * sr,
                            p2 * cc - p3 * sc, p3 * cc + p2 * sc], axis=-1)


def attn_probs(s, sink):
    if sink is None:
        return jax.nn.softmax(s, axis=-1)
    sk = sink.astype(F32)[None, :, :, None, None]
    m = jnp.maximum(jnp.max(s, -1, keepdims=True), sk)
    e = jnp.exp(s - m)
    return e / (jnp.sum(e, -1, keepdims=True) + jnp.exp(sk - m))


def blocked_attention(q, k, v, scale, sink):
    s = jnp.einsum('bkgqd,bksd->bkgqs', q, k).astype(F32) * scale
    p = attn_probs(s, sink)
    return jnp.einsum('bkgqs,bksv->bkgqv', p, v)


def window_context_attention(q, k, v, k_ctx, v_ctx, scale, sink):
    b, kvh, g, t, d = q.shape
    nb = t // Q_BLOCK
    pad = ((0, 0), (0, 0), (Q_BLOCK, Q_BLOCK), (0, 0))

    def bands(x):
        xb = jnp.pad(x, pad).reshape(b, kvh, nb + 2, Q_BLOCK, x.shape[-1])
        return jnp.concatenate([xb[:, :, :-2], xb[:, :, 1:-1], xb[:, :, 2:]], axis=3)

    kw, vw = bands(k), bands(v)
    qb = q.reshape(b, kvh, g, nb, Q_BLOCK, d)
    qpos = jnp.arange(nb)[:, None] * Q_BLOCK + jnp.arange(Q_BLOCK)[None]
    kpos = jnp.arange(nb)[:, None] * Q_BLOCK - Q_BLOCK + jnp.arange(3 * Q_BLOCK)[None]
    mask = ((jnp.abs(qpos[:, :, None] - kpos[:, None, :]) <= WINDOW)
            & (kpos[:, None, :] >= 0) & (kpos[:, None, :] < t))
    s_win = jnp.where(mask[None, None, None], jnp.einsum('bkgnqd,bknsd->bkgnqs', qb, kw) * scale, NEG_INF)
    s_ctx = jnp.einsum('bkgnqd,bksd->bkgnqs', qb, k_ctx) * scale
    s = jnp.concatenate([s_win, s_ctx], axis=-1)
    sk = sink.astype(F32)[None, :, :, None, None, None]
    m = jnp.maximum(jnp.max(s, -1, keepdims=True), sk)
    e = jnp.exp(s - m)
    p = e / (jnp.sum(e, -1, keepdims=True) + jnp.exp(sk - m))
    o = (jnp.einsum('bkgnqs,bknsv->bkgnqv', p[..., :3 * Q_BLOCK], vw)
         + jnp.einsum('bkgnqs,bksv->bkgnqv', p[..., 3 * Q_BLOCK:], v_ctx))
    return o.reshape(b, kvh, g, t, v.shape[-1])


def gla_chunked(q, k, v, log_a, s0):
    b, h, t, dk = q.shape
    dv = v.shape[-1]
    nc = t // C_CHUNK

    def chunks(x):
        return jnp.moveaxis(x.astype(F32).reshape(b, h, nc, C_CHUNK, x.shape[-1]), 2, 0)

    tril = jnp.tril(jnp.ones((C_CHUNK, C_CHUNK), dtype=bool))

    def step(S, xs):
        qc, kc, vc, lac = xs
        cb = jnp.cumsum(lac, axis=-2)
        inter = jnp.einsum('bhtd,bhdv->bhtv', qc * jnp.exp(cb), S)
        diff = jnp.where(tril[:, :, None], cb[..., :, None, :] - cb[..., None, :, :], -jnp.inf)
        att = jnp.einsum('bhtd,bhsd,bhtsd->bhts', qc, kc, jnp.exp(diff))
        intra = jnp.einsum('bhts,bhsv->bhtv', att, vc)
        cend = cb[..., -1:, :]
        S = jnp.exp(cend[..., 0, :])[..., None] * S + jnp.einsum('bhsd,bhsv->bhdv', kc * jnp.exp(cend - cb), vc)
        return S, inter + intra

    S, o = lax.scan(step, s0.astype(F32), (chunks(q), chunks(k), chunks(v), chunks(log_a)))
    return jnp.moveaxis(o, 0, 2).reshape(b, h, t, dv), S


def centred_shift(x):
    xp = jnp.pad(x, ((0, 0), (1, 1), (0, 0)))
    return 0.5 * (xp[:, :-2] + xp[:, 2:])


def mla_keys_values(lp, ckv, kpe):
    bn, s, _ = ckv.shape
    kv = to_heads(ckv @ lp['b_w_ukv'], B_HEADS)
    k_pe = jnp.broadcast_to(kpe[:, None], (bn, B_HEADS, s, B_ROPE))
    return jnp.concatenate([kv[..., :B_NOPE], k_pe], axis=-1), kv[..., B_NOPE:]


def rwkv_bidirectional(xs_dirs, s0):
    bn, h, t, n = xs_dirs[0][0].shape
    stacked = []
    for i in range(6):
        stacked.append(jnp.stack([xs_dirs[0][i], jnp.flip(xs_dirs[1][i], axis=2)], axis=1).reshape(bn * 2 * h, t, n))
    y, s_fin = rwkv7_chunked(*stacked, s0.reshape(bn * 2 * h, n, n))
    y = y.reshape(bn, 2, h, t, n)
    return y[:, 0] + jnp.flip(y[:, 1], axis=2), s_fin.reshape(bn, 2, h, n, n)


def token_mixers(lp, h, ctx):
    latent = ctx is not None
    bn, t, _ = h.shape
    z = jnp.einsum('btd,de->bte', h, lp['w_in'])
    za, zb, zc, zd, zg = split_cols(z, (A_COLS, B_COLS, C_COLS, D_COLS, G_COLS))

    a_q, a_k, a_v = split_cols(za, (A_HEADS * A_HD, A_KV_HEADS * A_HD, A_KV_HEADS * A_HD))
    a_q = a_q.reshape(bn, t, A_KV_HEADS, A_GROUP, A_HD).transpose(0, 2, 3, 1, 4)
    a_k = to_heads(a_k, A_KV_HEADS)
    a_v = to_heads(a_v, A_KV_HEADS)
    a_sink = lp['a_sink'].reshape(A_KV_HEADS, A_GROUP)
    if latent:
        cos, sin = axial_rope(t, A_HD)
        o_a = window_context_attention(apply_axial_rope(a_q, cos, sin), apply_axial_rope(a_k, cos, sin),
                                       a_v, ctx[0], ctx[1], A_HD ** -0.5, a_sink)
    else:
        o_a = blocked_attention(a_q, a_k, a_v, A_HD ** -0.5, a_sink)
    o_a = o_a.transpose(0, 3, 1, 2, 4).reshape(bn, t, BRANCH_W)

    b_cq, b_ckv, b_kpe = split_cols(zb, (B_Q_LORA, B_KV_LORA, B_ROPE))
    b_q = to_heads(rms_norm(b_cq, lp['b_q_norm']) @ lp['b_w_uq'], B_HEADS)
    b_ckv = rms_norm(b_ckv, lp['b_kv_norm'])
    if latent:
        cos, sin = axial_rope(t, B_ROPE)
        b_q = jnp.concatenate([b_q[..., :B_NOPE], apply_axial_rope(b_q[..., B_NOPE:], cos, sin)], axis=-1)
        k_l, v_l = mla_keys_values(lp, b_ckv, apply_axial_rope(b_kpe, cos, sin))
        k_c, v_c = mla_keys_values(lp, ctx[2], ctx[3])
        b_k = jnp.concatenate([k_c, k_l], axis=2)
        b_v = jnp.concatenate([v_c, v_l], axis=2)
    else:
        b_k, b_v = mla_keys_values(lp, b_ckv, b_kpe)
    o_b = blocked_attention(b_q[:, :, None], b_k, b_v, (B_NOPE + B_ROPE) ** -0.5, None)
    o_b = from_heads(o_b[:, :, 0])

    c_q, c_k, c_v, c_og, c_af, c_ab = split_cols(
        zc, (C_HEADS * C_DK, C_HEADS * C_DK, C_HEADS * C_DV, C_HEADS * C_DV, C_GATE_RANK, C_GATE_RANK))
    cq = to_heads(c_q, C_HEADS) * (C_DK ** -0.5)
    ck = to_heads(c_k, C_HEADS)
    cv = to_heads(c_v, C_HEADS)
    c_s0 = ctx[4] if latent else jnp.zeros((bn, 2, C_HEADS, C_DK, C_DV), F32)
    c_outs, c_states = [], []
    for direction, lr in enumerate((c_af, c_ab)):
        log_a = to_heads(jax.nn.log_sigmoid(
            (lr @ lp['c_w_gate'][direction] + lp['c_b_gate'][direction]).astype(F32)) / C_GATE_TEMP, C_HEADS)
        xs = (cq, ck, cv, log_a)
        if direction == 1:
            xs = tuple(jnp.flip(x, axis=2) for x in xs)
        o, s_fin = gla_chunked(*xs, c_s0[:, direction])
        c_outs.append(jnp.flip(o, axis=2) if direction == 1 else o)
        c_states.append(s_fin)
    o_c = from_heads(rms_norm(c_outs[0] + c_outs[1], lp['c_norm'])) * jax.nn.silu(c_og)

    zd = zd + (centred_shift(zd) - zd) * lp['d_mu']
    hn = D_HEADS * D_N
    d_r, d_k, d_v, d_wf, d_wb, d_af, d_ab, d_g = split_cols(
        zd, (hn, hn, hn, D_DECAY_RANK, D_DECAY_RANK, D_AAA_RANK, D_AAA_RANK, D_GATE_RANK))
    kk = to_heads(d_k * lp['d_k_k'], D_HEADS)
    kk = kk / jnp.maximum(jnp.sqrt(jnp.sum(jnp.square(kk), -1, keepdims=True)), 1e-12)
    r_h = to_heads(d_r, D_HEADS)
    v_h = to_heads(d_v, D_HEADS)
    d_s0 = ctx[5] if latent else jnp.zeros((bn, 2, D_HEADS, D_N, D_N), F32)
    xs_dirs, bonus = [], 0.0
    for direction, (lw, la) in enumerate(((d_wf, d_af), (d_wb, d_ab))):
        w_log = -jax.nn.softplus(-(lp['d_w0'][direction] + jnp.tanh(lw) @ lp['d_w2'][direction])) - 0.5
        log_decay = to_heads(-jnp.exp(w_log), D_HEADS)
        a = jax.nn.sigmoid(lp['d_a0'][direction] + la @ lp['d_a2'][direction])
        k_dir = to_heads(d_k * (1.0 + (a - 1.0) * lp['d_k_a']), D_HEADS)
        a_h = to_heads(a, D_HEADS)
        xs_dirs.append((r_h, log_decay, k_dir, v_h, kk, a_h))
        bonus = bonus + jnp.sum(r_h * k_dir * lp['d_r_k'][None, :, None, :], -1, keepdims=True) * v_h
    d_y, d_states = rwkv_bidirectional(xs_dirs, d_s0)
    mu = jnp.mean(d_y, -1, keepdims=True)
    var = jnp.mean(jnp.square(d_y - mu), -1, keepdims=True)
    y_gn = from_heads((d_y - mu) * lax.rsqrt(var + D_GN_EPS)) * lp['d_ln_g'] + lp['d_ln_b']
    o_d = (y_gn + from_heads(bonus)) * (jax.nn.sigmoid(d_g) @ lp['d_g2'])

    gates = jax.nn.sigmoid(zg).reshape(bn, t, N_BRANCH, D_MODEL)
    branches = jnp.stack([o_a, o_b, o_c, o_d], axis=2)
    proj = jnp.einsum('btnw,nwd->btnd', branches, lp['w_br'])
    out = jnp.sum(gates * proj, axis=2) @ lp['w_out']
    if latent:
        return out, None
    return out, (a_k, a_v, b_ckv, b_kpe, jnp.stack(c_states, axis=1), d_states)


def routed_experts(lp, h):
    n, d = h.shape
    logits = jnp.dot(h, lp['w_router'], precision=lax.Precision.HIGHEST) + lp['b_router']
    top_v, top_i = lax.top_k(logits, TOP_K)
    top_w = jax.nn.softmax(top_v, axis=-1)
    return moe_experts(h, top_i, top_w, lp['w_mlp1'], lp['b_mlp1'], lp['w_mlp2'], lp['b_mlp2'])


def modulation(lp, cond):
    mod = (jax.nn.silu(cond) @ lp['w_mod'] + lp['b_mod'])[:, None, :]
    return jnp.split(mod, 6, axis=-1)


def kernel(x_prompt, x_sample, cache_a_k, cache_a_v, cache_b_ckv, cache_b_kpe, state_c, state_d, c,
           c_ctx, w_mod, b_mod, w_in, a_sink, b_q_norm, b_w_uq, b_kv_norm, b_w_ukv, c_w_gate, c_b_gate,
           c_norm, d_mu, d_w0, d_w2, d_a0, d_a2, d_g2, d_k_k, d_k_a, d_r_k, d_ln_g, d_ln_b, w_br, w_out,
           ln_g, ln_b, w_router, b_router, w_mlp1, b_mlp1, w_mlp2, b_mlp2):
    params = dict(w_mod=w_mod, b_mod=b_mod, w_in=w_in, a_sink=a_sink, b_q_norm=b_q_norm, b_w_uq=b_w_uq,
                  b_kv_norm=b_kv_norm, b_w_ukv=b_w_ukv, c_w_gate=c_w_gate, c_b_gate=c_b_gate, c_norm=c_norm,
                  d_mu=d_mu, d_w0=d_w0, d_w2=d_w2, d_a0=d_a0, d_a2=d_a2, d_g2=d_g2, d_k_k=d_k_k, d_k_a=d_k_a,
                  d_r_k=d_r_k, d_ln_g=d_ln_g, d_ln_b=d_ln_b, w_br=w_br, w_out=w_out, ln_g=ln_g, ln_b=ln_b,
                  w_router=w_router, b_router=b_router, w_mlp1=w_mlp1, b_mlp1=b_mlp1, w_mlp2=w_mlp2,
                  b_mlp2=b_mlp2)
    y_p, y_s = x_prompt, x_sample
    ctx_layers = []
    n_p = x_prompt.shape[0] * x_prompt.shape[1]
    for l in range(DEPTH):
        lp = {name: val[l] for name, val in params.items()}
        mod_p = modulation(lp, c_ctx[None])
        mod_s = modulation(lp, c)
        mix_p, ctx_out = token_mixers(lp, y_p * (1.0 + mod_p[1]) + mod_p[0], None)
        ctx = (cache_a_k[:, l], cache_a_v[:, l], cache_b_ckv[:, l], cache_b_kpe[:, l], state_c[:, l], state_d[:, l])
        mix_s, _ = token_mixers(lp, y_s * (1.0 + mod_s[1]) + mod_s[0], ctx)
        ctx_layers.append(ctx_out)
        y_p = layer_norm(ALPHA * y_p + mod_p[2] * mix_p, lp['ln_g'][0], lp['ln_b'][0])
        y_s = layer_norm(ALPHA * y_s + mod_s[2] * mix_s, lp['ln_g'][0], lp['ln_b'][0])
        h_p = (y_p * (1.0 + mod_p[4]) + mod_p[3]).reshape(n_p, D_MODEL)
        h_s = (y_s * (1.0 + mod_s[4]) + mod_s[3]).reshape(-1, D_MODEL)
        moe = routed_experts(lp, jnp.concatenate([h_p, h_s], axis=0))
        y_p = layer_norm(ALPHA * y_p + mod_p[5] * moe[:n_p].reshape(y_p.shape), lp['ln_g'][1], lp['ln_b'][1])
        y_s = layer_norm(ALPHA * y_s + mod_s[5] * moe[n_p:].reshape(y_s.shape), lp['ln_g'][1], lp['ln_b'][1])
    outs = [jnp.stack([t[i] for t in ctx_layers], axis=1) for i in range(6)]
    return (y_p, y_s, *outs)
```

```python
import functools

import jax
import jax.numpy as jnp
import numpy as np
from jax import lax
from jax.experimental import pallas as pl
from jax.experimental.pallas import tpu as pltpu

F32 = jnp.float32
BF16 = jnp.bfloat16

V7X_VMEM_BYTES = 64 * 1024 * 1024
LANES = 128


def _split3(x):
    hi = x.astype(BF16)
    r1 = x - hi.astype(F32)
    mid = r1.astype(BF16)
    lo = (r1 - mid.astype(F32)).astype(BF16)
    return hi, mid, lo


def _split2(x):
    hi = x.astype(BF16)
    lo = (x - hi.astype(F32)).astype(BF16)
    return hi, lo


def _bdot(a, b, dims):
    return lax.dot_general(a, b, dims, preferred_element_type=F32)


def _dot1(a, b, dims):
    return _bdot(a.astype(BF16), b.astype(BF16), dims)


def _dot3(a, b, dims):
    ah, al = _split2(a)
    bh, bl = _split2(b)
    return _bdot(ah, bh, dims) + (_bdot(ah, bl, dims) + _bdot(al, bh, dims))


def _dot_exact_lhs(a01, b, dims):
    a = a01.astype(BF16)
    h, m, l = _split3(b)
    return _bdot(a, h, dims) + (_bdot(a, m, dims) + _bdot(a, l, dims))


_NN = (((2,), (1,)), ((0,), (0,)))
_NT = (((2,), (2,)), ((0,), (0,)))
_TN = (((1,), (1,)), ((0,), (0,)))


RWKV_CHUNK = 64


def _rwkv_kernel(r_ref, lw_ref, k_ref, v_ref, kk_ref, a_ref, s0_ref, y_ref, sfin_ref, s_scr, *, dot):
    c_idx = pl.program_id(1)

    @pl.when(c_idx == 0)
    def _():
        s_scr[...] = s0_ref[...]

    r = r_ref[...]
    lw = lw_ref[...]
    k = k_ref[...]
    v = v_ref[...]
    kk = kk_ref[...]
    a = a_ref[...]
    g, L, n = r.shape
    S = s_scr[...]

    row = lax.broadcasted_iota(jnp.int32, (L, L), 0)
    col = lax.broadcasted_iota(jnp.int32, (L, L), 1)
    incl = row >= col
    strict = row > col
    tri = jnp.broadcast_to(jnp.where(incl, 1.0, 0.0).astype(F32)[None], (g, L, L))
    ci = _dot_exact_lhs(tri, lw, _NN)
    ce = ci - lw
    cl = ci[:, L - 1:L, :]
    e_neg = jnp.exp(-ci)
    b = a * kk
    alpha = kk * jnp.exp(ce)
    rho = r * jnp.exp(ci)
    beta = b * e_neg
    kappa = k * e_neg
    e_end = jnp.exp(cl - ci)
    ar = jnp.concatenate([alpha, rho], axis=1)
    bk = jnp.concatenate([beta, kappa], axis=1)
    w = dot(ar, bk, _NT)
    nmat = jnp.where(strict[None], w[:, :L, :L], 0.0)
    mmat = jnp.where(strict[None], w[:, :L, L:], 0.0)
    p1 = jnp.where(incl[None], w[:, L:, :L], 0.0)
    p2 = jnp.where(incl[None], w[:, L:, L:], 0.0)

    eye = jnp.where(row == col, 1.0, 0.0).astype(F32)[None]
    x = eye - nmat
    p = dot(nmat, nmat, _NN)
    span = 2
    while True:
        x = x + dot(x, p, _NN)
        span *= 2
        if span >= L:
            break
        p = dot(p, p, _NN)

    us = dot(ar, S, _NT)
    rhs = us[:, :L] + dot(mmat, v, _NN)
    d = -dot(x, rhs, _NN)
    dv = jnp.concatenate([d, v], axis=1)
    pp = jnp.concatenate([p1, p2], axis=2)
    y_ref[...] = us[:, L:] + dot(pp, dv, _NN)
    bk_end = jnp.concatenate([b * e_end, k * e_end], axis=1)
    s_new = S * jnp.exp(cl) + dot(dv, bk_end, _TN)
    s_scr[...] = s_new

    @pl.when(c_idx == pl.num_programs(1) - 1)
    def _():
        sfin_ref[...] = s_new


def rwkv7_chunked(r, lw, k, v, kk, a, s0, *, group=8, dot=_dot3):
    C, T, n = r.shape
    L = RWKV_CHUNK
    g = min(group, C)
    assert C % g == 0 and T % L == 0
    seq = pl.BlockSpec((g, L, n), lambda i, c: (i, c, 0))
    st = pl.BlockSpec((g, n, n), lambda i, c: (i, 0, 0))
    return pl.pallas_call(
        functools.partial(_rwkv_kernel, dot=dot),
        grid=(C // g, T // L),
        in_specs=[seq] * 6 + [st],
        out_specs=[seq, st],
        out_shape=[jax.ShapeDtypeStruct((C, T, n), F32), jax.ShapeDtypeStruct((C, n, n), F32)],
        scratch_shapes=[pltpu.VMEM((g, n, n), F32)],
        compiler_params=pltpu.CompilerParams(dimension_semantics=("parallel", "arbitrary")),
        name="rwkv7_chunked",
    )(r, lw, k, v, kk, a, s0)


GLA_CHUNK = 64
GLA_SUB = 16


def _gla_kernel(q_ref, k_ref, la_ref, v_ref, s0_ref, o_ref, sfin_ref, s_scr, *, dot):
    c_idx = pl.program_id(1)

    @pl.when(c_idx == 0)
    def _():
        s_scr[...] = s0_ref[...]

    q4 = q_ref[...]
    k4 = k_ref[...]
    la4 = la_ref[...]
    v = v_ref[...]
    g, L, lanes = q4.shape
    dk = lanes // (L // GLA_SUB)
    st = s_scr[...]

    row = lax.broadcasted_iota(jnp.int32, (L, L), 0)
    col = lax.broadcasted_iota(jnp.int32, (L, L), 1)
    incl = row >= col
    tri = jnp.broadcast_to(jnp.where(incl, 1.0, 0.0).astype(F32)[None], (g, L, L))
    c = _dot_exact_lhs(tri, la4, _NN)
    lane_blk = lax.broadcasted_iota(jnp.int32, (1, L, lanes), 2) // dk
    row_blk = lax.broadcasted_iota(jnp.int32, (1, L, lanes), 1) // GLA_SUB
    cref = jnp.zeros((g, L, lanes), F32)
    for j in range(1, L // GLA_SUB):
        cref = jnp.where(lane_blk == j, c[:, j * GLA_SUB - 1:j * GLA_SUB, :], cref)
    q_on = row_blk == lane_blk
    k_on = row_blk <= lane_blk
    qh = jnp.where(q_on, q4 * jnp.exp(jnp.where(q_on, c - cref, 0.0)), 0.0)
    kh = jnp.where(k_on, k4 * jnp.exp(jnp.where(k_on, cref - c, 0.0)), 0.0)
    att = jnp.where(incl[None], dot(qh, kh, _NT), 0.0)
    cl = c[:, L - 1:L, :dk]
    qe = (q4 * jnp.exp(c))[:, :, :dk]
    ke = (k4 * jnp.exp(c[:, L - 1:L, :] - c))[:, :, :dk]
    o_ref[...] = dot(qe, st, _NT) + dot(att, v, _NN)
    s_new = st * jnp.exp(cl) + dot(v, ke, _TN)
    s_scr[...] = s_new

    @pl.when(c_idx == pl.num_programs(1) - 1)
    def _():
        sfin_ref[...] = s_new


def gla_chunked_pallas(q, k, la, v, s0, *, group=8, dot=_dot3):
    C, T, dk = q.shape
    dv = v.shape[-1]
    L = GLA_CHUNK
    rep = L // GLA_SUB
    g = min(group, C)
    assert C % g == 0 and T % L == 0
    q4, k4, la4 = (jnp.tile(x, (1, 1, rep)) for x in (q, k, la))
    wide = pl.BlockSpec((g, L, rep * dk), lambda i, c: (i, c, 0))
    vspec = pl.BlockSpec((g, L, dv), lambda i, c: (i, c, 0))
    st = pl.BlockSpec((g, dv, dk), lambda i, c: (i, 0, 0))
    o, s_fin = pl.pallas_call(
        functools.partial(_gla_kernel, dot=dot),
        grid=(C // g, T // L),
        in_specs=[wide, wide, wide, vspec, st],
        out_specs=[vspec, st],
        out_shape=[jax.ShapeDtypeStruct((C, T, dv), F32), jax.ShapeDtypeStruct((C, dv, dk), F32)],
        scratch_shapes=[pltpu.VMEM((g, dv, dk), F32)],
        compiler_params=pltpu.CompilerParams(dimension_semantics=("parallel", "arbitrary")),
        name="gla_chunked",
    )(q4, k4, la4, v, jnp.swapaxes(s0, 1, 2))
    return o, jnp.swapaxes(s_fin, 1, 2)


ATT_Q_BLOCK = 128
ATT_WINDOW = 128
ATT_NEG_INF = -1e30


def _softmax_pv(s, v, sink):
    m = jnp.max(s, axis=-1, keepdims=True)
    if sink is not None:
        m = jnp.maximum(m, sink)
    e = jnp.exp(s - m)
    den = jnp.sum(e, axis=-1, keepdims=True)
    if sink is not None:
        den = den + jnp.exp(sink - m)
    return jnp.dot(e.astype(BF16), v.astype(BF16), preferred_element_type=F32) / den


def _gqa_kernel(sink_ref, q_ref, k_ref, v_ref, *rest, hd, group, scale, windowed):
    if windowed:
        kp_ref, kn_ref, vp_ref, vn_ref, kc_ref, vc_ref, o_ref = rest
    else:
        (o_ref,) = rest
    i = pl.program_id(1)
    tq = q_ref.shape[1]
    n_kv = k_ref.shape[2] // hd
    if windowed:
        qpos = i * tq + lax.broadcasted_iota(jnp.int32, (tq, 3 * tq), 0)
        kpos = (i - 1) * tq + lax.broadcasted_iota(jnp.int32, (tq, 3 * tq), 1)
        n_tok = pl.num_programs(1) * tq
        mask = (jnp.abs(qpos - kpos) <= ATT_WINDOW) & (kpos >= 0) & (kpos < n_tok)
        mask = jnp.concatenate([mask] * group, axis=0)
    for kvh in range(n_kv):
        ks = slice(kvh * hd, (kvh + 1) * hd)
        qs = [q_ref[0, :, (kvh * group + g) * hd:(kvh * group + g + 1) * hd] for g in range(group)]
        q = (jnp.concatenate(qs, axis=0) * scale).astype(BF16)
        sink = jnp.concatenate(
            [jnp.full((tq, 1), sink_ref[kvh * group + g], F32) for g in range(group)], axis=0)
        if windowed:
            k_win = jnp.concatenate([kp_ref[0, :, ks], k_ref[0, :, ks], kn_ref[0, :, ks]], axis=0)
            v_win = jnp.concatenate([vp_ref[0, :, ks], v_ref[0, :, ks], vn_ref[0, :, ks]], axis=0)
            s_win = lax.dot_general(q, k_win.astype(BF16), (((1,), (1,)), ((), ())), preferred_element_type=F32)
            s_win = jnp.where(mask, s_win, ATT_NEG_INF)
            s_ctx = lax.dot_general(q, kc_ref[0, 0, kvh].astype(BF16), (((1,), (1,)), ((), ())),
                                    preferred_element_type=F32)
            s = jnp.concatenate([s_win, s_ctx], axis=1)
            v = jnp.concatenate([v_win, vc_ref[0, 0, kvh]], axis=0)
        else:
            s = lax.dot_general(q, k_ref[0, :, ks].astype(BF16), (((1,), (1,)), ((), ())),
                                preferred_element_type=F32)
            v = v_ref[0, :, ks]
        o = _softmax_pv(s, v, sink)
        for g in range(group):
            h = kvh * group + g
            o_ref[0, :, h * hd:(h + 1) * hd] = o[g * tq:(g + 1) * tq]


def gqa_attention(q, k, v, sink, *, hd, scale, ctx_kv=None, layer=0):
    b, t, qw = q.shape
    kw = k.shape[2]
    group = qw // kw
    windowed = ctx_kv is not None
    tq = ATT_Q_BLOCK if windowed else t
    nb = t // tq
    qspec = pl.BlockSpec((1, tq, qw), lambda bi, i, s: (bi, i, 0))
    kspec = pl.BlockSpec((1, tq, kw), lambda bi, i, s: (bi, i, 0))
    in_specs = [qspec, kspec, kspec]
    args = [q, k, v]
    if windowed:
        prev = pl.BlockSpec((1, tq, kw), lambda bi, i, s: (bi, jnp.maximum(i - 1, 0), 0))
        nxt = pl.BlockSpec((1, tq, kw), lambda bi, i, s: (bi, jnp.minimum(i + 1, nb - 1), 0))
        ck, cv = ctx_kv
        cspec = pl.BlockSpec((1, 1) + ck.shape[2:], lambda bi, i, s: (bi, layer, 0, 0, 0))
        in_specs += [prev, nxt, prev, nxt, cspec, cspec]
        args += [k, k, v, v, ck, cv]
    return pl.pallas_call(
        functools.partial(_gqa_kernel, hd=hd, group=group, scale=scale, windowed=windowed),
        grid_spec=pltpu.PrefetchScalarGridSpec(
            num_scalar_prefetch=1, grid=(b, nb), in_specs=in_specs, out_specs=qspec),
        out_shape=jax.ShapeDtypeStruct((b, t, qw), F32),
        compiler_params=pltpu.CompilerParams(dimension_semantics=("parallel", "parallel")),
        name="gqa_windowed" if windowed else "gqa_full",
    )(sink, *args)


def _mla_kernel(q_ref, ckv_ref, kpe_ref, wukv_ref, o_ref, kv_scr, *, n_heads, nope, rope, vd, scale):
    i = pl.program_id(1)

    @pl.when(i == 0)
    def _():
        kv_scr[...] = jnp.dot(ckv_ref[0].astype(BF16), wukv_ref[...].astype(BF16),
                              preferred_element_type=F32).astype(BF16)

    kpe = kpe_ref[0].astype(BF16)
    qd = nope + rope
    nt = (((1,), (1,)), ((), ()))
    for h in range(n_heads):
        qn = (q_ref[0, :, h * qd:h * qd + nope] * scale).astype(BF16)
        qp = (q_ref[0, :, h * qd + nope:(h + 1) * qd] * scale).astype(BF16)
        k_n = kv_scr[:, h * (nope + vd):h * (nope + vd) + nope]
        v = kv_scr[:, h * (nope + vd) + nope:(h + 1) * (nope + vd)]
        s = (lax.dot_general(qn, k_n, nt, preferred_element_type=F32)
             + lax.dot_general(qp, kpe, nt, preferred_element_type=F32))
        o_ref[0, :, h * vd:(h + 1) * vd] = _softmax_pv(s, v, None)


def mla_attention(q, ckv, kpe, w_ukv, *, n_heads, nope, rope, vd):
    b, t, qw = q.shape
    s_len, lora = ckv.shape[1:]
    tq = min(ATT_Q_BLOCK, t)
    return pl.pallas_call(
        functools.partial(_mla_kernel, n_heads=n_heads, nope=nope, rope=rope, vd=vd, scale=(nope + rope) ** -0.5),
        grid=(b, t // tq),
        in_specs=[pl.BlockSpec((1, tq, qw), lambda bi, i: (bi, i, 0)),
                  pl.BlockSpec((1, s_len, lora), lambda bi, i: (bi, 0, 0)),
                  pl.BlockSpec((1, s_len, rope), lambda bi, i: (bi, 0, 0)),
                  pl.BlockSpec(w_ukv.shape, lambda bi, i: (0, 0))],
        out_specs=pl.BlockSpec((1, tq, n_heads * vd), lambda bi, i: (bi, i, 0)),
        out_shape=jax.ShapeDtypeStruct((b, t, n_heads * vd), F32),
        scratch_shapes=[pltpu.VMEM((s_len, n_heads * (nope + vd)), BF16)],
        compiler_params=pltpu.CompilerParams(dimension_semantics=("parallel", "arbitrary")),
        name="mla_attention",
    )(q, ckv, kpe, w_ukv)


N_EXPERTS = 32
TOP_K = 4
SWIGLU_LIMIT = 7.0
SWIGLU_ALPHA = 1.702
MOE_ROW_TILE = 256
MXU_WIDTH = 256


def _moe_kernel(te_ref, tv_ref, x_ref, cw_ref, w1_ref, b1_ref, w2_ref, b2_ref, perm_ref, y_ref, w1s, w2s, hs):
    t = pl.program_id(0)
    e = te_ref[t]
    prev = te_ref[jnp.maximum(t - 1, 0)]
    new_expert = jnp.logical_or(t == 0, e != prev)
    valid = tv_ref[t] != 0
    d_model, two_f = w1s.shape
    n_blk = two_f // MXU_WIDTH
    half = MXU_WIDTH // 2

    @pl.when(new_expert)
    def _():
        for blk in range(n_blk):
            sl = slice(blk * MXU_WIDTH, (blk + 1) * MXU_WIDTH)
            wb = w1_ref[0, :, sl].astype(BF16)
            w1s[:, sl] = jnp.dot(wb, perm_ref[...], preferred_element_type=F32).astype(BF16)
        w2s[...] = w2_ref[0].astype(BF16)

    @pl.when(valid)
    def _():
        x = x_ref[...]
        for blk in range(n_blk):
            sl = slice(blk * MXU_WIDTH, (blk + 1) * MXU_WIDTH)
            u = jnp.dot(x, w1s[:, sl], preferred_element_type=F32) + b1_ref[0, :, sl]
            glu = jnp.minimum(u[:, :half], SWIGLU_LIMIT)
            lin = jnp.clip(u[:, half:], -SWIGLU_LIMIT, SWIGLU_LIMIT)
            act = glu * (1.0 / (1.0 + jnp.exp(-SWIGLU_ALPHA * glu))) * (lin + 1.0)
            hs[:, blk * half:(blk + 1) * half] = act.astype(BF16)
        y = jnp.dot(hs[...], w2s[...], preferred_element_type=F32) + b2_ref[0]
        y_ref[...] = cw_ref[...] * y

    @pl.when(jnp.logical_not(valid))
    def _():
        y_ref[...] = jnp.zeros_like(y_ref)


def _deinterleave_perm():
    half = MXU_WIDTH // 2
    src = np.arange(MXU_WIDTH)
    dst = np.where(src % 2 == 0, src // 2, half + src // 2)
    p = np.zeros((MXU_WIDTH, MXU_WIDTH), np.float32)
    p[src, dst] = 1.0
    return jnp.asarray(p, BF16)


def _moe_dispatch(top_i, top_w):
    n, k = top_i.shape
    tm = MOE_ROW_TILE
    p_rows = n * k + N_EXPERTS * tm
    sel = jnp.zeros((n, N_EXPERTS), jnp.int32).at[jnp.arange(n)[:, None], top_i].add(1)
    before = jnp.cumsum(sel, axis=0) - sel
    rank = jnp.take_along_axis(before, top_i, axis=1)
    counts = jnp.sum(sel, axis=0)
    padded = ((counts + tm - 1) // tm) * tm
    ends = jnp.cumsum(padded)
    pos = (ends - padded)[top_i] + rank
    flat_pos = pos.reshape(-1)
    src_tok = jnp.zeros((p_rows,), jnp.int32).at[flat_pos].set(jnp.repeat(jnp.arange(n, dtype=jnp.int32), k))
    cw = jnp.zeros((p_rows,), F32).at[flat_pos].set(top_w.reshape(-1))
    n_tiles = p_rows // tm
    tile_start = jnp.arange(n_tiles, dtype=jnp.int32) * tm
    tile_valid = (tile_start < ends[-1]).astype(jnp.int32)
    last_tile = ends[-1] // tm - 1
    tile_expert = jnp.sum(ends[None, :] <= jnp.minimum(tile_start, last_tile * tm)[:, None], axis=1).astype(jnp.int32)
    return src_tok, cw[:, None], pos, tile_expert, tile_valid


def moe_experts(xt, top_i, top_w, w1, b1, w2, b2):
    n, d = xt.shape
    e, _, two_f = w1.shape
    f = two_f // 2
    tm = MOE_ROW_TILE
    src_tok, cw, pos, tile_expert, tile_valid = _moe_dispatch(top_i, top_w)
    p_rows = src_tok.shape[0]
    xs = xt.astype(BF16)[src_tok]
    b1p = b1.reshape(e, two_f // MXU_WIDTH, MXU_WIDTH // 2, 2).transpose(0, 1, 3, 2).reshape(e, 1, two_f)
    grid_spec = pltpu.PrefetchScalarGridSpec(
        num_scalar_prefetch=2,
        grid=(p_rows // tm,),
        in_specs=[
            pl.BlockSpec((tm, d), lambda t, te, tv: (t, 0)),
            pl.BlockSpec((tm, 1), lambda t, te, tv: (t, 0)),
            pl.BlockSpec((1, d, two_f), lambda t, te, tv: (te[t], 0, 0)),
            pl.BlockSpec((1, 1, two_f), lambda t, te, tv: (te[t], 0, 0)),
            pl.BlockSpec((1, f, d), lambda t, te, tv: (te[t], 0, 0)),
            pl.BlockSpec((1, 1, d), lambda t, te, tv: (te[t], 0, 0)),
            pl.BlockSpec((MXU_WIDTH, MXU_WIDTH), lambda t, te, tv: (0, 0)),
        ],
        out_specs=pl.BlockSpec((tm, d), lambda t, te, tv: (t, 0)),
        scratch_shapes=[pltpu.VMEM((d, two_f), BF16), pltpu.VMEM((f, d), BF16), pltpu.VMEM((tm, f), BF16)],
    )
    ys = pl.pallas_call(
        _moe_kernel,
        grid_spec=grid_spec,
        out_shape=jax.ShapeDtypeStruct((p_rows, d), F32),
        compiler_params=pltpu.CompilerParams(dimension_semantics=("arbitrary",),
                                             vmem_limit_bytes=48 * 1024 * 1024),
        name="moe_experts",
    )(tile_expert, tile_valid, xs, cw, w1, b1p, w2, b2.reshape(e, 1, d), _deinterleave_perm())
    return jnp.sum(ys[pos], axis=1)


D_MODEL = 1024
DEPTH = 2
GRID_W = 64
ROPE_BASE = 10000.0
Q_BLOCK = 128
WINDOW = 128
NEG_INF = -1e30
A_HEADS, A_KV_HEADS, A_HD = 4, 2, 64
A_GROUP = A_HEADS // A_KV_HEADS
B_HEADS, B_NOPE, B_ROPE, B_VD, B_Q_LORA, B_KV_LORA = 4, 64, 32, 64, 192, 128
C_HEADS, C_DK, C_DV, C_GATE_RANK, C_GATE_TEMP, C_CHUNK = 4, 32, 64, 16, 16.0, 64
D_HEADS, D_N, D_DECAY_RANK, D_AAA_RANK, D_GATE_RANK, D_GN_EPS = 4, 64, 64, 64, 128, 64e-5
BRANCH_W = 256
N_BRANCH = 4
A_COLS = A_HEADS * A_HD + 2 * A_KV_HEADS * A_HD
B_COLS = B_Q_LORA + B_KV_LORA + B_ROPE
C_COLS = 2 * C_HEADS * C_DK + 2 * C_HEADS * C_DV + 2 * C_GATE_RANK
D_COLS = 3 * D_HEADS * D_N + 2 * D_DECAY_RANK + 2 * D_AAA_RANK + D_GATE_RANK
G_COLS = N_BRANCH * D_MODEL
ALPHA = (2 * DEPTH) ** 0.25


def split_cols(z, sizes):
    idx = np.cumsum(np.array(sizes))[:-1].tolist()
    return jnp.split(z, idx, axis=-1)


def layer_norm(x, g, b, eps=1e-5):
    mu = jnp.mean(x, -1, keepdims=True)
    var = jnp.mean(jnp.square(x - mu), -1, keepdims=True)
    return ((x - mu) * lax.rsqrt(var + eps)) * g + b


def rms_norm(x, g, eps=1e-6):
    return (x * lax.rsqrt(jnp.mean(jnp.square(x), -1, keepdims=True) + eps)) * g


def to_heads(x, n):
    b, t, _ = x.shape
    return x.reshape(b, t, n, -1).transpose(0, 2, 1, 3)


def from_heads(x):
    b, n, t, d = x.shape
    return x.transpose(0, 2, 1, 3).reshape(b, t, n * d)


def axial_rope(n_tok, rot_dim):
    rows = n_tok // GRID_W
    row = jnp.repeat(jnp.arange(rows, dtype=F32), GRID_W)
    col = jnp.tile(jnp.arange(GRID_W, dtype=F32), rows)
    quarter = rot_dim // 4
    inv = ROPE_BASE ** (-jnp.arange(quarter, dtype=F32) / quarter)
    ang = jnp.concatenate([row[:, None] * inv, col[:, None] * inv], axis=-1)
    return jnp.cos(ang), jnp.sin(ang)


def apply_axial_rope(x, cos, sin):
    q = x.shape[-1] // 4
    cr, cc, sr, sc = cos[..., :q], cos[..., q:], sin[..., :q], sin[..., q:]
    p0, p1, p2, p3 = x[..., :q], x[..., q:2 * q], x[..., 2 * q:3 * q], x[..., 3 * q:]
    return jnp.concatenate([p0 * cr - p1 * sr, p1 * cr + p0 * sr,
                            p2 * cc - p3 * sc, p3 * cc + p2 * sc], axis=-1)


def attn_probs(s, sink):
    if sink is None:
        return jax.nn.softmax(s, axis=-1)
    sk = sink.astype(F32)[None, :, :, None, None]
    m = jnp.maximum(jnp.max(s, -1, keepdims=True), sk)
    e = jnp.exp(s - m)
    return e / (jnp.sum(e, -1, keepdims=True) + jnp.exp(sk - m))


def blocked_attention(q, k, v, scale, sink):
    s = jnp.einsum('bkgqd,bksd->bkgqs', q, k).astype(F32) * scale
    p = attn_probs(s, sink)
    return jnp.einsum('bkgqs,bksv->bkgqv', p, v)


def window_context_attention(q, k, v, k_ctx, v_ctx, scale, sink):
    b, kvh, g, t, d = q.shape
    nb = t // Q_BLOCK
    pad = ((0, 0), (0, 0), (Q_BLOCK, Q_BLOCK), (0, 0))

    def bands(x):
        xb = jnp.pad(x, pad).reshape(b, kvh, nb + 2, Q_BLOCK, x.shape[-1])
        return jnp.concatenate([xb[:, :, :-2], xb[:, :, 1:-1], xb[:, :, 2:]], axis=3)

    kw, vw = bands(k), bands(v)
    qb = q.reshape(b, kvh, g, nb, Q_BLOCK, d)
    qpos = jnp.arange(nb)[:, None] * Q_BLOCK + jnp.arange(Q_BLOCK)[None]
    kpos = jnp.arange(nb)[:, None] * Q_BLOCK - Q_BLOCK + jnp.arange(3 * Q_BLOCK)[None]
    mask = ((jnp.abs(qpos[:, :, None] - kpos[:, None, :]) <= WINDOW)
            & (kpos[:, None, :] >= 0) & (kpos[:, None, :] < t))
    s_win = jnp.where(mask[None, None, None], jnp.einsum('bkgnqd,bknsd->bkgnqs', qb, kw) * scale, NEG_INF)
    s_ctx = jnp.einsum('bkgnqd,bksd->bkgnqs', qb, k_ctx) * scale
    s = jnp.concatenate([s_win, s_ctx], axis=-1)
    sk = sink.astype(F32)[None, :, :, None, None, None]
    m = jnp.maximum(jnp.max(s, -1, keepdims=True), sk)
    e = jnp.exp(s - m)
    p = e / (jnp.sum(e, -1, keepdims=True) + jnp.exp(sk - m))
    o = (jnp.einsum('bkgnqs,bknsv->bkgnqv', p[..., :3 * Q_BLOCK], vw)
         + jnp.einsum('bkgnqs,bksv->bkgnqv', p[..., 3 * Q_BLOCK:], v_ctx))
    return o.reshape(b, kvh, g, t, v.shape[-1])


def gla_chunked(q, k, v, log_a, s0):
    b, h, t, dk = q.shape
    dv = v.shape[-1]
    nc = t // C_CHUNK

    def chunks(x):
        return jnp.moveaxis(x.astype(F32).reshape(b, h, nc, C_CHUNK, x.shape[-1]), 2, 0)

    tril = jnp.tril(jnp.ones((C_CHUNK, C_CHUNK), dtype=bool))

    def step(S, xs):
        qc, kc, vc, lac = xs
        cb = jnp.cumsum(lac, axis=-2)
        inter = jnp.einsum('bhtd,bhdv->bhtv', qc * jnp.exp(cb), S)
        diff = jnp.where(tril[:, :, None], cb[..., :, None, :] - cb[..., None, :, :], -jnp.inf)
        att = jnp.einsum('bhtd,bhsd,bhtsd->bhts', qc, kc, jnp.exp(diff))
        intra = jnp.einsum('bhts,bhsv->bhtv', att, vc)
        cend = cb[..., -1:, :]
        S = jnp.exp(cend[..., 0, :])[..., None] * S + jnp.einsum('bhsd,bhsv->bhdv', kc * jnp.exp(cend - cb), vc)
        return S, inter + intra

    S, o = lax.scan(step, s0.astype(F32), (chunks(q), chunks(k), chunks(v), chunks(log_a)))
    return jnp.moveaxis(o, 0, 2).reshape(b, h, t, dv), S


def centred_shift(x):
    xp = jnp.pad(x, ((0, 0), (1, 1), (0, 0)))
    return 0.5 * (xp[:, :-2] + xp[:, 2:])


def mla_keys_values(lp, ckv, kpe):
    bn, s, _ = ckv.shape
    kv = to_heads(ckv @ lp['b_w_ukv'], B_HEADS)
    k_pe = jnp.broadcast_to(kpe[:, None], (bn, B_HEADS, s, B_ROPE))
    return jnp.concatenate([kv[..., :B_NOPE], k_pe], axis=-1), kv[..., B_NOPE:]


def rwkv_bidirectional(xs_dirs, s0):
    bn, h, t, n = xs_dirs[0][0].shape
    stacked = []
    for i in range(6):
        stacked.append(jnp.stack([xs_dirs[0][i], jnp.flip(xs_dirs[1][i], axis=2)], axis=1).reshape(bn * 2 * h, t, n))
    y, s_fin = rwkv7_chunked(*stacked, s0.reshape(bn * 2 * h, n, n))
    y = y.reshape(bn, 2, h, t, n)
    return y[:, 0] + jnp.flip(y[:, 1], axis=2), s_fin.reshape(bn, 2, h, n, n)


def rope_heads(x, n_heads, cos, sin):
    b, t, w = x.shape
    return apply_axial_rope(x.reshape(b, t, n_heads, w // n_heads), cos[:, None, :], sin[:, None, :]).reshape(b, t, w)


def token_mixers(lp, h, ctx, layer=0, caches=None):
    latent = ctx is not None
    bn, t, _ = h.shape
    z = jnp.einsum('btd,de->bte', h, lp['w_in'])
    za, zb, zc, zd, zg = split_cols(z, (A_COLS, B_COLS, C_COLS, D_COLS, G_COLS))

    a_q, a_k, a_v = split_cols(za, (A_HEADS * A_HD, A_KV_HEADS * A_HD, A_KV_HEADS * A_HD))
    if latent:
        cos, sin = axial_rope(t, A_HD)
        o_a = gqa_attention(rope_heads(a_q, A_HEADS, cos, sin), rope_heads(a_k, A_KV_HEADS, cos, sin), a_v,
                            lp['a_sink'], hd=A_HD, scale=A_HD ** -0.5, ctx_kv=caches, layer=layer)
    else:
        o_a = gqa_attention(a_q, a_k, a_v, lp['a_sink'], hd=A_HD, scale=A_HD ** -0.5)
    a_k = to_heads(a_k, A_KV_HEADS)
    a_v = to_heads(a_v, A_KV_HEADS)

    b_cq, b_ckv, b_kpe = split_cols(zb, (B_Q_LORA, B_KV_LORA, B_ROPE))
    b_q = rms_norm(b_cq, lp['b_q_norm']) @ lp['b_w_uq']
    b_ckv = rms_norm(b_ckv, lp['b_kv_norm'])
    if latent:
        cos, sin = axial_rope(t, B_ROPE)
        b_q4 = b_q.reshape(bn, t, B_HEADS, B_NOPE + B_ROPE)
        b_q = jnp.concatenate([b_q4[..., :B_NOPE], apply_axial_rope(b_q4[..., B_NOPE:], cos[:, None], sin[:, None])],
                              axis=-1).reshape(bn, t, -1)
        ckv_all = jnp.concatenate([ctx[2], b_ckv], axis=1)
        kpe_all = jnp.concatenate([ctx[3], apply_axial_rope(b_kpe, cos, sin)], axis=1)
    else:
        ckv_all, kpe_all = b_ckv, b_kpe
    o_b = mla_attention(b_q, ckv_all, kpe_all, lp['b_w_ukv'], n_heads=B_HEADS, nope=B_NOPE, rope=B_ROPE, vd=B_VD)

    c_q, c_k, c_v, c_og, c_af, c_ab = split_cols(
        zc, (C_HEADS * C_DK, C_HEADS * C_DK, C_HEADS * C_DV, C_HEADS * C_DV, C_GATE_RANK, C_GATE_RANK))
    cq = to_heads(c_q, C_HEADS) * (C_DK ** -0.5)
    ck = to_heads(c_k, C_HEADS)
    cv = to_heads(c_v, C_HEADS)
    c_s0 = ctx[4] if latent else jnp.zeros((bn, 2, C_HEADS, C_DK, C_DV), F32)
    log_as = [to_heads(jax.nn.log_sigmoid(lr @ lp['c_w_gate'][direction] + lp['c_b_gate'][direction]) / C_GATE_TEMP,
                       C_HEADS) for direction, lr in enumerate((c_af, c_ab))]

    def both(x_f, x_b):
        return jnp.stack([x_f, jnp.flip(x_b, axis=2)], axis=1).reshape(bn * 2 * C_HEADS, t, x_f.shape[-1])

    c_o, c_states = gla_chunked_pallas(both(cq, cq), both(ck, ck), both(*log_as), both(cv, cv),
                                       c_s0.reshape(bn * 2 * C_HEADS, C_DK, C_DV))
    c_o = c_o.reshape(bn, 2, C_HEADS, t, C_DV)
    c_states = c_states.reshape(bn, 2, C_HEADS, C_DK, C_DV)
    o_c = from_heads(rms_norm(c_o[:, 0] + jnp.flip(c_o[:, 1], axis=2), lp['c_norm'])) * jax.nn.silu(c_og)

    zd = zd + (centred_shift(zd) - zd) * lp['d_mu']
    hn = D_HEADS * D_N
    d_r, d_k, d_v, d_wf, d_wb, d_af, d_ab, d_g = split_cols(
        zd, (hn, hn, hn, D_DECAY_RANK, D_DECAY_RANK, D_AAA_RANK, D_AAA_RANK, D_GATE_RANK))
    kk = to_heads(d_k * lp['d_k_k'], D_HEADS)
    kk = kk / jnp.maximum(jnp.sqrt(jnp.sum(jnp.square(kk), -1, keepdims=True)), 1e-12)
    r_h = to_heads(d_r, D_HEADS)
    v_h = to_heads(d_v, D_HEADS)
    d_s0 = ctx[5] if latent else jnp.zeros((bn, 2, D_HEADS, D_N, D_N), F32)
    xs_dirs, bonus = [], 0.0
    for direction, (lw, la) in enumerate(((d_wf, d_af), (d_wb, d_ab))):
        w_log = -jax.nn.softplus(-(lp['d_w0'][direction] + jnp.tanh(lw) @ lp['d_w2'][direction])) - 0.5
        log_decay = to_heads(-jnp.exp(w_log), D_HEADS)
        a = jax.nn.sigmoid(lp['d_a0'][direction] + la @ lp['d_a2'][direction])
        k_dir = to_heads(d_k * (1.0 + (a - 1.0) * lp['d_k_a']), D_HEADS)
        a_h = to_heads(a, D_HEADS)
        xs_dirs.append((r_h, log_decay, k_dir, v_h, kk, a_h))
        bonus = bonus + jnp.sum(r_h * k_dir * lp['d_r_k'][None, :, None, :], -1, keepdims=True) * v_h
    d_y, d_states = rwkv_bidirectional(xs_dirs, d_s0)
    mu = jnp.mean(d_y, -1, keepdims=True)
    var = jnp.mean(jnp.square(d_y - mu), -1, keepdims=True)
    y_gn = from_heads((d_y - mu) * lax.rsqrt(var + D_GN_EPS)) * lp['d_ln_g'] + lp['d_ln_b']
    o_d = (y_gn + from_heads(bonus)) * (jax.nn.sigmoid(d_g) @ lp['d_g2'])

    gates = jax.nn.sigmoid(zg).reshape(bn, t, N_BRANCH, D_MODEL)
    branches = jnp.stack([o_a, o_b, o_c, o_d], axis=2)
    proj = jnp.einsum('btnw,nwd->btnd', branches, lp['w_br'])
    out = jnp.sum(gates * proj, axis=2) @ lp['w_out']
    if latent:
        return out, None
    return out, (a_k, a_v, b_ckv, b_kpe, c_states, d_states)


def routed_experts(lp, h):
    n, d = h.shape
    logits = jnp.dot(h, lp['w_router'], precision=lax.Precision.HIGHEST) + lp['b_router']
    top_v, top_i = lax.top_k(logits, TOP_K)
    top_w = jax.nn.softmax(top_v, axis=-1)
    return moe_experts(h, top_i, top_w, lp['w_mlp1'], lp['b_mlp1'], lp['w_mlp2'], lp['b_mlp2'])


def modulation(lp, cond):
    mod = (jax.nn.silu(cond) @ lp['w_mod'] + lp['b_mod'])[:, None, :]
    return jnp.split(mod, 6, axis=-1)


def kernel(x_prompt, x_sample, cache_a_k, cache_a_v, cache_b_ckv, cache_b_kpe, state_c, state_d, c,
           c_ctx, w_mod, b_mod, w_in, a_sink, b_q_norm, b_w_uq, b_kv_norm, b_w_ukv, c_w_gate, c_b_gate,
           c_norm, d_mu, d_w0, d_w2, d_a0, d_a2, d_g2, d_k_k, d_k_a, d_r_k, d_ln_g, d_ln_b, w_br, w_out,
           ln_g, ln_b, w_router, b_router, w_mlp1, b_mlp1, w_mlp2, b_mlp2):
    params = dict(w_mod=w_mod, b_mod=b_mod, w_in=w_in, a_sink=a_sink, b_q_norm=b_q_norm, b_w_uq=b_w_uq,
                  b_kv_norm=b_kv_norm, b_w_ukv=b_w_ukv, c_w_gate=c_w_gate, c_b_gate=c_b_gate, c_norm=c_norm,
                  d_mu=d_mu, d_w0=d_w0, d_w2=d_w2, d_a0=d_a0, d_a2=d_a2, d_g2=d_g2, d_k_k=d_k_k, d_k_a=d_k_a,
                  d_r_k=d_r_k, d_ln_g=d_ln_g, d_ln_b=d_ln_b, w_br=w_br, w_out=w_out, ln_g=ln_g, ln_b=ln_b,
                  w_router=w_router, b_router=b_router, w_mlp1=w_mlp1, b_mlp1=b_mlp1, w_mlp2=w_mlp2,
                  b_mlp2=b_mlp2)
    y_p, y_s = x_prompt, x_sample
    ctx_layers = []
    n_p = x_prompt.shape[0] * x_prompt.shape[1]
    for l in range(DEPTH):
        lp = {name: val[l] for name, val in params.items()}
        mod_p = modulation(lp, c_ctx[None])
        mod_s = modulation(lp, c)
        mix_p, ctx_out = token_mixers(lp, y_p * (1.0 + mod_p[1]) + mod_p[0], None)
        ctx = (cache_a_k[:, l], cache_a_v[:, l], cache_b_ckv[:, l], cache_b_kpe[:, l], state_c[:, l], state_d[:, l])
        mix_s, _ = token_mixers(lp, y_s * (1.0 + mod_s[1]) + mod_s[0], ctx, layer=l, caches=(cache_a_k, cache_a_v))
        ctx_layers.append(ctx_out)
        y_p = layer_norm(ALPHA * y_p + mod_p[2] * mix_p, lp['ln_g'][0], lp['ln_b'][0])
        y_s = layer_norm(ALPHA * y_s + mod_s[2] * mix_s, lp['ln_g'][0], lp['ln_b'][0])
        h_p = (y_p * (1.0 + mod_p[4]) + mod_p[3]).reshape(n_p, D_MODEL)
        h_s = (y_s * (1.0 + mod_s[4]) + mod_s[3]).reshape(-1, D_MODEL)
        moe = routed_experts(lp, jnp.concatenate([h_p, h_s], axis=0))
        y_p = layer_norm(ALPHA * y_p + mod_p[5] * moe[:n_p].reshape(y_p.shape), lp['ln_g'][1], lp['ln_b'][1])
        y_s = layer_norm(ALPHA * y_s + mod_s[5] * moe[n_p:].reshape(y_s.shape), lp['ln_g'][1], lp['ln_b'][1])
    outs = [jnp.stack([t[i] for t in ctx_layers], axis=1) for i in range(6)]
    return (y_p, y_s, *outs)
```

```python
import functools

import jax
import jax.numpy as jnp
import numpy as np
from jax import lax
from jax.experimental import pallas as pl
from jax.experimental.pallas import tpu as pltpu

F32 = jnp.float32
BF16 = jnp.bfloat16

MXU_WIDTH = 256
VMEM_LIMIT = 56 * 1024 * 1024

D_MODEL = 1024
DEPTH = 2
GRID_W = 64
ROPE_BASE = 10000.0
A_HEADS, A_KV_HEADS, A_HD = 4, 2, 64
B_HEADS, B_NOPE, B_ROPE, B_VD, B_Q_LORA, B_KV_LORA = 4, 64, 32, 64, 192, 128
C_HEADS, C_DK, C_DV, C_GATE_RANK, C_GATE_TEMP = 4, 32, 64, 16, 16.0
D_HEADS, D_N, D_DECAY_RANK, D_AAA_RANK, D_GATE_RANK, D_GN_EPS = 4, 64, 64, 64, 128, 64e-5
BRANCH_W = 256
N_BRANCH = 4
N_EXPERTS = 32
TOP_K = 4
SWIGLU_LIMIT = 7.0
SWIGLU_ALPHA = 1.702
ALPHA = (2 * DEPTH) ** 0.25
LN_EPS = 1e-5
RMS_EPS = 1e-6

N_CTX_SEQ, CTX_LEN = 16, 256
N_LAT_SEQ, LAT_LEN = 2, 2048
N_CTX = N_CTX_SEQ * CTX_LEN
N_TOK = N_CTX + N_LAT_SEQ * LAT_LEN
ROW_TILE = 256
N_TILES = N_TOK // ROW_TILE
CTX_TILES = N_CTX // ROW_TILE
LAT_TILES_PER_SEQ = LAT_LEN // ROW_TILE
N_SEQ = N_CTX_SEQ + N_LAT_SEQ

_ORIG = dict(aq=(0, 256), ak=(256, 384), av=(384, 512), bcq=(512, 704), bckv=(704, 832), bkpe=(832, 864),
             cq=(864, 992), ck=(992, 1120), cv=(1120, 1376), cog=(1376, 1632), caf=(1632, 1648), cab=(1648, 1664),
             zd=(1664, 2816))
_ORDER = ("aq", "ak", "av", "cq", "ck", "cv", "cog", "zd", "bcq", "caf", "cab", "bkpe", "bckv")
COL = {}
_off = 0
for _name in _ORDER:
    _w = _ORIG[_name][1] - _ORIG[_name][0]
    COL[_name] = (_off, _off + _w)
    _off += _w
SMALL_COLS = _off
G_START = 2816


def _cs(name):
    return slice(*COL[name])


def _split3(x):
    hi = x.astype(BF16)
    r1 = x - hi.astype(F32)
    mid = r1.astype(BF16)
    lo = (r1 - mid.astype(F32)).astype(BF16)
    return hi, mid, lo


def _split2(x):
    hi = x.astype(BF16)
    lo = (x - hi.astype(F32)).astype(BF16)
    return hi, lo


def _bdot(a, b, dims):
    return lax.dot_general(a, b, dims, preferred_element_type=F32)


_D2 = (((1,), (0,)), ((), ()))
_D2T = (((1,), (1,)), ((), ()))
_NN = (((2,), (1,)), ((0,), (0,)))
_NT = (((2,), (2,)), ((0,), (0,)))
_TN = (((1,), (1,)), ((0,), (0,)))


def _dot1(a, b, dims=_D2):
    return _bdot(a.astype(BF16), b.astype(BF16), dims)


def _dot3(a, b, dims=_D2):
    ah, al = _split2(a)
    bh, bl = _split2(b)
    return _bdot(ah, bh, dims) + (_bdot(ah, bl, dims) + _bdot(al, bh, dims))


def _dot_exact_lhs(a01, b, dims=_D2):
    a = a01.astype(BF16)
    h, m, l = _split3(b)
    return _bdot(a, h, dims) + (_bdot(a, m, dims) + _bdot(a, l, dims))


def _dot_exact_rhs(a, b01, dims=_D2):
    b = b01.astype(BF16)
    h, m, l = _split3(a)
    return _bdot(h, b, dims) + (_bdot(m, b, dims) + _bdot(l, b, dims))


def _dot6(a, b, dims=_D2):
    ah, am, al = _split3(a)
    bh, bm, bl = _split3(b)
    return (_bdot(ah, bh, dims) + (_bdot(ah, bm, dims) + _bdot(am, bh, dims))
            + (_bdot(am, bm, dims) + (_bdot(ah, bl, dims) + _bdot(al, bh, dims))))


def _sigmoid(x):
    return 1.0 / (1.0 + jnp.exp(-x))


def _softplus(x):
    return jnp.maximum(x, 0.0) + jnp.log(1.0 + jnp.exp(-jnp.abs(x)))


def _mod_row(t):
    return jnp.where(t < CTX_TILES, 0, 1 + (t - CTX_TILES) // LAT_TILES_PER_SEQ)


def _full(shape):
    nd = len(shape)
    return pl.BlockSpec(shape, lambda *_: (0,) * nd)


MOD_COL_TILE = 1536


def _mod_kernel(c_ref, w_ref, b_ref, o_ref):
    c = c_ref[...]
    o_ref[0] = _dot3(c * _sigmoid(c), w_ref[0]) + b_ref[0]


def modulation_table(cond8, w_mod, b_mod):
    depth, d, six_d = w_mod.shape
    return pl.pallas_call(
        _mod_kernel,
        grid=(depth, six_d // MOD_COL_TILE),
        in_specs=[pl.BlockSpec((8, d), lambda l, j: (0, 0)),
                  pl.BlockSpec((1, d, MOD_COL_TILE), lambda l, j: (l, 0, j)),
                  pl.BlockSpec((1, 1, MOD_COL_TILE), lambda l, j: (l, 0, j))],
        out_specs=pl.BlockSpec((1, 8, MOD_COL_TILE), lambda l, j: (l, 0, j)),
        out_shape=jax.ShapeDtypeStruct((depth, 8, six_d), F32),
        compiler_params=pltpu.CompilerParams(dimension_semantics=("parallel", "parallel")),
        name="modulation",
    )(cond8, w_mod, b_mod.reshape(depth, 1, six_d))


def _rot_pairs(x, half, lane_mod_base=0):
    w = x.shape[-1]
    lane = lax.broadcasted_iota(jnp.int32, (1, w), 1) - lane_mod_base
    first = (lane % (2 * half)) < half
    return jnp.where(first, -pltpu.roll(x, w - half, axis=1), pltpu.roll(x, half, axis=1))


def _pre_kernel(x_ref, xp_ref, xn_ref, mod_ref, w_ref, ca_ref, sa_ref, cb_ref, sb_ref, ck_ref, sk_ref,
                qnorm_ref, kvnorm_ref, wuq_ref, cwg_ref, cbg_ref, rep_ref, mu_ref, dw0_ref, dw2_ref, da0_ref,
                da2_ref, dg2_ref, dkk_ref, dka_ref, drk_ref, bd_ref,
                aq_o, ak_o, av_o, bq_o, bckv_o, bkpe_o, cq4_o, ck4_o, claf_o, clab_o, cv_o, cgate_o,
                r_o, v_o, kk_o, lwf_o, lwb_o, kf_o, kb_o, af_o, ab_o, bonus_o, dgate_o):
    t = pl.program_id(0)
    tm = x_ref.shape[0]
    sh1 = mod_ref[0, 0:1, :]
    sc1 = mod_ref[0, 1:2, :]

    def modulate(xv):
        return (xv * (1.0 + sc1) + sh1).astype(BF16)

    z = jnp.dot(modulate(x_ref[...]), w_ref[...], preferred_element_type=F32)

    aq = z[:, _cs("aq")]
    ak = z[:, _cs("ak")]
    aq_o[...] = aq * ca_ref[...] + _rot_pairs(aq, A_HD // 4) * sa_ref[...]
    ak_o[...] = ak * ca_ref[:, :ak.shape[1]] + _rot_pairs(ak, A_HD // 4) * sa_ref[:, :ak.shape[1]]
    av_o[...] = z[:, _cs("av")]

    bcq = z[:, _cs("bcq")]
    qn = bcq * lax.rsqrt(jnp.mean(bcq * bcq, axis=-1, keepdims=True) + RMS_EPS) * qnorm_ref[...]
    bq = _dot1(qn, wuq_ref[...])
    bq_o[...] = bq * cb_ref[...] + _rot_pairs(bq, B_ROPE // 4, lane_mod_base=B_NOPE) * sb_ref[...]
    bckv = z[:, _cs("bckv")]
    bckv_o[...] = bckv * lax.rsqrt(jnp.mean(bckv * bckv, axis=-1, keepdims=True) + RMS_EPS) * kvnorm_ref[...]
    kpe_lo = COL["bkpe"][0] // 128 * 128
    kblk = z[:, kpe_lo:kpe_lo + 128]
    kblk = kblk * ck_ref[...] + _rot_pairs(kblk, B_ROPE // 4) * sk_ref[...]
    bkpe_o[...] = kblk[:, COL["bkpe"][0] - kpe_lo:COL["bkpe"][1] - kpe_lo]

    rep = rep_ref[...]
    cq4_o[...] = _dot_exact_rhs(z[:, _cs("cq")] * (C_DK ** -0.5), rep)
    ck4_o[...] = _dot_exact_rhs(z[:, _cs("ck")], rep)
    cv_o[...] = z[:, _cs("cv")]
    cog = z[:, _cs("cog")]
    cgate_o[...] = cog * _sigmoid(cog)
    for direction, (name, out) in enumerate((("caf", claf_o), ("cab", clab_o))):
        pre = _dot3(z[:, _cs(name)], cwg_ref[direction]) + cbg_ref[direction]
        out[...] = _dot_exact_rhs(-_softplus(-pre) * (1.0 / C_GATE_TEMP), rep)

    zd_cols = _cs("zd")
    zd = z[:, zd_cols]
    wd = w_ref[:, zd_cols]
    j = (t - CTX_TILES) % LAT_TILES_PER_SEQ
    latent = t >= CTX_TILES
    has_prev = jnp.logical_and(latent, j != 0)
    has_next = jnp.logical_and(latent, j != LAT_TILES_PER_SEQ - 1)
    zp = jnp.dot(modulate(xp_ref[...]), wd, preferred_element_type=F32)
    zn = jnp.dot(modulate(xn_ref[...]), wd, preferred_element_type=F32)
    prev_row = jnp.where(has_prev, zp[7:8], 0.0)
    next_row = jnp.where(has_next, zn[0:1], 0.0)
    row = lax.broadcasted_iota(jnp.int32, (tm, 1), 0)
    up = jnp.where(row == 0, prev_row, pltpu.roll(zd, 1, axis=0))
    dn = jnp.where(row == tm - 1, next_row, pltpu.roll(zd, tm - 1, axis=0))
    zd = zd + (0.5 * (up + dn) - zd) * mu_ref[...]

    hn = D_HEADS * D_N
    d_r, d_k, d_v = zd[:, :hn], zd[:, hn:2 * hn], zd[:, 2 * hn:3 * hn]
    o = 3 * hn
    d_w = (zd[:, o:o + D_DECAY_RANK], zd[:, o + D_DECAY_RANK:o + 2 * D_DECAY_RANK])
    o += 2 * D_DECAY_RANK
    d_a = (zd[:, o:o + D_AAA_RANK], zd[:, o + D_AAA_RANK:o + 2 * D_AAA_RANK])
    o += 2 * D_AAA_RANK
    d_g = zd[:, o:o + D_GATE_RANK]
    bd = bd_ref[...]
    kk = d_k * dkk_ref[...]
    kk = kk / jnp.maximum(jnp.sqrt(_dot_exact_rhs(kk * kk, bd)), 1e-12)
    r_o[...] = d_r
    v_o[...] = d_v
    kk_o[...] = kk
    k_sum = None
    for direction, (lw_o, k_o, a_o) in enumerate(((lwf_o, kf_o, af_o), (lwb_o, kb_o, ab_o))):
        w_log = -_softplus(-(dw0_ref[direction] + _dot3(jnp.tanh(d_w[direction]), dw2_ref[direction]))) - 0.5
        lw_o[...] = -jnp.exp(w_log)
        a = _sigmoid(da0_ref[direction] + _dot3(d_a[direction], da2_ref[direction]))
        k_dir = d_k * (1.0 + (a - 1.0) * dka_ref[...])
        k_o[...] = k_dir
        a_o[...] = a
        k_sum = k_dir if k_sum is None else k_sum + k_dir
    bonus_o[...] = d_v * _dot_exact_rhs(d_r * drk_ref[...] * k_sum, bd)
    dgate_o[...] = _dot3(_sigmoid(d_g), dg2_ref[...])


def _rope_tables():
    pos = np.arange(LAT_LEN)
    rowp, colp = (pos // GRID_W).astype(np.float32), (pos % GRID_W).astype(np.float32)

    def head_tables(rot_dim):
        quarter = rot_dim // 4
        inv = (ROPE_BASE ** (-jnp.arange(quarter, dtype=F32) / quarter))
        ar = jnp.asarray(rowp)[:, None] * inv
        ac = jnp.asarray(colp)[:, None] * inv
        cos = jnp.concatenate([jnp.cos(ar), jnp.cos(ar), jnp.cos(ac), jnp.cos(ac)], axis=-1)
        sin = jnp.concatenate([jnp.sin(ar), jnp.sin(ar), jnp.sin(ac), jnp.sin(ac)], axis=-1)
        return cos, sin

    def with_identity(c, s):
        w = c.shape[1]
        return (jnp.concatenate([jnp.ones((ROW_TILE, w), F32), c], axis=0),
                jnp.concatenate([jnp.zeros((ROW_TILE, w), F32), s], axis=0))

    ca, sa = head_tables(A_HD)
    ca, sa = with_identity(jnp.tile(ca, (1, A_HEADS)), jnp.tile(sa, (1, A_HEADS)))
    cbh, sbh = head_tables(B_ROPE)
    ones, zeros = jnp.ones((LAT_LEN, B_NOPE), F32), jnp.zeros((LAT_LEN, B_NOPE), F32)
    cb, sb = with_identity(jnp.tile(jnp.concatenate([ones, cbh], axis=1), (1, B_HEADS)),
                           jnp.tile(jnp.concatenate([zeros, sbh], axis=1), (1, B_HEADS)))
    pad = 128 - B_ROPE
    ck, sk = with_identity(jnp.concatenate([jnp.ones((LAT_LEN, pad), F32), cbh], axis=1),
                           jnp.concatenate([jnp.zeros((LAT_LEN, pad), F32), sbh], axis=1))
    return ca, sa, cb, sb, ck, sk


def _lane_repeat_matrix():
    m = np.zeros((C_HEADS * C_DK, C_HEADS * 128), np.float32)
    for h in range(C_HEADS):
        for g in range(128 // C_DK):
            for d in range(C_DK):
                m[h * C_DK + d, h * 128 + g * C_DK + d] = 1.0
    return jnp.asarray(m, BF16)


def _head_block_diag():
    m = np.kron(np.eye(D_HEADS, dtype=np.float32), np.ones((D_N, D_N), np.float32))
    return jnp.asarray(m, BF16)


PRE_OUT_WIDTHS = (256, 128, 128, 384, 128, 32, 512, 512, 512, 512, 256, 256) + (256,) * 11


def mixer_prelude(x, mod_l, w_small, tables, lp):
    tm = ROW_TILE
    tab_idx = lambda t: (jnp.where(t < CTX_TILES, 0, 1 + (t - CTX_TILES) % LAT_TILES_PER_SEQ), 0)
    hn = D_HEADS * D_N
    small = [lp['b_q_norm'].reshape(1, -1), lp['b_kv_norm'].reshape(1, -1), lp['b_w_uq'], lp['c_w_gate'],
             lp['c_b_gate'].reshape(2, 1, -1), _lane_repeat_matrix(), lp['d_mu'].reshape(1, -1),
             lp['d_w0'].reshape(2, 1, hn), lp['d_w2'], lp['d_a0'].reshape(2, 1, hn), lp['d_a2'], lp['d_g2'],
             lp['d_k_k'].reshape(1, hn), lp['d_k_a'].reshape(1, hn), lp['d_r_k'].reshape(1, hn), _head_block_diag()]
    in_specs = ([pl.BlockSpec((tm, D_MODEL), lambda t: (t, 0)),
                 pl.BlockSpec((8, D_MODEL), lambda t: (jnp.maximum(t * (tm // 8) - 1, 0), 0)),
                 pl.BlockSpec((8, D_MODEL), lambda t: (jnp.minimum((t + 1) * (tm // 8), N_TOK // 8 - 1), 0)),
                 pl.BlockSpec((1, 6, D_MODEL), lambda t: (_mod_row(t), 0, 0)),
                 _full(w_small.shape)]
                + [pl.BlockSpec((tm, tab.shape[1]), tab_idx) for tab in tables]
                + [_full(a.shape) for a in small])
    return pl.pallas_call(
        _pre_kernel,
        grid=(N_TILES,),
        in_specs=in_specs,
        out_specs=[pl.BlockSpec((tm, w), lambda t: (t, 0)) for w in PRE_OUT_WIDTHS],
        out_shape=[jax.ShapeDtypeStruct((N_TOK, w), F32) for w in PRE_OUT_WIDTHS],
        compiler_params=pltpu.CompilerParams(dimension_semantics=("parallel",), vmem_limit_bytes=VMEM_LIMIT),
        name="mixer_prelude",
    )(x, x, x, mod_l, w_small, *tables, *small)


ATT_Q_BLOCK = 128
ATT_WINDOW = 128
ATT_NEG_INF = -1e30
CACHE_LEN = 512


def _softmax_pv(s, v, sink):
    m = jnp.max(s, axis=-1, keepdims=True)
    if sink is not None:
        m = jnp.maximum(m, sink)
    e = jnp.exp(s - m)
    den = jnp.sum(e, axis=-1, keepdims=True)
    if sink is not None:
        den = den + jnp.exp(sink - m)
    return jnp.dot(e.astype(BF16), v.astype(BF16), preferred_element_type=F32) / den


def _gqa_kernel(sink_ref, q_ref, k_ref, v_ref, *rest, hd, group, scale, windowed):
    if windowed:
        kp_ref, kn_ref, vp_ref, vn_ref, kc_ref, vc_ref, _, o_ref = rest
    else:
        (o_ref,) = rest
    i = pl.program_id(1)
    tq = q_ref.shape[0]
    n_kv = k_ref.shape[1] // hd
    if windowed:
        qpos = i * tq + lax.broadcasted_iota(jnp.int32, (tq, 3 * tq), 0)
        kpos = (i - 1) * tq + lax.broadcasted_iota(jnp.int32, (tq, 3 * tq), 1)
        n_tok = pl.num_programs(1) * tq
        mask = (jnp.abs(qpos - kpos) <= ATT_WINDOW) & (kpos >= 0) & (kpos < n_tok)
        mask = jnp.concatenate([mask] * group, axis=0)
    for kvh in range(n_kv):
        ks = slice(kvh * hd, (kvh + 1) * hd)
        qs = [q_ref[:, (kvh * group + g) * hd:(kvh * group + g + 1) * hd] for g in range(group)]
        q = (jnp.concatenate(qs, axis=0) * scale).astype(BF16)
        sink = jnp.concatenate(
            [jnp.full((tq, 1), sink_ref[kvh * group + g], F32) for g in range(group)], axis=0)
        if windowed:
            k_win = jnp.concatenate([kp_ref[:, ks], k_ref[:, ks], kn_ref[:, ks]], axis=0)
            v_win = jnp.concatenate([vp_ref[:, ks], v_ref[:, ks], vn_ref[:, ks]], axis=0)
            s_win = _bdot(q, k_win.astype(BF16), _D2T)
            s_win = jnp.where(mask, s_win, ATT_NEG_INF)
            s_ctx = _bdot(q, kc_ref[0, 0, kvh].astype(BF16), _D2T)
            s = jnp.concatenate([s_win, s_ctx], axis=1)
            v = jnp.concatenate([v_win, vc_ref[0, 0, kvh]], axis=0)
        else:
            s = _bdot(q, k_ref[:, ks].astype(BF16), _D2T)
            v = v_ref[:, ks]
        o = _softmax_pv(s, v, sink)
        for g in range(group):
            h = kvh * group + g
            o_ref[:, h * hd:(h + 1) * hd] = o[g * tq:(g + 1) * tq]


def gqa_attention(q, k, v, sink, cache_k, cache_v, layer):
    qw, kw = q.shape[1], k.shape[1]
    group = qw // kw
    scale = A_HD ** -0.5
    params = pltpu.CompilerParams(dimension_semantics=("parallel", "parallel"))
    out_shape = jax.ShapeDtypeStruct((N_TOK, qw), F32)
    ctx_spec = lambda w: pl.BlockSpec((CTX_LEN, w), lambda s, i, sk: (s, 0))
    o = pl.pallas_call(
        functools.partial(_gqa_kernel, hd=A_HD, group=group, scale=scale, windowed=False),
        grid_spec=pltpu.PrefetchScalarGridSpec(
            num_scalar_prefetch=1, grid=(N_CTX_SEQ, 1), in_specs=[ctx_spec(qw), ctx_spec(kw), ctx_spec(kw)],
            out_specs=ctx_spec(qw)),
        out_shape=out_shape, compiler_params=params, name="gqa_full",
    )(sink, q, k, v)
    tq = ATT_Q_BLOCK
    nb = LAT_LEN // tq
    base = N_CTX // tq
    blk = lambda w, f: pl.BlockSpec((tq, w), lambda b, i, sk: (base + nb * b + f(i), 0))
    same = lambda i: i
    prev = lambda i: jnp.maximum(i - 1, 0)
    nxt = lambda i: jnp.minimum(i + 1, nb - 1)
    cspec = pl.BlockSpec((1, 1) + cache_k.shape[2:], lambda b, i, sk: (b, layer, 0, 0, 0))
    return pl.pallas_call(
        functools.partial(_gqa_kernel, hd=A_HD, group=group, scale=scale, windowed=True),
        grid_spec=pltpu.PrefetchScalarGridSpec(
            num_scalar_prefetch=1, grid=(N_LAT_SEQ, nb),
            in_specs=[blk(qw, same), blk(kw, same), blk(kw, same), blk(kw, prev), blk(kw, nxt), blk(kw, prev),
                      blk(kw, nxt), cspec, cspec, pl.BlockSpec(memory_space=pl.ANY)],
            out_specs=blk(qw, same)),
        out_shape=out_shape, input_output_aliases={10: 0}, compiler_params=params, name="gqa_windowed",
    )(sink, q, k, v, k, k, v, v, cache_k, cache_v, o)


def _mla_kernel(q_ref, ckv_ref, kpe_ref, wukv_ref, *rest, n_heads, nope, rope, vd, scale, cached):
    if cached:
        cckv_ref, ckpe_ref, _, o_ref, kv_scr, kpe_scr = rest
    else:
        o_ref, kv_scr, kpe_scr = rest
    i = pl.program_id(1)
    n_cache = kv_scr.shape[0] - ckv_ref.shape[0]

    @pl.when(i == 0)
    def _():
        w = wukv_ref[...].astype(BF16)
        if cached:
            kv_scr[:n_cache] = jnp.dot(cckv_ref[0, 0].astype(BF16), w, preferred_element_type=F32).astype(BF16)
            kpe_scr[:n_cache] = ckpe_ref[0, 0].astype(BF16)
        kv_scr[n_cache:] = jnp.dot(ckv_ref[...].astype(BF16), w, preferred_element_type=F32).astype(BF16)
        kpe_scr[n_cache:] = kpe_ref[...].astype(BF16)

    kpe = kpe_scr[...]
    qd = nope + rope
    for h in range(n_heads):
        qn = (q_ref[:, h * qd:h * qd + nope] * scale).astype(BF16)
        qp = (q_ref[:, h * qd + nope:(h + 1) * qd] * scale).astype(BF16)
        k_n = kv_scr[:, h * (nope + vd):h * (nope + vd) + nope]
        v = kv_scr[:, h * (nope + vd) + nope:(h + 1) * (nope + vd)]
        s = _bdot(qn, k_n, _D2T) + _bdot(qp, kpe, _D2T)
        o_ref[:, h * vd:(h + 1) * vd] = _softmax_pv(s, v, None)


def mla_attention(q, ckv, kpe, w_ukv, cache_ckv, cache_kpe, layer):
    qw = q.shape[1]
    tq = ATT_Q_BLOCK
    kw = dict(n_heads=B_HEADS, nope=B_NOPE, rope=B_ROPE, vd=B_VD, scale=(B_NOPE + B_ROPE) ** -0.5)
    params = pltpu.CompilerParams(dimension_semantics=("parallel", "arbitrary"))
    out_shape = jax.ShapeDtypeStruct((N_TOK, B_HEADS * B_VD), F32)
    kvw = B_HEADS * (B_NOPE + B_VD)
    nbc = CTX_LEN // tq
    o = pl.pallas_call(
        functools.partial(_mla_kernel, cached=False, **kw),
        grid=(N_CTX_SEQ, nbc),
        in_specs=[pl.BlockSpec((tq, qw), lambda s, i: (s * nbc + i, 0)),
                  pl.BlockSpec((CTX_LEN, B_KV_LORA), lambda s, i: (s, 0)),
                  pl.BlockSpec((CTX_LEN, B_ROPE), lambda s, i: (s, 0)),
                  _full(w_ukv.shape)],
        out_specs=pl.BlockSpec((tq, B_HEADS * B_VD), lambda s, i: (s * nbc + i, 0)),
        out_shape=out_shape,
        scratch_shapes=[pltpu.VMEM((CTX_LEN, kvw), BF16), pltpu.VMEM((CTX_LEN, B_ROPE), BF16)],
        compiler_params=params, name="mla_context",
    )(q, ckv, kpe, w_ukv)
    nb = LAT_LEN // tq
    base = N_CTX // tq
    lat0 = N_CTX // LAT_LEN
    s_len = CACHE_LEN + LAT_LEN
    return pl.pallas_call(
        functools.partial(_mla_kernel, cached=True, **kw),
        grid=(N_LAT_SEQ, nb),
        in_specs=[pl.BlockSpec((tq, qw), lambda b, i: (base + nb * b + i, 0)),
                  pl.BlockSpec((LAT_LEN, B_KV_LORA), lambda b, i: (lat0 + b, 0)),
                  pl.BlockSpec((LAT_LEN, B_ROPE), lambda b, i: (lat0 + b, 0)),
                  _full(w_ukv.shape),
                  pl.BlockSpec((1, 1, CACHE_LEN, B_KV_LORA), lambda b, i: (b, layer, 0, 0)),
                  pl.BlockSpec((1, 1, CACHE_LEN, B_ROPE), lambda b, i: (b, layer, 0, 0)),
                  pl.BlockSpec(memory_space=pl.ANY)],
        out_specs=pl.BlockSpec((tq, B_HEADS * B_VD), lambda b, i: (base + nb * b + i, 0)),
        out_shape=out_shape, input_output_aliases={6: 0},
        scratch_shapes=[pltpu.VMEM((s_len, kvw), BF16), pltpu.VMEM((s_len, B_ROPE), BF16)],
        compiler_params=params, name="mla_latent",
    )(q, ckv, kpe, w_ukv, cache_ckv, cache_kpe, o)


CHUNK = 64
GLA_SUB = 16
N_CHAIN = 2 * 4


def _chunk_tables():
    fwd, bwd, sid, first, last = [], [], [], [], []
    seqs = [(s * (CTX_LEN // CHUNK), CTX_LEN // CHUNK) for s in range(N_CTX_SEQ)]
    seqs += [(N_CTX // CHUNK + b * (LAT_LEN // CHUNK), LAT_LEN // CHUNK) for b in range(N_LAT_SEQ)]
    for s, (base, nc) in enumerate(seqs):
        for c in range(nc):
            fwd.append(base + c)
            bwd.append(base + nc - 1 - c)
            sid.append(s)
            first.append(int(c == 0))
            last.append(int(c == nc - 1))
    return tuple(jnp.asarray(np.array(a, np.int32)) for a in (fwd, bwd, sid, first, last))


def _chains(ref_f, ref_b, width):
    return jnp.stack([ref[:, h * width:(h + 1) * width] for ref in (ref_f, ref_b) for h in range(4)], axis=0)


def _unchain(y, o_f, o_b):
    o_f[...] = jnp.concatenate([y[h] for h in range(4)], axis=-1)
    o_b[...] = jnp.concatenate([y[4 + h] for h in range(4)], axis=-1)


def _dir_masks(L):
    shape = (N_CHAIN, L, L)
    back = lax.broadcasted_iota(jnp.int32, shape, 0) >= 4
    row = lax.broadcasted_iota(jnp.int32, shape, 1)
    col = lax.broadcasted_iota(jnp.int32, shape, 2)
    ahead = jnp.where(back, col - row, row - col)
    return ahead >= 0, ahead > 0, row == col


def _chunk_end(ci):
    L = ci.shape[1]
    back = lax.broadcasted_iota(jnp.int32, (N_CHAIN, 1, 1), 0) >= 4
    return jnp.where(back, ci[:, 0:1], ci[:, L - 1:L])


def _rwkv_kernel(fwd_ref, bwd_ref, sid_ref, first_ref, last_ref,
                 rf, rb, vf, vb, kkf, kkb, lwf, lwb, kf, kb, af, ab, s0_ref, yf_ref, yb_ref, sfin_ref, s_scr, *, dot):
    step = pl.program_id(0)

    @pl.when(first_ref[step] == 1)
    def _():
        s_scr[...] = s0_ref[0]

    n = D_N
    r = _chains(rf, rb, n)
    v = _chains(vf, vb, n)
    kk = _chains(kkf, kkb, n)
    lw = _chains(lwf, lwb, n)
    k = _chains(kf, kb, n)
    a = _chains(af, ab, n)
    L = r.shape[1]
    S = s_scr[...]
    incl, strict, diag = _dir_masks(L)
    ci = _dot_exact_lhs(jnp.where(incl, 1.0, 0.0), lw, _NN)
    ce = ci - lw
    cl = _chunk_end(ci)
    e_neg = jnp.exp(-ci)
    b = a * kk
    alpha = kk * jnp.exp(ce)
    rho = r * jnp.exp(ci)
    beta = b * e_neg
    kappa = k * e_neg
    e_end = jnp.exp(cl - ci)
    ar = jnp.concatenate([alpha, rho], axis=1)
    bk = jnp.concatenate([beta, kappa], axis=1)
    w = dot(ar, bk, _NT)
    nmat = jnp.where(strict, w[:, :L, :L], 0.0)
    mmat = jnp.where(strict, w[:, :L, L:], 0.0)
    p1 = jnp.where(incl, w[:, L:, :L], 0.0)
    p2 = jnp.where(incl, w[:, L:, L:], 0.0)
    x = jnp.where(diag, 1.0, 0.0) - nmat
    p = dot(nmat, nmat, _NN)
    span = 2
    while True:
        x = x + dot(x, p, _NN)
        span *= 2
        if span >= L:
            break
        p = dot(p, p, _NN)
    us = dot(ar, S, _NT)
    rhs = us[:, :L] + dot(mmat, v, _NN)
    d = -dot(x, rhs, _NN)
    dv = jnp.concatenate([d, v], axis=1)
    pp = jnp.concatenate([p1, p2], axis=2)
    _unchain(us[:, L:] + dot(pp, dv, _NN), yf_ref, yb_ref)
    bk_end = jnp.concatenate([b * e_end, k * e_end], axis=1)
    s_new = S * jnp.exp(cl) + dot(dv, bk_end, _TN)
    s_scr[...] = s_new

    @pl.when(last_ref[step] == 1)
    def _():
        sfin_ref[0] = s_new


def _gla_kernel(fwd_ref, bwd_ref, sid_ref, first_ref, last_ref,
                qf, qb, kf, kb, vf, vb, laf, lab, s0_ref, of_ref, ob_ref, sfin_ref, s_scr, *, dot):
    step = pl.program_id(0)

    @pl.when(first_ref[step] == 1)
    def _():
        s_scr[...] = s0_ref[0]

    q4 = _chains(qf, qb, 128)
    k4 = _chains(kf, kb, 128)
    la4 = _chains(laf, lab, 128)
    v = _chains(vf, vb, C_DV)
    g, L, lanes = q4.shape
    dk = C_DK
    n_sub = L // GLA_SUB
    st = s_scr[...]
    incl, _, _ = _dir_masks(L)
    c = _dot_exact_lhs(jnp.where(incl, 1.0, 0.0), la4, _NN)
    shape = (g, L, lanes)
    back = lax.broadcasted_iota(jnp.int32, shape, 0) >= 4
    lane_blk = lax.broadcasted_iota(jnp.int32, shape, 2) // dk
    row_blk = lax.broadcasted_iota(jnp.int32, shape, 1) // GLA_SUB
    cref_f = jnp.zeros(shape, F32)
    cref_b = jnp.zeros(shape, F32)
    for j in range(1, n_sub):
        cref_f = jnp.where(lane_blk == j, c[:, j * GLA_SUB - 1:j * GLA_SUB], cref_f)
        cref_b = jnp.where(lane_blk == j - 1, c[:, j * GLA_SUB:j * GLA_SUB + 1], cref_b)
    cref = jnp.where(back, cref_b, cref_f)
    q_on = row_blk == lane_blk
    k_on = jnp.where(back, row_blk - lane_blk, lane_blk - row_blk) >= 0
    qh = jnp.where(q_on, q4 * jnp.exp(jnp.where(q_on, c - cref, 0.0)), 0.0)
    kh = jnp.where(k_on, k4 * jnp.exp(jnp.where(k_on, cref - c, 0.0)), 0.0)
    att = jnp.where(incl, dot(qh, kh, _NT), 0.0)
    cl = _chunk_end(c)
    qe = (q4 * jnp.exp(c))[:, :, :dk]
    ke = (k4 * jnp.exp(cl - c))[:, :, :dk]
    _unchain(dot(qe, st, _NT) + dot(att, v, _NN), of_ref, ob_ref)
    s_new = st * jnp.exp(cl[:, :, :dk]) + dot(v, ke, _TN)
    s_scr[...] = s_new

    @pl.when(last_ref[step] == 1)
    def _():
        sfin_ref[0] = s_new


def _recurrence_call(kernel_fn, name, tables, pairs, singles_f, singles_b, s0, out_width):
    fwd_map = lambda s, fwd, bwd, sid, first, last: (fwd[s], 0)
    bwd_map = lambda s, fwd, bwd, sid, first, last: (bwd[s], 0)
    st_map = lambda s, fwd, bwd, sid, first, last: (sid[s], 0, 0, 0)
    args, in_specs = [], []
    for arr in pairs:
        args += [arr, arr]
        in_specs += [pl.BlockSpec((CHUNK, arr.shape[1]), fwd_map), pl.BlockSpec((CHUNK, arr.shape[1]), bwd_map)]
    for af, ab in zip(singles_f, singles_b):
        args += [af, ab]
        in_specs += [pl.BlockSpec((CHUNK, af.shape[1]), fwd_map), pl.BlockSpec((CHUNK, ab.shape[1]), bwd_map)]
    st_spec = pl.BlockSpec((1,) + s0.shape[1:], st_map)
    n_steps = tables[0].shape[0]
    return pl.pallas_call(
        kernel_fn,
        grid_spec=pltpu.PrefetchScalarGridSpec(
            num_scalar_prefetch=5, grid=(n_steps,), in_specs=in_specs + [st_spec],
            out_specs=[pl.BlockSpec((CHUNK, out_width), fwd_map), pl.BlockSpec((CHUNK, out_width), bwd_map), st_spec],
            scratch_shapes=[pltpu.VMEM(s0.shape[1:], F32)]),
        out_shape=[jax.ShapeDtypeStruct((N_TOK, out_width), F32), jax.ShapeDtypeStruct((N_TOK, out_width), F32),
                   jax.ShapeDtypeStruct(s0.shape, F32)],
        compiler_params=pltpu.CompilerParams(dimension_semantics=("arbitrary",)),
        name=name,
    )(*tables, *args, s0)


def _layer_norm(x, g, b):
    mu = jnp.mean(x, axis=-1, keepdims=True)
    xc = x - mu
    var = jnp.mean(xc * xc, axis=-1, keepdims=True)
    return xc * lax.rsqrt(var + LN_EPS) * g + b


def _merge_kernel(x_ref, mod_ref, oa_ref, ob_ref, cof_ref, cob_ref, cgate_ref, yf_ref, yb_ref, bonus_ref, dgate_ref,
                  wg_ref, wbr_ref, wout_ref, cnorm_ref, dlng_ref, dlnb_ref, lng_ref, lnb_ref, wr_ref, br_ref, bd_ref,
                  x1_o, h2_o, topi_o, topw_o):
    x = x_ref[...]
    m = mod_ref[0]
    sh1, sc1, g1, sh2, sc2 = m[0:1], m[1:2], m[2:3], m[3:4], m[4:5]
    bd = bd_ref[...]
    inv_n = 1.0 / D_N
    co = cof_ref[...] + cob_ref[...]
    o_c = co * lax.rsqrt(_dot_exact_rhs(co * co, bd) * inv_n + RMS_EPS) * cnorm_ref[...] * cgate_ref[...]
    y = yf_ref[...] + yb_ref[...]
    yc = y - _dot_exact_rhs(y, bd) * inv_n
    var = _dot_exact_rhs(yc * yc, bd) * inv_n
    o_d = (yc * lax.rsqrt(var + D_GN_EPS) * dlng_ref[...] + dlnb_ref[...] + bonus_ref[...]) * dgate_ref[...]
    branches = (oa_ref[...], ob_ref[...], o_c, o_d)
    h = (x * (1.0 + sc1) + sh1).astype(BF16)
    merged = None
    for n in range(N_BRANCH):
        gate = _sigmoid(jnp.dot(h, wg_ref[:, n * D_MODEL:(n + 1) * D_MODEL], preferred_element_type=F32))
        term = gate * jnp.dot(branches[n].astype(BF16), wbr_ref[n], preferred_element_type=F32)
        merged = term if merged is None else merged + term
    mix = jnp.dot(merged.astype(BF16), wout_ref[...], preferred_element_type=F32)
    x1 = _layer_norm(ALPHA * x + g1 * mix, lng_ref[...], lnb_ref[...])
    x1_o[...] = x1
    h2 = x1 * (1.0 + sc2) + sh2
    h2_o[...] = h2.astype(BF16)
    logits = _dot6(h2, wr_ref[...]) + br_ref[...]
    tm, n_e = logits.shape
    lane_e = lax.broadcasted_iota(jnp.int32, (tm, n_e), 1)
    lane_o = lax.broadcasted_iota(jnp.int32, (tm, topi_o.shape[1]), 1)
    top_i = jnp.zeros((tm, topi_o.shape[1]), jnp.int32)
    top_v = jnp.zeros((tm, topw_o.shape[1]), F32)
    vals = []
    for kth in range(TOP_K):
        mx = jnp.max(logits, axis=-1, keepdims=True)
        idx = jnp.min(jnp.where(logits == mx, lane_e, n_e), axis=-1, keepdims=True)
        vals.append(mx)
        top_i = jnp.where(lane_o == kth, idx, top_i)
        logits = jnp.where(lane_e == idx, -jnp.inf, logits)
    es = [jnp.exp(vk - vals[0]) for vk in vals]
    den = es[0] + es[1] + es[2] + es[3]
    for kth in range(TOP_K):
        top_v = jnp.where(lane_o == kth, es[kth] / den, top_v)
    topi_o[...] = top_i
    topw_o[...] = top_v


def merge_and_route(x, mod_l, o_a, o_b, co_f, co_b, cgate, y_f, y_b, bonus, dgate, w_g, w_br, w_out, lp):
    tm = ROW_TILE
    hn = D_HEADS * D_N
    row = lambda w: pl.BlockSpec((tm, w), lambda t: (t, 0))
    small = [jnp.tile(lp['c_norm'], C_HEADS).reshape(1, hn), lp['d_ln_g'].reshape(1, hn), lp['d_ln_b'].reshape(1, hn),
             lp['ln_g'][0].reshape(1, -1), lp['ln_b'][0].reshape(1, -1), lp['w_router'],
             lp['b_router'].reshape(1, -1), _head_block_diag()]
    return pl.pallas_call(
        _merge_kernel,
        grid=(N_TILES,),
        in_specs=([row(D_MODEL), pl.BlockSpec((1, 6, D_MODEL), lambda t: (_mod_row(t), 0, 0))]
                  + [row(hn)] * 9 + [_full(w_g.shape), _full(w_br.shape), _full(w_out.shape)]
                  + [_full(a.shape) for a in small]),
        out_specs=[row(D_MODEL), row(D_MODEL), row(128), row(128)],
        out_shape=[jax.ShapeDtypeStruct((N_TOK, D_MODEL), F32), jax.ShapeDtypeStruct((N_TOK, D_MODEL), BF16),
                   jax.ShapeDtypeStruct((N_TOK, 128), jnp.int32), jax.ShapeDtypeStruct((N_TOK, 128), F32)],
        compiler_params=pltpu.CompilerParams(dimension_semantics=("parallel",), vmem_limit_bytes=VMEM_LIMIT),
        name="merge_and_route",
    )(x, mod_l, o_a, o_b, co_f, co_b, cgate, y_f, y_b, bonus, dgate, w_g, w_br, w_out, *small)


MOE_ROW_TILE = 256


def _moe_kernel(te_ref, tv_ref, x_ref, w1_ref, b1_ref, w2_ref, b2_ref, perm_ref, y_ref, w1s, w2s, hs):
    t = pl.program_id(0)
    e = te_ref[t]
    prev = te_ref[jnp.maximum(t - 1, 0)]
    new_expert = jnp.logical_or(t == 0, e != prev)
    valid = tv_ref[t] != 0
    d_model, two_f = w1s.shape
    n_blk = two_f // MXU_WIDTH
    half = MXU_WIDTH // 2

    @pl.when(new_expert)
    def _():
        for blk in range(n_blk):
            sl = slice(blk * MXU_WIDTH, (blk + 1) * MXU_WIDTH)
            wb = w1_ref[0, :, sl].astype(BF16)
            w1s[:, sl] = jnp.dot(wb, perm_ref[...], preferred_element_type=F32).astype(BF16)
        w2s[...] = w2_ref[0].astype(BF16)

    @pl.when(valid)
    def _():
        x = x_ref[...]
        for blk in range(n_blk):
            sl = slice(blk * MXU_WIDTH, (blk + 1) * MXU_WIDTH)
            u = jnp.dot(x, w1s[:, sl], preferred_element_type=F32) + b1_ref[0, :, sl]
            glu = jnp.minimum(u[:, :half], SWIGLU_LIMIT)
            lin = jnp.clip(u[:, half:], -SWIGLU_LIMIT, SWIGLU_LIMIT)
            hs[:, blk * half:(blk + 1) * half] = (glu * _sigmoid(SWIGLU_ALPHA * glu) * (lin + 1.0)).astype(BF16)
        y = jnp.dot(hs[...], w2s[...], preferred_element_type=F32) + b2_ref[0]
        y_ref[...] = y.astype(y_ref.dtype)

    @pl.when(jnp.logical_not(valid))
    def _():
        y_ref[...] = jnp.zeros_like(y_ref)


def _deinterleave_perm():
    half = MXU_WIDTH // 2
    src = np.arange(MXU_WIDTH)
    dst = np.where(src % 2 == 0, src // 2, half + src // 2)
    p = np.zeros((MXU_WIDTH, MXU_WIDTH), np.float32)
    p[src, dst] = 1.0
    return jnp.asarray(p, BF16)


def _moe_dispatch(top_i):
    n, k = top_i.shape
    tm = MOE_ROW_TILE
    p_rows = n * k + N_EXPERTS * tm
    sel = jnp.sum((top_i[:, :, None] == jnp.arange(N_EXPERTS, dtype=jnp.int32)).astype(jnp.int32), axis=1)
    before = jnp.cumsum(sel, axis=0) - sel
    rank = jnp.take_along_axis(before, top_i, axis=1)
    counts = jnp.sum(sel, axis=0)
    padded = ((counts + tm - 1) // tm) * tm
    ends = jnp.cumsum(padded)
    pos = (ends - padded)[top_i] + rank
    n_tiles = p_rows // tm
    tile_start = jnp.arange(n_tiles, dtype=jnp.int32) * tm
    tile_valid = (tile_start < ends[-1]).astype(jnp.int32)
    last_tile = ends[-1] // tm - 1
    tile_expert = jnp.sum(ends[None, :] <= jnp.minimum(tile_start, last_tile * tm)[:, None], axis=1).astype(jnp.int32)
    return pos, tile_expert, tile_valid, p_rows


def moe_experts(h2, top_i, w1, b1, w2, b2):
    n, d = h2.shape
    e, _, two_f = w1.shape
    f = two_f // 2
    tm = MOE_ROW_TILE
    pos, tile_expert, tile_valid, p_rows = _moe_dispatch(top_i)
    xs = jnp.zeros((p_rows, d), BF16).at[pos.T.reshape(-1)].set(jnp.tile(h2, (TOP_K, 1)), unique_indices=True)
    b1p = b1.reshape(e, two_f // MXU_WIDTH, MXU_WIDTH // 2, 2).transpose(0, 1, 3, 2).reshape(e, 1, two_f)
    grid_spec = pltpu.PrefetchScalarGridSpec(
        num_scalar_prefetch=2,
        grid=(p_rows // tm,),
        in_specs=[
            pl.BlockSpec((tm, d), lambda t, te, tv: (t, 0)),
            pl.BlockSpec((1, d, two_f), lambda t, te, tv: (te[t], 0, 0)),
            pl.BlockSpec((1, 1, two_f), lambda t, te, tv: (te[t], 0, 0)),
            pl.BlockSpec((1, f, d), lambda t, te, tv: (te[t], 0, 0)),
            pl.BlockSpec((1, 1, d), lambda t, te, tv: (te[t], 0, 0)),
            pl.BlockSpec((MXU_WIDTH, MXU_WIDTH), lambda t, te, tv: (0, 0)),
        ],
        out_specs=pl.BlockSpec((tm, d), lambda t, te, tv: (t, 0)),
        scratch_shapes=[pltpu.VMEM((d, two_f), BF16), pltpu.VMEM((f, d), BF16), pltpu.VMEM((tm, f), BF16)],
    )
    ys = pl.pallas_call(
        _moe_kernel,
        grid_spec=grid_spec,
        out_shape=jax.ShapeDtypeStruct((p_rows, d), BF16),
        compiler_params=pltpu.CompilerParams(dimension_semantics=("arbitrary",),
                                             vmem_limit_bytes=48 * 1024 * 1024),
        name="moe_experts",
    )(tile_expert, tile_valid, xs, w1, b1p, w2, b2.reshape(e, 1, d), _deinterleave_perm())
    return ys[pos.T]


def _final_kernel(x1_ref, mod_ref, ys_ref, topw_ref, lng_ref, lnb_ref, o_ref):
    g2 = mod_ref[0, 5:6]
    moe = None
    for kth in range(TOP_K):
        term = ys_ref[kth].astype(F32) * topw_ref[:, kth:kth + 1]
        moe = term if moe is None else moe + term
    o_ref[...] = _layer_norm(ALPHA * x1_ref[...] + g2 * moe, lng_ref[...], lnb_ref[...])


def combine_and_norm(x1, mod_l, ys, top_w, ln_g, ln_b):
    tm = ROW_TILE
    return pl.pallas_call(
        _final_kernel,
        grid=(N_TILES,),
        in_specs=[pl.BlockSpec((tm, D_MODEL), lambda t: (t, 0)),
                  pl.BlockSpec((1, 6, D_MODEL), lambda t: (_mod_row(t), 0, 0)),
                  pl.BlockSpec((TOP_K, tm, D_MODEL), lambda t: (0, t, 0)),
                  pl.BlockSpec((tm, 128), lambda t: (t, 0)),
                  _full((1, D_MODEL)), _full((1, D_MODEL))],
        out_specs=pl.BlockSpec((tm, D_MODEL), lambda t: (t, 0)),
        out_shape=jax.ShapeDtypeStruct((N_TOK, D_MODEL), F32),
        compiler_params=pltpu.CompilerParams(dimension_semantics=("parallel",)),
        name="combine_and_norm",
    )(x1, mod_l, ys, top_w, ln_g.reshape(1, -1), ln_b.reshape(1, -1))


def kernel(x_prompt, x_sample, cache_a_k, cache_a_v, cache_b_ckv, cache_b_kpe, state_c, state_d, c,
           c_ctx, w_mod, b_mod, w_in, a_sink, b_q_norm, b_w_uq, b_kv_norm, b_w_ukv, c_w_gate, c_b_gate,
           c_norm, d_mu, d_w0, d_w2, d_a0, d_a2, d_g2, d_k_k, d_k_a, d_r_k, d_ln_g, d_ln_b, w_br, w_out,
           ln_g, ln_b, w_router, b_router, w_mlp1, b_mlp1, w_mlp2, b_mlp2):
    params = dict(b_q_norm=b_q_norm, b_w_uq=b_w_uq, b_kv_norm=b_kv_norm, b_w_ukv=b_w_ukv, c_w_gate=c_w_gate,
                  c_b_gate=c_b_gate, c_norm=c_norm, d_mu=d_mu, d_w0=d_w0, d_w2=d_w2, d_a0=d_a0, d_a2=d_a2, d_g2=d_g2,
                  d_k_k=d_k_k, d_k_a=d_k_a, d_r_k=d_r_k, d_ln_g=d_ln_g, d_ln_b=d_ln_b, ln_g=ln_g, ln_b=ln_b,
                  w_router=w_router, b_router=b_router)
    assert x_prompt.shape == (N_CTX_SEQ, CTX_LEN, D_MODEL) and x_sample.shape == (N_LAT_SEQ, LAT_LEN, D_MODEL)
    x = jnp.concatenate([x_prompt.reshape(N_CTX, D_MODEL), x_sample.reshape(-1, D_MODEL)], axis=0)
    cond8 = jnp.concatenate([c_ctx[None], c, jnp.zeros((8 - 1 - N_LAT_SEQ, D_MODEL), F32)], axis=0)
    mod = modulation_table(cond8, w_mod, b_mod)[:, :1 + N_LAT_SEQ].reshape(DEPTH, 1 + N_LAT_SEQ, 6, D_MODEL)
    tables = _rope_tables()
    chunk_tables = _chunk_tables()
    zeros_c = jnp.zeros((N_CTX_SEQ, 2, C_HEADS, C_DK, C_DV), F32)
    zeros_d = jnp.zeros((N_CTX_SEQ, 2, D_HEADS, D_N, D_N), F32)
    new = {name: [] for name in ("a_k", "a_v", "b_ckv", "b_kpe", "c", "d")}
    for l in range(DEPTH):
        lp = {name: val[l] for name, val in params.items()}
        w_small = jnp.concatenate([w_in[l][:, _ORIG[nm][0]:_ORIG[nm][1]] for nm in _ORDER], axis=1).astype(BF16)
        w_g = w_in[l][:, G_START:].astype(BF16)
        w_br_l = w_br[l].astype(BF16)
        w_out_l = w_out[l].astype(BF16)

        (aq, ak, av, bq, bckv, bkpe, cq4, ck4, cla_f, cla_b, cv, cgate,
         r, v, kk, lw_f, lw_b, k_f, k_b, a_f, a_b, bonus, dgate) = mixer_prelude(x, mod[l], w_small, tables, lp)

        o_a = gqa_attention(aq, ak, av, a_sink[l], cache_a_k, cache_a_v, l)
        o_b = mla_attention(bq, bckv, bkpe, b_w_ukv[l], cache_b_ckv, cache_b_kpe, l)

        c_s0 = jnp.concatenate([zeros_c, state_c[:, l]], axis=0)
        c_s0 = jnp.swapaxes(c_s0, 3, 4).reshape(N_SEQ, N_CHAIN, C_DV, C_DK)
        co_f, co_b, c_fin = _recurrence_call(functools.partial(_gla_kernel, dot=_dot3), "gla_chunked", chunk_tables,
                                             [cq4, ck4, cv], [cla_f], [cla_b], c_s0, C_HEADS * C_DV)
        d_s0 = jnp.concatenate([zeros_d, state_d[:, l]], axis=0).reshape(N_SEQ, N_CHAIN, D_N, D_N)
        y_f, y_b, d_fin = _recurrence_call(functools.partial(_rwkv_kernel, dot=_dot3), "rwkv7_chunked", chunk_tables,
                                           [r, v, kk], [lw_f, k_f, a_f], [lw_b, k_b, a_b], d_s0, D_HEADS * D_N)

        x1, h2, top_i, top_w = merge_and_route(x, mod[l], o_a, o_b, co_f, co_b, cgate, y_f, y_b, bonus, dgate,
                                               w_g, w_br_l, w_out_l, lp)
        ys = moe_experts(h2, top_i[:, :TOP_K], w_mlp1[l], b_mlp1[l], w_mlp2[l], b_mlp2[l])
        x = combine_and_norm(x1, mod[l], ys, top_w, ln_g[l, 1], ln_b[l, 1])

        new["a_k"].append(ak[:N_CTX].reshape(N_CTX_SEQ, CTX_LEN, A_KV_HEADS, A_HD).transpose(0, 2, 1, 3))
        new["a_v"].append(av[:N_CTX].reshape(N_CTX_SEQ, CTX_LEN, A_KV_HEADS, A_HD).transpose(0, 2, 1, 3))
        new["b_ckv"].append(bckv[:N_CTX].reshape(N_CTX_SEQ, CTX_LEN, B_KV_LORA))
        new["b_kpe"].append(bkpe[:N_CTX].reshape(N_CTX_SEQ, CTX_LEN, B_ROPE))
        new["c"].append(jnp.swapaxes(c_fin[:N_CTX_SEQ].reshape(N_CTX_SEQ, 2, C_HEADS, C_DV, C_DK), 3, 4))
        new["d"].append(d_fin[:N_CTX_SEQ].reshape(N_CTX_SEQ, 2, D_HEADS, D_N, D_N))
    y_prompt = x[:N_CTX].reshape(x_prompt.shape)
    y_sample = x[N_CTX:].reshape(x_sample.shape)
    return (y_prompt, y_sample, *(jnp.stack(new[name], axis=1) for name in ("a_k", "a_v", "b_ckv", "b_kpe", "c", "d")))
```

```python
import functools

import jax
import jax.numpy as jnp
import numpy as np
from jax import lax
from jax.experimental import pallas as pl
from jax.experimental.pallas import tpu as pltpu

F32 = jnp.float32
BF16 = jnp.bfloat16

MXU_WIDTH = 256
VMEM_LIMIT = 56 * 1024 * 1024

D_MODEL = 1024
DEPTH = 2
GRID_W = 64
ROPE_BASE = 10000.0
A_HEADS, A_KV_HEADS, A_HD = 4, 2, 64
B_HEADS, B_NOPE, B_ROPE, B_VD, B_Q_LORA, B_KV_LORA = 4, 64, 32, 64, 192, 128
C_HEADS, C_DK, C_DV, C_GATE_RANK, C_GATE_TEMP = 4, 32, 64, 16, 16.0
D_HEADS, D_N, D_DECAY_RANK, D_AAA_RANK, D_GATE_RANK, D_GN_EPS = 4, 64, 64, 64, 128, 64e-5
BRANCH_W = 256
N_BRANCH = 4
N_EXPERTS = 32
TOP_K = 4
SWIGLU_LIMIT = 7.0
SWIGLU_ALPHA = 1.702
ALPHA = (2 * DEPTH) ** 0.25
LN_EPS = 1e-5
RMS_EPS = 1e-6

N_CTX_SEQ, CTX_LEN = 16, 256
N_LAT_SEQ, LAT_LEN = 2, 2048
N_CTX = N_CTX_SEQ * CTX_LEN
N_TOK = N_CTX + N_LAT_SEQ * LAT_LEN
ROW_TILE = 256
N_TILES = N_TOK // ROW_TILE
CTX_TILES = N_CTX // ROW_TILE
LAT_TILES_PER_SEQ = LAT_LEN // ROW_TILE
N_SEQ = N_CTX_SEQ + N_LAT_SEQ

_ORIG = dict(aq=(0, 256), ak=(256, 384), av=(384, 512), bcq=(512, 704), bckv=(704, 832), bkpe=(832, 864),
             cq=(864, 992), ck=(992, 1120), cv=(1120, 1376), cog=(1376, 1632), caf=(1632, 1648), cab=(1648, 1664),
             zd=(1664, 2816))
_ORDER = ("aq", "ak", "av", "cq", "ck", "cv", "cog", "zd", "bcq", "caf", "cab", "bkpe", "bckv")
COL = {}
_off = 0
for _name in _ORDER:
    _w = _ORIG[_name][1] - _ORIG[_name][0]
    COL[_name] = (_off, _off + _w)
    _off += _w
SMALL_COLS = _off
G_START = 2816


def _cs(name):
    return slice(*COL[name])


def _split3(x):
    hi = x.astype(BF16)
    r1 = x - hi.astype(F32)
    mid = r1.astype(BF16)
    lo = (r1 - mid.astype(F32)).astype(BF16)
    return hi, mid, lo


def _split2(x):
    hi = x.astype(BF16)
    lo = (x - hi.astype(F32)).astype(BF16)
    return hi, lo


def _bdot(a, b, dims):
    return lax.dot_general(a, b, dims, preferred_element_type=F32)


_D2 = (((1,), (0,)), ((), ()))
_D2T = (((1,), (1,)), ((), ()))
_NN = (((2,), (1,)), ((0,), (0,)))
_NT = (((2,), (2,)), ((0,), (0,)))
_TN = (((1,), (1,)), ((0,), (0,)))


def _dot1(a, b, dims=_D2):
    return _bdot(a.astype(BF16), b.astype(BF16), dims)


def _dot3(a, b, dims=_D2):
    ah, al = _split2(a)
    bh, bl = _split2(b)
    return _bdot(ah, bh, dims) + (_bdot(ah, bl, dims) + _bdot(al, bh, dims))


def _dot_exact_lhs(a01, b, dims=_D2):
    a = a01.astype(BF16)
    h, m, l = _split3(b)
    return _bdot(a, h, dims) + (_bdot(a, m, dims) + _bdot(a, l, dims))


def _dot_exact_rhs(a, b01, dims=_D2):
    b = b01.astype(BF16)
    h, m, l = _split3(a)
    return _bdot(h, b, dims) + (_bdot(m, b, dims) + _bdot(l, b, dims))


def _dot6(a, b, dims=_D2):
    ah, am, al = _split3(a)
    bh, bm, bl = _split3(b)
    return (_bdot(ah, bh, dims) + (_bdot(ah, bm, dims) + _bdot(am, bh, dims))
            + (_bdot(am, bm, dims) + (_bdot(ah, bl, dims) + _bdot(al, bh, dims))))


def _sigmoid(x):
    return 1.0 / (1.0 + jnp.exp(-x))


def _softplus(x):
    return jnp.maximum(x, 0.0) + jnp.log(1.0 + jnp.exp(-jnp.abs(x)))


def _mod_row(t):
    return jnp.where(t < CTX_TILES, 0, 1 + (t - CTX_TILES) // LAT_TILES_PER_SEQ)


def _full(shape):
    nd = len(shape)
    return pl.BlockSpec(shape, lambda *_: (0,) * nd)


MOD_COL_TILE = 1536


def _mod_kernel(c_ref, w_ref, b_ref, o_ref):
    c = c_ref[...]
    o_ref[0] = _dot3(c * _sigmoid(c), w_ref[0]) + b_ref[0]


def modulation_table(cond8, w_mod, b_mod):
    depth, d, six_d = w_mod.shape
    return pl.pallas_call(
        _mod_kernel,
        grid=(depth, six_d // MOD_COL_TILE),
        in_specs=[pl.BlockSpec((8, d), lambda l, j: (0, 0)),
                  pl.BlockSpec((1, d, MOD_COL_TILE), lambda l, j: (l, 0, j)),
                  pl.BlockSpec((1, 1, MOD_COL_TILE), lambda l, j: (l, 0, j))],
        out_specs=pl.BlockSpec((1, 8, MOD_COL_TILE), lambda l, j: (l, 0, j)),
        out_shape=jax.ShapeDtypeStruct((depth, 8, six_d), F32),
        compiler_params=pltpu.CompilerParams(dimension_semantics=("parallel", "parallel")),
        name="modulation",
    )(cond8, w_mod, b_mod.reshape(depth, 1, six_d))


def _rot_pairs(x, half, lane_mod_base=0):
    w = x.shape[-1]
    lane = lax.broadcasted_iota(jnp.int32, (1, w), 1) - lane_mod_base
    first = (lane % (2 * half)) < half
    return jnp.where(first, -pltpu.roll(x, w - half, axis=1), pltpu.roll(x, half, axis=1))


def _pre_kernel(x_ref, xp_ref, xn_ref, mod_ref, w_ref, ca_ref, sa_ref, cb_ref, sb_ref, ck_ref, sk_ref,
                qnorm_ref, kvnorm_ref, wuq_ref, cwg_ref, cbg_ref, rep_ref, mu_ref, dw0_ref, dw2_ref, da0_ref,
                da2_ref, dg2_ref, dkk_ref, dka_ref, drk_ref, bd_ref,
                aq_o, ak_o, av_o, bq_o, bckv_o, bkpe_o, cq4_o, ck4_o, claf_o, clab_o, cv_o, cgate_o,
                r_o, v_o, kk_o, lwf_o, lwb_o, kf_o, kb_o, af_o, ab_o, bonus_o, dgate_o):
    t = pl.program_id(0)
    tm = x_ref.shape[0]
    sh1 = mod_ref[0, 0:1, :]
    sc1 = mod_ref[0, 1:2, :]

    def modulate(xv):
        return (xv * (1.0 + sc1) + sh1).astype(BF16)

    z = jnp.dot(modulate(x_ref[...]), w_ref[...], preferred_element_type=F32)

    aq = z[:, _cs("aq")]
    ak = z[:, _cs("ak")]
    aq_o[...] = aq * ca_ref[...] + _rot_pairs(aq, A_HD // 4) * sa_ref[...]
    ak_o[...] = ak * ca_ref[:, :ak.shape[1]] + _rot_pairs(ak, A_HD // 4) * sa_ref[:, :ak.shape[1]]
    av_o[...] = z[:, _cs("av")]

    bcq = z[:, _cs("bcq")]
    qn = bcq * lax.rsqrt(jnp.mean(bcq * bcq, axis=-1, keepdims=True) + RMS_EPS) * qnorm_ref[...]
    bq = _dot1(qn, wuq_ref[...])
    bq_o[...] = bq * cb_ref[...] + _rot_pairs(bq, B_ROPE // 4, lane_mod_base=B_NOPE) * sb_ref[...]
    bckv = z[:, _cs("bckv")]
    bckv_o[...] = bckv * lax.rsqrt(jnp.mean(bckv * bckv, axis=-1, keepdims=True) + RMS_EPS) * kvnorm_ref[...]
    kpe_lo = COL["bkpe"][0] // 128 * 128
    kblk = z[:, kpe_lo:kpe_lo + 128]
    kblk = kblk * ck_ref[...] + _rot_pairs(kblk, B_ROPE // 4) * sk_ref[...]
    bkpe_o[...] = kblk[:, COL["bkpe"][0] - kpe_lo:COL["bkpe"][1] - kpe_lo]

    rep = rep_ref[...]
    cq4_o[...] = _dot_exact_rhs(z[:, _cs("cq")] * (C_DK ** -0.5), rep)
    ck4_o[...] = _dot_exact_rhs(z[:, _cs("ck")], rep)
    cv_o[...] = z[:, _cs("cv")]
    cog = z[:, _cs("cog")]
    cgate_o[...] = cog * _sigmoid(cog)
    for direction, (name, out) in enumerate((("caf", claf_o), ("cab", clab_o))):
        pre = _dot3(z[:, _cs(name)], cwg_ref[direction]) + cbg_ref[direction]
        out[...] = _dot_exact_rhs(-_softplus(-pre) * (1.0 / C_GATE_TEMP), rep)

    zd_cols = _cs("zd")
    zd = z[:, zd_cols]
    wd = w_ref[:, zd_cols]
    j = (t - CTX_TILES) % LAT_TILES_PER_SEQ
    latent = t >= CTX_TILES
    has_prev = jnp.logical_and(latent, j != 0)
    has_next = jnp.logical_and(latent, j != LAT_TILES_PER_SEQ - 1)
    zp = jnp.dot(modulate(xp_ref[...]), wd, preferred_element_type=F32)
    zn = jnp.dot(modulate(xn_ref[...]), wd, preferred_element_type=F32)
    prev_row = jnp.where(has_prev, zp[7:8], 0.0)
    next_row = jnp.where(has_next, zn[0:1], 0.0)
    row = lax.broadcasted_iota(jnp.int32, (tm, 1), 0)
    up = jnp.where(row == 0, prev_row, pltpu.roll(zd, 1, axis=0))
    dn = jnp.where(row == tm - 1, next_row, pltpu.roll(zd, tm - 1, axis=0))
    zd = zd + (0.5 * (up + dn) - zd) * mu_ref[...]

    hn = D_HEADS * D_N
    d_r, d_k, d_v = zd[:, :hn], zd[:, hn:2 * hn], zd[:, 2 * hn:3 * hn]
    o = 3 * hn
    d_w = (zd[:, o:o + D_DECAY_RANK], zd[:, o + D_DECAY_RANK:o + 2 * D_DECAY_RANK])
    o += 2 * D_DECAY_RANK
    d_a = (zd[:, o:o + D_AAA_RANK], zd[:, o + D_AAA_RANK:o + 2 * D_AAA_RANK])
    o += 2 * D_AAA_RANK
    d_g = zd[:, o:o + D_GATE_RANK]
    bd = bd_ref[...]
    kk = d_k * dkk_ref[...]
    kk = kk / jnp.maximum(jnp.sqrt(_dot_exact_rhs(kk * kk, bd)), 1e-12)
    r_o[...] = d_r
    v_o[...] = d_v
    kk_o[...] = kk
    k_sum = None
    for direction, (lw_o, k_o, a_o) in enumerate(((lwf_o, kf_o, af_o), (lwb_o, kb_o, ab_o))):
        w_log = -_softplus(-(dw0_ref[direction] + _dot3(jnp.tanh(d_w[direction]), dw2_ref[direction]))) - 0.5
        lw_o[...] = -jnp.exp(w_log)
        a = _sigmoid(da0_ref[direction] + _dot3(d_a[direction], da2_ref[direction]))
        k_dir = d_k * (1.0 + (a - 1.0) * dka_ref[...])
        k_o[...] = k_dir
        a_o[...] = a
        k_sum = k_dir if k_sum is None else k_sum + k_dir
    bonus_o[...] = d_v * _dot_exact_rhs(d_r * drk_ref[...] * k_sum, bd)
    dgate_o[...] = _dot3(_sigmoid(d_g), dg2_ref[...])


def _rope_tables():
    pos = np.arange(LAT_LEN)
    rowp, colp = (pos // GRID_W).astype(np.float32), (pos % GRID_W).astype(np.float32)

    def head_tables(rot_dim):
        quarter = rot_dim // 4
        inv = (ROPE_BASE ** (-jnp.arange(quarter, dtype=F32) / quarter))
        ar = jnp.asarray(rowp)[:, None] * inv
        ac = jnp.asarray(colp)[:, None] * inv
        cos = jnp.concatenate([jnp.cos(ar), jnp.cos(ar), jnp.cos(ac), jnp.cos(ac)], axis=-1)
        sin = jnp.concatenate([jnp.sin(ar), jnp.sin(ar), jnp.sin(ac), jnp.sin(ac)], axis=-1)
        return cos, sin

    def with_identity(c, s):
        w = c.shape[1]
        return (jnp.concatenate([jnp.ones((ROW_TILE, w), F32), c], axis=0),
                jnp.concatenate([jnp.zeros((ROW_TILE, w), F32), s], axis=0))

    ca, sa = head_tables(A_HD)
    ca, sa = with_identity(jnp.tile(ca, (1, A_HEADS)), jnp.tile(sa, (1, A_HEADS)))
    cbh, sbh = head_tables(B_ROPE)
    ones, zeros = jnp.ones((LAT_LEN, B_NOPE), F32), jnp.zeros((LAT_LEN, B_NOPE), F32)
    cb, sb = with_identity(jnp.tile(jnp.concatenate([ones, cbh], axis=1), (1, B_HEADS)),
                           jnp.tile(jnp.concatenate([zeros, sbh], axis=1), (1, B_HEADS)))
    pad = 128 - B_ROPE
    ck, sk = with_identity(jnp.concatenate([jnp.ones((LAT_LEN, pad), F32), cbh], axis=1),
                           jnp.concatenate([jnp.zeros((LAT_LEN, pad), F32), sbh], axis=1))
    return ca, sa, cb, sb, ck, sk


def _lane_repeat_matrix():
    m = np.zeros((C_HEADS * C_DK, C_HEADS * 128), np.float32)
    for h in range(C_HEADS):
        for g in range(128 // C_DK):
            for d in range(C_DK):
                m[h * C_DK + d, h * 128 + g * C_DK + d] = 1.0
    return jnp.asarray(m, BF16)


def _head_block_diag():
    m = np.kron(np.eye(D_HEADS, dtype=np.float32), np.ones((D_N, D_N), np.float32))
    return jnp.asarray(m, BF16)


PRE_OUT_WIDTHS = (256, 128, 128, 384, 128, 32, 512, 512, 512, 512, 256, 256) + (256,) * 11


def mixer_prelude(x, mod_l, w_small, tables, lp):
    tm = ROW_TILE
    tab_idx = lambda t: (jnp.where(t < CTX_TILES, 0, 1 + (t - CTX_TILES) % LAT_TILES_PER_SEQ), 0)
    hn = D_HEADS * D_N
    small = [lp['b_q_norm'].reshape(1, -1), lp['b_kv_norm'].reshape(1, -1), lp['b_w_uq'], lp['c_w_gate'],
             lp['c_b_gate'].reshape(2, 1, -1), _lane_repeat_matrix(), lp['d_mu'].reshape(1, -1),
             lp['d_w0'].reshape(2, 1, hn), lp['d_w2'], lp['d_a0'].reshape(2, 1, hn), lp['d_a2'], lp['d_g2'],
             lp['d_k_k'].reshape(1, hn), lp['d_k_a'].reshape(1, hn), lp['d_r_k'].reshape(1, hn), _head_block_diag()]
    in_specs = ([pl.BlockSpec((tm, D_MODEL), lambda t: (t, 0)),
                 pl.BlockSpec((8, D_MODEL), lambda t: (jnp.maximum(t * (tm // 8) - 1, 0), 0)),
                 pl.BlockSpec((8, D_MODEL), lambda t: (jnp.minimum((t + 1) * (tm // 8), N_TOK // 8 - 1), 0)),
                 pl.BlockSpec((1, 6, D_MODEL), lambda t: (_mod_row(t), 0, 0)),
                 _full(w_small.shape)]
                + [pl.BlockSpec((tm, tab.shape[1]), tab_idx) for tab in tables]
                + [_full(a.shape) for a in small])
    return pl.pallas_call(
        _pre_kernel,
        grid=(N_TILES,),
        in_specs=in_specs,
        out_specs=[pl.BlockSpec((tm, w), lambda t: (t, 0)) for w in PRE_OUT_WIDTHS],
        out_shape=[jax.ShapeDtypeStruct((N_TOK, w), F32) for w in PRE_OUT_WIDTHS],
        compiler_params=pltpu.CompilerParams(dimension_semantics=("parallel",), vmem_limit_bytes=VMEM_LIMIT),
        name="mixer_prelude",
    )(x, x, x, mod_l, w_small, *tables, *small)


ATT_Q_BLOCK = 128
ATT_WINDOW = 128
ATT_NEG_INF = -1e30
CACHE_LEN = 512


def _softmax_pv(s, v, sink):
    m = jnp.max(s, axis=-1, keepdims=True)
    if sink is not None:
        m = jnp.maximum(m, sink)
    e = jnp.exp(s - m)
    den = jnp.sum(e, axis=-1, keepdims=True)
    if sink is not None:
        den = den + jnp.exp(sink - m)
    return jnp.dot(e.astype(BF16), v.astype(BF16), preferred_element_type=F32) / den


def _gqa_kernel(sink_ref, q_ref, k_ref, v_ref, *rest, hd, group, scale, windowed):
    if windowed:
        kp_ref, kn_ref, vp_ref, vn_ref, kc_ref, vc_ref, _, o_ref = rest
    else:
        (o_ref,) = rest
    i = pl.program_id(1)
    tq = q_ref.shape[0]
    n_kv = k_ref.shape[1] // hd
    if windowed:
        qpos = i * tq + lax.broadcasted_iota(jnp.int32, (tq, 3 * tq), 0)
        kpos = (i - 1) * tq + lax.broadcasted_iota(jnp.int32, (tq, 3 * tq), 1)
        n_tok = pl.num_programs(1) * tq
        mask = (jnp.abs(qpos - kpos) <= ATT_WINDOW) & (kpos >= 0) & (kpos < n_tok)
        mask = jnp.concatenate([mask] * group, axis=0)
    for kvh in range(n_kv):
        ks = slice(kvh * hd, (kvh + 1) * hd)
        qs = [q_ref[:, (kvh * group + g) * hd:(kvh * group + g + 1) * hd] for g in range(group)]
        q = (jnp.concatenate(qs, axis=0) * scale).astype(BF16)
        sink = jnp.concatenate(
            [jnp.full((tq, 1), sink_ref[kvh * group + g], F32) for g in range(group)], axis=0)
        if windowed:
            k_win = jnp.concatenate([kp_ref[:, ks], k_ref[:, ks], kn_ref[:, ks]], axis=0)
            v_win = jnp.concatenate([vp_ref[:, ks], v_ref[:, ks], vn_ref[:, ks]], axis=0)
            s_win = _bdot(q, k_win.astype(BF16), _D2T)
            s_win = jnp.where(mask, s_win, ATT_NEG_INF)
            s_ctx = _bdot(q, kc_ref[0, 0, kvh].astype(BF16), _D2T)
            s = jnp.concatenate([s_win, s_ctx], axis=1)
            v = jnp.concatenate([v_win, vc_ref[0, 0, kvh]], axis=0)
        else:
            s = _bdot(q, k_ref[:, ks].astype(BF16), _D2T)
            v = v_ref[:, ks]
        o = _softmax_pv(s, v, sink)
        for g in range(group):
            h = kvh * group + g
            o_ref[:, h * hd:(h + 1) * hd] = o[g * tq:(g + 1) * tq]


def gqa_attention(q, k, v, sink, cache_k, cache_v, layer):
    qw, kw = q.shape[1], k.shape[1]
    group = qw // kw
    scale = A_HD ** -0.5
    params = pltpu.CompilerParams(dimension_semantics=("parallel", "parallel"))
    out_shape = jax.ShapeDtypeStruct((N_TOK, qw), F32)
    ctx_spec = lambda w: pl.BlockSpec((CTX_LEN, w), lambda s, i, sk: (s, 0))
    o = pl.pallas_call(
        functools.partial(_gqa_kernel, hd=A_HD, group=group, scale=scale, windowed=False),
        grid_spec=pltpu.PrefetchScalarGridSpec(
            num_scalar_prefetch=1, grid=(N_CTX_SEQ, 1), in_specs=[ctx_spec(qw), ctx_spec(kw), ctx_spec(kw)],
            out_specs=ctx_spec(qw)),
        out_shape=out_shape, compiler_params=params, name="gqa_full",
    )(sink, q, k, v)
    tq = ATT_Q_BLOCK
    nb = LAT_LEN // tq
    base = N_CTX // tq
    blk = lambda w, f: pl.BlockSpec((tq, w), lambda b, i, sk: (base + nb * b + f(i), 0))
    same = lambda i: i
    prev = lambda i: jnp.maximum(i - 1, 0)
    nxt = lambda i: jnp.minimum(i + 1, nb - 1)
    cspec = pl.BlockSpec((1, 1) + cache_k.shape[2:], lambda b, i, sk: (b, layer, 0, 0, 0))
    return pl.pallas_call(
        functools.partial(_gqa_kernel, hd=A_HD, group=group, scale=scale, windowed=True),
        grid_spec=pltpu.PrefetchScalarGridSpec(
            num_scalar_prefetch=1, grid=(N_LAT_SEQ, nb),
            in_specs=[blk(qw, same), blk(kw, same), blk(kw, same), blk(kw, prev), blk(kw, nxt), blk(kw, prev),
                      blk(kw, nxt), cspec, cspec, pl.BlockSpec(memory_space=pl.ANY)],
            out_specs=blk(qw, same)),
        out_shape=out_shape, input_output_aliases={10: 0}, compiler_params=params, name="gqa_windowed",
    )(sink, q, k, v, k, k, v, v, cache_k, cache_v, o)


def _mla_kernel(q_ref, ckv_ref, kpe_ref, wukv_ref, *rest, n_heads, nope, rope, vd, scale, cached):
    if cached:
        cckv_ref, ckpe_ref, _, o_ref, kv_scr, kpe_scr = rest
    else:
        o_ref, kv_scr, kpe_scr = rest
    i = pl.program_id(1)
    n_cache = kv_scr.shape[0] - ckv_ref.shape[0]

    @pl.when(i == 0)
    def _():
        w = wukv_ref[...].astype(BF16)
        if cached:
            kv_scr[:n_cache] = jnp.dot(cckv_ref[0, 0].astype(BF16), w, preferred_element_type=F32).astype(BF16)
            kpe_scr[:n_cache] = ckpe_ref[0, 0].astype(BF16)
        kv_scr[n_cache:] = jnp.dot(ckv_ref[...].astype(BF16), w, preferred_element_type=F32).astype(BF16)
        kpe_scr[n_cache:] = kpe_ref[...].astype(BF16)

    kpe = kpe_scr[...]
    qd = nope + rope
    for h in range(n_heads):
        qn = (q_ref[:, h * qd:h * qd + nope] * scale).astype(BF16)
        qp = (q_ref[:, h * qd + nope:(h + 1) * qd] * scale).astype(BF16)
        k_n = kv_scr[:, h * (nope + vd):h * (nope + vd) + nope]
        v = kv_scr[:, h * (nope + vd) + nope:(h + 1) * (nope + vd)]
        s = _bdot(qn, k_n, _D2T) + _bdot(qp, kpe, _D2T)
        o_ref[:, h * vd:(h + 1) * vd] = _softmax_pv(s, v, None)


def mla_attention(q, ckv, kpe, w_ukv, cache_ckv, cache_kpe, layer):
    qw = q.shape[1]
    tq = ATT_Q_BLOCK
    kw = dict(n_heads=B_HEADS, nope=B_NOPE, rope=B_ROPE, vd=B_VD, scale=(B_NOPE + B_ROPE) ** -0.5)
    params = pltpu.CompilerParams(dimension_semantics=("parallel", "arbitrary"))
    out_shape = jax.ShapeDtypeStruct((N_TOK, B_HEADS * B_VD), F32)
    kvw = B_HEADS * (B_NOPE + B_VD)
    nbc = CTX_LEN // tq
    o = pl.pallas_call(
        functools.partial(_mla_kernel, cached=False, **kw),
        grid=(N_CTX_SEQ, nbc),
        in_specs=[pl.BlockSpec((tq, qw), lambda s, i: (s * nbc + i, 0)),
                  pl.BlockSpec((CTX_LEN, B_KV_LORA), lambda s, i: (s, 0)),
                  pl.BlockSpec((CTX_LEN, B_ROPE), lambda s, i: (s, 0)),
                  _full(w_ukv.shape)],
        out_specs=pl.BlockSpec((tq, B_HEADS * B_VD), lambda s, i: (s * nbc + i, 0)),
        out_shape=out_shape,
        scratch_shapes=[pltpu.VMEM((CTX_LEN, kvw), BF16), pltpu.VMEM((CTX_LEN, B_ROPE), BF16)],
        compiler_params=params, name="mla_context",
    )(q, ckv, kpe, w_ukv)
    nb = LAT_LEN // tq
    base = N_CTX // tq
    lat0 = N_CTX // LAT_LEN
    s_len = CACHE_LEN + LAT_LEN
    return pl.pallas_call(
        functools.partial(_mla_kernel, cached=True, **kw),
        grid=(N_LAT_SEQ, nb),
        in_specs=[pl.BlockSpec((tq, qw), lambda b, i: (base + nb * b + i, 0)),
                  pl.BlockSpec((LAT_LEN, B_KV_LORA), lambda b, i: (lat0 + b, 0)),
                  pl.BlockSpec((LAT_LEN, B_ROPE), lambda b, i: (lat0 + b, 0)),
                  _full(w_ukv.shape),
                  pl.BlockSpec((1, 1, CACHE_LEN, B_KV_LORA), lambda b, i: (b, layer, 0, 0)),
                  pl.BlockSpec((1, 1, CACHE_LEN, B_ROPE), lambda b, i: (b, layer, 0, 0)),
                  pl.BlockSpec(memory_space=pl.ANY)],
        out_specs=pl.BlockSpec((tq, B_HEADS * B_VD), lambda b, i: (base + nb * b + i, 0)),
        out_shape=out_shape, input_output_aliases={6: 0},
        scratch_shapes=[pltpu.VMEM((s_len, kvw), BF16), pltpu.VMEM((s_len, B_ROPE), BF16)],
        compiler_params=params, name="mla_latent",
    )(q, ckv, kpe, w_ukv, cache_ckv, cache_kpe, o)


CHUNK = 64
GLA_SUB = 16
N_CHAIN = 2 * 4


def _chunk_tables():
    fwd, bwd, sid, first, last = [], [], [], [], []
    seqs = [(s * (CTX_LEN // CHUNK), CTX_LEN // CHUNK) for s in range(N_CTX_SEQ)]
    seqs += [(N_CTX // CHUNK + b * (LAT_LEN // CHUNK), LAT_LEN // CHUNK) for b in range(N_LAT_SEQ)]
    for s, (base, nc) in enumerate(seqs):
        for c in range(nc):
            fwd.append(base + c)
            bwd.append(base + nc - 1 - c)
            sid.append(s)
            first.append(int(c == 0))
            last.append(int(c == nc - 1))
    return tuple(jnp.asarray(np.array(a, np.int32)) for a in (fwd, bwd, sid, first, last))


def _chains(ref_f, ref_b, width):
    return jnp.stack([ref[:, h * width:(h + 1) * width] for ref in (ref_f, ref_b) for h in range(4)], axis=0)


def _unchain(y, o_f, o_b):
    o_f[...] = jnp.concatenate([y[h] for h in range(4)], axis=-1)
    o_b[...] = jnp.concatenate([y[4 + h] for h in range(4)], axis=-1)


def _dir_masks(L):
    shape = (N_CHAIN, L, L)
    back = lax.broadcasted_iota(jnp.int32, shape, 0) >= 4
    row = lax.broadcasted_iota(jnp.int32, shape, 1)
    col = lax.broadcasted_iota(jnp.int32, shape, 2)
    ahead = jnp.where(back, col - row, row - col)
    return ahead >= 0, ahead > 0, row == col


def _chunk_end(ci):
    L = ci.shape[1]
    back = lax.broadcasted_iota(jnp.int32, (N_CHAIN, 1, 1), 0) >= 4
    return jnp.where(back, ci[:, 0:1], ci[:, L - 1:L])


def _rwkv_kernel(fwd_ref, bwd_ref, sid_ref, first_ref, last_ref,
                 rf, rb, vf, vb, kkf, kkb, lwf, lwb, kf, kb, af, ab, s0_ref, yf_ref, yb_ref, sfin_ref, s_scr, *, dot):
    step = pl.program_id(0)

    @pl.when(first_ref[step] == 1)
    def _():
        s_scr[...] = s0_ref[0]

    n = D_N
    r = _chains(rf, rb, n)
    v = _chains(vf, vb, n)
    kk = _chains(kkf, kkb, n)
    lw = _chains(lwf, lwb, n)
    k = _chains(kf, kb, n)
    a = _chains(af, ab, n)
    L = r.shape[1]
    S = s_scr[...]
    incl, strict, diag = _dir_masks(L)
    ci = _dot_exact_lhs(jnp.where(incl, 1.0, 0.0), lw, _NN)
    ce = ci - lw
    cl = _chunk_end(ci)
    e_neg = jnp.exp(-ci)
    b = a * kk
    alpha = kk * jnp.exp(ce)
    rho = r * jnp.exp(ci)
    beta = b * e_neg
    kappa = k * e_neg
    e_end = jnp.exp(cl - ci)
    ar = jnp.concatenate([alpha, rho], axis=1)
    bk = jnp.concatenate([beta, kappa], axis=1)
    w = dot(ar, bk, _NT)
    nmat = jnp.where(strict, w[:, :L, :L], 0.0)
    mmat = jnp.where(strict, w[:, :L, L:], 0.0)
    p1 = jnp.where(incl, w[:, L:, :L], 0.0)
    p2 = jnp.where(incl, w[:, L:, L:], 0.0)
    x = jnp.where(diag, 1.0, 0.0) - nmat
    p = dot(nmat, nmat, _NN)
    span = 2
    while True:
        x = x + dot(x, p, _NN)
        span *= 2
        if span >= L:
            break
        p = dot(p, p, _NN)
    us = dot(ar, S, _NT)
    rhs = us[:, :L] + dot(mmat, v, _NN)
    d = -dot(x, rhs, _NN)
    dv = jnp.concatenate([d, v], axis=1)
    pp = jnp.concatenate([p1, p2], axis=2)
    _unchain(us[:, L:] + dot(pp, dv, _NN), yf_ref, yb_ref)
    bk_end = jnp.concatenate([b * e_end, k * e_end], axis=1)
    s_new = S * jnp.exp(cl) + dot(dv, bk_end, _TN)
    s_scr[...] = s_new

    @pl.when(last_ref[step] == 1)
    def _():
        sfin_ref[0] = s_new


def _gla_kernel(fwd_ref, bwd_ref, sid_ref, first_ref, last_ref,
                qf, qb, kf, kb, vf, vb, laf, lab, s0_ref, of_ref, ob_ref, sfin_ref, s_scr, *, dot):
    step = pl.program_id(0)

    @pl.when(first_ref[step] == 1)
    def _():
        s_scr[...] = s0_ref[0]

    q4 = _chains(qf, qb, 128)
    k4 = _chains(kf, kb, 128)
    la4 = _chains(laf, lab, 128)
    v = _chains(vf, vb, C_DV)
    g, L, lanes = q4.shape
    dk = C_DK
    n_sub = L // GLA_SUB
    st = s_scr[...]
    incl, _, _ = _dir_masks(L)
    c = _dot_exact_lhs(jnp.where(incl, 1.0, 0.0), la4, _NN)
    shape = (g, L, lanes)
    back = lax.broadcasted_iota(jnp.int32, shape, 0) >= 4
    lane_blk = lax.broadcasted_iota(jnp.int32, shape, 2) // dk
    row_blk = lax.broadcasted_iota(jnp.int32, shape, 1) // GLA_SUB
    cref_f = jnp.zeros(shape, F32)
    cref_b = jnp.zeros(shape, F32)
    for j in range(1, n_sub):
        cref_f = jnp.where(lane_blk == j, c[:, j * GLA_SUB - 1:j * GLA_SUB], cref_f)
        cref_b = jnp.where(lane_blk == j - 1, c[:, j * GLA_SUB:j * GLA_SUB + 1], cref_b)
    cref = jnp.where(back, cref_b, cref_f)
    q_on = row_blk == lane_blk
    k_on = jnp.where(back, row_blk - lane_blk, lane_blk - row_blk) >= 0
    qh = jnp.where(q_on, q4 * jnp.exp(jnp.where(q_on, c - cref, 0.0)), 0.0)
    kh = jnp.where(k_on, k4 * jnp.exp(jnp.where(k_on, cref - c, 0.0)), 0.0)
    att = jnp.where(incl, dot(qh, kh, _NT), 0.0)
    cl = _chunk_end(c)
    qe = (q4 * jnp.exp(c))[:, :, :dk]
    ke = (k4 * jnp.exp(cl - c))[:, :, :dk]
    _unchain(dot(qe, st, _NT) + dot(att, v, _NN), of_ref, ob_ref)
    s_new = st * jnp.exp(cl[:, :, :dk]) + dot(v, ke, _TN)
    s_scr[...] = s_new

    @pl.when(last_ref[step] == 1)
    def _():
        sfin_ref[0] = s_new


def _recurrence_call(kernel_fn, name, tables, pairs, singles_f, singles_b, s0, out_width):
    fwd_map = lambda s, fwd, bwd, sid, first, last: (fwd[s], 0)
    bwd_map = lambda s, fwd, bwd, sid, first, last: (bwd[s], 0)
    st_map = lambda s, fwd, bwd, sid, first, last: (sid[s], 0, 0, 0)
    args, in_specs = [], []
    for arr in pairs:
        args += [arr, arr]
        in_specs += [pl.BlockSpec((CHUNK, arr.shape[1]), fwd_map), pl.BlockSpec((CHUNK, arr.shape[1]), bwd_map)]
    for af, ab in zip(singles_f, singles_b):
        args += [af, ab]
        in_specs += [pl.BlockSpec((CHUNK, af.shape[1]), fwd_map), pl.BlockSpec((CHUNK, ab.shape[1]), bwd_map)]
    st_spec = pl.BlockSpec((1,) + s0.shape[1:], st_map)
    n_steps = tables[0].shape[0]
    return pl.pallas_call(
        kernel_fn,
        grid_spec=pltpu.PrefetchScalarGridSpec(
            num_scalar_prefetch=5, grid=(n_steps,), in_specs=in_specs + [st_spec],
            out_specs=[pl.BlockSpec((CHUNK, out_width), fwd_map), pl.BlockSpec((CHUNK, out_width), bwd_map), st_spec],
            scratch_shapes=[pltpu.VMEM(s0.shape[1:], F32)]),
        out_shape=[jax.ShapeDtypeStruct((N_TOK, out_width), F32), jax.ShapeDtypeStruct((N_TOK, out_width), F32),
                   jax.ShapeDtypeStruct(s0.shape, F32)],
        compiler_params=pltpu.CompilerParams(dimension_semantics=("arbitrary",)),
        name=name,
    )(*tables, *args, s0)


def _layer_norm(x, g, b):
    mu = jnp.mean(x, axis=-1, keepdims=True)
    xc = x - mu
    var = jnp.mean(xc * xc, axis=-1, keepdims=True)
    return xc * lax.rsqrt(var + LN_EPS) * g + b


def _merge_kernel(x_ref, mod_ref, oa_ref, ob_ref, cof_ref, cob_ref, cgate_ref, yf_ref, yb_ref, bonus_ref, dgate_ref,
                  wg_ref, wbr_ref, wout_ref, cnorm_ref, dlng_ref, dlnb_ref, lng_ref, lnb_ref, wr_ref, br_ref, bd_ref,
                  x1_o, h2_o, topi_o, topw_o):
    x = x_ref[...]
    m = mod_ref[0]
    sh1, sc1, g1, sh2, sc2 = m[0:1], m[1:2], m[2:3], m[3:4], m[4:5]
    bd = bd_ref[...]
    inv_n = 1.0 / D_N
    co = cof_ref[...] + cob_ref[...]
    o_c = co * lax.rsqrt(_dot_exact_rhs(co * co, bd) * inv_n + RMS_EPS) * cnorm_ref[...] * cgate_ref[...]
    y = yf_ref[...] + yb_ref[...]
    yc = y - _dot_exact_rhs(y, bd) * inv_n
    var = _dot_exact_rhs(yc * yc, bd) * inv_n
    o_d = (yc * lax.rsqrt(var + D_GN_EPS) * dlng_ref[...] + dlnb_ref[...] + bonus_ref[...]) * dgate_ref[...]
    branches = (oa_ref[...], ob_ref[...], o_c, o_d)
    h = (x * (1.0 + sc1) + sh1).astype(BF16)
    merged = None
    for n in range(N_BRANCH):
        gate = _sigmoid(jnp.dot(h, wg_ref[:, n * D_MODEL:(n + 1) * D_MODEL], preferred_element_type=F32))
        term = gate * jnp.dot(branches[n].astype(BF16), wbr_ref[n], preferred_element_type=F32)
        merged = term if merged is None else merged + term
    mix = jnp.dot(merged.astype(BF16), wout_ref[...], preferred_element_type=F32)
    x1 = _layer_norm(ALPHA * x + g1 * mix, lng_ref[...], lnb_ref[...])
    x1_o[...] = x1
    h2 = x1 * (1.0 + sc2) + sh2
    h2_o[...] = h2.astype(BF16)
    logits = _dot6(h2, wr_ref[...]) + br_ref[...]
    tm, n_e = logits.shape
    lane_e = lax.broadcasted_iota(jnp.int32, (tm, n_e), 1)
    lane_o = lax.broadcasted_iota(jnp.int32, (tm, topi_o.shape[1]), 1)
    top_i = jnp.zeros((tm, topi_o.shape[1]), jnp.int32)
    top_v = jnp.zeros((tm, topw_o.shape[1]), F32)
    vals = []
    for kth in range(TOP_K):
        mx = jnp.max(logits, axis=-1, keepdims=True)
        idx = jnp.min(jnp.where(logits == mx, lane_e, n_e), axis=-1, keepdims=True)
        vals.append(mx)
        top_i = jnp.where(lane_o == kth, idx, top_i)
        logits = jnp.where(lane_e == idx, -jnp.inf, logits)
    es = [jnp.exp(vk - vals[0]) for vk in vals]
    den = es[0] + es[1] + es[2] + es[3]
    for kth in range(TOP_K):
        top_v = jnp.where(lane_o == kth, es[kth] / den, top_v)
    topi_o[...] = top_i
    topw_o[...] = top_v


def merge_and_route(x, mod_l, o_a, o_b, co_f, co_b, cgate, y_f, y_b, bonus, dgate, w_g, w_br, w_out, lp):
    tm = ROW_TILE
    hn = D_HEADS * D_N
    row = lambda w: pl.BlockSpec((tm, w), lambda t: (t, 0))
    small = [jnp.tile(lp['c_norm'], C_HEADS).reshape(1, hn), lp['d_ln_g'].reshape(1, hn), lp['d_ln_b'].reshape(1, hn),
             lp['ln_g'][0].reshape(1, -1), lp['ln_b'][0].reshape(1, -1), lp['w_router'],
             lp['b_router'].reshape(1, -1), _head_block_diag()]
    return pl.pallas_call(
        _merge_kernel,
        grid=(N_TILES,),
        in_specs=([row(D_MODEL), pl.BlockSpec((1, 6, D_MODEL), lambda t: (_mod_row(t), 0, 0))]
                  + [row(hn)] * 9 + [_full(w_g.shape), _full(w_br.shape), _full(w_out.shape)]
                  + [_full(a.shape) for a in small]),
        out_specs=[row(D_MODEL), row(D_MODEL), row(128), row(128)],
        out_shape=[jax.ShapeDtypeStruct((N_TOK, D_MODEL), F32), jax.ShapeDtypeStruct((N_TOK, D_MODEL), BF16),
                   jax.ShapeDtypeStruct((N_TOK, 128), jnp.int32), jax.ShapeDtypeStruct((N_TOK, 128), F32)],
        compiler_params=pltpu.CompilerParams(dimension_semantics=("parallel",), vmem_limit_bytes=VMEM_LIMIT),
        name="merge_and_route",
    )(x, mod_l, o_a, o_b, co_f, co_b, cgate, y_f, y_b, bonus, dgate, w_g, w_br, w_out, *small)


MOE_ROW_TILE = 256


def _moe_kernel(te_ref, tv_ref, x_ref, w1_ref, b1_ref, w2_ref, b2_ref, perm_ref, y_ref, w1s, w2s, hs):
    t = pl.program_id(0)
    e = te_ref[t]
    prev = te_ref[jnp.maximum(t - 1, 0)]
    new_expert = jnp.logical_or(t == 0, e != prev)
    valid = tv_ref[t] != 0
    d_model, two_f = w1s.shape
    n_blk = two_f // MXU_WIDTH
    half = MXU_WIDTH // 2

    @pl.when(new_expert)
    def _():
        for blk in range(n_blk):
            sl = slice(blk * MXU_WIDTH, (blk + 1) * MXU_WIDTH)
            wb = w1_ref[0, 0, :, sl].astype(BF16)
            w1s[:, sl] = jnp.dot(wb, perm_ref[...], preferred_element_type=F32).astype(BF16)
        w2s[...] = w2_ref[0, 0].astype(BF16)

    @pl.when(valid)
    def _():
        x = x_ref[...]
        for blk in range(n_blk):
            sl = slice(blk * MXU_WIDTH, (blk + 1) * MXU_WIDTH)
            u = jnp.dot(x, w1s[:, sl], preferred_element_type=F32) + b1_ref[0, 0, :, sl]
            glu = jnp.minimum(u[:, :half], SWIGLU_LIMIT)
            lin = jnp.clip(u[:, half:], -SWIGLU_LIMIT, SWIGLU_LIMIT)
            hs[:, blk * half:(blk + 1) * half] = (glu * _sigmoid(SWIGLU_ALPHA * glu) * (lin + 1.0)).astype(BF16)
        y = jnp.dot(hs[...], w2s[...], preferred_element_type=F32) + b2_ref[0, 0]
        y_ref[...] = y.astype(y_ref.dtype)

    @pl.when(jnp.logical_not(valid))
    def _():
        y_ref[...] = jnp.zeros_like(y_ref)


def _deinterleave_perm():
    half = MXU_WIDTH // 2
    src = np.arange(MXU_WIDTH)
    dst = np.where(src % 2 == 0, src // 2, half + src // 2)
    p = np.zeros((MXU_WIDTH, MXU_WIDTH), np.float32)
    p[src, dst] = 1.0
    return jnp.asarray(p, BF16)


def _moe_dispatch(top_i):
    n, k = top_i.shape
    tm = MOE_ROW_TILE
    p_rows = n * k + N_EXPERTS * tm
    sel = jnp.sum((top_i[:, :, None] == jnp.arange(N_EXPERTS, dtype=jnp.int32)).astype(jnp.int32), axis=1)
    before = jnp.cumsum(sel, axis=0) - sel
    rank = jnp.take_along_axis(before, top_i, axis=1)
    counts = jnp.sum(sel, axis=0)
    padded = ((counts + tm - 1) // tm) * tm
    ends = jnp.cumsum(padded)
    starts = ends - padded
    pos = starts[top_i] + rank
    n_tiles = p_rows // tm
    tile_start = jnp.arange(n_tiles, dtype=jnp.int32) * tm
    tile_valid = (tile_start < ends[-1]).astype(jnp.int32)
    last_tile = ends[-1] // tm - 1
    tile_expert = jnp.sum(ends[None, :] <= jnp.minimum(tile_start, last_tile * tm)[:, None], axis=1).astype(jnp.int32)
    keys = jnp.sort((top_i * n + jnp.arange(n, dtype=jnp.int32)[:, None]).reshape(-1))
    row_e = jnp.repeat(tile_expert, tm)
    row_rank = jnp.arange(p_rows, dtype=jnp.int32) - starts[row_e]
    sorted_at = jnp.clip((jnp.cumsum(counts) - counts)[row_e] + row_rank, 0, n * k - 1)
    src_tok = jnp.where(row_rank < counts[row_e], keys[sorted_at] % n, 0)
    return pos, src_tok, tile_expert, tile_valid, p_rows


def moe_experts(h2, top_i, layer, w1, b1, w2, b2):
    n, d = h2.shape
    depth, e, _, two_f = w1.shape
    f = two_f // 2
    tm = MOE_ROW_TILE
    pos, src_tok, tile_expert, tile_valid, p_rows = _moe_dispatch(top_i)
    xs = h2[src_tok]
    b1p = b1.reshape(depth, e, two_f // MXU_WIDTH, MXU_WIDTH // 2, 2).swapaxes(3, 4).reshape(depth, e, 1, two_f)
    grid_spec = pltpu.PrefetchScalarGridSpec(
        num_scalar_prefetch=2,
        grid=(p_rows // tm,),
        in_specs=[
            pl.BlockSpec((tm, d), lambda t, te, tv: (t, 0)),
            pl.BlockSpec((1, 1, d, two_f), lambda t, te, tv: (layer, te[t], 0, 0)),
            pl.BlockSpec((1, 1, 1, two_f), lambda t, te, tv: (layer, te[t], 0, 0)),
            pl.BlockSpec((1, 1, f, d), lambda t, te, tv: (layer, te[t], 0, 0)),
            pl.BlockSpec((1, 1, 1, d), lambda t, te, tv: (layer, te[t], 0, 0)),
            pl.BlockSpec((MXU_WIDTH, MXU_WIDTH), lambda t, te, tv: (0, 0)),
        ],
        out_specs=pl.BlockSpec((tm, d), lambda t, te, tv: (t, 0)),
        scratch_shapes=[pltpu.VMEM((d, two_f), BF16), pltpu.VMEM((f, d), BF16), pltpu.VMEM((tm, f), BF16)],
    )
    ys = pl.pallas_call(
        _moe_kernel,
        grid_spec=grid_spec,
        out_shape=jax.ShapeDtypeStruct((p_rows, d), BF16),
        compiler_params=pltpu.CompilerParams(dimension_semantics=("arbitrary",),
                                             vmem_limit_bytes=48 * 1024 * 1024),
        name="moe_experts",
    )(tile_expert, tile_valid, xs, w1, b1p, w2, b2.reshape(depth, e, 1, d), _deinterleave_perm())
    return ys[pos.T]


def _final_kernel(x1_ref, mod_ref, ys_ref, topw_ref, lng_ref, lnb_ref, o_ref):
    g2 = mod_ref[0, 5:6]
    moe = None
    for kth in range(TOP_K):
        term = ys_ref[kth].astype(F32) * topw_ref[:, kth:kth + 1]
        moe = term if moe is None else moe + term
    o_ref[...] = _layer_norm(ALPHA * x1_ref[...] + g2 * moe, lng_ref[...], lnb_ref[...])


def combine_and_norm(x1, mod_l, ys, top_w, ln_g, ln_b):
    tm = ROW_TILE
    return pl.pallas_call(
        _final_kernel,
        grid=(N_TILES,),
        in_specs=[pl.BlockSpec((tm, D_MODEL), lambda t: (t, 0)),
                  pl.BlockSpec((1, 6, D_MODEL), lambda t: (_mod_row(t), 0, 0)),
                  pl.BlockSpec((TOP_K, tm, D_MODEL), lambda t: (0, t, 0)),
                  pl.BlockSpec((tm, 128), lambda t: (t, 0)),
                  _full((1, D_MODEL)), _full((1, D_MODEL))],
        out_specs=pl.BlockSpec((tm, D_MODEL), lambda t: (t, 0)),
        out_shape=jax.ShapeDtypeStruct((N_TOK, D_MODEL), F32),
        compiler_params=pltpu.CompilerParams(dimension_semantics=("parallel",)),
        name="combine_and_norm",
    )(x1, mod_l, ys, top_w, ln_g.reshape(1, -1), ln_b.reshape(1, -1))


def kernel(x_prompt, x_sample, cache_a_k, cache_a_v, cache_b_ckv, cache_b_kpe, state_c, state_d, c,
           c_ctx, w_mod, b_mod, w_in, a_sink, b_q_norm, b_w_uq, b_kv_norm, b_w_ukv, c_w_gate, c_b_gate,
           c_norm, d_mu, d_w0, d_w2, d_a0, d_a2, d_g2, d_k_k, d_k_a, d_r_k, d_ln_g, d_ln_b, w_br, w_out,
           ln_g, ln_b, w_router, b_router, w_mlp1, b_mlp1, w_mlp2, b_mlp2):
    params = dict(b_q_norm=b_q_norm, b_w_uq=b_w_uq, b_kv_norm=b_kv_norm, b_w_ukv=b_w_ukv, c_w_gate=c_w_gate,
                  c_b_gate=c_b_gate, c_norm=c_norm, d_mu=d_mu, d_w0=d_w0, d_w2=d_w2, d_a0=d_a0, d_a2=d_a2, d_g2=d_g2,
                  d_k_k=d_k_k, d_k_a=d_k_a, d_r_k=d_r_k, d_ln_g=d_ln_g, d_ln_b=d_ln_b, ln_g=ln_g, ln_b=ln_b,
                  w_router=w_router, b_router=b_router)
    assert x_prompt.shape == (N_CTX_SEQ, CTX_LEN, D_MODEL) and x_sample.shape == (N_LAT_SEQ, LAT_LEN, D_MODEL)
    x = jnp.concatenate([x_prompt.reshape(N_CTX, D_MODEL), x_sample.reshape(-1, D_MODEL)], axis=0)
    cond8 = jnp.concatenate([c_ctx[None], c, jnp.zeros((8 - 1 - N_LAT_SEQ, D_MODEL), F32)], axis=0)
    mod = modulation_table(cond8, w_mod, b_mod)[:, :1 + N_LAT_SEQ].reshape(DEPTH, 1 + N_LAT_SEQ, 6, D_MODEL)
    tables = _rope_tables()
    chunk_tables = _chunk_tables()
    zeros_c = jnp.zeros((N_CTX_SEQ, 2, C_HEADS, C_DK, C_DV), F32)
    zeros_d = jnp.zeros((N_CTX_SEQ, 2, D_HEADS, D_N, D_N), F32)
    new = {name: [] for name in ("a_k", "a_v", "b_ckv", "b_kpe", "c", "d")}
    for l in range(DEPTH):
        lp = {name: val[l] for name, val in params.items()}
        w_small = jnp.concatenate([w_in[l][:, _ORIG[nm][0]:_ORIG[nm][1]] for nm in _ORDER], axis=1).astype(BF16)
        w_g = w_in[l][:, G_START:].astype(BF16)
        w_br_l = w_br[l].astype(BF16)
        w_out_l = w_out[l].astype(BF16)

        (aq, ak, av, bq, bckv, bkpe, cq4, ck4, cla_f, cla_b, cv, cgate,
         r, v, kk, lw_f, lw_b, k_f, k_b, a_f, a_b, bonus, dgate) = mixer_prelude(x, mod[l], w_small, tables, lp)

        o_a = gqa_attention(aq, ak, av, a_sink[l], cache_a_k, cache_a_v, l)
        o_b = mla_attention(bq, bckv, bkpe, b_w_ukv[l], cache_b_ckv, cache_b_kpe, l)

        c_s0 = jnp.concatenate([zeros_c, state_c[:, l]], axis=0)
        c_s0 = jnp.swapaxes(c_s0, 3, 4).reshape(N_SEQ, N_CHAIN, C_DV, C_DK)
        co_f, co_b, c_fin = _recurrence_call(functools.partial(_gla_kernel, dot=_dot1), "gla_chunked", chunk_tables,
                                             [cq4, ck4, cv], [cla_f], [cla_b], c_s0, C_HEADS * C_DV)
        d_s0 = jnp.concatenate([zeros_d, state_d[:, l]], axis=0).reshape(N_SEQ, N_CHAIN, D_N, D_N)
        y_f, y_b, d_fin = _recurrence_call(functools.partial(_rwkv_kernel, dot=_dot1), "rwkv7_chunked", chunk_tables,
                                           [r, v, kk], [lw_f, k_f, a_f], [lw_b, k_b, a_b], d_s0, D_HEADS * D_N)

        x1, h2, top_i, top_w = merge_and_route(x, mod[l], o_a, o_b, co_f, co_b, cgate, y_f, y_b, bonus, dgate,
                                               w_g, w_br_l, w_out_l, lp)
        ys = moe_experts(h2, top_i[:, :TOP_K], l, w_mlp1, b_mlp1, w_mlp2, b_mlp2)
        x = combine_and_norm(x1, mod[l], ys, top_w, ln_g[l, 1], ln_b[l, 1])

        new["a_k"].append(ak[:N_CTX].reshape(N_CTX_SEQ, CTX_LEN, A_KV_HEADS, A_HD).transpose(0, 2, 1, 3))
        new["a_v"].append(av[:N_CTX].reshape(N_CTX_SEQ, CTX_LEN, A_KV_HEADS, A_HD).transpose(0, 2, 1, 3))
        new["b_ckv"].append(bckv[:N_CTX].reshape(N_CTX_SEQ, CTX_LEN, B_KV_LORA))
        new["b_kpe"].append(bkpe[:N_CTX].reshape(N_CTX_SEQ, CTX_LEN, B_ROPE))
        new["c"].append(jnp.swapaxes(c_fin[:N_CTX_SEQ].reshape(N_CTX_SEQ, 2, C_HEADS, C_DV, C_DK), 3, 4))
        new["d"].append(d_fin[:N_CTX_SEQ].reshape(N_CTX_SEQ, 2, D_HEADS, D_N, D_N))
    y_prompt = x[:N_CTX].reshape(x_prompt.shape)
    y_sample = x[N_CTX:].reshape(x_sample.shape)
    return (y_prompt, y_sample, *(jnp.stack(new[name], axis=1) for name in ("a_k", "a_v", "b_ckv", "b_kpe", "c", "d")))
```

```python
import functools

import jax
import jax.numpy as jnp
import numpy as np
from jax import lax
from jax.experimental import pallas as pl
from jax.experimental.pallas import tpu as pltpu

F32 = jnp.float32
BF16 = jnp.bfloat16

MXU_WIDTH = 256
VMEM_LIMIT = 56 * 1024 * 1024

D_MODEL = 1024
DEPTH = 2
GRID_W = 64
ROPE_BASE = 10000.0
A_HEADS, A_KV_HEADS, A_HD = 4, 2, 64
B_HEADS, B_NOPE, B_ROPE, B_VD, B_Q_LORA, B_KV_LORA = 4, 64, 32, 64, 192, 128
C_HEADS, C_DK, C_DV, C_GATE_RANK, C_GATE_TEMP = 4, 32, 64, 16, 16.0
D_HEADS, D_N, D_DECAY_RANK, D_AAA_RANK, D_GATE_RANK, D_GN_EPS = 4, 64, 64, 64, 128, 64e-5
BRANCH_W = 256
N_BRANCH = 4
N_EXPERTS = 32
TOP_K = 4
SWIGLU_LIMIT = 7.0
SWIGLU_ALPHA = 1.702
ALPHA = (2 * DEPTH) ** 0.25
LN_EPS = 1e-5
RMS_EPS = 1e-6

N_CTX_SEQ, CTX_LEN = 16, 256
N_LAT_SEQ, LAT_LEN = 2, 2048
N_CTX = N_CTX_SEQ * CTX_LEN
N_TOK = N_CTX + N_LAT_SEQ * LAT_LEN
ROW_TILE = 256
N_TILES = N_TOK // ROW_TILE
CTX_TILES = N_CTX // ROW_TILE
LAT_TILES_PER_SEQ = LAT_LEN // ROW_TILE
N_SEQ = N_CTX_SEQ + N_LAT_SEQ

_ORIG = dict(aq=(0, 256), ak=(256, 384), av=(384, 512), bcq=(512, 704), bckv=(704, 832), bkpe=(832, 864),
             cq=(864, 992), ck=(992, 1120), cv=(1120, 1376), cog=(1376, 1632), caf=(1632, 1648), cab=(1648, 1664),
             zd=(1664, 2816))
_ORDER = ("aq", "ak", "av", "cq", "ck", "cv", "cog", "zd", "bcq", "caf", "cab", "bkpe", "bckv")
COL = {}
_off = 0
for _name in _ORDER:
    _w = _ORIG[_name][1] - _ORIG[_name][0]
    COL[_name] = (_off, _off + _w)
    _off += _w
SMALL_COLS = _off
G_START = 2816


def _cs(name):
    return slice(*COL[name])


def _split3(x):
    hi = x.astype(BF16)
    r1 = x - hi.astype(F32)
    mid = r1.astype(BF16)
    lo = (r1 - mid.astype(F32)).astype(BF16)
    return hi, mid, lo


def _split2(x):
    hi = x.astype(BF16)
    lo = (x - hi.astype(F32)).astype(BF16)
    return hi, lo


def _bdot(a, b, dims):
    return lax.dot_general(a, b, dims, preferred_element_type=F32)


_D2 = (((1,), (0,)), ((), ()))
_D2T = (((1,), (1,)), ((), ()))
_NN = (((2,), (1,)), ((0,), (0,)))
_NT = (((2,), (2,)), ((0,), (0,)))
_TN = (((1,), (1,)), ((0,), (0,)))


def _dot1(a, b, dims=_D2):
    return _bdot(a.astype(BF16), b.astype(BF16), dims)


def _dot3(a, b, dims=_D2):
    ah, al = _split2(a)
    bh, bl = _split2(b)
    return _bdot(ah, bh, dims) + (_bdot(ah, bl, dims) + _bdot(al, bh, dims))


def _dot_exact_lhs(a01, b, dims=_D2):
    a = a01.astype(BF16)
    h, m, l = _split3(b)
    return _bdot(a, h, dims) + (_bdot(a, m, dims) + _bdot(a, l, dims))


def _dot_exact_rhs(a, b01, dims=_D2):
    b = b01.astype(BF16)
    h, m, l = _split3(a)
    return _bdot(h, b, dims) + (_bdot(m, b, dims) + _bdot(l, b, dims))


def _dot6(a, b, dims=_D2):
    ah, am, al = _split3(a)
    bh, bm, bl = _split3(b)
    return (_bdot(ah, bh, dims) + (_bdot(ah, bm, dims) + _bdot(am, bh, dims))
            + (_bdot(am, bm, dims) + (_bdot(ah, bl, dims) + _bdot(al, bh, dims))))


def _sigmoid(x):
    return 1.0 / (1.0 + jnp.exp(-x))


def _softplus(x):
    return jnp.maximum(x, 0.0) + jnp.log(1.0 + jnp.exp(-jnp.abs(x)))


def _mod_row(t):
    return jnp.where(t < CTX_TILES, 0, 1 + (t - CTX_TILES) // LAT_TILES_PER_SEQ)


def _full(shape):
    nd = len(shape)
    return pl.BlockSpec(shape, lambda *_: (0,) * nd)


MOD_COL_TILE = 1536


def _mod_kernel(c_ref, w_ref, b_ref, o_ref):
    c = c_ref[...]
    o_ref[0] = _dot3(c * _sigmoid(c), w_ref[0]) + b_ref[0]


def modulation_table(cond8, w_mod, b_mod):
    depth, d, six_d = w_mod.shape
    return pl.pallas_call(
        _mod_kernel,
        grid=(depth, six_d // MOD_COL_TILE),
        in_specs=[pl.BlockSpec((8, d), lambda l, j: (0, 0)),
                  pl.BlockSpec((1, d, MOD_COL_TILE), lambda l, j: (l, 0, j)),
                  pl.BlockSpec((1, 1, MOD_COL_TILE), lambda l, j: (l, 0, j))],
        out_specs=pl.BlockSpec((1, 8, MOD_COL_TILE), lambda l, j: (l, 0, j)),
        out_shape=jax.ShapeDtypeStruct((depth, 8, six_d), F32),
        compiler_params=pltpu.CompilerParams(dimension_semantics=("parallel", "parallel")),
        name="modulation",
    )(cond8, w_mod, b_mod.reshape(depth, 1, six_d))


def _rot_pairs(x, half, lane_mod_base=0):
    w = x.shape[-1]
    lane = lax.broadcasted_iota(jnp.int32, (1, w), 1) - lane_mod_base
    first = (lane % (2 * half)) < half
    return jnp.where(first, -pltpu.roll(x, w - half, axis=1), pltpu.roll(x, half, axis=1))


def _pre_kernel(x_ref, xp_ref, xn_ref, mod_ref, w_ref, ca_ref, sa_ref, cb_ref, sb_ref, ck_ref, sk_ref,
                qnorm_ref, kvnorm_ref, wuq_ref, cwg_ref, cbg_ref, rep_ref, mu_ref, dw0_ref, dw2_ref, da0_ref,
                da2_ref, dg2_ref, dkk_ref, dka_ref, drk_ref, bd_ref,
                aq_o, ak_o, av_o, bq_o, bckv_o, bkpe_o, cq4_o, ck4_o, claf_o, clab_o, cv_o, cgate_o,
                r_o, v_o, kk_o, lwf_o, lwb_o, kf_o, kb_o, af_o, ab_o, bonus_o, dgate_o):
    t = pl.program_id(0)
    tm = x_ref.shape[0]
    sh1 = mod_ref[0, 0:1, :]
    sc1 = mod_ref[0, 1:2, :]

    def modulate(xv):
        return (xv * (1.0 + sc1) + sh1).astype(BF16)

    z = jnp.dot(modulate(x_ref[...]), w_ref[...], preferred_element_type=F32)

    aq = z[:, _cs("aq")]
    ak = z[:, _cs("ak")]
    aq_o[...] = aq * ca_ref[...] + _rot_pairs(aq, A_HD // 4) * sa_ref[...]
    ak_o[...] = ak * ca_ref[:, :ak.shape[1]] + _rot_pairs(ak, A_HD // 4) * sa_ref[:, :ak.shape[1]]
    av_o[...] = z[:, _cs("av")]

    bcq = z[:, _cs("bcq")]
    qn = bcq * lax.rsqrt(jnp.mean(bcq * bcq, axis=-1, keepdims=True) + RMS_EPS) * qnorm_ref[...]
    bq = _dot1(qn, wuq_ref[...])
    bq_o[...] = bq * cb_ref[...] + _rot_pairs(bq, B_ROPE // 4, lane_mod_base=B_NOPE) * sb_ref[...]
    bckv = z[:, _cs("bckv")]
    bckv_o[...] = bckv * lax.rsqrt(jnp.mean(bckv * bckv, axis=-1, keepdims=True) + RMS_EPS) * kvnorm_ref[...]
    kpe_lo = COL["bkpe"][0] // 128 * 128
    kblk = z[:, kpe_lo:kpe_lo + 128]
    kblk = kblk * ck_ref[...] + _rot_pairs(kblk, B_ROPE // 4) * sk_ref[...]
    bkpe_o[...] = kblk[:, COL["bkpe"][0] - kpe_lo:COL["bkpe"][1] - kpe_lo]

    rep = rep_ref[...]
    cq4_o[...] = _dot_exact_rhs(z[:, _cs("cq")] * (C_DK ** -0.5), rep)
    ck4_o[...] = _dot_exact_rhs(z[:, _cs("ck")], rep)
    cv_o[...] = z[:, _cs("cv")]
    cog = z[:, _cs("cog")]
    cgate_o[...] = cog * _sigmoid(cog)
    for direction, (name, out) in enumerate((("caf", claf_o), ("cab", clab_o))):
        pre = _dot3(z[:, _cs(name)], cwg_ref[direction]) + cbg_ref[direction]
        out[...] = _dot_exact_rhs(-_softplus(-pre) * (1.0 / C_GATE_TEMP), rep)

    zd_cols = _cs("zd")
    zd = z[:, zd_cols]
    wd = w_ref[:, zd_cols]
    j = (t - CTX_TILES) % LAT_TILES_PER_SEQ
    latent = t >= CTX_TILES
    has_prev = jnp.logical_and(latent, j != 0)
    has_next = jnp.logical_and(latent, j != LAT_TILES_PER_SEQ - 1)
    zp = jnp.dot(modulate(xp_ref[...]), wd, preferred_element_type=F32)
    zn = jnp.dot(modulate(xn_ref[...]), wd, preferred_element_type=F32)
    prev_row = jnp.where(has_prev, zp[7:8], 0.0)
    next_row = jnp.where(has_next, zn[0:1], 0.0)
    row = lax.broadcasted_iota(jnp.int32, (tm, 1), 0)
    up = jnp.where(row == 0, prev_row, pltpu.roll(zd, 1, axis=0))
    dn = jnp.where(row == tm - 1, next_row, pltpu.roll(zd, tm - 1, axis=0))
    zd = zd + (0.5 * (up + dn) - zd) * mu_ref[...]

    hn = D_HEADS * D_N
    d_r, d_k, d_v = zd[:, :hn], zd[:, hn:2 * hn], zd[:, 2 * hn:3 * hn]
    o = 3 * hn
    d_w = (zd[:, o:o + D_DECAY_RANK], zd[:, o + D_DECAY_RANK:o + 2 * D_DECAY_RANK])
    o += 2 * D_DECAY_RANK
    d_a = (zd[:, o:o + D_AAA_RANK], zd[:, o + D_AAA_RANK:o + 2 * D_AAA_RANK])
    o += 2 * D_AAA_RANK
    d_g = zd[:, o:o + D_GATE_RANK]
    bd = bd_ref[...]
    kk = d_k * dkk_ref[...]
    kk = kk / jnp.maximum(jnp.sqrt(_dot_exact_rhs(kk * kk, bd)), 1e-12)
    r_o[...] = d_r
    v_o[...] = d_v
    kk_o[...] = kk
    k_sum = None
    for direction, (lw_o, k_o, a_o) in enumerate(((lwf_o, kf_o, af_o), (lwb_o, kb_o, ab_o))):
        w_log = -_softplus(-(dw0_ref[direction] + _dot3(jnp.tanh(d_w[direction]), dw2_ref[direction]))) - 0.5
        lw_o[...] = -jnp.exp(w_log)
        a = _sigmoid(da0_ref[direction] + _dot3(d_a[direction], da2_ref[direction]))
        k_dir = d_k * (1.0 + (a - 1.0) * dka_ref[...])
        k_o[...] = k_dir
        a_o[...] = a
        k_sum = k_dir if k_sum is None else k_sum + k_dir
    bonus_o[...] = d_v * _dot_exact_rhs(d_r * drk_ref[...] * k_sum, bd)
    dgate_o[...] = _dot3(_sigmoid(d_g), dg2_ref[...])


def _rope_tables():
    pos = np.arange(LAT_LEN)
    rowp, colp = (pos // GRID_W).astype(np.float32), (pos % GRID_W).astype(np.float32)

    f32 = np.float32

    def head_tables(rot_dim):
        quarter = rot_dim // 4
        inv = (f32(ROPE_BASE) ** (-np.arange(quarter, dtype=f32) / f32(quarter))).astype(f32)
        ar = (rowp[:, None] * inv).astype(f32)
        ac = (colp[:, None] * inv).astype(f32)
        cos = np.concatenate([np.cos(ar), np.cos(ar), np.cos(ac), np.cos(ac)], axis=-1).astype(f32)
        sin = np.concatenate([np.sin(ar), np.sin(ar), np.sin(ac), np.sin(ac)], axis=-1).astype(f32)
        return cos, sin

    def with_identity(c, s):
        w = c.shape[1]
        return (jnp.asarray(np.concatenate([np.ones((ROW_TILE, w), f32), c], axis=0)),
                jnp.asarray(np.concatenate([np.zeros((ROW_TILE, w), f32), s], axis=0)))

    ca, sa = head_tables(A_HD)
    ca, sa = with_identity(np.tile(ca, (1, A_HEADS)), np.tile(sa, (1, A_HEADS)))
    cbh, sbh = head_tables(B_ROPE)
    ones, zeros = np.ones((LAT_LEN, B_NOPE), f32), np.zeros((LAT_LEN, B_NOPE), f32)
    cb, sb = with_identity(np.tile(np.concatenate([ones, cbh], axis=1), (1, B_HEADS)),
                           np.tile(np.concatenate([zeros, sbh], axis=1), (1, B_HEADS)))
    pad = 128 - B_ROPE
    ck, sk = with_identity(np.concatenate([np.ones((LAT_LEN, pad), f32), cbh], axis=1),
                           np.concatenate([np.zeros((LAT_LEN, pad), f32), sbh], axis=1))
    return ca, sa, cb, sb, ck, sk


def _lane_repeat_matrix():
    m = np.zeros((C_HEADS * C_DK, C_HEADS * 128), np.float32)
    for h in range(C_HEADS):
        for g in range(128 // C_DK):
            for d in range(C_DK):
                m[h * C_DK + d, h * 128 + g * C_DK + d] = 1.0
    return jnp.asarray(m, BF16)


def _head_block_diag():
    m = np.kron(np.eye(D_HEADS, dtype=np.float32), np.ones((D_N, D_N), np.float32))
    return jnp.asarray(m, BF16)


PRE_OUT_WIDTHS = (256, 128, 128, 384, 128, 32, 512, 512, 512, 512, 256, 256) + (256,) * 11


def mixer_prelude(x, mod_l, w_small, tables, lp):
    tm = ROW_TILE
    tab_idx = lambda t: (jnp.where(t < CTX_TILES, 0, 1 + (t - CTX_TILES) % LAT_TILES_PER_SEQ), 0)
    hn = D_HEADS * D_N
    small = [lp['b_q_norm'].reshape(1, -1), lp['b_kv_norm'].reshape(1, -1), lp['b_w_uq'], lp['c_w_gate'],
             lp['c_b_gate'].reshape(2, 1, -1), _lane_repeat_matrix(), lp['d_mu'].reshape(1, -1),
             lp['d_w0'].reshape(2, 1, hn), lp['d_w2'], lp['d_a0'].reshape(2, 1, hn), lp['d_a2'], lp['d_g2'],
             lp['d_k_k'].reshape(1, hn), lp['d_k_a'].reshape(1, hn), lp['d_r_k'].reshape(1, hn), _head_block_diag()]
    in_specs = ([pl.BlockSpec((tm, D_MODEL), lambda t: (t, 0)),
                 pl.BlockSpec((8, D_MODEL), lambda t: (jnp.maximum(t * (tm // 8) - 1, 0), 0)),
                 pl.BlockSpec((8, D_MODEL), lambda t: (jnp.minimum((t + 1) * (tm // 8), N_TOK // 8 - 1), 0)),
                 pl.BlockSpec((1, 6, D_MODEL), lambda t: (_mod_row(t), 0, 0)),
                 _full(w_small.shape)]
                + [pl.BlockSpec((tm, tab.shape[1]), tab_idx) for tab in tables]
                + [_full(a.shape) for a in small])
    return pl.pallas_call(
        _pre_kernel,
        grid=(N_TILES,),
        in_specs=in_specs,
        out_specs=[pl.BlockSpec((tm, w), lambda t: (t, 0)) for w in PRE_OUT_WIDTHS],
        out_shape=[jax.ShapeDtypeStruct((N_TOK, w), F32) for w in PRE_OUT_WIDTHS],
        compiler_params=pltpu.CompilerParams(dimension_semantics=("parallel",), vmem_limit_bytes=VMEM_LIMIT),
        name="mixer_prelude",
    )(x, x, x, mod_l, w_small, *tables, *small)


ATT_Q_BLOCK = 128
ATT_WINDOW = 128
ATT_NEG_INF = -1e30
CACHE_LEN = 512


def _softmax_pv(s, v, sink):
    m = jnp.max(s, axis=-1, keepdims=True)
    if sink is not None:
        m = jnp.maximum(m, sink)
    e = jnp.exp(s - m)
    den = jnp.sum(e, axis=-1, keepdims=True)
    if sink is not None:
        den = den + jnp.exp(sink - m)
    return jnp.dot(e.astype(BF16), v.astype(BF16), preferred_element_type=F32) / den


def _gqa_kernel(sink_ref, q_ref, k_ref, v_ref, *rest, hd, group, scale, windowed):
    if windowed:
        kp_ref, kn_ref, vp_ref, vn_ref, kc_ref, vc_ref, _, o_ref = rest
    else:
        (o_ref,) = rest
    i = pl.program_id(1)
    tq = q_ref.shape[0]
    n_kv = k_ref.shape[1] // hd
    if windowed:
        qpos = i * tq + lax.broadcasted_iota(jnp.int32, (tq, 3 * tq), 0)
        kpos = (i - 1) * tq + lax.broadcasted_iota(jnp.int32, (tq, 3 * tq), 1)
        n_tok = pl.num_programs(1) * tq
        mask = (jnp.abs(qpos - kpos) <= ATT_WINDOW) & (kpos >= 0) & (kpos < n_tok)
        mask = jnp.concatenate([mask] * group, axis=0)
    for kvh in range(n_kv):
        ks = slice(kvh * hd, (kvh + 1) * hd)
        qs = [q_ref[:, (kvh * group + g) * hd:(kvh * group + g + 1) * hd] for g in range(group)]
        q = (jnp.concatenate(qs, axis=0) * scale).astype(BF16)
        sink = jnp.concatenate(
            [jnp.full((tq, 1), sink_ref[kvh * group + g], F32) for g in range(group)], axis=0)
        if windowed:
            k_win = jnp.concatenate([kp_ref[:, ks], k_ref[:, ks], kn_ref[:, ks]], axis=0)
            v_win = jnp.concatenate([vp_ref[:, ks], v_ref[:, ks], vn_ref[:, ks]], axis=0)
            s_win = _bdot(q, k_win.astype(BF16), _D2T)
            s_win = jnp.where(mask, s_win, ATT_NEG_INF)
            s_ctx = _bdot(q, kc_ref[0, 0, kvh].astype(BF16), _D2T)
            s = jnp.concatenate([s_win, s_ctx], axis=1)
            v = jnp.concatenate([v_win, vc_ref[0, 0, kvh]], axis=0)
        else:
            s = _bdot(q, k_ref[:, ks].astype(BF16), _D2T)
            v = v_ref[:, ks]
        o = _softmax_pv(s, v, sink)
        for g in range(group):
            h = kvh * group + g
            o_ref[:, h * hd:(h + 1) * hd] = o[g * tq:(g + 1) * tq]


def gqa_attention(q, k, v, sink, cache_k, cache_v, layer):
    qw, kw = q.shape[1], k.shape[1]
    group = qw // kw
    scale = A_HD ** -0.5
    params = pltpu.CompilerParams(dimension_semantics=("parallel", "parallel"))
    out_shape = jax.ShapeDtypeStruct((N_TOK, qw), F32)
    ctx_spec = lambda w: pl.BlockSpec((CTX_LEN, w), lambda s, i, sk: (s, 0))
    o = pl.pallas_call(
        functools.partial(_gqa_kernel, hd=A_HD, group=group, scale=scale, windowed=False),
        grid_spec=pltpu.PrefetchScalarGridSpec(
            num_scalar_prefetch=1, grid=(N_CTX_SEQ, 1), in_specs=[ctx_spec(qw), ctx_spec(kw), ctx_spec(kw)],
            out_specs=ctx_spec(qw)),
        out_shape=out_shape, compiler_params=params, name="gqa_full",
    )(sink, q, k, v)
    tq = ATT_Q_BLOCK
    nb = LAT_LEN // tq
    base = N_CTX // tq
    blk = lambda w, f: pl.BlockSpec((tq, w), lambda b, i, sk: (base + nb * b + f(i), 0))
    same = lambda i: i
    prev = lambda i: jnp.maximum(i - 1, 0)
    nxt = lambda i: jnp.minimum(i + 1, nb - 1)
    cspec = pl.BlockSpec((1, 1) + cache_k.shape[2:], lambda b, i, sk: (b, layer, 0, 0, 0))
    return pl.pallas_call(
        functools.partial(_gqa_kernel, hd=A_HD, group=group, scale=scale, windowed=True),
        grid_spec=pltpu.PrefetchScalarGridSpec(
            num_scalar_prefetch=1, grid=(N_LAT_SEQ, nb),
            in_specs=[blk(qw, same), blk(kw, same), blk(kw, same), blk(kw, prev), blk(kw, nxt), blk(kw, prev),
                      blk(kw, nxt), cspec, cspec, pl.BlockSpec(memory_space=pl.ANY)],
            out_specs=blk(qw, same)),
        out_shape=out_shape, input_output_aliases={10: 0}, compiler_params=params, name="gqa_windowed",
    )(sink, q, k, v, k, k, v, v, cache_k, cache_v, o)


def _mla_kernel(q_ref, ckv_ref, kpe_ref, wukv_ref, *rest, n_heads, nope, rope, vd, scale, cached):
    if cached:
        cckv_ref, ckpe_ref, _, o_ref, kv_scr, kpe_scr = rest
    else:
        o_ref, kv_scr, kpe_scr = rest
    i = pl.program_id(1)
    n_cache = kv_scr.shape[0] - ckv_ref.shape[0]

    @pl.when(i == 0)
    def _():
        w = wukv_ref[...].astype(BF16)
        if cached:
            kv_scr[:n_cache] = jnp.dot(cckv_ref[0, 0].astype(BF16), w, preferred_element_type=F32).astype(BF16)
            kpe_scr[:n_cache] = ckpe_ref[0, 0].astype(BF16)
        kv_scr[n_cache:] = jnp.dot(ckv_ref[...].astype(BF16), w, preferred_element_type=F32).astype(BF16)
        kpe_scr[n_cache:] = kpe_ref[...].astype(BF16)

    kpe = kpe_scr[...]
    qd = nope + rope
    for h in range(n_heads):
        qn = (q_ref[:, h * qd:h * qd + nope] * scale).astype(BF16)
        qp = (q_ref[:, h * qd + nope:(h + 1) * qd] * scale).astype(BF16)
        k_n = kv_scr[:, h * (nope + vd):h * (nope + vd) + nope]
        v = kv_scr[:, h * (nope + vd) + nope:(h + 1) * (nope + vd)]
        s = _bdot(qn, k_n, _D2T) + _bdot(qp, kpe, _D2T)
        o_ref[:, h * vd:(h + 1) * vd] = _softmax_pv(s, v, None)


def mla_attention(q, ckv, kpe, w_ukv, cache_ckv, cache_kpe, layer):
    qw = q.shape[1]
    tq = ATT_Q_BLOCK
    kw = dict(n_heads=B_HEADS, nope=B_NOPE, rope=B_ROPE, vd=B_VD, scale=(B_NOPE + B_ROPE) ** -0.5)
    params = pltpu.CompilerParams(dimension_semantics=("parallel", "arbitrary"))
    out_shape = jax.ShapeDtypeStruct((N_TOK, B_HEADS * B_VD), F32)
    kvw = B_HEADS * (B_NOPE + B_VD)
    nbc = CTX_LEN // tq
    o = pl.pallas_call(
        functools.partial(_mla_kernel, cached=False, **kw),
        grid=(N_CTX_SEQ, nbc),
        in_specs=[pl.BlockSpec((tq, qw), lambda s, i: (s * nbc + i, 0)),
                  pl.BlockSpec((CTX_LEN, B_KV_LORA), lambda s, i: (s, 0)),
                  pl.BlockSpec((CTX_LEN, B_ROPE), lambda s, i: (s, 0)),
                  _full(w_ukv.shape)],
        out_specs=pl.BlockSpec((tq, B_HEADS * B_VD), lambda s, i: (s * nbc + i, 0)),
        out_shape=out_shape,
        scratch_shapes=[pltpu.VMEM((CTX_LEN, kvw), BF16), pltpu.VMEM((CTX_LEN, B_ROPE), BF16)],
        compiler_params=params, name="mla_context",
    )(q, ckv, kpe, w_ukv)
    nb = LAT_LEN // tq
    base = N_CTX // tq
    lat0 = N_CTX // LAT_LEN
    s_len = CACHE_LEN + LAT_LEN
    return pl.pallas_call(
        functools.partial(_mla_kernel, cached=True, **kw),
        grid=(N_LAT_SEQ, nb),
        in_specs=[pl.BlockSpec((tq, qw), lambda b, i: (base + nb * b + i, 0)),
                  pl.BlockSpec((LAT_LEN, B_KV_LORA), lambda b, i: (lat0 + b, 0)),
                  pl.BlockSpec((LAT_LEN, B_ROPE), lambda b, i: (lat0 + b, 0)),
                  _full(w_ukv.shape),
                  pl.BlockSpec((1, 1, CACHE_LEN, B_KV_LORA), lambda b, i: (b, layer, 0, 0)),
                  pl.BlockSpec((1, 1, CACHE_LEN, B_ROPE), lambda b, i: (b, layer, 0, 0)),
                  pl.BlockSpec(memory_space=pl.ANY)],
        out_specs=pl.BlockSpec((tq, B_HEADS * B_VD), lambda b, i: (base + nb * b + i, 0)),
        out_shape=out_shape, input_output_aliases={6: 0},
        scratch_shapes=[pltpu.VMEM((s_len, kvw), BF16), pltpu.VMEM((s_len, B_ROPE), BF16)],
        compiler_params=params, name="mla_latent",
    )(q, ckv, kpe, w_ukv, cache_ckv, cache_kpe, o)


CHUNK = 64
GLA_SUB = 16
PAIR = 2
N_CHAIN = PAIR * 2 * 4


def _is_back(shape):
    return (lax.broadcasted_iota(jnp.int32, shape, 0) // 4) % 2 == 1


def _chains(ref_f, ref_b, width):
    return jnp.stack([ref[0, s, 0, :, h * width:(h + 1) * width]
                      for s in range(PAIR) for ref in (ref_f, ref_b) for h in range(4)], axis=0)


def _unchain(y, o_f, o_b):
    for s in range(PAIR):
        o_f[0, s, 0] = jnp.concatenate([y[s * 8 + h] for h in range(4)], axis=-1)
        o_b[0, s, 0] = jnp.concatenate([y[s * 8 + 4 + h] for h in range(4)], axis=-1)


def _dir_masks(L):
    shape = (N_CHAIN, L, L)
    back = _is_back(shape)
    row = lax.broadcasted_iota(jnp.int32, shape, 1)
    col = lax.broadcasted_iota(jnp.int32, shape, 2)
    ahead = jnp.where(back, col - row, row - col)
    return ahead >= 0, ahead > 0, row == col


def _chunk_end(ci):
    L = ci.shape[1]
    return jnp.where(_is_back((N_CHAIN, 1, 1)), ci[:, 0:1], ci[:, L - 1:L])


def _split_refs(refs, n_in, has_s0, has_sfin):
    ins = refs[:n_in]
    pos = n_in
    s0_ref = None
    if has_s0:
        s0_ref = refs[pos]
        pos += 3
    of_ref, ob_ref = refs[pos], refs[pos + 1]
    pos += 2
    sfin_ref = refs[pos] if has_sfin else None
    return ins, s0_ref, of_ref, ob_ref, sfin_ref, refs[-1]


def _init_state(s_scr, s0_ref):
    @pl.when(pl.program_id(1) == 0)
    def _():
        if s0_ref is None:
            s_scr[...] = jnp.zeros_like(s_scr)
        else:
            s_scr[...] = s0_ref[0]


def _emit_state(sfin_ref, s_new):
    if sfin_ref is None:
        return

    @pl.when(pl.program_id(1) == pl.num_programs(1) - 1)
    def _():
        sfin_ref[0] = s_new


def _rwkv_kernel(*refs, dot, has_s0, has_sfin):
    (rf, rb, vf, vb, kkf, kkb, lwf, lwb, kf, kb, af, ab), s0_ref, yf_ref, yb_ref, sfin_ref, s_scr = _split_refs(
        refs, 12, has_s0, has_sfin)
    _init_state(s_scr, s0_ref)
    n = D_N
    r = _chains(rf, rb, n)
    v = _chains(vf, vb, n)
    kk = _chains(kkf, kkb, n)
    lw = _chains(lwf, lwb, n)
    k = _chains(kf, kb, n)
    a = _chains(af, ab, n)
    L = r.shape[1]
    S = s_scr[...]
    incl, strict, diag = _dir_masks(L)
    ci = _dot_exact_lhs(jnp.where(incl, 1.0, 0.0), lw, _NN)
    ce = ci - lw
    cl = _chunk_end(ci)
    e_neg = jnp.exp(-ci)
    b = a * kk
    alpha = kk * jnp.exp(ce)
    rho = r * jnp.exp(ci)
    beta = b * e_neg
    kappa = k * e_neg
    e_end = jnp.exp(cl - ci)
    ar = jnp.concatenate([alpha, rho], axis=1)
    bk = jnp.concatenate([beta, kappa], axis=1)
    w = dot(ar, bk, _NT)
    nmat = jnp.where(strict, w[:, :L, :L], 0.0)
    mmat = jnp.where(strict, w[:, :L, L:], 0.0)
    p1 = jnp.where(incl, w[:, L:, :L], 0.0)
    p2 = jnp.where(incl, w[:, L:, L:], 0.0)
    x = jnp.where(diag, 1.0, 0.0) - nmat
    p = dot(nmat, nmat, _NN)
    span = 2
    while True:
        x = x + dot(x, p, _NN)
        span *= 2
        if span >= L:
            break
        p = dot(p, p, _NN)
    us = dot(ar, S, _NT)
    rhs = us[:, :L] + dot(mmat, v, _NN)
    d = -dot(x, rhs, _NN)
    dv = jnp.concatenate([d, v], axis=1)
    pp = jnp.concatenate([p1, p2], axis=2)
    _unchain(us[:, L:] + dot(pp, dv, _NN), yf_ref, yb_ref)
    bk_end = jnp.concatenate([b * e_end, k * e_end], axis=1)
    s_new = S * jnp.exp(cl) + dot(dv, bk_end, _TN)
    s_scr[...] = s_new
    _emit_state(sfin_ref, s_new)


def _gla_kernel(*refs, dot, has_s0, has_sfin):
    (qf, qb, kf, kb, vf, vb, laf, lab), s0_ref, of_ref, ob_ref, sfin_ref, s_scr = _split_refs(
        refs, 8, has_s0, has_sfin)
    _init_state(s_scr, s0_ref)
    q4 = _chains(qf, qb, 128)
    k4 = _chains(kf, kb, 128)
    la4 = _chains(laf, lab, 128)
    v = _chains(vf, vb, C_DV)
    g, L, lanes = q4.shape
    dk = C_DK
    n_sub = L // GLA_SUB
    st = s_scr[...]
    incl, _, _ = _dir_masks(L)
    c = _dot_exact_lhs(jnp.where(incl, 1.0, 0.0), la4, _NN)
    shape = (g, L, lanes)
    back = _is_back(shape)
    lane_blk = lax.broadcasted_iota(jnp.int32, shape, 2) // dk
    row_blk = lax.broadcasted_iota(jnp.int32, shape, 1) // GLA_SUB
    cref_f = jnp.zeros(shape, F32)
    cref_b = jnp.zeros(shape, F32)
    for j in range(1, n_sub):
        cref_f = jnp.where(lane_blk == j, c[:, j * GLA_SUB - 1:j * GLA_SUB], cref_f)
        cref_b = jnp.where(lane_blk == j - 1, c[:, j * GLA_SUB:j * GLA_SUB + 1], cref_b)
    cref = jnp.where(back, cref_b, cref_f)
    q_on = row_blk == lane_blk
    k_on = jnp.where(back, row_blk - lane_blk, lane_blk - row_blk) >= 0
    qh = jnp.where(q_on, q4 * jnp.exp(jnp.where(q_on, c - cref, 0.0)), 0.0)
    kh = jnp.where(k_on, k4 * jnp.exp(jnp.where(k_on, cref - c, 0.0)), 0.0)
    att = jnp.where(incl, dot(qh, kh, _NT), 0.0)
    cl = _chunk_end(c)
    qe = (q4 * jnp.exp(c))[:, :, :dk]
    ke = (k4 * jnp.exp(cl - c))[:, :, :dk]
    _unchain(dot(qe, st, _NT) + dot(att, v, _NN), of_ref, ob_ref)
    s_new = st * jnp.exp(cl[:, :, :dk]) + dot(v, ke, _TN)
    s_scr[...] = s_new
    _emit_state(sfin_ref, s_new)


def _recurrence_calls(kernel_fn, name, pairs, singles_f, singles_b, s0_lat, state_dims, out_width):
    def run(view, grid, group, s0, prev_out):
        nc = view[2]
        fwd_map = lambda p, c: (group(p), 0, c, 0, 0)
        bwd_map = lambda p, c: (group(p), 0, nc - 1 - c, 0, 0)
        blk = lambda w: (1, PAIR, 1, CHUNK, w)
        args, in_specs = [], []
        for af, ab in [(a, a) for a in pairs] + list(zip(singles_f, singles_b)):
            w = af.shape[-1]
            args += [af.reshape(view + (w,)), ab.reshape(view + (w,))]
            in_specs += [pl.BlockSpec(blk(w), fwd_map), pl.BlockSpec(blk(w), bwd_map)]
        out_specs = [pl.BlockSpec(blk(out_width), fwd_map), pl.BlockSpec(blk(out_width), bwd_map)]
        out_shape = [jax.ShapeDtypeStruct(view + (out_width,), F32)] * 2
        aliases = {}
        if s0 is not None:
            args += [s0] + [o.reshape(view + (out_width,)) for o in prev_out]
            in_specs += [_full(s0.shape), pl.BlockSpec(memory_space=pl.ANY), pl.BlockSpec(memory_space=pl.ANY)]
            aliases = {len(args) - 2: 0, len(args) - 1: 1}
        else:
            out_specs.append(pl.BlockSpec((1, N_CHAIN) + state_dims, lambda p, c: (p, 0, 0, 0)))
            out_shape.append(jax.ShapeDtypeStruct((grid[0], N_CHAIN) + state_dims, F32))
        return pl.pallas_call(
            functools.partial(kernel_fn, has_s0=s0 is not None, has_sfin=s0 is None),
            grid=grid, in_specs=in_specs, out_specs=out_specs, out_shape=out_shape,
            input_output_aliases=aliases, scratch_shapes=[pltpu.VMEM((N_CHAIN,) + state_dims, F32)],
            compiler_params=pltpu.CompilerParams(dimension_semantics=("parallel", "arbitrary")),
            name=name + ("_latent" if s0 is not None else "_context"),
        )(*args)

    ctx_nc = CTX_LEN // CHUNK
    ctx_view = (N_TOK // (PAIR * CTX_LEN), PAIR, ctx_nc, CHUNK)
    o_f, o_b, s_fin = run(ctx_view, (N_CTX_SEQ // PAIR, ctx_nc), lambda p: p, None, None)
    lat_nc = LAT_LEN // CHUNK
    lat_view = (N_TOK // (PAIR * LAT_LEN), PAIR, lat_nc, CHUNK)
    o_f, o_b = run(lat_view, (1, lat_nc), lambda p: N_CTX // (PAIR * LAT_LEN), s0_lat, (o_f, o_b))
    return o_f.reshape(N_TOK, out_width), o_b.reshape(N_TOK, out_width), s_fin


def _layer_norm(x, g, b):
    mu = jnp.mean(x, axis=-1, keepdims=True)
    xc = x - mu
    var = jnp.mean(xc * xc, axis=-1, keepdims=True)
    return xc * lax.rsqrt(var + LN_EPS) * g + b


def _merge_kernel(x_ref, mod_ref, oa_ref, ob_ref, cof_ref, cob_ref, cgate_ref, yf_ref, yb_ref, bonus_ref, dgate_ref,
                  wg_ref, wbr_ref, wout_ref, cnorm_ref, dlng_ref, dlnb_ref, lng_ref, lnb_ref, wr_ref, br_ref, bd_ref,
                  x1_o, h2_o, topi_o, topw_o):
    x = x_ref[...]
    m = mod_ref[0]
    sh1, sc1, g1, sh2, sc2 = m[0:1], m[1:2], m[2:3], m[3:4], m[4:5]
    bd = bd_ref[...]
    inv_n = 1.0 / D_N
    co = cof_ref[...] + cob_ref[...]
    o_c = co * lax.rsqrt(_dot_exact_rhs(co * co, bd) * inv_n + RMS_EPS) * cnorm_ref[...] * cgate_ref[...]
    y = yf_ref[...] + yb_ref[...]
    yc = y - _dot_exact_rhs(y, bd) * inv_n
    var = _dot_exact_rhs(yc * yc, bd) * inv_n
    o_d = (yc * lax.rsqrt(var + D_GN_EPS) * dlng_ref[...] + dlnb_ref[...] + bonus_ref[...]) * dgate_ref[...]
    branches = (oa_ref[...], ob_ref[...], o_c, o_d)
    h = (x * (1.0 + sc1) + sh1).astype(BF16)
    merged = None
    for n in range(N_BRANCH):
        gate = _sigmoid(jnp.dot(h, wg_ref[:, n * D_MODEL:(n + 1) * D_MODEL], preferred_element_type=F32))
        term = gate * jnp.dot(branches[n].astype(BF16), wbr_ref[n], preferred_element_type=F32)
        merged = term if merged is None else merged + term
    mix = jnp.dot(merged.astype(BF16), wout_ref[...], preferred_element_type=F32)
    x1 = _layer_norm(ALPHA * x + g1 * mix, lng_ref[...], lnb_ref[...])
    x1_o[...] = x1
    h2 = x1 * (1.0 + sc2) + sh2
    h2_o[...] = h2.astype(BF16)
    logits = _dot6(h2, wr_ref[...]) + br_ref[...]
    tm, n_e = logits.shape
    lane_e = lax.broadcasted_iota(jnp.int32, (tm, n_e), 1)
    lane_o = lax.broadcasted_iota(jnp.int32, (tm, topi_o.shape[1]), 1)
    top_i = jnp.zeros((tm, topi_o.shape[1]), jnp.int32)
    top_v = jnp.zeros((tm, topw_o.shape[1]), F32)
    vals = []
    for kth in range(TOP_K):
        mx = jnp.max(logits, axis=-1, keepdims=True)
        idx = jnp.min(jnp.where(logits == mx, lane_e, n_e), axis=-1, keepdims=True)
        vals.append(mx)
        top_i = jnp.where(lane_o == kth, idx, top_i)
        logits = jnp.where(lane_e == idx, -jnp.inf, logits)
    es = [jnp.exp(vk - vals[0]) for vk in vals]
    den = es[0] + es[1] + es[2] + es[3]
    for kth in range(TOP_K):
        top_v = jnp.where(lane_o == kth, es[kth] / den, top_v)
    topi_o[...] = top_i
    topw_o[...] = top_v


def merge_and_route(x, mod_l, o_a, o_b, co_f, co_b, cgate, y_f, y_b, bonus, dgate, w_g, w_br, w_out, lp):
    tm = ROW_TILE
    hn = D_HEADS * D_N
    row = lambda w: pl.BlockSpec((tm, w), lambda t: (t, 0))
    small = [jnp.tile(lp['c_norm'], C_HEADS).reshape(1, hn), lp['d_ln_g'].reshape(1, hn), lp['d_ln_b'].reshape(1, hn),
             lp['ln_g'][0].reshape(1, -1), lp['ln_b'][0].reshape(1, -1), lp['w_router'],
             lp['b_router'].reshape(1, -1), _head_block_diag()]
    return pl.pallas_call(
        _merge_kernel,
        grid=(N_TILES,),
        in_specs=([row(D_MODEL), pl.BlockSpec((1, 6, D_MODEL), lambda t: (_mod_row(t), 0, 0))]
                  + [row(hn)] * 9 + [_full(w_g.shape), _full(w_br.shape), _full(w_out.shape)]
                  + [_full(a.shape) for a in small]),
        out_specs=[row(D_MODEL), row(D_MODEL), row(128), row(128)],
        out_shape=[jax.ShapeDtypeStruct((N_TOK, D_MODEL), F32), jax.ShapeDtypeStruct((N_TOK, D_MODEL), BF16),
                   jax.ShapeDtypeStruct((N_TOK, 128), jnp.int32), jax.ShapeDtypeStruct((N_TOK, 128), F32)],
        compiler_params=pltpu.CompilerParams(dimension_semantics=("parallel",), vmem_limit_bytes=VMEM_LIMIT),
        name="merge_and_route",
    )(x, mod_l, o_a, o_b, co_f, co_b, cgate, y_f, y_b, bonus, dgate, w_g, w_br, w_out, *small)


MOE_ROW_TILE = 256


def _moe_kernel(te_ref, tv_ref, x_ref, w1_ref, b1_ref, w2_ref, b2_ref, perm_ref, y_ref, w1s, w2s, hs):
    t = pl.program_id(0)
    e = te_ref[t]
    prev = te_ref[jnp.maximum(t - 1, 0)]
    new_expert = jnp.logical_or(t == 0, e != prev)
    valid = tv_ref[t] != 0
    d_model, two_f = w1s.shape
    n_blk = two_f // MXU_WIDTH
    half = MXU_WIDTH // 2

    @pl.when(new_expert)
    def _():
        for blk in range(n_blk):
            sl = slice(blk * MXU_WIDTH, (blk + 1) * MXU_WIDTH)
            wb = w1_ref[0, 0, :, sl].astype(BF16)
            w1s[:, sl] = jnp.dot(wb, perm_ref[...], preferred_element_type=F32).astype(BF16)
        w2s[...] = w2_ref[0, 0].astype(BF16)

    @pl.when(valid)
    def _():
        x = x_ref[...]
        for blk in range(n_blk):
            sl = slice(blk * MXU_WIDTH, (blk + 1) * MXU_WIDTH)
            u = jnp.dot(x, w1s[:, sl], preferred_element_type=F32) + b1_ref[0, 0, :, sl]
            glu = jnp.minimum(u[:, :half], SWIGLU_LIMIT)
            lin = jnp.clip(u[:, half:], -SWIGLU_LIMIT, SWIGLU_LIMIT)
            hs[:, blk * half:(blk + 1) * half] = (glu * _sigmoid(SWIGLU_ALPHA * glu) * (lin + 1.0)).astype(BF16)
        y = jnp.dot(hs[...], w2s[...], preferred_element_type=F32) + b2_ref[0, 0]
        y_ref[...] = y.astype(y_ref.dtype)

    @pl.when(jnp.logical_not(valid))
    def _():
        y_ref[...] = jnp.zeros_like(y_ref)


def _deinterleave_perm():
    half = MXU_WIDTH // 2
    src = np.arange(MXU_WIDTH)
    dst = np.where(src % 2 == 0, src // 2, half + src // 2)
    p = np.zeros((MXU_WIDTH, MXU_WIDTH), np.float32)
    p[src, dst] = 1.0
    return jnp.asarray(p, BF16)


def _moe_dispatch(top_i):
    n, k = top_i.shape
    tm = MOE_ROW_TILE
    p_rows = n * k + N_EXPERTS * tm
    sel = jnp.sum((top_i[:, :, None] == jnp.arange(N_EXPERTS, dtype=jnp.int32)).astype(jnp.int32), axis=1)
    before = jnp.cumsum(sel, axis=0) - sel
    rank = jnp.take_along_axis(before, top_i, axis=1)
    counts = jnp.sum(sel, axis=0)
    padded = ((counts + tm - 1) // tm) * tm
    ends = jnp.cumsum(padded)
    starts = ends - padded
    pos = starts[top_i] + rank
    n_tiles = p_rows // tm
    tile_start = jnp.arange(n_tiles, dtype=jnp.int32) * tm
    tile_valid = (tile_start < ends[-1]).astype(jnp.int32)
    last_tile = ends[-1] // tm - 1
    tile_expert = jnp.sum(ends[None, :] <= jnp.minimum(tile_start, last_tile * tm)[:, None], axis=1).astype(jnp.int32)
    keys = jnp.sort((top_i * n + jnp.arange(n, dtype=jnp.int32)[:, None]).reshape(-1))
    row_e = jnp.repeat(tile_expert, tm)
    row_rank = jnp.arange(p_rows, dtype=jnp.int32) - starts[row_e]
    sorted_at = jnp.clip((jnp.cumsum(counts) - counts)[row_e] + row_rank, 0, n * k - 1)
    src_tok = jnp.where(row_rank < counts[row_e], keys[sorted_at] % n, 0)
    return pos, src_tok, tile_expert, tile_valid, p_rows


def moe_experts(h2, top_i, layer, w1, b1, w2, b2):
    n, d = h2.shape
    depth, e, _, two_f = w1.shape
    f = two_f // 2
    tm = MOE_ROW_TILE
    pos, src_tok, tile_expert, tile_valid, p_rows = _moe_dispatch(top_i)
    xs = h2[src_tok]
    b1p = b1.reshape(depth, e, two_f // MXU_WIDTH, MXU_WIDTH // 2, 2).swapaxes(3, 4).reshape(depth, e, 1, two_f)
    grid_spec = pltpu.PrefetchScalarGridSpec(
        num_scalar_prefetch=2,
        grid=(p_rows // tm,),
        in_specs=[
            pl.BlockSpec((tm, d), lambda t, te, tv: (t, 0)),
            pl.BlockSpec((1, 1, d, two_f), lambda t, te, tv: (layer, te[t], 0, 0)),
            pl.BlockSpec((1, 1, 1, two_f), lambda t, te, tv: (layer, te[t], 0, 0)),
            pl.BlockSpec((1, 1, f, d), lambda t, te, tv: (layer, te[t], 0, 0)),
            pl.BlockSpec((1, 1, 1, d), lambda t, te, tv: (layer, te[t], 0, 0)),
            pl.BlockSpec((MXU_WIDTH, MXU_WIDTH), lambda t, te, tv: (0, 0)),
        ],
        out_specs=pl.BlockSpec((tm, d), lambda t, te, tv: (t, 0)),
        scratch_shapes=[pltpu.VMEM((d, two_f), BF16), pltpu.VMEM((f, d), BF16), pltpu.VMEM((tm, f), BF16)],
    )
    ys = pl.pallas_call(
        _moe_kernel,
        grid_spec=grid_spec,
        out_shape=jax.ShapeDtypeStruct((p_rows, d), BF16),
        compiler_params=pltpu.CompilerParams(dimension_semantics=("arbitrary",),
                                             vmem_limit_bytes=48 * 1024 * 1024),
        name="moe_experts",
    )(tile_expert, tile_valid, xs, w1, b1p, w2, b2.reshape(depth, e, 1, d), _deinterleave_perm())
    return ys[pos.T.reshape(-1)].reshape(TOP_K, n, d)


def _final_kernel(x1_ref, mod_ref, ys_ref, topw_ref, lng_ref, lnb_ref, o_ref):
    g2 = mod_ref[0, 5:6]
    moe = None
    for kth in range(TOP_K):
        term = ys_ref[kth].astype(F32) * topw_ref[:, kth:kth + 1]
        moe = term if moe is None else moe + term
    o_ref[...] = _layer_norm(ALPHA * x1_ref[...] + g2 * moe, lng_ref[...], lnb_ref[...])


def combine_and_norm(x1, mod_l, ys, top_w, ln_g, ln_b):
    tm = ROW_TILE
    return pl.pallas_call(
        _final_kernel,
        grid=(N_TILES,),
        in_specs=[pl.BlockSpec((tm, D_MODEL), lambda t: (t, 0)),
                  pl.BlockSpec((1, 6, D_MODEL), lambda t: (_mod_row(t), 0, 0)),
                  pl.BlockSpec((TOP_K, tm, D_MODEL), lambda t: (0, t, 0)),
                  pl.BlockSpec((tm, 128), lambda t: (t, 0)),
                  _full((1, D_MODEL)), _full((1, D_MODEL))],
        out_specs=pl.BlockSpec((tm, D_MODEL), lambda t: (t, 0)),
        out_shape=jax.ShapeDtypeStruct((N_TOK, D_MODEL), F32),
        compiler_params=pltpu.CompilerParams(dimension_semantics=("parallel",)),
        name="combine_and_norm",
    )(x1, mod_l, ys, top_w, ln_g.reshape(1, -1), ln_b.reshape(1, -1))


def kernel(x_prompt, x_sample, cache_a_k, cache_a_v, cache_b_ckv, cache_b_kpe, state_c, state_d, c,
           c_ctx, w_mod, b_mod, w_in, a_sink, b_q_norm, b_w_uq, b_kv_norm, b_w_ukv, c_w_gate, c_b_gate,
           c_norm, d_mu, d_w0, d_w2, d_a0, d_a2, d_g2, d_k_k, d_k_a, d_r_k, d_ln_g, d_ln_b, w_br, w_out,
           ln_g, ln_b, w_router, b_router, w_mlp1, b_mlp1, w_mlp2, b_mlp2):
    params = dict(b_q_norm=b_q_norm, b_w_uq=b_w_uq, b_kv_norm=b_kv_norm, b_w_ukv=b_w_ukv, c_w_gate=c_w_gate,
                  c_b_gate=c_b_gate, c_norm=c_norm, d_mu=d_mu, d_w0=d_w0, d_w2=d_w2, d_a0=d_a0, d_a2=d_a2, d_g2=d_g2,
                  d_k_k=d_k_k, d_k_a=d_k_a, d_r_k=d_r_k, d_ln_g=d_ln_g, d_ln_b=d_ln_b, ln_g=ln_g, ln_b=ln_b,
                  w_router=w_router, b_router=b_router)
    assert x_prompt.shape == (N_CTX_SEQ, CTX_LEN, D_MODEL) and x_sample.shape == (N_LAT_SEQ, LAT_LEN, D_MODEL)
    x = jnp.concatenate([x_prompt.reshape(N_CTX, D_MODEL), x_sample.reshape(-1, D_MODEL)], axis=0)
    cond8 = jnp.concatenate([c_ctx[None], c, jnp.zeros((8 - 1 - N_LAT_SEQ, D_MODEL), F32)], axis=0)
    mod = modulation_table(cond8, w_mod, b_mod)[:, :1 + N_LAT_SEQ].reshape(DEPTH, 1 + N_LAT_SEQ, 6, D_MODEL)
    tables = _rope_tables()
    new = {name: [] for name in ("a_k", "a_v", "b_ckv", "b_kpe", "c", "d")}
    for l in range(DEPTH):
        lp = {name: val[l] for name, val in params.items()}
        w_small = jnp.concatenate([w_in[l][:, _ORIG[nm][0]:_ORIG[nm][1]] for nm in _ORDER], axis=1).astype(BF16)
        w_g = w_in[l][:, G_START:].astype(BF16)
        w_br_l = w_br[l].astype(BF16)
        w_out_l = w_out[l].astype(BF16)

        (aq, ak, av, bq, bckv, bkpe, cq4, ck4, cla_f, cla_b, cv, cgate,
         r, v, kk, lw_f, lw_b, k_f, k_b, a_f, a_b, bonus, dgate) = mixer_prelude(x, mod[l], w_small, tables, lp)

        o_a = gqa_attention(aq, ak, av, a_sink[l], cache_a_k, cache_a_v, l)
        o_b = mla_attention(bq, bckv, bkpe, b_w_ukv[l], cache_b_ckv, cache_b_kpe, l)

        c_s0 = jnp.swapaxes(state_c[:, l], 3, 4).reshape(1, N_CHAIN, C_DV, C_DK)
        co_f, co_b, c_fin = _recurrence_calls(functools.partial(_gla_kernel, dot=_dot1), "gla", [cq4, ck4, cv],
                                              [cla_f], [cla_b], c_s0, (C_DV, C_DK), C_HEADS * C_DV)
        d_s0 = state_d[:, l].reshape(1, N_CHAIN, D_N, D_N)
        y_f, y_b, d_fin = _recurrence_calls(functools.partial(_rwkv_kernel, dot=_dot1), "rwkv7", [r, v, kk],
                                            [lw_f, k_f, a_f], [lw_b, k_b, a_b], d_s0, (D_N, D_N), D_HEADS * D_N)

        x1, h2, top_i, top_w = merge_and_route(x, mod[l], o_a, o_b, co_f, co_b, cgate, y_f, y_b, bonus, dgate,
                                               w_g, w_br_l, w_out_l, lp)
        ys = moe_experts(h2, top_i[:, :TOP_K], l, w_mlp1, b_mlp1, w_mlp2, b_mlp2)
        x = combine_and_norm(x1, mod[l], ys, top_w, ln_g[l, 1], ln_b[l, 1])

        new["a_k"].append(ak[:N_CTX].reshape(N_CTX_SEQ, CTX_LEN, A_KV_HEADS, A_HD).transpose(0, 2, 1, 3))
        new["a_v"].append(av[:N_CTX].reshape(N_CTX_SEQ, CTX_LEN, A_KV_HEADS, A_HD).transpose(0, 2, 1, 3))
        new["b_ckv"].append(bckv[:N_CTX].reshape(N_CTX_SEQ, CTX_LEN, B_KV_LORA))
        new["b_kpe"].append(bkpe[:N_CTX].reshape(N_CTX_SEQ, CTX_LEN, B_ROPE))
        new["c"].append(jnp.swapaxes(c_fin.reshape(N_CTX_SEQ, 2, C_HEADS, C_DV, C_DK), 3, 4))
        new["d"].append(d_fin.reshape(N_CTX_SEQ, 2, D_HEADS, D_N, D_N))
    y_prompt = x[:N_CTX].reshape(x_prompt.shape)
    y_sample = x[N_CTX:].reshape(x_sample.shape)
    return (y_prompt, y_sample, *(jnp.stack(new[name], axis=1) for name in ("a_k", "a_v", "b_ckv", "b_kpe", "c", "d")))
```

```python
import functools

import jax
import jax.numpy as jnp
import numpy as np
from jax import lax
from jax.experimental import pallas as pl
from jax.experimental.pallas import tpu as pltpu

F32 = jnp.float32
BF16 = jnp.bfloat16

MXU_WIDTH = 256
VMEM_LIMIT = 56 * 1024 * 1024

D_MODEL = 1024
DEPTH = 2
GRID_W = 64
ROPE_BASE = 10000.0
A_HEADS, A_KV_HEADS, A_HD = 4, 2, 64
B_HEADS, B_NOPE, B_ROPE, B_VD, B_Q_LORA, B_KV_LORA = 4, 64, 32, 64, 192, 128
C_HEADS, C_DK, C_DV, C_GATE_RANK, C_GATE_TEMP = 4, 32, 64, 16, 16.0
D_HEADS, D_N, D_DECAY_RANK, D_AAA_RANK, D_GATE_RANK, D_GN_EPS = 4, 64, 64, 64, 128, 64e-5
BRANCH_W = 256
N_BRANCH = 4
N_EXPERTS = 32
TOP_K = 4
SWIGLU_LIMIT = 7.0
SWIGLU_ALPHA = 1.702
ALPHA = (2 * DEPTH) ** 0.25
LN_EPS = 1e-5
RMS_EPS = 1e-6

N_CTX_SEQ, CTX_LEN = 16, 256
N_LAT_SEQ, LAT_LEN = 2, 2048
N_CTX = N_CTX_SEQ * CTX_LEN
N_TOK = N_CTX + N_LAT_SEQ * LAT_LEN
ROW_TILE = 256
N_TILES = N_TOK // ROW_TILE
CTX_TILES = N_CTX // ROW_TILE
LAT_TILES_PER_SEQ = LAT_LEN // ROW_TILE
N_SEQ = N_CTX_SEQ + N_LAT_SEQ

_ORIG = dict(aq=(0, 256), ak=(256, 384), av=(384, 512), bcq=(512, 704), bckv=(704, 832), bkpe=(832, 864),
             cq=(864, 992), ck=(992, 1120), cv=(1120, 1376), cog=(1376, 1632), caf=(1632, 1648), cab=(1648, 1664),
             zd=(1664, 2816))
_ORDER = ("aq", "ak", "av", "cq", "ck", "cv", "cog", "zd", "bcq", "caf", "cab", "bkpe", "bckv")
COL = {}
_off = 0
for _name in _ORDER:
    _w = _ORIG[_name][1] - _ORIG[_name][0]
    COL[_name] = (_off, _off + _w)
    _off += _w
SMALL_COLS = _off
G_START = 2816


def _cs(name):
    return slice(*COL[name])


def _split3(x):
    hi = x.astype(BF16)
    r1 = x - hi.astype(F32)
    mid = r1.astype(BF16)
    lo = (r1 - mid.astype(F32)).astype(BF16)
    return hi, mid, lo


def _split2(x):
    hi = x.astype(BF16)
    lo = (x - hi.astype(F32)).astype(BF16)
    return hi, lo


def _bdot(a, b, dims):
    return lax.dot_general(a, b, dims, preferred_element_type=F32)


_D2 = (((1,), (0,)), ((), ()))
_D2T = (((1,), (1,)), ((), ()))
_NN = (((2,), (1,)), ((0,), (0,)))
_NT = (((2,), (2,)), ((0,), (0,)))
_TN = (((1,), (1,)), ((0,), (0,)))


def _dot1(a, b, dims=_D2):
    return _bdot(a.astype(BF16), b.astype(BF16), dims)


def _dot3(a, b, dims=_D2):
    ah, al = _split2(a)
    bh, bl = _split2(b)
    return _bdot(ah, bh, dims) + (_bdot(ah, bl, dims) + _bdot(al, bh, dims))


def _dot_exact_lhs(a01, b, dims=_D2):
    a = a01.astype(BF16)
    h, m, l = _split3(b)
    return _bdot(a, h, dims) + (_bdot(a, m, dims) + _bdot(a, l, dims))


def _dot_exact_rhs(a, b01, dims=_D2):
    b = b01.astype(BF16)
    h, m, l = _split3(a)
    return _bdot(h, b, dims) + (_bdot(m, b, dims) + _bdot(l, b, dims))


def _dot6(a, b, dims=_D2):
    ah, am, al = _split3(a)
    bh, bm, bl = _split3(b)
    return (_bdot(ah, bh, dims) + (_bdot(ah, bm, dims) + _bdot(am, bh, dims))
            + (_bdot(am, bm, dims) + (_bdot(ah, bl, dims) + _bdot(al, bh, dims))))


def _sigmoid(x):
    return 1.0 / (1.0 + jnp.exp(-x))


def _softplus(x):
    return jnp.maximum(x, 0.0) + jnp.log(1.0 + jnp.exp(-jnp.abs(x)))


def _mod_row(t):
    return jnp.where(t < CTX_TILES, 0, 1 + (t - CTX_TILES) // LAT_TILES_PER_SEQ)


def _full(shape):
    nd = len(shape)
    return pl.BlockSpec(shape, lambda *_: (0,) * nd)


MOD_COL_TILE = 1536


def _mod_kernel(c_ref, w_ref, b_ref, o_ref):
    c = c_ref[...]
    o_ref[0] = _dot3(c * _sigmoid(c), w_ref[0]) + b_ref[0]


def modulation_table(cond8, w_mod, b_mod):
    depth, d, six_d = w_mod.shape
    return pl.pallas_call(
        _mod_kernel,
        grid=(depth, six_d // MOD_COL_TILE),
        in_specs=[pl.BlockSpec((8, d), lambda l, j: (0, 0)),
                  pl.BlockSpec((1, d, MOD_COL_TILE), lambda l, j: (l, 0, j)),
                  pl.BlockSpec((1, 1, MOD_COL_TILE), lambda l, j: (l, 0, j))],
        out_specs=pl.BlockSpec((1, 8, MOD_COL_TILE), lambda l, j: (l, 0, j)),
        out_shape=jax.ShapeDtypeStruct((depth, 8, six_d), F32),
        compiler_params=pltpu.CompilerParams(dimension_semantics=("parallel", "parallel")),
        name="modulation",
    )(cond8, w_mod, b_mod.reshape(depth, 1, six_d))


def _rot_pairs(x, half, lane_mod_base=0):
    w = x.shape[-1]
    lane = lax.broadcasted_iota(jnp.int32, (1, w), 1) - lane_mod_base
    first = (lane % (2 * half)) < half
    return jnp.where(first, -pltpu.roll(x, w - half, axis=1), pltpu.roll(x, half, axis=1))


def _pre_kernel(x_ref, xp_ref, xn_ref, mod_ref, w_ref, ca_ref, sa_ref, cb_ref, sb_ref, ck_ref, sk_ref,
                qnorm_ref, kvnorm_ref, wuq_ref, cwg_ref, cbg_ref, rep_ref, mu_ref, dw0_ref, dw2_ref, da0_ref,
                da2_ref, dg2_ref, dkk_ref, dka_ref, drk_ref, bd_ref,
                aq_o, ak_o, av_o, bq_o, bckv_o, bkpe_o, cq4_o, ck4_o, claf_o, clab_o, cv_o, cgate_o,
                r_o, v_o, kk_o, lwf_o, lwb_o, kf_o, kb_o, af_o, ab_o, bonus_o, dgate_o):
    t = pl.program_id(0)
    tm = x_ref.shape[0]
    sh1 = mod_ref[0, 0:1, :]
    sc1 = mod_ref[0, 1:2, :]

    def modulate(xv):
        return (xv * (1.0 + sc1) + sh1).astype(BF16)

    z = jnp.dot(modulate(x_ref[...]), w_ref[...], preferred_element_type=F32)

    aq = z[:, _cs("aq")]
    ak = z[:, _cs("ak")]
    aq_o[...] = aq * ca_ref[...] + _rot_pairs(aq, A_HD // 4) * sa_ref[...]
    ak_o[...] = ak * ca_ref[:, :ak.shape[1]] + _rot_pairs(ak, A_HD // 4) * sa_ref[:, :ak.shape[1]]
    av_o[...] = z[:, _cs("av")]

    bcq = z[:, _cs("bcq")]
    qn = bcq * lax.rsqrt(jnp.mean(bcq * bcq, axis=-1, keepdims=True) + RMS_EPS) * qnorm_ref[...]
    bq = _dot1(qn, wuq_ref[...])
    bq_o[...] = bq * cb_ref[...] + _rot_pairs(bq, B_ROPE // 4, lane_mod_base=B_NOPE) * sb_ref[...]
    bckv = z[:, _cs("bckv")]
    bckv_o[...] = bckv * lax.rsqrt(jnp.mean(bckv * bckv, axis=-1, keepdims=True) + RMS_EPS) * kvnorm_ref[...]
    kpe_lo = COL["bkpe"][0] // 128 * 128
    kblk = z[:, kpe_lo:kpe_lo + 128]
    kblk = kblk * ck_ref[...] + _rot_pairs(kblk, B_ROPE // 4) * sk_ref[...]
    bkpe_o[...] = kblk[:, COL["bkpe"][0] - kpe_lo:COL["bkpe"][1] - kpe_lo]

    rep = rep_ref[...]
    cq4_o[...] = _dot_exact_rhs(z[:, _cs("cq")] * (C_DK ** -0.5), rep)
    ck4_o[...] = _dot_exact_rhs(z[:, _cs("ck")], rep)
    cv_o[...] = z[:, _cs("cv")]
    cog = z[:, _cs("cog")]
    cgate_o[...] = cog * _sigmoid(cog)
    for direction, (name, out) in enumerate((("caf", claf_o), ("cab", clab_o))):
        pre = _dot3(z[:, _cs(name)], cwg_ref[direction]) + cbg_ref[direction]
        out[...] = _dot_exact_rhs(-_softplus(-pre) * (1.0 / C_GATE_TEMP), rep)

    zd_cols = _cs("zd")
    zd = z[:, zd_cols]
    wd = w_ref[:, zd_cols]
    j = (t - CTX_TILES) % LAT_TILES_PER_SEQ
    latent = t >= CTX_TILES
    has_prev = jnp.logical_and(latent, j != 0)
    has_next = jnp.logical_and(latent, j != LAT_TILES_PER_SEQ - 1)
    zp = jnp.dot(modulate(xp_ref[...]), wd, preferred_element_type=F32)
    zn = jnp.dot(modulate(xn_ref[...]), wd, preferred_element_type=F32)
    prev_row = jnp.where(has_prev, zp[7:8], 0.0)
    next_row = jnp.where(has_next, zn[0:1], 0.0)
    row = lax.broadcasted_iota(jnp.int32, (tm, 1), 0)
    up = jnp.where(row == 0, prev_row, pltpu.roll(zd, 1, axis=0))
    dn = jnp.where(row == tm - 1, next_row, pltpu.roll(zd, tm - 1, axis=0))
    zd = zd + (0.5 * (up + dn) - zd) * mu_ref[...]

    hn = D_HEADS * D_N
    d_r, d_k, d_v = zd[:, :hn], zd[:, hn:2 * hn], zd[:, 2 * hn:3 * hn]
    o = 3 * hn
    d_w = (zd[:, o:o + D_DECAY_RANK], zd[:, o + D_DECAY_RANK:o + 2 * D_DECAY_RANK])
    o += 2 * D_DECAY_RANK
    d_a = (zd[:, o:o + D_AAA_RANK], zd[:, o + D_AAA_RANK:o + 2 * D_AAA_RANK])
    o += 2 * D_AAA_RANK
    d_g = zd[:, o:o + D_GATE_RANK]
    bd = bd_ref[...]
    kk = d_k * dkk_ref[...]
    kk = kk / jnp.maximum(jnp.sqrt(_dot_exact_rhs(kk * kk, bd)), 1e-12)
    r_o[...] = d_r
    v_o[...] = d_v
    kk_o[...] = kk
    k_sum = None
    for direction, (lw_o, k_o, a_o) in enumerate(((lwf_o, kf_o, af_o), (lwb_o, kb_o, ab_o))):
        w_log = -_softplus(-(dw0_ref[direction] + _dot3(jnp.tanh(d_w[direction]), dw2_ref[direction]))) - 0.5
        lw_o[...] = -jnp.exp(w_log)
        a = _sigmoid(da0_ref[direction] + _dot3(d_a[direction], da2_ref[direction]))
        k_dir = d_k * (1.0 + (a - 1.0) * dka_ref[...])
        k_o[...] = k_dir
        a_o[...] = a
        k_sum = k_dir if k_sum is None else k_sum + k_dir
    bonus_o[...] = d_v * _dot_exact_rhs(d_r * drk_ref[...] * k_sum, bd)
    dgate_o[...] = _dot3(_sigmoid(d_g), dg2_ref[...])


def _rope_tables():
    pos = np.arange(LAT_LEN)
    rowp, colp = (pos // GRID_W).astype(np.float32), (pos % GRID_W).astype(np.float32)

    f32 = np.float32

    def head_tables(rot_dim):
        quarter = rot_dim // 4
        inv = (f32(ROPE_BASE) ** (-np.arange(quarter, dtype=f32) / f32(quarter))).astype(f32)
        ar = (rowp[:, None] * inv).astype(f32)
        ac = (colp[:, None] * inv).astype(f32)
        cos = np.concatenate([np.cos(ar), np.cos(ar), np.cos(ac), np.cos(ac)], axis=-1).astype(f32)
        sin = np.concatenate([np.sin(ar), np.sin(ar), np.sin(ac), np.sin(ac)], axis=-1).astype(f32)
        return cos, sin

    def with_identity(c, s):
        w = c.shape[1]
        return (jnp.asarray(np.concatenate([np.ones((ROW_TILE, w), f32), c], axis=0)),
                jnp.asarray(np.concatenate([np.zeros((ROW_TILE, w), f32), s], axis=0)))

    ca, sa = head_tables(A_HD)
    ca, sa = with_identity(np.tile(ca, (1, A_HEADS)), np.tile(sa, (1, A_HEADS)))
    cbh, sbh = head_tables(B_ROPE)
    ones, zeros = np.ones((LAT_LEN, B_NOPE), f32), np.zeros((LAT_LEN, B_NOPE), f32)
    cb, sb = with_identity(np.tile(np.concatenate([ones, cbh], axis=1), (1, B_HEADS)),
                           np.tile(np.concatenate([zeros, sbh], axis=1), (1, B_HEADS)))
    pad = 128 - B_ROPE
    ck, sk = with_identity(np.concatenate([np.ones((LAT_LEN, pad), f32), cbh], axis=1),
                           np.concatenate([np.zeros((LAT_LEN, pad), f32), sbh], axis=1))
    return ca, sa, cb, sb, ck, sk


def _lane_repeat_matrix():
    m = np.zeros((C_HEADS * C_DK, C_HEADS * 128), np.float32)
    for h in range(C_HEADS):
        for g in range(128 // C_DK):
            for d in range(C_DK):
                m[h * C_DK + d, h * 128 + g * C_DK + d] = 1.0
    return jnp.asarray(m, BF16)


def _head_block_diag():
    m = np.kron(np.eye(D_HEADS, dtype=np.float32), np.ones((D_N, D_N), np.float32))
    return jnp.asarray(m, BF16)


PRE_OUT_WIDTHS = (256, 128, 128, 384, 128, 32, 512, 512, 512, 512, 256, 256) + (256,) * 11


def mixer_prelude(x, mod_l, w_small, tables, lp):
    tm = ROW_TILE
    tab_idx = lambda t: (jnp.where(t < CTX_TILES, 0, 1 + (t - CTX_TILES) % LAT_TILES_PER_SEQ), 0)
    hn = D_HEADS * D_N
    small = [lp['b_q_norm'].reshape(1, -1), lp['b_kv_norm'].reshape(1, -1), lp['b_w_uq'], lp['c_w_gate'],
             lp['c_b_gate'].reshape(2, 1, -1), _lane_repeat_matrix(), lp['d_mu'].reshape(1, -1),
             lp['d_w0'].reshape(2, 1, hn), lp['d_w2'], lp['d_a0'].reshape(2, 1, hn), lp['d_a2'], lp['d_g2'],
             lp['d_k_k'].reshape(1, hn), lp['d_k_a'].reshape(1, hn), lp['d_r_k'].reshape(1, hn), _head_block_diag()]
    in_specs = ([pl.BlockSpec((tm, D_MODEL), lambda t: (t, 0)),
                 pl.BlockSpec((8, D_MODEL), lambda t: (jnp.maximum(t * (tm // 8) - 1, 0), 0)),
                 pl.BlockSpec((8, D_MODEL), lambda t: (jnp.minimum((t + 1) * (tm // 8), N_TOK // 8 - 1), 0)),
                 pl.BlockSpec((1, 6, D_MODEL), lambda t: (_mod_row(t), 0, 0)),
                 _full(w_small.shape)]
                + [pl.BlockSpec((tm, tab.shape[1]), tab_idx) for tab in tables]
                + [_full(a.shape) for a in small])
    return pl.pallas_call(
        _pre_kernel,
        grid=(N_TILES,),
        in_specs=in_specs,
        out_specs=[pl.BlockSpec((tm, w), lambda t: (t, 0)) for w in PRE_OUT_WIDTHS],
        out_shape=[jax.ShapeDtypeStruct((N_TOK, w), F32) for w in PRE_OUT_WIDTHS],
        compiler_params=pltpu.CompilerParams(dimension_semantics=("parallel",), vmem_limit_bytes=VMEM_LIMIT),
        name="mixer_prelude",
    )(x, x, x, mod_l, w_small, *tables, *small)


ATT_Q_BLOCK = 128
ATT_WINDOW = 128
ATT_NEG_INF = -1e30
CACHE_LEN = 512


def _softmax_pv(s, v, sink):
    m = jnp.max(s, axis=-1, keepdims=True)
    if sink is not None:
        m = jnp.maximum(m, sink)
    e = jnp.exp(s - m)
    den = jnp.sum(e, axis=-1, keepdims=True)
    if sink is not None:
        den = den + jnp.exp(sink - m)
    return jnp.dot(e.astype(BF16), v.astype(BF16), preferred_element_type=F32) / den


def _gqa_kernel(sink_ref, q_ref, k_ref, v_ref, *rest, hd, group, scale, windowed):
    if windowed:
        kp_ref, kn_ref, vp_ref, vn_ref, kc_ref, vc_ref, _, o_ref = rest
    else:
        (o_ref,) = rest
    i = pl.program_id(1)
    tq = q_ref.shape[0]
    n_kv = k_ref.shape[1] // hd
    if windowed:
        qpos = i * tq + lax.broadcasted_iota(jnp.int32, (tq, 3 * tq), 0)
        kpos = (i - 1) * tq + lax.broadcasted_iota(jnp.int32, (tq, 3 * tq), 1)
        n_tok = pl.num_programs(1) * tq
        mask = (jnp.abs(qpos - kpos) <= ATT_WINDOW) & (kpos >= 0) & (kpos < n_tok)
        mask = jnp.concatenate([mask] * group, axis=0)
    for kvh in range(n_kv):
        ks = slice(kvh * hd, (kvh + 1) * hd)
        qs = [q_ref[:, (kvh * group + g) * hd:(kvh * group + g + 1) * hd] for g in range(group)]
        q = (jnp.concatenate(qs, axis=0) * scale).astype(BF16)
        sink = jnp.concatenate(
            [jnp.full((tq, 1), sink_ref[kvh * group + g], F32) for g in range(group)], axis=0)
        if windowed:
            k_win = jnp.concatenate([kp_ref[:, ks], k_ref[:, ks], kn_ref[:, ks]], axis=0)
            v_win = jnp.concatenate([vp_ref[:, ks], v_ref[:, ks], vn_ref[:, ks]], axis=0)
            s_win = _bdot(q, k_win.astype(BF16), _D2T)
            s_win = jnp.where(mask, s_win, ATT_NEG_INF)
            s_ctx = _bdot(q, kc_ref[0, 0, kvh].astype(BF16), _D2T)
            s = jnp.concatenate([s_win, s_ctx], axis=1)
            v = jnp.concatenate([v_win, vc_ref[0, 0, kvh]], axis=0)
        else:
            s = _bdot(q, k_ref[:, ks].astype(BF16), _D2T)
            v = v_ref[:, ks]
        o = _softmax_pv(s, v, sink)
        for g in range(group):
            h = kvh * group + g
            o_ref[:, h * hd:(h + 1) * hd] = o[g * tq:(g + 1) * tq]


def gqa_attention(q, k, v, sink, cache_k, cache_v, layer):
    qw, kw = q.shape[1], k.shape[1]
    group = qw // kw
    scale = A_HD ** -0.5
    params = pltpu.CompilerParams(dimension_semantics=("parallel", "parallel"))
    out_shape = jax.ShapeDtypeStruct((N_TOK, qw), F32)
    ctx_spec = lambda w: pl.BlockSpec((CTX_LEN, w), lambda s, i, sk: (s, 0))
    o = pl.pallas_call(
        functools.partial(_gqa_kernel, hd=A_HD, group=group, scale=scale, windowed=False),
        grid_spec=pltpu.PrefetchScalarGridSpec(
            num_scalar_prefetch=1, grid=(N_CTX_SEQ, 1), in_specs=[ctx_spec(qw), ctx_spec(kw), ctx_spec(kw)],
            out_specs=ctx_spec(qw)),
        out_shape=out_shape, compiler_params=params, name="gqa_full",
    )(sink, q, k, v)
    tq = ATT_Q_BLOCK
    nb = LAT_LEN // tq
    base = N_CTX // tq
    blk = lambda w, f: pl.BlockSpec((tq, w), lambda b, i, sk: (base + nb * b + f(i), 0))
    same = lambda i: i
    prev = lambda i: jnp.maximum(i - 1, 0)
    nxt = lambda i: jnp.minimum(i + 1, nb - 1)
    cspec = pl.BlockSpec((1, 1) + cache_k.shape[2:], lambda b, i, sk: (b, layer, 0, 0, 0))
    return pl.pallas_call(
        functools.partial(_gqa_kernel, hd=A_HD, group=group, scale=scale, windowed=True),
        grid_spec=pltpu.PrefetchScalarGridSpec(
            num_scalar_prefetch=1, grid=(N_LAT_SEQ, nb),
            in_specs=[blk(qw, same), blk(kw, same), blk(kw, same), blk(kw, prev), blk(kw, nxt), blk(kw, prev),
                      blk(kw, nxt), cspec, cspec, pl.BlockSpec(memory_space=pl.ANY)],
            out_specs=blk(qw, same)),
        out_shape=out_shape, input_output_aliases={10: 0}, compiler_params=params, name="gqa_windowed",
    )(sink, q, k, v, k, k, v, v, cache_k, cache_v, o)


def _mla_kernel(q_ref, ckv_ref, kpe_ref, wukv_ref, *rest, n_heads, nope, rope, vd, scale, cached):
    if cached:
        cckv_ref, ckpe_ref, _, o_ref, kv_scr, kpe_scr = rest
    else:
        o_ref, kv_scr, kpe_scr = rest
    i = pl.program_id(1)
    n_cache = kv_scr.shape[0] - ckv_ref.shape[0]

    @pl.when(i == 0)
    def _():
        w = wukv_ref[...].astype(BF16)
        if cached:
            kv_scr[:n_cache] = jnp.dot(cckv_ref[0, 0].astype(BF16), w, preferred_element_type=F32).astype(BF16)
            kpe_scr[:n_cache] = ckpe_ref[0, 0].astype(BF16)
        kv_scr[n_cache:] = jnp.dot(ckv_ref[...].astype(BF16), w, preferred_element_type=F32).astype(BF16)
        kpe_scr[n_cache:] = kpe_ref[...].astype(BF16)

    kpe = kpe_scr[...]
    qd = nope + rope
    for h in range(n_heads):
        qn = (q_ref[:, h * qd:h * qd + nope] * scale).astype(BF16)
        qp = (q_ref[:, h * qd + nope:(h + 1) * qd] * scale).astype(BF16)
        k_n = kv_scr[:, h * (nope + vd):h * (nope + vd) + nope]
        v = kv_scr[:, h * (nope + vd) + nope:(h + 1) * (nope + vd)]
        s = _bdot(qn, k_n, _D2T) + _bdot(qp, kpe, _D2T)
        o_ref[:, h * vd:(h + 1) * vd] = _softmax_pv(s, v, None)


def mla_attention(q, ckv, kpe, w_ukv, cache_ckv, cache_kpe, layer):
    qw = q.shape[1]
    tq = ATT_Q_BLOCK
    kw = dict(n_heads=B_HEADS, nope=B_NOPE, rope=B_ROPE, vd=B_VD, scale=(B_NOPE + B_ROPE) ** -0.5)
    params = pltpu.CompilerParams(dimension_semantics=("parallel", "arbitrary"))
    out_shape = jax.ShapeDtypeStruct((N_TOK, B_HEADS * B_VD), F32)
    kvw = B_HEADS * (B_NOPE + B_VD)
    nbc = CTX_LEN // tq
    o = pl.pallas_call(
        functools.partial(_mla_kernel, cached=False, **kw),
        grid=(N_CTX_SEQ, nbc),
        in_specs=[pl.BlockSpec((tq, qw), lambda s, i: (s * nbc + i, 0)),
                  pl.BlockSpec((CTX_LEN, B_KV_LORA), lambda s, i: (s, 0)),
                  pl.BlockSpec((CTX_LEN, B_ROPE), lambda s, i: (s, 0)),
                  _full(w_ukv.shape)],
        out_specs=pl.BlockSpec((tq, B_HEADS * B_VD), lambda s, i: (s * nbc + i, 0)),
        out_shape=out_shape,
        scratch_shapes=[pltpu.VMEM((CTX_LEN, kvw), BF16), pltpu.VMEM((CTX_LEN, B_ROPE), BF16)],
        compiler_params=params, name="mla_context",
    )(q, ckv, kpe, w_ukv)
    nb = LAT_LEN // tq
    base = N_CTX // tq
    lat0 = N_CTX // LAT_LEN
    s_len = CACHE_LEN + LAT_LEN
    return pl.pallas_call(
        functools.partial(_mla_kernel, cached=True, **kw),
        grid=(N_LAT_SEQ, nb),
        in_specs=[pl.BlockSpec((tq, qw), lambda b, i: (base + nb * b + i, 0)),
                  pl.BlockSpec((LAT_LEN, B_KV_LORA), lambda b, i: (lat0 + b, 0)),
                  pl.BlockSpec((LAT_LEN, B_ROPE), lambda b, i: (lat0 + b, 0)),
                  _full(w_ukv.shape),
                  pl.BlockSpec((1, 1, CACHE_LEN, B_KV_LORA), lambda b, i: (b, layer, 0, 0)),
                  pl.BlockSpec((1, 1, CACHE_LEN, B_ROPE), lambda b, i: (b, layer, 0, 0)),
                  pl.BlockSpec(memory_space=pl.ANY)],
        out_specs=pl.BlockSpec((tq, B_HEADS * B_VD), lambda b, i: (base + nb * b + i, 0)),
        out_shape=out_shape, input_output_aliases={6: 0},
        scratch_shapes=[pltpu.VMEM((s_len, kvw), BF16), pltpu.VMEM((s_len, B_ROPE), BF16)],
        compiler_params=params, name="mla_latent",
    )(q, ckv, kpe, w_ukv, cache_ckv, cache_kpe, o)


CHUNK = 64
GLA_SUB = 16
PAIR = 2
N_CHAIN = PAIR * 2 * 4


def _is_back(shape):
    return (lax.broadcasted_iota(jnp.int32, shape, 0) // 4) % 2 == 1


def _chains(ref_f, ref_b, width):
    return jnp.stack([ref[0, s, 0, :, h * width:(h + 1) * width]
                      for s in range(PAIR) for ref in (ref_f, ref_b) for h in range(4)], axis=0)


def _unchain(y, o_f, o_b):
    for s in range(PAIR):
        o_f[0, s, 0] = jnp.concatenate([y[s * 8 + h] for h in range(4)], axis=-1)
        o_b[0, s, 0] = jnp.concatenate([y[s * 8 + 4 + h] for h in range(4)], axis=-1)


def _dir_masks(L):
    shape = (N_CHAIN, L, L)
    back = _is_back(shape)
    row = lax.broadcasted_iota(jnp.int32, shape, 1)
    col = lax.broadcasted_iota(jnp.int32, shape, 2)
    ahead = jnp.where(back, col - row, row - col)
    return ahead >= 0, ahead > 0, row == col


def _chunk_end(ci):
    L = ci.shape[1]
    return jnp.where(_is_back((N_CHAIN, 1, 1)), ci[:, 0:1], ci[:, L - 1:L])


def _split_refs(refs, n_in, has_s0, has_sfin):
    ins = refs[:n_in]
    pos = n_in
    s0_ref = None
    if has_s0:
        s0_ref = refs[pos]
        pos += 3
    of_ref, ob_ref = refs[pos], refs[pos + 1]
    pos += 2
    sfin_ref = refs[pos] if has_sfin else None
    return ins, s0_ref, of_ref, ob_ref, sfin_ref, refs[-1]


def _init_state(s_scr, s0_ref):
    @pl.when(pl.program_id(1) == 0)
    def _():
        if s0_ref is None:
            s_scr[...] = jnp.zeros_like(s_scr)
        else:
            s_scr[...] = s0_ref[0]


def _emit_state(sfin_ref, s_new):
    if sfin_ref is None:
        return

    @pl.when(pl.program_id(1) == pl.num_programs(1) - 1)
    def _():
        sfin_ref[0] = s_new


def _rwkv_kernel(*refs, dot, has_s0, has_sfin):
    (rf, rb, vf, vb, kkf, kkb, lwf, lwb, kf, kb, af, ab), s0_ref, yf_ref, yb_ref, sfin_ref, s_scr = _split_refs(
        refs, 12, has_s0, has_sfin)
    _init_state(s_scr, s0_ref)
    n = D_N
    r = _chains(rf, rb, n)
    v = _chains(vf, vb, n)
    kk = _chains(kkf, kkb, n)
    lw = _chains(lwf, lwb, n)
    k = _chains(kf, kb, n)
    a = _chains(af, ab, n)
    L = r.shape[1]
    S = s_scr[...]
    incl, strict, diag = _dir_masks(L)
    ci = _dot_exact_lhs(jnp.where(incl, 1.0, 0.0), lw, _NN)
    ce = ci - lw
    cl = _chunk_end(ci)
    e_neg = jnp.exp(-ci)
    b = a * kk
    alpha = kk * jnp.exp(ce)
    rho = r * jnp.exp(ci)
    beta = b * e_neg
    kappa = k * e_neg
    e_end = jnp.exp(cl - ci)
    ar = jnp.concatenate([alpha, rho], axis=1)
    bk = jnp.concatenate([beta, kappa], axis=1)
    w = dot(ar, bk, _NT)
    nmat = jnp.where(strict, w[:, :L, :L], 0.0)
    mmat = jnp.where(strict, w[:, :L, L:], 0.0)
    p1 = jnp.where(incl, w[:, L:, :L], 0.0)
    p2 = jnp.where(incl, w[:, L:, L:], 0.0)
    x = jnp.where(diag, 1.0, 0.0) - nmat
    p = dot(nmat, nmat, _NN)
    span = 2
    while True:
        x = x + dot(x, p, _NN)
        span *= 2
        if span >= L:
            break
        p = dot(p, p, _NN)
    us = dot(ar, S, _NT)
    rhs = us[:, :L] + dot(mmat, v, _NN)
    d = -dot(x, rhs, _NN)
    dv = jnp.concatenate([d, v], axis=1)
    pp = jnp.concatenate([p1, p2], axis=2)
    _unchain(us[:, L:] + dot(pp, dv, _NN), yf_ref, yb_ref)
    bk_end = jnp.concatenate([b * e_end, k * e_end], axis=1)
    s_new = S * jnp.exp(cl) + dot(dv, bk_end, _TN)
    s_scr[...] = s_new
    _emit_state(sfin_ref, s_new)


def _gla_kernel(*refs, dot, has_s0, has_sfin):
    (qf, qb, kf, kb, vf, vb, laf, lab), s0_ref, of_ref, ob_ref, sfin_ref, s_scr = _split_refs(
        refs, 8, has_s0, has_sfin)
    _init_state(s_scr, s0_ref)
    q4 = _chains(qf, qb, 128)
    k4 = _chains(kf, kb, 128)
    la4 = _chains(laf, lab, 128)
    v = _chains(vf, vb, C_DV)
    g, L, lanes = q4.shape
    dk = C_DK
    n_sub = L // GLA_SUB
    st = s_scr[...]
    incl, _, _ = _dir_masks(L)
    c = _dot_exact_lhs(jnp.where(incl, 1.0, 0.0), la4, _NN)
    shape = (g, L, lanes)
    back = _is_back(shape)
    lane_blk = lax.broadcasted_iota(jnp.int32, shape, 2) // dk
    row_blk = lax.broadcasted_iota(jnp.int32, shape, 1) // GLA_SUB
    cref_f = jnp.zeros(shape, F32)
    cref_b = jnp.zeros(shape, F32)
    for j in range(1, n_sub):
        cref_f = jnp.where(lane_blk == j, c[:, j * GLA_SUB - 1:j * GLA_SUB], cref_f)
        cref_b = jnp.where(lane_blk == j - 1, c[:, j * GLA_SUB:j * GLA_SUB + 1], cref_b)
    cref = jnp.where(back, cref_b, cref_f)
    q_on = row_blk == lane_blk
    k_on = jnp.where(back, row_blk - lane_blk, lane_blk - row_blk) >= 0
    qh = jnp.where(q_on, q4 * jnp.exp(jnp.where(q_on, c - cref, 0.0)), 0.0)
    kh = jnp.where(k_on, k4 * jnp.exp(jnp.where(k_on, cref - c, 0.0)), 0.0)
    att = jnp.where(incl, dot(qh, kh, _NT), 0.0)
    cl = _chunk_end(c)
    qe = (q4 * jnp.exp(c))[:, :, :dk]
    ke = (k4 * jnp.exp(cl - c))[:, :, :dk]
    _unchain(dot(qe, st, _NT) + dot(att, v, _NN), of_ref, ob_ref)
    s_new = st * jnp.exp(cl[:, :, :dk]) + dot(v, ke, _TN)
    s_scr[...] = s_new
    _emit_state(sfin_ref, s_new)


def _recurrence_calls(kernel_fn, name, pairs, singles_f, singles_b, s0_lat, state_dims, out_width):
    def run(view, grid, group, s0, prev_out):
        nc = view[2]
        fwd_map = lambda p, c: (group(p), 0, c, 0, 0)
        bwd_map = lambda p, c: (group(p), 0, nc - 1 - c, 0, 0)
        blk = lambda w: (1, PAIR, 1, CHUNK, w)
        args, in_specs = [], []
        for af, ab in [(a, a) for a in pairs] + list(zip(singles_f, singles_b)):
            w = af.shape[-1]
            args += [af.reshape(view + (w,)), ab.reshape(view + (w,))]
            in_specs += [pl.BlockSpec(blk(w), fwd_map), pl.BlockSpec(blk(w), bwd_map)]
        out_specs = [pl.BlockSpec(blk(out_width), fwd_map), pl.BlockSpec(blk(out_width), bwd_map)]
        out_shape = [jax.ShapeDtypeStruct(view + (out_width,), F32)] * 2
        aliases = {}
        if s0 is not None:
            args += [s0] + [o.reshape(view + (out_width,)) for o in prev_out]
            in_specs += [_full(s0.shape), pl.BlockSpec(memory_space=pl.ANY), pl.BlockSpec(memory_space=pl.ANY)]
            aliases = {len(args) - 2: 0, len(args) - 1: 1}
        else:
            out_specs.append(pl.BlockSpec((1, N_CHAIN) + state_dims, lambda p, c: (p, 0, 0, 0)))
            out_shape.append(jax.ShapeDtypeStruct((grid[0], N_CHAIN) + state_dims, F32))
        return pl.pallas_call(
            functools.partial(kernel_fn, has_s0=s0 is not None, has_sfin=s0 is None),
            grid=grid, in_specs=in_specs, out_specs=out_specs, out_shape=out_shape,
            input_output_aliases=aliases, scratch_shapes=[pltpu.VMEM((N_CHAIN,) + state_dims, F32)],
            compiler_params=pltpu.CompilerParams(dimension_semantics=("parallel", "arbitrary")),
            name=name + ("_latent" if s0 is not None else "_context"),
        )(*args)

    ctx_nc = CTX_LEN // CHUNK
    ctx_view = (N_TOK // (PAIR * CTX_LEN), PAIR, ctx_nc, CHUNK)
    o_f, o_b, s_fin = run(ctx_view, (N_CTX_SEQ // PAIR, ctx_nc), lambda p: p, None, None)
    lat_nc = LAT_LEN // CHUNK
    lat_view = (N_TOK // (PAIR * LAT_LEN), PAIR, lat_nc, CHUNK)
    o_f, o_b = run(lat_view, (1, lat_nc), lambda p: N_CTX // (PAIR * LAT_LEN), s0_lat, (o_f, o_b))
    return o_f.reshape(N_TOK, out_width), o_b.reshape(N_TOK, out_width), s_fin


def _layer_norm(x, g, b):
    mu = jnp.mean(x, axis=-1, keepdims=True)
    xc = x - mu
    var = jnp.mean(xc * xc, axis=-1, keepdims=True)
    return xc * lax.rsqrt(var + LN_EPS) * g + b


def _merge_kernel(x_ref, mod_ref, oa_ref, ob_ref, cof_ref, cob_ref, cgate_ref, yf_ref, yb_ref, bonus_ref, dgate_ref,
                  wg_ref, wbr_ref, wout_ref, cnorm_ref, dlng_ref, dlnb_ref, lng_ref, lnb_ref, wr_ref, br_ref, bd_ref,
                  x1_o, h2_o, topi_o, topw_o):
    x = x_ref[...]
    m = mod_ref[0]
    sh1, sc1, g1, sh2, sc2 = m[0:1], m[1:2], m[2:3], m[3:4], m[4:5]
    bd = bd_ref[...]
    inv_n = 1.0 / D_N
    co = cof_ref[...] + cob_ref[...]
    o_c = co * lax.rsqrt(_dot_exact_rhs(co * co, bd) * inv_n + RMS_EPS) * cnorm_ref[...] * cgate_ref[...]
    y = yf_ref[...] + yb_ref[...]
    yc = y - _dot_exact_rhs(y, bd) * inv_n
    var = _dot_exact_rhs(yc * yc, bd) * inv_n
    o_d = (yc * lax.rsqrt(var + D_GN_EPS) * dlng_ref[...] + dlnb_ref[...] + bonus_ref[...]) * dgate_ref[...]
    branches = (oa_ref[...], ob_ref[...], o_c, o_d)
    h = (x * (1.0 + sc1) + sh1).astype(BF16)
    merged = None
    for n in range(N_BRANCH):
        gate = _sigmoid(jnp.dot(h, wg_ref[:, n * D_MODEL:(n + 1) * D_MODEL], preferred_element_type=F32))
        term = gate * jnp.dot(branches[n].astype(BF16), wbr_ref[n], preferred_element_type=F32)
        merged = term if merged is None else merged + term
    mix = jnp.dot(merged.astype(BF16), wout_ref[...], preferred_element_type=F32)
    x1 = _layer_norm(ALPHA * x + g1 * mix, lng_ref[...], lnb_ref[...])
    x1_o[...] = x1
    h2 = x1 * (1.0 + sc2) + sh2
    h2_o[...] = h2.astype(BF16)
    logits = _dot6(h2, wr_ref[...]) + br_ref[...]
    tm, n_e = logits.shape
    lane_e = lax.broadcasted_iota(jnp.int32, (tm, n_e), 1)
    lane_o = lax.broadcasted_iota(jnp.int32, (tm, topi_o.shape[1]), 1)
    top_i = jnp.zeros((tm, topi_o.shape[1]), jnp.int32)
    top_v = jnp.zeros((tm, topw_o.shape[1]), F32)
    vals = []
    for kth in range(TOP_K):
        mx = jnp.max(logits, axis=-1, keepdims=True)
        idx = jnp.min(jnp.where(logits == mx, lane_e, n_e), axis=-1, keepdims=True)
        vals.append(mx)
        top_i = jnp.where(lane_o == kth, idx, top_i)
        logits = jnp.where(lane_e == idx, -jnp.inf, logits)
    es = [jnp.exp(vk - vals[0]) for vk in vals]
    den = es[0] + es[1] + es[2] + es[3]
    for kth in range(TOP_K):
        top_v = jnp.where(lane_o == kth, es[kth] / den, top_v)
    topi_o[...] = top_i
    topw_o[...] = top_v


def merge_and_route(x, mod_l, o_a, o_b, co_f, co_b, cgate, y_f, y_b, bonus, dgate, w_g, w_br, w_out, lp):
    tm = ROW_TILE
    hn = D_HEADS * D_N
    row = lambda w: pl.BlockSpec((tm, w), lambda t: (t, 0))
    small = [jnp.tile(lp['c_norm'], C_HEADS).reshape(1, hn), lp['d_ln_g'].reshape(1, hn), lp['d_ln_b'].reshape(1, hn),
             lp['ln_g'][0].reshape(1, -1), lp['ln_b'][0].reshape(1, -1), lp['w_router'],
             lp['b_router'].reshape(1, -1), _head_block_diag()]
    return pl.pallas_call(
        _merge_kernel,
        grid=(N_TILES,),
        in_specs=([row(D_MODEL), pl.BlockSpec((1, 6, D_MODEL), lambda t: (_mod_row(t), 0, 0))]
                  + [row(hn)] * 9 + [_full(w_g.shape), _full(w_br.shape), _full(w_out.shape)]
                  + [_full(a.shape) for a in small]),
        out_specs=[row(D_MODEL), row(D_MODEL), row(128), row(128)],
        out_shape=[jax.ShapeDtypeStruct((N_TOK, D_MODEL), F32), jax.ShapeDtypeStruct((N_TOK, D_MODEL), BF16),
                   jax.ShapeDtypeStruct((N_TOK, 128), jnp.int32), jax.ShapeDtypeStruct((N_TOK, 128), F32)],
        compiler_params=pltpu.CompilerParams(dimension_semantics=("parallel",), vmem_limit_bytes=VMEM_LIMIT),
        name="merge_and_route",
    )(x, mod_l, o_a, o_b, co_f, co_b, cgate, y_f, y_b, bonus, dgate, w_g, w_br, w_out, *small)


MOE_ROW_TILE = 256


def _moe_kernel(te_ref, tv_ref, first_ref, slot_ref, next_ref, x_ref, w1_hbm, b1_ref, w2_hbm, b2_ref, perm_ref, y_ref,
                w1buf, w2buf, sem, w1s, w2s, hs, *, layer):
    t = pl.program_id(0)
    valid = tv_ref[t] != 0
    d_model, two_f = w1s.shape
    n_blk = two_f // MXU_WIDTH
    half = MXU_WIDTH // 2

    def fetch(expert, slot):
        return (pltpu.make_async_copy(w1_hbm.at[layer, expert], w1buf.at[slot], sem.at[0, slot]),
                pltpu.make_async_copy(w2_hbm.at[layer, expert], w2buf.at[slot], sem.at[1, slot]))

    @pl.when(t == 0)
    def _():
        for cp in fetch(te_ref[0], 0):
            cp.start()

    @pl.when(first_ref[t] == 1)
    def _():
        slot = slot_ref[t]
        for cp in fetch(te_ref[t], slot):
            cp.wait()

        @pl.when(next_ref[t] >= 0)
        def _():
            for cp in fetch(next_ref[t], 1 - slot):
                cp.start()

        for blk in range(n_blk):
            sl = slice(blk * MXU_WIDTH, (blk + 1) * MXU_WIDTH)
            wb = w1buf[slot, :, sl].astype(BF16)
            w1s[:, sl] = jnp.dot(wb, perm_ref[...], preferred_element_type=F32).astype(BF16)
        w2s[...] = w2buf[slot].astype(BF16)

    @pl.when(valid)
    def _():
        x = x_ref[...]
        for blk in range(n_blk):
            sl = slice(blk * MXU_WIDTH, (blk + 1) * MXU_WIDTH)
            u = jnp.dot(x, w1s[:, sl], preferred_element_type=F32) + b1_ref[0, 0, :, sl]
            glu = jnp.minimum(u[:, :half], SWIGLU_LIMIT)
            lin = jnp.clip(u[:, half:], -SWIGLU_LIMIT, SWIGLU_LIMIT)
            hs[:, blk * half:(blk + 1) * half] = (glu * _sigmoid(SWIGLU_ALPHA * glu) * (lin + 1.0)).astype(BF16)
        y = jnp.dot(hs[...], w2s[...], preferred_element_type=F32) + b2_ref[0, 0]
        y_ref[...] = y.astype(y_ref.dtype)

    @pl.when(jnp.logical_not(valid))
    def _():
        y_ref[...] = jnp.zeros_like(y_ref)


def _deinterleave_perm():
    half = MXU_WIDTH // 2
    src = np.arange(MXU_WIDTH)
    dst = np.where(src % 2 == 0, src // 2, half + src // 2)
    p = np.zeros((MXU_WIDTH, MXU_WIDTH), np.float32)
    p[src, dst] = 1.0
    return jnp.asarray(p, BF16)


def _moe_dispatch(top_i):
    n, k = top_i.shape
    tm = MOE_ROW_TILE
    p_rows = n * k + N_EXPERTS * tm
    experts = jnp.arange(N_EXPERTS, dtype=jnp.int32)
    onehot = top_i[:, :, None] == experts
    sel = jnp.sum(onehot.astype(jnp.int32), axis=1)
    before = jnp.cumsum(sel, axis=0) - sel
    counts = jnp.sum(sel, axis=0)
    padded = ((counts + tm - 1) // tm) * tm
    ends = jnp.cumsum(padded)
    starts = ends - padded
    pos = jnp.sum(jnp.where(onehot, (before + starts)[:, None, :], 0), axis=-1)
    n_tiles = p_rows // tm
    tile_start = jnp.arange(n_tiles, dtype=jnp.int32) * tm
    tile_valid = (tile_start < ends[-1]).astype(jnp.int32)
    last_tile = ends[-1] // tm - 1
    tile_expert = jnp.sum(ends[None, :] <= jnp.minimum(tile_start, last_tile * tm)[:, None], axis=1).astype(jnp.int32)
    keys = jnp.sort((top_i * n + jnp.arange(n, dtype=jnp.int32)[:, None]).reshape(-1))
    tile_rank0 = tile_start - starts[tile_expert]
    rank = tile_rank0[:, None] + jnp.arange(tm, dtype=jnp.int32)[None, :]
    sorted_at = jnp.clip((jnp.cumsum(counts) - counts)[tile_expert][:, None] + rank, 0, n * k - 1)
    src_tok = jnp.where(rank < counts[tile_expert][:, None], keys[sorted_at.reshape(-1)].reshape(n_tiles, tm) % n, 0)
    is_first = jnp.concatenate([jnp.ones((1,), jnp.int32),
                                (tile_expert[1:] != tile_expert[:-1]).astype(jnp.int32)])
    slot = (jnp.cumsum(is_first) - 1) % 2
    later = jnp.logical_and(experts[None, :] > experts[:, None], (counts > 0)[None, :])
    next_of = jnp.min(jnp.where(later, experts[None, :], N_EXPERTS), axis=1)
    next_expert = jnp.where(next_of < N_EXPERTS, next_of, -1)[tile_expert]
    tables = (tile_expert, tile_valid, is_first, slot.astype(jnp.int32), next_expert.astype(jnp.int32))
    return pos, src_tok.reshape(-1), tables, p_rows


def moe_experts(h2, top_i, layer, w1, b1, w2, b2):
    n, d = h2.shape
    depth, e, _, two_f = w1.shape
    f = two_f // 2
    tm = MOE_ROW_TILE
    pos, src_tok, tables, p_rows = _moe_dispatch(top_i)
    xs = h2.at[lax.optimization_barrier(src_tok)].get(mode="promise_in_bounds")
    b1p = b1.reshape(depth, e, two_f // MXU_WIDTH, MXU_WIDTH // 2, 2).swapaxes(3, 4).reshape(depth, e, 1, two_f)
    expert_vec = lambda w: pl.BlockSpec((1, 1, 1, w), lambda t, te, *_: (layer, te[t], 0, 0))
    grid_spec = pltpu.PrefetchScalarGridSpec(
        num_scalar_prefetch=len(tables),
        grid=(p_rows // tm,),
        in_specs=[
            pl.BlockSpec((tm, d), lambda t, *_: (t, 0)),
            pl.BlockSpec(memory_space=pl.ANY),
            expert_vec(two_f),
            pl.BlockSpec(memory_space=pl.ANY),
            expert_vec(d),
            pl.BlockSpec((MXU_WIDTH, MXU_WIDTH), lambda t, *_: (0, 0)),
        ],
        out_specs=pl.BlockSpec((tm, d), lambda t, *_: (t, 0)),
        scratch_shapes=[pltpu.VMEM((2, d, two_f), F32), pltpu.VMEM((2, f, d), F32),
                        pltpu.SemaphoreType.DMA((2, 2)),
                        pltpu.VMEM((d, two_f), BF16), pltpu.VMEM((f, d), BF16), pltpu.VMEM((tm, f), BF16)],
    )
    ys = pl.pallas_call(
        functools.partial(_moe_kernel, layer=layer),
        grid_spec=grid_spec,
        out_shape=jax.ShapeDtypeStruct((p_rows, d), BF16),
        compiler_params=pltpu.CompilerParams(dimension_semantics=("arbitrary",),
                                             vmem_limit_bytes=48 * 1024 * 1024),
        name="moe_experts",
    )(*tables, xs, w1, b1p, w2, b2.reshape(depth, e, 1, d), _deinterleave_perm())
    return ys.at[lax.optimization_barrier(pos.T.reshape(-1))].get(mode="promise_in_bounds").reshape(TOP_K, n, d)


def _final_kernel(x1_ref, mod_ref, ys_ref, topw_ref, lng_ref, lnb_ref, o_ref):
    g2 = mod_ref[0, 5:6]
    moe = None
    for kth in range(TOP_K):
        term = ys_ref[kth].astype(F32) * topw_ref[:, kth:kth + 1]
        moe = term if moe is None else moe + term
    o_ref[...] = _layer_norm(ALPHA * x1_ref[...] + g2 * moe, lng_ref[...], lnb_ref[...])


def combine_and_norm(x1, mod_l, ys, top_w, ln_g, ln_b):
    tm = ROW_TILE
    return pl.pallas_call(
        _final_kernel,
        grid=(N_TILES,),
        in_specs=[pl.BlockSpec((tm, D_MODEL), lambda t: (t, 0)),
                  pl.BlockSpec((1, 6, D_MODEL), lambda t: (_mod_row(t), 0, 0)),
                  pl.BlockSpec((TOP_K, tm, D_MODEL), lambda t: (0, t, 0)),
                  pl.BlockSpec((tm, 128), lambda t: (t, 0)),
                  _full((1, D_MODEL)), _full((1, D_MODEL))],
        out_specs=pl.BlockSpec((tm, D_MODEL), lambda t: (t, 0)),
        out_shape=jax.ShapeDtypeStruct((N_TOK, D_MODEL), F32),
        compiler_params=pltpu.CompilerParams(dimension_semantics=("parallel",)),
        name="combine_and_norm",
    )(x1, mod_l, ys, top_w, ln_g.reshape(1, -1), ln_b.reshape(1, -1))


def kernel(x_prompt, x_sample, cache_a_k, cache_a_v, cache_b_ckv, cache_b_kpe, state_c, state_d, c,
           c_ctx, w_mod, b_mod, w_in, a_sink, b_q_norm, b_w_uq, b_kv_norm, b_w_ukv, c_w_gate, c_b_gate,
           c_norm, d_mu, d_w0, d_w2, d_a0, d_a2, d_g2, d_k_k, d_k_a, d_r_k, d_ln_g, d_ln_b, w_br, w_out,
           ln_g, ln_b, w_router, b_router, w_mlp1, b_mlp1, w_mlp2, b_mlp2):
    params = dict(b_q_norm=b_q_norm, b_w_uq=b_w_uq, b_kv_norm=b_kv_norm, b_w_ukv=b_w_ukv, c_w_gate=c_w_gate,
                  c_b_gate=c_b_gate, c_norm=c_norm, d_mu=d_mu, d_w0=d_w0, d_w2=d_w2, d_a0=d_a0, d_a2=d_a2, d_g2=d_g2,
                  d_k_k=d_k_k, d_k_a=d_k_a, d_r_k=d_r_k, d_ln_g=d_ln_g, d_ln_b=d_ln_b, ln_g=ln_g, ln_b=ln_b,
                  w_router=w_router, b_router=b_router)
    assert x_prompt.shape == (N_CTX_SEQ, CTX_LEN, D_MODEL) and x_sample.shape == (N_LAT_SEQ, LAT_LEN, D_MODEL)
    x = jnp.concatenate([x_prompt.reshape(N_CTX, D_MODEL), x_sample.reshape(-1, D_MODEL)], axis=0)
    cond8 = jnp.concatenate([c_ctx[None], c, jnp.zeros((8 - 1 - N_LAT_SEQ, D_MODEL), F32)], axis=0)
    mod = modulation_table(cond8, w_mod, b_mod)[:, :1 + N_LAT_SEQ].reshape(DEPTH, 1 + N_LAT_SEQ, 6, D_MODEL)
    tables = _rope_tables()
    new = {name: [] for name in ("a_k", "a_v", "b_ckv", "b_kpe", "c", "d")}
    for l in range(DEPTH):
        lp = {name: val[l] for name, val in params.items()}
        w_small = jnp.concatenate([w_in[l][:, _ORIG[nm][0]:_ORIG[nm][1]] for nm in _ORDER], axis=1).astype(BF16)
        w_g = w_in[l][:, G_START:].astype(BF16)
        w_br_l = w_br[l].astype(BF16)
        w_out_l = w_out[l].astype(BF16)

        (aq, ak, av, bq, bckv, bkpe, cq4, ck4, cla_f, cla_b, cv, cgate,
         r, v, kk, lw_f, lw_b, k_f, k_b, a_f, a_b, bonus, dgate) = mixer_prelude(x, mod[l], w_small, tables, lp)

        o_a = gqa_attention(aq, ak, av, a_sink[l], cache_a_k, cache_a_v, l)
        o_b = mla_attention(bq, bckv, bkpe, b_w_ukv[l], cache_b_ckv, cache_b_kpe, l)

        c_s0 = jnp.swapaxes(state_c[:, l], 3, 4).reshape(1, N_CHAIN, C_DV, C_DK)
        co_f, co_b, c_fin = _recurrence_calls(functools.partial(_gla_kernel, dot=_dot1), "gla", [cq4, ck4, cv],
                                              [cla_f], [cla_b], c_s0, (C_DV, C_DK), C_HEADS * C_DV)
        d_s0 = state_d[:, l].reshape(1, N_CHAIN, D_N, D_N)
        y_f, y_b, d_fin = _recurrence_calls(functools.partial(_rwkv_kernel, dot=_dot1), "rwkv7", [r, v, kk],
                                            [lw_f, k_f, a_f], [lw_b, k_b, a_b], d_s0, (D_N, D_N), D_HEADS * D_N)

        x1, h2, top_i, top_w = merge_and_route(x, mod[l], o_a, o_b, co_f, co_b, cgate, y_f, y_b, bonus, dgate,
                                               w_g, w_br_l, w_out_l, lp)
        ys = moe_experts(h2, top_i[:, :TOP_K], l, w_mlp1, b_mlp1, w_mlp2, b_mlp2)
        x = combine_and_norm(x1, mod[l], ys, top_w, ln_g[l, 1], ln_b[l, 1])

        new["a_k"].append(ak[:N_CTX].reshape(N_CTX_SEQ, CTX_LEN, A_KV_HEADS, A_HD).transpose(0, 2, 1, 3))
        new["a_v"].append(av[:N_CTX].reshape(N_CTX_SEQ, CTX_LEN, A_KV_HEADS, A_HD).transpose(0, 2, 1, 3))
        new["b_ckv"].append(bckv[:N_CTX].reshape(N_CTX_SEQ, CTX_LEN, B_KV_LORA))
        new["b_kpe"].append(bkpe[:N_CTX].reshape(N_CTX_SEQ, CTX_LEN, B_ROPE))
        new["c"].append(jnp.swapaxes(c_fin.reshape(N_CTX_SEQ, 2, C_HEADS, C_DV, C_DK), 3, 4))
        new["d"].append(d_fin.reshape(N_CTX_SEQ, 2, D_HEADS, D_N, D_N))
    y_prompt = x[:N_CTX].reshape(x_prompt.shape)
    y_sample = x[N_CTX:].reshape(x_sample.shape)
    return (y_prompt, y_sample, *(jnp.stack(new[name], axis=1) for name in ("a_k", "a_v", "b_ckv", "b_kpe", "c", "d")))
```

```python
import functools

import jax
import jax.numpy as jnp
import numpy as np
from jax import lax
from jax.experimental import pallas as pl
from jax.experimental.pallas import tpu as pltpu

F32 = jnp.float32
BF16 = jnp.bfloat16

MXU_WIDTH = 256
VMEM_LIMIT = 56 * 1024 * 1024

D_MODEL = 1024
DEPTH = 2
GRID_W = 64
ROPE_BASE = 10000.0
A_HEADS, A_KV_HEADS, A_HD = 4, 2, 64
B_HEADS, B_NOPE, B_ROPE, B_VD, B_Q_LORA, B_KV_LORA = 4, 64, 32, 64, 192, 128
C_HEADS, C_DK, C_DV, C_GATE_RANK, C_GATE_TEMP = 4, 32, 64, 16, 16.0
D_HEADS, D_N, D_DECAY_RANK, D_AAA_RANK, D_GATE_RANK, D_GN_EPS = 4, 64, 64, 64, 128, 64e-5
BRANCH_W = 256
N_BRANCH = 4
N_EXPERTS = 32
TOP_K = 4
SWIGLU_LIMIT = 7.0
SWIGLU_ALPHA = 1.702
ALPHA = (2 * DEPTH) ** 0.25
LN_EPS = 1e-5
RMS_EPS = 1e-6

N_CTX_SEQ, CTX_LEN = 16, 256
N_LAT_SEQ, LAT_LEN = 2, 2048
N_CTX = N_CTX_SEQ * CTX_LEN
N_TOK = N_CTX + N_LAT_SEQ * LAT_LEN
ROW_TILE = 256
N_TILES = N_TOK // ROW_TILE
CTX_TILES = N_CTX // ROW_TILE
LAT_TILES_PER_SEQ = LAT_LEN // ROW_TILE
N_SEQ = N_CTX_SEQ + N_LAT_SEQ

_ORIG = dict(aq=(0, 256), ak=(256, 384), av=(384, 512), bcq=(512, 704), bckv=(704, 832), bkpe=(832, 864),
             cq=(864, 992), ck=(992, 1120), cv=(1120, 1376), cog=(1376, 1632), caf=(1632, 1648), cab=(1648, 1664),
             zd=(1664, 2816))
_ORDER = ("aq", "ak", "av", "cq", "ck", "cv", "cog", "zd", "bcq", "caf", "cab", "bkpe", "bckv")
COL = {}
_off = 0
for _name in _ORDER:
    _w = _ORIG[_name][1] - _ORIG[_name][0]
    COL[_name] = (_off, _off + _w)
    _off += _w
SMALL_COLS = _off
G_START = 2816


def _cs(name):
    return slice(*COL[name])


def _split3(x):
    hi = x.astype(BF16)
    r1 = x - hi.astype(F32)
    mid = r1.astype(BF16)
    lo = (r1 - mid.astype(F32)).astype(BF16)
    return hi, mid, lo


def _split2(x):
    hi = x.astype(BF16)
    lo = (x - hi.astype(F32)).astype(BF16)
    return hi, lo


def _bdot(a, b, dims):
    return lax.dot_general(a, b, dims, preferred_element_type=F32)


_D2 = (((1,), (0,)), ((), ()))
_D2T = (((1,), (1,)), ((), ()))
_NN = (((2,), (1,)), ((0,), (0,)))
_NT = (((2,), (2,)), ((0,), (0,)))
_TN = (((1,), (1,)), ((0,), (0,)))


def _dot1(a, b, dims=_D2):
    return _bdot(a.astype(BF16), b.astype(BF16), dims)


def _dot3(a, b, dims=_D2):
    ah, al = _split2(a)
    bh, bl = _split2(b)
    return _bdot(ah, bh, dims) + (_bdot(ah, bl, dims) + _bdot(al, bh, dims))


def _dot_exact_lhs(a01, b, dims=_D2):
    a = a01.astype(BF16)
    h, m, l = _split3(b)
    return _bdot(a, h, dims) + (_bdot(a, m, dims) + _bdot(a, l, dims))


def _dot_exact_rhs(a, b01, dims=_D2):
    b = b01.astype(BF16)
    h, m, l = _split3(a)
    return _bdot(h, b, dims) + (_bdot(m, b, dims) + _bdot(l, b, dims))


def _dot6(a, b, dims=_D2):
    ah, am, al = _split3(a)
    bh, bm, bl = _split3(b)
    return (_bdot(ah, bh, dims) + (_bdot(ah, bm, dims) + _bdot(am, bh, dims))
            + (_bdot(am, bm, dims) + (_bdot(ah, bl, dims) + _bdot(al, bh, dims))))


def _sigmoid(x):
    return 1.0 / (1.0 + jnp.exp(-x))


def _softplus(x):
    return jnp.maximum(x, 0.0) + jnp.log(1.0 + jnp.exp(-jnp.abs(x)))


def _mod_row(t):
    return jnp.where(t < CTX_TILES, 0, 1 + (t - CTX_TILES) // LAT_TILES_PER_SEQ)


def _full(shape):
    nd = len(shape)
    return pl.BlockSpec(shape, lambda *_: (0,) * nd)


MOD_COL_TILE = 1536


def _mod_kernel(c_ref, w_ref, b_ref, o_ref):
    c = c_ref[...]
    o_ref[0] = _dot3(c * _sigmoid(c), w_ref[0]) + b_ref[0]


def modulation_table(cond8, w_mod, b_mod):
    depth, d, six_d = w_mod.shape
    return pl.pallas_call(
        _mod_kernel,
        grid=(depth, six_d // MOD_COL_TILE),
        in_specs=[pl.BlockSpec((8, d), lambda l, j: (0, 0)),
                  pl.BlockSpec((1, d, MOD_COL_TILE), lambda l, j: (l, 0, j)),
                  pl.BlockSpec((1, 1, MOD_COL_TILE), lambda l, j: (l, 0, j))],
        out_specs=pl.BlockSpec((1, 8, MOD_COL_TILE), lambda l, j: (l, 0, j)),
        out_shape=jax.ShapeDtypeStruct((depth, 8, six_d), F32),
        compiler_params=pltpu.CompilerParams(dimension_semantics=("parallel", "parallel")),
        name="modulation",
    )(cond8, w_mod, b_mod.reshape(depth, 1, six_d))


def _rot_pairs(x, half, lane_mod_base=0):
    w = x.shape[-1]
    lane = lax.broadcasted_iota(jnp.int32, (1, w), 1) - lane_mod_base
    first = (lane % (2 * half)) < half
    return jnp.where(first, -pltpu.roll(x, w - half, axis=1), pltpu.roll(x, half, axis=1))


def _pre_kernel(x_ref, xp_ref, xn_ref, mod_ref, w_ref, ca_ref, sa_ref, cb_ref, sb_ref, ck_ref, sk_ref,
                qnorm_ref, kvnorm_ref, wuq_ref, cwg_ref, cbg_ref, rep_ref, mu_ref, dw0_ref, dw2_ref, da0_ref,
                da2_ref, dg2_ref, dkk_ref, dka_ref, drk_ref, bd_ref,
                aq_o, ak_o, av_o, bq_o, bckv_o, bkpe_o, cq4_o, ck4_o, claf_o, clab_o, cv_o, cgate_o,
                r_o, v_o, kk_o, lwf_o, lwb_o, kf_o, kb_o, af_o, ab_o, bonus_o, dgate_o):
    t = pl.program_id(0)
    tm = x_ref.shape[0]
    sh1 = mod_ref[0, 0:1, :]
    sc1 = mod_ref[0, 1:2, :]

    def modulate(xv):
        return (xv * (1.0 + sc1) + sh1).astype(BF16)

    z = jnp.dot(modulate(x_ref[...]), w_ref[...], preferred_element_type=F32)

    aq = z[:, _cs("aq")]
    ak = z[:, _cs("ak")]
    aq_o[...] = aq * ca_ref[...] + _rot_pairs(aq, A_HD // 4) * sa_ref[...]
    ak_o[...] = ak * ca_ref[:, :ak.shape[1]] + _rot_pairs(ak, A_HD // 4) * sa_ref[:, :ak.shape[1]]
    av_o[...] = z[:, _cs("av")]

    bcq = z[:, _cs("bcq")]
    qn = bcq * lax.rsqrt(jnp.mean(bcq * bcq, axis=-1, keepdims=True) + RMS_EPS) * qnorm_ref[...]
    bq = _dot1(qn, wuq_ref[...])
    bq_o[...] = bq * cb_ref[...] + _rot_pairs(bq, B_ROPE // 4, lane_mod_base=B_NOPE) * sb_ref[...]
    bckv = z[:, _cs("bckv")]
    bckv_o[...] = bckv * lax.rsqrt(jnp.mean(bckv * bckv, axis=-1, keepdims=True) + RMS_EPS) * kvnorm_ref[...]
    kpe_lo = COL["bkpe"][0] // 128 * 128
    kblk = z[:, kpe_lo:kpe_lo + 128]
    kblk = kblk * ck_ref[...] + _rot_pairs(kblk, B_ROPE // 4) * sk_ref[...]
    bkpe_o[...] = kblk[:, COL["bkpe"][0] - kpe_lo:COL["bkpe"][1] - kpe_lo]

    rep = rep_ref[...]
    cq4_o[...] = _dot_exact_rhs(z[:, _cs("cq")] * (C_DK ** -0.5), rep)
    ck4_o[...] = _dot_exact_rhs(z[:, _cs("ck")], rep)
    cv_o[...] = z[:, _cs("cv")]
    cog = z[:, _cs("cog")]
    cgate_o[...] = cog * _sigmoid(cog)
    for direction, (name, out) in enumerate((("caf", claf_o), ("cab", clab_o))):
        pre = _dot3(z[:, _cs(name)], cwg_ref[direction]) + cbg_ref[direction]
        out[...] = _dot_exact_rhs(-_softplus(-pre) * (1.0 / C_GATE_TEMP), rep)

    zd_cols = _cs("zd")
    zd = z[:, zd_cols]
    wd = w_ref[:, zd_cols]
    j = (t - CTX_TILES) % LAT_TILES_PER_SEQ
    latent = t >= CTX_TILES
    has_prev = jnp.logical_and(latent, j != 0)
    has_next = jnp.logical_and(latent, j != LAT_TILES_PER_SEQ - 1)
    zp = jnp.dot(modulate(xp_ref[...]), wd, preferred_element_type=F32)
    zn = jnp.dot(modulate(xn_ref[...]), wd, preferred_element_type=F32)
    prev_row = jnp.where(has_prev, zp[7:8], 0.0)
    next_row = jnp.where(has_next, zn[0:1], 0.0)
    row = lax.broadcasted_iota(jnp.int32, (tm, 1), 0)
    up = jnp.where(row == 0, prev_row, pltpu.roll(zd, 1, axis=0))
    dn = jnp.where(row == tm - 1, next_row, pltpu.roll(zd, tm - 1, axis=0))
    zd = zd + (0.5 * (up + dn) - zd) * mu_ref[...]

    hn = D_HEADS * D_N
    d_r, d_k, d_v = zd[:, :hn], zd[:, hn:2 * hn], zd[:, 2 * hn:3 * hn]
    o = 3 * hn
    d_w = (zd[:, o:o + D_DECAY_RANK], zd[:, o + D_DECAY_RANK:o + 2 * D_DECAY_RANK])
    o += 2 * D_DECAY_RANK
    d_a = (zd[:, o:o + D_AAA_RANK], zd[:, o + D_AAA_RANK:o + 2 * D_AAA_RANK])
    o += 2 * D_AAA_RANK
    d_g = zd[:, o:o + D_GATE_RANK]
    bd = bd_ref[...]
    kk = d_k * dkk_ref[...]
    kk = kk / jnp.maximum(jnp.sqrt(_dot_exact_rhs(kk * kk, bd)), 1e-12)
    r_o[...] = d_r
    v_o[...] = d_v
    kk_o[...] = kk
    k_sum = None
    for direction, (lw_o, k_o, a_o) in enumerate(((lwf_o, kf_o, af_o), (lwb_o, kb_o, ab_o))):
        w_log = -_softplus(-(dw0_ref[direction] + _dot3(jnp.tanh(d_w[direction]), dw2_ref[direction]))) - 0.5
        lw_o[...] = -jnp.exp(w_log)
        a = _sigmoid(da0_ref[direction] + _dot3(d_a[direction], da2_ref[direction]))
        k_dir = d_k * (1.0 + (a - 1.0) * dka_ref[...])
        k_o[...] = k_dir
        a_o[...] = a
        k_sum = k_dir if k_sum is None else k_sum + k_dir
    bonus_o[...] = d_v * _dot_exact_rhs(d_r * drk_ref[...] * k_sum, bd)
    dgate_o[...] = _dot3(_sigmoid(d_g), dg2_ref[...])


def _rope_tables():
    pos = np.arange(LAT_LEN)
    rowp, colp = (pos // GRID_W).astype(np.float32), (pos % GRID_W).astype(np.float32)

    f32 = np.float32

    def head_tables(rot_dim):
        quarter = rot_dim // 4
        inv = (f32(ROPE_BASE) ** (-np.arange(quarter, dtype=f32) / f32(quarter))).astype(f32)
        ar = (rowp[:, None] * inv).astype(f32)
        ac = (colp[:, None] * inv).astype(f32)
        cos = np.concatenate([np.cos(ar), np.cos(ar), np.cos(ac), np.cos(ac)], axis=-1).astype(f32)
        sin = np.concatenate([np.sin(ar), np.sin(ar), np.sin(ac), np.sin(ac)], axis=-1).astype(f32)
        return cos, sin

    def with_identity(c, s):
        w = c.shape[1]
        return (jnp.asarray(np.concatenate([np.ones((ROW_TILE, w), f32), c], axis=0)),
                jnp.asarray(np.concatenate([np.zeros((ROW_TILE, w), f32), s], axis=0)))

    ca, sa = head_tables(A_HD)
    ca, sa = with_identity(np.tile(ca, (1, A_HEADS)), np.tile(sa, (1, A_HEADS)))
    cbh, sbh = head_tables(B_ROPE)
    ones, zeros = np.ones((LAT_LEN, B_NOPE), f32), np.zeros((LAT_LEN, B_NOPE), f32)
    cb, sb = with_identity(np.tile(np.concatenate([ones, cbh], axis=1), (1, B_HEADS)),
                           np.tile(np.concatenate([zeros, sbh], axis=1), (1, B_HEADS)))
    pad = 128 - B_ROPE
    ck, sk = with_identity(np.concatenate([np.ones((LAT_LEN, pad), f32), cbh], axis=1),
                           np.concatenate([np.zeros((LAT_LEN, pad), f32), sbh], axis=1))
    return ca, sa, cb, sb, ck, sk


def _lane_repeat_matrix():
    m = np.zeros((C_HEADS * C_DK, C_HEADS * 128), np.float32)
    for h in range(C_HEADS):
        for g in range(128 // C_DK):
            for d in range(C_DK):
                m[h * C_DK + d, h * 128 + g * C_DK + d] = 1.0
    return jnp.asarray(m, BF16)


def _head_block_diag():
    m = np.kron(np.eye(D_HEADS, dtype=np.float32), np.ones((D_N, D_N), np.float32))
    return jnp.asarray(m, BF16)


PRE_OUT_WIDTHS = (256, 128, 128, 384, 128, 32, 512, 512, 512, 512, 256, 256) + (256,) * 11


def mixer_prelude(x, mod_l, w_small, tables, lp):
    tm = ROW_TILE
    tab_idx = lambda t: (jnp.where(t < CTX_TILES, 0, 1 + (t - CTX_TILES) % LAT_TILES_PER_SEQ), 0)
    hn = D_HEADS * D_N
    small = [lp['b_q_norm'].reshape(1, -1), lp['b_kv_norm'].reshape(1, -1), lp['b_w_uq'], lp['c_w_gate'],
             lp['c_b_gate'].reshape(2, 1, -1), _lane_repeat_matrix(), lp['d_mu'].reshape(1, -1),
             lp['d_w0'].reshape(2, 1, hn), lp['d_w2'], lp['d_a0'].reshape(2, 1, hn), lp['d_a2'], lp['d_g2'],
             lp['d_k_k'].reshape(1, hn), lp['d_k_a'].reshape(1, hn), lp['d_r_k'].reshape(1, hn), _head_block_diag()]
    in_specs = ([pl.BlockSpec((tm, D_MODEL), lambda t: (t, 0)),
                 pl.BlockSpec((8, D_MODEL), lambda t: (jnp.maximum(t * (tm // 8) - 1, 0), 0)),
                 pl.BlockSpec((8, D_MODEL), lambda t: (jnp.minimum((t + 1) * (tm // 8), N_TOK // 8 - 1), 0)),
                 pl.BlockSpec((1, 6, D_MODEL), lambda t: (_mod_row(t), 0, 0)),
                 _full(w_small.shape)]
                + [pl.BlockSpec((tm, tab.shape[1]), tab_idx) for tab in tables]
                + [_full(a.shape) for a in small])
    return pl.pallas_call(
        _pre_kernel,
        grid=(N_TILES,),
        in_specs=in_specs,
        out_specs=[pl.BlockSpec((tm, w), lambda t: (t, 0)) for w in PRE_OUT_WIDTHS],
        out_shape=[jax.ShapeDtypeStruct((N_TOK, w), F32) for w in PRE_OUT_WIDTHS],
        compiler_params=pltpu.CompilerParams(dimension_semantics=("parallel",), vmem_limit_bytes=VMEM_LIMIT),
        name="mixer_prelude",
    )(x, x, x, mod_l, w_small, *tables, *small)


ATT_Q_BLOCK = 128
ATT_WINDOW = 128
ATT_NEG_INF = -1e30
CACHE_LEN = 512


def _softmax_pv(s, v, sink):
    m = jnp.max(s, axis=-1, keepdims=True)
    if sink is not None:
        m = jnp.maximum(m, sink)
    e = jnp.exp(s - m)
    den = jnp.sum(e, axis=-1, keepdims=True)
    if sink is not None:
        den = den + jnp.exp(sink - m)
    return jnp.dot(e.astype(BF16), v.astype(BF16), preferred_element_type=F32) / den


def _gqa_kernel(sink_ref, q_ref, k_ref, v_ref, *rest, hd, group, scale, windowed):
    if windowed:
        kp_ref, kn_ref, vp_ref, vn_ref, kc_ref, vc_ref, _, o_ref = rest
    else:
        (o_ref,) = rest
    i = pl.program_id(1)
    tq = q_ref.shape[0]
    n_kv = k_ref.shape[1] // hd
    if windowed:
        qpos = i * tq + lax.broadcasted_iota(jnp.int32, (tq, 3 * tq), 0)
        kpos = (i - 1) * tq + lax.broadcasted_iota(jnp.int32, (tq, 3 * tq), 1)
        n_tok = pl.num_programs(1) * tq
        mask = (jnp.abs(qpos - kpos) <= ATT_WINDOW) & (kpos >= 0) & (kpos < n_tok)
        mask = jnp.concatenate([mask] * group, axis=0)
    for kvh in range(n_kv):
        ks = slice(kvh * hd, (kvh + 1) * hd)
        qs = [q_ref[:, (kvh * group + g) * hd:(kvh * group + g + 1) * hd] for g in range(group)]
        q = (jnp.concatenate(qs, axis=0) * scale).astype(BF16)
        sink = jnp.concatenate(
            [jnp.full((tq, 1), sink_ref[kvh * group + g], F32) for g in range(group)], axis=0)
        if windowed:
            k_win = jnp.concatenate([kp_ref[:, ks], k_ref[:, ks], kn_ref[:, ks]], axis=0)
            v_win = jnp.concatenate([vp_ref[:, ks], v_ref[:, ks], vn_ref[:, ks]], axis=0)
            s_win = _bdot(q, k_win.astype(BF16), _D2T)
            s_win = jnp.where(mask, s_win, ATT_NEG_INF)
            s_ctx = _bdot(q, kc_ref[0, 0, kvh].astype(BF16), _D2T)
            s = jnp.concatenate([s_win, s_ctx], axis=1)
            v = jnp.concatenate([v_win, vc_ref[0, 0, kvh]], axis=0)
        else:
            s = _bdot(q, k_ref[:, ks].astype(BF16), _D2T)
            v = v_ref[:, ks]
        o = _softmax_pv(s, v, sink)
        for g in range(group):
            h = kvh * group + g
            o_ref[:, h * hd:(h + 1) * hd] = o[g * tq:(g + 1) * tq]


def gqa_attention(q, k, v, sink, cache_k, cache_v, layer):
    qw, kw = q.shape[1], k.shape[1]
    group = qw // kw
    scale = A_HD ** -0.5
    params = pltpu.CompilerParams(dimension_semantics=("parallel", "parallel"))
    out_shape = jax.ShapeDtypeStruct((N_TOK, qw), F32)
    ctx_spec = lambda w: pl.BlockSpec((CTX_LEN, w), lambda s, i, sk: (s, 0))
    o = pl.pallas_call(
        functools.partial(_gqa_kernel, hd=A_HD, group=group, scale=scale, windowed=False),
        grid_spec=pltpu.PrefetchScalarGridSpec(
            num_scalar_prefetch=1, grid=(N_CTX_SEQ, 1), in_specs=[ctx_spec(qw), ctx_spec(kw), ctx_spec(kw)],
            out_specs=ctx_spec(qw)),
        out_shape=out_shape, compiler_params=params, name="gqa_full",
    )(sink, q, k, v)
    tq = ATT_Q_BLOCK
    nb = LAT_LEN // tq
    base = N_CTX // tq
    blk = lambda w, f: pl.BlockSpec((tq, w), lambda b, i, sk: (base + nb * b + f(i), 0))
    same = lambda i: i
    prev = lambda i: jnp.maximum(i - 1, 0)
    nxt = lambda i: jnp.minimum(i + 1, nb - 1)
    cspec = pl.BlockSpec((1, 1) + cache_k.shape[2:], lambda b, i, sk: (b, layer, 0, 0, 0))
    return pl.pallas_call(
        functools.partial(_gqa_kernel, hd=A_HD, group=group, scale=scale, windowed=True),
        grid_spec=pltpu.PrefetchScalarGridSpec(
            num_scalar_prefetch=1, grid=(N_LAT_SEQ, nb),
            in_specs=[blk(qw, same), blk(kw, same), blk(kw, same), blk(kw, prev), blk(kw, nxt), blk(kw, prev),
                      blk(kw, nxt), cspec, cspec, pl.BlockSpec(memory_space=pl.ANY)],
            out_specs=blk(qw, same)),
        out_shape=out_shape, input_output_aliases={10: 0}, compiler_params=params, name="gqa_windowed",
    )(sink, q, k, v, k, k, v, v, cache_k, cache_v, o)


def _mla_kernel(q_ref, ckv_ref, kpe_ref, wukv_ref, *rest, n_heads, nope, rope, vd, scale, cached):
    if cached:
        cckv_ref, ckpe_ref, _, o_ref, kv_scr, kpe_scr = rest
    else:
        o_ref, kv_scr, kpe_scr = rest
    i = pl.program_id(1)
    n_cache = kv_scr.shape[0] - ckv_ref.shape[0]

    @pl.when(i == 0)
    def _():
        w = wukv_ref[...].astype(BF16)
        if cached:
            kv_scr[:n_cache] = jnp.dot(cckv_ref[0, 0].astype(BF16), w, preferred_element_type=F32).astype(BF16)
            kpe_scr[:n_cache] = ckpe_ref[0, 0].astype(BF16)
        kv_scr[n_cache:] = jnp.dot(ckv_ref[...].astype(BF16), w, preferred_element_type=F32).astype(BF16)
        kpe_scr[n_cache:] = kpe_ref[...].astype(BF16)

    kpe = kpe_scr[...]
    qd = nope + rope
    for h in range(n_heads):
        qn = (q_ref[:, h * qd:h * qd + nope] * scale).astype(BF16)
        qp = (q_ref[:, h * qd + nope:(h + 1) * qd] * scale).astype(BF16)
        k_n = kv_scr[:, h * (nope + vd):h * (nope + vd) + nope]
        v = kv_scr[:, h * (nope + vd) + nope:(h + 1) * (nope + vd)]
        s = _bdot(qn, k_n, _D2T) + _bdot(qp, kpe, _D2T)
        o_ref[:, h * vd:(h + 1) * vd] = _softmax_pv(s, v, None)


def mla_attention(q, ckv, kpe, w_ukv, cache_ckv, cache_kpe, layer):
    qw = q.shape[1]
    tq = ATT_Q_BLOCK
    kw = dict(n_heads=B_HEADS, nope=B_NOPE, rope=B_ROPE, vd=B_VD, scale=(B_NOPE + B_ROPE) ** -0.5)
    params = pltpu.CompilerParams(dimension_semantics=("parallel", "arbitrary"))
    out_shape = jax.ShapeDtypeStruct((N_TOK, B_HEADS * B_VD), F32)
    kvw = B_HEADS * (B_NOPE + B_VD)
    nbc = CTX_LEN // tq
    o = pl.pallas_call(
        functools.partial(_mla_kernel, cached=False, **kw),
        grid=(N_CTX_SEQ, nbc),
        in_specs=[pl.BlockSpec((tq, qw), lambda s, i: (s * nbc + i, 0)),
                  pl.BlockSpec((CTX_LEN, B_KV_LORA), lambda s, i: (s, 0)),
                  pl.BlockSpec((CTX_LEN, B_ROPE), lambda s, i: (s, 0)),
                  _full(w_ukv.shape)],
        out_specs=pl.BlockSpec((tq, B_HEADS * B_VD), lambda s, i: (s * nbc + i, 0)),
        out_shape=out_shape,
        scratch_shapes=[pltpu.VMEM((CTX_LEN, kvw), BF16), pltpu.VMEM((CTX_LEN, B_ROPE), BF16)],
        compiler_params=params, name="mla_context",
    )(q, ckv, kpe, w_ukv)
    nb = LAT_LEN // tq
    base = N_CTX // tq
    lat0 = N_CTX // LAT_LEN
    s_len = CACHE_LEN + LAT_LEN
    return pl.pallas_call(
        functools.partial(_mla_kernel, cached=True, **kw),
        grid=(N_LAT_SEQ, nb),
        in_specs=[pl.BlockSpec((tq, qw), lambda b, i: (base + nb * b + i, 0)),
                  pl.BlockSpec((LAT_LEN, B_KV_LORA), lambda b, i: (lat0 + b, 0)),
                  pl.BlockSpec((LAT_LEN, B_ROPE), lambda b, i: (lat0 + b, 0)),
                  _full(w_ukv.shape),
                  pl.BlockSpec((1, 1, CACHE_LEN, B_KV_LORA), lambda b, i: (b, layer, 0, 0)),
                  pl.BlockSpec((1, 1, CACHE_LEN, B_ROPE), lambda b, i: (b, layer, 0, 0)),
                  pl.BlockSpec(memory_space=pl.ANY)],
        out_specs=pl.BlockSpec((tq, B_HEADS * B_VD), lambda b, i: (base + nb * b + i, 0)),
        out_shape=out_shape, input_output_aliases={6: 0},
        scratch_shapes=[pltpu.VMEM((s_len, kvw), BF16), pltpu.VMEM((s_len, B_ROPE), BF16)],
        compiler_params=params, name="mla_latent",
    )(q, ckv, kpe, w_ukv, cache_ckv, cache_kpe, o)


CHUNK = 64
GLA_SUB = 16
PAIR = 2
N_CHAIN = PAIR * 2 * 4


def _is_back(shape):
    return (lax.broadcasted_iota(jnp.int32, shape, 0) // 4) % 2 == 1


def _chains(ref_f, ref_b, width):
    return jnp.stack([ref[0, s, 0, :, h * width:(h + 1) * width]
                      for s in range(PAIR) for ref in (ref_f, ref_b) for h in range(4)], axis=0)


def _unchain(y, o_f, o_b):
    for s in range(PAIR):
        o_f[0, s, 0] = jnp.concatenate([y[s * 8 + h] for h in range(4)], axis=-1)
        o_b[0, s, 0] = jnp.concatenate([y[s * 8 + 4 + h] for h in range(4)], axis=-1)


def _dir_masks(L):
    shape = (N_CHAIN, L, L)
    back = _is_back(shape)
    row = lax.broadcasted_iota(jnp.int32, shape, 1)
    col = lax.broadcasted_iota(jnp.int32, shape, 2)
    ahead = jnp.where(back, col - row, row - col)
    return ahead >= 0, ahead > 0, row == col


def _chunk_end(ci):
    L = ci.shape[1]
    return jnp.where(_is_back((N_CHAIN, 1, 1)), ci[:, 0:1], ci[:, L - 1:L])


def _split_refs(refs, n_in, has_s0, has_sfin):
    ins = refs[:n_in]
    pos = n_in
    s0_ref = None
    if has_s0:
        s0_ref = refs[pos]
        pos += 3
    of_ref, ob_ref = refs[pos], refs[pos + 1]
    pos += 2
    sfin_ref = refs[pos] if has_sfin else None
    return ins, s0_ref, of_ref, ob_ref, sfin_ref, refs[-1]


def _init_state(s_scr, s0_ref):
    @pl.when(pl.program_id(1) == 0)
    def _():
        if s0_ref is None:
            s_scr[...] = jnp.zeros_like(s_scr)
        else:
            s_scr[...] = s0_ref[0]


def _emit_state(sfin_ref, s_new):
    if sfin_ref is None:
        return

    @pl.when(pl.program_id(1) == pl.num_programs(1) - 1)
    def _():
        sfin_ref[0] = s_new


def _rwkv_kernel(*refs, dot, has_s0, has_sfin):
    (rf, rb, vf, vb, kkf, kkb, lwf, lwb, kf, kb, af, ab), s0_ref, yf_ref, yb_ref, sfin_ref, s_scr = _split_refs(
        refs, 12, has_s0, has_sfin)
    _init_state(s_scr, s0_ref)
    n = D_N
    r = _chains(rf, rb, n)
    v = _chains(vf, vb, n)
    kk = _chains(kkf, kkb, n)
    lw = _chains(lwf, lwb, n)
    k = _chains(kf, kb, n)
    a = _chains(af, ab, n)
    L = r.shape[1]
    S = s_scr[...]
    incl, strict, diag = _dir_masks(L)
    ci = _dot_exact_lhs(jnp.where(incl, 1.0, 0.0), lw, _NN)
    ce = ci - lw
    cl = _chunk_end(ci)
    e_neg = jnp.exp(-ci)
    b = a * kk
    alpha = kk * jnp.exp(ce)
    rho = r * jnp.exp(ci)
    beta = b * e_neg
    kappa = k * e_neg
    e_end = jnp.exp(cl - ci)
    ar = jnp.concatenate([alpha, rho], axis=1)
    bk = jnp.concatenate([beta, kappa], axis=1)
    w = dot(ar, bk, _NT)
    nmat = jnp.where(strict, w[:, :L, :L], 0.0)
    mmat = jnp.where(strict, w[:, :L, L:], 0.0)
    p1 = jnp.where(incl, w[:, L:, :L], 0.0)
    p2 = jnp.where(incl, w[:, L:, L:], 0.0)
    x = jnp.where(diag, 1.0, 0.0) - nmat
    p = dot(nmat, nmat, _NN)
    span = 2
    while True:
        x = x + dot(x, p, _NN)
        span *= 2
        if span >= L:
            break
        p = dot(p, p, _NN)
    us = dot(ar, S, _NT)
    rhs = us[:, :L] + dot(mmat, v, _NN)
    d = -dot(x, rhs, _NN)
    dv = jnp.concatenate([d, v], axis=1)
    pp = jnp.concatenate([p1, p2], axis=2)
    _unchain(us[:, L:] + dot(pp, dv, _NN), yf_ref, yb_ref)
    bk_end = jnp.concatenate([b * e_end, k * e_end], axis=1)
    s_new = S * jnp.exp(cl) + dot(dv, bk_end, _TN)
    s_scr[...] = s_new
    _emit_state(sfin_ref, s_new)


def _gla_kernel(*refs, dot, has_s0, has_sfin):
    (qf, qb, kf, kb, vf, vb, laf, lab), s0_ref, of_ref, ob_ref, sfin_ref, s_scr = _split_refs(
        refs, 8, has_s0, has_sfin)
    _init_state(s_scr, s0_ref)
    q4 = _chains(qf, qb, 128)
    k4 = _chains(kf, kb, 128)
    la4 = _chains(laf, lab, 128)
    v = _chains(vf, vb, C_DV)
    g, L, lanes = q4.shape
    dk = C_DK
    n_sub = L // GLA_SUB
    st = s_scr[...]
    incl, _, _ = _dir_masks(L)
    c = _dot_exact_lhs(jnp.where(incl, 1.0, 0.0), la4, _NN)
    shape = (g, L, lanes)
    back = _is_back(shape)
    lane_blk = lax.broadcasted_iota(jnp.int32, shape, 2) // dk
    row_blk = lax.broadcasted_iota(jnp.int32, shape, 1) // GLA_SUB
    cref_f = jnp.zeros(shape, F32)
    cref_b = jnp.zeros(shape, F32)
    for j in range(1, n_sub):
        cref_f = jnp.where(lane_blk == j, c[:, j * GLA_SUB - 1:j * GLA_SUB], cref_f)
        cref_b = jnp.where(lane_blk == j - 1, c[:, j * GLA_SUB:j * GLA_SUB + 1], cref_b)
    cref = jnp.where(back, cref_b, cref_f)
    q_on = row_blk == lane_blk
    k_on = jnp.where(back, row_blk - lane_blk, lane_blk - row_blk) >= 0
    qh = jnp.where(q_on, q4 * jnp.exp(jnp.where(q_on, c - cref, 0.0)), 0.0)
    kh = jnp.where(k_on, k4 * jnp.exp(jnp.where(k_on, cref - c, 0.0)), 0.0)
    att = jnp.where(incl, dot(qh, kh, _NT), 0.0)
    cl = _chunk_end(c)
    qe = (q4 * jnp.exp(c))[:, :, :dk]
    ke = (k4 * jnp.exp(cl - c))[:, :, :dk]
    _unchain(dot(qe, st, _NT) + dot(att, v, _NN), of_ref, ob_ref)
    s_new = st * jnp.exp(cl[:, :, :dk]) + dot(v, ke, _TN)
    s_scr[...] = s_new
    _emit_state(sfin_ref, s_new)


def _recurrence_calls(kernel_fn, name, pairs, singles_f, singles_b, s0_lat, state_dims, out_width):
    def run(view, grid, group, s0, prev_out):
        nc = view[2]
        fwd_map = lambda p, c: (group(p), 0, c, 0, 0)
        bwd_map = lambda p, c: (group(p), 0, nc - 1 - c, 0, 0)
        blk = lambda w: (1, PAIR, 1, CHUNK, w)
        args, in_specs = [], []
        for af, ab in [(a, a) for a in pairs] + list(zip(singles_f, singles_b)):
            w = af.shape[-1]
            args += [af.reshape(view + (w,)), ab.reshape(view + (w,))]
            in_specs += [pl.BlockSpec(blk(w), fwd_map), pl.BlockSpec(blk(w), bwd_map)]
        out_specs = [pl.BlockSpec(blk(out_width), fwd_map), pl.BlockSpec(blk(out_width), bwd_map)]
        out_shape = [jax.ShapeDtypeStruct(view + (out_width,), F32)] * 2
        aliases = {}
        if s0 is not None:
            args += [s0] + [o.reshape(view + (out_width,)) for o in prev_out]
            in_specs += [_full(s0.shape), pl.BlockSpec(memory_space=pl.ANY), pl.BlockSpec(memory_space=pl.ANY)]
            aliases = {len(args) - 2: 0, len(args) - 1: 1}
        else:
            out_specs.append(pl.BlockSpec((1, N_CHAIN) + state_dims, lambda p, c: (p, 0, 0, 0)))
            out_shape.append(jax.ShapeDtypeStruct((grid[0], N_CHAIN) + state_dims, F32))
        return pl.pallas_call(
            functools.partial(kernel_fn, has_s0=s0 is not None, has_sfin=s0 is None),
            grid=grid, in_specs=in_specs, out_specs=out_specs, out_shape=out_shape,
            input_output_aliases=aliases, scratch_shapes=[pltpu.VMEM((N_CHAIN,) + state_dims, F32)],
            compiler_params=pltpu.CompilerParams(dimension_semantics=("parallel", "arbitrary")),
            name=name + ("_latent" if s0 is not None else "_context"),
        )(*args)

    ctx_nc = CTX_LEN // CHUNK
    ctx_view = (N_TOK // (PAIR * CTX_LEN), PAIR, ctx_nc, CHUNK)
    o_f, o_b, s_fin = run(ctx_view, (N_CTX_SEQ // PAIR, ctx_nc), lambda p: p, None, None)
    lat_nc = LAT_LEN // CHUNK
    lat_view = (N_TOK // (PAIR * LAT_LEN), PAIR, lat_nc, CHUNK)
    o_f, o_b = run(lat_view, (1, lat_nc), lambda p: N_CTX // (PAIR * LAT_LEN), s0_lat, (o_f, o_b))
    return o_f.reshape(N_TOK, out_width), o_b.reshape(N_TOK, out_width), s_fin


def _layer_norm(x, g, b):
    mu = jnp.mean(x, axis=-1, keepdims=True)
    xc = x - mu
    var = jnp.mean(xc * xc, axis=-1, keepdims=True)
    return xc * lax.rsqrt(var + LN_EPS) * g + b


def _merge_kernel(x_ref, mod_ref, oa_ref, ob_ref, cof_ref, cob_ref, cgate_ref, yf_ref, yb_ref, bonus_ref, dgate_ref,
                  wg_ref, wbr_ref, wout_ref, cnorm_ref, dlng_ref, dlnb_ref, lng_ref, lnb_ref, wr_ref, br_ref, bd_ref,
                  x1_o, h2_o, topi_o, topw_o):
    x = x_ref[...]
    m = mod_ref[0]
    sh1, sc1, g1, sh2, sc2 = m[0:1], m[1:2], m[2:3], m[3:4], m[4:5]
    bd = bd_ref[...]
    inv_n = 1.0 / D_N
    co = cof_ref[...] + cob_ref[...]
    o_c = co * lax.rsqrt(_dot_exact_rhs(co * co, bd) * inv_n + RMS_EPS) * cnorm_ref[...] * cgate_ref[...]
    y = yf_ref[...] + yb_ref[...]
    yc = y - _dot_exact_rhs(y, bd) * inv_n
    var = _dot_exact_rhs(yc * yc, bd) * inv_n
    o_d = (yc * lax.rsqrt(var + D_GN_EPS) * dlng_ref[...] + dlnb_ref[...] + bonus_ref[...]) * dgate_ref[...]
    branches = (oa_ref[...], ob_ref[...], o_c, o_d)
    h = (x * (1.0 + sc1) + sh1).astype(BF16)
    merged = None
    for n in range(N_BRANCH):
        gate = _sigmoid(jnp.dot(h, wg_ref[:, n * D_MODEL:(n + 1) * D_MODEL], preferred_element_type=F32))
        term = gate * jnp.dot(branches[n].astype(BF16), wbr_ref[n], preferred_element_type=F32)
        merged = term if merged is None else merged + term
    mix = jnp.dot(merged.astype(BF16), wout_ref[...], preferred_element_type=F32)
    x1 = _layer_norm(ALPHA * x + g1 * mix, lng_ref[...], lnb_ref[...])
    x1_o[...] = x1
    h2 = x1 * (1.0 + sc2) + sh2
    h2_o[...] = h2.astype(BF16)
    logits = _dot6(h2, wr_ref[...]) + br_ref[...]
    tm, n_e = logits.shape
    lane_e = lax.broadcasted_iota(jnp.int32, (tm, n_e), 1)
    lane_o = lax.broadcasted_iota(jnp.int32, (tm, topi_o.shape[1]), 1)
    top_i = jnp.zeros((tm, topi_o.shape[1]), jnp.int32)
    top_v = jnp.zeros((tm, topw_o.shape[1]), F32)
    vals = []
    for kth in range(TOP_K):
        mx = jnp.max(logits, axis=-1, keepdims=True)
        idx = jnp.min(jnp.where(logits == mx, lane_e, n_e), axis=-1, keepdims=True)
        vals.append(mx)
        top_i = jnp.where(lane_o == kth, idx, top_i)
        logits = jnp.where(lane_e == idx, -jnp.inf, logits)
    es = [jnp.exp(vk - vals[0]) for vk in vals]
    den = es[0] + es[1] + es[2] + es[3]
    for kth in range(TOP_K):
        top_v = jnp.where(lane_o == kth, es[kth] / den, top_v)
    topi_o[...] = top_i
    topw_o[...] = top_v


def merge_and_route(x, mod_l, o_a, o_b, co_f, co_b, cgate, y_f, y_b, bonus, dgate, w_g, w_br, w_out, lp):
    tm = ROW_TILE
    hn = D_HEADS * D_N
    row = lambda w: pl.BlockSpec((tm, w), lambda t: (t, 0))
    small = [jnp.tile(lp['c_norm'], C_HEADS).reshape(1, hn), lp['d_ln_g'].reshape(1, hn), lp['d_ln_b'].reshape(1, hn),
             lp['ln_g'][0].reshape(1, -1), lp['ln_b'][0].reshape(1, -1), lp['w_router'],
             lp['b_router'].reshape(1, -1), _head_block_diag()]
    return pl.pallas_call(
        _merge_kernel,
        grid=(N_TILES,),
        in_specs=([row(D_MODEL), pl.BlockSpec((1, 6, D_MODEL), lambda t: (_mod_row(t), 0, 0))]
                  + [row(hn)] * 9 + [_full(w_g.shape), _full(w_br.shape), _full(w_out.shape)]
                  + [_full(a.shape) for a in small]),
        out_specs=[row(D_MODEL), row(D_MODEL), row(128), row(128)],
        out_shape=[jax.ShapeDtypeStruct((N_TOK, D_MODEL), F32), jax.ShapeDtypeStruct((N_TOK, D_MODEL), BF16),
                   jax.ShapeDtypeStruct((N_TOK, 128), jnp.int32), jax.ShapeDtypeStruct((N_TOK, 128), F32)],
        compiler_params=pltpu.CompilerParams(dimension_semantics=("parallel",), vmem_limit_bytes=VMEM_LIMIT),
        name="merge_and_route",
    )(x, mod_l, o_a, o_b, co_f, co_b, cgate, y_f, y_b, bonus, dgate, w_g, w_br, w_out, *small)


MOE_ROW_TILE = 256


def _moe_kernel(te_ref, tv_ref, first_ref, slot_ref, next_ref, x_ref, w1_hbm, b1_ref, w2_hbm, b2_ref, perm_ref, y_ref,
                w1buf, w2buf, sem, w1s, w2s, hs, *, layer):
    t = pl.program_id(0)
    valid = tv_ref[t] != 0
    d_model, two_f = w1s.shape
    n_blk = two_f // MXU_WIDTH
    half = MXU_WIDTH // 2

    def fetch(expert, slot):
        return (pltpu.make_async_copy(w1_hbm.at[layer, expert], w1buf.at[slot], sem.at[0, slot]),
                pltpu.make_async_copy(w2_hbm.at[layer, expert], w2buf.at[slot], sem.at[1, slot]))

    @pl.when(t == 0)
    def _():
        for cp in fetch(te_ref[0], 0):
            cp.start()

    @pl.when(first_ref[t] == 1)
    def _():
        slot = slot_ref[t]
        for cp in fetch(te_ref[t], slot):
            cp.wait()

        @pl.when(next_ref[t] >= 0)
        def _():
            for cp in fetch(next_ref[t], 1 - slot):
                cp.start()

        for blk in range(n_blk):
            sl = slice(blk * MXU_WIDTH, (blk + 1) * MXU_WIDTH)
            wb = w1buf[slot, :, sl].astype(BF16)
            w1s[:, sl] = jnp.dot(wb, perm_ref[...], preferred_element_type=F32).astype(BF16)
        w2s[...] = w2buf[slot].astype(BF16)

    @pl.when(valid)
    def _():
        x = x_ref[...]
        for blk in range(n_blk):
            sl = slice(blk * MXU_WIDTH, (blk + 1) * MXU_WIDTH)
            u = jnp.dot(x, w1s[:, sl], preferred_element_type=F32) + b1_ref[0, 0, :, sl]
            glu = jnp.minimum(u[:, :half], SWIGLU_LIMIT)
            lin = jnp.clip(u[:, half:], -SWIGLU_LIMIT, SWIGLU_LIMIT)
            hs[:, blk * half:(blk + 1) * half] = (glu * _sigmoid(SWIGLU_ALPHA * glu) * (lin + 1.0)).astype(BF16)
        y = jnp.dot(hs[...], w2s[...], preferred_element_type=F32) + b2_ref[0, 0]
        y_ref[...] = y.astype(y_ref.dtype)

    @pl.when(jnp.logical_not(valid))
    def _():
        y_ref[...] = jnp.zeros_like(y_ref)


def _deinterleave_perm():
    half = MXU_WIDTH // 2
    src = np.arange(MXU_WIDTH)
    dst = np.where(src % 2 == 0, src // 2, half + src // 2)
    p = np.zeros((MXU_WIDTH, MXU_WIDTH), np.float32)
    p[src, dst] = 1.0
    return jnp.asarray(p, BF16)


def _moe_dispatch(top_i):
    n, k = top_i.shape
    tm = MOE_ROW_TILE
    p_rows = n * k + N_EXPERTS * tm
    experts = jnp.arange(N_EXPERTS, dtype=jnp.int32)
    onehot = top_i[:, :, None] == experts
    sel = jnp.sum(onehot.astype(jnp.int32), axis=1)
    before = jnp.cumsum(sel, axis=0) - sel
    counts = jnp.sum(sel, axis=0)
    padded = ((counts + tm - 1) // tm) * tm
    ends = jnp.cumsum(padded)
    starts = ends - padded
    pos = jnp.sum(jnp.where(onehot, (before + starts)[:, None, :], 0), axis=-1)
    n_tiles = p_rows // tm
    tile_start = jnp.arange(n_tiles, dtype=jnp.int32) * tm
    tile_valid = (tile_start < ends[-1]).astype(jnp.int32)
    last_tile = ends[-1] // tm - 1
    tile_expert = jnp.sum(ends[None, :] <= jnp.minimum(tile_start, last_tile * tm)[:, None], axis=1).astype(jnp.int32)
    keys = jnp.sort((top_i * n + jnp.arange(n, dtype=jnp.int32)[:, None]).reshape(-1))
    tile_onehot = tile_expert[:, None] == experts[None, :]
    lookup = lambda table: jnp.sum(jnp.where(tile_onehot, table[None, :], 0), axis=1)
    tile_rank0 = tile_start - lookup(starts)
    rank = tile_rank0[:, None] + jnp.arange(tm, dtype=jnp.int32)[None, :]
    tile_at = jnp.clip(lookup(jnp.cumsum(counts) - counts) + tile_rank0, 0, n * k - 1)
    keys_ext = jnp.concatenate([keys, jnp.zeros((tm,), jnp.int32)])
    tile_keys = jax.vmap(lambda o: lax.dynamic_slice(keys_ext, (o,), (tm,)))(tile_at)
    src_tok = jnp.where(rank < lookup(counts)[:, None], tile_keys % n, 0)
    is_first = jnp.concatenate([jnp.ones((1,), jnp.int32),
                                (tile_expert[1:] != tile_expert[:-1]).astype(jnp.int32)])
    slot = (jnp.cumsum(is_first) - 1) % 2
    later = jnp.logical_and(experts[None, :] > experts[:, None], (counts > 0)[None, :])
    next_of = jnp.min(jnp.where(later, experts[None, :], N_EXPERTS), axis=1)
    next_expert = lookup(jnp.where(next_of < N_EXPERTS, next_of, -1))
    tables = (tile_expert, tile_valid, is_first, slot.astype(jnp.int32), next_expert.astype(jnp.int32))
    return pos, src_tok.reshape(-1), tables, p_rows


def moe_experts(h2, top_i, layer, w1, b1, w2, b2):
    n, d = h2.shape
    depth, e, _, two_f = w1.shape
    f = two_f // 2
    tm = MOE_ROW_TILE
    pos, src_tok, tables, p_rows = _moe_dispatch(top_i)
    h2_big = jnp.concatenate([h2, jnp.zeros((p_rows - n, d), BF16)], axis=0)
    xs = h2_big.at[lax.optimization_barrier(src_tok)].get(mode="promise_in_bounds")
    b1p = b1.reshape(depth, e, two_f // MXU_WIDTH, MXU_WIDTH // 2, 2).swapaxes(3, 4).reshape(depth, e, 1, two_f)
    expert_vec = lambda w: pl.BlockSpec((1, 1, 1, w), lambda t, te, *_: (layer, te[t], 0, 0))
    grid_spec = pltpu.PrefetchScalarGridSpec(
        num_scalar_prefetch=len(tables),
        grid=(p_rows // tm,),
        in_specs=[
            pl.BlockSpec((tm, d), lambda t, *_: (t, 0)),
            pl.BlockSpec(memory_space=pl.ANY),
            expert_vec(two_f),
            pl.BlockSpec(memory_space=pl.ANY),
            expert_vec(d),
            pl.BlockSpec((MXU_WIDTH, MXU_WIDTH), lambda t, *_: (0, 0)),
        ],
        out_specs=pl.BlockSpec((tm, d), lambda t, *_: (t, 0)),
        scratch_shapes=[pltpu.VMEM((2, d, two_f), F32), pltpu.VMEM((2, f, d), F32),
                        pltpu.SemaphoreType.DMA((2, 2)),
                        pltpu.VMEM((d, two_f), BF16), pltpu.VMEM((f, d), BF16), pltpu.VMEM((tm, f), BF16)],
    )
    ys = pl.pallas_call(
        functools.partial(_moe_kernel, layer=layer),
        grid_spec=grid_spec,
        out_shape=jax.ShapeDtypeStruct((p_rows, d), BF16),
        compiler_params=pltpu.CompilerParams(dimension_semantics=("arbitrary",),
                                             vmem_limit_bytes=48 * 1024 * 1024),
        name="moe_experts",
    )(*tables, xs, w1, b1p, w2, b2.reshape(depth, e, 1, d), _deinterleave_perm())
    return ys.at[lax.optimization_barrier(pos.T.reshape(-1))].get(mode="promise_in_bounds").reshape(TOP_K, n, d)


def _final_kernel(x1_ref, mod_ref, ys_ref, topw_ref, lng_ref, lnb_ref, o_ref):
    g2 = mod_ref[0, 5:6]
    moe = None
    for kth in range(TOP_K):
        term = ys_ref[kth].astype(F32) * topw_ref[:, kth:kth + 1]
        moe = term if moe is None else moe + term
    o_ref[...] = _layer_norm(ALPHA * x1_ref[...] + g2 * moe, lng_ref[...], lnb_ref[...])


def combine_and_norm(x1, mod_l, ys, top_w, ln_g, ln_b):
    tm = ROW_TILE
    return pl.pallas_call(
        _final_kernel,
        grid=(N_TILES,),
        in_specs=[pl.BlockSpec((tm, D_MODEL), lambda t: (t, 0)),
                  pl.BlockSpec((1, 6, D_MODEL), lambda t: (_mod_row(t), 0, 0)),
                  pl.BlockSpec((TOP_K, tm, D_MODEL), lambda t: (0, t, 0)),
                  pl.BlockSpec((tm, 128), lambda t: (t, 0)),
                  _full((1, D_MODEL)), _full((1, D_MODEL))],
        out_specs=pl.BlockSpec((tm, D_MODEL), lambda t: (t, 0)),
        out_shape=jax.ShapeDtypeStruct((N_TOK, D_MODEL), F32),
        compiler_params=pltpu.CompilerParams(dimension_semantics=("parallel",)),
        name="combine_and_norm",
    )(x1, mod_l, ys, top_w, ln_g.reshape(1, -1), ln_b.reshape(1, -1))


def kernel(x_prompt, x_sample, cache_a_k, cache_a_v, cache_b_ckv, cache_b_kpe, state_c, state_d, c,
           c_ctx, w_mod, b_mod, w_in, a_sink, b_q_norm, b_w_uq, b_kv_norm, b_w_ukv, c_w_gate, c_b_gate,
           c_norm, d_mu, d_w0, d_w2, d_a0, d_a2, d_g2, d_k_k, d_k_a, d_r_k, d_ln_g, d_ln_b, w_br, w_out,
           ln_g, ln_b, w_router, b_router, w_mlp1, b_mlp1, w_mlp2, b_mlp2):
    params = dict(b_q_norm=b_q_norm, b_w_uq=b_w_uq, b_kv_norm=b_kv_norm, b_w_ukv=b_w_ukv, c_w_gate=c_w_gate,
                  c_b_gate=c_b_gate, c_norm=c_norm, d_mu=d_mu, d_w0=d_w0, d_w2=d_w2, d_a0=d_a0, d_a2=d_a2, d_g2=d_g2,
                  d_k_k=d_k_k, d_k_a=d_k_a, d_r_k=d_r_k, d_ln_g=d_ln_g, d_ln_b=d_ln_b, ln_g=ln_g, ln_b=ln_b,
                  w_router=w_router, b_router=b_router)
    assert x_prompt.shape == (N_CTX_SEQ, CTX_LEN, D_MODEL) and x_sample.shape == (N_LAT_SEQ, LAT_LEN, D_MODEL)
    x = jnp.concatenate([x_prompt.reshape(N_CTX, D_MODEL), x_sample.reshape(-1, D_MODEL)], axis=0)
    cond8 = jnp.concatenate([c_ctx[None], c, jnp.zeros((8 - 1 - N_LAT_SEQ, D_MODEL), F32)], axis=0)
    mod = modulation_table(cond8, w_mod, b_mod)[:, :1 + N_LAT_SEQ].reshape(DEPTH, 1 + N_LAT_SEQ, 6, D_MODEL)
    tables = _rope_tables()
    new = {name: [] for name in ("a_k", "a_v", "b_ckv", "b_kpe", "c", "d")}
    for l in range(DEPTH):
        lp = {name: val[l] for name, val in params.items()}
        w_small = jnp.concatenate([w_in[l][:, _ORIG[nm][0]:_ORIG[nm][1]] for nm in _ORDER], axis=1).astype(BF16)
        w_g = w_in[l][:, G_START:].astype(BF16)
        w_br_l = w_br[l].astype(BF16)
        w_out_l = w_out[l].astype(BF16)

        (aq, ak, av, bq, bckv, bkpe, cq4, ck4, cla_f, cla_b, cv, cgate,
         r, v, kk, lw_f, lw_b, k_f, k_b, a_f, a_b, bonus, dgate) = mixer_prelude(x, mod[l], w_small, tables, lp)

        o_a = gqa_attention(aq, ak, av, a_sink[l], cache_a_k, cache_a_v, l)
        o_b = mla_attention(bq, bckv, bkpe, b_w_ukv[l], cache_b_ckv, cache_b_kpe, l)

        c_s0 = jnp.swapaxes(state_c[:, l], 3, 4).reshape(1, N_CHAIN, C_DV, C_DK)
        co_f, co_b, c_fin = _recurrence_calls(functools.partial(_gla_kernel, dot=_dot1), "gla", [cq4, ck4, cv],
                                              [cla_f], [cla_b], c_s0, (C_DV, C_DK), C_HEADS * C_DV)
        d_s0 = state_d[:, l].reshape(1, N_CHAIN, D_N, D_N)
        y_f, y_b, d_fin = _recurrence_calls(functools.partial(_rwkv_kernel, dot=_dot1), "rwkv7", [r, v, kk],
                                            [lw_f, k_f, a_f], [lw_b, k_b, a_b], d_s0, (D_N, D_N), D_HEADS * D_N)

        x1, h2, top_i, top_w = merge_and_route(x, mod[l], o_a, o_b, co_f, co_b, cgate, y_f, y_b, bonus, dgate,
                                               w_g, w_br_l, w_out_l, lp)
        ys = moe_experts(h2, top_i[:, :TOP_K], l, w_mlp1, b_mlp1, w_mlp2, b_mlp2)
        x = combine_and_norm(x1, mod[l], ys, top_w, ln_g[l, 1], ln_b[l, 1])

        new["a_k"].append(ak[:N_CTX].reshape(N_CTX_SEQ, CTX_LEN, A_KV_HEADS, A_HD).transpose(0, 2, 1, 3))
        new["a_v"].append(av[:N_CTX].reshape(N_CTX_SEQ, CTX_LEN, A_KV_HEADS, A_HD).transpose(0, 2, 1, 3))
        new["b_ckv"].append(bckv[:N_CTX].reshape(N_CTX_SEQ, CTX_LEN, B_KV_LORA))
        new["b_kpe"].append(bkpe[:N_CTX].reshape(N_CTX_SEQ, CTX_LEN, B_ROPE))
        new["c"].append(jnp.swapaxes(c_fin.reshape(N_CTX_SEQ, 2, C_HEADS, C_DV, C_DK), 3, 4))
        new["d"].append(d_fin.reshape(N_CTX_SEQ, 2, D_HEADS, D_N, D_N))
    y_prompt = x[:N_CTX].reshape(x_prompt.shape)
    y_sample = x[N_CTX:].reshape(x_sample.shape)
    return (y_prompt, y_sample, *(jnp.stack(new[name], axis=1) for name in ("a_k", "a_v", "b_ckv", "b_kpe", "c", "d")))
```

```python
import functools

import jax
import jax.numpy as jnp
import numpy as np
from jax import lax
from jax.experimental import pallas as pl
from jax.experimental.pallas import tpu as pltpu

F32 = jnp.float32
BF16 = jnp.bfloat16

MXU_WIDTH = 256
VMEM_LIMIT = 56 * 1024 * 1024

D_MODEL = 1024
DEPTH = 2
GRID_W = 64
ROPE_BASE = 10000.0
A_HEADS, A_KV_HEADS, A_HD = 4, 2, 64
B_HEADS, B_NOPE, B_ROPE, B_VD, B_Q_LORA, B_KV_LORA = 4, 64, 32, 64, 192, 128
C_HEADS, C_DK, C_DV, C_GATE_RANK, C_GATE_TEMP = 4, 32, 64, 16, 16.0
D_HEADS, D_N, D_DECAY_RANK, D_AAA_RANK, D_GATE_RANK, D_GN_EPS = 4, 64, 64, 64, 128, 64e-5
BRANCH_W = 256
N_BRANCH = 4
N_EXPERTS = 32
TOP_K = 4
SWIGLU_LIMIT = 7.0
SWIGLU_ALPHA = 1.702
ALPHA = (2 * DEPTH) ** 0.25
LN_EPS = 1e-5
RMS_EPS = 1e-6

N_CTX_SEQ, CTX_LEN = 16, 256
N_LAT_SEQ, LAT_LEN = 2, 2048
N_CTX = N_CTX_SEQ * CTX_LEN
N_TOK = N_CTX + N_LAT_SEQ * LAT_LEN
ROW_TILE = 256
N_TILES = N_TOK // ROW_TILE
CTX_TILES = N_CTX // ROW_TILE
LAT_TILES_PER_SEQ = LAT_LEN // ROW_TILE
N_SEQ = N_CTX_SEQ + N_LAT_SEQ
MOE_ROW_TILE = 256
MOE_ROWS = N_TOK * TOP_K + N_EXPERTS * MOE_ROW_TILE

_ORIG = dict(aq=(0, 256), ak=(256, 384), av=(384, 512), bcq=(512, 704), bckv=(704, 832), bkpe=(832, 864),
             cq=(864, 992), ck=(992, 1120), cv=(1120, 1376), cog=(1376, 1632), caf=(1632, 1648), cab=(1648, 1664),
             zd=(1664, 2816))
_ORDER = ("aq", "ak", "av", "cq", "ck", "cv", "cog", "zd", "bcq", "caf", "cab", "bkpe", "bckv")
COL = {}
_off = 0
for _name in _ORDER:
    _w = _ORIG[_name][1] - _ORIG[_name][0]
    COL[_name] = (_off, _off + _w)
    _off += _w
SMALL_COLS = _off
G_START = 2816


def _cs(name):
    return slice(*COL[name])


def _split3(x):
    hi = x.astype(BF16)
    r1 = x - hi.astype(F32)
    mid = r1.astype(BF16)
    lo = (r1 - mid.astype(F32)).astype(BF16)
    return hi, mid, lo


def _split2(x):
    hi = x.astype(BF16)
    lo = (x - hi.astype(F32)).astype(BF16)
    return hi, lo


def _bdot(a, b, dims):
    return lax.dot_general(a, b, dims, preferred_element_type=F32)


_D2 = (((1,), (0,)), ((), ()))
_D2T = (((1,), (1,)), ((), ()))
_NN = (((2,), (1,)), ((0,), (0,)))
_NT = (((2,), (2,)), ((0,), (0,)))
_TN = (((1,), (1,)), ((0,), (0,)))


def _dot1(a, b, dims=_D2):
    return _bdot(a.astype(BF16), b.astype(BF16), dims)


def _dot3(a, b, dims=_D2):
    ah, al = _split2(a)
    bh, bl = _split2(b)
    return _bdot(ah, bh, dims) + (_bdot(ah, bl, dims) + _bdot(al, bh, dims))


def _dot_exact_lhs(a01, b, dims=_D2):
    a = a01.astype(BF16)
    h, m, l = _split3(b)
    return _bdot(a, h, dims) + (_bdot(a, m, dims) + _bdot(a, l, dims))


def _dot_exact_rhs(a, b01, dims=_D2):
    b = b01.astype(BF16)
    h, m, l = _split3(a)
    return _bdot(h, b, dims) + (_bdot(m, b, dims) + _bdot(l, b, dims))


def _dot6(a, b, dims=_D2):
    ah, am, al = _split3(a)
    bh, bm, bl = _split3(b)
    return (_bdot(ah, bh, dims) + (_bdot(ah, bm, dims) + _bdot(am, bh, dims))
            + (_bdot(am, bm, dims) + (_bdot(ah, bl, dims) + _bdot(al, bh, dims))))


def _sigmoid(x):
    return 1.0 / (1.0 + jnp.exp(-x))


def _softplus(x):
    return jnp.maximum(x, 0.0) + jnp.log(1.0 + jnp.exp(-jnp.abs(x)))


def _mod_row(t):
    return jnp.where(t < CTX_TILES, 0, 1 + (t - CTX_TILES) // LAT_TILES_PER_SEQ)


def _full(shape):
    nd = len(shape)
    return pl.BlockSpec(shape, lambda *_: (0,) * nd)


MOD_COL_TILE = 1536


def _mod_kernel(c_ref, w_ref, b_ref, o_ref):
    c = c_ref[...]
    o_ref[0] = _dot3(c * _sigmoid(c), w_ref[0]) + b_ref[0]


def modulation_table(cond8, w_mod, b_mod):
    depth, d, six_d = w_mod.shape
    return pl.pallas_call(
        _mod_kernel,
        grid=(depth, six_d // MOD_COL_TILE),
        in_specs=[pl.BlockSpec((8, d), lambda l, j: (0, 0)),
                  pl.BlockSpec((1, d, MOD_COL_TILE), lambda l, j: (l, 0, j)),
                  pl.BlockSpec((1, 1, MOD_COL_TILE), lambda l, j: (l, 0, j))],
        out_specs=pl.BlockSpec((1, 8, MOD_COL_TILE), lambda l, j: (l, 0, j)),
        out_shape=jax.ShapeDtypeStruct((depth, 8, six_d), F32),
        compiler_params=pltpu.CompilerParams(dimension_semantics=("parallel", "parallel")),
        name="modulation",
    )(cond8, w_mod, b_mod.reshape(depth, 1, six_d))


def _rot_pairs(x, half, lane_mod_base=0):
    w = x.shape[-1]
    lane = lax.broadcasted_iota(jnp.int32, (1, w), 1) - lane_mod_base
    first = (lane % (2 * half)) < half
    return jnp.where(first, -pltpu.roll(x, w - half, axis=1), pltpu.roll(x, half, axis=1))


def _pre_kernel(x_ref, xp_ref, xn_ref, mod_ref, w_ref, ca_ref, sa_ref, cb_ref, sb_ref, ck_ref, sk_ref,
                qnorm_ref, kvnorm_ref, wuq_ref, cwg_ref, cbg_ref, rep_ref, mu_ref, dw0_ref, dw2_ref, da0_ref,
                da2_ref, dg2_ref, dkk_ref, dka_ref, drk_ref, bd_ref,
                aq_o, ak_o, av_o, bq_o, bckv_o, bkpe_o, cq4_o, ck4_o, claf_o, clab_o, cv_o, cgate_o,
                r_o, v_o, kk_o, lwf_o, lwb_o, kf_o, kb_o, af_o, ab_o, bonus_o, dgate_o):
    t = pl.program_id(0)
    tm = x_ref.shape[0]
    sh1 = mod_ref[0, 0:1, :]
    sc1 = mod_ref[0, 1:2, :]

    def modulate(xv):
        return (xv * (1.0 + sc1) + sh1).astype(BF16)

    z = jnp.dot(modulate(x_ref[...]), w_ref[...], preferred_element_type=F32)

    aq = z[:, _cs("aq")]
    ak = z[:, _cs("ak")]
    aq_o[...] = aq * ca_ref[...] + _rot_pairs(aq, A_HD // 4) * sa_ref[...]
    ak_o[...] = ak * ca_ref[:, :ak.shape[1]] + _rot_pairs(ak, A_HD // 4) * sa_ref[:, :ak.shape[1]]
    av_o[...] = z[:, _cs("av")]

    bcq = z[:, _cs("bcq")]
    qn = bcq * lax.rsqrt(jnp.mean(bcq * bcq, axis=-1, keepdims=True) + RMS_EPS) * qnorm_ref[...]
    bq = _dot1(qn, wuq_ref[...])
    bq_o[...] = bq * cb_ref[...] + _rot_pairs(bq, B_ROPE // 4, lane_mod_base=B_NOPE) * sb_ref[...]
    bckv = z[:, _cs("bckv")]
    bckv_o[...] = bckv * lax.rsqrt(jnp.mean(bckv * bckv, axis=-1, keepdims=True) + RMS_EPS) * kvnorm_ref[...]
    kpe_lo = COL["bkpe"][0] // 128 * 128
    kblk = z[:, kpe_lo:kpe_lo + 128]
    kblk = kblk * ck_ref[...] + _rot_pairs(kblk, B_ROPE // 4) * sk_ref[...]
    bkpe_o[...] = kblk[:, COL["bkpe"][0] - kpe_lo:COL["bkpe"][1] - kpe_lo]

    rep = rep_ref[...]
    cq4_o[...] = _dot_exact_rhs(z[:, _cs("cq")] * (C_DK ** -0.5), rep)
    ck4_o[...] = _dot_exact_rhs(z[:, _cs("ck")], rep)
    cv_o[...] = z[:, _cs("cv")]
    cog = z[:, _cs("cog")]
    cgate_o[...] = cog * _sigmoid(cog)
    for direction, (name, out) in enumerate((("caf", claf_o), ("cab", clab_o))):
        pre = _dot3(z[:, _cs(name)], cwg_ref[direction]) + cbg_ref[direction]
        out[...] = _dot_exact_rhs(-_softplus(-pre) * (1.0 / C_GATE_TEMP), rep)

    zd_cols = _cs("zd")
    zd = z[:, zd_cols]
    wd = w_ref[:, zd_cols]
    j = (t - CTX_TILES) % LAT_TILES_PER_SEQ
    latent = t >= CTX_TILES
    has_prev = jnp.logical_and(latent, j != 0)
    has_next = jnp.logical_and(latent, j != LAT_TILES_PER_SEQ - 1)
    zp = jnp.dot(modulate(xp_ref[...]), wd, preferred_element_type=F32)
    zn = jnp.dot(modulate(xn_ref[...]), wd, preferred_element_type=F32)
    prev_row = jnp.where(has_prev, zp[7:8], 0.0)
    next_row = jnp.where(has_next, zn[0:1], 0.0)
    row = lax.broadcasted_iota(jnp.int32, (tm, 1), 0)
    up = jnp.where(row == 0, prev_row, pltpu.roll(zd, 1, axis=0))
    dn = jnp.where(row == tm - 1, next_row, pltpu.roll(zd, tm - 1, axis=0))
    zd = zd + (0.5 * (up + dn) - zd) * mu_ref[...]

    hn = D_HEADS * D_N
    d_r, d_k, d_v = zd[:, :hn], zd[:, hn:2 * hn], zd[:, 2 * hn:3 * hn]
    o = 3 * hn
    d_w = (zd[:, o:o + D_DECAY_RANK], zd[:, o + D_DECAY_RANK:o + 2 * D_DECAY_RANK])
    o += 2 * D_DECAY_RANK
    d_a = (zd[:, o:o + D_AAA_RANK], zd[:, o + D_AAA_RANK:o + 2 * D_AAA_RANK])
    o += 2 * D_AAA_RANK
    d_g = zd[:, o:o + D_GATE_RANK]
    bd = bd_ref[...]
    kk = d_k * dkk_ref[...]
    kk = kk / jnp.maximum(jnp.sqrt(_dot_exact_rhs(kk * kk, bd)), 1e-12)
    r_o[...] = d_r
    v_o[...] = d_v
    kk_o[...] = kk
    k_sum = None
    for direction, (lw_o, k_o, a_o) in enumerate(((lwf_o, kf_o, af_o), (lwb_o, kb_o, ab_o))):
        w_log = -_softplus(-(dw0_ref[direction] + _dot3(jnp.tanh(d_w[direction]), dw2_ref[direction]))) - 0.5
        lw_o[...] = -jnp.exp(w_log)
        a = _sigmoid(da0_ref[direction] + _dot3(d_a[direction], da2_ref[direction]))
        k_dir = d_k * (1.0 + (a - 1.0) * dka_ref[...])
        k_o[...] = k_dir
        a_o[...] = a
        k_sum = k_dir if k_sum is None else k_sum + k_dir
    bonus_o[...] = d_v * _dot_exact_rhs(d_r * drk_ref[...] * k_sum, bd)
    dgate_o[...] = _dot3(_sigmoid(d_g), dg2_ref[...])


def _rope_tables():
    pos = np.arange(LAT_LEN)
    rowp, colp = (pos // GRID_W).astype(np.float32), (pos % GRID_W).astype(np.float32)

    f32 = np.float32

    def head_tables(rot_dim):
        quarter = rot_dim // 4
        inv = (f32(ROPE_BASE) ** (-np.arange(quarter, dtype=f32) / f32(quarter))).astype(f32)
        ar = (rowp[:, None] * inv).astype(f32)
        ac = (colp[:, None] * inv).astype(f32)
        cos = np.concatenate([np.cos(ar), np.cos(ar), np.cos(ac), np.cos(ac)], axis=-1).astype(f32)
        sin = np.concatenate([np.sin(ar), np.sin(ar), np.sin(ac), np.sin(ac)], axis=-1).astype(f32)
        return cos, sin

    def with_identity(c, s):
        w = c.shape[1]
        return (jnp.asarray(np.concatenate([np.ones((ROW_TILE, w), f32), c], axis=0)),
                jnp.asarray(np.concatenate([np.zeros((ROW_TILE, w), f32), s], axis=0)))

    ca, sa = head_tables(A_HD)
    ca, sa = with_identity(np.tile(ca, (1, A_HEADS)), np.tile(sa, (1, A_HEADS)))
    cbh, sbh = head_tables(B_ROPE)
    ones, zeros = np.ones((LAT_LEN, B_NOPE), f32), np.zeros((LAT_LEN, B_NOPE), f32)
    cb, sb = with_identity(np.tile(np.concatenate([ones, cbh], axis=1), (1, B_HEADS)),
                           np.tile(np.concatenate([zeros, sbh], axis=1), (1, B_HEADS)))
    pad = 128 - B_ROPE
    ck, sk = with_identity(np.concatenate([np.ones((LAT_LEN, pad), f32), cbh], axis=1),
                           np.concatenate([np.zeros((LAT_LEN, pad), f32), sbh], axis=1))
    return ca, sa, cb, sb, ck, sk


def _lane_repeat_matrix():
    m = np.zeros((C_HEADS * C_DK, C_HEADS * 128), np.float32)
    for h in range(C_HEADS):
        for g in range(128 // C_DK):
            for d in range(C_DK):
                m[h * C_DK + d, h * 128 + g * C_DK + d] = 1.0
    return jnp.asarray(m, BF16)


def _head_block_diag():
    m = np.kron(np.eye(D_HEADS, dtype=np.float32), np.ones((D_N, D_N), np.float32))
    return jnp.asarray(m, BF16)


PRE_OUT_WIDTHS = (256, 128, 128, 384, 128, 32, 512, 512, 512, 512, 256, 256) + (256,) * 11


def mixer_prelude(x, mod_l, w_small, tables, lp):
    tm = ROW_TILE
    tab_idx = lambda t: (jnp.where(t < CTX_TILES, 0, 1 + (t - CTX_TILES) % LAT_TILES_PER_SEQ), 0)
    hn = D_HEADS * D_N
    small = [lp['b_q_norm'].reshape(1, -1), lp['b_kv_norm'].reshape(1, -1), lp['b_w_uq'], lp['c_w_gate'],
             lp['c_b_gate'].reshape(2, 1, -1), _lane_repeat_matrix(), lp['d_mu'].reshape(1, -1),
             lp['d_w0'].reshape(2, 1, hn), lp['d_w2'], lp['d_a0'].reshape(2, 1, hn), lp['d_a2'], lp['d_g2'],
             lp['d_k_k'].reshape(1, hn), lp['d_k_a'].reshape(1, hn), lp['d_r_k'].reshape(1, hn), _head_block_diag()]
    in_specs = ([pl.BlockSpec((tm, D_MODEL), lambda t: (t, 0)),
                 pl.BlockSpec((8, D_MODEL), lambda t: (jnp.maximum(t * (tm // 8) - 1, 0), 0)),
                 pl.BlockSpec((8, D_MODEL), lambda t: (jnp.minimum((t + 1) * (tm // 8), N_TOK // 8 - 1), 0)),
                 pl.BlockSpec((1, 6, D_MODEL), lambda t: (_mod_row(t), 0, 0)),
                 _full(w_small.shape)]
                + [pl.BlockSpec((tm, tab.shape[1]), tab_idx) for tab in tables]
                + [_full(a.shape) for a in small])
    return pl.pallas_call(
        _pre_kernel,
        grid=(N_TILES,),
        in_specs=in_specs,
        out_specs=[pl.BlockSpec((tm, w), lambda t: (t, 0)) for w in PRE_OUT_WIDTHS],
        out_shape=[jax.ShapeDtypeStruct((N_TOK, w), F32) for w in PRE_OUT_WIDTHS],
        compiler_params=pltpu.CompilerParams(dimension_semantics=("parallel",), vmem_limit_bytes=VMEM_LIMIT),
        name="mixer_prelude",
    )(x, x, x, mod_l, w_small, *tables, *small)


ATT_Q_BLOCK = 128
ATT_WINDOW = 128
ATT_NEG_INF = -1e30
CACHE_LEN = 512


def _softmax_pv(s, v, sink):
    m = jnp.max(s, axis=-1, keepdims=True)
    if sink is not None:
        m = jnp.maximum(m, sink)
    e = jnp.exp(s - m)
    den = jnp.sum(e, axis=-1, keepdims=True)
    if sink is not None:
        den = den + jnp.exp(sink - m)
    return jnp.dot(e.astype(BF16), v.astype(BF16), preferred_element_type=F32) / den


def _gqa_kernel(sink_ref, q_ref, k_ref, v_ref, *rest, hd, group, scale, windowed):
    if windowed:
        kp_ref, kn_ref, vp_ref, vn_ref, kc_ref, vc_ref, _, o_ref = rest
    else:
        (o_ref,) = rest
    i = pl.program_id(1)
    tq = q_ref.shape[0]
    n_kv = k_ref.shape[1] // hd
    if windowed:
        qpos = i * tq + lax.broadcasted_iota(jnp.int32, (tq, 3 * tq), 0)
        kpos = (i - 1) * tq + lax.broadcasted_iota(jnp.int32, (tq, 3 * tq), 1)
        n_tok = pl.num_programs(1) * tq
        mask = (jnp.abs(qpos - kpos) <= ATT_WINDOW) & (kpos >= 0) & (kpos < n_tok)
        mask = jnp.concatenate([mask] * group, axis=0)
    for kvh in range(n_kv):
        ks = slice(kvh * hd, (kvh + 1) * hd)
        qs = [q_ref[:, (kvh * group + g) * hd:(kvh * group + g + 1) * hd] for g in range(group)]
        q = (jnp.concatenate(qs, axis=0) * scale).astype(BF16)
        sink = jnp.concatenate(
            [jnp.full((tq, 1), sink_ref[kvh * group + g], F32) for g in range(group)], axis=0)
        if windowed:
            k_win = jnp.concatenate([kp_ref[:, ks], k_ref[:, ks], kn_ref[:, ks]], axis=0)
            v_win = jnp.concatenate([vp_ref[:, ks], v_ref[:, ks], vn_ref[:, ks]], axis=0)
            s_win = _bdot(q, k_win.astype(BF16), _D2T)
            s_win = jnp.where(mask, s_win, ATT_NEG_INF)
            s_ctx = _bdot(q, kc_ref[0, 0, kvh].astype(BF16), _D2T)
            s = jnp.concatenate([s_win, s_ctx], axis=1)
            v = jnp.concatenate([v_win, vc_ref[0, 0, kvh]], axis=0)
        else:
            s = _bdot(q, k_ref[:, ks].astype(BF16), _D2T)
            v = v_ref[:, ks]
        o = _softmax_pv(s, v, sink)
        for g in range(group):
            h = kvh * group + g
            o_ref[:, h * hd:(h + 1) * hd] = o[g * tq:(g + 1) * tq]


def gqa_attention(q, k, v, sink, cache_k, cache_v, layer):
    qw, kw = q.shape[1], k.shape[1]
    group = qw // kw
    scale = A_HD ** -0.5
    params = pltpu.CompilerParams(dimension_semantics=("parallel", "parallel"))
    out_shape = jax.ShapeDtypeStruct((N_TOK, qw), F32)
    ctx_spec = lambda w: pl.BlockSpec((CTX_LEN, w), lambda s, i, sk: (s, 0))
    o = pl.pallas_call(
        functools.partial(_gqa_kernel, hd=A_HD, group=group, scale=scale, windowed=False),
        grid_spec=pltpu.PrefetchScalarGridSpec(
            num_scalar_prefetch=1, grid=(N_CTX_SEQ, 1), in_specs=[ctx_spec(qw), ctx_spec(kw), ctx_spec(kw)],
            out_specs=ctx_spec(qw)),
        out_shape=out_shape, compiler_params=params, name="gqa_full",
    )(sink, q, k, v)
    tq = ATT_Q_BLOCK
    nb = LAT_LEN // tq
    base = N_CTX // tq
    blk = lambda w, f: pl.BlockSpec((tq, w), lambda b, i, sk: (base + nb * b + f(i), 0))
    same = lambda i: i
    prev = lambda i: jnp.maximum(i - 1, 0)
    nxt = lambda i: jnp.minimum(i + 1, nb - 1)
    cspec = pl.BlockSpec((1, 1) + cache_k.shape[2:], lambda b, i, sk: (b, layer, 0, 0, 0))
    return pl.pallas_call(
        functools.partial(_gqa_kernel, hd=A_HD, group=group, scale=scale, windowed=True),
        grid_spec=pltpu.PrefetchScalarGridSpec(
            num_scalar_prefetch=1, grid=(N_LAT_SEQ, nb),
            in_specs=[blk(qw, same), blk(kw, same), blk(kw, same), blk(kw, prev), blk(kw, nxt), blk(kw, prev),
                      blk(kw, nxt), cspec, cspec, pl.BlockSpec(memory_space=pl.ANY)],
            out_specs=blk(qw, same)),
        out_shape=out_shape, input_output_aliases={10: 0}, compiler_params=params, name="gqa_windowed",
    )(sink, q, k, v, k, k, v, v, cache_k, cache_v, o)


def _mla_kernel(q_ref, ckv_ref, kpe_ref, wukv_ref, *rest, n_heads, nope, rope, vd, scale, cached):
    if cached:
        cckv_ref, ckpe_ref, _, o_ref, kv_scr, kpe_scr = rest
    else:
        o_ref, kv_scr, kpe_scr = rest
    i = pl.program_id(1)
    n_cache = kv_scr.shape[0] - ckv_ref.shape[0]

    @pl.when(i == 0)
    def _():
        w = wukv_ref[...].astype(BF16)
        if cached:
            kv_scr[:n_cache] = jnp.dot(cckv_ref[0, 0].astype(BF16), w, preferred_element_type=F32).astype(BF16)
            kpe_scr[:n_cache] = ckpe_ref[0, 0].astype(BF16)
        kv_scr[n_cache:] = jnp.dot(ckv_ref[...].astype(BF16), w, preferred_element_type=F32).astype(BF16)
        kpe_scr[n_cache:] = kpe_ref[...].astype(BF16)

    kpe = kpe_scr[...]
    qd = nope + rope
    for h in range(n_heads):
        qn = (q_ref[:, h * qd:h * qd + nope] * scale).astype(BF16)
        qp = (q_ref[:, h * qd + nope:(h + 1) * qd] * scale).astype(BF16)
        k_n = kv_scr[:, h * (nope + vd):h * (nope + vd) + nope]
        v = kv_scr[:, h * (nope + vd) + nope:(h + 1) * (nope + vd)]
        s = _bdot(qn, k_n, _D2T) + _bdot(qp, kpe, _D2T)
        o_ref[:, h * vd:(h + 1) * vd] = _softmax_pv(s, v, None)


def mla_attention(q, ckv, kpe, w_ukv, cache_ckv, cache_kpe, layer):
    qw = q.shape[1]
    tq = ATT_Q_BLOCK
    kw = dict(n_heads=B_HEADS, nope=B_NOPE, rope=B_ROPE, vd=B_VD, scale=(B_NOPE + B_ROPE) ** -0.5)
    params = pltpu.CompilerParams(dimension_semantics=("parallel", "arbitrary"))
    out_shape = jax.ShapeDtypeStruct((N_TOK, B_HEADS * B_VD), F32)
    kvw = B_HEADS * (B_NOPE + B_VD)
    nbc = CTX_LEN // tq
    o = pl.pallas_call(
        functools.partial(_mla_kernel, cached=False, **kw),
        grid=(N_CTX_SEQ, nbc),
        in_specs=[pl.BlockSpec((tq, qw), lambda s, i: (s * nbc + i, 0)),
                  pl.BlockSpec((CTX_LEN, B_KV_LORA), lambda s, i: (s, 0)),
                  pl.BlockSpec((CTX_LEN, B_ROPE), lambda s, i: (s, 0)),
                  _full(w_ukv.shape)],
        out_specs=pl.BlockSpec((tq, B_HEADS * B_VD), lambda s, i: (s * nbc + i, 0)),
        out_shape=out_shape,
        scratch_shapes=[pltpu.VMEM((CTX_LEN, kvw), BF16), pltpu.VMEM((CTX_LEN, B_ROPE), BF16)],
        compiler_params=params, name="mla_context",
    )(q, ckv, kpe, w_ukv)
    nb = LAT_LEN // tq
    base = N_CTX // tq
    lat0 = N_CTX // LAT_LEN
    s_len = CACHE_LEN + LAT_LEN
    return pl.pallas_call(
        functools.partial(_mla_kernel, cached=True, **kw),
        grid=(N_LAT_SEQ, nb),
        in_specs=[pl.BlockSpec((tq, qw), lambda b, i: (base + nb * b + i, 0)),
                  pl.BlockSpec((LAT_LEN, B_KV_LORA), lambda b, i: (lat0 + b, 0)),
                  pl.BlockSpec((LAT_LEN, B_ROPE), lambda b, i: (lat0 + b, 0)),
                  _full(w_ukv.shape),
                  pl.BlockSpec((1, 1, CACHE_LEN, B_KV_LORA), lambda b, i: (b, layer, 0, 0)),
                  pl.BlockSpec((1, 1, CACHE_LEN, B_ROPE), lambda b, i: (b, layer, 0, 0)),
                  pl.BlockSpec(memory_space=pl.ANY)],
        out_specs=pl.BlockSpec((tq, B_HEADS * B_VD), lambda b, i: (base + nb * b + i, 0)),
        out_shape=out_shape, input_output_aliases={6: 0},
        scratch_shapes=[pltpu.VMEM((s_len, kvw), BF16), pltpu.VMEM((s_len, B_ROPE), BF16)],
        compiler_params=params, name="mla_latent",
    )(q, ckv, kpe, w_ukv, cache_ckv, cache_kpe, o)


CHUNK = 64
GLA_SUB = 16
PAIR = 2
N_CHAIN = PAIR * 2 * 4


def _is_back(shape):
    return (lax.broadcasted_iota(jnp.int32, shape, 0) // 4) % 2 == 1


def _chains(ref_f, ref_b, width):
    return jnp.stack([ref[0, s, 0, :, h * width:(h + 1) * width]
                      for s in range(PAIR) for ref in (ref_f, ref_b) for h in range(4)], axis=0)


def _unchain(y, o_f, o_b):
    for s in range(PAIR):
        o_f[0, s, 0] = jnp.concatenate([y[s * 8 + h] for h in range(4)], axis=-1)
        o_b[0, s, 0] = jnp.concatenate([y[s * 8 + 4 + h] for h in range(4)], axis=-1)


def _dir_masks(L):
    shape = (N_CHAIN, L, L)
    back = _is_back(shape)
    row = lax.broadcasted_iota(jnp.int32, shape, 1)
    col = lax.broadcasted_iota(jnp.int32, shape, 2)
    ahead = jnp.where(back, col - row, row - col)
    return ahead >= 0, ahead > 0, row == col


def _chunk_end(ci):
    L = ci.shape[1]
    return jnp.where(_is_back((N_CHAIN, 1, 1)), ci[:, 0:1], ci[:, L - 1:L])


def _split_refs(refs, n_in, has_s0, has_sfin):
    ins = refs[:n_in]
    pos = n_in
    s0_ref = None
    if has_s0:
        s0_ref = refs[pos]
        pos += 3
    of_ref, ob_ref = refs[pos], refs[pos + 1]
    pos += 2
    sfin_ref = refs[pos] if has_sfin else None
    return ins, s0_ref, of_ref, ob_ref, sfin_ref, refs[-1]


def _init_state(s_scr, s0_ref):
    @pl.when(pl.program_id(1) == 0)
    def _():
        if s0_ref is None:
            s_scr[...] = jnp.zeros_like(s_scr)
        else:
            s_scr[...] = s0_ref[0]


def _emit_state(sfin_ref, s_new):
    if sfin_ref is None:
        return

    @pl.when(pl.program_id(1) == pl.num_programs(1) - 1)
    def _():
        sfin_ref[0] = s_new


def _rwkv_kernel(*refs, dot, has_s0, has_sfin):
    (rf, rb, vf, vb, kkf, kkb, lwf, lwb, kf, kb, af, ab), s0_ref, yf_ref, yb_ref, sfin_ref, s_scr = _split_refs(
        refs, 12, has_s0, has_sfin)
    _init_state(s_scr, s0_ref)
    n = D_N
    r = _chains(rf, rb, n)
    v = _chains(vf, vb, n)
    kk = _chains(kkf, kkb, n)
    lw = _chains(lwf, lwb, n)
    k = _chains(kf, kb, n)
    a = _chains(af, ab, n)
    L = r.shape[1]
    S = s_scr[...]
    incl, strict, diag = _dir_masks(L)
    ci = _dot_exact_lhs(jnp.where(incl, 1.0, 0.0), lw, _NN)
    ce = ci - lw
    cl = _chunk_end(ci)
    e_neg = jnp.exp(-ci)
    b = a * kk
    alpha = kk * jnp.exp(ce)
    rho = r * jnp.exp(ci)
    beta = b * e_neg
    kappa = k * e_neg
    e_end = jnp.exp(cl - ci)
    ar = jnp.concatenate([alpha, rho], axis=1)
    bk = jnp.concatenate([beta, kappa], axis=1)
    w = dot(ar, bk, _NT)
    nmat = jnp.where(strict, w[:, :L, :L], 0.0)
    mmat = jnp.where(strict, w[:, :L, L:], 0.0)
    p1 = jnp.where(incl, w[:, L:, :L], 0.0)
    p2 = jnp.where(incl, w[:, L:, L:], 0.0)
    x = jnp.where(diag, 1.0, 0.0) - nmat
    p = dot(nmat, nmat, _NN)
    span = 2
    while True:
        x = x + dot(x, p, _NN)
        span *= 2
        if span >= L:
            break
        p = dot(p, p, _NN)
    us = dot(ar, S, _NT)
    rhs = us[:, :L] + dot(mmat, v, _NN)
    d = -dot(x, rhs, _NN)
    dv = jnp.concatenate([d, v], axis=1)
    pp = jnp.concatenate([p1, p2], axis=2)
    _unchain(us[:, L:] + dot(pp, dv, _NN), yf_ref, yb_ref)
    bk_end = jnp.concatenate([b * e_end, k * e_end], axis=1)
    s_new = S * jnp.exp(cl) + dot(dv, bk_end, _TN)
    s_scr[...] = s_new
    _emit_state(sfin_ref, s_new)


def _gla_kernel(*refs, dot, has_s0, has_sfin):
    (qf, qb, kf, kb, vf, vb, laf, lab), s0_ref, of_ref, ob_ref, sfin_ref, s_scr = _split_refs(
        refs, 8, has_s0, has_sfin)
    _init_state(s_scr, s0_ref)
    q4 = _chains(qf, qb, 128)
    k4 = _chains(kf, kb, 128)
    la4 = _chains(laf, lab, 128)
    v = _chains(vf, vb, C_DV)
    g, L, lanes = q4.shape
    dk = C_DK
    n_sub = L // GLA_SUB
    st = s_scr[...]
    incl, _, _ = _dir_masks(L)
    c = _dot_exact_lhs(jnp.where(incl, 1.0, 0.0), la4, _NN)
    shape = (g, L, lanes)
    back = _is_back(shape)
    lane_blk = lax.broadcasted_iota(jnp.int32, shape, 2) // dk
    row_blk = lax.broadcasted_iota(jnp.int32, shape, 1) // GLA_SUB
    cref_f = jnp.zeros(shape, F32)
    cref_b = jnp.zeros(shape, F32)
    for j in range(1, n_sub):
        cref_f = jnp.where(lane_blk == j, c[:, j * GLA_SUB - 1:j * GLA_SUB], cref_f)
        cref_b = jnp.where(lane_blk == j - 1, c[:, j * GLA_SUB:j * GLA_SUB + 1], cref_b)
    cref = jnp.where(back, cref_b, cref_f)
    q_on = row_blk == lane_blk
    k_on = jnp.where(back, row_blk - lane_blk, lane_blk - row_blk) >= 0
    qh = jnp.where(q_on, q4 * jnp.exp(jnp.where(q_on, c - cref, 0.0)), 0.0)
    kh = jnp.where(k_on, k4 * jnp.exp(jnp.where(k_on, cref - c, 0.0)), 0.0)
    att = jnp.where(incl, dot(qh, kh, _NT), 0.0)
    cl = _chunk_end(c)
    qe = (q4 * jnp.exp(c))[:, :, :dk]
    ke = (k4 * jnp.exp(cl - c))[:, :, :dk]
    _unchain(dot(qe, st, _NT) + dot(att, v, _NN), of_ref, ob_ref)
    s_new = st * jnp.exp(cl[:, :, :dk]) + dot(v, ke, _TN)
    s_scr[...] = s_new
    _emit_state(sfin_ref, s_new)


def _recurrence_calls(kernel_fn, name, pairs, singles_f, singles_b, s0_lat, state_dims, out_width):
    def run(view, grid, group, s0, prev_out):
        nc = view[2]
        fwd_map = lambda p, c: (group(p), 0, c, 0, 0)
        bwd_map = lambda p, c: (group(p), 0, nc - 1 - c, 0, 0)
        blk = lambda w: (1, PAIR, 1, CHUNK, w)
        args, in_specs = [], []
        for af, ab in [(a, a) for a in pairs] + list(zip(singles_f, singles_b)):
            w = af.shape[-1]
            args += [af.reshape(view + (w,)), ab.reshape(view + (w,))]
            in_specs += [pl.BlockSpec(blk(w), fwd_map), pl.BlockSpec(blk(w), bwd_map)]
        out_specs = [pl.BlockSpec(blk(out_width), fwd_map), pl.BlockSpec(blk(out_width), bwd_map)]
        out_shape = [jax.ShapeDtypeStruct(view + (out_width,), F32)] * 2
        aliases = {}
        if s0 is not None:
            args += [s0] + [o.reshape(view + (out_width,)) for o in prev_out]
            in_specs += [_full(s0.shape), pl.BlockSpec(memory_space=pl.ANY), pl.BlockSpec(memory_space=pl.ANY)]
            aliases = {len(args) - 2: 0, len(args) - 1: 1}
        else:
            out_specs.append(pl.BlockSpec((1, N_CHAIN) + state_dims, lambda p, c: (p, 0, 0, 0)))
            out_shape.append(jax.ShapeDtypeStruct((grid[0], N_CHAIN) + state_dims, F32))
        return pl.pallas_call(
            functools.partial(kernel_fn, has_s0=s0 is not None, has_sfin=s0 is None),
            grid=grid, in_specs=in_specs, out_specs=out_specs, out_shape=out_shape,
            input_output_aliases=aliases, scratch_shapes=[pltpu.VMEM((N_CHAIN,) + state_dims, F32)],
            compiler_params=pltpu.CompilerParams(dimension_semantics=("parallel", "arbitrary")),
            name=name + ("_latent" if s0 is not None else "_context"),
        )(*args)

    ctx_nc = CTX_LEN // CHUNK
    ctx_view = (N_TOK // (PAIR * CTX_LEN), PAIR, ctx_nc, CHUNK)
    o_f, o_b, s_fin = run(ctx_view, (N_CTX_SEQ // PAIR, ctx_nc), lambda p: p, None, None)
    lat_nc = LAT_LEN // CHUNK
    lat_view = (N_TOK // (PAIR * LAT_LEN), PAIR, lat_nc, CHUNK)
    o_f, o_b = run(lat_view, (1, lat_nc), lambda p: N_CTX // (PAIR * LAT_LEN), s0_lat, (o_f, o_b))
    return o_f.reshape(N_TOK, out_width), o_b.reshape(N_TOK, out_width), s_fin


def _layer_norm(x, g, b):
    mu = jnp.mean(x, axis=-1, keepdims=True)
    xc = x - mu
    var = jnp.mean(xc * xc, axis=-1, keepdims=True)
    return xc * lax.rsqrt(var + LN_EPS) * g + b


def _merge_kernel(x_ref, mod_ref, oa_ref, ob_ref, cof_ref, cob_ref, cgate_ref, yf_ref, yb_ref, bonus_ref, dgate_ref,
                  wg_ref, wbr_ref, wout_ref, cnorm_ref, dlng_ref, dlnb_ref, lng_ref, lnb_ref, wr_ref, br_ref, bd_ref,
                  x1_o, h2_o, topi_o, topw_o):
    x = x_ref[...]
    m = mod_ref[0]
    sh1, sc1, g1, sh2, sc2 = m[0:1], m[1:2], m[2:3], m[3:4], m[4:5]
    bd = bd_ref[...]
    inv_n = 1.0 / D_N
    co = cof_ref[...] + cob_ref[...]
    o_c = co * lax.rsqrt(_dot_exact_rhs(co * co, bd) * inv_n + RMS_EPS) * cnorm_ref[...] * cgate_ref[...]
    y = yf_ref[...] + yb_ref[...]
    yc = y - _dot_exact_rhs(y, bd) * inv_n
    var = _dot_exact_rhs(yc * yc, bd) * inv_n
    o_d = (yc * lax.rsqrt(var + D_GN_EPS) * dlng_ref[...] + dlnb_ref[...] + bonus_ref[...]) * dgate_ref[...]
    branches = (oa_ref[...], ob_ref[...], o_c, o_d)
    h = (x * (1.0 + sc1) + sh1).astype(BF16)
    merged = None
    for n in range(N_BRANCH):
        gate = _sigmoid(jnp.dot(h, wg_ref[:, n * D_MODEL:(n + 1) * D_MODEL], preferred_element_type=F32))
        term = gate * jnp.dot(branches[n].astype(BF16), wbr_ref[n], preferred_element_type=F32)
        merged = term if merged is None else merged + term
    mix = jnp.dot(merged.astype(BF16), wout_ref[...], preferred_element_type=F32)
    x1 = _layer_norm(ALPHA * x + g1 * mix, lng_ref[...], lnb_ref[...])
    x1_o[...] = x1
    h2 = x1 * (1.0 + sc2) + sh2
    h2_o[...] = h2.astype(BF16)
    logits = _dot6(h2, wr_ref[...]) + br_ref[...]
    tm, n_e = logits.shape
    lane_e = lax.broadcasted_iota(jnp.int32, (tm, n_e), 1)
    lane_o = lax.broadcasted_iota(jnp.int32, (tm, topi_o.shape[1]), 1)
    top_i = jnp.zeros((tm, topi_o.shape[1]), jnp.int32)
    top_v = jnp.zeros((tm, topw_o.shape[1]), F32)
    vals = []
    for kth in range(TOP_K):
        mx = jnp.max(logits, axis=-1, keepdims=True)
        idx = jnp.min(jnp.where(logits == mx, lane_e, n_e), axis=-1, keepdims=True)
        vals.append(mx)
        top_i = jnp.where(lane_o == kth, idx, top_i)
        logits = jnp.where(lane_e == idx, -jnp.inf, logits)
    es = [jnp.exp(vk - vals[0]) for vk in vals]
    den = es[0] + es[1] + es[2] + es[3]
    for kth in range(TOP_K):
        top_v = jnp.where(lane_o == kth, es[kth] / den, top_v)
    topi_o[...] = top_i
    topw_o[...] = top_v


def merge_and_route(x, mod_l, o_a, o_b, co_f, co_b, cgate, y_f, y_b, bonus, dgate, w_g, w_br, w_out, lp):
    tm = ROW_TILE
    hn = D_HEADS * D_N
    row = lambda w: pl.BlockSpec((tm, w), lambda t: (t, 0))
    small = [jnp.tile(lp['c_norm'], C_HEADS).reshape(1, hn), lp['d_ln_g'].reshape(1, hn), lp['d_ln_b'].reshape(1, hn),
             lp['ln_g'][0].reshape(1, -1), lp['ln_b'][0].reshape(1, -1), lp['w_router'],
             lp['b_router'].reshape(1, -1), _head_block_diag()]
    return pl.pallas_call(
        _merge_kernel,
        grid=(N_TILES,),
        in_specs=([row(D_MODEL), pl.BlockSpec((1, 6, D_MODEL), lambda t: (_mod_row(t), 0, 0))]
                  + [row(hn)] * 9 + [_full(w_g.shape), _full(w_br.shape), _full(w_out.shape)]
                  + [_full(a.shape) for a in small]),
        out_specs=[row(D_MODEL), row(D_MODEL), row(128), row(128)],
        out_shape=[jax.ShapeDtypeStruct((N_TOK, D_MODEL), F32), jax.ShapeDtypeStruct((MOE_ROWS, D_MODEL), BF16),
                   jax.ShapeDtypeStruct((N_TOK, 128), jnp.int32), jax.ShapeDtypeStruct((N_TOK, 128), F32)],
        compiler_params=pltpu.CompilerParams(dimension_semantics=("parallel",), vmem_limit_bytes=VMEM_LIMIT),
        name="merge_and_route",
    )(x, mod_l, o_a, o_b, co_f, co_b, cgate, y_f, y_b, bonus, dgate, w_g, w_br, w_out, *small)


def _moe_kernel(te_ref, tv_ref, first_ref, slot_ref, next_ref, x_ref, w1_hbm, b1_ref, w2_hbm, b2_ref, perm_ref, y_ref,
                w1buf, w2buf, sem, w1s, w2s, hs, *, layer):
    t = pl.program_id(0)
    valid = tv_ref[t] != 0
    d_model, two_f = w1s.shape
    n_blk = two_f // MXU_WIDTH
    half = MXU_WIDTH // 2

    def fetch(expert, slot):
        return (pltpu.make_async_copy(w1_hbm.at[layer, expert], w1buf.at[slot], sem.at[0, slot]),
                pltpu.make_async_copy(w2_hbm.at[layer, expert], w2buf.at[slot], sem.at[1, slot]))

    @pl.when(t == 0)
    def _():
        for cp in fetch(te_ref[0], 0):
            cp.start()

    @pl.when(first_ref[t] == 1)
    def _():
        slot = slot_ref[t]
        for cp in fetch(te_ref[t], slot):
            cp.wait()

        @pl.when(next_ref[t] >= 0)
        def _():
            for cp in fetch(next_ref[t], 1 - slot):
                cp.start()

        for blk in range(n_blk):
            sl = slice(blk * MXU_WIDTH, (blk + 1) * MXU_WIDTH)
            wb = w1buf[slot, :, sl].astype(BF16)
            w1s[:, sl] = jnp.dot(wb, perm_ref[...], preferred_element_type=F32).astype(BF16)
        w2s[...] = w2buf[slot].astype(BF16)

    @pl.when(valid)
    def _():
        x = x_ref[...]
        for blk in range(n_blk):
            sl = slice(blk * MXU_WIDTH, (blk + 1) * MXU_WIDTH)
            u = jnp.dot(x, w1s[:, sl], preferred_element_type=F32) + b1_ref[0, 0, :, sl]
            glu = jnp.minimum(u[:, :half], SWIGLU_LIMIT)
            lin = jnp.clip(u[:, half:], -SWIGLU_LIMIT, SWIGLU_LIMIT)
            hs[:, blk * half:(blk + 1) * half] = (glu * _sigmoid(SWIGLU_ALPHA * glu) * (lin + 1.0)).astype(BF16)
        y = jnp.dot(hs[...], w2s[...], preferred_element_type=F32) + b2_ref[0, 0]
        y_ref[...] = y.astype(y_ref.dtype)

    @pl.when(jnp.logical_not(valid))
    def _():
        y_ref[...] = jnp.zeros_like(y_ref)


def _deinterleave_perm():
    half = MXU_WIDTH // 2
    src = np.arange(MXU_WIDTH)
    dst = np.where(src % 2 == 0, src // 2, half + src // 2)
    p = np.zeros((MXU_WIDTH, MXU_WIDTH), np.float32)
    p[src, dst] = 1.0
    return jnp.asarray(p, BF16)


def _moe_dispatch(top_i):
    n, k = top_i.shape
    tm = MOE_ROW_TILE
    p_rows = n * k + N_EXPERTS * tm
    experts = jnp.arange(N_EXPERTS, dtype=jnp.int32)
    onehot = top_i[:, :, None] == experts
    sel = jnp.sum(onehot.astype(jnp.int32), axis=1)
    before = jnp.cumsum(sel, axis=0) - sel
    counts = jnp.sum(sel, axis=0)
    padded = ((counts + tm - 1) // tm) * tm
    ends = jnp.cumsum(padded)
    starts = ends - padded
    pos = jnp.sum(jnp.where(onehot, (before + starts)[:, None, :], 0), axis=-1)
    n_tiles = p_rows // tm
    tile_start = jnp.arange(n_tiles, dtype=jnp.int32) * tm
    tile_valid = (tile_start < ends[-1]).astype(jnp.int32)
    last_tile = ends[-1] // tm - 1
    tile_expert = jnp.sum(ends[None, :] <= jnp.minimum(tile_start, last_tile * tm)[:, None], axis=1).astype(jnp.int32)
    keys = jnp.sort((top_i * n + jnp.arange(n, dtype=jnp.int32)[:, None]).reshape(-1))
    tile_onehot = tile_expert[:, None] == experts[None, :]
    lookup = lambda table: jnp.sum(jnp.where(tile_onehot, table[None, :], 0), axis=1)
    tile_rank0 = tile_start - lookup(starts)
    rank = tile_rank0[:, None] + jnp.arange(tm, dtype=jnp.int32)[None, :]
    sorted_at = jnp.clip(lookup(jnp.cumsum(counts) - counts)[:, None] + rank, 0, n * k - 1)
    tile_keys = keys[sorted_at.reshape(-1)].reshape(n_tiles, tm)
    filler = (tile_start[:, None] + jnp.arange(tm, dtype=jnp.int32)[None, :]) % n
    src_tok = jnp.where(rank < lookup(counts)[:, None], tile_keys % n, filler)
    is_first = jnp.concatenate([jnp.ones((1,), jnp.int32),
                                (tile_expert[1:] != tile_expert[:-1]).astype(jnp.int32)])
    slot = (jnp.cumsum(is_first) - 1) % 2
    later = jnp.logical_and(experts[None, :] > experts[:, None], (counts > 0)[None, :])
    next_of = jnp.min(jnp.where(later, experts[None, :], N_EXPERTS), axis=1)
    next_expert = lookup(jnp.where(next_of < N_EXPERTS, next_of, -1))
    tables = (tile_expert, tile_valid, is_first, slot.astype(jnp.int32), next_expert.astype(jnp.int32))
    return pos, src_tok.reshape(-1), tables, p_rows


def moe_experts(h2, top_i, layer, w1, b1, w2, b2):
    n = top_i.shape[0]
    d = h2.shape[1]
    depth, e, _, two_f = w1.shape
    f = two_f // 2
    tm = MOE_ROW_TILE
    pos, src_tok, tables, p_rows = _moe_dispatch(top_i)
    assert h2.shape[0] == p_rows
    xs = h2.at[lax.optimization_barrier(src_tok)].get(mode="promise_in_bounds")
    b1p = b1.reshape(depth, e, two_f // MXU_WIDTH, MXU_WIDTH // 2, 2).swapaxes(3, 4).reshape(depth, e, 1, two_f)
    expert_vec = lambda w: pl.BlockSpec((1, 1, 1, w), lambda t, te, *_: (layer, te[t], 0, 0))
    grid_spec = pltpu.PrefetchScalarGridSpec(
        num_scalar_prefetch=len(tables),
        grid=(p_rows // tm,),
        in_specs=[
            pl.BlockSpec((tm, d), lambda t, *_: (t, 0)),
            pl.BlockSpec(memory_space=pl.ANY),
            expert_vec(two_f),
            pl.BlockSpec(memory_space=pl.ANY),
            expert_vec(d),
            pl.BlockSpec((MXU_WIDTH, MXU_WIDTH), lambda t, *_: (0, 0)),
        ],
        out_specs=pl.BlockSpec((tm, d), lambda t, *_: (t, 0)),
        scratch_shapes=[pltpu.VMEM((2, d, two_f), F32), pltpu.VMEM((2, f, d), F32),
                        pltpu.SemaphoreType.DMA((2, 2)),
                        pltpu.VMEM((d, two_f), BF16), pltpu.VMEM((f, d), BF16), pltpu.VMEM((tm, f), BF16)],
    )
    ys = pl.pallas_call(
        functools.partial(_moe_kernel, layer=layer),
        grid_spec=grid_spec,
        out_shape=jax.ShapeDtypeStruct((p_rows, d), BF16),
        compiler_params=pltpu.CompilerParams(dimension_semantics=("arbitrary",),
                                             vmem_limit_bytes=48 * 1024 * 1024),
        name="moe_experts",
    )(*tables, xs, w1, b1p, w2, b2.reshape(depth, e, 1, d), _deinterleave_perm())
    return ys.at[lax.optimization_barrier(pos.T.reshape(-1))].get(mode="promise_in_bounds").reshape(TOP_K, n, d)


def _final_kernel(x1_ref, mod_ref, ys_ref, topw_ref, lng_ref, lnb_ref, o_ref):
    g2 = mod_ref[0, 5:6]
    moe = None
    for kth in range(TOP_K):
        term = ys_ref[kth].astype(F32) * topw_ref[:, kth:kth + 1]
        moe = term if moe is None else moe + term
    o_ref[...] = _layer_norm(ALPHA * x1_ref[...] + g2 * moe, lng_ref[...], lnb_ref[...])


def combine_and_norm(x1, mod_l, ys, top_w, ln_g, ln_b):
    tm = ROW_TILE
    return pl.pallas_call(
        _final_kernel,
        grid=(N_TILES,),
        in_specs=[pl.BlockSpec((tm, D_MODEL), lambda t: (t, 0)),
                  pl.BlockSpec((1, 6, D_MODEL), lambda t: (_mod_row(t), 0, 0)),
                  pl.BlockSpec((TOP_K, tm, D_MODEL), lambda t: (0, t, 0)),
                  pl.BlockSpec((tm, 128), lambda t: (t, 0)),
                  _full((1, D_MODEL)), _full((1, D_MODEL))],
        out_specs=pl.BlockSpec((tm, D_MODEL), lambda t: (t, 0)),
        out_shape=jax.ShapeDtypeStruct((N_TOK, D_MODEL), F32),
        compiler_params=pltpu.CompilerParams(dimension_semantics=("parallel",)),
        name="combine_and_norm",
    )(x1, mod_l, ys, top_w, ln_g.reshape(1, -1), ln_b.reshape(1, -1))


def kernel(x_prompt, x_sample, cache_a_k, cache_a_v, cache_b_ckv, cache_b_kpe, state_c, state_d, c,
           c_ctx, w_mod, b_mod, w_in, a_sink, b_q_norm, b_w_uq, b_kv_norm, b_w_ukv, c_w_gate, c_b_gate,
           c_norm, d_mu, d_w0, d_w2, d_a0, d_a2, d_g2, d_k_k, d_k_a, d_r_k, d_ln_g, d_ln_b, w_br, w_out,
           ln_g, ln_b, w_router, b_router, w_mlp1, b_mlp1, w_mlp2, b_mlp2):
    params = dict(b_q_norm=b_q_norm, b_w_uq=b_w_uq, b_kv_norm=b_kv_norm, b_w_ukv=b_w_ukv, c_w_gate=c_w_gate,
                  c_b_gate=c_b_gate, c_norm=c_norm, d_mu=d_mu, d_w0=d_w0, d_w2=d_w2, d_a0=d_a0, d_a2=d_a2, d_g2=d_g2,
                  d_k_k=d_k_k, d_k_a=d_k_a, d_r_k=d_r_k, d_ln_g=d_ln_g, d_ln_b=d_ln_b, ln_g=ln_g, ln_b=ln_b,
                  w_router=w_router, b_router=b_router)
    assert x_prompt.shape == (N_CTX_SEQ, CTX_LEN, D_MODEL) and x_sample.shape == (N_LAT_SEQ, LAT_LEN, D_MODEL)
    x = jnp.concatenate([x_prompt.reshape(N_CTX, D_MODEL), x_sample.reshape(-1, D_MODEL)], axis=0)
    cond8 = jnp.concatenate([c_ctx[None], c, jnp.zeros((8 - 1 - N_LAT_SEQ, D_MODEL), F32)], axis=0)
    mod = modulation_table(cond8, w_mod, b_mod)[:, :1 + N_LAT_SEQ].reshape(DEPTH, 1 + N_LAT_SEQ, 6, D_MODEL)
    tables = _rope_tables()
    new = {name: [] for name in ("a_k", "a_v", "b_ckv", "b_kpe", "c", "d")}
    for l in range(DEPTH):
        lp = {name: val[l] for name, val in params.items()}
        w_small = jnp.concatenate([w_in[l][:, _ORIG[nm][0]:_ORIG[nm][1]] for nm in _ORDER], axis=1).astype(BF16)
        w_g = w_in[l][:, G_START:].astype(BF16)
        w_br_l = w_br[l].astype(BF16)
        w_out_l = w_out[l].astype(BF16)

        (aq, ak, av, bq, bckv, bkpe, cq4, ck4, cla_f, cla_b, cv, cgate,
         r, v, kk, lw_f, lw_b, k_f, k_b, a_f, a_b, bonus, dgate) = mixer_prelude(x, mod[l], w_small, tables, lp)

        o_a = gqa_attention(aq, ak, av, a_sink[l], cache_a_k, cache_a_v, l)
        o_b = mla_attention(bq, bckv, bkpe, b_w_ukv[l], cache_b_ckv, cache_b_kpe, l)

        c_s0 = jnp.swapaxes(state_c[:, l], 3, 4).reshape(1, N_CHAIN, C_DV, C_DK)
        co_f, co_b, c_fin = _recurrence_calls(functools.partial(_gla_kernel, dot=_dot1), "gla", [cq4, ck4, cv],
                                              [cla_f], [cla_b], c_s0, (C_DV, C_DK), C_HEADS * C_DV)
        d_s0 = state_d[:, l].reshape(1, N_CHAIN, D_N, D_N)
        y_f, y_b, d_fin = _recurrence_calls(functools.partial(_rwkv_kernel, dot=_dot1), "rwkv7", [r, v, kk],
                                            [lw_f, k_f, a_f], [lw_b, k_b, a_b], d_s0, (D_N, D_N), D_HEADS * D_N)

        x1, h2, top_i, top_w = merge_and_route(x, mod[l], o_a, o_b, co_f, co_b, cgate, y_f, y_b, bonus, dgate,
                                               w_g, w_br_l, w_out_l, lp)
        ys = moe_experts(h2, top_i[:, :TOP_K], l, w_mlp1, b_mlp1, w_mlp2, b_mlp2)
        x = combine_and_norm(x1, mod[l], ys, top_w, ln_g[l, 1], ln_b[l, 1])

        new["a_k"].append(ak[:N_CTX].reshape(N_CTX_SEQ, CTX_LEN, A_KV_HEADS, A_HD).transpose(0, 2, 1, 3))
        new["a_v"].append(av[:N_CTX].reshape(N_CTX_SEQ, CTX_LEN, A_KV_HEADS, A_HD).transpose(0, 2, 1, 3))
        new["b_ckv"].append(bckv[:N_CTX].reshape(N_CTX_SEQ, CTX_LEN, B_KV_LORA))
        new["b_kpe"].append(bkpe[:N_CTX].reshape(N_CTX_SEQ, CTX_LEN, B_ROPE))
        new["c"].append(jnp.swapaxes(c_fin.reshape(N_CTX_SEQ, 2, C_HEADS, C_DV, C_DK), 3, 4))
        new["d"].append(d_fin.reshape(N_CTX_SEQ, 2, D_HEADS, D_N, D_N))
    y_prompt = x[:N_CTX].reshape(x_prompt.shape)
    y_sample = x[N_CTX:].reshape(x_sample.shape)
    return (y_prompt, y_sample, *(jnp.stack(new[name], axis=1) for name in ("a_k", "a_v", "b_ckv", "b_kpe", "c", "d")))
```

```python
import functools

import jax
import jax.numpy as jnp
import numpy as np
from jax import lax
from jax.experimental import pallas as pl
from jax.experimental.pallas import tpu as pltpu

F32 = jnp.float32
BF16 = jnp.bfloat16

MXU_WIDTH = 256
VMEM_LIMIT = 56 * 1024 * 1024

D_MODEL = 1024
DEPTH = 2
GRID_W = 64
ROPE_BASE = 10000.0
A_HEADS, A_KV_HEADS, A_HD = 4, 2, 64
B_HEADS, B_NOPE, B_ROPE, B_VD, B_Q_LORA, B_KV_LORA = 4, 64, 32, 64, 192, 128
C_HEADS, C_DK, C_DV, C_GATE_RANK, C_GATE_TEMP = 4, 32, 64, 16, 16.0
D_HEADS, D_N, D_DECAY_RANK, D_AAA_RANK, D_GATE_RANK, D_GN_EPS = 4, 64, 64, 64, 128, 64e-5
BRANCH_W = 256
N_BRANCH = 4
N_EXPERTS = 32
TOP_K = 4
SWIGLU_LIMIT = 7.0
SWIGLU_ALPHA = 1.702
ALPHA = (2 * DEPTH) ** 0.25
LN_EPS = 1e-5
RMS_EPS = 1e-6

N_CTX_SEQ, CTX_LEN = 16, 256
N_LAT_SEQ, LAT_LEN = 2, 2048
N_CTX = N_CTX_SEQ * CTX_LEN
N_TOK = N_CTX + N_LAT_SEQ * LAT_LEN
ROW_TILE = 256
N_TILES = N_TOK // ROW_TILE
CTX_TILES = N_CTX // ROW_TILE
LAT_TILES_PER_SEQ = LAT_LEN // ROW_TILE
N_SEQ = N_CTX_SEQ + N_LAT_SEQ
MOE_ROW_TILE = 256
MOE_ROWS = N_TOK * TOP_K + N_EXPERTS * MOE_ROW_TILE

_ORIG = dict(aq=(0, 256), ak=(256, 384), av=(384, 512), bcq=(512, 704), bckv=(704, 832), bkpe=(832, 864),
             cq=(864, 992), ck=(992, 1120), cv=(1120, 1376), cog=(1376, 1632), caf=(1632, 1648), cab=(1648, 1664),
             zd=(1664, 2816))
_ORDER = ("aq", "ak", "av", "cq", "ck", "cv", "cog", "zd", "bcq", "caf", "cab", "bkpe", "bckv")
COL = {}
_off = 0
for _name in _ORDER:
    _w = _ORIG[_name][1] - _ORIG[_name][0]
    COL[_name] = (_off, _off + _w)
    _off += _w
SMALL_COLS = _off
G_START = 2816


def _cs(name):
    return slice(*COL[name])


def _split3(x):
    hi = x.astype(BF16)
    r1 = x - hi.astype(F32)
    mid = r1.astype(BF16)
    lo = (r1 - mid.astype(F32)).astype(BF16)
    return hi, mid, lo


def _split2(x):
    hi = x.astype(BF16)
    lo = (x - hi.astype(F32)).astype(BF16)
    return hi, lo


def _bdot(a, b, dims):
    return lax.dot_general(a, b, dims, preferred_element_type=F32)


_D2 = (((1,), (0,)), ((), ()))
_D2T = (((1,), (1,)), ((), ()))
_NN = (((2,), (1,)), ((0,), (0,)))
_NT = (((2,), (2,)), ((0,), (0,)))
_TN = (((1,), (1,)), ((0,), (0,)))


def _dot1(a, b, dims=_D2):
    return _bdot(a.astype(BF16), b.astype(BF16), dims)


def _dot3(a, b, dims=_D2):
    ah, al = _split2(a)
    bh, bl = _split2(b)
    return _bdot(ah, bh, dims) + (_bdot(ah, bl, dims) + _bdot(al, bh, dims))


def _dot_exact_lhs(a01, b, dims=_D2):
    a = a01.astype(BF16)
    h, m, l = _split3(b)
    return _bdot(a, h, dims) + (_bdot(a, m, dims) + _bdot(a, l, dims))


def _dot_exact_rhs(a, b01, dims=_D2):
    b = b01.astype(BF16)
    h, m, l = _split3(a)
    return _bdot(h, b, dims) + (_bdot(m, b, dims) + _bdot(l, b, dims))


def _dot6(a, b, dims=_D2):
    ah, am, al = _split3(a)
    bh, bm, bl = _split3(b)
    return (_bdot(ah, bh, dims) + (_bdot(ah, bm, dims) + _bdot(am, bh, dims))
            + (_bdot(am, bm, dims) + (_bdot(ah, bl, dims) + _bdot(al, bh, dims))))


def _sigmoid(x):
    return 0.5 * jnp.tanh(0.5 * x) + 0.5


def _softplus(x):
    return jnp.maximum(x, 0.0) + jnp.log(1.0 + jnp.exp(-jnp.abs(x)))


def _mod_row(t):
    return jnp.where(t < CTX_TILES, 0, 1 + (t - CTX_TILES) // LAT_TILES_PER_SEQ)


def _full(shape):
    nd = len(shape)
    return pl.BlockSpec(shape, lambda *_: (0,) * nd)


MOD_COL_TILE = 1536


def _mod_kernel(c_ref, w_ref, b_ref, o_ref):
    c = c_ref[...]
    o_ref[0] = _dot3(c * _sigmoid(c), w_ref[0]) + b_ref[0]


def modulation_table(cond8, w_mod, b_mod):
    depth, d, six_d = w_mod.shape
    return pl.pallas_call(
        _mod_kernel,
        grid=(depth, six_d // MOD_COL_TILE),
        in_specs=[pl.BlockSpec((8, d), lambda l, j: (0, 0)),
                  pl.BlockSpec((1, d, MOD_COL_TILE), lambda l, j: (l, 0, j)),
                  pl.BlockSpec((1, 1, MOD_COL_TILE), lambda l, j: (l, 0, j))],
        out_specs=pl.BlockSpec((1, 8, MOD_COL_TILE), lambda l, j: (l, 0, j)),
        out_shape=jax.ShapeDtypeStruct((depth, 8, six_d), F32),
        compiler_params=pltpu.CompilerParams(dimension_semantics=("parallel", "parallel")),
        name="modulation",
    )(cond8, w_mod, b_mod.reshape(depth, 1, six_d))


WPREP_ROWS = 128


def _wprep_kernel(w_ref, small_ref, gate_ref):
    for name in _ORDER:
        lo, hi = _ORIG[name]
        small_ref[0, :, _cs(name)] = w_ref[0, :, lo:hi].astype(BF16)
    gate_ref[0] = w_ref[0, :, G_START:].astype(BF16)


def prepare_in_weights(w_in):
    depth, d, cols = w_in.shape
    return pl.pallas_call(
        _wprep_kernel,
        grid=(depth, d // WPREP_ROWS),
        in_specs=[pl.BlockSpec((1, WPREP_ROWS, cols), lambda l, r: (l, r, 0))],
        out_specs=[pl.BlockSpec((1, WPREP_ROWS, SMALL_COLS), lambda l, r: (l, r, 0)),
                   pl.BlockSpec((1, WPREP_ROWS, cols - G_START), lambda l, r: (l, r, 0))],
        out_shape=[jax.ShapeDtypeStruct((depth, d, SMALL_COLS), BF16),
                   jax.ShapeDtypeStruct((depth, d, cols - G_START), BF16)],
        compiler_params=pltpu.CompilerParams(dimension_semantics=("parallel", "parallel")),
        name="prepare_in_weights",
    )(w_in)


def _rot_pairs(x, half, lane_mod_base=0):
    w = x.shape[-1]
    lane = lax.broadcasted_iota(jnp.int32, (1, w), 1) - lane_mod_base
    first = (lane % (2 * half)) < half
    return jnp.where(first, -pltpu.roll(x, w - half, axis=1), pltpu.roll(x, half, axis=1))


def _pre_kernel(x_ref, xp_ref, xn_ref, mod_ref, w_ref, ca_ref, sa_ref, cb_ref, sb_ref, ck_ref, sk_ref,
                qnorm_ref, kvnorm_ref, wuq_ref, cwg_ref, cbg_ref, rep_ref, mu_ref, dw0_ref, dw2_ref, da0_ref,
                da2_ref, dg2_ref, dkk_ref, dka_ref, drk_ref, bd_ref,
                aq_o, ak_o, av_o, bq_o, bckv_o, bkpe_o, cq4_o, ck4_o, claf_o, clab_o, cv_o, cgate_o,
                r_o, v_o, kk_o, lwf_o, lwb_o, kf_o, kb_o, af_o, ab_o, bonus_o, dgate_o):
    t = pl.program_id(0)
    tm = x_ref.shape[0]
    sh1 = mod_ref[0, 0:1, :]
    sc1 = mod_ref[0, 1:2, :]

    def modulate(xv):
        return (xv * (1.0 + sc1) + sh1).astype(BF16)

    z = jnp.dot(modulate(x_ref[...]), w_ref[0], preferred_element_type=F32)

    aq = z[:, _cs("aq")]
    ak = z[:, _cs("ak")]
    aq_o[...] = aq * ca_ref[...] + _rot_pairs(aq, A_HD // 4) * sa_ref[...]
    ak_o[...] = ak * ca_ref[:, :ak.shape[1]] + _rot_pairs(ak, A_HD // 4) * sa_ref[:, :ak.shape[1]]
    av_o[...] = z[:, _cs("av")]

    bcq = z[:, _cs("bcq")]
    qn = bcq * lax.rsqrt(jnp.mean(bcq * bcq, axis=-1, keepdims=True) + RMS_EPS) * qnorm_ref[...]
    bq = _dot1(qn, wuq_ref[...])
    bq_o[...] = bq * cb_ref[...] + _rot_pairs(bq, B_ROPE // 4, lane_mod_base=B_NOPE) * sb_ref[...]
    bckv = z[:, _cs("bckv")]
    bckv_o[...] = bckv * lax.rsqrt(jnp.mean(bckv * bckv, axis=-1, keepdims=True) + RMS_EPS) * kvnorm_ref[...]
    kpe_lo = COL["bkpe"][0] // 128 * 128
    kblk = z[:, kpe_lo:kpe_lo + 128]
    kblk = kblk * ck_ref[...] + _rot_pairs(kblk, B_ROPE // 4) * sk_ref[...]
    bkpe_o[...] = kblk[:, COL["bkpe"][0] - kpe_lo:COL["bkpe"][1] - kpe_lo]

    rep = rep_ref[...]
    cq4_o[...] = _dot_exact_rhs(z[:, _cs("cq")] * (C_DK ** -0.5), rep)
    ck4_o[...] = _dot_exact_rhs(z[:, _cs("ck")], rep)
    cv_o[...] = z[:, _cs("cv")]
    cog = z[:, _cs("cog")]
    cgate_o[...] = cog * _sigmoid(cog)
    for direction, (name, out) in enumerate((("caf", claf_o), ("cab", clab_o))):
        pre = _dot3(z[:, _cs(name)], cwg_ref[direction]) + cbg_ref[direction]
        out[...] = _dot_exact_rhs(-_softplus(-pre) * (1.0 / C_GATE_TEMP), rep)

    zd_cols = _cs("zd")
    zd = z[:, zd_cols]
    wd = w_ref[0, :, zd_cols]
    j = (t - CTX_TILES) % LAT_TILES_PER_SEQ
    latent = t >= CTX_TILES
    has_prev = jnp.logical_and(latent, j != 0)
    has_next = jnp.logical_and(latent, j != LAT_TILES_PER_SEQ - 1)
    zp = jnp.dot(modulate(xp_ref[...]), wd, preferred_element_type=F32)
    zn = jnp.dot(modulate(xn_ref[...]), wd, preferred_element_type=F32)
    prev_row = jnp.where(has_prev, zp[7:8], 0.0)
    next_row = jnp.where(has_next, zn[0:1], 0.0)
    row = lax.broadcasted_iota(jnp.int32, (tm, 1), 0)
    up = jnp.where(row == 0, prev_row, pltpu.roll(zd, 1, axis=0))
    dn = jnp.where(row == tm - 1, next_row, pltpu.roll(zd, tm - 1, axis=0))
    zd = zd + (0.5 * (up + dn) - zd) * mu_ref[...]

    hn = D_HEADS * D_N
    d_r, d_k, d_v = zd[:, :hn], zd[:, hn:2 * hn], zd[:, 2 * hn:3 * hn]
    o = 3 * hn
    d_w = (zd[:, o:o + D_DECAY_RANK], zd[:, o + D_DECAY_RANK:o + 2 * D_DECAY_RANK])
    o += 2 * D_DECAY_RANK
    d_a = (zd[:, o:o + D_AAA_RANK], zd[:, o + D_AAA_RANK:o + 2 * D_AAA_RANK])
    o += 2 * D_AAA_RANK
    d_g = zd[:, o:o + D_GATE_RANK]
    bd = bd_ref[...]
    kk = d_k * dkk_ref[...]
    kk = kk / jnp.maximum(jnp.sqrt(_dot_exact_rhs(kk * kk, bd)), 1e-12)
    r_o[...] = d_r
    v_o[...] = d_v
    kk_o[...] = kk
    k_sum = None
    for direction, (lw_o, k_o, a_o) in enumerate(((lwf_o, kf_o, af_o), (lwb_o, kb_o, ab_o))):
        w_log = -_softplus(-(dw0_ref[direction] + _dot3(jnp.tanh(d_w[direction]), dw2_ref[direction]))) - 0.5
        lw_o[...] = -jnp.exp(w_log)
        a = _sigmoid(da0_ref[direction] + _dot3(d_a[direction], da2_ref[direction]))
        k_dir = d_k * (1.0 + (a - 1.0) * dka_ref[...])
        k_o[...] = k_dir
        a_o[...] = a
        k_sum = k_dir if k_sum is None else k_sum + k_dir
    bonus_o[...] = d_v * _dot_exact_rhs(d_r * drk_ref[...] * k_sum, bd)
    dgate_o[...] = _dot3(_sigmoid(d_g), dg2_ref[...])


def _rope_tables():
    pos = np.arange(LAT_LEN)
    rowp, colp = (pos // GRID_W).astype(np.float32), (pos % GRID_W).astype(np.float32)

    f32 = np.float32

    def head_tables(rot_dim):
        quarter = rot_dim // 4
        inv = (f32(ROPE_BASE) ** (-np.arange(quarter, dtype=f32) / f32(quarter))).astype(f32)
        ar = (rowp[:, None] * inv).astype(f32)
        ac = (colp[:, None] * inv).astype(f32)
        cos = np.concatenate([np.cos(ar), np.cos(ar), np.cos(ac), np.cos(ac)], axis=-1).astype(f32)
        sin = np.concatenate([np.sin(ar), np.sin(ar), np.sin(ac), np.sin(ac)], axis=-1).astype(f32)
        return cos, sin

    def with_identity(c, s):
        w = c.shape[1]
        return (jnp.asarray(np.concatenate([np.ones((ROW_TILE, w), f32), c], axis=0)),
                jnp.asarray(np.concatenate([np.zeros((ROW_TILE, w), f32), s], axis=0)))

    ca, sa = head_tables(A_HD)
    ca, sa = with_identity(np.tile(ca, (1, A_HEADS)), np.tile(sa, (1, A_HEADS)))
    cbh, sbh = head_tables(B_ROPE)
    ones, zeros = np.ones((LAT_LEN, B_NOPE), f32), np.zeros((LAT_LEN, B_NOPE), f32)
    cb, sb = with_identity(np.tile(np.concatenate([ones, cbh], axis=1), (1, B_HEADS)),
                           np.tile(np.concatenate([zeros, sbh], axis=1), (1, B_HEADS)))
    pad = 128 - B_ROPE
    ck, sk = with_identity(np.concatenate([np.ones((LAT_LEN, pad), f32), cbh], axis=1),
                           np.concatenate([np.zeros((LAT_LEN, pad), f32), sbh], axis=1))
    return ca, sa, cb, sb, ck, sk


def _lane_repeat_matrix():
    m = np.zeros((C_HEADS * C_DK, C_HEADS * 128), np.float32)
    for h in range(C_HEADS):
        for g in range(128 // C_DK):
            for d in range(C_DK):
                m[h * C_DK + d, h * 128 + g * C_DK + d] = 1.0
    return jnp.asarray(m, BF16)


def _head_block_diag():
    m = np.kron(np.eye(D_HEADS, dtype=np.float32), np.ones((D_N, D_N), np.float32))
    return jnp.asarray(m, BF16)


PRE_OUT_WIDTHS = (256, 128, 128, 384, 128, 32, 512, 512, 512, 512, 256, 256) + (256,) * 11


def _layer_block(arr, layer):
    nd = arr.ndim
    return pl.BlockSpec((1,) + arr.shape[1:], lambda *_: (layer,) + (0,) * (nd - 1))


def mixer_prelude(x, mod_l, w_small, layer, tables, lp):
    tm = ROW_TILE
    tab_idx = lambda t: (jnp.where(t < CTX_TILES, 0, 1 + (t - CTX_TILES) % LAT_TILES_PER_SEQ), 0)
    hn = D_HEADS * D_N
    small = [lp['b_q_norm'].reshape(1, -1), lp['b_kv_norm'].reshape(1, -1), lp['b_w_uq'], lp['c_w_gate'],
             lp['c_b_gate'].reshape(2, 1, -1), _lane_repeat_matrix(), lp['d_mu'].reshape(1, -1),
             lp['d_w0'].reshape(2, 1, hn), lp['d_w2'], lp['d_a0'].reshape(2, 1, hn), lp['d_a2'], lp['d_g2'],
             lp['d_k_k'].reshape(1, hn), lp['d_k_a'].reshape(1, hn), lp['d_r_k'].reshape(1, hn), _head_block_diag()]
    in_specs = ([pl.BlockSpec((tm, D_MODEL), lambda t: (t, 0)),
                 pl.BlockSpec((8, D_MODEL), lambda t: (jnp.maximum(t * (tm // 8) - 1, 0), 0)),
                 pl.BlockSpec((8, D_MODEL), lambda t: (jnp.minimum((t + 1) * (tm // 8), N_TOK // 8 - 1), 0)),
                 pl.BlockSpec((1, 6, D_MODEL), lambda t: (_mod_row(t), 0, 0)),
                 _layer_block(w_small, layer)]
                + [pl.BlockSpec((tm, tab.shape[1]), tab_idx) for tab in tables]
                + [_full(a.shape) for a in small])
    return pl.pallas_call(
        _pre_kernel,
        grid=(N_TILES,),
        in_specs=in_specs,
        out_specs=[pl.BlockSpec((tm, w), lambda t: (t, 0)) for w in PRE_OUT_WIDTHS],
        out_shape=[jax.ShapeDtypeStruct((N_TOK, w), F32) for w in PRE_OUT_WIDTHS],
        compiler_params=pltpu.CompilerParams(dimension_semantics=("parallel",), vmem_limit_bytes=VMEM_LIMIT),
        name="mixer_prelude",
    )(x, x, x, mod_l, w_small, *tables, *small)


ATT_Q_BLOCK = 128
ATT_WINDOW = 128
ATT_NEG_INF = -1e30
CACHE_LEN = 512


def _softmax_pv(s, v, sink):
    dv = v.shape[1] // 2
    m = jnp.max(s, axis=-1, keepdims=True)
    if sink is not None:
        m = jnp.maximum(m, sink)
    e = jnp.exp((s - m).astype(BF16))
    o = jnp.dot(e, v, preferred_element_type=F32)
    den = o[:, dv:dv + 1]
    if sink is not None:
        den = den + jnp.exp(sink - m)
    return o[:, :dv] / den


def _with_ones(v):
    return jnp.concatenate([v.astype(BF16), jnp.ones(v.shape, BF16)], axis=1)


def _gqa_kernel(sink_ref, q_ref, k_ref, v_ref, *rest, hd, group, scale, windowed):
    if windowed:
        kp_ref, kn_ref, vp_ref, vn_ref, kc_ref, vc_ref, _, o_ref = rest
    else:
        (o_ref,) = rest
    i = pl.program_id(1)
    tq = q_ref.shape[0]
    n_kv = k_ref.shape[1] // hd
    if windowed:
        qpos = i * tq + lax.broadcasted_iota(jnp.int32, (tq, 3 * tq), 0)
        kpos = (i - 1) * tq + lax.broadcasted_iota(jnp.int32, (tq, 3 * tq), 1)
        n_tok = pl.num_programs(1) * tq
        mask = (jnp.abs(qpos - kpos) <= ATT_WINDOW) & (kpos >= 0) & (kpos < n_tok)
        mask = jnp.concatenate([mask] * group, axis=0)
    for kvh in range(n_kv):
        ks = slice(kvh * hd, (kvh + 1) * hd)
        qs = [q_ref[:, (kvh * group + g) * hd:(kvh * group + g + 1) * hd] for g in range(group)]
        q = (jnp.concatenate(qs, axis=0) * scale).astype(BF16)
        sink = jnp.concatenate(
            [jnp.full((tq, 1), sink_ref[kvh * group + g], F32) for g in range(group)], axis=0)
        if windowed:
            k_win = jnp.concatenate([kp_ref[:, ks], k_ref[:, ks], kn_ref[:, ks]], axis=0)
            v_win = jnp.concatenate([vp_ref[:, ks], v_ref[:, ks], vn_ref[:, ks]], axis=0)
            s_win = _bdot(q, k_win.astype(BF16), _D2T)
            s_win = jnp.where(mask, s_win, ATT_NEG_INF)
            s_ctx = _bdot(q, kc_ref[0, 0, kvh].astype(BF16), _D2T)
            s = jnp.concatenate([s_win, s_ctx], axis=1)
            v = jnp.concatenate([v_win, vc_ref[0, 0, kvh]], axis=0)
        else:
            s = _bdot(q, k_ref[:, ks].astype(BF16), _D2T)
            v = v_ref[:, ks]
        o = _softmax_pv(s, _with_ones(v), sink)
        for g in range(group):
            h = kvh * group + g
            o_ref[:, h * hd:(h + 1) * hd] = o[g * tq:(g + 1) * tq]


def gqa_attention(q, k, v, sink, cache_k, cache_v, layer):
    qw, kw = q.shape[1], k.shape[1]
    group = qw // kw
    scale = A_HD ** -0.5
    params = pltpu.CompilerParams(dimension_semantics=("parallel", "parallel"))
    out_shape = jax.ShapeDtypeStruct((N_TOK, qw), F32)
    ctx_spec = lambda w: pl.BlockSpec((CTX_LEN, w), lambda s, i, sk: (s, 0))
    o = pl.pallas_call(
        functools.partial(_gqa_kernel, hd=A_HD, group=group, scale=scale, windowed=False),
        grid_spec=pltpu.PrefetchScalarGridSpec(
            num_scalar_prefetch=1, grid=(N_CTX_SEQ, 1), in_specs=[ctx_spec(qw), ctx_spec(kw), ctx_spec(kw)],
            out_specs=ctx_spec(qw)),
        out_shape=out_shape, compiler_params=params, name="gqa_full",
    )(sink, q, k, v)
    tq = ATT_Q_BLOCK
    nb = LAT_LEN // tq
    base = N_CTX // tq
    blk = lambda w, f: pl.BlockSpec((tq, w), lambda b, i, sk: (base + nb * b + f(i), 0))
    same = lambda i: i
    prev = lambda i: jnp.maximum(i - 1, 0)
    nxt = lambda i: jnp.minimum(i + 1, nb - 1)
    cspec = pl.BlockSpec((1, 1) + cache_k.shape[2:], lambda b, i, sk: (b, layer, 0, 0, 0))
    return pl.pallas_call(
        functools.partial(_gqa_kernel, hd=A_HD, group=group, scale=scale, windowed=True),
        grid_spec=pltpu.PrefetchScalarGridSpec(
            num_scalar_prefetch=1, grid=(N_LAT_SEQ, nb),
            in_specs=[blk(qw, same), blk(kw, same), blk(kw, same), blk(kw, prev), blk(kw, nxt), blk(kw, prev),
                      blk(kw, nxt), cspec, cspec, pl.BlockSpec(memory_space=pl.ANY)],
            out_specs=blk(qw, same)),
        out_shape=out_shape, input_output_aliases={10: 0}, compiler_params=params, name="gqa_windowed",
    )(sink, q, k, v, k, k, v, v, cache_k, cache_v, o)


def _mla_kernel(q_ref, ckv_ref, kpe_ref, wukv_ref, *rest, n_heads, nope, rope, vd, scale, cached):
    if cached:
        cckv_ref, ckpe_ref, _, o_ref, kv_scr, vext_scr, kpe_scr = rest
    else:
        o_ref, kv_scr, vext_scr, kpe_scr = rest
    i = pl.program_id(1)
    n_cache = kv_scr.shape[0] - ckv_ref.shape[0]
    hw = nope + vd

    @pl.when(i == 0)
    def _():
        w = wukv_ref[...].astype(BF16)

        def expand(rows, lo, hi):
            kv = jnp.dot(rows.astype(BF16), w, preferred_element_type=F32).astype(BF16)
            kv_scr[lo:hi] = kv
            for h in range(n_heads):
                vext_scr[lo:hi, 2 * vd * h:2 * vd * h + vd] = kv[:, h * hw + nope:(h + 1) * hw]
                vext_scr[lo:hi, 2 * vd * h + vd:2 * vd * (h + 1)] = jnp.ones((hi - lo, vd), BF16)

        if cached:
            expand(cckv_ref[0, 0], 0, n_cache)
            kpe_scr[:n_cache] = ckpe_ref[0, 0].astype(BF16)
        expand(ckv_ref[...], n_cache, kv_scr.shape[0])
        kpe_scr[n_cache:] = kpe_ref[...].astype(BF16)

    kpe = kpe_scr[...]
    qd = nope + rope
    for h in range(n_heads):
        qn = (q_ref[:, h * qd:h * qd + nope] * scale).astype(BF16)
        qp = (q_ref[:, h * qd + nope:(h + 1) * qd] * scale).astype(BF16)
        k_n = kv_scr[:, h * hw:h * hw + nope]
        s = _bdot(qn, k_n, _D2T) + _bdot(qp, kpe, _D2T)
        o_ref[:, h * vd:(h + 1) * vd] = _softmax_pv(s, vext_scr[:, 2 * vd * h:2 * vd * (h + 1)], None)


def mla_attention(q, ckv, kpe, w_ukv, cache_ckv, cache_kpe, layer):
    qw = q.shape[1]
    tq = ATT_Q_BLOCK
    kw = dict(n_heads=B_HEADS, nope=B_NOPE, rope=B_ROPE, vd=B_VD, scale=(B_NOPE + B_ROPE) ** -0.5)
    params = pltpu.CompilerParams(dimension_semantics=("parallel", "arbitrary"))
    out_shape = jax.ShapeDtypeStruct((N_TOK, B_HEADS * B_VD), F32)
    kvw = B_HEADS * (B_NOPE + B_VD)
    nbc = CTX_LEN // tq
    o = pl.pallas_call(
        functools.partial(_mla_kernel, cached=False, **kw),
        grid=(N_CTX_SEQ, nbc),
        in_specs=[pl.BlockSpec((tq, qw), lambda s, i: (s * nbc + i, 0)),
                  pl.BlockSpec((CTX_LEN, B_KV_LORA), lambda s, i: (s, 0)),
                  pl.BlockSpec((CTX_LEN, B_ROPE), lambda s, i: (s, 0)),
                  _full(w_ukv.shape)],
        out_specs=pl.BlockSpec((tq, B_HEADS * B_VD), lambda s, i: (s * nbc + i, 0)),
        out_shape=out_shape,
        scratch_shapes=[pltpu.VMEM((CTX_LEN, kvw), BF16), pltpu.VMEM((CTX_LEN, 2 * B_HEADS * B_VD), BF16),
                        pltpu.VMEM((CTX_LEN, B_ROPE), BF16)],
        compiler_params=params, name="mla_context",
    )(q, ckv, kpe, w_ukv)
    nb = LAT_LEN // tq
    base = N_CTX // tq
    lat0 = N_CTX // LAT_LEN
    s_len = CACHE_LEN + LAT_LEN
    return pl.pallas_call(
        functools.partial(_mla_kernel, cached=True, **kw),
        grid=(N_LAT_SEQ, nb),
        in_specs=[pl.BlockSpec((tq, qw), lambda b, i: (base + nb * b + i, 0)),
                  pl.BlockSpec((LAT_LEN, B_KV_LORA), lambda b, i: (lat0 + b, 0)),
                  pl.BlockSpec((LAT_LEN, B_ROPE), lambda b, i: (lat0 + b, 0)),
                  _full(w_ukv.shape),
                  pl.BlockSpec((1, 1, CACHE_LEN, B_KV_LORA), lambda b, i: (b, layer, 0, 0)),
                  pl.BlockSpec((1, 1, CACHE_LEN, B_ROPE), lambda b, i: (b, layer, 0, 0)),
                  pl.BlockSpec(memory_space=pl.ANY)],
        out_specs=pl.BlockSpec((tq, B_HEADS * B_VD), lambda b, i: (base + nb * b + i, 0)),
        out_shape=out_shape, input_output_aliases={6: 0},
        scratch_shapes=[pltpu.VMEM((s_len, kvw), BF16), pltpu.VMEM((s_len, 2 * B_HEADS * B_VD), BF16),
                        pltpu.VMEM((s_len, B_ROPE), BF16)],
        compiler_params=params, name="mla_latent",
    )(q, ckv, kpe, w_ukv, cache_ckv, cache_kpe, o)


CHUNK = 64
GLA_SUB = 16
PAIR = 2
N_CHAIN = PAIR * 2 * 4


def _is_back(shape):
    return (lax.broadcasted_iota(jnp.int32, shape, 0) // 4) % 2 == 1


def _chains(ref_f, ref_b, width):
    return jnp.stack([ref[0, s, 0, :, h * width:(h + 1) * width]
                      for s in range(PAIR) for ref in (ref_f, ref_b) for h in range(4)], axis=0)


def _unchain(y, o_f, o_b):
    for s in range(PAIR):
        o_f[0, s, 0] = jnp.concatenate([y[s * 8 + h] for h in range(4)], axis=-1)
        o_b[0, s, 0] = jnp.concatenate([y[s * 8 + 4 + h] for h in range(4)], axis=-1)


def _dir_masks(L):
    shape = (N_CHAIN, L, L)
    back = _is_back(shape)
    row = lax.broadcasted_iota(jnp.int32, shape, 1)
    col = lax.broadcasted_iota(jnp.int32, shape, 2)
    ahead = jnp.where(back, col - row, row - col)
    return ahead >= 0, ahead > 0, row == col


def _chunk_end(ci):
    L = ci.shape[1]
    return jnp.where(_is_back((N_CHAIN, 1, 1)), ci[:, 0:1], ci[:, L - 1:L])


def _split_refs(refs, n_in, has_s0, has_sfin):
    ins = refs[:n_in]
    pos = n_in
    s0_ref = None
    if has_s0:
        s0_ref = refs[pos]
        pos += 3
    of_ref, ob_ref = refs[pos], refs[pos + 1]
    pos += 2
    sfin_ref = refs[pos] if has_sfin else None
    return ins, s0_ref, of_ref, ob_ref, sfin_ref, refs[-1]


def _init_state(s_scr, s0_ref):
    @pl.when(pl.program_id(1) == 0)
    def _():
        if s0_ref is None:
            s_scr[...] = jnp.zeros_like(s_scr)
        else:
            s_scr[...] = s0_ref[0]


def _emit_state(sfin_ref, s_new):
    if sfin_ref is None:
        return

    @pl.when(pl.program_id(1) == pl.num_programs(1) - 1)
    def _():
        sfin_ref[0] = s_new


def _rwkv_kernel(*refs, dot, has_s0, has_sfin):
    (rf, rb, vf, vb, kkf, kkb, lwf, lwb, kf, kb, af, ab), s0_ref, yf_ref, yb_ref, sfin_ref, s_scr = _split_refs(
        refs, 12, has_s0, has_sfin)
    _init_state(s_scr, s0_ref)
    n = D_N
    r = _chains(rf, rb, n)
    v = _chains(vf, vb, n)
    kk = _chains(kkf, kkb, n)
    lw = _chains(lwf, lwb, n)
    k = _chains(kf, kb, n)
    a = _chains(af, ab, n)
    L = r.shape[1]
    S = s_scr[...]
    incl, strict, diag = _dir_masks(L)
    ci = _dot_exact_lhs(jnp.where(incl, 1.0, 0.0), lw, _NN)
    ce = ci - lw
    cl = _chunk_end(ci)
    e_neg = jnp.exp(-ci)
    b = a * kk
    alpha = kk * jnp.exp(ce)
    rho = r * jnp.exp(ci)
    beta = b * e_neg
    kappa = k * e_neg
    e_end = jnp.exp(cl - ci)
    ar = jnp.concatenate([alpha, rho], axis=1)
    bk = jnp.concatenate([beta, kappa], axis=1)
    w = dot(ar, bk, _NT)
    nmat = jnp.where(strict, w[:, :L, :L], 0.0)
    mmat = jnp.where(strict, w[:, :L, L:], 0.0)
    p1 = jnp.where(incl, w[:, L:, :L], 0.0)
    p2 = jnp.where(incl, w[:, L:, L:], 0.0)
    x = jnp.where(diag, 1.0, 0.0) - nmat
    p = dot(nmat, nmat, _NN)
    span = 2
    while True:
        x = x + dot(x, p, _NN)
        span *= 2
        if span >= L:
            break
        p = dot(p, p, _NN)
    us = dot(ar, S, _NT)
    rhs = us[:, :L] + dot(mmat, v, _NN)
    d = -dot(x, rhs, _NN)
    dv = jnp.concatenate([d, v], axis=1)
    pp = jnp.concatenate([p1, p2], axis=2)
    _unchain(us[:, L:] + dot(pp, dv, _NN), yf_ref, yb_ref)
    bk_end = jnp.concatenate([b * e_end, k * e_end], axis=1)
    s_new = S * jnp.exp(cl) + dot(dv, bk_end, _TN)
    s_scr[...] = s_new
    _emit_state(sfin_ref, s_new)


def _gla_kernel(*refs, dot, has_s0, has_sfin):
    (qf, qb, kf, kb, vf, vb, laf, lab), s0_ref, of_ref, ob_ref, sfin_ref, s_scr = _split_refs(
        refs, 8, has_s0, has_sfin)
    _init_state(s_scr, s0_ref)
    q4 = _chains(qf, qb, 128)
    k4 = _chains(kf, kb, 128)
    la4 = _chains(laf, lab, 128)
    v = _chains(vf, vb, C_DV)
    g, L, lanes = q4.shape
    dk = C_DK
    n_sub = L // GLA_SUB
    st = s_scr[...]
    incl, _, _ = _dir_masks(L)
    c = _dot_exact_lhs(jnp.where(incl, 1.0, 0.0), la4, _NN)
    shape = (g, L, lanes)
    back = _is_back(shape)
    lane_blk = lax.broadcasted_iota(jnp.int32, shape, 2) // dk
    row_blk = lax.broadcasted_iota(jnp.int32, shape, 1) // GLA_SUB
    cref_f = jnp.zeros(shape, F32)
    cref_b = jnp.zeros(shape, F32)
    for j in range(1, n_sub):
        cref_f = jnp.where(lane_blk == j, c[:, j * GLA_SUB - 1:j * GLA_SUB], cref_f)
        cref_b = jnp.where(lane_blk == j - 1, c[:, j * GLA_SUB:j * GLA_SUB + 1], cref_b)
    cref = jnp.where(back, cref_b, cref_f)
    q_on = row_blk == lane_blk
    k_on = jnp.where(back, row_blk - lane_blk, lane_blk - row_blk) >= 0
    qh = jnp.where(q_on, q4 * jnp.exp(jnp.where(q_on, c - cref, 0.0)), 0.0)
    kh = jnp.where(k_on, k4 * jnp.exp(jnp.where(k_on, cref - c, 0.0)), 0.0)
    att = jnp.where(incl, dot(qh, kh, _NT), 0.0)
    cl = _chunk_end(c)
    qe = (q4 * jnp.exp(c))[:, :, :dk]
    ke = (k4 * jnp.exp(cl - c))[:, :, :dk]
    _unchain(dot(qe, st, _NT) + dot(att, v, _NN), of_ref, ob_ref)
    s_new = st * jnp.exp(cl[:, :, :dk]) + dot(v, ke, _TN)
    s_scr[...] = s_new
    _emit_state(sfin_ref, s_new)


def _recurrence_calls(kernel_fn, name, pairs, singles_f, singles_b, s0_lat, state_dims, out_width):
    def run(view, grid, group, s0, prev_out):
        nc = view[2]
        fwd_map = lambda p, c: (group(p), 0, c, 0, 0)
        bwd_map = lambda p, c: (group(p), 0, nc - 1 - c, 0, 0)
        blk = lambda w: (1, PAIR, 1, CHUNK, w)
        args, in_specs = [], []
        for af, ab in [(a, a) for a in pairs] + list(zip(singles_f, singles_b)):
            w = af.shape[-1]
            args += [af.reshape(view + (w,)), ab.reshape(view + (w,))]
            in_specs += [pl.BlockSpec(blk(w), fwd_map), pl.BlockSpec(blk(w), bwd_map)]
        out_specs = [pl.BlockSpec(blk(out_width), fwd_map), pl.BlockSpec(blk(out_width), bwd_map)]
        out_shape = [jax.ShapeDtypeStruct(view + (out_width,), F32)] * 2
        aliases = {}
        if s0 is not None:
            args += [s0] + [o.reshape(view + (out_width,)) for o in prev_out]
            in_specs += [_full(s0.shape), pl.BlockSpec(memory_space=pl.ANY), pl.BlockSpec(memory_space=pl.ANY)]
            aliases = {len(args) - 2: 0, len(args) - 1: 1}
        else:
            out_specs.append(pl.BlockSpec((1, N_CHAIN) + state_dims, lambda p, c: (p, 0, 0, 0)))
            out_shape.append(jax.ShapeDtypeStruct((grid[0], N_CHAIN) + state_dims, F32))
        return pl.pallas_call(
            functools.partial(kernel_fn, has_s0=s0 is not None, has_sfin=s0 is None),
            grid=grid, in_specs=in_specs, out_specs=out_specs, out_shape=out_shape,
            input_output_aliases=aliases, scratch_shapes=[pltpu.VMEM((N_CHAIN,) + state_dims, F32)],
            compiler_params=pltpu.CompilerParams(dimension_semantics=("parallel", "arbitrary")),
            name=name + ("_latent" if s0 is not None else "_context"),
        )(*args)

    ctx_nc = CTX_LEN // CHUNK
    ctx_view = (N_TOK // (PAIR * CTX_LEN), PAIR, ctx_nc, CHUNK)
    o_f, o_b, s_fin = run(ctx_view, (N_CTX_SEQ // PAIR, ctx_nc), lambda p: p, None, None)
    lat_nc = LAT_LEN // CHUNK
    lat_view = (N_TOK // (PAIR * LAT_LEN), PAIR, lat_nc, CHUNK)
    o_f, o_b = run(lat_view, (1, lat_nc), lambda p: N_CTX // (PAIR * LAT_LEN), s0_lat, (o_f, o_b))
    return o_f.reshape(N_TOK, out_width), o_b.reshape(N_TOK, out_width), s_fin


def _layer_norm(x, g, b):
    mu = jnp.mean(x, axis=-1, keepdims=True)
    xc = x - mu
    var = jnp.mean(xc * xc, axis=-1, keepdims=True)
    return xc * lax.rsqrt(var + LN_EPS) * g + b


def _merge_kernel(x_ref, mod_ref, oa_ref, ob_ref, cof_ref, cob_ref, cgate_ref, yf_ref, yb_ref, bonus_ref, dgate_ref,
                  wg_ref, wbr_ref, wout_ref, cnorm_ref, dlng_ref, dlnb_ref, lng_ref, lnb_ref, wr_ref, br_ref, bd_ref,
                  x1_o, h2_o, topi_o, topw_o):
    x = x_ref[...]
    m = mod_ref[0]
    sh1, sc1, g1, sh2, sc2 = m[0:1], m[1:2], m[2:3], m[3:4], m[4:5]
    bd = bd_ref[...]
    inv_n = 1.0 / D_N
    co = cof_ref[...] + cob_ref[...]
    o_c = co * lax.rsqrt(_dot_exact_rhs(co * co, bd) * inv_n + RMS_EPS) * cnorm_ref[...] * cgate_ref[...]
    y = yf_ref[...] + yb_ref[...]
    yc = y - _dot_exact_rhs(y, bd) * inv_n
    var = _dot_exact_rhs(yc * yc, bd) * inv_n
    o_d = (yc * lax.rsqrt(var + D_GN_EPS) * dlng_ref[...] + dlnb_ref[...] + bonus_ref[...]) * dgate_ref[...]
    branches = (oa_ref[...], ob_ref[...], o_c, o_d)
    h = (x * (1.0 + sc1) + sh1).astype(BF16)
    merged = None
    for n in range(N_BRANCH):
        gate = _sigmoid(jnp.dot(h, wg_ref[0, :, n * D_MODEL:(n + 1) * D_MODEL], preferred_element_type=F32))
        term = gate * jnp.dot(branches[n].astype(BF16), wbr_ref[0, n], preferred_element_type=F32)
        merged = term if merged is None else merged + term
    mix = jnp.dot(merged.astype(BF16), wout_ref[0], preferred_element_type=F32)
    x1 = _layer_norm(ALPHA * x + g1 * mix, lng_ref[...], lnb_ref[...])
    x1_o[...] = x1
    h2 = x1 * (1.0 + sc2) + sh2
    h2_o[...] = h2.astype(BF16)
    logits = _dot6(h2, wr_ref[...]) + br_ref[...]
    tm, n_e = logits.shape
    lane_e = lax.broadcasted_iota(jnp.int32, (tm, n_e), 1)
    lane_o = lax.broadcasted_iota(jnp.int32, (tm, topi_o.shape[1]), 1)
    top_i = jnp.zeros((tm, topi_o.shape[1]), jnp.int32)
    top_v = jnp.zeros((tm, topw_o.shape[1]), F32)
    vals = []
    for kth in range(TOP_K):
        mx = jnp.max(logits, axis=-1, keepdims=True)
        idx = jnp.min(jnp.where(logits == mx, lane_e, n_e), axis=-1, keepdims=True)
        vals.append(mx)
        top_i = jnp.where(lane_o == kth, idx, top_i)
        logits = jnp.where(lane_e == idx, -jnp.inf, logits)
    es = [jnp.exp(vk - vals[0]) for vk in vals]
    den = es[0] + es[1] + es[2] + es[3]
    for kth in range(TOP_K):
        top_v = jnp.where(lane_o == kth, es[kth] / den, top_v)
    topi_o[...] = top_i
    topw_o[...] = top_v


def merge_and_route(x, mod_l, o_a, o_b, co_f, co_b, cgate, y_f, y_b, bonus, dgate, w_g, w_br, w_out, layer, lp):
    tm = ROW_TILE
    hn = D_HEADS * D_N
    row = lambda w: pl.BlockSpec((tm, w), lambda t: (t, 0))
    small = [jnp.tile(lp['c_norm'], C_HEADS).reshape(1, hn), lp['d_ln_g'].reshape(1, hn), lp['d_ln_b'].reshape(1, hn),
             lp['ln_g'][0].reshape(1, -1), lp['ln_b'][0].reshape(1, -1), lp['w_router'],
             lp['b_router'].reshape(1, -1), _head_block_diag()]
    return pl.pallas_call(
        _merge_kernel,
        grid=(N_TILES,),
        in_specs=([row(D_MODEL), pl.BlockSpec((1, 6, D_MODEL), lambda t: (_mod_row(t), 0, 0))]
                  + [row(hn)] * 9 + [_layer_block(w, layer) for w in (w_g, w_br, w_out)]
                  + [_full(a.shape) for a in small]),
        out_specs=[row(D_MODEL), row(D_MODEL), row(128), row(128)],
        out_shape=[jax.ShapeDtypeStruct((N_TOK, D_MODEL), F32), jax.ShapeDtypeStruct((MOE_ROWS, D_MODEL), BF16),
                   jax.ShapeDtypeStruct((N_TOK, 128), jnp.int32), jax.ShapeDtypeStruct((N_TOK, 128), F32)],
        compiler_params=pltpu.CompilerParams(dimension_semantics=("parallel",), vmem_limit_bytes=VMEM_LIMIT),
        name="merge_and_route",
    )(x, mod_l, o_a, o_b, co_f, co_b, cgate, y_f, y_b, bonus, dgate, w_g, w_br, w_out, *small)


def _moe_kernel(te_ref, tv_ref, first_ref, slot_ref, next_ref, x_ref, w1_hbm, b1_ref, w2_hbm, b2_ref, perm_ref, y_ref,
                w1buf, w2buf, sem, w1s, w2s, hs, *, layer):
    t = pl.program_id(0)
    valid = tv_ref[t] != 0
    d_model, two_f = w1s.shape
    n_blk = two_f // MXU_WIDTH
    half = MXU_WIDTH // 2

    def fetch(expert, slot):
        return (pltpu.make_async_copy(w1_hbm.at[layer, expert], w1buf.at[slot], sem.at[0, slot]),
                pltpu.make_async_copy(w2_hbm.at[layer, expert], w2buf.at[slot], sem.at[1, slot]))

    @pl.when(t == 0)
    def _():
        for cp in fetch(te_ref[0], 0):
            cp.start()

    @pl.when(first_ref[t] == 1)
    def _():
        slot = slot_ref[t]
        for cp in fetch(te_ref[t], slot):
            cp.wait()

        @pl.when(next_ref[t] >= 0)
        def _():
            for cp in fetch(next_ref[t], 1 - slot):
                cp.start()

        for blk in range(n_blk):
            sl = slice(blk * MXU_WIDTH, (blk + 1) * MXU_WIDTH)
            wb = w1buf[slot, :, sl].astype(BF16)
            w1s[:, sl] = jnp.dot(wb, perm_ref[...], preferred_element_type=F32).astype(BF16)
        w2s[...] = w2buf[slot].astype(BF16)

    @pl.when(valid)
    def _():
        x = x_ref[...]
        for blk in range(n_blk):
            sl = slice(blk * MXU_WIDTH, (blk + 1) * MXU_WIDTH)
            u = jnp.dot(x, w1s[:, sl], preferred_element_type=F32) + b1_ref[0, 0, :, sl]
            glu = jnp.minimum(u[:, :half], SWIGLU_LIMIT)
            lin = jnp.clip(u[:, half:], -SWIGLU_LIMIT, SWIGLU_LIMIT)
            hs[:, blk * half:(blk + 1) * half] = (glu * _sigmoid(SWIGLU_ALPHA * glu) * (lin + 1.0)).astype(BF16)
        y = jnp.dot(hs[...], w2s[...], preferred_element_type=F32) + b2_ref[0, 0]
        y_ref[...] = y.astype(y_ref.dtype)

    @pl.when(jnp.logical_not(valid))
    def _():
        y_ref[...] = jnp.zeros_like(y_ref)


def _deinterleave_perm():
    half = MXU_WIDTH // 2
    src = np.arange(MXU_WIDTH)
    dst = np.where(src % 2 == 0, src // 2, half + src // 2)
    p = np.zeros((MXU_WIDTH, MXU_WIDTH), np.float32)
    p[src, dst] = 1.0
    return jnp.asarray(p, BF16)


def _moe_dispatch(top_i):
    n, k = top_i.shape
    tm = MOE_ROW_TILE
    p_rows = n * k + N_EXPERTS * tm
    experts = jnp.arange(N_EXPERTS, dtype=jnp.int32)
    onehot = top_i[:, :, None] == experts
    sel = jnp.sum(onehot.astype(jnp.int32), axis=1)
    before = jnp.cumsum(sel, axis=0) - sel
    counts = jnp.sum(sel, axis=0)
    padded = ((counts + tm - 1) // tm) * tm
    ends = jnp.cumsum(padded)
    starts = ends - padded
    pos = jnp.sum(jnp.where(onehot, (before + starts)[:, None, :], 0), axis=-1)
    n_tiles = p_rows // tm
    tile_start = jnp.arange(n_tiles, dtype=jnp.int32) * tm
    tile_valid = (tile_start < ends[-1]).astype(jnp.int32)
    last_tile = ends[-1] // tm - 1
    tile_expert = jnp.sum(ends[None, :] <= jnp.minimum(tile_start, last_tile * tm)[:, None], axis=1).astype(jnp.int32)
    keys = jnp.sort((top_i * n + jnp.arange(n, dtype=jnp.int32)[:, None]).reshape(-1))
    tile_onehot = tile_expert[:, None] == experts[None, :]
    lookup = lambda table: jnp.sum(jnp.where(tile_onehot, table[None, :], 0), axis=1)
    tile_rank0 = tile_start - lookup(starts)
    rank = tile_rank0[:, None] + jnp.arange(tm, dtype=jnp.int32)[None, :]
    sorted_at = jnp.clip(lookup(jnp.cumsum(counts) - counts)[:, None] + rank, 0, n * k - 1)
    tile_keys = keys[sorted_at.reshape(-1)].reshape(n_tiles, tm)
    filler = (tile_start[:, None] + jnp.arange(tm, dtype=jnp.int32)[None, :]) % n
    src_tok = jnp.where(rank < lookup(counts)[:, None], tile_keys % n, filler)
    is_first = jnp.concatenate([jnp.ones((1,), jnp.int32),
                                (tile_expert[1:] != tile_expert[:-1]).astype(jnp.int32)])
    slot = (jnp.cumsum(is_first) - 1) % 2
    later = jnp.logical_and(experts[None, :] > experts[:, None], (counts > 0)[None, :])
    next_of = jnp.min(jnp.where(later, experts[None, :], N_EXPERTS), axis=1)
    next_expert = lookup(jnp.where(next_of < N_EXPERTS, next_of, -1))
    tables = (tile_expert, tile_valid, is_first, slot.astype(jnp.int32), next_expert.astype(jnp.int32))
    return pos, src_tok.reshape(-1), tables, p_rows


def moe_experts(h2, top_i, layer, w1, b1, w2, b2):
    n = top_i.shape[0]
    d = h2.shape[1]
    depth, e, _, two_f = w1.shape
    f = two_f // 2
    tm = MOE_ROW_TILE
    pos, src_tok, tables, p_rows = _moe_dispatch(top_i)
    assert h2.shape[0] == p_rows
    xs = h2.at[lax.optimization_barrier(src_tok)].get(mode="promise_in_bounds")
    b1p = b1.reshape(depth, e, two_f // MXU_WIDTH, MXU_WIDTH // 2, 2).swapaxes(3, 4).reshape(depth, e, 1, two_f)
    expert_vec = lambda w: pl.BlockSpec((1, 1, 1, w), lambda t, te, *_: (layer, te[t], 0, 0))
    grid_spec = pltpu.PrefetchScalarGridSpec(
        num_scalar_prefetch=len(tables),
        grid=(p_rows // tm,),
        in_specs=[
            pl.BlockSpec((tm, d), lambda t, *_: (t, 0)),
            pl.BlockSpec(memory_space=pl.ANY),
            expert_vec(two_f),
            pl.BlockSpec(memory_space=pl.ANY),
            expert_vec(d),
            pl.BlockSpec((MXU_WIDTH, MXU_WIDTH), lambda t, *_: (0, 0)),
        ],
        out_specs=pl.BlockSpec((tm, d), lambda t, *_: (t, 0)),
        scratch_shapes=[pltpu.VMEM((2, d, two_f), F32), pltpu.VMEM((2, f, d), F32),
                        pltpu.SemaphoreType.DMA((2, 2)),
                        pltpu.VMEM((d, two_f), BF16), pltpu.VMEM((f, d), BF16), pltpu.VMEM((tm, f), BF16)],
    )
    ys = pl.pallas_call(
        functools.partial(_moe_kernel, layer=layer),
        grid_spec=grid_spec,
        out_shape=jax.ShapeDtypeStruct((p_rows, d), BF16),
        compiler_params=pltpu.CompilerParams(dimension_semantics=("arbitrary",),
                                             vmem_limit_bytes=48 * 1024 * 1024),
        name="moe_experts",
    )(*tables, xs, w1, b1p, w2, b2.reshape(depth, e, 1, d), _deinterleave_perm())
    return ys.at[lax.optimization_barrier(pos.T.reshape(-1))].get(mode="promise_in_bounds").reshape(TOP_K, n, d)


def _final_kernel(x1_ref, mod_ref, ys_ref, topw_ref, lng_ref, lnb_ref, o_ref):
    g2 = mod_ref[0, 5:6]
    moe = None
    for kth in range(TOP_K):
        term = ys_ref[kth].astype(F32) * topw_ref[:, kth:kth + 1]
        moe = term if moe is None else moe + term
    o_ref[...] = _layer_norm(ALPHA * x1_ref[...] + g2 * moe, lng_ref[...], lnb_ref[...])


def combine_and_norm(x1, mod_l, ys, top_w, ln_g, ln_b):
    tm = ROW_TILE
    return pl.pallas_call(
        _final_kernel,
        grid=(N_TILES,),
        in_specs=[pl.BlockSpec((tm, D_MODEL), lambda t: (t, 0)),
                  pl.BlockSpec((1, 6, D_MODEL), lambda t: (_mod_row(t), 0, 0)),
                  pl.BlockSpec((TOP_K, tm, D_MODEL), lambda t: (0, t, 0)),
                  pl.BlockSpec((tm, 128), lambda t: (t, 0)),
                  _full((1, D_MODEL)), _full((1, D_MODEL))],
        out_specs=pl.BlockSpec((tm, D_MODEL), lambda t: (t, 0)),
        out_shape=jax.ShapeDtypeStruct((N_TOK, D_MODEL), F32),
        compiler_params=pltpu.CompilerParams(dimension_semantics=("parallel",)),
        name="combine_and_norm",
    )(x1, mod_l, ys, top_w, ln_g.reshape(1, -1), ln_b.reshape(1, -1))


def kernel(x_prompt, x_sample, cache_a_k, cache_a_v, cache_b_ckv, cache_b_kpe, state_c, state_d, c,
           c_ctx, w_mod, b_mod, w_in, a_sink, b_q_norm, b_w_uq, b_kv_norm, b_w_ukv, c_w_gate, c_b_gate,
           c_norm, d_mu, d_w0, d_w2, d_a0, d_a2, d_g2, d_k_k, d_k_a, d_r_k, d_ln_g, d_ln_b, w_br, w_out,
           ln_g, ln_b, w_router, b_router, w_mlp1, b_mlp1, w_mlp2, b_mlp2):
    params = dict(b_q_norm=b_q_norm, b_w_uq=b_w_uq, b_kv_norm=b_kv_norm, b_w_ukv=b_w_ukv, c_w_gate=c_w_gate,
                  c_b_gate=c_b_gate, c_norm=c_norm, d_mu=d_mu, d_w0=d_w0, d_w2=d_w2, d_a0=d_a0, d_a2=d_a2, d_g2=d_g2,
                  d_k_k=d_k_k, d_k_a=d_k_a, d_r_k=d_r_k, d_ln_g=d_ln_g, d_ln_b=d_ln_b, ln_g=ln_g, ln_b=ln_b,
                  w_router=w_router, b_router=b_router)
    assert x_prompt.shape == (N_CTX_SEQ, CTX_LEN, D_MODEL) and x_sample.shape == (N_LAT_SEQ, LAT_LEN, D_MODEL)
    x = jnp.concatenate([x_prompt.reshape(N_CTX, D_MODEL), x_sample.reshape(-1, D_MODEL)], axis=0)
    cond8 = jnp.concatenate([c_ctx[None], c, jnp.zeros((8 - 1 - N_LAT_SEQ, D_MODEL), F32)], axis=0)
    mod = modulation_table(cond8, w_mod, b_mod)[:, :1 + N_LAT_SEQ].reshape(DEPTH, 1 + N_LAT_SEQ, 6, D_MODEL)
    tables = _rope_tables()
    w_small, w_g = prepare_in_weights(w_in)
    w_br_bf, w_out_bf = w_br.astype(BF16), w_out.astype(BF16)
    new = {name: [] for name in ("a_k", "a_v", "b_ckv", "b_kpe", "c", "d")}
    for l in range(DEPTH):
        lp = {name: val[l] for name, val in params.items()}
        (aq, ak, av, bq, bckv, bkpe, cq4, ck4, cla_f, cla_b, cv, cgate,
         r, v, kk, lw_f, lw_b, k_f, k_b, a_f, a_b, bonus, dgate) = mixer_prelude(x, mod[l], w_small, l, tables, lp)

        o_a = gqa_attention(aq, ak, av, a_sink[l], cache_a_k, cache_a_v, l)
        o_b = mla_attention(bq, bckv, bkpe, b_w_ukv[l], cache_b_ckv, cache_b_kpe, l)

        c_s0 = jnp.swapaxes(state_c[:, l], 3, 4).reshape(1, N_CHAIN, C_DV, C_DK)
        co_f, co_b, c_fin = _recurrence_calls(functools.partial(_gla_kernel, dot=_dot1), "gla", [cq4, ck4, cv],
                                              [cla_f], [cla_b], c_s0, (C_DV, C_DK), C_HEADS * C_DV)
        d_s0 = state_d[:, l].reshape(1, N_CHAIN, D_N, D_N)
        y_f, y_b, d_fin = _recurrence_calls(functools.partial(_rwkv_kernel, dot=_dot1), "rwkv7", [r, v, kk],
                                            [lw_f, k_f, a_f], [lw_b, k_b, a_b], d_s0, (D_N, D_N), D_HEADS * D_N)

        x1, h2, top_i, top_w = merge_and_route(x, mod[l], o_a, o_b, co_f, co_b, cgate, y_f, y_b, bonus, dgate,
                                               w_g, w_br_bf, w_out_bf, l, lp)
        ys = moe_experts(h2, top_i[:, :TOP_K], l, w_mlp1, b_mlp1, w_mlp2, b_mlp2)
        x = combine_and_norm(x1, mod[l], ys, top_w, ln_g[l, 1], ln_b[l, 1])

        new["a_k"].append(ak[:N_CTX].reshape(N_CTX_SEQ, CTX_LEN, A_KV_HEADS, A_HD).transpose(0, 2, 1, 3))
        new["a_v"].append(av[:N_CTX].reshape(N_CTX_SEQ, CTX_LEN, A_KV_HEADS, A_HD).transpose(0, 2, 1, 3))
        new["b_ckv"].append(bckv[:N_CTX].reshape(N_CTX_SEQ, CTX_LEN, B_KV_LORA))
        new["b_kpe"].append(bkpe[:N_CTX].reshape(N_CTX_SEQ, CTX_LEN, B_ROPE))
        new["c"].append(jnp.swapaxes(c_fin.reshape(N_CTX_SEQ, 2, C_HEADS, C_DV, C_DK), 3, 4))
        new["d"].append(d_fin.reshape(N_CTX_SEQ, 2, D_HEADS, D_N, D_N))
    y_prompt = x[:N_CTX].reshape(x_prompt.shape)
    y_sample = x[N_CTX:].reshape(x_sample.shape)
    return (y_prompt, y_sample, *(jnp.stack(new[name], axis=1) for name in ("a_k", "a_v", "b_ckv", "b_kpe", "c", "d")))
```

```python
import functools

import jax
import jax.numpy as jnp
import numpy as np
from jax import lax
from jax.experimental import pallas as pl
from jax.experimental.pallas import tpu as pltpu

F32 = jnp.float32
BF16 = jnp.bfloat16

MXU_WIDTH = 256
VMEM_LIMIT = 56 * 1024 * 1024

D_MODEL = 1024
DEPTH = 2
GRID_W = 64
ROPE_BASE = 10000.0
A_HEADS, A_KV_HEADS, A_HD = 4, 2, 64
B_HEADS, B_NOPE, B_ROPE, B_VD, B_Q_LORA, B_KV_LORA = 4, 64, 32, 64, 192, 128
C_HEADS, C_DK, C_DV, C_GATE_RANK, C_GATE_TEMP = 4, 32, 64, 16, 16.0
D_HEADS, D_N, D_DECAY_RANK, D_AAA_RANK, D_GATE_RANK, D_GN_EPS = 4, 64, 64, 64, 128, 64e-5
BRANCH_W = 256
N_BRANCH = 4
N_EXPERTS = 32
TOP_K = 4
SWIGLU_LIMIT = 7.0
SWIGLU_ALPHA = 1.702
ALPHA = (2 * DEPTH) ** 0.25
LN_EPS = 1e-5
RMS_EPS = 1e-6

N_CTX_SEQ, CTX_LEN = 16, 256
N_LAT_SEQ, LAT_LEN = 2, 2048
N_CTX = N_CTX_SEQ * CTX_LEN
N_TOK = N_CTX + N_LAT_SEQ * LAT_LEN
ROW_TILE = 256
N_TILES = N_TOK // ROW_TILE
CTX_TILES = N_CTX // ROW_TILE
LAT_TILES_PER_SEQ = LAT_LEN // ROW_TILE
N_SEQ = N_CTX_SEQ + N_LAT_SEQ
MLA_HEAD_LANES = 128
MOE_ROW_TILE = 256
MOE_ROWS = N_TOK * TOP_K + N_EXPERTS * MOE_ROW_TILE

_ORIG = dict(aq=(0, 256), ak=(256, 384), av=(384, 512), bcq=(512, 704), bckv=(704, 832), bkpe=(832, 864),
             cq=(864, 992), ck=(992, 1120), cv=(1120, 1376), cog=(1376, 1632), caf=(1632, 1648), cab=(1648, 1664),
             zd=(1664, 2816))
_ORDER = ("aq", "ak", "av", "cq", "ck", "cv", "cog", "zd", "bcq", "caf", "cab", "bkpe", "bckv")
COL = {}
_off = 0
for _name in _ORDER:
    _w = _ORIG[_name][1] - _ORIG[_name][0]
    COL[_name] = (_off, _off + _w)
    _off += _w
SMALL_COLS = _off
G_START = 2816


def _cs(name):
    return slice(*COL[name])


def _split3(x):
    hi = x.astype(BF16)
    r1 = x - hi.astype(F32)
    mid = r1.astype(BF16)
    lo = (r1 - mid.astype(F32)).astype(BF16)
    return hi, mid, lo


def _split2(x):
    hi = x.astype(BF16)
    lo = (x - hi.astype(F32)).astype(BF16)
    return hi, lo


def _bdot(a, b, dims):
    return lax.dot_general(a, b, dims, preferred_element_type=F32)


_D2 = (((1,), (0,)), ((), ()))
_D2T = (((1,), (1,)), ((), ()))
_NN = (((2,), (1,)), ((0,), (0,)))
_NT = (((2,), (2,)), ((0,), (0,)))
_TN = (((1,), (1,)), ((0,), (0,)))


def _dot1(a, b, dims=_D2):
    return _bdot(a.astype(BF16), b.astype(BF16), dims)


def _dot3(a, b, dims=_D2):
    ah, al = _split2(a)
    bh, bl = _split2(b)
    return _bdot(ah, bh, dims) + (_bdot(ah, bl, dims) + _bdot(al, bh, dims))


def _dot_exact_lhs(a01, b, dims=_D2):
    a = a01.astype(BF16)
    h, m, l = _split3(b)
    return _bdot(a, h, dims) + (_bdot(a, m, dims) + _bdot(a, l, dims))


def _dot_exact_rhs(a, b01, dims=_D2):
    b = b01.astype(BF16)
    h, m, l = _split3(a)
    return _bdot(h, b, dims) + (_bdot(m, b, dims) + _bdot(l, b, dims))


def _dot6(a, b, dims=_D2):
    ah, am, al = _split3(a)
    bh, bm, bl = _split3(b)
    return (_bdot(ah, bh, dims) + (_bdot(ah, bm, dims) + _bdot(am, bh, dims))
            + (_bdot(am, bm, dims) + (_bdot(ah, bl, dims) + _bdot(al, bh, dims))))


def _sigmoid(x):
    return 0.5 * jnp.tanh(0.5 * x) + 0.5


def _softplus(x):
    return jnp.maximum(x, 0.0) + jnp.log(1.0 + jnp.exp(-jnp.abs(x)))


def _mod_row(t):
    return jnp.where(t < CTX_TILES, 0, 1 + (t - CTX_TILES) // LAT_TILES_PER_SEQ)


def _full(shape):
    nd = len(shape)
    return pl.BlockSpec(shape, lambda *_: (0,) * nd)


MOD_COL_TILE = 1536


def _mod_kernel(c_ref, w_ref, b_ref, o_ref):
    c = c_ref[...]
    o_ref[0] = _dot3(c * _sigmoid(c), w_ref[0]) + b_ref[0]


def modulation_table(cond8, w_mod, b_mod):
    depth, d, six_d = w_mod.shape
    return pl.pallas_call(
        _mod_kernel,
        grid=(depth, six_d // MOD_COL_TILE),
        in_specs=[pl.BlockSpec((8, d), lambda l, j: (0, 0)),
                  pl.BlockSpec((1, d, MOD_COL_TILE), lambda l, j: (l, 0, j)),
                  pl.BlockSpec((1, 1, MOD_COL_TILE), lambda l, j: (l, 0, j))],
        out_specs=pl.BlockSpec((1, 8, MOD_COL_TILE), lambda l, j: (l, 0, j)),
        out_shape=jax.ShapeDtypeStruct((depth, 8, six_d), F32),
        compiler_params=pltpu.CompilerParams(dimension_semantics=("parallel", "parallel")),
        name="modulation",
    )(cond8, w_mod, b_mod.reshape(depth, 1, six_d))


WPREP_ROWS = 128


def _wprep_kernel(w_ref, small_ref, gate_ref):
    for name in _ORDER:
        lo, hi = _ORIG[name]
        small_ref[0, :, _cs(name)] = w_ref[0, :, lo:hi].astype(BF16)
    gate_ref[0] = w_ref[0, :, G_START:].astype(BF16)


def prepare_in_weights(w_in):
    depth, d, cols = w_in.shape
    return pl.pallas_call(
        _wprep_kernel,
        grid=(depth, d // WPREP_ROWS),
        in_specs=[pl.BlockSpec((1, WPREP_ROWS, cols), lambda l, r: (l, r, 0))],
        out_specs=[pl.BlockSpec((1, WPREP_ROWS, SMALL_COLS), lambda l, r: (l, r, 0)),
                   pl.BlockSpec((1, WPREP_ROWS, cols - G_START), lambda l, r: (l, r, 0))],
        out_shape=[jax.ShapeDtypeStruct((depth, d, SMALL_COLS), BF16),
                   jax.ShapeDtypeStruct((depth, d, cols - G_START), BF16)],
        compiler_params=pltpu.CompilerParams(dimension_semantics=("parallel", "parallel")),
        name="prepare_in_weights",
    )(w_in)


def _rot_pairs(x, half, lane_mod_base=0):
    w = x.shape[-1]
    lane = lax.broadcasted_iota(jnp.int32, (1, w), 1) - lane_mod_base
    first = (lane % (2 * half)) < half
    return jnp.where(first, -pltpu.roll(x, w - half, axis=1), pltpu.roll(x, half, axis=1))


def _pre_kernel(x_ref, xp_ref, xn_ref, mod_ref, w_ref, ca_ref, sa_ref, cb_ref, sb_ref, ck_ref, sk_ref,
                qnorm_ref, kvnorm_ref, wuq_ref, cwg_ref, cbg_ref, rep_ref, mu_ref, dw0_ref, dw2_ref, da0_ref,
                da2_ref, dg2_ref, dkk_ref, dka_ref, drk_ref, bd_ref,
                aq_o, ak_o, av_o, bq_o, bckv_o, bkpe_o, cq4_o, ck4_o, claf_o, clab_o, cv_o, cgate_o,
                r_o, v_o, kk_o, lwf_o, lwb_o, kf_o, kb_o, af_o, ab_o, bonus_o, dgate_o):
    t = pl.program_id(0)
    tm = x_ref.shape[0]
    sh1 = mod_ref[0, 0:1, :]
    sc1 = mod_ref[0, 1:2, :]

    def modulate(xv):
        return (xv * (1.0 + sc1) + sh1).astype(BF16)

    h_all = jnp.concatenate([modulate(x_ref[...]), modulate(xp_ref[...]), modulate(xn_ref[...])], axis=0)
    z_all = jnp.dot(h_all, w_ref[0], preferred_element_type=F32)
    z = z_all[:tm]

    aq = z[:, _cs("aq")]
    ak = z[:, _cs("ak")]
    aq_o[...] = aq * ca_ref[...] + _rot_pairs(aq, A_HD // 4) * sa_ref[...]
    ak_o[...] = ak * ca_ref[:, :ak.shape[1]] + _rot_pairs(ak, A_HD // 4) * sa_ref[:, :ak.shape[1]]
    av_o[...] = z[:, _cs("av")]

    bcq = z[:, _cs("bcq")]
    qn = bcq * lax.rsqrt(jnp.mean(bcq * bcq, axis=-1, keepdims=True) + RMS_EPS) * qnorm_ref[...]
    bq = _dot1(qn, wuq_ref[...])
    bq_o[...] = bq * cb_ref[...] + _rot_pairs(bq, B_ROPE // 4, lane_mod_base=B_NOPE) * sb_ref[...]
    bckv = z[:, _cs("bckv")]
    bckv_o[...] = bckv * lax.rsqrt(jnp.mean(bckv * bckv, axis=-1, keepdims=True) + RMS_EPS) * kvnorm_ref[...]
    kpe_lo = COL["bkpe"][0] // 128 * 128
    kblk = z[:, kpe_lo:kpe_lo + 128]
    kblk = kblk * ck_ref[...] + _rot_pairs(kblk, B_ROPE // 4) * sk_ref[...]
    bkpe_o[...] = kblk[:, COL["bkpe"][0] - kpe_lo:COL["bkpe"][1] - kpe_lo]

    rep = rep_ref[...]
    cq4_o[...] = _dot1(z[:, _cs("cq")] * (C_DK ** -0.5), rep)
    ck4_o[...] = _dot1(z[:, _cs("ck")], rep)
    cv_o[...] = z[:, _cs("cv")]
    cog = z[:, _cs("cog")]
    cgate_o[...] = cog * _sigmoid(cog)
    for direction, (name, out) in enumerate((("caf", claf_o), ("cab", clab_o))):
        pre = _dot3(z[:, _cs(name)], cwg_ref[direction]) + cbg_ref[direction]
        la_hi, la_lo = _split2(-_softplus(-pre) * (1.0 / C_GATE_TEMP))
        out[...] = _bdot(la_hi, rep, _D2) + _bdot(la_lo, rep, _D2)

    zd_cols = _cs("zd")
    zd = z[:, zd_cols]
    j = (t - CTX_TILES) % LAT_TILES_PER_SEQ
    latent = t >= CTX_TILES
    has_prev = jnp.logical_and(latent, j != 0)
    has_next = jnp.logical_and(latent, j != LAT_TILES_PER_SEQ - 1)
    prev_row = jnp.where(has_prev, z_all[tm + 7:tm + 8, zd_cols], 0.0)
    next_row = jnp.where(has_next, z_all[tm + 8:tm + 9, zd_cols], 0.0)
    row = lax.broadcasted_iota(jnp.int32, (tm, 1), 0)
    up = jnp.where(row == 0, prev_row, pltpu.roll(zd, 1, axis=0))
    dn = jnp.where(row == tm - 1, next_row, pltpu.roll(zd, tm - 1, axis=0))
    zd = zd + (0.5 * (up + dn) - zd) * mu_ref[...]

    hn = D_HEADS * D_N
    d_r, d_k, d_v = zd[:, :hn], zd[:, hn:2 * hn], zd[:, 2 * hn:3 * hn]
    o = 3 * hn
    d_w = (zd[:, o:o + D_DECAY_RANK], zd[:, o + D_DECAY_RANK:o + 2 * D_DECAY_RANK])
    o += 2 * D_DECAY_RANK
    d_a = (zd[:, o:o + D_AAA_RANK], zd[:, o + D_AAA_RANK:o + 2 * D_AAA_RANK])
    o += 2 * D_AAA_RANK
    d_g = zd[:, o:o + D_GATE_RANK]
    bd = bd_ref[...]
    kk = d_k * dkk_ref[...]
    kk = kk / jnp.maximum(jnp.sqrt(_dot_exact_rhs(kk * kk, bd)), 1e-12)
    r_o[...] = d_r
    v_o[...] = d_v
    kk_o[...] = kk
    k_sum = None
    for direction, (lw_o, k_o, a_o) in enumerate(((lwf_o, kf_o, af_o), (lwb_o, kb_o, ab_o))):
        w_log = -_softplus(-(dw0_ref[direction] + _dot3(jnp.tanh(d_w[direction]), dw2_ref[direction]))) - 0.5
        lw_o[...] = -jnp.exp(w_log)
        a = _sigmoid(da0_ref[direction] + _dot1(d_a[direction], da2_ref[direction]))
        k_dir = d_k * (1.0 + (a - 1.0) * dka_ref[...])
        k_o[...] = k_dir
        a_o[...] = a
        k_sum = k_dir if k_sum is None else k_sum + k_dir
    bonus_o[...] = d_v * _dot_exact_rhs(d_r * drk_ref[...] * k_sum, bd)
    dgate_o[...] = _dot1(_sigmoid(d_g), dg2_ref[...])


def _rope_tables():
    pos = np.arange(LAT_LEN)
    rowp, colp = (pos // GRID_W).astype(np.float32), (pos % GRID_W).astype(np.float32)

    f32 = np.float32

    def head_tables(rot_dim):
        quarter = rot_dim // 4
        inv = (f32(ROPE_BASE) ** (-np.arange(quarter, dtype=f32) / f32(quarter))).astype(f32)
        ar = (rowp[:, None] * inv).astype(f32)
        ac = (colp[:, None] * inv).astype(f32)
        cos = np.concatenate([np.cos(ar), np.cos(ar), np.cos(ac), np.cos(ac)], axis=-1).astype(f32)
        sin = np.concatenate([np.sin(ar), np.sin(ar), np.sin(ac), np.sin(ac)], axis=-1).astype(f32)
        return cos, sin

    def with_identity(c, s):
        w = c.shape[1]
        return (jnp.asarray(np.concatenate([np.ones((ROW_TILE, w), f32), c], axis=0)),
                jnp.asarray(np.concatenate([np.zeros((ROW_TILE, w), f32), s], axis=0)))

    ca, sa = head_tables(A_HD)
    ca, sa = with_identity(np.tile(ca, (1, A_HEADS)), np.tile(sa, (1, A_HEADS)))
    cbh, sbh = head_tables(B_ROPE)
    ones, zeros = np.ones((LAT_LEN, B_NOPE), f32), np.zeros((LAT_LEN, B_NOPE), f32)
    qpad = MLA_HEAD_LANES - B_NOPE - B_ROPE
    cb, sb = with_identity(
        np.tile(np.concatenate([ones, cbh, np.ones((LAT_LEN, qpad), f32)], axis=1), (1, B_HEADS)),
        np.tile(np.concatenate([zeros, sbh, np.zeros((LAT_LEN, qpad), f32)], axis=1), (1, B_HEADS)))
    pad = 128 - B_ROPE
    ck, sk = with_identity(np.concatenate([np.ones((LAT_LEN, pad), f32), cbh], axis=1),
                           np.concatenate([np.zeros((LAT_LEN, pad), f32), sbh], axis=1))
    return ca, sa, cb, sb, ck, sk


def _lane_repeat_matrix():
    m = np.zeros((C_HEADS * C_DK, C_HEADS * 128), np.float32)
    for h in range(C_HEADS):
        for g in range(128 // C_DK):
            for d in range(C_DK):
                m[h * C_DK + d, h * 128 + g * C_DK + d] = 1.0
    return jnp.asarray(m, BF16)


def _head_block_diag():
    m = np.kron(np.eye(D_HEADS, dtype=np.float32), np.ones((D_N, D_N), np.float32))
    return jnp.asarray(m, BF16)


PRE_OUT_WIDTHS = (256, 128, 128, B_HEADS * MLA_HEAD_LANES, 128, 32, 512, 512, 512, 512, 256, 256) + (256,) * 11


def _layer_block(arr, layer):
    nd = arr.ndim
    return pl.BlockSpec((1,) + arr.shape[1:], lambda *_: (layer,) + (0,) * (nd - 1))


def mixer_prelude(x, mod_l, w_small, layer, tables, lp):
    tm = ROW_TILE
    tab_idx = lambda t: (jnp.where(t < CTX_TILES, 0, 1 + (t - CTX_TILES) % LAT_TILES_PER_SEQ), 0)
    hn = D_HEADS * D_N
    w_uq = lp['b_w_uq'].reshape(B_Q_LORA, B_HEADS, B_NOPE + B_ROPE)
    w_uq = jnp.pad(w_uq, ((0, 0), (0, 0), (0, MLA_HEAD_LANES - B_NOPE - B_ROPE))).reshape(B_Q_LORA, -1)
    small = [lp['b_q_norm'].reshape(1, -1), lp['b_kv_norm'].reshape(1, -1), w_uq, lp['c_w_gate'],
             lp['c_b_gate'].reshape(2, 1, -1), _lane_repeat_matrix(), lp['d_mu'].reshape(1, -1),
             lp['d_w0'].reshape(2, 1, hn), lp['d_w2'], lp['d_a0'].reshape(2, 1, hn), lp['d_a2'], lp['d_g2'],
             lp['d_k_k'].reshape(1, hn), lp['d_k_a'].reshape(1, hn), lp['d_r_k'].reshape(1, hn), _head_block_diag()]
    in_specs = ([pl.BlockSpec((tm, D_MODEL), lambda t: (t, 0)),
                 pl.BlockSpec((8, D_MODEL), lambda t: (jnp.maximum(t * (tm // 8) - 1, 0), 0)),
                 pl.BlockSpec((8, D_MODEL), lambda t: (jnp.minimum((t + 1) * (tm // 8), N_TOK // 8 - 1), 0)),
                 pl.BlockSpec((1, 6, D_MODEL), lambda t: (_mod_row(t), 0, 0)),
                 _layer_block(w_small, layer)]
                + [pl.BlockSpec((tm, tab.shape[1]), tab_idx) for tab in tables]
                + [_full(a.shape) for a in small])
    return pl.pallas_call(
        _pre_kernel,
        grid=(N_TILES,),
        in_specs=in_specs,
        out_specs=[pl.BlockSpec((tm, w), lambda t: (t, 0)) for w in PRE_OUT_WIDTHS],
        out_shape=[jax.ShapeDtypeStruct((N_TOK, w), F32) for w in PRE_OUT_WIDTHS],
        compiler_params=pltpu.CompilerParams(dimension_semantics=("parallel",), vmem_limit_bytes=VMEM_LIMIT),
        name="mixer_prelude",
    )(x, x, x, mod_l, w_small, *tables, *small)


ATT_Q_BLOCK = 128
MLA_Q_BLOCK = 256
ATT_WINDOW = 128
ATT_NEG_INF = -1e30
CACHE_LEN = 512


def _softmax_pv(s, v, sink):
    dv = v.shape[1] // 2
    m = jnp.max(s, axis=-1, keepdims=True)
    if sink is not None:
        m = jnp.maximum(m, sink)
    e = jnp.exp((s - m).astype(BF16))
    o = jnp.dot(e, v, preferred_element_type=F32)
    den = o[:, dv:dv + 1]
    if sink is not None:
        den = den + jnp.exp(sink - m)
    return o[:, :dv] / den


def _with_ones(v):
    return jnp.concatenate([v.astype(BF16), jnp.ones(v.shape, BF16)], axis=1)


def _gqa_kernel(sink_ref, q_ref, k_ref, v_ref, *rest, hd, group, scale, windowed):
    if windowed:
        kp_ref, kn_ref, vp_ref, vn_ref, kc_ref, vc_ref, _, o_ref = rest
    else:
        (o_ref,) = rest
    i = pl.program_id(1)
    tq = q_ref.shape[0]
    n_kv = k_ref.shape[1] // hd
    if windowed:
        qpos = i * tq + lax.broadcasted_iota(jnp.int32, (tq, 3 * tq), 0)
        kpos = (i - 1) * tq + lax.broadcasted_iota(jnp.int32, (tq, 3 * tq), 1)
        n_tok = pl.num_programs(1) * tq
        mask = (jnp.abs(qpos - kpos) <= ATT_WINDOW) & (kpos >= 0) & (kpos < n_tok)
        mask = jnp.concatenate([mask] * group, axis=0)
    for kvh in range(n_kv):
        ks = slice(kvh * hd, (kvh + 1) * hd)
        qs = [q_ref[:, (kvh * group + g) * hd:(kvh * group + g + 1) * hd] for g in range(group)]
        q = (jnp.concatenate(qs, axis=0) * scale).astype(BF16)
        sink = jnp.concatenate(
            [jnp.full((tq, 1), sink_ref[kvh * group + g], F32) for g in range(group)], axis=0)
        if windowed:
            k_win = jnp.concatenate([kp_ref[:, ks], k_ref[:, ks], kn_ref[:, ks]], axis=0)
            v_win = jnp.concatenate([vp_ref[:, ks], v_ref[:, ks], vn_ref[:, ks]], axis=0)
            s_win = _bdot(q, k_win.astype(BF16), _D2T)
            s_win = jnp.where(mask, s_win, ATT_NEG_INF)
            s_ctx = _bdot(q, kc_ref[0, 0, kvh].astype(BF16), _D2T)
            s = jnp.concatenate([s_win, s_ctx], axis=1)
            v = jnp.concatenate([v_win, vc_ref[0, 0, kvh]], axis=0)
        else:
            s = _bdot(q, k_ref[:, ks].astype(BF16), _D2T)
            v = v_ref[:, ks]
        o = _softmax_pv(s, _with_ones(v), sink)
        for g in range(group):
            h = kvh * group + g
            o_ref[:, h * hd:(h + 1) * hd] = o[g * tq:(g + 1) * tq]


def gqa_attention(q, k, v, sink, cache_k, cache_v, layer):
    qw, kw = q.shape[1], k.shape[1]
    group = qw // kw
    scale = A_HD ** -0.5
    params = pltpu.CompilerParams(dimension_semantics=("parallel", "parallel"))
    out_shape = jax.ShapeDtypeStruct((N_TOK, qw), F32)
    ctx_spec = lambda w: pl.BlockSpec((CTX_LEN, w), lambda s, i, sk: (s, 0))
    o = pl.pallas_call(
        functools.partial(_gqa_kernel, hd=A_HD, group=group, scale=scale, windowed=False),
        grid_spec=pltpu.PrefetchScalarGridSpec(
            num_scalar_prefetch=1, grid=(N_CTX_SEQ, 1), in_specs=[ctx_spec(qw), ctx_spec(kw), ctx_spec(kw)],
            out_specs=ctx_spec(qw)),
        out_shape=out_shape, compiler_params=params, name="gqa_full",
    )(sink, q, k, v)
    tq = ATT_Q_BLOCK
    nb = LAT_LEN // tq
    base = N_CTX // tq
    blk = lambda w, f: pl.BlockSpec((tq, w), lambda b, i, sk: (base + nb * b + f(i), 0))
    same = lambda i: i
    prev = lambda i: jnp.maximum(i - 1, 0)
    nxt = lambda i: jnp.minimum(i + 1, nb - 1)
    cspec = pl.BlockSpec((1, 1) + cache_k.shape[2:], lambda b, i, sk: (b, layer, 0, 0, 0))
    return pl.pallas_call(
        functools.partial(_gqa_kernel, hd=A_HD, group=group, scale=scale, windowed=True),
        grid_spec=pltpu.PrefetchScalarGridSpec(
            num_scalar_prefetch=1, grid=(N_LAT_SEQ, nb),
            in_specs=[blk(qw, same), blk(kw, same), blk(kw, same), blk(kw, prev), blk(kw, nxt), blk(kw, prev),
                      blk(kw, nxt), cspec, cspec, pl.BlockSpec(memory_space=pl.ANY)],
            out_specs=blk(qw, same)),
        out_shape=out_shape, input_output_aliases={10: 0}, compiler_params=params, name="gqa_windowed",
    )(sink, q, k, v, k, k, v, v, cache_k, cache_v, o)


def _mla_kernel(q_ref, ckv_ref, kpe_ref, wukv_ref, *rest, n_heads, nope, rope, vd, scale, cached):
    if cached:
        cckv_ref, ckpe_ref, _, o_ref, k_scr, vext_scr = rest
    else:
        o_ref, k_scr, vext_scr = rest
    i = pl.program_id(1)
    n_cache = k_scr.shape[0] - ckv_ref.shape[0]
    hw = nope + vd
    hl = MLA_HEAD_LANES

    @pl.when(i == 0)
    def _():
        w = wukv_ref[...].astype(BF16)

        def expand(rows, kpe_rows, lo, hi):
            kv = jnp.dot(rows.astype(BF16), w, preferred_element_type=F32).astype(BF16)
            n = hi - lo
            kpe = kpe_rows.astype(BF16)
            for h in range(n_heads):
                k_scr[lo:hi, hl * h:hl * (h + 1)] = jnp.concatenate(
                    [kv[:, h * hw:h * hw + nope], kpe, jnp.zeros((n, hl - nope - rope), BF16)], axis=1)
                vext_scr[lo:hi, 2 * vd * h:2 * vd * (h + 1)] = jnp.concatenate(
                    [kv[:, h * hw + nope:(h + 1) * hw], jnp.ones((n, vd), BF16)], axis=1)

        if cached:
            expand(cckv_ref[0, 0], ckpe_ref[0, 0], 0, n_cache)
        expand(ckv_ref[...], kpe_ref[...], n_cache, k_scr.shape[0])

    for h in range(n_heads):
        qh = (q_ref[:, hl * h:hl * (h + 1)] * scale).astype(BF16)
        s = _bdot(qh, k_scr[:, hl * h:hl * (h + 1)], _D2T)
        o_ref[:, h * vd:(h + 1) * vd] = _softmax_pv(s, vext_scr[:, 2 * vd * h:2 * vd * (h + 1)], None)


def mla_attention(q, ckv, kpe, w_ukv, cache_ckv, cache_kpe, layer):
    qw = q.shape[1]
    tq = MLA_Q_BLOCK
    kw = dict(n_heads=B_HEADS, nope=B_NOPE, rope=B_ROPE, vd=B_VD, scale=(B_NOPE + B_ROPE) ** -0.5)
    params = pltpu.CompilerParams(dimension_semantics=("parallel", "arbitrary"))
    out_shape = jax.ShapeDtypeStruct((N_TOK, B_HEADS * B_VD), F32)
    scratch = lambda rows: [pltpu.VMEM((rows, B_HEADS * MLA_HEAD_LANES), BF16),
                            pltpu.VMEM((rows, 2 * B_HEADS * B_VD), BF16)]
    nbc = CTX_LEN // tq
    o = pl.pallas_call(
        functools.partial(_mla_kernel, cached=False, **kw),
        grid=(N_CTX_SEQ, nbc),
        in_specs=[pl.BlockSpec((tq, qw), lambda s, i: (s * nbc + i, 0)),
                  pl.BlockSpec((CTX_LEN, B_KV_LORA), lambda s, i: (s, 0)),
                  pl.BlockSpec((CTX_LEN, B_ROPE), lambda s, i: (s, 0)),
                  _full(w_ukv.shape)],
        out_specs=pl.BlockSpec((tq, B_HEADS * B_VD), lambda s, i: (s * nbc + i, 0)),
        out_shape=out_shape,
        scratch_shapes=scratch(CTX_LEN),
        compiler_params=params, name="mla_context",
    )(q, ckv, kpe, w_ukv)
    nb = LAT_LEN // tq
    base = N_CTX // tq
    lat0 = N_CTX // LAT_LEN
    s_len = CACHE_LEN + LAT_LEN
    return pl.pallas_call(
        functools.partial(_mla_kernel, cached=True, **kw),
        grid=(N_LAT_SEQ, nb),
        in_specs=[pl.BlockSpec((tq, qw), lambda b, i: (base + nb * b + i, 0)),
                  pl.BlockSpec((LAT_LEN, B_KV_LORA), lambda b, i: (lat0 + b, 0)),
                  pl.BlockSpec((LAT_LEN, B_ROPE), lambda b, i: (lat0 + b, 0)),
                  _full(w_ukv.shape),
                  pl.BlockSpec((1, 1, CACHE_LEN, B_KV_LORA), lambda b, i: (b, layer, 0, 0)),
                  pl.BlockSpec((1, 1, CACHE_LEN, B_ROPE), lambda b, i: (b, layer, 0, 0)),
                  pl.BlockSpec(memory_space=pl.ANY)],
        out_specs=pl.BlockSpec((tq, B_HEADS * B_VD), lambda b, i: (base + nb * b + i, 0)),
        out_shape=out_shape, input_output_aliases={6: 0},
        scratch_shapes=scratch(s_len),
        compiler_params=params, name="mla_latent",
    )(q, ckv, kpe, w_ukv, cache_ckv, cache_kpe, o)


CHUNK = 64
GLA_SUB = 16
PAIR = 2
N_CHAIN = PAIR * 2 * 4


def _is_back(shape):
    return (lax.broadcasted_iota(jnp.int32, shape, 0) // 4) % 2 == 1


def _chains(ref_f, ref_b, width):
    return jnp.stack([ref[0, s, 0, :, h * width:(h + 1) * width]
                      for s in range(PAIR) for ref in (ref_f, ref_b) for h in range(4)], axis=0)


def _unchain(y, o_f, o_b):
    for s in range(PAIR):
        o_f[0, s, 0] = jnp.concatenate([y[s * 8 + h] for h in range(4)], axis=-1)
        o_b[0, s, 0] = jnp.concatenate([y[s * 8 + 4 + h] for h in range(4)], axis=-1)


def _dir_masks(L):
    shape = (N_CHAIN, L, L)
    back = _is_back(shape)
    row = lax.broadcasted_iota(jnp.int32, shape, 1)
    col = lax.broadcasted_iota(jnp.int32, shape, 2)
    ahead = jnp.where(back, col - row, row - col)
    return ahead >= 0, ahead > 0, row == col


def _chunk_end(ci):
    L = ci.shape[1]
    return jnp.where(_is_back((N_CHAIN, 1, 1)), ci[:, 0:1], ci[:, L - 1:L])


def _split_refs(refs, n_in, has_s0, has_sfin):
    ins = refs[:n_in]
    pos = n_in
    s0_ref = None
    if has_s0:
        s0_ref = refs[pos]
        pos += 3
    of_ref, ob_ref = refs[pos], refs[pos + 1]
    pos += 2
    sfin_ref = refs[pos] if has_sfin else None
    return ins, s0_ref, of_ref, ob_ref, sfin_ref, refs[-1]


def _init_state(s_scr, s0_ref):
    @pl.when(pl.program_id(1) == 0)
    def _():
        if s0_ref is None:
            s_scr[...] = jnp.zeros_like(s_scr)
        else:
            s_scr[...] = s0_ref[0]


def _emit_state(sfin_ref, s_new):
    if sfin_ref is None:
        return

    @pl.when(pl.program_id(1) == pl.num_programs(1) - 1)
    def _():
        sfin_ref[0] = s_new


def _rwkv_kernel(*refs, dot, has_s0, has_sfin):
    (rf, rb, vf, vb, kkf, kkb, lwf, lwb, kf, kb, af, ab), s0_ref, yf_ref, yb_ref, sfin_ref, s_scr = _split_refs(
        refs, 12, has_s0, has_sfin)
    _init_state(s_scr, s0_ref)
    n = D_N
    r = _chains(rf, rb, n)
    v = _chains(vf, vb, n)
    kk = _chains(kkf, kkb, n)
    lw = _chains(lwf, lwb, n)
    k = _chains(kf, kb, n)
    a = _chains(af, ab, n)
    L = r.shape[1]
    S = s_scr[...]
    incl, strict, diag = _dir_masks(L)
    ci = _dot_exact_lhs(jnp.where(incl, 1.0, 0.0), lw, _NN)
    ce = ci - lw
    cl = _chunk_end(ci)
    e_neg = jnp.exp(-ci)
    b = a * kk
    alpha = kk * jnp.exp(ce)
    rho = r * jnp.exp(ci)
    beta = b * e_neg
    kappa = k * e_neg
    e_end = jnp.exp(cl - ci)
    ar = jnp.concatenate([alpha, rho], axis=1)
    bk = jnp.concatenate([beta, kappa], axis=1)
    w = dot(ar, bk, _NT)
    nmat = jnp.where(strict, w[:, :L, :L], 0.0)
    mmat = jnp.where(strict, w[:, :L, L:], 0.0)
    p1 = jnp.where(incl, w[:, L:, :L], 0.0)
    p2 = jnp.where(incl, w[:, L:, L:], 0.0)
    x = jnp.where(diag, 1.0, 0.0) - nmat
    p = dot(nmat, nmat, _NN)
    span = 2
    while True:
        x = x + dot(x, p, _NN)
        span *= 2
        if span >= L:
            break
        p = dot(p, p, _NN)
    us = dot(ar, S, _NT)
    rhs = us[:, :L] + dot(mmat, v, _NN)
    d = -dot(x, rhs, _NN)
    dv = jnp.concatenate([d, v], axis=1)
    pp = jnp.concatenate([p1, p2], axis=2)
    _unchain(us[:, L:] + dot(pp, dv, _NN), yf_ref, yb_ref)
    bk_end = jnp.concatenate([b * e_end, k * e_end], axis=1)
    s_new = S * jnp.exp(cl) + dot(dv, bk_end, _TN)
    s_scr[...] = s_new
    _emit_state(sfin_ref, s_new)


def _gla_kernel(*refs, dot, has_s0, has_sfin):
    (qf, qb, kf, kb, vf, vb, laf, lab), s0_ref, of_ref, ob_ref, sfin_ref, s_scr = _split_refs(
        refs, 8, has_s0, has_sfin)
    _init_state(s_scr, s0_ref)
    q4 = _chains(qf, qb, 128)
    k4 = _chains(kf, kb, 128)
    la4 = _chains(laf, lab, 128)
    v = _chains(vf, vb, C_DV)
    g, L, lanes = q4.shape
    dk = C_DK
    n_sub = L // GLA_SUB
    st = s_scr[...]
    incl, _, _ = _dir_masks(L)
    c = _dot_exact_lhs(jnp.where(incl, 1.0, 0.0), la4, _NN)
    shape = (g, L, lanes)
    back = _is_back(shape)
    lane_blk = lax.broadcasted_iota(jnp.int32, shape, 2) // dk
    row_blk = lax.broadcasted_iota(jnp.int32, shape, 1) // GLA_SUB
    cref_f = jnp.zeros(shape, F32)
    cref_b = jnp.zeros(shape, F32)
    for j in range(1, n_sub):
        cref_f = jnp.where(lane_blk == j, c[:, j * GLA_SUB - 1:j * GLA_SUB], cref_f)
        cref_b = jnp.where(lane_blk == j - 1, c[:, j * GLA_SUB:j * GLA_SUB + 1], cref_b)
    cref = jnp.where(back, cref_b, cref_f)
    q_on = row_blk == lane_blk
    k_on = jnp.where(back, row_blk - lane_blk, lane_blk - row_blk) >= 0
    qh = jnp.where(q_on, q4 * jnp.exp(jnp.where(q_on, c - cref, 0.0)), 0.0)
    kh = jnp.where(k_on, k4 * jnp.exp(jnp.where(k_on, cref - c, 0.0)), 0.0)
    att = jnp.where(incl, dot(qh, kh, _NT), 0.0)
    cl = _chunk_end(c)
    qe = (q4 * jnp.exp(c))[:, :, :dk]
    ke = (k4 * jnp.exp(cl - c))[:, :, :dk]
    _unchain(dot(qe, st, _NT) + dot(att, v, _NN), of_ref, ob_ref)
    s_new = st * jnp.exp(cl[:, :, :dk]) + dot(v, ke, _TN)
    s_scr[...] = s_new
    _emit_state(sfin_ref, s_new)


def _recurrence_calls(kernel_fn, name, pairs, singles_f, singles_b, s0_lat, state_dims, out_width):
    def run(view, grid, group, s0, prev_out):
        nc = view[2]
        fwd_map = lambda p, c: (group(p), 0, c, 0, 0)
        bwd_map = lambda p, c: (group(p), 0, nc - 1 - c, 0, 0)
        blk = lambda w: (1, PAIR, 1, CHUNK, w)
        args, in_specs = [], []
        for af, ab in [(a, a) for a in pairs] + list(zip(singles_f, singles_b)):
            w = af.shape[-1]
            args += [af.reshape(view + (w,)), ab.reshape(view + (w,))]
            in_specs += [pl.BlockSpec(blk(w), fwd_map), pl.BlockSpec(blk(w), bwd_map)]
        out_specs = [pl.BlockSpec(blk(out_width), fwd_map), pl.BlockSpec(blk(out_width), bwd_map)]
        out_shape = [jax.ShapeDtypeStruct(view + (out_width,), F32)] * 2
        aliases = {}
        if s0 is not None:
            args += [s0] + [o.reshape(view + (out_width,)) for o in prev_out]
            in_specs += [_full(s0.shape), pl.BlockSpec(memory_space=pl.ANY), pl.BlockSpec(memory_space=pl.ANY)]
            aliases = {len(args) - 2: 0, len(args) - 1: 1}
        else:
            out_specs.append(pl.BlockSpec((1, N_CHAIN) + state_dims, lambda p, c: (p, 0, 0, 0)))
            out_shape.append(jax.ShapeDtypeStruct((grid[0], N_CHAIN) + state_dims, F32))
        return pl.pallas_call(
            functools.partial(kernel_fn, has_s0=s0 is not None, has_sfin=s0 is None),
            grid=grid, in_specs=in_specs, out_specs=out_specs, out_shape=out_shape,
            input_output_aliases=aliases, scratch_shapes=[pltpu.VMEM((N_CHAIN,) + state_dims, F32)],
            compiler_params=pltpu.CompilerParams(dimension_semantics=("parallel", "arbitrary")),
            name=name + ("_latent" if s0 is not None else "_context"),
        )(*args)

    ctx_nc = CTX_LEN // CHUNK
    ctx_view = (N_TOK // (PAIR * CTX_LEN), PAIR, ctx_nc, CHUNK)
    o_f, o_b, s_fin = run(ctx_view, (N_CTX_SEQ // PAIR, ctx_nc), lambda p: p, None, None)
    lat_nc = LAT_LEN // CHUNK
    lat_view = (N_TOK // (PAIR * LAT_LEN), PAIR, lat_nc, CHUNK)
    o_f, o_b = run(lat_view, (1, lat_nc), lambda p: N_CTX // (PAIR * LAT_LEN), s0_lat, (o_f, o_b))
    return o_f.reshape(N_TOK, out_width), o_b.reshape(N_TOK, out_width), s_fin


def _layer_norm(x, g, b):
    mu = jnp.mean(x, axis=-1, keepdims=True)
    xc = x - mu
    var = jnp.mean(xc * xc, axis=-1, keepdims=True)
    return xc * lax.rsqrt(var + LN_EPS) * g + b


def _merge_kernel(x_ref, mod_ref, oa_ref, ob_ref, cof_ref, cob_ref, cgate_ref, yf_ref, yb_ref, bonus_ref, dgate_ref,
                  wg_ref, wbr_ref, wout_ref, cnorm_ref, dlng_ref, dlnb_ref, lng_ref, lnb_ref, wr_ref, br_ref, bd_ref,
                  x1_o, h2_o, topi_o, topw_o):
    x = x_ref[...]
    m = mod_ref[0]
    sh1, sc1, g1, sh2, sc2 = m[0:1], m[1:2], m[2:3], m[3:4], m[4:5]
    bd = bd_ref[...]
    inv_n = 1.0 / D_N
    co = cof_ref[...] + cob_ref[...]
    o_c = co * lax.rsqrt(_dot_exact_rhs(co * co, bd) * inv_n + RMS_EPS) * cnorm_ref[...] * cgate_ref[...]
    y = yf_ref[...] + yb_ref[...]
    yc = y - _dot_exact_rhs(y, bd) * inv_n
    var = _dot_exact_rhs(yc * yc, bd) * inv_n
    o_d = (yc * lax.rsqrt(var + D_GN_EPS) * dlng_ref[...] + dlnb_ref[...] + bonus_ref[...]) * dgate_ref[...]
    branches = (oa_ref[...], ob_ref[...], o_c, o_d)
    h = (x * (1.0 + sc1) + sh1).astype(BF16)
    merged = None
    for n in range(N_BRANCH):
        gate = _sigmoid(jnp.dot(h, wg_ref[0, :, n * D_MODEL:(n + 1) * D_MODEL], preferred_element_type=F32))
        term = gate * jnp.dot(branches[n].astype(BF16), wbr_ref[0, n], preferred_element_type=F32)
        merged = term if merged is None else merged + term
    mix = jnp.dot(merged.astype(BF16), wout_ref[0], preferred_element_type=F32)
    x1 = _layer_norm(ALPHA * x + g1 * mix, lng_ref[...], lnb_ref[...])
    x1_o[...] = x1
    h2 = x1 * (1.0 + sc2) + sh2
    h2_o[...] = h2.astype(BF16)
    logits = _dot6(h2, wr_ref[...]) + br_ref[...]
    tm, n_e = logits.shape
    lane_e = lax.broadcasted_iota(jnp.int32, (tm, n_e), 1)
    lane_o = lax.broadcasted_iota(jnp.int32, (tm, topi_o.shape[1]), 1)
    top_i = jnp.zeros((tm, topi_o.shape[1]), jnp.int32)
    top_v = jnp.zeros((tm, topw_o.shape[1]), F32)
    vals = []
    for kth in range(TOP_K):
        mx = jnp.max(logits, axis=-1, keepdims=True)
        idx = jnp.min(jnp.where(logits == mx, lane_e, n_e), axis=-1, keepdims=True)
        vals.append(mx)
        top_i = jnp.where(lane_o == kth, idx, top_i)
        logits = jnp.where(lane_e == idx, -jnp.inf, logits)
    es = [jnp.exp(vk - vals[0]) for vk in vals]
    den = es[0] + es[1] + es[2] + es[3]
    for kth in range(TOP_K):
        top_v = jnp.where(lane_o == kth, es[kth] / den, top_v)
    topi_o[...] = top_i
    topw_o[...] = top_v


def merge_and_route(x, mod_l, o_a, o_b, co_f, co_b, cgate, y_f, y_b, bonus, dgate, w_g, w_br, w_out, layer, lp):
    tm = ROW_TILE
    hn = D_HEADS * D_N
    row = lambda w: pl.BlockSpec((tm, w), lambda t: (t, 0))
    small = [jnp.tile(lp['c_norm'], C_HEADS).reshape(1, hn), lp['d_ln_g'].reshape(1, hn), lp['d_ln_b'].reshape(1, hn),
             lp['ln_g'][0].reshape(1, -1), lp['ln_b'][0].reshape(1, -1), lp['w_router'],
             lp['b_router'].reshape(1, -1), _head_block_diag()]
    return pl.pallas_call(
        _merge_kernel,
        grid=(N_TILES,),
        in_specs=([row(D_MODEL), pl.BlockSpec((1, 6, D_MODEL), lambda t: (_mod_row(t), 0, 0))]
                  + [row(hn)] * 9 + [_layer_block(w, layer) for w in (w_g, w_br, w_out)]
                  + [_full(a.shape) for a in small]),
        out_specs=[row(D_MODEL), row(D_MODEL), row(128), row(128)],
        out_shape=[jax.ShapeDtypeStruct((N_TOK, D_MODEL), F32), jax.ShapeDtypeStruct((MOE_ROWS, D_MODEL), BF16),
                   jax.ShapeDtypeStruct((N_TOK, 128), jnp.int32), jax.ShapeDtypeStruct((N_TOK, 128), F32)],
        compiler_params=pltpu.CompilerParams(dimension_semantics=("parallel",), vmem_limit_bytes=VMEM_LIMIT),
        name="merge_and_route",
    )(x, mod_l, o_a, o_b, co_f, co_b, cgate, y_f, y_b, bonus, dgate, w_g, w_br, w_out, *small)


def _moe_kernel(te_ref, tv_ref, first_ref, slot_ref, next_ref, x_ref, w1_hbm, b1_ref, w2_hbm, b2_ref, perm_ref, y_ref,
                w1buf, w2buf, sem, w1s, w2s, hs, *, layer):
    t = pl.program_id(0)
    valid = tv_ref[t] != 0
    d_model, two_f = w1s.shape
    n_blk = two_f // MXU_WIDTH
    half = MXU_WIDTH // 2

    def fetch(expert, slot):
        return (pltpu.make_async_copy(w1_hbm.at[layer, expert], w1buf.at[slot], sem.at[0, slot]),
                pltpu.make_async_copy(w2_hbm.at[layer, expert], w2buf.at[slot], sem.at[1, slot]))

    @pl.when(t == 0)
    def _():
        for cp in fetch(te_ref[0], 0):
            cp.start()

    @pl.when(first_ref[t] == 1)
    def _():
        slot = slot_ref[t]
        for cp in fetch(te_ref[t], slot):
            cp.wait()

        @pl.when(next_ref[t] >= 0)
        def _():
            for cp in fetch(next_ref[t], 1 - slot):
                cp.start()

        for blk in range(n_blk):
            sl = slice(blk * MXU_WIDTH, (blk + 1) * MXU_WIDTH)
            wb = w1buf[slot, :, sl].astype(BF16)
            w1s[:, sl] = jnp.dot(wb, perm_ref[...], preferred_element_type=F32).astype(BF16)
        w2s[...] = w2buf[slot].astype(BF16)

    @pl.when(valid)
    def _():
        x = x_ref[...]
        for blk in range(n_blk):
            sl = slice(blk * MXU_WIDTH, (blk + 1) * MXU_WIDTH)
            u = jnp.dot(x, w1s[:, sl], preferred_element_type=F32) + b1_ref[0, 0, :, sl]
            glu = jnp.minimum(u[:, :half], SWIGLU_LIMIT)
            lin = jnp.clip(u[:, half:], -SWIGLU_LIMIT, SWIGLU_LIMIT)
            hs[:, blk * half:(blk + 1) * half] = (glu * _sigmoid(SWIGLU_ALPHA * glu) * (lin + 1.0)).astype(BF16)
        y = jnp.dot(hs[...], w2s[...], preferred_element_type=F32) + b2_ref[0, 0]
        y_ref[...] = y.astype(y_ref.dtype)

    @pl.when(jnp.logical_not(valid))
    def _():
        y_ref[...] = jnp.zeros_like(y_ref)


def _deinterleave_perm():
    half = MXU_WIDTH // 2
    src = np.arange(MXU_WIDTH)
    dst = np.where(src % 2 == 0, src // 2, half + src // 2)
    p = np.zeros((MXU_WIDTH, MXU_WIDTH), np.float32)
    p[src, dst] = 1.0
    return jnp.asarray(p, BF16)


def _moe_dispatch(top_i):
    n, k = top_i.shape
    tm = MOE_ROW_TILE
    p_rows = n * k + N_EXPERTS * tm
    experts = jnp.arange(N_EXPERTS, dtype=jnp.int32)
    onehot = top_i[:, :, None] == experts
    sel = jnp.sum(onehot.astype(jnp.int32), axis=1)
    before = jnp.cumsum(sel, axis=0) - sel
    counts = jnp.sum(sel, axis=0)
    padded = ((counts + tm - 1) // tm) * tm
    ends = jnp.cumsum(padded)
    starts = ends - padded
    pos = jnp.sum(jnp.where(onehot, (before + starts)[:, None, :], 0), axis=-1)
    n_tiles = p_rows // tm
    tile_start = jnp.arange(n_tiles, dtype=jnp.int32) * tm
    tile_valid = (tile_start < ends[-1]).astype(jnp.int32)
    last_tile = ends[-1] // tm - 1
    tile_expert = jnp.sum(ends[None, :] <= jnp.minimum(tile_start, last_tile * tm)[:, None], axis=1).astype(jnp.int32)
    keys = jnp.sort((top_i * n + jnp.arange(n, dtype=jnp.int32)[:, None]).reshape(-1))
    tile_onehot = tile_expert[:, None] == experts[None, :]
    lookup = lambda table: jnp.sum(jnp.where(tile_onehot, table[None, :], 0), axis=1)
    tile_rank0 = tile_start - lookup(starts)
    rank = tile_rank0[:, None] + jnp.arange(tm, dtype=jnp.int32)[None, :]
    sorted_at = jnp.clip(lookup(jnp.cumsum(counts) - counts)[:, None] + rank, 0, n * k - 1)
    tile_keys = keys[sorted_at.reshape(-1)].reshape(n_tiles, tm)
    filler = (tile_start[:, None] + jnp.arange(tm, dtype=jnp.int32)[None, :]) % n
    src_tok = jnp.where(rank < lookup(counts)[:, None], tile_keys % n, filler)
    is_first = jnp.concatenate([jnp.ones((1,), jnp.int32),
                                (tile_expert[1:] != tile_expert[:-1]).astype(jnp.int32)])
    slot = (jnp.cumsum(is_first) - 1) % 2
    later = jnp.logical_and(experts[None, :] > experts[:, None], (counts > 0)[None, :])
    next_of = jnp.min(jnp.where(later, experts[None, :], N_EXPERTS), axis=1)
    next_expert = lookup(jnp.where(next_of < N_EXPERTS, next_of, -1))
    tables = (tile_expert, tile_valid, is_first, slot.astype(jnp.int32), next_expert.astype(jnp.int32))
    return pos, src_tok.reshape(-1), tables, p_rows


def moe_experts(h2, top_i, layer, w1, b1, w2, b2):
    n = top_i.shape[0]
    d = h2.shape[1]
    depth, e, _, two_f = w1.shape
    f = two_f // 2
    tm = MOE_ROW_TILE
    pos, src_tok, tables, p_rows = _moe_dispatch(top_i)
    assert h2.shape[0] == p_rows
    xs = h2.at[lax.optimization_barrier(src_tok)].get(mode="promise_in_bounds")
    b1p = b1.reshape(depth, e, two_f // MXU_WIDTH, MXU_WIDTH // 2, 2).swapaxes(3, 4).reshape(depth, e, 1, two_f)
    expert_vec = lambda w: pl.BlockSpec((1, 1, 1, w), lambda t, te, *_: (layer, te[t], 0, 0))
    grid_spec = pltpu.PrefetchScalarGridSpec(
        num_scalar_prefetch=len(tables),
        grid=(p_rows // tm,),
        in_specs=[
            pl.BlockSpec((tm, d), lambda t, *_: (t, 0)),
            pl.BlockSpec(memory_space=pl.ANY),
            expert_vec(two_f),
            pl.BlockSpec(memory_space=pl.ANY),
            expert_vec(d),
            pl.BlockSpec((MXU_WIDTH, MXU_WIDTH), lambda t, *_: (0, 0)),
        ],
        out_specs=pl.BlockSpec((tm, d), lambda t, *_: (t, 0)),
        scratch_shapes=[pltpu.VMEM((2, d, two_f), F32), pltpu.VMEM((2, f, d), F32),
                        pltpu.SemaphoreType.DMA((2, 2)),
                        pltpu.VMEM((d, two_f), BF16), pltpu.VMEM((f, d), BF16), pltpu.VMEM((tm, f), BF16)],
    )
    ys = pl.pallas_call(
        functools.partial(_moe_kernel, layer=layer),
        grid_spec=grid_spec,
        out_shape=jax.ShapeDtypeStruct((p_rows, d), BF16),
        compiler_params=pltpu.CompilerParams(dimension_semantics=("arbitrary",),
                                             vmem_limit_bytes=48 * 1024 * 1024),
        name="moe_experts",
    )(*tables, xs, w1, b1p, w2, b2.reshape(depth, e, 1, d), _deinterleave_perm())
    return ys.at[lax.optimization_barrier(pos.T.reshape(-1))].get(mode="promise_in_bounds").reshape(TOP_K, n, d)


def _final_kernel(x1_ref, mod_ref, ys_ref, topw_ref, lng_ref, lnb_ref, o_ref):
    g2 = mod_ref[0, 5:6]
    moe = None
    for kth in range(TOP_K):
        term = ys_ref[kth].astype(F32) * topw_ref[:, kth:kth + 1]
        moe = term if moe is None else moe + term
    o_ref[...] = _layer_norm(ALPHA * x1_ref[...] + g2 * moe, lng_ref[...], lnb_ref[...])


def combine_and_norm(x1, mod_l, ys, top_w, ln_g, ln_b):
    tm = ROW_TILE
    return pl.pallas_call(
        _final_kernel,
        grid=(N_TILES,),
        in_specs=[pl.BlockSpec((tm, D_MODEL), lambda t: (t, 0)),
                  pl.BlockSpec((1, 6, D_MODEL), lambda t: (_mod_row(t), 0, 0)),
                  pl.BlockSpec((TOP_K, tm, D_MODEL), lambda t: (0, t, 0)),
                  pl.BlockSpec((tm, 128), lambda t: (t, 0)),
                  _full((1, D_MODEL)), _full((1, D_MODEL))],
        out_specs=pl.BlockSpec((tm, D_MODEL), lambda t: (t, 0)),
        out_shape=jax.ShapeDtypeStruct((N_TOK, D_MODEL), F32),
        compiler_params=pltpu.CompilerParams(dimension_semantics=("parallel",)),
        name="combine_and_norm",
    )(x1, mod_l, ys, top_w, ln_g.reshape(1, -1), ln_b.reshape(1, -1))


def kernel(x_prompt, x_sample, cache_a_k, cache_a_v, cache_b_ckv, cache_b_kpe, state_c, state_d, c,
           c_ctx, w_mod, b_mod, w_in, a_sink, b_q_norm, b_w_uq, b_kv_norm, b_w_ukv, c_w_gate, c_b_gate,
           c_norm, d_mu, d_w0, d_w2, d_a0, d_a2, d_g2, d_k_k, d_k_a, d_r_k, d_ln_g, d_ln_b, w_br, w_out,
           ln_g, ln_b, w_router, b_router, w_mlp1, b_mlp1, w_mlp2, b_mlp2):
    params = dict(b_q_norm=b_q_norm, b_w_uq=b_w_uq, b_kv_norm=b_kv_norm, b_w_ukv=b_w_ukv, c_w_gate=c_w_gate,
                  c_b_gate=c_b_gate, c_norm=c_norm, d_mu=d_mu, d_w0=d_w0, d_w2=d_w2, d_a0=d_a0, d_a2=d_a2, d_g2=d_g2,
                  d_k_k=d_k_k, d_k_a=d_k_a, d_r_k=d_r_k, d_ln_g=d_ln_g, d_ln_b=d_ln_b, ln_g=ln_g, ln_b=ln_b,
                  w_router=w_router, b_router=b_router)
    assert x_prompt.shape == (N_CTX_SEQ, CTX_LEN, D_MODEL) and x_sample.shape == (N_LAT_SEQ, LAT_LEN, D_MODEL)
    x = jnp.concatenate([x_prompt.reshape(N_CTX, D_MODEL), x_sample.reshape(-1, D_MODEL)], axis=0)
    cond8 = jnp.concatenate([c_ctx[None], c, jnp.zeros((8 - 1 - N_LAT_SEQ, D_MODEL), F32)], axis=0)
    mod = modulation_table(cond8, w_mod, b_mod)[:, :1 + N_LAT_SEQ].reshape(DEPTH, 1 + N_LAT_SEQ, 6, D_MODEL)
    tables = _rope_tables()
    w_small, w_g = prepare_in_weights(w_in)
    w_br_bf, w_out_bf = w_br.astype(BF16), w_out.astype(BF16)
    new = {name: [] for name in ("a_k", "a_v", "b_ckv", "b_kpe", "c", "d")}
    for l in range(DEPTH):
        lp = {name: val[l] for name, val in params.items()}
        (aq, ak, av, bq, bckv, bkpe, cq4, ck4, cla_f, cla_b, cv, cgate,
         r, v, kk, lw_f, lw_b, k_f, k_b, a_f, a_b, bonus, dgate) = mixer_prelude(x, mod[l], w_small, l, tables, lp)

        o_a = gqa_attention(aq, ak, av, a_sink[l], cache_a_k, cache_a_v, l)
        o_b = mla_attention(bq, bckv, bkpe, b_w_ukv[l], cache_b_ckv, cache_b_kpe, l)

        c_s0 = jnp.swapaxes(state_c[:, l], 3, 4).reshape(1, N_CHAIN, C_DV, C_DK)
        co_f, co_b, c_fin = _recurrence_calls(functools.partial(_gla_kernel, dot=_dot1), "gla", [cq4, ck4, cv],
                                              [cla_f], [cla_b], c_s0, (C_DV, C_DK), C_HEADS * C_DV)
        d_s0 = state_d[:, l].reshape(1, N_CHAIN, D_N, D_N)
        y_f, y_b, d_fin = _recurrence_calls(functools.partial(_rwkv_kernel, dot=_dot1), "rwkv7", [r, v, kk],
                                            [lw_f, k_f, a_f], [lw_b, k_b, a_b], d_s0, (D_N, D_N), D_HEADS * D_N)

        x1, h2, top_i, top_w = merge_and_route(x, mod[l], o_a, o_b, co_f, co_b, cgate, y_f, y_b, bonus, dgate,
                                               w_g, w_br_bf, w_out_bf, l, lp)
        ys = moe_experts(h2, top_i[:, :TOP_K], l, w_mlp1, b_mlp1, w_mlp2, b_mlp2)
        x = combine_and_norm(x1, mod[l], ys, top_w, ln_g[l, 1], ln_b[l, 1])

        new["a_k"].append(ak[:N_CTX].reshape(N_CTX_SEQ, CTX_LEN, A_KV_HEADS, A_HD).transpose(0, 2, 1, 3))
        new["a_v"].append(av[:N_CTX].reshape(N_CTX_SEQ, CTX_LEN, A_KV_HEADS, A_HD).transpose(0, 2, 1, 3))
        new["b_ckv"].append(bckv[:N_CTX].reshape(N_CTX_SEQ, CTX_LEN, B_KV_LORA))
        new["b_kpe"].append(bkpe[:N_CTX].reshape(N_CTX_SEQ, CTX_LEN, B_ROPE))
        new["c"].append(jnp.swapaxes(c_fin.reshape(N_CTX_SEQ, 2, C_HEADS, C_DV, C_DK), 3, 4))
        new["d"].append(d_fin.reshape(N_CTX_SEQ, 2, D_HEADS, D_N, D_N))
    y_prompt = x[:N_CTX].reshape(x_prompt.shape)
    y_sample = x[N_CTX:].reshape(x_sample.shape)
    return (y_prompt, y_sample, *(jnp.stack(new[name], axis=1) for name in ("a_k", "a_v", "b_ckv", "b_kpe", "c", "d")))
```

```python
import functools

import jax
import jax.numpy as jnp
import numpy as np
from jax import lax
from jax.experimental import pallas as pl
from jax.experimental.pallas import tpu as pltpu

F32 = jnp.float32
BF16 = jnp.bfloat16

MXU_WIDTH = 256
VMEM_LIMIT = 56 * 1024 * 1024

D_MODEL = 1024
DEPTH = 2
GRID_W = 64
ROPE_BASE = 10000.0
A_HEADS, A_KV_HEADS, A_HD = 4, 2, 64
B_HEADS, B_NOPE, B_ROPE, B_VD, B_Q_LORA, B_KV_LORA = 4, 64, 32, 64, 192, 128
C_HEADS, C_DK, C_DV, C_GATE_RANK, C_GATE_TEMP = 4, 32, 64, 16, 16.0
D_HEADS, D_N, D_DECAY_RANK, D_AAA_RANK, D_GATE_RANK, D_GN_EPS = 4, 64, 64, 64, 128, 64e-5
BRANCH_W = 256
N_BRANCH = 4
N_EXPERTS = 32
TOP_K = 4
SWIGLU_LIMIT = 7.0
SWIGLU_ALPHA = 1.702
ALPHA = (2 * DEPTH) ** 0.25
LN_EPS = 1e-5
RMS_EPS = 1e-6

N_CTX_SEQ, CTX_LEN = 16, 256
N_LAT_SEQ, LAT_LEN = 2, 2048
N_CTX = N_CTX_SEQ * CTX_LEN
N_TOK = N_CTX + N_LAT_SEQ * LAT_LEN
ROW_TILE = 256
N_TILES = N_TOK // ROW_TILE
CTX_TILES = N_CTX // ROW_TILE
LAT_TILES_PER_SEQ = LAT_LEN // ROW_TILE
N_SEQ = N_CTX_SEQ + N_LAT_SEQ
MLA_HEAD_LANES = 128
MOE_ROW_TILE = 256
MOE_ROWS = N_TOK * TOP_K + N_EXPERTS * MOE_ROW_TILE

_ORIG = dict(aq=(0, 256), ak=(256, 384), av=(384, 512), bcq=(512, 704), bckv=(704, 832), bkpe=(832, 864),
             cq=(864, 992), ck=(992, 1120), cv=(1120, 1376), cog=(1376, 1632), caf=(1632, 1648), cab=(1648, 1664),
             zd=(1664, 2816))
_ORDER = ("aq", "ak", "av", "cq", "ck", "cv", "cog", "zd", "bcq", "caf", "cab", "bkpe", "bckv")
COL = {}
_off = 0
for _name in _ORDER:
    _w = _ORIG[_name][1] - _ORIG[_name][0]
    COL[_name] = (_off, _off + _w)
    _off += _w
SMALL_COLS = _off
G_START = 2816


def _cs(name):
    return slice(*COL[name])


def _split3(x):
    hi = x.astype(BF16)
    r1 = x - hi.astype(F32)
    mid = r1.astype(BF16)
    lo = (r1 - mid.astype(F32)).astype(BF16)
    return hi, mid, lo


def _split2(x):
    hi = x.astype(BF16)
    lo = (x - hi.astype(F32)).astype(BF16)
    return hi, lo


def _bdot(a, b, dims):
    return lax.dot_general(a, b, dims, preferred_element_type=F32)


_D2 = (((1,), (0,)), ((), ()))
_D2T = (((1,), (1,)), ((), ()))
_NN = (((2,), (1,)), ((0,), (0,)))
_NT = (((2,), (2,)), ((0,), (0,)))
_TN = (((1,), (1,)), ((0,), (0,)))


def _dot1(a, b, dims=_D2):
    return _bdot(a.astype(BF16), b.astype(BF16), dims)


def _dot3(a, b, dims=_D2):
    ah, al = _split2(a)
    bh, bl = _split2(b)
    return _bdot(ah, bh, dims) + (_bdot(ah, bl, dims) + _bdot(al, bh, dims))


def _dot_exact_lhs(a01, b, dims=_D2):
    a = a01.astype(BF16)
    h, m, l = _split3(b)
    return _bdot(a, h, dims) + (_bdot(a, m, dims) + _bdot(a, l, dims))


def _dot_exact_rhs(a, b01, dims=_D2):
    b = b01.astype(BF16)
    h, m, l = _split3(a)
    return _bdot(h, b, dims) + (_bdot(m, b, dims) + _bdot(l, b, dims))


def _dot6(a, b, dims=_D2):
    ah, am, al = _split3(a)
    bh, bm, bl = _split3(b)
    return (_bdot(ah, bh, dims) + (_bdot(ah, bm, dims) + _bdot(am, bh, dims))
            + (_bdot(am, bm, dims) + (_bdot(ah, bl, dims) + _bdot(al, bh, dims))))


def _sigmoid(x):
    return 0.5 * jnp.tanh(0.5 * x) + 0.5


def _softplus(x):
    return jnp.maximum(x, 0.0) + jnp.log(1.0 + jnp.exp(-jnp.abs(x)))


def _mod_row(t):
    return jnp.where(t < CTX_TILES, 0, 1 + (t - CTX_TILES) // LAT_TILES_PER_SEQ)


def _full(shape):
    nd = len(shape)
    return pl.BlockSpec(shape, lambda *_: (0,) * nd)


MOD_COL_TILE = 1536


def _mod_kernel(c_ref, w_ref, b_ref, o_ref):
    c = c_ref[...]
    o_ref[0] = _dot3(c * _sigmoid(c), w_ref[0]) + b_ref[0]


def modulation_table(cond8, w_mod, b_mod):
    depth, d, six_d = w_mod.shape
    return pl.pallas_call(
        _mod_kernel,
        grid=(depth, six_d // MOD_COL_TILE),
        in_specs=[pl.BlockSpec((8, d), lambda l, j: (0, 0)),
                  pl.BlockSpec((1, d, MOD_COL_TILE), lambda l, j: (l, 0, j)),
                  pl.BlockSpec((1, 1, MOD_COL_TILE), lambda l, j: (l, 0, j))],
        out_specs=pl.BlockSpec((1, 8, MOD_COL_TILE), lambda l, j: (l, 0, j)),
        out_shape=jax.ShapeDtypeStruct((depth, 8, six_d), F32),
        compiler_params=pltpu.CompilerParams(dimension_semantics=("parallel", "parallel")),
        name="modulation",
    )(cond8, w_mod, b_mod.reshape(depth, 1, six_d))


WPREP_ROWS = 128


def _wprep_kernel(w_ref, small_ref, gate_ref):
    for name in _ORDER:
        lo, hi = _ORIG[name]
        small_ref[0, :, _cs(name)] = w_ref[0, :, lo:hi].astype(BF16)
    gate_ref[0] = w_ref[0, :, G_START:].astype(BF16)


def prepare_in_weights(w_in):
    depth, d, cols = w_in.shape
    return pl.pallas_call(
        _wprep_kernel,
        grid=(depth, d // WPREP_ROWS),
        in_specs=[pl.BlockSpec((1, WPREP_ROWS, cols), lambda l, r: (l, r, 0))],
        out_specs=[pl.BlockSpec((1, WPREP_ROWS, SMALL_COLS), lambda l, r: (l, r, 0)),
                   pl.BlockSpec((1, WPREP_ROWS, cols - G_START), lambda l, r: (l, r, 0))],
        out_shape=[jax.ShapeDtypeStruct((depth, d, SMALL_COLS), BF16),
                   jax.ShapeDtypeStruct((depth, d, cols - G_START), BF16)],
        compiler_params=pltpu.CompilerParams(dimension_semantics=("parallel", "parallel")),
        name="prepare_in_weights",
    )(w_in)


def _rot_pairs(x, half, lane_mod_base=0):
    w = x.shape[-1]
    lane = lax.broadcasted_iota(jnp.int32, (1, w), 1) - lane_mod_base
    first = (lane % (2 * half)) < half
    return jnp.where(first, -pltpu.roll(x, w - half, axis=1), pltpu.roll(x, half, axis=1))


def _pre_kernel(x_ref, xp_ref, xn_ref, mod_ref, w_ref, ca_ref, sa_ref, cb_ref, sb_ref, ck_ref, sk_ref,
                qnorm_ref, kvnorm_ref, wuq_ref, cwg_ref, cbg_ref, rep_ref, mu_ref, dw0_ref, dw2_ref, da0_ref,
                da2_ref, dg2_ref, dkk_ref, dka_ref, drk_ref, bd_ref,
                aq_o, ak_o, av_o, bq_o, bckv_o, bkpe_o, cq4_o, ck4_o, claf_o, clab_o, cv_o, cgate_o,
                r_o, v_o, kk_o, lwf_o, lwb_o, kf_o, kb_o, af_o, ab_o, bonus_o, dgate_o):
    t = pl.program_id(0)
    tm = x_ref.shape[0]
    sh1 = mod_ref[0, 0:1, :]
    sc1 = mod_ref[0, 1:2, :]

    def modulate(xv):
        return (xv * (1.0 + sc1) + sh1).astype(BF16)

    h_all = jnp.concatenate([modulate(x_ref[...]), modulate(xp_ref[...]), modulate(xn_ref[...])], axis=0)
    z_all = jnp.dot(h_all, w_ref[0], preferred_element_type=F32)
    z = z_all[:tm]

    aq = z[:, _cs("aq")]
    ak = z[:, _cs("ak")]
    aq_o[...] = aq * ca_ref[...] + _rot_pairs(aq, A_HD // 4) * sa_ref[...]
    ak_o[...] = ak * ca_ref[:, :ak.shape[1]] + _rot_pairs(ak, A_HD // 4) * sa_ref[:, :ak.shape[1]]
    av_o[...] = z[:, _cs("av")]

    bcq = z[:, _cs("bcq")]
    qn = bcq * lax.rsqrt(jnp.mean(bcq * bcq, axis=-1, keepdims=True) + RMS_EPS) * qnorm_ref[...]
    bq = _dot1(qn, wuq_ref[...])
    bq_o[...] = bq * cb_ref[...] + _rot_pairs(bq, B_ROPE // 4, lane_mod_base=B_NOPE) * sb_ref[...]
    bckv = z[:, _cs("bckv")]
    bckv_o[...] = bckv * lax.rsqrt(jnp.mean(bckv * bckv, axis=-1, keepdims=True) + RMS_EPS) * kvnorm_ref[...]
    kpe_lo = COL["bkpe"][0] // 128 * 128
    kblk = z[:, kpe_lo:kpe_lo + 128]
    kblk = kblk * ck_ref[...] + _rot_pairs(kblk, B_ROPE // 4) * sk_ref[...]
    bkpe_o[...] = kblk[:, COL["bkpe"][0] - kpe_lo:COL["bkpe"][1] - kpe_lo]

    rep = rep_ref[...]
    cq4_o[...] = _dot1(z[:, _cs("cq")] * (C_DK ** -0.5), rep)
    ck4_o[...] = _dot1(z[:, _cs("ck")], rep)
    cv_o[...] = z[:, _cs("cv")]
    cog = z[:, _cs("cog")]
    cgate_o[...] = cog * _sigmoid(cog)
    for direction, (name, out) in enumerate((("caf", claf_o), ("cab", clab_o))):
        pre = _dot3(z[:, _cs(name)], cwg_ref[direction]) + cbg_ref[direction]
        la_hi, la_lo = _split2(-_softplus(-pre) * (1.0 / C_GATE_TEMP))
        out[...] = _bdot(la_hi, rep, _D2) + _bdot(la_lo, rep, _D2)

    zd_cols = _cs("zd")
    zd = z[:, zd_cols]
    j = (t - CTX_TILES) % LAT_TILES_PER_SEQ
    latent = t >= CTX_TILES
    has_prev = jnp.logical_and(latent, j != 0)
    has_next = jnp.logical_and(latent, j != LAT_TILES_PER_SEQ - 1)
    prev_row = jnp.where(has_prev, z_all[tm + 7:tm + 8, zd_cols], 0.0)
    next_row = jnp.where(has_next, z_all[tm + 8:tm + 9, zd_cols], 0.0)
    row = lax.broadcasted_iota(jnp.int32, (tm, 1), 0)
    up = jnp.where(row == 0, prev_row, pltpu.roll(zd, 1, axis=0))
    dn = jnp.where(row == tm - 1, next_row, pltpu.roll(zd, tm - 1, axis=0))
    zd = zd + (0.5 * (up + dn) - zd) * mu_ref[...]

    hn = D_HEADS * D_N
    d_r, d_k, d_v = zd[:, :hn], zd[:, hn:2 * hn], zd[:, 2 * hn:3 * hn]
    o = 3 * hn
    d_w = (zd[:, o:o + D_DECAY_RANK], zd[:, o + D_DECAY_RANK:o + 2 * D_DECAY_RANK])
    o += 2 * D_DECAY_RANK
    d_a = (zd[:, o:o + D_AAA_RANK], zd[:, o + D_AAA_RANK:o + 2 * D_AAA_RANK])
    o += 2 * D_AAA_RANK
    d_g = zd[:, o:o + D_GATE_RANK]
    bd = bd_ref[...]
    kk = d_k * dkk_ref[...]
    kk = kk / jnp.maximum(jnp.sqrt(_dot_exact_rhs(kk * kk, bd)), 1e-12)
    r_o[...] = d_r
    v_o[...] = d_v
    kk_o[...] = kk
    k_sum = None
    for direction, (lw_o, k_o, a_o) in enumerate(((lwf_o, kf_o, af_o), (lwb_o, kb_o, ab_o))):
        w_log = -_softplus(-(dw0_ref[direction] + _dot3(jnp.tanh(d_w[direction]), dw2_ref[direction]))) - 0.5
        lw_o[...] = -jnp.exp(w_log)
        a = _sigmoid(da0_ref[direction] + _dot1(d_a[direction], da2_ref[direction]))
        k_dir = d_k * (1.0 + (a - 1.0) * dka_ref[...])
        k_o[...] = k_dir
        a_o[...] = a
        k_sum = k_dir if k_sum is None else k_sum + k_dir
    bonus_o[...] = d_v * _dot_exact_rhs(d_r * drk_ref[...] * k_sum, bd)
    dgate_o[...] = _dot1(_sigmoid(d_g), dg2_ref[...])


def _rope_tables():
    pos = np.arange(LAT_LEN)
    rowp, colp = (pos // GRID_W).astype(np.float32), (pos % GRID_W).astype(np.float32)

    f32 = np.float32

    def head_tables(rot_dim):
        quarter = rot_dim // 4
        inv = (f32(ROPE_BASE) ** (-np.arange(quarter, dtype=f32) / f32(quarter))).astype(f32)
        ar = (rowp[:, None] * inv).astype(f32)
        ac = (colp[:, None] * inv).astype(f32)
        cos = np.concatenate([np.cos(ar), np.cos(ar), np.cos(ac), np.cos(ac)], axis=-1).astype(f32)
        sin = np.concatenate([np.sin(ar), np.sin(ar), np.sin(ac), np.sin(ac)], axis=-1).astype(f32)
        return cos, sin

    def with_identity(c, s):
        w = c.shape[1]
        return (jnp.asarray(np.concatenate([np.ones((ROW_TILE, w), f32), c], axis=0)),
                jnp.asarray(np.concatenate([np.zeros((ROW_TILE, w), f32), s], axis=0)))

    ca, sa = head_tables(A_HD)
    ca, sa = with_identity(np.tile(ca, (1, A_HEADS)), np.tile(sa, (1, A_HEADS)))
    cbh, sbh = head_tables(B_ROPE)
    ones, zeros = np.ones((LAT_LEN, B_NOPE), f32), np.zeros((LAT_LEN, B_NOPE), f32)
    qpad = MLA_HEAD_LANES - B_NOPE - B_ROPE
    cb, sb = with_identity(
        np.tile(np.concatenate([ones, cbh, np.ones((LAT_LEN, qpad), f32)], axis=1), (1, B_HEADS)),
        np.tile(np.concatenate([zeros, sbh, np.zeros((LAT_LEN, qpad), f32)], axis=1), (1, B_HEADS)))
    pad = 128 - B_ROPE
    ck, sk = with_identity(np.concatenate([np.ones((LAT_LEN, pad), f32), cbh], axis=1),
                           np.concatenate([np.zeros((LAT_LEN, pad), f32), sbh], axis=1))
    return ca, sa, cb, sb, ck, sk


def _lane_repeat_matrix():
    m = np.zeros((C_HEADS * C_DK, C_HEADS * 128), np.float32)
    for h in range(C_HEADS):
        for g in range(128 // C_DK):
            for d in range(C_DK):
                m[h * C_DK + d, h * 128 + g * C_DK + d] = 1.0
    return jnp.asarray(m, BF16)


def _head_block_diag():
    m = np.kron(np.eye(D_HEADS, dtype=np.float32), np.ones((D_N, D_N), np.float32))
    return jnp.asarray(m, BF16)


PRE_OUT_WIDTHS = (256, 128, 128, B_HEADS * MLA_HEAD_LANES, 128, 32, 512, 512, 512, 512, 256, 256) + (256,) * 11


def _layer_block(arr, layer):
    nd = arr.ndim
    return pl.BlockSpec((1,) + arr.shape[1:], lambda *_: (layer,) + (0,) * (nd - 1))


def _layer_item(arr, index):
    nd = arr.ndim
    return pl.BlockSpec((None,) + arr.shape[1:], lambda *_: (index,) + (0,) * (nd - 1))


def _mod_spec(layer):
    return pl.BlockSpec((1, 6, D_MODEL), lambda t: (layer * (1 + N_LAT_SEQ) + _mod_row(t), 0, 0))


def stacked_params(p):
    depth = p['d_mu'].shape[0]
    hn = D_HEADS * D_N
    w_uq = p['b_w_uq'].reshape(depth, B_Q_LORA, B_HEADS, B_NOPE + B_ROPE)
    w_uq = jnp.pad(w_uq, ((0, 0), (0, 0), (0, 0), (0, MLA_HEAD_LANES - B_NOPE - B_ROPE)))
    row = lambda a: a.reshape(depth, 1, -1)
    return dict(
        b_q_norm=row(p['b_q_norm']), b_kv_norm=row(p['b_kv_norm']), w_uq=w_uq.reshape(depth, B_Q_LORA, -1),
        c_w_gate=p['c_w_gate'], c_b_gate=p['c_b_gate'].reshape(depth, 2, 1, -1), d_mu=row(p['d_mu']),
        d_w0=p['d_w0'].reshape(depth, 2, 1, hn), d_w2=p['d_w2'], d_a0=p['d_a0'].reshape(depth, 2, 1, hn),
        d_a2=p['d_a2'], d_g2=p['d_g2'], d_k_k=row(p['d_k_k']), d_k_a=row(p['d_k_a']), d_r_k=row(p['d_r_k']),
        c_norm=row(jnp.tile(p['c_norm'], (1, C_HEADS))), d_ln_g=row(p['d_ln_g']), d_ln_b=row(p['d_ln_b']),
        ln_g=p['ln_g'].reshape(depth * 2, 1, -1), ln_b=p['ln_b'].reshape(depth * 2, 1, -1),
        w_router=p['w_router'], b_router=row(p['b_router']), a_sink=p['a_sink'].reshape(-1), b_w_ukv=p['b_w_ukv'])


def mixer_prelude(x, mod, w_small, layer, tables, sp):
    tm = ROW_TILE
    tab_idx = lambda t: (jnp.where(t < CTX_TILES, 0, 1 + (t - CTX_TILES) % LAT_TILES_PER_SEQ), 0)
    small = [sp[k] for k in ('b_q_norm', 'b_kv_norm', 'w_uq', 'c_w_gate', 'c_b_gate')] + [_lane_repeat_matrix()]
    small += [sp[k] for k in ('d_mu', 'd_w0', 'd_w2', 'd_a0', 'd_a2', 'd_g2', 'd_k_k', 'd_k_a', 'd_r_k')]
    small += [_head_block_diag()]
    const = lambda a: _full(a.shape) if a.dtype == BF16 else _layer_item(a, layer)
    in_specs = ([pl.BlockSpec((tm, D_MODEL), lambda t: (t, 0)),
                 pl.BlockSpec((8, D_MODEL), lambda t: (jnp.maximum(t * (tm // 8) - 1, 0), 0)),
                 pl.BlockSpec((8, D_MODEL), lambda t: (jnp.minimum((t + 1) * (tm // 8), N_TOK // 8 - 1), 0)),
                 _mod_spec(layer),
                 _layer_block(w_small, layer)]
                + [pl.BlockSpec((tm, tab.shape[1]), tab_idx) for tab in tables]
                + [const(a) for a in small])
    return pl.pallas_call(
        _pre_kernel,
        grid=(N_TILES,),
        in_specs=in_specs,
        out_specs=[pl.BlockSpec((tm, w), lambda t: (t, 0)) for w in PRE_OUT_WIDTHS],
        out_shape=[jax.ShapeDtypeStruct((N_TOK, w), F32) for w in PRE_OUT_WIDTHS],
        compiler_params=pltpu.CompilerParams(dimension_semantics=("parallel",), vmem_limit_bytes=VMEM_LIMIT),
        name="mixer_prelude",
    )(x, x, x, mod, w_small, *tables, *small)


ATT_Q_BLOCK = 128
MLA_Q_BLOCK = 256
ATT_WINDOW = 128
ATT_NEG_INF = -1e30
CACHE_LEN = 512


def _softmax_pv(s, v, sink):
    dv = v.shape[1] // 2
    m = jnp.max(s, axis=-1, keepdims=True)
    if sink is not None:
        m = jnp.maximum(m, sink)
    e = jnp.exp((s - m).astype(BF16))
    o = jnp.dot(e, v, preferred_element_type=F32)
    den = o[:, dv:dv + 1]
    if sink is not None:
        den = den + jnp.exp(sink - m)
    return o[:, :dv] / den


def _with_ones(v):
    return jnp.concatenate([v.astype(BF16), jnp.ones(v.shape, BF16)], axis=1)


def _gqa_kernel(sink_ref, q_ref, k_ref, v_ref, *rest, hd, group, scale, windowed, sink_base):
    if windowed:
        kp_ref, kn_ref, vp_ref, vn_ref, kc_ref, vc_ref, _, o_ref = rest
    else:
        (o_ref,) = rest
    i = pl.program_id(1)
    tq = q_ref.shape[0]
    n_kv = k_ref.shape[1] // hd
    if windowed:
        qpos = i * tq + lax.broadcasted_iota(jnp.int32, (tq, 3 * tq), 0)
        kpos = (i - 1) * tq + lax.broadcasted_iota(jnp.int32, (tq, 3 * tq), 1)
        n_tok = pl.num_programs(1) * tq
        mask = (jnp.abs(qpos - kpos) <= ATT_WINDOW) & (kpos >= 0) & (kpos < n_tok)
        mask = jnp.concatenate([mask] * group, axis=0)
    for kvh in range(n_kv):
        ks = slice(kvh * hd, (kvh + 1) * hd)
        qs = [q_ref[:, (kvh * group + g) * hd:(kvh * group + g + 1) * hd] for g in range(group)]
        q = (jnp.concatenate(qs, axis=0) * scale).astype(BF16)
        sink = jnp.concatenate(
            [jnp.full((tq, 1), sink_ref[sink_base + kvh * group + g], F32) for g in range(group)], axis=0)
        if windowed:
            k_win = jnp.concatenate([kp_ref[:, ks], k_ref[:, ks], kn_ref[:, ks]], axis=0)
            v_win = jnp.concatenate([vp_ref[:, ks], v_ref[:, ks], vn_ref[:, ks]], axis=0)
            s_win = _bdot(q, k_win.astype(BF16), _D2T)
            s_win = jnp.where(mask, s_win, ATT_NEG_INF)
            s_ctx = _bdot(q, kc_ref[0, 0, kvh].astype(BF16), _D2T)
            s = jnp.concatenate([s_win, s_ctx], axis=1)
            v = jnp.concatenate([v_win, vc_ref[0, 0, kvh]], axis=0)
        else:
            s = _bdot(q, k_ref[:, ks].astype(BF16), _D2T)
            v = v_ref[:, ks]
        o = _softmax_pv(s, _with_ones(v), sink)
        for g in range(group):
            h = kvh * group + g
            o_ref[:, h * hd:(h + 1) * hd] = o[g * tq:(g + 1) * tq]


def gqa_attention(q, k, v, sink, cache_k, cache_v, layer):
    qw, kw = q.shape[1], k.shape[1]
    group = qw // kw
    scale = A_HD ** -0.5
    params = pltpu.CompilerParams(dimension_semantics=("parallel", "parallel"))
    out_shape = jax.ShapeDtypeStruct((N_TOK, qw), F32)
    ctx_spec = lambda w: pl.BlockSpec((CTX_LEN, w), lambda s, i, sk: (s, 0))
    o = pl.pallas_call(
        functools.partial(_gqa_kernel, hd=A_HD, group=group, scale=scale, windowed=False, sink_base=layer * A_HEADS),
        grid_spec=pltpu.PrefetchScalarGridSpec(
            num_scalar_prefetch=1, grid=(N_CTX_SEQ, 1), in_specs=[ctx_spec(qw), ctx_spec(kw), ctx_spec(kw)],
            out_specs=ctx_spec(qw)),
        out_shape=out_shape, compiler_params=params, name="gqa_full",
    )(sink, q, k, v)
    tq = ATT_Q_BLOCK
    nb = LAT_LEN // tq
    base = N_CTX // tq
    blk = lambda w, f: pl.BlockSpec((tq, w), lambda b, i, sk: (base + nb * b + f(i), 0))
    same = lambda i: i
    prev = lambda i: jnp.maximum(i - 1, 0)
    nxt = lambda i: jnp.minimum(i + 1, nb - 1)
    cspec = pl.BlockSpec((1, 1) + cache_k.shape[2:], lambda b, i, sk: (b, layer, 0, 0, 0))
    return pl.pallas_call(
        functools.partial(_gqa_kernel, hd=A_HD, group=group, scale=scale, windowed=True, sink_base=layer * A_HEADS),
        grid_spec=pltpu.PrefetchScalarGridSpec(
            num_scalar_prefetch=1, grid=(N_LAT_SEQ, nb),
            in_specs=[blk(qw, same), blk(kw, same), blk(kw, same), blk(kw, prev), blk(kw, nxt), blk(kw, prev),
                      blk(kw, nxt), cspec, cspec, pl.BlockSpec(memory_space=pl.ANY)],
            out_specs=blk(qw, same)),
        out_shape=out_shape, input_output_aliases={10: 0}, compiler_params=params, name="gqa_windowed",
    )(sink, q, k, v, k, k, v, v, cache_k, cache_v, o)


def _mla_kernel(q_ref, ckv_ref, kpe_ref, wukv_ref, *rest, n_heads, nope, rope, vd, scale, cached):
    if cached:
        cckv_ref, ckpe_ref, _, o_ref, k_scr, vext_scr = rest
    else:
        o_ref, k_scr, vext_scr = rest
    i = pl.program_id(1)
    n_cache = k_scr.shape[0] - ckv_ref.shape[0]
    hw = nope + vd
    hl = MLA_HEAD_LANES

    @pl.when(i == 0)
    def _():
        w = wukv_ref[...].astype(BF16)

        def expand(rows, kpe_rows, lo, hi):
            kv = jnp.dot(rows.astype(BF16), w, preferred_element_type=F32).astype(BF16)
            n = hi - lo
            kpe = kpe_rows.astype(BF16)
            for h in range(n_heads):
                k_scr[lo:hi, hl * h:hl * (h + 1)] = jnp.concatenate(
                    [kv[:, h * hw:h * hw + nope], kpe, jnp.zeros((n, hl - nope - rope), BF16)], axis=1)
                vext_scr[lo:hi, 2 * vd * h:2 * vd * (h + 1)] = jnp.concatenate(
                    [kv[:, h * hw + nope:(h + 1) * hw], jnp.ones((n, vd), BF16)], axis=1)

        if cached:
            expand(cckv_ref[0, 0], ckpe_ref[0, 0], 0, n_cache)
        expand(ckv_ref[...], kpe_ref[...], n_cache, k_scr.shape[0])

    for h in range(n_heads):
        qh = (q_ref[:, hl * h:hl * (h + 1)] * scale).astype(BF16)
        s = _bdot(qh, k_scr[:, hl * h:hl * (h + 1)], _D2T)
        o_ref[:, h * vd:(h + 1) * vd] = _softmax_pv(s, vext_scr[:, 2 * vd * h:2 * vd * (h + 1)], None)


def mla_attention(q, ckv, kpe, w_ukv, cache_ckv, cache_kpe, layer):
    qw = q.shape[1]
    tq = MLA_Q_BLOCK
    kw = dict(n_heads=B_HEADS, nope=B_NOPE, rope=B_ROPE, vd=B_VD, scale=(B_NOPE + B_ROPE) ** -0.5)
    params = pltpu.CompilerParams(dimension_semantics=("parallel", "arbitrary"))
    out_shape = jax.ShapeDtypeStruct((N_TOK, B_HEADS * B_VD), F32)
    scratch = lambda rows: [pltpu.VMEM((rows, B_HEADS * MLA_HEAD_LANES), BF16),
                            pltpu.VMEM((rows, 2 * B_HEADS * B_VD), BF16)]
    nbc = CTX_LEN // tq
    o = pl.pallas_call(
        functools.partial(_mla_kernel, cached=False, **kw),
        grid=(N_CTX_SEQ, nbc),
        in_specs=[pl.BlockSpec((tq, qw), lambda s, i: (s * nbc + i, 0)),
                  pl.BlockSpec((CTX_LEN, B_KV_LORA), lambda s, i: (s, 0)),
                  pl.BlockSpec((CTX_LEN, B_ROPE), lambda s, i: (s, 0)),
                  _layer_item(w_ukv, layer)],
        out_specs=pl.BlockSpec((tq, B_HEADS * B_VD), lambda s, i: (s * nbc + i, 0)),
        out_shape=out_shape,
        scratch_shapes=scratch(CTX_LEN),
        compiler_params=params, name="mla_context",
    )(q, ckv, kpe, w_ukv)
    nb = LAT_LEN // tq
    base = N_CTX // tq
    lat0 = N_CTX // LAT_LEN
    s_len = CACHE_LEN + LAT_LEN
    return pl.pallas_call(
        functools.partial(_mla_kernel, cached=True, **kw),
        grid=(N_LAT_SEQ, nb),
        in_specs=[pl.BlockSpec((tq, qw), lambda b, i: (base + nb * b + i, 0)),
                  pl.BlockSpec((LAT_LEN, B_KV_LORA), lambda b, i: (lat0 + b, 0)),
                  pl.BlockSpec((LAT_LEN, B_ROPE), lambda b, i: (lat0 + b, 0)),
                  _layer_item(w_ukv, layer),
                  pl.BlockSpec((1, 1, CACHE_LEN, B_KV_LORA), lambda b, i: (b, layer, 0, 0)),
                  pl.BlockSpec((1, 1, CACHE_LEN, B_ROPE), lambda b, i: (b, layer, 0, 0)),
                  pl.BlockSpec(memory_space=pl.ANY)],
        out_specs=pl.BlockSpec((tq, B_HEADS * B_VD), lambda b, i: (base + nb * b + i, 0)),
        out_shape=out_shape, input_output_aliases={6: 0},
        scratch_shapes=scratch(s_len),
        compiler_params=params, name="mla_latent",
    )(q, ckv, kpe, w_ukv, cache_ckv, cache_kpe, o)


CHUNK = 64
GLA_SUB = 16
PAIR = 2
N_CHAIN = PAIR * 2 * 4


def _is_back(shape):
    return (lax.broadcasted_iota(jnp.int32, shape, 0) // 4) % 2 == 1


def _chains(ref_f, ref_b, width):
    return jnp.stack([ref[0, s, 0, :, h * width:(h + 1) * width]
                      for s in range(PAIR) for ref in (ref_f, ref_b) for h in range(4)], axis=0)


def _unchain(y, o_f, o_b):
    for s in range(PAIR):
        o_f[0, s, 0] = jnp.concatenate([y[s * 8 + h] for h in range(4)], axis=-1)
        o_b[0, s, 0] = jnp.concatenate([y[s * 8 + 4 + h] for h in range(4)], axis=-1)


def _dir_masks(L):
    shape = (N_CHAIN, L, L)
    back = _is_back(shape)
    row = lax.broadcasted_iota(jnp.int32, shape, 1)
    col = lax.broadcasted_iota(jnp.int32, shape, 2)
    ahead = jnp.where(back, col - row, row - col)
    return ahead >= 0, ahead > 0, row == col


def _chunk_end(ci):
    L = ci.shape[1]
    return jnp.where(_is_back((N_CHAIN, 1, 1)), ci[:, 0:1], ci[:, L - 1:L])


def _split_refs(refs, n_in, has_s0, has_sfin):
    ins = refs[:n_in]
    pos = n_in
    s0_ref = None
    if has_s0:
        s0_ref = refs[pos]
        pos += 3
    of_ref, ob_ref = refs[pos], refs[pos + 1]
    pos += 2
    sfin_ref = refs[pos] if has_sfin else None
    return ins, s0_ref, of_ref, ob_ref, sfin_ref, refs[-1]


def _init_state(s_scr, s0_ref):
    @pl.when(pl.program_id(1) == 0)
    def _():
        if s0_ref is None:
            s_scr[...] = jnp.zeros_like(s_scr)
        else:
            s_scr[...] = s0_ref[0]


def _emit_state(sfin_ref, s_new):
    if sfin_ref is None:
        return

    @pl.when(pl.program_id(1) == pl.num_programs(1) - 1)
    def _():
        sfin_ref[0] = s_new


def _rwkv_kernel(*refs, dot, has_s0, has_sfin):
    (rf, rb, vf, vb, kkf, kkb, lwf, lwb, kf, kb, af, ab), s0_ref, yf_ref, yb_ref, sfin_ref, s_scr = _split_refs(
        refs, 12, has_s0, has_sfin)
    _init_state(s_scr, s0_ref)
    n = D_N
    r = _chains(rf, rb, n)
    v = _chains(vf, vb, n)
    kk = _chains(kkf, kkb, n)
    lw = _chains(lwf, lwb, n)
    k = _chains(kf, kb, n)
    a = _chains(af, ab, n)
    L = r.shape[1]
    S = s_scr[...]
    incl, strict, diag = _dir_masks(L)
    ci = _dot_exact_lhs(jnp.where(incl, 1.0, 0.0), lw, _NN)
    ce = ci - lw
    cl = _chunk_end(ci)
    e_neg = jnp.exp(-ci)
    b = a * kk
    alpha = kk * jnp.exp(ce)
    rho = r * jnp.exp(ci)
    beta = b * e_neg
    kappa = k * e_neg
    e_end = jnp.exp(cl - ci)
    ar = jnp.concatenate([alpha, rho], axis=1)
    bk = jnp.concatenate([beta, kappa], axis=1)
    w = dot(ar, bk, _NT)
    nmat = jnp.where(strict, w[:, :L, :L], 0.0)
    mmat = jnp.where(strict, w[:, :L, L:], 0.0)
    p1 = jnp.where(incl, w[:, L:, :L], 0.0)
    p2 = jnp.where(incl, w[:, L:, L:], 0.0)
    x = jnp.where(diag, 1.0, 0.0) - nmat
    p = dot(nmat, nmat, _NN)
    span = 2
    while True:
        x = x + dot(x, p, _NN)
        span *= 2
        if span >= L:
            break
        p = dot(p, p, _NN)
    us = dot(ar, S, _NT)
    rhs = us[:, :L] + dot(mmat, v, _NN)
    d = -dot(x, rhs, _NN)
    dv = jnp.concatenate([d, v], axis=1)
    pp = jnp.concatenate([p1, p2], axis=2)
    _unchain(us[:, L:] + dot(pp, dv, _NN), yf_ref, yb_ref)
    bk_end = jnp.concatenate([b * e_end, k * e_end], axis=1)
    s_new = S * jnp.exp(cl) + dot(dv, bk_end, _TN)
    s_scr[...] = s_new
    _emit_state(sfin_ref, s_new)


def _gla_kernel(*refs, dot, has_s0, has_sfin):
    (qf, qb, kf, kb, vf, vb, laf, lab), s0_ref, of_ref, ob_ref, sfin_ref, s_scr = _split_refs(
        refs, 8, has_s0, has_sfin)
    _init_state(s_scr, s0_ref)
    q4 = _chains(qf, qb, 128)
    k4 = _chains(kf, kb, 128)
    la4 = _chains(laf, lab, 128)
    v = _chains(vf, vb, C_DV)
    g, L, lanes = q4.shape
    dk = C_DK
    n_sub = L // GLA_SUB
    st = s_scr[...]
    incl, _, _ = _dir_masks(L)
    c = _dot_exact_lhs(jnp.where(incl, 1.0, 0.0), la4, _NN)
    shape = (g, L, lanes)
    back = _is_back(shape)
    lane_blk = lax.broadcasted_iota(jnp.int32, shape, 2) // dk
    row_blk = lax.broadcasted_iota(jnp.int32, shape, 1) // GLA_SUB
    cref_f = jnp.zeros(shape, F32)
    cref_b = jnp.zeros(shape, F32)
    for j in range(1, n_sub):
        cref_f = jnp.where(lane_blk == j, c[:, j * GLA_SUB - 1:j * GLA_SUB], cref_f)
        cref_b = jnp.where(lane_blk == j - 1, c[:, j * GLA_SUB:j * GLA_SUB + 1], cref_b)
    cref = jnp.where(back, cref_b, cref_f)
    q_on = row_blk == lane_blk
    k_on = jnp.where(back, row_blk - lane_blk, lane_blk - row_blk) >= 0
    qh = jnp.where(q_on, q4 * jnp.exp(jnp.where(q_on, c - cref, 0.0)), 0.0)
    kh = jnp.where(k_on, k4 * jnp.exp(jnp.where(k_on, cref - c, 0.0)), 0.0)
    att = jnp.where(incl, dot(qh, kh, _NT), 0.0)
    cl = _chunk_end(c)
    qe = (q4 * jnp.exp(c))[:, :, :dk]
    ke = (k4 * jnp.exp(cl - c))[:, :, :dk]
    _unchain(dot(qe, st, _NT) + dot(att, v, _NN), of_ref, ob_ref)
    s_new = st * jnp.exp(cl[:, :, :dk]) + dot(v, ke, _TN)
    s_scr[...] = s_new
    _emit_state(sfin_ref, s_new)


def _recurrence_calls(kernel_fn, name, pairs, singles_f, singles_b, s0_lat, state_dims, out_width):
    def run(view, grid, group, s0, prev_out):
        nc = view[2]
        fwd_map = lambda p, c: (group(p), 0, c, 0, 0)
        bwd_map = lambda p, c: (group(p), 0, nc - 1 - c, 0, 0)
        blk = lambda w: (1, PAIR, 1, CHUNK, w)
        args, in_specs = [], []
        for af, ab in [(a, a) for a in pairs] + list(zip(singles_f, singles_b)):
            w = af.shape[-1]
            args += [af.reshape(view + (w,)), ab.reshape(view + (w,))]
            in_specs += [pl.BlockSpec(blk(w), fwd_map), pl.BlockSpec(blk(w), bwd_map)]
        out_specs = [pl.BlockSpec(blk(out_width), fwd_map), pl.BlockSpec(blk(out_width), bwd_map)]
        out_shape = [jax.ShapeDtypeStruct(view + (out_width,), F32)] * 2
        aliases = {}
        if s0 is not None:
            args += [s0] + [o.reshape(view + (out_width,)) for o in prev_out]
            in_specs += [_full(s0.shape), pl.BlockSpec(memory_space=pl.ANY), pl.BlockSpec(memory_space=pl.ANY)]
            aliases = {len(args) - 2: 0, len(args) - 1: 1}
        else:
            out_specs.append(pl.BlockSpec((1, N_CHAIN) + state_dims, lambda p, c: (p, 0, 0, 0)))
            out_shape.append(jax.ShapeDtypeStruct((grid[0], N_CHAIN) + state_dims, F32))
        return pl.pallas_call(
            functools.partial(kernel_fn, has_s0=s0 is not None, has_sfin=s0 is None),
            grid=grid, in_specs=in_specs, out_specs=out_specs, out_shape=out_shape,
            input_output_aliases=aliases, scratch_shapes=[pltpu.VMEM((N_CHAIN,) + state_dims, F32)],
            compiler_params=pltpu.CompilerParams(dimension_semantics=("parallel", "arbitrary")),
            name=name + ("_latent" if s0 is not None else "_context"),
        )(*args)

    ctx_nc = CTX_LEN // CHUNK
    ctx_view = (N_TOK // (PAIR * CTX_LEN), PAIR, ctx_nc, CHUNK)
    o_f, o_b, s_fin = run(ctx_view, (N_CTX_SEQ // PAIR, ctx_nc), lambda p: p, None, None)
    lat_nc = LAT_LEN // CHUNK
    lat_view = (N_TOK // (PAIR * LAT_LEN), PAIR, lat_nc, CHUNK)
    o_f, o_b = run(lat_view, (1, lat_nc), lambda p: N_CTX // (PAIR * LAT_LEN), s0_lat, (o_f, o_b))
    return o_f.reshape(N_TOK, out_width), o_b.reshape(N_TOK, out_width), s_fin


def _layer_norm(x, g, b):
    mu = jnp.mean(x, axis=-1, keepdims=True)
    xc = x - mu
    var = jnp.mean(xc * xc, axis=-1, keepdims=True)
    return xc * lax.rsqrt(var + LN_EPS) * g + b


def _merge_kernel(x_ref, mod_ref, oa_ref, ob_ref, cof_ref, cob_ref, cgate_ref, yf_ref, yb_ref, bonus_ref, dgate_ref,
                  wg_ref, wbr_ref, wout_ref, cnorm_ref, dlng_ref, dlnb_ref, lng_ref, lnb_ref, wr_ref, br_ref, bd_ref,
                  x1_o, h2_o, topi_o, topw_o):
    x = x_ref[...]
    m = mod_ref[0]
    sh1, sc1, g1, sh2, sc2 = m[0:1], m[1:2], m[2:3], m[3:4], m[4:5]
    bd = bd_ref[...]
    inv_n = 1.0 / D_N
    co = cof_ref[...] + cob_ref[...]
    o_c = co * lax.rsqrt(_dot_exact_rhs(co * co, bd) * inv_n + RMS_EPS) * cnorm_ref[...] * cgate_ref[...]
    y = yf_ref[...] + yb_ref[...]
    yc = y - _dot_exact_rhs(y, bd) * inv_n
    var = _dot_exact_rhs(yc * yc, bd) * inv_n
    o_d = (yc * lax.rsqrt(var + D_GN_EPS) * dlng_ref[...] + dlnb_ref[...] + bonus_ref[...]) * dgate_ref[...]
    branches = (oa_ref[...], ob_ref[...], o_c, o_d)
    h = (x * (1.0 + sc1) + sh1).astype(BF16)
    merged = None
    for n in range(N_BRANCH):
        gate = _sigmoid(jnp.dot(h, wg_ref[0, :, n * D_MODEL:(n + 1) * D_MODEL], preferred_element_type=F32))
        term = gate * jnp.dot(branches[n].astype(BF16), wbr_ref[0, n], preferred_element_type=F32)
        merged = term if merged is None else merged + term
    mix = jnp.dot(merged.astype(BF16), wout_ref[0], preferred_element_type=F32)
    x1 = _layer_norm(ALPHA * x + g1 * mix, lng_ref[...], lnb_ref[...])
    x1_o[...] = x1
    h2 = x1 * (1.0 + sc2) + sh2
    h2_o[...] = h2.astype(BF16)
    logits = _dot6(h2, wr_ref[...]) + br_ref[...]
    tm, n_e = logits.shape
    lane_e = lax.broadcasted_iota(jnp.int32, (tm, n_e), 1)
    lane_o = lax.broadcasted_iota(jnp.int32, (tm, topi_o.shape[1]), 1)
    top_i = jnp.zeros((tm, topi_o.shape[1]), jnp.int32)
    top_v = jnp.zeros((tm, topw_o.shape[1]), F32)
    vals = []
    for kth in range(TOP_K):
        mx = jnp.max(logits, axis=-1, keepdims=True)
        idx = jnp.min(jnp.where(logits == mx, lane_e, n_e), axis=-1, keepdims=True)
        vals.append(mx)
        top_i = jnp.where(lane_o == kth, idx, top_i)
        logits = jnp.where(lane_e == idx, -jnp.inf, logits)
    es = [jnp.exp(vk - vals[0]) for vk in vals]
    den = es[0] + es[1] + es[2] + es[3]
    for kth in range(TOP_K):
        top_v = jnp.where(lane_o == kth, es[kth] / den, top_v)
    topi_o[...] = top_i
    topw_o[...] = top_v


def merge_and_route(x, mod, o_a, o_b, co_f, co_b, cgate, y_f, y_b, bonus, dgate, w_g, w_br, w_out, layer, sp):
    tm = ROW_TILE
    hn = D_HEADS * D_N
    row = lambda w: pl.BlockSpec((tm, w), lambda t: (t, 0))
    small = [sp[k] for k in ('c_norm', 'd_ln_g', 'd_ln_b', 'ln_g', 'ln_b', 'w_router', 'b_router')]
    index = [layer, layer, layer, 2 * layer, 2 * layer, layer, layer]
    bd = _head_block_diag()
    return pl.pallas_call(
        _merge_kernel,
        grid=(N_TILES,),
        in_specs=([row(D_MODEL), _mod_spec(layer)]
                  + [row(hn)] * 9 + [_layer_block(w, layer) for w in (w_g, w_br, w_out)]
                  + [_layer_item(a, i) for a, i in zip(small, index)] + [_full(bd.shape)]),
        out_specs=[row(D_MODEL), row(D_MODEL), row(128), row(128)],
        out_shape=[jax.ShapeDtypeStruct((N_TOK, D_MODEL), F32), jax.ShapeDtypeStruct((MOE_ROWS, D_MODEL), BF16),
                   jax.ShapeDtypeStruct((N_TOK, 128), jnp.int32), jax.ShapeDtypeStruct((N_TOK, 128), F32)],
        compiler_params=pltpu.CompilerParams(dimension_semantics=("parallel",), vmem_limit_bytes=VMEM_LIMIT),
        name="merge_and_route",
    )(x, mod, o_a, o_b, co_f, co_b, cgate, y_f, y_b, bonus, dgate, w_g, w_br, w_out, *small, bd)


def _moe_kernel(te_ref, tv_ref, first_ref, slot_ref, next_ref, x_ref, w1_hbm, b1_ref, w2_hbm, b2_ref, perm_ref, y_ref,
                w1buf, w2buf, sem, w1s, w2s, hs, *, layer):
    t = pl.program_id(0)
    valid = tv_ref[t] != 0
    d_model, two_f = w1s.shape
    n_blk = two_f // MXU_WIDTH
    half = MXU_WIDTH // 2

    def fetch(expert, slot):
        return (pltpu.make_async_copy(w1_hbm.at[layer, expert], w1buf.at[slot], sem.at[0, slot]),
                pltpu.make_async_copy(w2_hbm.at[layer, expert], w2buf.at[slot], sem.at[1, slot]))

    @pl.when(t == 0)
    def _():
        for cp in fetch(te_ref[0], 0):
            cp.start()

    @pl.when(first_ref[t] == 1)
    def _():
        slot = slot_ref[t]
        for cp in fetch(te_ref[t], slot):
            cp.wait()

        @pl.when(next_ref[t] >= 0)
        def _():
            for cp in fetch(next_ref[t], 1 - slot):
                cp.start()

        for blk in range(n_blk):
            sl = slice(blk * MXU_WIDTH, (blk + 1) * MXU_WIDTH)
            wb = w1buf[slot, :, sl].astype(BF16)
            w1s[:, sl] = jnp.dot(wb, perm_ref[...], preferred_element_type=F32).astype(BF16)
        w2s[...] = w2buf[slot].astype(BF16)

    @pl.when(valid)
    def _():
        x = x_ref[...]
        for blk in range(n_blk):
            sl = slice(blk * MXU_WIDTH, (blk + 1) * MXU_WIDTH)
            u = jnp.dot(x, w1s[:, sl], preferred_element_type=F32) + b1_ref[0, 0, :, sl]
            glu = jnp.minimum(u[:, :half], SWIGLU_LIMIT)
            lin = jnp.clip(u[:, half:], -SWIGLU_LIMIT, SWIGLU_LIMIT)
            hs[:, blk * half:(blk + 1) * half] = (glu * _sigmoid(SWIGLU_ALPHA * glu) * (lin + 1.0)).astype(BF16)
        y = jnp.dot(hs[...], w2s[...], preferred_element_type=F32) + b2_ref[0, 0]
        y_ref[...] = y.astype(y_ref.dtype)

    @pl.when(jnp.logical_not(valid))
    def _():
        y_ref[...] = jnp.zeros_like(y_ref)


def _deinterleave_perm():
    half = MXU_WIDTH // 2
    src = np.arange(MXU_WIDTH)
    dst = np.where(src % 2 == 0, src // 2, half + src // 2)
    p = np.zeros((MXU_WIDTH, MXU_WIDTH), np.float32)
    p[src, dst] = 1.0
    return jnp.asarray(p, BF16)


def _moe_dispatch(top_i):
    n, k = top_i.shape
    tm = MOE_ROW_TILE
    p_rows = n * k + N_EXPERTS * tm
    experts = jnp.arange(N_EXPERTS, dtype=jnp.int32)
    onehot = top_i[:, :, None] == experts
    sel = jnp.sum(onehot.astype(jnp.int32), axis=1)
    before = jnp.cumsum(sel, axis=0) - sel
    counts = jnp.sum(sel, axis=0)
    padded = ((counts + tm - 1) // tm) * tm
    ends = jnp.cumsum(padded)
    starts = ends - padded
    pos = jnp.sum(jnp.where(onehot, (before + starts)[:, None, :], 0), axis=-1)
    n_tiles = p_rows // tm
    tile_start = jnp.arange(n_tiles, dtype=jnp.int32) * tm
    tile_valid = (tile_start < ends[-1]).astype(jnp.int32)
    last_tile = ends[-1] // tm - 1
    tile_expert = jnp.sum(ends[None, :] <= jnp.minimum(tile_start, last_tile * tm)[:, None], axis=1).astype(jnp.int32)
    keys = jnp.sort((top_i * n + jnp.arange(n, dtype=jnp.int32)[:, None]).reshape(-1))
    tile_onehot = tile_expert[:, None] == experts[None, :]
    lookup = lambda table: jnp.sum(jnp.where(tile_onehot, table[None, :], 0), axis=1)
    tile_rank0 = tile_start - lookup(starts)
    rank = tile_rank0[:, None] + jnp.arange(tm, dtype=jnp.int32)[None, :]
    sorted_at = jnp.clip(lookup(jnp.cumsum(counts) - counts)[:, None] + rank, 0, n * k - 1)
    tile_keys = keys[sorted_at.reshape(-1)].reshape(n_tiles, tm)
    filler = (tile_start[:, None] + jnp.arange(tm, dtype=jnp.int32)[None, :]) % n
    src_tok = jnp.where(rank < lookup(counts)[:, None], tile_keys % n, filler)
    is_first = jnp.concatenate([jnp.ones((1,), jnp.int32),
                                (tile_expert[1:] != tile_expert[:-1]).astype(jnp.int32)])
    slot = (jnp.cumsum(is_first) - 1) % 2
    later = jnp.logical_and(experts[None, :] > experts[:, None], (counts > 0)[None, :])
    next_of = jnp.min(jnp.where(later, experts[None, :], N_EXPERTS), axis=1)
    next_expert = lookup(jnp.where(next_of < N_EXPERTS, next_of, -1))
    tables = (tile_expert, tile_valid, is_first, slot.astype(jnp.int32), next_expert.astype(jnp.int32))
    return pos, src_tok.reshape(-1), tables, p_rows


def moe_experts(h2, top_i, layer, w1, b1, w2, b2):
    n = top_i.shape[0]
    d = h2.shape[1]
    depth, e, _, two_f = w1.shape
    f = two_f // 2
    tm = MOE_ROW_TILE
    pos, src_tok, tables, p_rows = _moe_dispatch(top_i)
    assert h2.shape[0] == p_rows
    xs = h2.at[lax.optimization_barrier(src_tok)].get(mode="promise_in_bounds")
    b1p = b1.reshape(depth, e, two_f // MXU_WIDTH, MXU_WIDTH // 2, 2).swapaxes(3, 4).reshape(depth, e, 1, two_f)
    expert_vec = lambda w: pl.BlockSpec((1, 1, 1, w), lambda t, te, *_: (layer, te[t], 0, 0))
    grid_spec = pltpu.PrefetchScalarGridSpec(
        num_scalar_prefetch=len(tables),
        grid=(p_rows // tm,),
        in_specs=[
            pl.BlockSpec((tm, d), lambda t, *_: (t, 0)),
            pl.BlockSpec(memory_space=pl.ANY),
            expert_vec(two_f),
            pl.BlockSpec(memory_space=pl.ANY),
            expert_vec(d),
            pl.BlockSpec((MXU_WIDTH, MXU_WIDTH), lambda t, *_: (0, 0)),
        ],
        out_specs=pl.BlockSpec((tm, d), lambda t, *_: (t, 0)),
        scratch_shapes=[pltpu.VMEM((2, d, two_f), F32), pltpu.VMEM((2, f, d), F32),
                        pltpu.SemaphoreType.DMA((2, 2)),
                        pltpu.VMEM((d, two_f), BF16), pltpu.VMEM((f, d), BF16), pltpu.VMEM((tm, f), BF16)],
    )
    ys = pl.pallas_call(
        functools.partial(_moe_kernel, layer=layer),
        grid_spec=grid_spec,
        out_shape=jax.ShapeDtypeStruct((p_rows, d), BF16),
        compiler_params=pltpu.CompilerParams(dimension_semantics=("arbitrary",),
                                             vmem_limit_bytes=48 * 1024 * 1024),
        name="moe_experts",
    )(*tables, xs, w1, b1p, w2, b2.reshape(depth, e, 1, d), _deinterleave_perm())
    return ys.at[lax.optimization_barrier(pos.T.reshape(-1))].get(mode="promise_in_bounds").reshape(TOP_K, n, d)


def _final_kernel(x1_ref, mod_ref, ys_ref, topw_ref, lng_ref, lnb_ref, o_ref):
    g2 = mod_ref[0, 5:6]
    moe = None
    for kth in range(TOP_K):
        term = ys_ref[kth].astype(F32) * topw_ref[:, kth:kth + 1]
        moe = term if moe is None else moe + term
    o_ref[...] = _layer_norm(ALPHA * x1_ref[...] + g2 * moe, lng_ref[...], lnb_ref[...])


def combine_and_norm(x1, mod, ys, top_w, layer, sp):
    tm = ROW_TILE
    ln_g, ln_b = sp['ln_g'], sp['ln_b']
    return pl.pallas_call(
        _final_kernel,
        grid=(N_TILES,),
        in_specs=[pl.BlockSpec((tm, D_MODEL), lambda t: (t, 0)),
                  _mod_spec(layer),
                  pl.BlockSpec((TOP_K, tm, D_MODEL), lambda t: (0, t, 0)),
                  pl.BlockSpec((tm, 128), lambda t: (t, 0)),
                  _layer_item(ln_g, 2 * layer + 1), _layer_item(ln_b, 2 * layer + 1)],
        out_specs=pl.BlockSpec((tm, D_MODEL), lambda t: (t, 0)),
        out_shape=jax.ShapeDtypeStruct((N_TOK, D_MODEL), F32),
        compiler_params=pltpu.CompilerParams(dimension_semantics=("parallel",)),
        name="combine_and_norm",
    )(x1, mod, ys, top_w, ln_g, ln_b)


def kernel(x_prompt, x_sample, cache_a_k, cache_a_v, cache_b_ckv, cache_b_kpe, state_c, state_d, c,
           c_ctx, w_mod, b_mod, w_in, a_sink, b_q_norm, b_w_uq, b_kv_norm, b_w_ukv, c_w_gate, c_b_gate,
           c_norm, d_mu, d_w0, d_w2, d_a0, d_a2, d_g2, d_k_k, d_k_a, d_r_k, d_ln_g, d_ln_b, w_br, w_out,
           ln_g, ln_b, w_router, b_router, w_mlp1, b_mlp1, w_mlp2, b_mlp2):
    sp = stacked_params(dict(
        a_sink=a_sink, b_q_norm=b_q_norm, b_w_uq=b_w_uq, b_kv_norm=b_kv_norm, b_w_ukv=b_w_ukv, c_w_gate=c_w_gate,
        c_b_gate=c_b_gate, c_norm=c_norm, d_mu=d_mu, d_w0=d_w0, d_w2=d_w2, d_a0=d_a0, d_a2=d_a2, d_g2=d_g2,
        d_k_k=d_k_k, d_k_a=d_k_a, d_r_k=d_r_k, d_ln_g=d_ln_g, d_ln_b=d_ln_b, ln_g=ln_g, ln_b=ln_b,
        w_router=w_router, b_router=b_router))
    assert x_prompt.shape == (N_CTX_SEQ, CTX_LEN, D_MODEL) and x_sample.shape == (N_LAT_SEQ, LAT_LEN, D_MODEL)
    x = jnp.concatenate([x_prompt.reshape(N_CTX, D_MODEL), x_sample.reshape(-1, D_MODEL)], axis=0)
    cond8 = jnp.concatenate([c_ctx[None], c, jnp.zeros((8 - 1 - N_LAT_SEQ, D_MODEL), F32)], axis=0)
    mod = modulation_table(cond8, w_mod, b_mod)[:, :1 + N_LAT_SEQ].reshape(DEPTH * (1 + N_LAT_SEQ), 6, D_MODEL)
    tables = _rope_tables()
    w_small, w_g = prepare_in_weights(w_in)
    w_br_bf, w_out_bf = w_br.astype(BF16), w_out.astype(BF16)
    new = {name: [] for name in ("a_k", "a_v", "b_ckv", "b_kpe", "c", "d")}
    for l in range(DEPTH):
        (aq, ak, av, bq, bckv, bkpe, cq4, ck4, cla_f, cla_b, cv, cgate,
         r, v, kk, lw_f, lw_b, k_f, k_b, a_f, a_b, bonus, dgate) = mixer_prelude(x, mod, w_small, l, tables, sp)

        o_a = gqa_attention(aq, ak, av, sp['a_sink'], cache_a_k, cache_a_v, l)
        o_b = mla_attention(bq, bckv, bkpe, sp['b_w_ukv'], cache_b_ckv, cache_b_kpe, l)

        c_s0 = jnp.swapaxes(state_c[:, l], 3, 4).reshape(1, N_CHAIN, C_DV, C_DK)
        co_f, co_b, c_fin = _recurrence_calls(functools.partial(_gla_kernel, dot=_dot1), "gla", [cq4, ck4, cv],
                                              [cla_f], [cla_b], c_s0, (C_DV, C_DK), C_HEADS * C_DV)
        d_s0 = state_d[:, l].reshape(1, N_CHAIN, D_N, D_N)
        y_f, y_b, d_fin = _recurrence_calls(functools.partial(_rwkv_kernel, dot=_dot1), "rwkv7", [r, v, kk],
                                            [lw_f, k_f, a_f], [lw_b, k_b, a_b], d_s0, (D_N, D_N), D_HEADS * D_N)

        x1, h2, top_i, top_w = merge_and_route(x, mod, o_a, o_b, co_f, co_b, cgate, y_f, y_b, bonus, dgate,
                                               w_g, w_br_bf, w_out_bf, l, sp)
        ys = moe_experts(h2, top_i[:, :TOP_K], l, w_mlp1, b_mlp1, w_mlp2, b_mlp2)
        x = combine_and_norm(x1, mod, ys, top_w, l, sp)

        new["a_k"].append(ak[:N_CTX].reshape(N_CTX_SEQ, CTX_LEN, A_KV_HEADS, A_HD).transpose(0, 2, 1, 3))
        new["a_v"].append(av[:N_CTX].reshape(N_CTX_SEQ, CTX_LEN, A_KV_HEADS, A_HD).transpose(0, 2, 1, 3))
        new["b_ckv"].append(bckv[:N_CTX].reshape(N_CTX_SEQ, CTX_LEN, B_KV_LORA))
        new["b_kpe"].append(bkpe[:N_CTX].reshape(N_CTX_SEQ, CTX_LEN, B_ROPE))
        new["c"].append(jnp.swapaxes(c_fin.reshape(N_CTX_SEQ, 2, C_HEADS, C_DV, C_DK), 3, 4))
        new["d"].append(d_fin.reshape(N_CTX_SEQ, 2, D_HEADS, D_N, D_N))
    y_prompt = x[:N_CTX].reshape(x_prompt.shape)
    y_sample = x[N_CTX:].reshape(x_sample.shape)
    return (y_prompt, y_sample, *(jnp.stack(new[name], axis=1) for name in ("a_k", "a_v", "b_ckv", "b_kpe", "c", "d")))
```

```python
import functools

import jax
import jax.numpy as jnp
import numpy as np
from jax import lax
from jax.experimental import pallas as pl
from jax.experimental.pallas import tpu as pltpu

F32 = jnp.float32
BF16 = jnp.bfloat16

MXU_WIDTH = 256
VMEM_LIMIT = 56 * 1024 * 1024

D_MODEL = 1024
DEPTH = 2
GRID_W = 64
ROPE_BASE = 10000.0
A_HEADS, A_KV_HEADS, A_HD = 4, 2, 64
B_HEADS, B_NOPE, B_ROPE, B_VD, B_Q_LORA, B_KV_LORA = 4, 64, 32, 64, 192, 128
C_HEADS, C_DK, C_DV, C_GATE_RANK, C_GATE_TEMP = 4, 32, 64, 16, 16.0
D_HEADS, D_N, D_DECAY_RANK, D_AAA_RANK, D_GATE_RANK, D_GN_EPS = 4, 64, 64, 64, 128, 64e-5
BRANCH_W = 256
N_BRANCH = 4
N_EXPERTS = 32
TOP_K = 4
SWIGLU_LIMIT = 7.0
SWIGLU_ALPHA = 1.702
ALPHA = (2 * DEPTH) ** 0.25
LN_EPS = 1e-5
RMS_EPS = 1e-6

N_CTX_SEQ, CTX_LEN = 16, 256
N_LAT_SEQ, LAT_LEN = 2, 2048
N_CTX = N_CTX_SEQ * CTX_LEN
N_TOK = N_CTX + N_LAT_SEQ * LAT_LEN
ROW_TILE = 256
N_TILES = N_TOK // ROW_TILE
CTX_TILES = N_CTX // ROW_TILE
LAT_TILES_PER_SEQ = LAT_LEN // ROW_TILE
N_SEQ = N_CTX_SEQ + N_LAT_SEQ
MLA_HEAD_LANES = 128
MOE_ROW_TILE = 256
MOE_SPLITS = 2
MOE_ROWS = N_TOK * TOP_K + N_EXPERTS * MOE_ROW_TILE

_ORIG = dict(aq=(0, 256), ak=(256, 384), av=(384, 512), bcq=(512, 704), bckv=(704, 832), bkpe=(832, 864),
             cq=(864, 992), ck=(992, 1120), cv=(1120, 1376), cog=(1376, 1632), caf=(1632, 1648), cab=(1648, 1664),
             zd=(1664, 2816))
_ORDER = ("aq", "ak", "av", "cq", "ck", "cv", "cog", "zd", "bcq", "caf", "cab", "bkpe", "bckv")
COL = {}
_off = 0
for _name in _ORDER:
    _w = _ORIG[_name][1] - _ORIG[_name][0]
    COL[_name] = (_off, _off + _w)
    _off += _w
SMALL_COLS = _off
G_START = 2816


def _cs(name):
    return slice(*COL[name])


def _split3(x):
    hi = x.astype(BF16)
    r1 = x - hi.astype(F32)
    mid = r1.astype(BF16)
    lo = (r1 - mid.astype(F32)).astype(BF16)
    return hi, mid, lo


def _split2(x):
    hi = x.astype(BF16)
    lo = (x - hi.astype(F32)).astype(BF16)
    return hi, lo


def _bdot(a, b, dims):
    return lax.dot_general(a, b, dims, preferred_element_type=F32)


_D2 = (((1,), (0,)), ((), ()))
_D2T = (((1,), (1,)), ((), ()))
_NN = (((2,), (1,)), ((0,), (0,)))
_NT = (((2,), (2,)), ((0,), (0,)))
_TN = (((1,), (1,)), ((0,), (0,)))


def _dot1(a, b, dims=_D2):
    return _bdot(a.astype(BF16), b.astype(BF16), dims)


def _dot3(a, b, dims=_D2):
    ah, al = _split2(a)
    bh, bl = _split2(b)
    return _bdot(ah, bh, dims) + (_bdot(ah, bl, dims) + _bdot(al, bh, dims))


def _dot_exact_lhs(a01, b, dims=_D2):
    a = a01.astype(BF16)
    h, m, l = _split3(b)
    return _bdot(a, h, dims) + (_bdot(a, m, dims) + _bdot(a, l, dims))


def _dot_exact_rhs(a, b01, dims=_D2):
    b = b01.astype(BF16)
    h, m, l = _split3(a)
    return _bdot(h, b, dims) + (_bdot(m, b, dims) + _bdot(l, b, dims))


def _dot6(a, b, dims=_D2):
    ah, am, al = _split3(a)
    bh, bm, bl = _split3(b)
    return (_bdot(ah, bh, dims) + (_bdot(ah, bm, dims) + _bdot(am, bh, dims))
            + (_bdot(am, bm, dims) + (_bdot(ah, bl, dims) + _bdot(al, bh, dims))))


def _sigmoid(x):
    return 0.5 * jnp.tanh(0.5 * x) + 0.5


def _softplus(x):
    return jnp.maximum(x, 0.0) + jnp.log(1.0 + jnp.exp(-jnp.abs(x)))


def _mod_row(t):
    return jnp.where(t < CTX_TILES, 0, 1 + (t - CTX_TILES) // LAT_TILES_PER_SEQ)


def _full(shape):
    nd = len(shape)
    return pl.BlockSpec(shape, lambda *_: (0,) * nd)


MOD_COL_TILE = 1536


def _mod_kernel(c_ref, w_ref, b_ref, o_ref):
    c = c_ref[...]
    o_ref[0] = _dot3(c * _sigmoid(c), w_ref[0]) + b_ref[0]


def modulation_table(cond8, w_mod, b_mod):
    depth, d, six_d = w_mod.shape
    return pl.pallas_call(
        _mod_kernel,
        grid=(depth, six_d // MOD_COL_TILE),
        in_specs=[pl.BlockSpec((8, d), lambda l, j: (0, 0)),
                  pl.BlockSpec((1, d, MOD_COL_TILE), lambda l, j: (l, 0, j)),
                  pl.BlockSpec((1, 1, MOD_COL_TILE), lambda l, j: (l, 0, j))],
        out_specs=pl.BlockSpec((1, 8, MOD_COL_TILE), lambda l, j: (l, 0, j)),
        out_shape=jax.ShapeDtypeStruct((depth, 8, six_d), F32),
        compiler_params=pltpu.CompilerParams(dimension_semantics=("parallel", "parallel")),
        name="modulation",
    )(cond8, w_mod, b_mod.reshape(depth, 1, six_d))


WPREP_ROWS = 128


def _wprep_kernel(w_ref, small_ref, gate_ref):
    for name in _ORDER:
        lo, hi = _ORIG[name]
        small_ref[0, :, _cs(name)] = w_ref[0, :, lo:hi].astype(BF16)
    gate_ref[0] = w_ref[0, :, G_START:].astype(BF16)


def prepare_in_weights(w_in):
    depth, d, cols = w_in.shape
    return pl.pallas_call(
        _wprep_kernel,
        grid=(depth, d // WPREP_ROWS),
        in_specs=[pl.BlockSpec((1, WPREP_ROWS, cols), lambda l, r: (l, r, 0))],
        out_specs=[pl.BlockSpec((1, WPREP_ROWS, SMALL_COLS), lambda l, r: (l, r, 0)),
                   pl.BlockSpec((1, WPREP_ROWS, cols - G_START), lambda l, r: (l, r, 0))],
        out_shape=[jax.ShapeDtypeStruct((depth, d, SMALL_COLS), BF16),
                   jax.ShapeDtypeStruct((depth, d, cols - G_START), BF16)],
        compiler_params=pltpu.CompilerParams(dimension_semantics=("parallel", "parallel")),
        name="prepare_in_weights",
    )(w_in)


def _rot_pairs(x, half, lane_mod_base=0):
    w = x.shape[-1]
    lane = lax.broadcasted_iota(jnp.int32, (1, w), 1) - lane_mod_base
    first = (lane % (2 * half)) < half
    return jnp.where(first, -pltpu.roll(x, w - half, axis=1), pltpu.roll(x, half, axis=1))


def _pre_kernel(x_ref, xp_ref, xn_ref, mod_ref, w_ref, ca_ref, sa_ref, cb_ref, sb_ref, ck_ref, sk_ref,
                qnorm_ref, kvnorm_ref, wuq_ref, cwg_ref, cbg_ref, rep_ref, mu_ref, dw0_ref, dw2_ref, da0_ref,
                da2_ref, dg2_ref, dkk_ref, dka_ref, drk_ref, bd_ref,
                aq_o, ak_o, av_o, bq_o, bckv_o, bkpe_o, cq4_o, ck4_o, claf_o, clab_o, cv_o, cgate_o,
                r_o, v_o, kk_o, lwf_o, lwb_o, kf_o, kb_o, af_o, ab_o, bonus_o, dgate_o):
    t = pl.program_id(0)
    tm = x_ref.shape[0]
    sh1 = mod_ref[0, 0:1, :]
    sc1 = mod_ref[0, 1:2, :]

    def modulate(xv):
        return (xv * (1.0 + sc1) + sh1).astype(BF16)

    h_all = jnp.concatenate([modulate(x_ref[...]), modulate(xp_ref[...]), modulate(xn_ref[...])], axis=0)
    z_all = jnp.dot(h_all, w_ref[0], preferred_element_type=F32)
    z = z_all[:tm]

    aq = z[:, _cs("aq")]
    ak = z[:, _cs("ak")]
    aq_o[...] = aq * ca_ref[...] + _rot_pairs(aq, A_HD // 4) * sa_ref[...]
    ak_o[...] = ak * ca_ref[:, :ak.shape[1]] + _rot_pairs(ak, A_HD // 4) * sa_ref[:, :ak.shape[1]]
    av_o[...] = z[:, _cs("av")]

    bcq = z[:, _cs("bcq")]
    qn = bcq * lax.rsqrt(jnp.mean(bcq * bcq, axis=-1, keepdims=True) + RMS_EPS) * qnorm_ref[...]
    bq = _dot1(qn, wuq_ref[...])
    bq_o[...] = bq * cb_ref[...] + _rot_pairs(bq, B_ROPE // 4, lane_mod_base=B_NOPE) * sb_ref[...]
    bckv = z[:, _cs("bckv")]
    bckv_o[...] = bckv * lax.rsqrt(jnp.mean(bckv * bckv, axis=-1, keepdims=True) + RMS_EPS) * kvnorm_ref[...]
    kpe_lo = COL["bkpe"][0] // 128 * 128
    kblk = z[:, kpe_lo:kpe_lo + 128]
    kblk = kblk * ck_ref[...] + _rot_pairs(kblk, B_ROPE // 4) * sk_ref[...]
    bkpe_o[...] = kblk[:, COL["bkpe"][0] - kpe_lo:COL["bkpe"][1] - kpe_lo]

    rep = rep_ref[...]
    cq4_o[...] = _dot1(z[:, _cs("cq")] * (C_DK ** -0.5), rep)
    ck4_o[...] = _dot1(z[:, _cs("ck")], rep)
    cv_o[...] = z[:, _cs("cv")]
    cog = z[:, _cs("cog")]
    cgate_o[...] = cog * _sigmoid(cog)
    for direction, (name, out) in enumerate((("caf", claf_o), ("cab", clab_o))):
        pre = _dot3(z[:, _cs(name)], cwg_ref[direction]) + cbg_ref[direction]
        la_hi, la_lo = _split2(-_softplus(-pre) * (1.0 / C_GATE_TEMP))
        out[...] = _bdot(la_hi, rep, _D2) + _bdot(la_lo, rep, _D2)

    zd_cols = _cs("zd")
    zd = z[:, zd_cols]
    j = (t - CTX_TILES) % LAT_TILES_PER_SEQ
    latent = t >= CTX_TILES
    has_prev = jnp.logical_and(latent, j != 0)
    has_next = jnp.logical_and(latent, j != LAT_TILES_PER_SEQ - 1)
    prev_row = jnp.where(has_prev, z_all[tm + 7:tm + 8, zd_cols], 0.0)
    next_row = jnp.where(has_next, z_all[tm + 8:tm + 9, zd_cols], 0.0)
    row = lax.broadcasted_iota(jnp.int32, (tm, 1), 0)
    up = jnp.where(row == 0, prev_row, pltpu.roll(zd, 1, axis=0))
    dn = jnp.where(row == tm - 1, next_row, pltpu.roll(zd, tm - 1, axis=0))
    zd = zd + (0.5 * (up + dn) - zd) * mu_ref[...]

    hn = D_HEADS * D_N
    d_r, d_k, d_v = zd[:, :hn], zd[:, hn:2 * hn], zd[:, 2 * hn:3 * hn]
    o = 3 * hn
    d_w = (zd[:, o:o + D_DECAY_RANK], zd[:, o + D_DECAY_RANK:o + 2 * D_DECAY_RANK])
    o += 2 * D_DECAY_RANK
    d_a = (zd[:, o:o + D_AAA_RANK], zd[:, o + D_AAA_RANK:o + 2 * D_AAA_RANK])
    o += 2 * D_AAA_RANK
    d_g = zd[:, o:o + D_GATE_RANK]
    bd = bd_ref[...]
    kk = d_k * dkk_ref[...]
    kk = kk / jnp.maximum(jnp.sqrt(_dot_exact_rhs(kk * kk, bd)), 1e-12)
    r_o[...] = d_r
    v_o[...] = d_v
    kk_o[...] = kk
    k_sum = None
    for direction, (lw_o, k_o, a_o) in enumerate(((lwf_o, kf_o, af_o), (lwb_o, kb_o, ab_o))):
        w_log = -_softplus(-(dw0_ref[direction] + _dot3(jnp.tanh(d_w[direction]), dw2_ref[direction]))) - 0.5
        lw_o[...] = -jnp.exp(w_log)
        a = _sigmoid(da0_ref[direction] + _dot1(d_a[direction], da2_ref[direction]))
        k_dir = d_k * (1.0 + (a - 1.0) * dka_ref[...])
        k_o[...] = k_dir
        a_o[...] = a
        k_sum = k_dir if k_sum is None else k_sum + k_dir
    bonus_o[...] = d_v * _dot_exact_rhs(d_r * drk_ref[...] * k_sum, bd)
    dgate_o[...] = _dot1(_sigmoid(d_g), dg2_ref[...])


def _rope_tables():
    pos = np.arange(LAT_LEN)
    rowp, colp = (pos // GRID_W).astype(np.float32), (pos % GRID_W).astype(np.float32)

    f32 = np.float32

    def head_tables(rot_dim):
        quarter = rot_dim // 4
        inv = (f32(ROPE_BASE) ** (-np.arange(quarter, dtype=f32) / f32(quarter))).astype(f32)
        ar = (rowp[:, None] * inv).astype(f32)
        ac = (colp[:, None] * inv).astype(f32)
        cos = np.concatenate([np.cos(ar), np.cos(ar), np.cos(ac), np.cos(ac)], axis=-1).astype(f32)
        sin = np.concatenate([np.sin(ar), np.sin(ar), np.sin(ac), np.sin(ac)], axis=-1).astype(f32)
        return cos, sin

    def with_identity(c, s):
        w = c.shape[1]
        return (jnp.asarray(np.concatenate([np.ones((ROW_TILE, w), f32), c], axis=0)),
                jnp.asarray(np.concatenate([np.zeros((ROW_TILE, w), f32), s], axis=0)))

    ca, sa = head_tables(A_HD)
    ca, sa = with_identity(np.tile(ca, (1, A_HEADS)), np.tile(sa, (1, A_HEADS)))
    cbh, sbh = head_tables(B_ROPE)
    ones, zeros = np.ones((LAT_LEN, B_NOPE), f32), np.zeros((LAT_LEN, B_NOPE), f32)
    qpad = MLA_HEAD_LANES - B_NOPE - B_ROPE
    cb, sb = with_identity(
        np.tile(np.concatenate([ones, cbh, np.ones((LAT_LEN, qpad), f32)], axis=1), (1, B_HEADS)),
        np.tile(np.concatenate([zeros, sbh, np.zeros((LAT_LEN, qpad), f32)], axis=1), (1, B_HEADS)))
    pad = 128 - B_ROPE
    ck, sk = with_identity(np.concatenate([np.ones((LAT_LEN, pad), f32), cbh], axis=1),
                           np.concatenate([np.zeros((LAT_LEN, pad), f32), sbh], axis=1))
    return ca, sa, cb, sb, ck, sk


def _lane_repeat_matrix():
    m = np.zeros((C_HEADS * C_DK, C_HEADS * 128), np.float32)
    for h in range(C_HEADS):
        for g in range(128 // C_DK):
            for d in range(C_DK):
                m[h * C_DK + d, h * 128 + g * C_DK + d] = 1.0
    return jnp.asarray(m, BF16)


def _head_block_diag():
    m = np.kron(np.eye(D_HEADS, dtype=np.float32), np.ones((D_N, D_N), np.float32))
    return jnp.asarray(m, BF16)


PRE_OUT_WIDTHS = (256, 128, 128, B_HEADS * MLA_HEAD_LANES, 128, 32, 512, 512, 512, 512, 256, 256) + (256,) * 11


def _layer_block(arr, layer):
    nd = arr.ndim
    return pl.BlockSpec((1,) + arr.shape[1:], lambda *_: (layer,) + (0,) * (nd - 1))


def _layer_item(arr, index):
    nd = arr.ndim
    return pl.BlockSpec((None,) + arr.shape[1:], lambda *_: (index,) + (0,) * (nd - 1))


def _mod_spec(layer):
    return pl.BlockSpec((1, 6, D_MODEL), lambda t: (layer * (1 + N_LAT_SEQ) + _mod_row(t), 0, 0))


def stacked_params(p):
    depth = p['d_mu'].shape[0]
    hn = D_HEADS * D_N
    w_uq = p['b_w_uq'].reshape(depth, B_Q_LORA, B_HEADS, B_NOPE + B_ROPE)
    w_uq = jnp.pad(w_uq, ((0, 0), (0, 0), (0, 0), (0, MLA_HEAD_LANES - B_NOPE - B_ROPE)))
    row = lambda a: a.reshape(depth, 1, -1)
    return dict(
        b_q_norm=row(p['b_q_norm']), b_kv_norm=row(p['b_kv_norm']), w_uq=w_uq.reshape(depth, B_Q_LORA, -1),
        c_w_gate=p['c_w_gate'], c_b_gate=p['c_b_gate'].reshape(depth, 2, 1, -1), d_mu=row(p['d_mu']),
        d_w0=p['d_w0'].reshape(depth, 2, 1, hn), d_w2=p['d_w2'], d_a0=p['d_a0'].reshape(depth, 2, 1, hn),
        d_a2=p['d_a2'], d_g2=p['d_g2'], d_k_k=row(p['d_k_k']), d_k_a=row(p['d_k_a']), d_r_k=row(p['d_r_k']),
        c_norm=row(jnp.tile(p['c_norm'], (1, C_HEADS))), d_ln_g=row(p['d_ln_g']), d_ln_b=row(p['d_ln_b']),
        ln_g=p['ln_g'].reshape(depth * 2, 1, -1), ln_b=p['ln_b'].reshape(depth * 2, 1, -1),
        w_router=p['w_router'], b_router=row(p['b_router']), a_sink=p['a_sink'].reshape(-1), b_w_ukv=p['b_w_ukv'])


def mixer_prelude(x, mod, w_small, layer, tables, sp):
    tm = ROW_TILE
    tab_idx = lambda t: (jnp.where(t < CTX_TILES, 0, 1 + (t - CTX_TILES) % LAT_TILES_PER_SEQ), 0)
    small = [sp[k] for k in ('b_q_norm', 'b_kv_norm', 'w_uq', 'c_w_gate', 'c_b_gate')] + [_lane_repeat_matrix()]
    small += [sp[k] for k in ('d_mu', 'd_w0', 'd_w2', 'd_a0', 'd_a2', 'd_g2', 'd_k_k', 'd_k_a', 'd_r_k')]
    small += [_head_block_diag()]
    const = lambda a: _full(a.shape) if a.dtype == BF16 else _layer_item(a, layer)
    in_specs = ([pl.BlockSpec((tm, D_MODEL), lambda t: (t, 0)),
                 pl.BlockSpec((8, D_MODEL), lambda t: (jnp.maximum(t * (tm // 8) - 1, 0), 0)),
                 pl.BlockSpec((8, D_MODEL), lambda t: (jnp.minimum((t + 1) * (tm // 8), N_TOK // 8 - 1), 0)),
                 _mod_spec(layer),
                 _layer_block(w_small, layer)]
                + [pl.BlockSpec((tm, tab.shape[1]), tab_idx) for tab in tables]
                + [const(a) for a in small])
    return pl.pallas_call(
        _pre_kernel,
        grid=(N_TILES,),
        in_specs=in_specs,
        out_specs=[pl.BlockSpec((tm, w), lambda t: (t, 0)) for w in PRE_OUT_WIDTHS],
        out_shape=[jax.ShapeDtypeStruct((N_TOK, w), F32) for w in PRE_OUT_WIDTHS],
        compiler_params=pltpu.CompilerParams(dimension_semantics=("parallel",), vmem_limit_bytes=VMEM_LIMIT),
        name="mixer_prelude",
    )(x, x, x, mod, w_small, *tables, *small)


ATT_Q_BLOCK = 128
MLA_Q_BLOCK = 256
ATT_WINDOW = 128
ATT_NEG_INF = -1e30
CACHE_LEN = 512


def _softmax_pv(s, v, sink):
    dv = v.shape[1] // 2
    m = jnp.max(s, axis=-1, keepdims=True)
    if sink is not None:
        m = jnp.maximum(m, sink)
    e = jnp.exp((s - m).astype(BF16))
    o = jnp.dot(e, v, preferred_element_type=F32)
    den = o[:, dv:dv + 1]
    if sink is not None:
        den = den + jnp.exp(sink - m)
    return o[:, :dv] / den


def _with_ones(v):
    return jnp.concatenate([v.astype(BF16), jnp.ones(v.shape, BF16)], axis=1)


def _gqa_kernel(sink_ref, q_ref, k_ref, v_ref, *rest, hd, group, scale, windowed, sink_base):
    if windowed:
        kp_ref, kn_ref, vp_ref, vn_ref, kc_ref, vc_ref, _, o_ref = rest
    else:
        (o_ref,) = rest
    i = pl.program_id(1)
    tq = q_ref.shape[0]
    n_kv = k_ref.shape[1] // hd
    if windowed:
        qpos = i * tq + lax.broadcasted_iota(jnp.int32, (tq, 3 * tq), 0)
        kpos = (i - 1) * tq + lax.broadcasted_iota(jnp.int32, (tq, 3 * tq), 1)
        n_tok = pl.num_programs(1) * tq
        mask = (jnp.abs(qpos - kpos) <= ATT_WINDOW) & (kpos >= 0) & (kpos < n_tok)
        mask = jnp.concatenate([mask] * group, axis=0)
    for kvh in range(n_kv):
        ks = slice(kvh * hd, (kvh + 1) * hd)
        qs = [q_ref[:, (kvh * group + g) * hd:(kvh * group + g + 1) * hd] for g in range(group)]
        q = (jnp.concatenate(qs, axis=0) * scale).astype(BF16)
        sink = jnp.concatenate(
            [jnp.full((tq, 1), sink_ref[sink_base + kvh * group + g], F32) for g in range(group)], axis=0)
        if windowed:
            k_win = jnp.concatenate([kp_ref[:, ks], k_ref[:, ks], kn_ref[:, ks]], axis=0)
            v_win = jnp.concatenate([vp_ref[:, ks], v_ref[:, ks], vn_ref[:, ks]], axis=0)
            s_win = _bdot(q, k_win.astype(BF16), _D2T)
            s_win = jnp.where(mask, s_win, ATT_NEG_INF)
            s_ctx = _bdot(q, kc_ref[0, 0, kvh].astype(BF16), _D2T)
            s = jnp.concatenate([s_win, s_ctx], axis=1)
            v = jnp.concatenate([v_win, vc_ref[0, 0, kvh]], axis=0)
        else:
            s = _bdot(q, k_ref[:, ks].astype(BF16), _D2T)
            v = v_ref[:, ks]
        o = _softmax_pv(s, _with_ones(v), sink)
        for g in range(group):
            h = kvh * group + g
            o_ref[:, h * hd:(h + 1) * hd] = o[g * tq:(g + 1) * tq]


def gqa_attention(q, k, v, sink, cache_k, cache_v, layer):
    qw, kw = q.shape[1], k.shape[1]
    group = qw // kw
    scale = A_HD ** -0.5
    params = pltpu.CompilerParams(dimension_semantics=("parallel", "parallel"))
    out_shape = jax.ShapeDtypeStruct((N_TOK, qw), F32)
    ctx_spec = lambda w: pl.BlockSpec((CTX_LEN, w), lambda s, i, sk: (s, 0))
    o = pl.pallas_call(
        functools.partial(_gqa_kernel, hd=A_HD, group=group, scale=scale, windowed=False, sink_base=layer * A_HEADS),
        grid_spec=pltpu.PrefetchScalarGridSpec(
            num_scalar_prefetch=1, grid=(N_CTX_SEQ, 1), in_specs=[ctx_spec(qw), ctx_spec(kw), ctx_spec(kw)],
            out_specs=ctx_spec(qw)),
        out_shape=out_shape, compiler_params=params, name="gqa_full",
    )(sink, q, k, v)
    tq = ATT_Q_BLOCK
    nb = LAT_LEN // tq
    base = N_CTX // tq
    blk = lambda w, f: pl.BlockSpec((tq, w), lambda b, i, sk: (base + nb * b + f(i), 0))
    same = lambda i: i
    prev = lambda i: jnp.maximum(i - 1, 0)
    nxt = lambda i: jnp.minimum(i + 1, nb - 1)
    cspec = pl.BlockSpec((1, 1) + cache_k.shape[2:], lambda b, i, sk: (b, layer, 0, 0, 0))
    return pl.pallas_call(
        functools.partial(_gqa_kernel, hd=A_HD, group=group, scale=scale, windowed=True, sink_base=layer * A_HEADS),
        grid_spec=pltpu.PrefetchScalarGridSpec(
            num_scalar_prefetch=1, grid=(N_LAT_SEQ, nb),
            in_specs=[blk(qw, same), blk(kw, same), blk(kw, same), blk(kw, prev), blk(kw, nxt), blk(kw, prev),
                      blk(kw, nxt), cspec, cspec, pl.BlockSpec(memory_space=pl.ANY)],
            out_specs=blk(qw, same)),
        out_shape=out_shape, input_output_aliases={10: 0}, compiler_params=params, name="gqa_windowed",
    )(sink, q, k, v, k, k, v, v, cache_k, cache_v, o)


def _mla_kernel(q_ref, ckv_ref, kpe_ref, wukv_ref, *rest, n_heads, nope, rope, vd, scale, cached):
    if cached:
        cckv_ref, ckpe_ref, _, o_ref, k_scr, vext_scr = rest
    else:
        o_ref, k_scr, vext_scr = rest
    i = pl.program_id(1)
    n_cache = k_scr.shape[0] - ckv_ref.shape[0]
    hw = nope + vd
    hl = MLA_HEAD_LANES

    @pl.when(i == 0)
    def _():
        w = wukv_ref[...].astype(BF16)

        def expand(rows, kpe_rows, lo, hi):
            kv = jnp.dot(rows.astype(BF16), w, preferred_element_type=F32).astype(BF16)
            n = hi - lo
            kpe = kpe_rows.astype(BF16)
            for h in range(n_heads):
                k_scr[lo:hi, hl * h:hl * (h + 1)] = jnp.concatenate(
                    [kv[:, h * hw:h * hw + nope], kpe, jnp.zeros((n, hl - nope - rope), BF16)], axis=1)
                vext_scr[lo:hi, 2 * vd * h:2 * vd * (h + 1)] = jnp.concatenate(
                    [kv[:, h * hw + nope:(h + 1) * hw], jnp.ones((n, vd), BF16)], axis=1)

        if cached:
            expand(cckv_ref[0, 0], ckpe_ref[0, 0], 0, n_cache)
        expand(ckv_ref[...], kpe_ref[...], n_cache, k_scr.shape[0])

    for h in range(n_heads):
        qh = (q_ref[:, hl * h:hl * (h + 1)] * scale).astype(BF16)
        s = _bdot(qh, k_scr[:, hl * h:hl * (h + 1)], _D2T)
        o_ref[:, h * vd:(h + 1) * vd] = _softmax_pv(s, vext_scr[:, 2 * vd * h:2 * vd * (h + 1)], None)


def mla_attention(q, ckv, kpe, w_ukv, cache_ckv, cache_kpe, layer):
    qw = q.shape[1]
    tq = MLA_Q_BLOCK
    kw = dict(n_heads=B_HEADS, nope=B_NOPE, rope=B_ROPE, vd=B_VD, scale=(B_NOPE + B_ROPE) ** -0.5)
    params = pltpu.CompilerParams(dimension_semantics=("parallel", "arbitrary"))
    out_shape = jax.ShapeDtypeStruct((N_TOK, B_HEADS * B_VD), F32)
    scratch = lambda rows: [pltpu.VMEM((rows, B_HEADS * MLA_HEAD_LANES), BF16),
                            pltpu.VMEM((rows, 2 * B_HEADS * B_VD), BF16)]
    nbc = CTX_LEN // tq
    o = pl.pallas_call(
        functools.partial(_mla_kernel, cached=False, **kw),
        grid=(N_CTX_SEQ, nbc),
        in_specs=[pl.BlockSpec((tq, qw), lambda s, i: (s * nbc + i, 0)),
                  pl.BlockSpec((CTX_LEN, B_KV_LORA), lambda s, i: (s, 0)),
                  pl.BlockSpec((CTX_LEN, B_ROPE), lambda s, i: (s, 0)),
                  _layer_item(w_ukv, layer)],
        out_specs=pl.BlockSpec((tq, B_HEADS * B_VD), lambda s, i: (s * nbc + i, 0)),
        out_shape=out_shape,
        scratch_shapes=scratch(CTX_LEN),
        compiler_params=params, name="mla_context",
    )(q, ckv, kpe, w_ukv)
    nb = LAT_LEN // tq
    base = N_CTX // tq
    lat0 = N_CTX // LAT_LEN
    s_len = CACHE_LEN + LAT_LEN
    return pl.pallas_call(
        functools.partial(_mla_kernel, cached=True, **kw),
        grid=(N_LAT_SEQ, nb),
        in_specs=[pl.BlockSpec((tq, qw), lambda b, i: (base + nb * b + i, 0)),
                  pl.BlockSpec((LAT_LEN, B_KV_LORA), lambda b, i: (lat0 + b, 0)),
                  pl.BlockSpec((LAT_LEN, B_ROPE), lambda b, i: (lat0 + b, 0)),
                  _layer_item(w_ukv, layer),
                  pl.BlockSpec((1, 1, CACHE_LEN, B_KV_LORA), lambda b, i: (b, layer, 0, 0)),
                  pl.BlockSpec((1, 1, CACHE_LEN, B_ROPE), lambda b, i: (b, layer, 0, 0)),
                  pl.BlockSpec(memory_space=pl.ANY)],
        out_specs=pl.BlockSpec((tq, B_HEADS * B_VD), lambda b, i: (base + nb * b + i, 0)),
        out_shape=out_shape, input_output_aliases={6: 0},
        scratch_shapes=scratch(s_len),
        compiler_params=params, name="mla_latent",
    )(q, ckv, kpe, w_ukv, cache_ckv, cache_kpe, o)


CHUNK = 64
GLA_SUB = 16
PAIR = 2
N_CHAIN = PAIR * 2 * 4


def _is_back(shape):
    return (lax.broadcasted_iota(jnp.int32, shape, 0) // 4) % 2 == 1


def _chains(ref_f, ref_b, width):
    return jnp.stack([ref[0, s, 0, :, h * width:(h + 1) * width]
                      for s in range(PAIR) for ref in (ref_f, ref_b) for h in range(4)], axis=0)


def _unchain(y, o_f, o_b):
    for s in range(PAIR):
        o_f[0, s, 0] = jnp.concatenate([y[s * 8 + h] for h in range(4)], axis=-1)
        o_b[0, s, 0] = jnp.concatenate([y[s * 8 + 4 + h] for h in range(4)], axis=-1)


def _dir_masks(L):
    shape = (N_CHAIN, L, L)
    back = _is_back(shape)
    row = lax.broadcasted_iota(jnp.int32, shape, 1)
    col = lax.broadcasted_iota(jnp.int32, shape, 2)
    ahead = jnp.where(back, col - row, row - col)
    return ahead >= 0, ahead > 0, row == col


def _chunk_end(ci):
    L = ci.shape[1]
    return jnp.where(_is_back((N_CHAIN, 1, 1)), ci[:, 0:1], ci[:, L - 1:L])


def _split_refs(refs, n_in, has_s0, has_sfin):
    ins = refs[:n_in]
    pos = n_in
    s0_ref = None
    if has_s0:
        s0_ref = refs[pos]
        pos += 3
    of_ref, ob_ref = refs[pos], refs[pos + 1]
    pos += 2
    sfin_ref = refs[pos] if has_sfin else None
    return ins, s0_ref, of_ref, ob_ref, sfin_ref, refs[-1]


def _init_state(s_scr, s0_ref):
    @pl.when(pl.program_id(1) == 0)
    def _():
        if s0_ref is None:
            s_scr[...] = jnp.zeros_like(s_scr)
        else:
            s_scr[...] = s0_ref[0]


def _emit_state(sfin_ref, s_new):
    if sfin_ref is None:
        return

    @pl.when(pl.program_id(1) == pl.num_programs(1) - 1)
    def _():
        sfin_ref[0] = s_new


def _rwkv_kernel(*refs, dot, has_s0, has_sfin):
    (rf, rb, vf, vb, kkf, kkb, lwf, lwb, kf, kb, af, ab), s0_ref, yf_ref, yb_ref, sfin_ref, s_scr = _split_refs(
        refs, 12, has_s0, has_sfin)
    _init_state(s_scr, s0_ref)
    n = D_N
    r = _chains(rf, rb, n)
    v = _chains(vf, vb, n)
    kk = _chains(kkf, kkb, n)
    lw = _chains(lwf, lwb, n)
    k = _chains(kf, kb, n)
    a = _chains(af, ab, n)
    L = r.shape[1]
    S = s_scr[...]
    incl, strict, diag = _dir_masks(L)
    ci = _dot_exact_lhs(jnp.where(incl, 1.0, 0.0), lw, _NN)
    ce = ci - lw
    cl = _chunk_end(ci)
    e_neg = jnp.exp(-ci)
    b = a * kk
    alpha = kk * jnp.exp(ce)
    rho = r * jnp.exp(ci)
    beta = b * e_neg
    kappa = k * e_neg
    e_end = jnp.exp(cl - ci)
    ar = jnp.concatenate([alpha, rho], axis=1)
    bk = jnp.concatenate([beta, kappa], axis=1)
    w = dot(ar, bk, _NT)
    nmat = jnp.where(strict, w[:, :L, :L], 0.0)
    mmat = jnp.where(strict, w[:, :L, L:], 0.0)
    p1 = jnp.where(incl, w[:, L:, :L], 0.0)
    p2 = jnp.where(incl, w[:, L:, L:], 0.0)
    x = jnp.where(diag, 1.0, 0.0) - nmat
    p = dot(nmat, nmat, _NN)
    span = 2
    while True:
        x = x + dot(x, p, _NN)
        span *= 2
        if span >= L:
            break
        p = dot(p, p, _NN)
    us = dot(ar, S, _NT)
    rhs = us[:, :L] + dot(mmat, v, _NN)
    d = -dot(x, rhs, _NN)
    dv = jnp.concatenate([d, v], axis=1)
    pp = jnp.concatenate([p1, p2], axis=2)
    _unchain(us[:, L:] + dot(pp, dv, _NN), yf_ref, yb_ref)
    bk_end = jnp.concatenate([b * e_end, k * e_end], axis=1)
    s_new = S * jnp.exp(cl) + dot(dv, bk_end, _TN)
    s_scr[...] = s_new
    _emit_state(sfin_ref, s_new)


def _gla_kernel(*refs, dot, has_s0, has_sfin):
    (qf, qb, kf, kb, vf, vb, laf, lab), s0_ref, of_ref, ob_ref, sfin_ref, s_scr = _split_refs(
        refs, 8, has_s0, has_sfin)
    _init_state(s_scr, s0_ref)
    q4 = _chains(qf, qb, 128)
    k4 = _chains(kf, kb, 128)
    la4 = _chains(laf, lab, 128)
    v = _chains(vf, vb, C_DV)
    g, L, lanes = q4.shape
    dk = C_DK
    n_sub = L // GLA_SUB
    st = s_scr[...]
    incl, _, _ = _dir_masks(L)
    c = _dot_exact_lhs(jnp.where(incl, 1.0, 0.0), la4, _NN)
    shape = (g, L, lanes)
    back = _is_back(shape)
    lane_blk = lax.broadcasted_iota(jnp.int32, shape, 2) // dk
    row_blk = lax.broadcasted_iota(jnp.int32, shape, 1) // GLA_SUB
    cref_f = jnp.zeros(shape, F32)
    cref_b = jnp.zeros(shape, F32)
    for j in range(1, n_sub):
        cref_f = jnp.where(lane_blk == j, c[:, j * GLA_SUB - 1:j * GLA_SUB], cref_f)
        cref_b = jnp.where(lane_blk == j - 1, c[:, j * GLA_SUB:j * GLA_SUB + 1], cref_b)
    cref = jnp.where(back, cref_b, cref_f)
    q_on = row_blk == lane_blk
    k_on = jnp.where(back, row_blk - lane_blk, lane_blk - row_blk) >= 0
    qh = jnp.where(q_on, q4 * jnp.exp(jnp.where(q_on, c - cref, 0.0)), 0.0)
    kh = jnp.where(k_on, k4 * jnp.exp(jnp.where(k_on, cref - c, 0.0)), 0.0)
    att = jnp.where(incl, dot(qh, kh, _NT), 0.0)
    cl = _chunk_end(c)
    qe = (q4 * jnp.exp(c))[:, :, :dk]
    ke = (k4 * jnp.exp(cl - c))[:, :, :dk]
    _unchain(dot(qe, st, _NT) + dot(att, v, _NN), of_ref, ob_ref)
    s_new = st * jnp.exp(cl[:, :, :dk]) + dot(v, ke, _TN)
    s_scr[...] = s_new
    _emit_state(sfin_ref, s_new)


def _recurrence_calls(kernel_fn, name, pairs, singles_f, singles_b, s0_lat, state_dims, out_width):
    def run(view, grid, group, s0, prev_out):
        nc = view[2]
        fwd_map = lambda p, c: (group(p), 0, c, 0, 0)
        bwd_map = lambda p, c: (group(p), 0, nc - 1 - c, 0, 0)
        blk = lambda w: (1, PAIR, 1, CHUNK, w)
        args, in_specs = [], []
        for af, ab in [(a, a) for a in pairs] + list(zip(singles_f, singles_b)):
            w = af.shape[-1]
            args += [af.reshape(view + (w,)), ab.reshape(view + (w,))]
            in_specs += [pl.BlockSpec(blk(w), fwd_map), pl.BlockSpec(blk(w), bwd_map)]
        out_specs = [pl.BlockSpec(blk(out_width), fwd_map), pl.BlockSpec(blk(out_width), bwd_map)]
        out_shape = [jax.ShapeDtypeStruct(view + (out_width,), F32)] * 2
        aliases = {}
        if s0 is not None:
            args += [s0] + [o.reshape(view + (out_width,)) for o in prev_out]
            in_specs += [_full(s0.shape), pl.BlockSpec(memory_space=pl.ANY), pl.BlockSpec(memory_space=pl.ANY)]
            aliases = {len(args) - 2: 0, len(args) - 1: 1}
        else:
            out_specs.append(pl.BlockSpec((1, N_CHAIN) + state_dims, lambda p, c: (p, 0, 0, 0)))
            out_shape.append(jax.ShapeDtypeStruct((grid[0], N_CHAIN) + state_dims, F32))
        return pl.pallas_call(
            functools.partial(kernel_fn, has_s0=s0 is not None, has_sfin=s0 is None),
            grid=grid, in_specs=in_specs, out_specs=out_specs, out_shape=out_shape,
            input_output_aliases=aliases, scratch_shapes=[pltpu.VMEM((N_CHAIN,) + state_dims, F32)],
            compiler_params=pltpu.CompilerParams(dimension_semantics=("parallel", "arbitrary")),
            name=name + ("_latent" if s0 is not None else "_context"),
        )(*args)

    ctx_nc = CTX_LEN // CHUNK
    ctx_view = (N_TOK // (PAIR * CTX_LEN), PAIR, ctx_nc, CHUNK)
    o_f, o_b, s_fin = run(ctx_view, (N_CTX_SEQ // PAIR, ctx_nc), lambda p: p, None, None)
    lat_nc = LAT_LEN // CHUNK
    lat_view = (N_TOK // (PAIR * LAT_LEN), PAIR, lat_nc, CHUNK)
    o_f, o_b = run(lat_view, (1, lat_nc), lambda p: N_CTX // (PAIR * LAT_LEN), s0_lat, (o_f, o_b))
    return o_f.reshape(N_TOK, out_width), o_b.reshape(N_TOK, out_width), s_fin


def _layer_norm(x, g, b):
    mu = jnp.mean(x, axis=-1, keepdims=True)
    xc = x - mu
    var = jnp.mean(xc * xc, axis=-1, keepdims=True)
    return xc * lax.rsqrt(var + LN_EPS) * g + b


def _merge_kernel(x_ref, mod_ref, oa_ref, ob_ref, cof_ref, cob_ref, cgate_ref, yf_ref, yb_ref, bonus_ref, dgate_ref,
                  wg_ref, wbr_ref, wout_ref, cnorm_ref, dlng_ref, dlnb_ref, lng_ref, lnb_ref, wr_ref, br_ref, bd_ref,
                  x1_o, h2_o, topi_o, topw_o):
    x = x_ref[...]
    m = mod_ref[0]
    sh1, sc1, g1, sh2, sc2 = m[0:1], m[1:2], m[2:3], m[3:4], m[4:5]
    bd = bd_ref[...]
    inv_n = 1.0 / D_N
    co = cof_ref[...] + cob_ref[...]
    o_c = co * lax.rsqrt(_dot_exact_rhs(co * co, bd) * inv_n + RMS_EPS) * cnorm_ref[...] * cgate_ref[...]
    y = yf_ref[...] + yb_ref[...]
    yc = y - _dot_exact_rhs(y, bd) * inv_n
    var = _dot_exact_rhs(yc * yc, bd) * inv_n
    o_d = (yc * lax.rsqrt(var + D_GN_EPS) * dlng_ref[...] + dlnb_ref[...] + bonus_ref[...]) * dgate_ref[...]
    branches = (oa_ref[...], ob_ref[...], o_c, o_d)
    h = (x * (1.0 + sc1) + sh1).astype(BF16)
    merged = None
    for n in range(N_BRANCH):
        gate = _sigmoid(jnp.dot(h, wg_ref[0, :, n * D_MODEL:(n + 1) * D_MODEL], preferred_element_type=F32))
        term = gate * jnp.dot(branches[n].astype(BF16), wbr_ref[0, n], preferred_element_type=F32)
        merged = term if merged is None else merged + term
    mix = jnp.dot(merged.astype(BF16), wout_ref[0], preferred_element_type=F32)
    x1 = _layer_norm(ALPHA * x + g1 * mix, lng_ref[...], lnb_ref[...])
    x1_o[...] = x1
    h2 = x1 * (1.0 + sc2) + sh2
    h2_o[...] = h2.astype(BF16)
    logits = _dot6(h2, wr_ref[...]) + br_ref[...]
    tm, n_e = logits.shape
    lane_e = lax.broadcasted_iota(jnp.int32, (tm, n_e), 1)
    lane_o = lax.broadcasted_iota(jnp.int32, (tm, topi_o.shape[1]), 1)
    top_i = jnp.zeros((tm, topi_o.shape[1]), jnp.int32)
    top_v = jnp.zeros((tm, topw_o.shape[1]), F32)
    vals = []
    for kth in range(TOP_K):
        mx = jnp.max(logits, axis=-1, keepdims=True)
        idx = jnp.min(jnp.where(logits == mx, lane_e, n_e), axis=-1, keepdims=True)
        vals.append(mx)
        top_i = jnp.where(lane_o == kth, idx, top_i)
        logits = jnp.where(lane_e == idx, -jnp.inf, logits)
    es = [jnp.exp(vk - vals[0]) for vk in vals]
    den = es[0] + es[1] + es[2] + es[3]
    for kth in range(TOP_K):
        top_v = jnp.where(lane_o == kth, es[kth] / den, top_v)
    topi_o[...] = top_i
    topw_o[...] = top_v


def merge_and_route(x, mod, o_a, o_b, co_f, co_b, cgate, y_f, y_b, bonus, dgate, w_g, w_br, w_out, layer, sp):
    tm = ROW_TILE
    hn = D_HEADS * D_N
    row = lambda w: pl.BlockSpec((tm, w), lambda t: (t, 0))
    small = [sp[k] for k in ('c_norm', 'd_ln_g', 'd_ln_b', 'ln_g', 'ln_b', 'w_router', 'b_router')]
    index = [layer, layer, layer, 2 * layer, 2 * layer, layer, layer]
    bd = _head_block_diag()
    return pl.pallas_call(
        _merge_kernel,
        grid=(N_TILES,),
        in_specs=([row(D_MODEL), _mod_spec(layer)]
                  + [row(hn)] * 9 + [_layer_block(w, layer) for w in (w_g, w_br, w_out)]
                  + [_layer_item(a, i) for a, i in zip(small, index)] + [_full(bd.shape)]),
        out_specs=[row(D_MODEL), row(D_MODEL), row(128), row(128)],
        out_shape=[jax.ShapeDtypeStruct((N_TOK, D_MODEL), F32), jax.ShapeDtypeStruct((MOE_ROWS, D_MODEL), BF16),
                   jax.ShapeDtypeStruct((N_TOK, 128), jnp.int32), jax.ShapeDtypeStruct((N_TOK, 128), F32)],
        compiler_params=pltpu.CompilerParams(dimension_semantics=("parallel",), vmem_limit_bytes=VMEM_LIMIT),
        name="merge_and_route",
    )(x, mod, o_a, o_b, co_f, co_b, cgate, y_f, y_b, bonus, dgate, w_g, w_br, w_out, *small, bd)


def _moe_kernel(te_ref, tv_ref, first_ref, slot_ref, next_ref, x_ref, w1_hbm, b1_ref, w2_hbm, b2_ref, perm_ref, *rest,
                layer):
    y_ref, w1buf, w2buf, sem, w1s, w2s, hs = rest[-7:]
    t = pl.program_id(0)
    valid = tv_ref[t] != 0
    d_model, two_f = w1s.shape
    n_blk = two_f // MXU_WIDTH
    half = MXU_WIDTH // 2

    def fetch(expert, slot):
        return (pltpu.make_async_copy(w1_hbm.at[layer, expert], w1buf.at[slot], sem.at[0, slot]),
                pltpu.make_async_copy(w2_hbm.at[layer, expert], w2buf.at[slot], sem.at[1, slot]))

    @pl.when(t == 0)
    def _():
        for cp in fetch(te_ref[0], 0):
            cp.start()

    @pl.when(first_ref[t] == 1)
    def _():
        slot = slot_ref[t]
        for cp in fetch(te_ref[t], slot):
            cp.wait()

        @pl.when(next_ref[t] >= 0)
        def _():
            for cp in fetch(next_ref[t], 1 - slot):
                cp.start()

        for blk in range(n_blk):
            sl = slice(blk * MXU_WIDTH, (blk + 1) * MXU_WIDTH)
            wb = w1buf[slot, :, sl].astype(BF16)
            w1s[:, sl] = jnp.dot(wb, perm_ref[...], preferred_element_type=F32).astype(BF16)
        w2s[...] = w2buf[slot].astype(BF16)

    @pl.when(valid)
    def _():
        x = x_ref[...]
        for blk in range(n_blk):
            sl = slice(blk * MXU_WIDTH, (blk + 1) * MXU_WIDTH)
            u = jnp.dot(x, w1s[:, sl], preferred_element_type=F32) + b1_ref[0, 0, :, sl]
            glu = jnp.minimum(u[:, :half], SWIGLU_LIMIT)
            lin = jnp.clip(u[:, half:], -SWIGLU_LIMIT, SWIGLU_LIMIT)
            hs[:, blk * half:(blk + 1) * half] = (glu * _sigmoid(SWIGLU_ALPHA * glu) * (lin + 1.0)).astype(BF16)
        y = jnp.dot(hs[...], w2s[...], preferred_element_type=F32) + b2_ref[0, 0]
        y_ref[...] = y.astype(y_ref.dtype)

    @pl.when(jnp.logical_not(valid))
    def _():
        y_ref[...] = jnp.zeros_like(y_ref)


def _deinterleave_perm():
    half = MXU_WIDTH // 2
    src = np.arange(MXU_WIDTH)
    dst = np.where(src % 2 == 0, src // 2, half + src // 2)
    p = np.zeros((MXU_WIDTH, MXU_WIDTH), np.float32)
    p[src, dst] = 1.0
    return jnp.asarray(p, BF16)


def _moe_dispatch(top_i):
    n, k = top_i.shape
    tm = MOE_ROW_TILE
    p_rows = n * k + N_EXPERTS * tm
    experts = jnp.arange(N_EXPERTS, dtype=jnp.int32)
    onehot = top_i[:, :, None] == experts
    sel = jnp.sum(onehot.astype(jnp.int32), axis=1)
    before = jnp.cumsum(sel, axis=0) - sel
    counts = jnp.sum(sel, axis=0)
    padded = ((counts + tm - 1) // tm) * tm
    ends = jnp.cumsum(padded)
    starts = ends - padded
    pos = jnp.sum(jnp.where(onehot, (before + starts)[:, None, :], 0), axis=-1)
    n_tiles = p_rows // tm
    tile_start = jnp.arange(n_tiles, dtype=jnp.int32) * tm
    tile_valid = (tile_start < ends[-1]).astype(jnp.int32)
    last_tile = ends[-1] // tm - 1
    tile_expert = jnp.sum(ends[None, :] <= jnp.minimum(tile_start, last_tile * tm)[:, None], axis=1).astype(jnp.int32)
    keys = jnp.sort((top_i * n + jnp.arange(n, dtype=jnp.int32)[:, None]).reshape(-1))
    tile_onehot = tile_expert[:, None] == experts[None, :]
    lookup = lambda table: jnp.sum(jnp.where(tile_onehot, table[None, :], 0), axis=1)
    tile_rank0 = tile_start - lookup(starts)
    rank = tile_rank0[:, None] + jnp.arange(tm, dtype=jnp.int32)[None, :]
    sorted_at = jnp.clip(lookup(jnp.cumsum(counts) - counts)[:, None] + rank, 0, n * k - 1)
    tile_keys = keys[sorted_at.reshape(-1)].reshape(n_tiles, tm)
    filler = (tile_start[:, None] + jnp.arange(tm, dtype=jnp.int32)[None, :]) % n
    src_tok = jnp.where(rank < lookup(counts)[:, None], tile_keys % n, filler)
    nst = n_tiles // MOE_SPLITS
    idx = jnp.arange(nst, dtype=jnp.int32)
    tables = []
    for h in range(MOE_SPLITS):
        te_h = tile_expert[h * nst:(h + 1) * nst]
        is_first = jnp.concatenate([jnp.ones((1,), jnp.int32), (te_h[1:] != te_h[:-1]).astype(jnp.int32)])
        slot = (jnp.cumsum(is_first) - 1) % 2
        nxt = jnp.min(jnp.where(jnp.logical_and(idx[None, :] > idx[:, None], is_first[None, :] == 1),
                                idx[None, :], nst), axis=1)
        next_expert = jnp.sum(jnp.where(idx[None, :] == nxt[:, None], te_h[None, :] + 1, 0), axis=1) - 1
        tables.append((te_h, tile_valid[h * nst:(h + 1) * nst], is_first, slot.astype(jnp.int32),
                       next_expert.astype(jnp.int32)))
    return pos, src_tok.reshape(-1), tables, p_rows


def moe_experts(h2, top_i, layer, w1, b1, w2, b2):
    n = top_i.shape[0]
    d = h2.shape[1]
    depth, e, _, two_f = w1.shape
    f = two_f // 2
    tm = MOE_ROW_TILE
    pos, src_tok, tables, p_rows = _moe_dispatch(top_i)
    assert h2.shape[0] == p_rows
    src_tok = lax.optimization_barrier(src_tok)
    b1p = b1.reshape(depth, e, two_f // MXU_WIDTH, MXU_WIDTH // 2, 2).swapaxes(3, 4).reshape(depth, e, 1, two_f)
    b2r = b2.reshape(depth, e, 1, d)
    expert_vec = lambda w: pl.BlockSpec((1, 1, 1, w), lambda t, te, *_: (layer, te[t], 0, 0))
    rows = p_rows // MOE_SPLITS
    nst = rows // tm
    xs = [h2.at[src_tok[h * rows:(h + 1) * rows]].get(mode="promise_in_bounds") for h in range(MOE_SPLITS)]
    ys = None
    for h in range(MOE_SPLITS):
        in_specs = [
            pl.BlockSpec((tm, d), lambda t, *_: (t, 0)),
            pl.BlockSpec(memory_space=pl.ANY),
            expert_vec(two_f),
            pl.BlockSpec(memory_space=pl.ANY),
            expert_vec(d),
            pl.BlockSpec((MXU_WIDTH, MXU_WIDTH), lambda t, *_: (0, 0)),
        ]
        args = [*tables[h], xs[h], w1, b1p, w2, b2r, _deinterleave_perm()]
        aliases = {}
        if ys is not None:
            in_specs.append(pl.BlockSpec(memory_space=pl.ANY))
            aliases = {len(args): 0}
            args.append(ys)
        ys = pl.pallas_call(
            functools.partial(_moe_kernel, layer=layer),
            grid_spec=pltpu.PrefetchScalarGridSpec(
                num_scalar_prefetch=len(tables[h]), grid=(nst,), in_specs=in_specs,
                out_specs=pl.BlockSpec((tm, d), lambda t, *_, h=h: (h * nst + t, 0)),
                scratch_shapes=[pltpu.VMEM((2, d, two_f), F32), pltpu.VMEM((2, f, d), F32),
                                pltpu.SemaphoreType.DMA((2, 2)),
                                pltpu.VMEM((d, two_f), BF16), pltpu.VMEM((f, d), BF16), pltpu.VMEM((tm, f), BF16)]),
            out_shape=jax.ShapeDtypeStruct((p_rows, d), BF16),
            input_output_aliases=aliases,
            compiler_params=pltpu.CompilerParams(dimension_semantics=("arbitrary",),
                                                 vmem_limit_bytes=48 * 1024 * 1024),
            name="moe_experts",
        )(*args)
    return ys.at[lax.optimization_barrier(pos.T.reshape(-1))].get(mode="promise_in_bounds").reshape(TOP_K, n, d)


def _final_kernel(x1_ref, mod_ref, ys_ref, topw_ref, lng_ref, lnb_ref, o_ref):
    g2 = mod_ref[0, 5:6]
    moe = None
    for kth in range(TOP_K):
        term = ys_ref[kth].astype(F32) * topw_ref[:, kth:kth + 1]
        moe = term if moe is None else moe + term
    o_ref[...] = _layer_norm(ALPHA * x1_ref[...] + g2 * moe, lng_ref[...], lnb_ref[...])


def combine_and_norm(x1, mod, ys, top_w, layer, sp):
    tm = ROW_TILE
    ln_g, ln_b = sp['ln_g'], sp['ln_b']
    return pl.pallas_call(
        _final_kernel,
        grid=(N_TILES,),
        in_specs=[pl.BlockSpec((tm, D_MODEL), lambda t: (t, 0)),
                  _mod_spec(layer),
                  pl.BlockSpec((TOP_K, tm, D_MODEL), lambda t: (0, t, 0)),
                  pl.BlockSpec((tm, 128), lambda t: (t, 0)),
                  _layer_item(ln_g, 2 * layer + 1), _layer_item(ln_b, 2 * layer + 1)],
        out_specs=pl.BlockSpec((tm, D_MODEL), lambda t: (t, 0)),
        out_shape=jax.ShapeDtypeStruct((N_TOK, D_MODEL), F32),
        compiler_params=pltpu.CompilerParams(dimension_semantics=("parallel",)),
        name="combine_and_norm",
    )(x1, mod, ys, top_w, ln_g, ln_b)


def kernel(x_prompt, x_sample, cache_a_k, cache_a_v, cache_b_ckv, cache_b_kpe, state_c, state_d, c,
           c_ctx, w_mod, b_mod, w_in, a_sink, b_q_norm, b_w_uq, b_kv_norm, b_w_ukv, c_w_gate, c_b_gate,
           c_norm, d_mu, d_w0, d_w2, d_a0, d_a2, d_g2, d_k_k, d_k_a, d_r_k, d_ln_g, d_ln_b, w_br, w_out,
           ln_g, ln_b, w_router, b_router, w_mlp1, b_mlp1, w_mlp2, b_mlp2):
    sp = stacked_params(dict(
        a_sink=a_sink, b_q_norm=b_q_norm, b_w_uq=b_w_uq, b_kv_norm=b_kv_norm, b_w_ukv=b_w_ukv, c_w_gate=c_w_gate,
        c_b_gate=c_b_gate, c_norm=c_norm, d_mu=d_mu, d_w0=d_w0, d_w2=d_w2, d_a0=d_a0, d_a2=d_a2, d_g2=d_g2,
        d_k_k=d_k_k, d_k_a=d_k_a, d_r_k=d_r_k, d_ln_g=d_ln_g, d_ln_b=d_ln_b, ln_g=ln_g, ln_b=ln_b,
        w_router=w_router, b_router=b_router))
    assert x_prompt.shape == (N_CTX_SEQ, CTX_LEN, D_MODEL) and x_sample.shape == (N_LAT_SEQ, LAT_LEN, D_MODEL)
    x = jnp.concatenate([x_prompt.reshape(N_CTX, D_MODEL), x_sample.reshape(-1, D_MODEL)], axis=0)
    cond8 = jnp.concatenate([c_ctx[None], c, jnp.zeros((8 - 1 - N_LAT_SEQ, D_MODEL), F32)], axis=0)
    mod = modulation_table(cond8, w_mod, b_mod)[:, :1 + N_LAT_SEQ].reshape(DEPTH * (1 + N_LAT_SEQ), 6, D_MODEL)
    tables = _rope_tables()
    w_small, w_g = prepare_in_weights(w_in)
    w_br_bf, w_out_bf = w_br.astype(BF16), w_out.astype(BF16)
    new = {name: [] for name in ("a_k", "a_v", "b_ckv", "b_kpe", "c", "d")}
    for l in range(DEPTH):
        (aq, ak, av, bq, bckv, bkpe, cq4, ck4, cla_f, cla_b, cv, cgate,
         r, v, kk, lw_f, lw_b, k_f, k_b, a_f, a_b, bonus, dgate) = mixer_prelude(x, mod, w_small, l, tables, sp)

        o_a = gqa_attention(aq, ak, av, sp['a_sink'], cache_a_k, cache_a_v, l)
        o_b = mla_attention(bq, bckv, bkpe, sp['b_w_ukv'], cache_b_ckv, cache_b_kpe, l)

        c_s0 = jnp.swapaxes(state_c[:, l], 3, 4).reshape(1, N_CHAIN, C_DV, C_DK)
        co_f, co_b, c_fin = _recurrence_calls(functools.partial(_gla_kernel, dot=_dot1), "gla", [cq4, ck4, cv],
                                              [cla_f], [cla_b], c_s0, (C_DV, C_DK), C_HEADS * C_DV)
        d_s0 = state_d[:, l].reshape(1, N_CHAIN, D_N, D_N)
        y_f, y_b, d_fin = _recurrence_calls(functools.partial(_rwkv_kernel, dot=_dot1), "rwkv7", [r, v, kk],
                                            [lw_f, k_f, a_f], [lw_b, k_b, a_b], d_s0, (D_N, D_N), D_HEADS * D_N)

        x1, h2, top_i, top_w = merge_and_route(x, mod, o_a, o_b, co_f, co_b, cgate, y_f, y_b, bonus, dgate,
                                               w_g, w_br_bf, w_out_bf, l, sp)
        ys = moe_experts(h2, top_i[:, :TOP_K], l, w_mlp1, b_mlp1, w_mlp2, b_mlp2)
        x = combine_and_norm(x1, mod, ys, top_w, l, sp)

        new["a_k"].append(ak[:N_CTX].reshape(N_CTX_SEQ, CTX_LEN, A_KV_HEADS, A_HD).transpose(0, 2, 1, 3))
        new["a_v"].append(av[:N_CTX].reshape(N_CTX_SEQ, CTX_LEN, A_KV_HEADS, A_HD).transpose(0, 2, 1, 3))
        new["b_ckv"].append(bckv[:N_CTX].reshape(N_CTX_SEQ, CTX_LEN, B_KV_LORA))
        new["b_kpe"].append(bkpe[:N_CTX].reshape(N_CTX_SEQ, CTX_LEN, B_ROPE))
        new["c"].append(jnp.swapaxes(c_fin.reshape(N_CTX_SEQ, 2, C_HEADS, C_DV, C_DK), 3, 4))
        new["d"].append(d_fin.reshape(N_CTX_SEQ, 2, D_HEADS, D_N, D_N))
    y_prompt = x[:N_CTX].reshape(x_prompt.shape)
    y_sample = x[N_CTX:].reshape(x_sample.shape)
    return (y_prompt, y_sample, *(jnp.stack(new[name], axis=1) for name in ("a_k", "a_v", "b_ckv", "b_kpe", "c", "d")))
```

```python
import functools

import jax
import jax.numpy as jnp
import numpy as np
from jax import lax
from jax.experimental import pallas as pl
from jax.experimental.pallas import tpu as pltpu

F32 = jnp.float32
BF16 = jnp.bfloat16

MXU_WIDTH = 256
VMEM_LIMIT = 56 * 1024 * 1024

D_MODEL = 1024
DEPTH = 2
GRID_W = 64
ROPE_BASE = 10000.0
A_HEADS, A_KV_HEADS, A_HD = 4, 2, 64
B_HEADS, B_NOPE, B_ROPE, B_VD, B_Q_LORA, B_KV_LORA = 4, 64, 32, 64, 192, 128
C_HEADS, C_DK, C_DV, C_GATE_RANK, C_GATE_TEMP = 4, 32, 64, 16, 16.0
D_HEADS, D_N, D_DECAY_RANK, D_AAA_RANK, D_GATE_RANK, D_GN_EPS = 4, 64, 64, 64, 128, 64e-5
BRANCH_W = 256
N_BRANCH = 4
N_EXPERTS = 32
TOP_K = 4
SWIGLU_LIMIT = 7.0
SWIGLU_ALPHA = 1.702
ALPHA = (2 * DEPTH) ** 0.25
LN_EPS = 1e-5
RMS_EPS = 1e-6

N_CTX_SEQ, CTX_LEN = 16, 256
N_LAT_SEQ, LAT_LEN = 2, 2048
N_CTX = N_CTX_SEQ * CTX_LEN
N_TOK = N_CTX + N_LAT_SEQ * LAT_LEN
ROW_TILE = 256
N_TILES = N_TOK // ROW_TILE
CTX_TILES = N_CTX // ROW_TILE
LAT_TILES_PER_SEQ = LAT_LEN // ROW_TILE
N_SEQ = N_CTX_SEQ + N_LAT_SEQ
MLA_HEAD_LANES = 128
MOE_ROW_TILE = 256
MOE_SPLITS = 2
MOE_ROWS = N_TOK * TOP_K + N_EXPERTS * MOE_ROW_TILE

_ORIG = dict(aq=(0, 256), ak=(256, 384), av=(384, 512), bcq=(512, 704), bckv=(704, 832), bkpe=(832, 864),
             cq=(864, 992), ck=(992, 1120), cv=(1120, 1376), cog=(1376, 1632), caf=(1632, 1648), cab=(1648, 1664),
             zd=(1664, 2816))
_ORDER = ("aq", "ak", "av", "cq", "ck", "cv", "cog", "zd", "bcq", "caf", "cab", "bkpe", "bckv")
COL = {}
_off = 0
for _name in _ORDER:
    _w = _ORIG[_name][1] - _ORIG[_name][0]
    COL[_name] = (_off, _off + _w)
    _off += _w
SMALL_COLS = _off
G_START = 2816


def _cs(name):
    return slice(*COL[name])


def _split3(x):
    hi = x.astype(BF16)
    r1 = x - hi.astype(F32)
    mid = r1.astype(BF16)
    lo = (r1 - mid.astype(F32)).astype(BF16)
    return hi, mid, lo


def _split2(x):
    hi = x.astype(BF16)
    lo = (x - hi.astype(F32)).astype(BF16)
    return hi, lo


def _bdot(a, b, dims):
    return lax.dot_general(a, b, dims, preferred_element_type=F32)


_D2 = (((1,), (0,)), ((), ()))
_D2T = (((1,), (1,)), ((), ()))
_NN = (((2,), (1,)), ((0,), (0,)))
_NT = (((2,), (2,)), ((0,), (0,)))
_TN = (((1,), (1,)), ((0,), (0,)))


def _dot1(a, b, dims=_D2):
    return _bdot(a.astype(BF16), b.astype(BF16), dims)


def _dot3(a, b, dims=_D2):
    ah, al = _split2(a)
    bh, bl = _split2(b)
    return _bdot(ah, bh, dims) + (_bdot(ah, bl, dims) + _bdot(al, bh, dims))


def _dot_exact_lhs(a01, b, dims=_D2):
    a = a01.astype(BF16)
    h, m, l = _split3(b)
    return _bdot(a, h, dims) + (_bdot(a, m, dims) + _bdot(a, l, dims))


def _dot_exact_rhs(a, b01, dims=_D2):
    b = b01.astype(BF16)
    h, m, l = _split3(a)
    return _bdot(h, b, dims) + (_bdot(m, b, dims) + _bdot(l, b, dims))


def _dot6(a, b, dims=_D2):
    ah, am, al = _split3(a)
    bh, bm, bl = _split3(b)
    return (_bdot(ah, bh, dims) + (_bdot(ah, bm, dims) + _bdot(am, bh, dims))
            + (_bdot(am, bm, dims) + (_bdot(ah, bl, dims) + _bdot(al, bh, dims))))


def _sigmoid(x):
    return 0.5 * jnp.tanh(0.5 * x) + 0.5


def _softplus(x):
    return jnp.maximum(x, 0.0) + jnp.log(1.0 + jnp.exp(-jnp.abs(x)))


def _mod_row(t):
    return jnp.where(t < CTX_TILES, 0, 1 + (t - CTX_TILES) // LAT_TILES_PER_SEQ)


def _full(shape):
    nd = len(shape)
    return pl.BlockSpec(shape, lambda *_: (0,) * nd)


MOD_COL_TILE = 1536


def _mod_kernel(c_ref, w_ref, b_ref, o_ref):
    c = c_ref[...]
    o_ref[0] = _dot3(c * _sigmoid(c), w_ref[0]) + b_ref[0]


def modulation_table(cond8, w_mod, b_mod):
    depth, d, six_d = w_mod.shape
    return pl.pallas_call(
        _mod_kernel,
        grid=(depth, six_d // MOD_COL_TILE),
        in_specs=[pl.BlockSpec((8, d), lambda l, j: (0, 0)),
                  pl.BlockSpec((1, d, MOD_COL_TILE), lambda l, j: (l, 0, j)),
                  pl.BlockSpec((1, 1, MOD_COL_TILE), lambda l, j: (l, 0, j))],
        out_specs=pl.BlockSpec((1, 8, MOD_COL_TILE), lambda l, j: (l, 0, j)),
        out_shape=jax.ShapeDtypeStruct((depth, 8, six_d), F32),
        compiler_params=pltpu.CompilerParams(dimension_semantics=("parallel", "parallel")),
        name="modulation",
    )(cond8, w_mod, b_mod.reshape(depth, 1, six_d))


WPREP_ROWS = 128


def _wprep_kernel(w_ref, small_ref, gate_ref):
    for name in _ORDER:
        lo, hi = _ORIG[name]
        small_ref[0, :, _cs(name)] = w_ref[0, :, lo:hi].astype(BF16)
    gate_ref[0] = w_ref[0, :, G_START:].astype(BF16)


def prepare_in_weights(w_in):
    depth, d, cols = w_in.shape
    return pl.pallas_call(
        _wprep_kernel,
        grid=(depth, d // WPREP_ROWS),
        in_specs=[pl.BlockSpec((1, WPREP_ROWS, cols), lambda l, r: (l, r, 0))],
        out_specs=[pl.BlockSpec((1, WPREP_ROWS, SMALL_COLS), lambda l, r: (l, r, 0)),
                   pl.BlockSpec((1, WPREP_ROWS, cols - G_START), lambda l, r: (l, r, 0))],
        out_shape=[jax.ShapeDtypeStruct((depth, d, SMALL_COLS), BF16),
                   jax.ShapeDtypeStruct((depth, d, cols - G_START), BF16)],
        compiler_params=pltpu.CompilerParams(dimension_semantics=("parallel", "parallel")),
        name="prepare_in_weights",
    )(w_in)


def _rot_pairs(x, half, lane_mod_base=0):
    w = x.shape[-1]
    lane = lax.broadcasted_iota(jnp.int32, (1, w), 1) - lane_mod_base
    first = (lane % (2 * half)) < half
    return jnp.where(first, -pltpu.roll(x, w - half, axis=1), pltpu.roll(x, half, axis=1))


def _pre_kernel(x_ref, xp_ref, xn_ref, mod_ref, w_ref, ca_ref, sa_ref, cb_ref, sb_ref, ck_ref, sk_ref,
                qnorm_ref, kvnorm_ref, wuq_ref, cwg_ref, cbg_ref, rep_ref, mu_ref, dw0_ref, dw2_ref, da0_ref,
                da2_ref, dg2_ref, dkk_ref, dka_ref, drk_ref, bd_ref, tri_ref,
                aq_o, ak_o, av_o, bq_o, bckv_o, bkpe_o, cq4_o, ck4_o, claf_o, clab_o, cv_o, cgate_o,
                r_o, v_o, kk_o, lwf_o, lwb_o, kf_o, kb_o, af_o, ab_o, bonus_o, dgate_o, cif_o, cib_o):
    t = pl.program_id(0)
    tm = x_ref.shape[0]
    sh1 = mod_ref[0, 0:1, :]
    sc1 = mod_ref[0, 1:2, :]

    def modulate(xv):
        return (xv * (1.0 + sc1) + sh1).astype(BF16)

    h_all = jnp.concatenate([modulate(x_ref[...]), modulate(xp_ref[...]), modulate(xn_ref[...])], axis=0)
    z_all = jnp.dot(h_all, w_ref[0], preferred_element_type=F32)
    z = z_all[:tm]

    aq = z[:, _cs("aq")]
    ak = z[:, _cs("ak")]
    aq_o[...] = aq * ca_ref[...] + _rot_pairs(aq, A_HD // 4) * sa_ref[...]
    ak_o[...] = ak * ca_ref[:, :ak.shape[1]] + _rot_pairs(ak, A_HD // 4) * sa_ref[:, :ak.shape[1]]
    av_o[...] = z[:, _cs("av")]

    bcq = z[:, _cs("bcq")]
    qn = bcq * lax.rsqrt(jnp.mean(bcq * bcq, axis=-1, keepdims=True) + RMS_EPS) * qnorm_ref[...]
    bq = _dot1(qn, wuq_ref[...])
    bq_o[...] = bq * cb_ref[...] + _rot_pairs(bq, B_ROPE // 4, lane_mod_base=B_NOPE) * sb_ref[...]
    bckv = z[:, _cs("bckv")]
    bckv_o[...] = bckv * lax.rsqrt(jnp.mean(bckv * bckv, axis=-1, keepdims=True) + RMS_EPS) * kvnorm_ref[...]
    kpe_lo = COL["bkpe"][0] // 128 * 128
    kblk = z[:, kpe_lo:kpe_lo + 128]
    kblk = kblk * ck_ref[...] + _rot_pairs(kblk, B_ROPE // 4) * sk_ref[...]
    bkpe_o[...] = kblk[:, COL["bkpe"][0] - kpe_lo:COL["bkpe"][1] - kpe_lo]

    rep = rep_ref[...]
    cq4_o[...] = _dot1(z[:, _cs("cq")] * (C_DK ** -0.5), rep)
    ck4_o[...] = _dot1(z[:, _cs("ck")], rep)
    cv_o[...] = z[:, _cs("cv")]
    cog = z[:, _cs("cog")]
    cgate_o[...] = cog * _sigmoid(cog)
    for direction, (name, out) in enumerate((("caf", claf_o), ("cab", clab_o))):
        pre = _dot3(z[:, _cs(name)], cwg_ref[direction]) + cbg_ref[direction]
        c_dir = _dot_exact_lhs(tri_ref[direction], -_softplus(-pre) * (1.0 / C_GATE_TEMP))
        out[...] = _dot_exact_rhs(c_dir, rep)

    zd_cols = _cs("zd")
    zd = z[:, zd_cols]
    j = (t - CTX_TILES) % LAT_TILES_PER_SEQ
    latent = t >= CTX_TILES
    has_prev = jnp.logical_and(latent, j != 0)
    has_next = jnp.logical_and(latent, j != LAT_TILES_PER_SEQ - 1)
    prev_row = jnp.where(has_prev, z_all[tm + 7:tm + 8, zd_cols], 0.0)
    next_row = jnp.where(has_next, z_all[tm + 8:tm + 9, zd_cols], 0.0)
    row = lax.broadcasted_iota(jnp.int32, (tm, 1), 0)
    up = jnp.where(row == 0, prev_row, pltpu.roll(zd, 1, axis=0))
    dn = jnp.where(row == tm - 1, next_row, pltpu.roll(zd, tm - 1, axis=0))
    zd = zd + (0.5 * (up + dn) - zd) * mu_ref[...]

    hn = D_HEADS * D_N
    d_r, d_k, d_v = zd[:, :hn], zd[:, hn:2 * hn], zd[:, 2 * hn:3 * hn]
    o = 3 * hn
    d_w = (zd[:, o:o + D_DECAY_RANK], zd[:, o + D_DECAY_RANK:o + 2 * D_DECAY_RANK])
    o += 2 * D_DECAY_RANK
    d_a = (zd[:, o:o + D_AAA_RANK], zd[:, o + D_AAA_RANK:o + 2 * D_AAA_RANK])
    o += 2 * D_AAA_RANK
    d_g = zd[:, o:o + D_GATE_RANK]
    bd = bd_ref[...]
    kk = d_k * dkk_ref[...]
    kk = kk / jnp.maximum(jnp.sqrt(_dot_exact_rhs(kk * kk, bd)), 1e-12)
    r_o[...] = d_r
    v_o[...] = d_v
    kk_o[...] = kk
    k_sum = None
    for direction, (lw_o, k_o, a_o, ci_o) in enumerate(((lwf_o, kf_o, af_o, cif_o), (lwb_o, kb_o, ab_o, cib_o))):
        w_log = -_softplus(-(dw0_ref[direction] + _dot3(jnp.tanh(d_w[direction]), dw2_ref[direction]))) - 0.5
        lw = -jnp.exp(w_log)
        lw_o[...] = lw
        ci_o[...] = _dot_exact_lhs(tri_ref[direction], lw)
        a = _sigmoid(da0_ref[direction] + _dot1(d_a[direction], da2_ref[direction]))
        k_dir = d_k * (1.0 + (a - 1.0) * dka_ref[...])
        k_o[...] = k_dir
        a_o[...] = a
        k_sum = k_dir if k_sum is None else k_sum + k_dir
    bonus_o[...] = d_v * _dot_exact_rhs(d_r * drk_ref[...] * k_sum, bd)
    dgate_o[...] = _dot1(_sigmoid(d_g), dg2_ref[...])


def _rope_tables():
    pos = np.arange(LAT_LEN)
    rowp, colp = (pos // GRID_W).astype(np.float32), (pos % GRID_W).astype(np.float32)

    f32 = np.float32

    def head_tables(rot_dim):
        quarter = rot_dim // 4
        inv = (f32(ROPE_BASE) ** (-np.arange(quarter, dtype=f32) / f32(quarter))).astype(f32)
        ar = (rowp[:, None] * inv).astype(f32)
        ac = (colp[:, None] * inv).astype(f32)
        cos = np.concatenate([np.cos(ar), np.cos(ar), np.cos(ac), np.cos(ac)], axis=-1).astype(f32)
        sin = np.concatenate([np.sin(ar), np.sin(ar), np.sin(ac), np.sin(ac)], axis=-1).astype(f32)
        return cos, sin

    def with_identity(c, s):
        w = c.shape[1]
        return (jnp.asarray(np.concatenate([np.ones((ROW_TILE, w), f32), c], axis=0)),
                jnp.asarray(np.concatenate([np.zeros((ROW_TILE, w), f32), s], axis=0)))

    ca, sa = head_tables(A_HD)
    ca, sa = with_identity(np.tile(ca, (1, A_HEADS)), np.tile(sa, (1, A_HEADS)))
    cbh, sbh = head_tables(B_ROPE)
    ones, zeros = np.ones((LAT_LEN, B_NOPE), f32), np.zeros((LAT_LEN, B_NOPE), f32)
    qpad = MLA_HEAD_LANES - B_NOPE - B_ROPE
    cb, sb = with_identity(
        np.tile(np.concatenate([ones, cbh, np.ones((LAT_LEN, qpad), f32)], axis=1), (1, B_HEADS)),
        np.tile(np.concatenate([zeros, sbh, np.zeros((LAT_LEN, qpad), f32)], axis=1), (1, B_HEADS)))
    pad = 128 - B_ROPE
    ck, sk = with_identity(np.concatenate([np.ones((LAT_LEN, pad), f32), cbh], axis=1),
                           np.concatenate([np.zeros((LAT_LEN, pad), f32), sbh], axis=1))
    return ca, sa, cb, sb, ck, sk


def _lane_repeat_matrix():
    m = np.zeros((C_HEADS * C_DK, C_HEADS * 128), np.float32)
    for h in range(C_HEADS):
        for g in range(128 // C_DK):
            for d in range(C_DK):
                m[h * C_DK + d, h * 128 + g * C_DK + d] = 1.0
    return jnp.asarray(m, BF16)


def _head_block_diag():
    m = np.kron(np.eye(D_HEADS, dtype=np.float32), np.ones((D_N, D_N), np.float32))
    return jnp.asarray(m, BF16)


PRE_OUT_WIDTHS = (256, 128, 128, B_HEADS * MLA_HEAD_LANES, 128, 32, 512, 512, 512, 512, 256, 256) + (256,) * 13


def _chunk_cumsum_matrices():
    i = np.arange(ROW_TILE)[:, None]
    j = np.arange(ROW_TILE)[None, :]
    same = (i // CHUNK) == (j // CHUNK)
    return jnp.asarray(np.stack([same & (j <= i), same & (j >= i)]).astype(np.float32), BF16)


def _layer_block(arr, layer):
    nd = arr.ndim
    return pl.BlockSpec((1,) + arr.shape[1:], lambda *_: (layer,) + (0,) * (nd - 1))


def _layer_item(arr, index):
    nd = arr.ndim
    return pl.BlockSpec((None,) + arr.shape[1:], lambda *_: (index,) + (0,) * (nd - 1))


def _mod_spec(layer):
    return pl.BlockSpec((1, 6, D_MODEL), lambda t: (layer * (1 + N_LAT_SEQ) + _mod_row(t), 0, 0))


def stacked_params(p):
    depth = p['d_mu'].shape[0]
    hn = D_HEADS * D_N
    w_uq = p['b_w_uq'].reshape(depth, B_Q_LORA, B_HEADS, B_NOPE + B_ROPE)
    w_uq = jnp.pad(w_uq, ((0, 0), (0, 0), (0, 0), (0, MLA_HEAD_LANES - B_NOPE - B_ROPE)))
    row = lambda a: a.reshape(depth, 1, -1)
    return dict(
        b_q_norm=row(p['b_q_norm']), b_kv_norm=row(p['b_kv_norm']), w_uq=w_uq.reshape(depth, B_Q_LORA, -1),
        c_w_gate=p['c_w_gate'], c_b_gate=p['c_b_gate'].reshape(depth, 2, 1, -1), d_mu=row(p['d_mu']),
        d_w0=p['d_w0'].reshape(depth, 2, 1, hn), d_w2=p['d_w2'], d_a0=p['d_a0'].reshape(depth, 2, 1, hn),
        d_a2=p['d_a2'], d_g2=p['d_g2'], d_k_k=row(p['d_k_k']), d_k_a=row(p['d_k_a']), d_r_k=row(p['d_r_k']),
        c_norm=row(jnp.tile(p['c_norm'], (1, C_HEADS))), d_ln_g=row(p['d_ln_g']), d_ln_b=row(p['d_ln_b']),
        ln_g=p['ln_g'].reshape(depth * 2, 1, -1), ln_b=p['ln_b'].reshape(depth * 2, 1, -1),
        w_router=p['w_router'], b_router=row(p['b_router']), a_sink=p['a_sink'].reshape(-1), b_w_ukv=p['b_w_ukv'])


def mixer_prelude(x, mod, w_small, layer, tables, sp):
    tm = ROW_TILE
    tab_idx = lambda t: (jnp.where(t < CTX_TILES, 0, 1 + (t - CTX_TILES) % LAT_TILES_PER_SEQ), 0)
    small = [sp[k] for k in ('b_q_norm', 'b_kv_norm', 'w_uq', 'c_w_gate', 'c_b_gate')] + [_lane_repeat_matrix()]
    small += [sp[k] for k in ('d_mu', 'd_w0', 'd_w2', 'd_a0', 'd_a2', 'd_g2', 'd_k_k', 'd_k_a', 'd_r_k')]
    small += [_head_block_diag(), _chunk_cumsum_matrices()]
    const = lambda a: _full(a.shape) if a.dtype == BF16 else _layer_item(a, layer)
    in_specs = ([pl.BlockSpec((tm, D_MODEL), lambda t: (t, 0)),
                 pl.BlockSpec((8, D_MODEL), lambda t: (jnp.maximum(t * (tm // 8) - 1, 0), 0)),
                 pl.BlockSpec((8, D_MODEL), lambda t: (jnp.minimum((t + 1) * (tm // 8), N_TOK // 8 - 1), 0)),
                 _mod_spec(layer),
                 _layer_block(w_small, layer)]
                + [pl.BlockSpec((tm, tab.shape[1]), tab_idx) for tab in tables]
                + [const(a) for a in small])
    return pl.pallas_call(
        _pre_kernel,
        grid=(N_TILES,),
        in_specs=in_specs,
        out_specs=[pl.BlockSpec((tm, w), lambda t: (t, 0)) for w in PRE_OUT_WIDTHS],
        out_shape=[jax.ShapeDtypeStruct((N_TOK, w), F32) for w in PRE_OUT_WIDTHS],
        compiler_params=pltpu.CompilerParams(dimension_semantics=("parallel",), vmem_limit_bytes=VMEM_LIMIT),
        name="mixer_prelude",
    )(x, x, x, mod, w_small, *tables, *small)


ATT_Q_BLOCK = 128
MLA_Q_BLOCK = 256
ATT_WINDOW = 128
ATT_NEG_INF = -1e30
CACHE_LEN = 512


def _softmax_pv(s, v, sink):
    dv = v.shape[1] // 2
    m = jnp.max(s, axis=-1, keepdims=True)
    if sink is not None:
        m = jnp.maximum(m, sink)
    e = jnp.exp((s - m).astype(BF16))
    o = jnp.dot(e, v, preferred_element_type=F32)
    den = o[:, dv:dv + 1]
    if sink is not None:
        den = den + jnp.exp(sink - m)
    return o[:, :dv] / den


def _with_ones(v):
    return jnp.concatenate([v.astype(BF16), jnp.ones(v.shape, BF16)], axis=1)


def _gqa_kernel(sink_ref, q_ref, k_ref, v_ref, *rest, hd, group, scale, windowed, sink_base):
    if windowed:
        kp_ref, kn_ref, vp_ref, vn_ref, kc_ref, vc_ref, _, o_ref = rest
    else:
        (o_ref,) = rest
    i = pl.program_id(1)
    tq = q_ref.shape[0]
    n_kv = k_ref.shape[1] // hd
    if windowed:
        qpos = i * tq + lax.broadcasted_iota(jnp.int32, (tq, 3 * tq), 0)
        kpos = (i - 1) * tq + lax.broadcasted_iota(jnp.int32, (tq, 3 * tq), 1)
        n_tok = pl.num_programs(1) * tq
        mask = (jnp.abs(qpos - kpos) <= ATT_WINDOW) & (kpos >= 0) & (kpos < n_tok)
        mask = jnp.concatenate([mask] * group, axis=0)
    for kvh in range(n_kv):
        ks = slice(kvh * hd, (kvh + 1) * hd)
        qs = [q_ref[:, (kvh * group + g) * hd:(kvh * group + g + 1) * hd] for g in range(group)]
        q = (jnp.concatenate(qs, axis=0) * scale).astype(BF16)
        sink = jnp.concatenate(
            [jnp.full((tq, 1), sink_ref[sink_base + kvh * group + g], F32) for g in range(group)], axis=0)
        if windowed:
            k_win = jnp.concatenate([kp_ref[:, ks], k_ref[:, ks], kn_ref[:, ks]], axis=0)
            v_win = jnp.concatenate([vp_ref[:, ks], v_ref[:, ks], vn_ref[:, ks]], axis=0)
            s_win = _bdot(q, k_win.astype(BF16), _D2T)
            s_win = jnp.where(mask, s_win, ATT_NEG_INF)
            s_ctx = _bdot(q, kc_ref[0, 0, kvh].astype(BF16), _D2T)
            s = jnp.concatenate([s_win, s_ctx], axis=1)
            v = jnp.concatenate([v_win, vc_ref[0, 0, kvh]], axis=0)
        else:
            s = _bdot(q, k_ref[:, ks].astype(BF16), _D2T)
            v = v_ref[:, ks]
        o = _softmax_pv(s, _with_ones(v), sink)
        for g in range(group):
            h = kvh * group + g
            o_ref[:, h * hd:(h + 1) * hd] = o[g * tq:(g + 1) * tq]


def gqa_attention(q, k, v, sink, cache_k, cache_v, layer):
    qw, kw = q.shape[1], k.shape[1]
    group = qw // kw
    scale = A_HD ** -0.5
    params = pltpu.CompilerParams(dimension_semantics=("parallel", "parallel"))
    out_shape = jax.ShapeDtypeStruct((N_TOK, qw), F32)
    ctx_spec = lambda w: pl.BlockSpec((CTX_LEN, w), lambda s, i, sk: (s, 0))
    o = pl.pallas_call(
        functools.partial(_gqa_kernel, hd=A_HD, group=group, scale=scale, windowed=False, sink_base=layer * A_HEADS),
        grid_spec=pltpu.PrefetchScalarGridSpec(
            num_scalar_prefetch=1, grid=(N_CTX_SEQ, 1), in_specs=[ctx_spec(qw), ctx_spec(kw), ctx_spec(kw)],
            out_specs=ctx_spec(qw)),
        out_shape=out_shape, compiler_params=params, name="gqa_full",
    )(sink, q, k, v)
    tq = ATT_Q_BLOCK
    nb = LAT_LEN // tq
    base = N_CTX // tq
    blk = lambda w, f: pl.BlockSpec((tq, w), lambda b, i, sk: (base + nb * b + f(i), 0))
    same = lambda i: i
    prev = lambda i: jnp.maximum(i - 1, 0)
    nxt = lambda i: jnp.minimum(i + 1, nb - 1)
    cspec = pl.BlockSpec((1, 1) + cache_k.shape[2:], lambda b, i, sk: (b, layer, 0, 0, 0))
    return pl.pallas_call(
        functools.partial(_gqa_kernel, hd=A_HD, group=group, scale=scale, windowed=True, sink_base=layer * A_HEADS),
        grid_spec=pltpu.PrefetchScalarGridSpec(
            num_scalar_prefetch=1, grid=(N_LAT_SEQ, nb),
            in_specs=[blk(qw, same), blk(kw, same), blk(kw, same), blk(kw, prev), blk(kw, nxt), blk(kw, prev),
                      blk(kw, nxt), cspec, cspec, pl.BlockSpec(memory_space=pl.ANY)],
            out_specs=blk(qw, same)),
        out_shape=out_shape, input_output_aliases={10: 0}, compiler_params=params, name="gqa_windowed",
    )(sink, q, k, v, k, k, v, v, cache_k, cache_v, o)


def _mla_kernel(q_ref, ckv_ref, kpe_ref, wukv_ref, *rest, n_heads, nope, rope, vd, scale, cached):
    if cached:
        cckv_ref, ckpe_ref, _, o_ref, k_scr, vext_scr = rest
    else:
        o_ref, k_scr, vext_scr = rest
    i = pl.program_id(1)
    n_cache = k_scr.shape[0] - ckv_ref.shape[0]
    hw = nope + vd
    hl = MLA_HEAD_LANES

    @pl.when(i == 0)
    def _():
        w = wukv_ref[...].astype(BF16)

        def expand(rows, kpe_rows, lo, hi):
            kv = jnp.dot(rows.astype(BF16), w, preferred_element_type=F32).astype(BF16)
            n = hi - lo
            kpe = kpe_rows.astype(BF16)
            for h in range(n_heads):
                k_scr[lo:hi, hl * h:hl * (h + 1)] = jnp.concatenate(
                    [kv[:, h * hw:h * hw + nope], kpe, jnp.zeros((n, hl - nope - rope), BF16)], axis=1)
                vext_scr[lo:hi, 2 * vd * h:2 * vd * (h + 1)] = jnp.concatenate(
                    [kv[:, h * hw + nope:(h + 1) * hw], jnp.ones((n, vd), BF16)], axis=1)

        if cached:
            expand(cckv_ref[0, 0], ckpe_ref[0, 0], 0, n_cache)
        expand(ckv_ref[...], kpe_ref[...], n_cache, k_scr.shape[0])

    for h in range(n_heads):
        qh = (q_ref[:, hl * h:hl * (h + 1)] * scale).astype(BF16)
        s = _bdot(qh, k_scr[:, hl * h:hl * (h + 1)], _D2T)
        o_ref[:, h * vd:(h + 1) * vd] = _softmax_pv(s, vext_scr[:, 2 * vd * h:2 * vd * (h + 1)], None)


def mla_attention(q, ckv, kpe, w_ukv, cache_ckv, cache_kpe, layer):
    qw = q.shape[1]
    tq = MLA_Q_BLOCK
    kw = dict(n_heads=B_HEADS, nope=B_NOPE, rope=B_ROPE, vd=B_VD, scale=(B_NOPE + B_ROPE) ** -0.5)
    params = pltpu.CompilerParams(dimension_semantics=("parallel", "arbitrary"))
    out_shape = jax.ShapeDtypeStruct((N_TOK, B_HEADS * B_VD), F32)
    scratch = lambda rows: [pltpu.VMEM((rows, B_HEADS * MLA_HEAD_LANES), BF16),
                            pltpu.VMEM((rows, 2 * B_HEADS * B_VD), BF16)]
    nbc = CTX_LEN // tq
    o = pl.pallas_call(
        functools.partial(_mla_kernel, cached=False, **kw),
        grid=(N_CTX_SEQ, nbc),
        in_specs=[pl.BlockSpec((tq, qw), lambda s, i: (s * nbc + i, 0)),
                  pl.BlockSpec((CTX_LEN, B_KV_LORA), lambda s, i: (s, 0)),
                  pl.BlockSpec((CTX_LEN, B_ROPE), lambda s, i: (s, 0)),
                  _layer_item(w_ukv, layer)],
        out_specs=pl.BlockSpec((tq, B_HEADS * B_VD), lambda s, i: (s * nbc + i, 0)),
        out_shape=out_shape,
        scratch_shapes=scratch(CTX_LEN),
        compiler_params=params, name="mla_context",
    )(q, ckv, kpe, w_ukv)
    nb = LAT_LEN // tq
    base = N_CTX // tq
    lat0 = N_CTX // LAT_LEN
    s_len = CACHE_LEN + LAT_LEN
    return pl.pallas_call(
        functools.partial(_mla_kernel, cached=True, **kw),
        grid=(N_LAT_SEQ, nb),
        in_specs=[pl.BlockSpec((tq, qw), lambda b, i: (base + nb * b + i, 0)),
                  pl.BlockSpec((LAT_LEN, B_KV_LORA), lambda b, i: (lat0 + b, 0)),
                  pl.BlockSpec((LAT_LEN, B_ROPE), lambda b, i: (lat0 + b, 0)),
                  _layer_item(w_ukv, layer),
                  pl.BlockSpec((1, 1, CACHE_LEN, B_KV_LORA), lambda b, i: (b, layer, 0, 0)),
                  pl.BlockSpec((1, 1, CACHE_LEN, B_ROPE), lambda b, i: (b, layer, 0, 0)),
                  pl.BlockSpec(memory_space=pl.ANY)],
        out_specs=pl.BlockSpec((tq, B_HEADS * B_VD), lambda b, i: (base + nb * b + i, 0)),
        out_shape=out_shape, input_output_aliases={6: 0},
        scratch_shapes=scratch(s_len),
        compiler_params=params, name="mla_latent",
    )(q, ckv, kpe, w_ukv, cache_ckv, cache_kpe, o)


CHUNK = 64
GLA_SUB = 16
PAIR = 2
N_CHAIN = PAIR * 2 * 4


def _is_back(shape):
    return (lax.broadcasted_iota(jnp.int32, shape, 0) // 4) % 2 == 1


def _chains(ref_f, ref_b, width):
    return jnp.stack([ref[0, s, 0, :, h * width:(h + 1) * width]
                      for s in range(PAIR) for ref in (ref_f, ref_b) for h in range(4)], axis=0)


def _unchain(y, o_f, o_b):
    for s in range(PAIR):
        o_f[0, s, 0] = jnp.concatenate([y[s * 8 + h] for h in range(4)], axis=-1)
        o_b[0, s, 0] = jnp.concatenate([y[s * 8 + 4 + h] for h in range(4)], axis=-1)


def _dir_masks(L):
    shape = (N_CHAIN, L, L)
    back = _is_back(shape)
    row = lax.broadcasted_iota(jnp.int32, shape, 1)
    col = lax.broadcasted_iota(jnp.int32, shape, 2)
    ahead = jnp.where(back, col - row, row - col)
    return ahead >= 0, ahead > 0, row == col


def _chunk_end(ci):
    L = ci.shape[1]
    return jnp.where(_is_back((N_CHAIN, 1, 1)), ci[:, 0:1], ci[:, L - 1:L])


def _split_refs(refs, n_in, has_s0, has_sfin):
    ins = refs[:n_in]
    pos = n_in
    s0_ref = None
    if has_s0:
        s0_ref = refs[pos]
        pos += 3
    of_ref, ob_ref = refs[pos], refs[pos + 1]
    pos += 2
    sfin_ref = refs[pos] if has_sfin else None
    return ins, s0_ref, of_ref, ob_ref, sfin_ref, refs[-1]


def _init_state(s_scr, s0_ref):
    @pl.when(pl.program_id(1) == 0)
    def _():
        if s0_ref is None:
            s_scr[...] = jnp.zeros_like(s_scr)
        else:
            s_scr[...] = s0_ref[0]


def _emit_state(sfin_ref, s_new):
    if sfin_ref is None:
        return

    @pl.when(pl.program_id(1) == pl.num_programs(1) - 1)
    def _():
        sfin_ref[0] = s_new


def _rwkv_kernel(*refs, dot, has_s0, has_sfin):
    ((rf, rb, vf, vb, kkf, kkb, lwf, lwb, kf, kb, af, ab, cif, cib), s0_ref, yf_ref, yb_ref, sfin_ref,
     s_scr) = _split_refs(refs, 14, has_s0, has_sfin)
    _init_state(s_scr, s0_ref)
    n = D_N
    r = _chains(rf, rb, n)
    v = _chains(vf, vb, n)
    kk = _chains(kkf, kkb, n)
    lw = _chains(lwf, lwb, n)
    k = _chains(kf, kb, n)
    a = _chains(af, ab, n)
    L = r.shape[1]
    S = s_scr[...]
    incl, strict, diag = _dir_masks(L)
    ci = _chains(cif, cib, n)
    ce = ci - lw
    cl = _chunk_end(ci)
    e_neg = jnp.exp(-ci)
    b = a * kk
    alpha = kk * jnp.exp(ce)
    rho = r * jnp.exp(ci)
    beta = b * e_neg
    kappa = k * e_neg
    e_end = jnp.exp(cl - ci)
    ar = jnp.concatenate([alpha, rho], axis=1)
    bk = jnp.concatenate([beta, kappa], axis=1)
    w = dot(ar, bk, _NT)
    nmat = jnp.where(strict, w[:, :L, :L], 0.0)
    mmat = jnp.where(strict, w[:, :L, L:], 0.0)
    p1 = jnp.where(incl, w[:, L:, :L], 0.0)
    p2 = jnp.where(incl, w[:, L:, L:], 0.0)
    x = jnp.where(diag, 1.0, 0.0) - nmat
    p = dot(nmat, nmat, _NN)
    span = 2
    while True:
        x = x + dot(x, p, _NN)
        span *= 2
        if span >= L:
            break
        p = dot(p, p, _NN)
    us = dot(ar, S, _NT)
    rhs = us[:, :L] + dot(mmat, v, _NN)
    d = -dot(x, rhs, _NN)
    dv = jnp.concatenate([d, v], axis=1)
    pp = jnp.concatenate([p1, p2], axis=2)
    _unchain(us[:, L:] + dot(pp, dv, _NN), yf_ref, yb_ref)
    bk_end = jnp.concatenate([b * e_end, k * e_end], axis=1)
    s_new = S * jnp.exp(cl) + dot(dv, bk_end, _TN)
    s_scr[...] = s_new
    _emit_state(sfin_ref, s_new)


def _gla_kernel(*refs, dot, has_s0, has_sfin):
    (qf, qb, kf, kb, vf, vb, cf, cb), s0_ref, of_ref, ob_ref, sfin_ref, s_scr = _split_refs(
        refs, 8, has_s0, has_sfin)
    _init_state(s_scr, s0_ref)
    q4 = _chains(qf, qb, 128)
    k4 = _chains(kf, kb, 128)
    c = _chains(cf, cb, 128)
    v = _chains(vf, vb, C_DV)
    g, L, lanes = q4.shape
    dk = C_DK
    n_sub = L // GLA_SUB
    st = s_scr[...]
    incl, _, _ = _dir_masks(L)
    shape = (g, L, lanes)
    back = _is_back(shape)
    lane_blk = lax.broadcasted_iota(jnp.int32, shape, 2) // dk
    row_blk = lax.broadcasted_iota(jnp.int32, shape, 1) // GLA_SUB
    cref_f = jnp.zeros(shape, F32)
    cref_b = jnp.zeros(shape, F32)
    for j in range(1, n_sub):
        cref_f = jnp.where(lane_blk == j, c[:, j * GLA_SUB - 1:j * GLA_SUB], cref_f)
        cref_b = jnp.where(lane_blk == j - 1, c[:, j * GLA_SUB:j * GLA_SUB + 1], cref_b)
    cref = jnp.where(back, cref_b, cref_f)
    q_on = row_blk == lane_blk
    k_on = jnp.where(back, row_blk - lane_blk, lane_blk - row_blk) >= 0
    qh = jnp.where(q_on, q4 * jnp.exp(jnp.where(q_on, c - cref, 0.0)), 0.0)
    kh = jnp.where(k_on, k4 * jnp.exp(jnp.where(k_on, cref - c, 0.0)), 0.0)
    att = jnp.where(incl, dot(qh, kh, _NT), 0.0)
    cl = _chunk_end(c)
    qe = (q4 * jnp.exp(c))[:, :, :dk]
    ke = (k4 * jnp.exp(cl - c))[:, :, :dk]
    _unchain(dot(qe, st, _NT) + dot(att, v, _NN), of_ref, ob_ref)
    s_new = st * jnp.exp(cl[:, :, :dk]) + dot(v, ke, _TN)
    s_scr[...] = s_new
    _emit_state(sfin_ref, s_new)


def _recurrence_calls(kernel_fn, name, pairs, singles_f, singles_b, s0_lat, state_dims, out_width):
    def run(view, grid, group, s0, prev_out):
        nc = view[2]
        fwd_map = lambda p, c: (group(p), 0, c, 0, 0)
        bwd_map = lambda p, c: (group(p), 0, nc - 1 - c, 0, 0)
        blk = lambda w: (1, PAIR, 1, CHUNK, w)
        args, in_specs = [], []
        for af, ab in [(a, a) for a in pairs] + list(zip(singles_f, singles_b)):
            w = af.shape[-1]
            args += [af.reshape(view + (w,)), ab.reshape(view + (w,))]
            in_specs += [pl.BlockSpec(blk(w), fwd_map), pl.BlockSpec(blk(w), bwd_map)]
        out_specs = [pl.BlockSpec(blk(out_width), fwd_map), pl.BlockSpec(blk(out_width), bwd_map)]
        out_shape = [jax.ShapeDtypeStruct(view + (out_width,), F32)] * 2
        aliases = {}
        if s0 is not None:
            args += [s0] + [o.reshape(view + (out_width,)) for o in prev_out]
            in_specs += [_full(s0.shape), pl.BlockSpec(memory_space=pl.ANY), pl.BlockSpec(memory_space=pl.ANY)]
            aliases = {len(args) - 2: 0, len(args) - 1: 1}
        else:
            out_specs.append(pl.BlockSpec((1, N_CHAIN) + state_dims, lambda p, c: (p, 0, 0, 0)))
            out_shape.append(jax.ShapeDtypeStruct((grid[0], N_CHAIN) + state_dims, F32))
        return pl.pallas_call(
            functools.partial(kernel_fn, has_s0=s0 is not None, has_sfin=s0 is None),
            grid=grid, in_specs=in_specs, out_specs=out_specs, out_shape=out_shape,
            input_output_aliases=aliases, scratch_shapes=[pltpu.VMEM((N_CHAIN,) + state_dims, F32)],
            compiler_params=pltpu.CompilerParams(dimension_semantics=("parallel", "arbitrary")),
            name=name + ("_latent" if s0 is not None else "_context"),
        )(*args)

    ctx_nc = CTX_LEN // CHUNK
    ctx_view = (N_TOK // (PAIR * CTX_LEN), PAIR, ctx_nc, CHUNK)
    o_f, o_b, s_fin = run(ctx_view, (N_CTX_SEQ // PAIR, ctx_nc), lambda p: p, None, None)
    lat_nc = LAT_LEN // CHUNK
    lat_view = (N_TOK // (PAIR * LAT_LEN), PAIR, lat_nc, CHUNK)
    o_f, o_b = run(lat_view, (1, lat_nc), lambda p: N_CTX // (PAIR * LAT_LEN), s0_lat, (o_f, o_b))
    return o_f.reshape(N_TOK, out_width), o_b.reshape(N_TOK, out_width), s_fin


def _layer_norm(x, g, b):
    mu = jnp.mean(x, axis=-1, keepdims=True)
    xc = x - mu
    var = jnp.mean(xc * xc, axis=-1, keepdims=True)
    return xc * lax.rsqrt(var + LN_EPS) * g + b


def _merge_kernel(x_ref, mod_ref, oa_ref, ob_ref, cof_ref, cob_ref, cgate_ref, yf_ref, yb_ref, bonus_ref, dgate_ref,
                  wg_ref, wbr_ref, wout_ref, cnorm_ref, dlng_ref, dlnb_ref, lng_ref, lnb_ref, wr_ref, br_ref, bd_ref,
                  x1_o, h2_o, topi_o, topw_o):
    x = x_ref[...]
    m = mod_ref[0]
    sh1, sc1, g1, sh2, sc2 = m[0:1], m[1:2], m[2:3], m[3:4], m[4:5]
    bd = bd_ref[...]
    inv_n = 1.0 / D_N
    co = cof_ref[...] + cob_ref[...]
    o_c = co * lax.rsqrt(_dot_exact_rhs(co * co, bd) * inv_n + RMS_EPS) * cnorm_ref[...] * cgate_ref[...]
    y = yf_ref[...] + yb_ref[...]
    yc = y - _dot_exact_rhs(y, bd) * inv_n
    var = _dot_exact_rhs(yc * yc, bd) * inv_n
    o_d = (yc * lax.rsqrt(var + D_GN_EPS) * dlng_ref[...] + dlnb_ref[...] + bonus_ref[...]) * dgate_ref[...]
    branches = (oa_ref[...], ob_ref[...], o_c, o_d)
    h = (x * (1.0 + sc1) + sh1).astype(BF16)
    merged = None
    for n in range(N_BRANCH):
        gate = _sigmoid(jnp.dot(h, wg_ref[0, :, n * D_MODEL:(n + 1) * D_MODEL], preferred_element_type=F32))
        term = gate * jnp.dot(branches[n].astype(BF16), wbr_ref[0, n], preferred_element_type=F32)
        merged = term if merged is None else merged + term
    mix = jnp.dot(merged.astype(BF16), wout_ref[0], preferred_element_type=F32)
    x1 = _layer_norm(ALPHA * x + g1 * mix, lng_ref[...], lnb_ref[...])
    x1_o[...] = x1
    h2 = x1 * (1.0 + sc2) + sh2
    h2_o[...] = h2.astype(BF16)
    logits = _dot6(h2, wr_ref[...]) + br_ref[...]
    tm, n_e = logits.shape
    lane_e = lax.broadcasted_iota(jnp.int32, (tm, n_e), 1)
    lane_o = lax.broadcasted_iota(jnp.int32, (tm, topi_o.shape[1]), 1)
    top_i = jnp.zeros((tm, topi_o.shape[1]), jnp.int32)
    top_v = jnp.zeros((tm, topw_o.shape[1]), F32)
    vals = []
    for kth in range(TOP_K):
        mx = jnp.max(logits, axis=-1, keepdims=True)
        idx = jnp.min(jnp.where(logits == mx, lane_e, n_e), axis=-1, keepdims=True)
        vals.append(mx)
        top_i = jnp.where(lane_o == kth, idx, top_i)
        logits = jnp.where(lane_e == idx, -jnp.inf, logits)
    es = [jnp.exp(vk - vals[0]) for vk in vals]
    den = es[0] + es[1] + es[2] + es[3]
    for kth in range(TOP_K):
        top_v = jnp.where(lane_o == kth, es[kth] / den, top_v)
    topi_o[...] = top_i
    topw_o[...] = top_v


def merge_and_route(x, mod, o_a, o_b, co_f, co_b, cgate, y_f, y_b, bonus, dgate, w_g, w_br, w_out, layer, sp):
    tm = ROW_TILE
    hn = D_HEADS * D_N
    row = lambda w: pl.BlockSpec((tm, w), lambda t: (t, 0))
    small = [sp[k] for k in ('c_norm', 'd_ln_g', 'd_ln_b', 'ln_g', 'ln_b', 'w_router', 'b_router')]
    index = [layer, layer, layer, 2 * layer, 2 * layer, layer, layer]
    bd = _head_block_diag()
    return pl.pallas_call(
        _merge_kernel,
        grid=(N_TILES,),
        in_specs=([row(D_MODEL), _mod_spec(layer)]
                  + [row(hn)] * 9 + [_layer_block(w, layer) for w in (w_g, w_br, w_out)]
                  + [_layer_item(a, i) for a, i in zip(small, index)] + [_full(bd.shape)]),
        out_specs=[row(D_MODEL), row(D_MODEL), row(128), row(128)],
        out_shape=[jax.ShapeDtypeStruct((N_TOK, D_MODEL), F32), jax.ShapeDtypeStruct((MOE_ROWS, D_MODEL), BF16),
                   jax.ShapeDtypeStruct((N_TOK, 128), jnp.int32), jax.ShapeDtypeStruct((N_TOK, 128), F32)],
        compiler_params=pltpu.CompilerParams(dimension_semantics=("parallel",), vmem_limit_bytes=VMEM_LIMIT),
        name="merge_and_route",
    )(x, mod, o_a, o_b, co_f, co_b, cgate, y_f, y_b, bonus, dgate, w_g, w_br, w_out, *small, bd)


def _moe_kernel(te_ref, tv_ref, first_ref, slot_ref, next_ref, x_ref, w1_hbm, b1_ref, w2_hbm, b2_ref, perm_ref, *rest,
                layer):
    y_ref, w1buf, w2buf, sem, w1s, w2s, hs = rest[-7:]
    t = pl.program_id(0)
    valid = tv_ref[t] != 0
    d_model, two_f = w1s.shape
    n_blk = two_f // MXU_WIDTH
    half = MXU_WIDTH // 2

    def fetch(expert, slot):
        return (pltpu.make_async_copy(w1_hbm.at[layer, expert], w1buf.at[slot], sem.at[0, slot]),
                pltpu.make_async_copy(w2_hbm.at[layer, expert], w2buf.at[slot], sem.at[1, slot]))

    @pl.when(t == 0)
    def _():
        for cp in fetch(te_ref[0], 0):
            cp.start()

    @pl.when(first_ref[t] == 1)
    def _():
        slot = slot_ref[t]
        for cp in fetch(te_ref[t], slot):
            cp.wait()

        @pl.when(next_ref[t] >= 0)
        def _():
            for cp in fetch(next_ref[t], 1 - slot):
                cp.start()

        for blk in range(n_blk):
            sl = slice(blk * MXU_WIDTH, (blk + 1) * MXU_WIDTH)
            wb = w1buf[slot, :, sl].astype(BF16)
            w1s[:, sl] = jnp.dot(wb, perm_ref[...], preferred_element_type=F32).astype(BF16)
        w2s[...] = w2buf[slot].astype(BF16)

    @pl.when(valid)
    def _():
        x = x_ref[...]
        for blk in range(n_blk):
            sl = slice(blk * MXU_WIDTH, (blk + 1) * MXU_WIDTH)
            u = jnp.dot(x, w1s[:, sl], preferred_element_type=F32) + b1_ref[0, 0, :, sl]
            glu = jnp.minimum(u[:, :half], SWIGLU_LIMIT)
            lin = jnp.clip(u[:, half:], -SWIGLU_LIMIT, SWIGLU_LIMIT)
            hs[:, blk * half:(blk + 1) * half] = (glu * _sigmoid(SWIGLU_ALPHA * glu) * (lin + 1.0)).astype(BF16)
        y = jnp.dot(hs[...], w2s[...], preferred_element_type=F32) + b2_ref[0, 0]
        y_ref[...] = y.astype(y_ref.dtype)

    @pl.when(jnp.logical_not(valid))
    def _():
        y_ref[...] = jnp.zeros_like(y_ref)


def _deinterleave_perm():
    half = MXU_WIDTH // 2
    src = np.arange(MXU_WIDTH)
    dst = np.where(src % 2 == 0, src // 2, half + src // 2)
    p = np.zeros((MXU_WIDTH, MXU_WIDTH), np.float32)
    p[src, dst] = 1.0
    return jnp.asarray(p, BF16)


def _moe_dispatch(top_i):
    n, k = top_i.shape
    tm = MOE_ROW_TILE
    p_rows = n * k + N_EXPERTS * tm
    experts = jnp.arange(N_EXPERTS, dtype=jnp.int32)
    onehot = top_i[:, :, None] == experts
    sel = jnp.sum(onehot.astype(jnp.int32), axis=1)
    before = jnp.cumsum(sel, axis=0) - sel
    counts = jnp.sum(sel, axis=0)
    padded = ((counts + tm - 1) // tm) * tm
    ends = jnp.cumsum(padded)
    starts = ends - padded
    pos = jnp.sum(jnp.where(onehot, (before + starts)[:, None, :], 0), axis=-1)
    n_tiles = p_rows // tm
    tile_start = jnp.arange(n_tiles, dtype=jnp.int32) * tm
    tile_valid = (tile_start < ends[-1]).astype(jnp.int32)
    last_tile = ends[-1] // tm - 1
    tile_expert = jnp.sum(ends[None, :] <= jnp.minimum(tile_start, last_tile * tm)[:, None], axis=1).astype(jnp.int32)
    keys = jnp.sort((top_i * n + jnp.arange(n, dtype=jnp.int32)[:, None]).reshape(-1))
    tile_onehot = tile_expert[:, None] == experts[None, :]
    lookup = lambda table: jnp.sum(jnp.where(tile_onehot, table[None, :], 0), axis=1)
    tile_rank0 = tile_start - lookup(starts)
    rank = tile_rank0[:, None] + jnp.arange(tm, dtype=jnp.int32)[None, :]
    sorted_at = jnp.clip(lookup(jnp.cumsum(counts) - counts)[:, None] + rank, 0, n * k - 1)
    tile_keys = keys[sorted_at.reshape(-1)].reshape(n_tiles, tm)
    filler = (tile_start[:, None] + jnp.arange(tm, dtype=jnp.int32)[None, :]) % n
    src_tok = jnp.where(rank < lookup(counts)[:, None], tile_keys % n, filler)
    nst = n_tiles // MOE_SPLITS
    idx = jnp.arange(nst, dtype=jnp.int32)
    tables = []
    for h in range(MOE_SPLITS):
        te_h = tile_expert[h * nst:(h + 1) * nst]
        is_first = jnp.concatenate([jnp.ones((1,), jnp.int32), (te_h[1:] != te_h[:-1]).astype(jnp.int32)])
        slot = (jnp.cumsum(is_first) - 1) % 2
        nxt = jnp.min(jnp.where(jnp.logical_and(idx[None, :] > idx[:, None], is_first[None, :] == 1),
                                idx[None, :], nst), axis=1)
        next_expert = jnp.sum(jnp.where(idx[None, :] == nxt[:, None], te_h[None, :] + 1, 0), axis=1) - 1
        tables.append((te_h, tile_valid[h * nst:(h + 1) * nst], is_first, slot.astype(jnp.int32),
                       next_expert.astype(jnp.int32)))
    return pos, src_tok.reshape(-1), tables, p_rows


def moe_experts(h2, top_i, layer, w1, b1, w2, b2):
    n = top_i.shape[0]
    d = h2.shape[1]
    depth, e, _, two_f = w1.shape
    f = two_f // 2
    tm = MOE_ROW_TILE
    pos, src_tok, tables, p_rows = _moe_dispatch(top_i)
    assert h2.shape[0] == p_rows
    src_tok = lax.optimization_barrier(src_tok)
    b1p = b1.reshape(depth, e, two_f // MXU_WIDTH, MXU_WIDTH // 2, 2).swapaxes(3, 4).reshape(depth, e, 1, two_f)
    b2r = b2.reshape(depth, e, 1, d)
    expert_vec = lambda w: pl.BlockSpec((1, 1, 1, w), lambda t, te, *_: (layer, te[t], 0, 0))
    rows = p_rows // MOE_SPLITS
    nst = rows // tm
    xs = [h2.at[src_tok[h * rows:(h + 1) * rows]].get(mode="promise_in_bounds") for h in range(MOE_SPLITS)]
    ys = None
    for h in range(MOE_SPLITS):
        in_specs = [
            pl.BlockSpec((tm, d), lambda t, *_: (t, 0)),
            pl.BlockSpec(memory_space=pl.ANY),
            expert_vec(two_f),
            pl.BlockSpec(memory_space=pl.ANY),
            expert_vec(d),
            pl.BlockSpec((MXU_WIDTH, MXU_WIDTH), lambda t, *_: (0, 0)),
        ]
        args = [*tables[h], xs[h], w1, b1p, w2, b2r, _deinterleave_perm()]
        aliases = {}
        if ys is not None:
            in_specs.append(pl.BlockSpec(memory_space=pl.ANY))
            aliases = {len(args): 0}
            args.append(ys)
        ys = pl.pallas_call(
            functools.partial(_moe_kernel, layer=layer),
            grid_spec=pltpu.PrefetchScalarGridSpec(
                num_scalar_prefetch=len(tables[h]), grid=(nst,), in_specs=in_specs,
                out_specs=pl.BlockSpec((tm, d), lambda t, *_, h=h: (h * nst + t, 0)),
                scratch_shapes=[pltpu.VMEM((2, d, two_f), F32), pltpu.VMEM((2, f, d), F32),
                                pltpu.SemaphoreType.DMA((2, 2)),
                                pltpu.VMEM((d, two_f), BF16), pltpu.VMEM((f, d), BF16), pltpu.VMEM((tm, f), BF16)]),
            out_shape=jax.ShapeDtypeStruct((p_rows, d), BF16),
            input_output_aliases=aliases,
            compiler_params=pltpu.CompilerParams(dimension_semantics=("arbitrary",),
                                                 vmem_limit_bytes=48 * 1024 * 1024),
            name="moe_experts",
        )(*args)
    return ys.at[lax.optimization_barrier(pos.T.reshape(-1))].get(mode="promise_in_bounds").reshape(TOP_K, n, d)


def _final_kernel(x1_ref, mod_ref, ys_ref, topw_ref, lng_ref, lnb_ref, o_ref):
    g2 = mod_ref[0, 5:6]
    moe = None
    for kth in range(TOP_K):
        term = ys_ref[kth].astype(F32) * topw_ref[:, kth:kth + 1]
        moe = term if moe is None else moe + term
    o_ref[...] = _layer_norm(ALPHA * x1_ref[...] + g2 * moe, lng_ref[...], lnb_ref[...])


def combine_and_norm(x1, mod, ys, top_w, layer, sp):
    tm = ROW_TILE
    ln_g, ln_b = sp['ln_g'], sp['ln_b']
    return pl.pallas_call(
        _final_kernel,
        grid=(N_TILES,),
        in_specs=[pl.BlockSpec((tm, D_MODEL), lambda t: (t, 0)),
                  _mod_spec(layer),
                  pl.BlockSpec((TOP_K, tm, D_MODEL), lambda t: (0, t, 0)),
                  pl.BlockSpec((tm, 128), lambda t: (t, 0)),
                  _layer_item(ln_g, 2 * layer + 1), _layer_item(ln_b, 2 * layer + 1)],
        out_specs=pl.BlockSpec((tm, D_MODEL), lambda t: (t, 0)),
        out_shape=jax.ShapeDtypeStruct((N_TOK, D_MODEL), F32),
        compiler_params=pltpu.CompilerParams(dimension_semantics=("parallel",)),
        name="combine_and_norm",
    )(x1, mod, ys, top_w, ln_g, ln_b)


def kernel(x_prompt, x_sample, cache_a_k, cache_a_v, cache_b_ckv, cache_b_kpe, state_c, state_d, c,
           c_ctx, w_mod, b_mod, w_in, a_sink, b_q_norm, b_w_uq, b_kv_norm, b_w_ukv, c_w_gate, c_b_gate,
           c_norm, d_mu, d_w0, d_w2, d_a0, d_a2, d_g2, d_k_k, d_k_a, d_r_k, d_ln_g, d_ln_b, w_br, w_out,
           ln_g, ln_b, w_router, b_router, w_mlp1, b_mlp1, w_mlp2, b_mlp2):
    sp = stacked_params(dict(
        a_sink=a_sink, b_q_norm=b_q_norm, b_w_uq=b_w_uq, b_kv_norm=b_kv_norm, b_w_ukv=b_w_ukv, c_w_gate=c_w_gate,
        c_b_gate=c_b_gate, c_norm=c_norm, d_mu=d_mu, d_w0=d_w0, d_w2=d_w2, d_a0=d_a0, d_a2=d_a2, d_g2=d_g2,
        d_k_k=d_k_k, d_k_a=d_k_a, d_r_k=d_r_k, d_ln_g=d_ln_g, d_ln_b=d_ln_b, ln_g=ln_g, ln_b=ln_b,
        w_router=w_router, b_router=b_router))
    assert x_prompt.shape == (N_CTX_SEQ, CTX_LEN, D_MODEL) and x_sample.shape == (N_LAT_SEQ, LAT_LEN, D_MODEL)
    x = jnp.concatenate([x_prompt.reshape(N_CTX, D_MODEL), x_sample.reshape(-1, D_MODEL)], axis=0)
    cond8 = jnp.concatenate([c_ctx[None], c, jnp.zeros((8 - 1 - N_LAT_SEQ, D_MODEL), F32)], axis=0)
    mod = modulation_table(cond8, w_mod, b_mod)[:, :1 + N_LAT_SEQ].reshape(DEPTH * (1 + N_LAT_SEQ), 6, D_MODEL)
    tables = _rope_tables()
    w_small, w_g = prepare_in_weights(w_in)
    w_br_bf, w_out_bf = w_br.astype(BF16), w_out.astype(BF16)
    new = {name: [] for name in ("a_k", "a_v", "b_ckv", "b_kpe", "c", "d")}
    for l in range(DEPTH):
        (aq, ak, av, bq, bckv, bkpe, cq4, ck4, cla_f, cla_b, cv, cgate,
         r, v, kk, lw_f, lw_b, k_f, k_b, a_f, a_b, bonus, dgate, ci_f, ci_b) = mixer_prelude(
             x, mod, w_small, l, tables, sp)

        o_a = gqa_attention(aq, ak, av, sp['a_sink'], cache_a_k, cache_a_v, l)
        o_b = mla_attention(bq, bckv, bkpe, sp['b_w_ukv'], cache_b_ckv, cache_b_kpe, l)

        c_s0 = jnp.swapaxes(state_c[:, l], 3, 4).reshape(1, N_CHAIN, C_DV, C_DK)
        co_f, co_b, c_fin = _recurrence_calls(functools.partial(_gla_kernel, dot=_dot1), "gla", [cq4, ck4, cv],
                                              [cla_f], [cla_b], c_s0, (C_DV, C_DK), C_HEADS * C_DV)
        d_s0 = state_d[:, l].reshape(1, N_CHAIN, D_N, D_N)
        y_f, y_b, d_fin = _recurrence_calls(functools.partial(_rwkv_kernel, dot=_dot1), "rwkv7", [r, v, kk],
                                            [lw_f, k_f, a_f, ci_f], [lw_b, k_b, a_b, ci_b], d_s0, (D_N, D_N),
                                            D_HEADS * D_N)

        x1, h2, top_i, top_w = merge_and_route(x, mod, o_a, o_b, co_f, co_b, cgate, y_f, y_b, bonus, dgate,
                                               w_g, w_br_bf, w_out_bf, l, sp)
        ys = moe_experts(h2, top_i[:, :TOP_K], l, w_mlp1, b_mlp1, w_mlp2, b_mlp2)
        x = combine_and_norm(x1, mod, ys, top_w, l, sp)

        new["a_k"].append(ak[:N_CTX].reshape(N_CTX_SEQ, CTX_LEN, A_KV_HEADS, A_HD).transpose(0, 2, 1, 3))
        new["a_v"].append(av[:N_CTX].reshape(N_CTX_SEQ, CTX_LEN, A_KV_HEADS, A_HD).transpose(0, 2, 1, 3))
        new["b_ckv"].append(bckv[:N_CTX].reshape(N_CTX_SEQ, CTX_LEN, B_KV_LORA))
        new["b_kpe"].append(bkpe[:N_CTX].reshape(N_CTX_SEQ, CTX_LEN, B_ROPE))
        new["c"].append(jnp.swapaxes(c_fin.reshape(N_CTX_SEQ, 2, C_HEADS, C_DV, C_DK), 3, 4))
        new["d"].append(d_fin.reshape(N_CTX_SEQ, 2, D_HEADS, D_N, D_N))
    y_prompt = x[:N_CTX].reshape(x_prompt.shape)
    y_sample = x[N_CTX:].reshape(x_sample.shape)
    return (y_prompt, y_sample, *(jnp.stack(new[name], axis=1) for name in ("a_k", "a_v", "b_ckv", "b_kpe", "c", "d")))
```

```python
import functools

import jax
import jax.numpy as jnp
import numpy as np
from jax import lax
from jax.experimental import pallas as pl
from jax.experimental.pallas import tpu as pltpu

F32 = jnp.float32
BF16 = jnp.bfloat16

MXU_WIDTH = 256
VMEM_LIMIT = 56 * 1024 * 1024

D_MODEL = 1024
DEPTH = 2
GRID_W = 64
ROPE_BASE = 10000.0
A_HEADS, A_KV_HEADS, A_HD = 4, 2, 64
B_HEADS, B_NOPE, B_ROPE, B_VD, B_Q_LORA, B_KV_LORA = 4, 64, 32, 64, 192, 128
C_HEADS, C_DK, C_DV, C_GATE_RANK, C_GATE_TEMP = 4, 32, 64, 16, 16.0
D_HEADS, D_N, D_DECAY_RANK, D_AAA_RANK, D_GATE_RANK, D_GN_EPS = 4, 64, 64, 64, 128, 64e-5
BRANCH_W = 256
N_BRANCH = 4
N_EXPERTS = 32
TOP_K = 4
SWIGLU_LIMIT = 7.0
SWIGLU_ALPHA = 1.702
ALPHA = (2 * DEPTH) ** 0.25
LN_EPS = 1e-5
RMS_EPS = 1e-6

N_CTX_SEQ, CTX_LEN = 16, 256
N_LAT_SEQ, LAT_LEN = 2, 2048
N_CTX = N_CTX_SEQ * CTX_LEN
N_TOK = N_CTX + N_LAT_SEQ * LAT_LEN
ROW_TILE = 256
N_TILES = N_TOK // ROW_TILE
CTX_TILES = N_CTX // ROW_TILE
LAT_TILES_PER_SEQ = LAT_LEN // ROW_TILE
N_SEQ = N_CTX_SEQ + N_LAT_SEQ
MLA_HEAD_LANES = 128
MOE_ROW_TILE = 256
MOE_SPLITS = 2
MOE_ROWS = N_TOK * TOP_K + N_EXPERTS * MOE_ROW_TILE

_ORIG = dict(aq=(0, 256), ak=(256, 384), av=(384, 512), bcq=(512, 704), bckv=(704, 832), bkpe=(832, 864),
             cq=(864, 992), ck=(992, 1120), cv=(1120, 1376), cog=(1376, 1632), caf=(1632, 1648), cab=(1648, 1664),
             zd=(1664, 2816))
_ORDER = ("aq", "ak", "av", "cq", "ck", "cv", "cog", "zd", "bcq", "caf", "cab", "bkpe", "bckv")
COL = {}
_off = 0
for _name in _ORDER:
    _w = _ORIG[_name][1] - _ORIG[_name][0]
    COL[_name] = (_off, _off + _w)
    _off += _w
SMALL_COLS = _off
G_START = 2816


def _cs(name):
    return slice(*COL[name])


def _split3(x):
    hi = x.astype(BF16)
    r1 = x - hi.astype(F32)
    mid = r1.astype(BF16)
    lo = (r1 - mid.astype(F32)).astype(BF16)
    return hi, mid, lo


def _split2(x):
    hi = x.astype(BF16)
    lo = (x - hi.astype(F32)).astype(BF16)
    return hi, lo


def _bdot(a, b, dims):
    return lax.dot_general(a, b, dims, preferred_element_type=F32)


_D2 = (((1,), (0,)), ((), ()))
_D2T = (((1,), (1,)), ((), ()))
_NN = (((2,), (1,)), ((0,), (0,)))
_NT = (((2,), (2,)), ((0,), (0,)))
_TN = (((1,), (1,)), ((0,), (0,)))


def _dot1(a, b, dims=_D2):
    return _bdot(a.astype(BF16), b.astype(BF16), dims)


def _dot3(a, b, dims=_D2):
    ah, al = _split2(a)
    bh, bl = _split2(b)
    return _bdot(ah, bh, dims) + (_bdot(ah, bl, dims) + _bdot(al, bh, dims))


def _dot_exact_lhs(a01, b, dims=_D2):
    a = a01.astype(BF16)
    h, m, l = _split3(b)
    return _bdot(a, h, dims) + (_bdot(a, m, dims) + _bdot(a, l, dims))


def _dot_exact_rhs(a, b01, dims=_D2):
    b = b01.astype(BF16)
    h, l = _split2(a)
    return _bdot(h, b, dims) + _bdot(l, b, dims)


def _dot6(a, b, dims=_D2):
    ah, am, al = _split3(a)
    bh, bm, bl = _split3(b)
    return (_bdot(ah, bh, dims) + (_bdot(ah, bm, dims) + _bdot(am, bh, dims))
            + (_bdot(am, bm, dims) + (_bdot(ah, bl, dims) + _bdot(al, bh, dims))))


def _sigmoid(x):
    return 0.5 * jnp.tanh(0.5 * x) + 0.5


def _softplus(x):
    return jnp.maximum(x, 0.0) + jnp.log(1.0 + jnp.exp(-jnp.abs(x)))


def _mod_row(t):
    return jnp.where(t < CTX_TILES, 0, 1 + (t - CTX_TILES) // LAT_TILES_PER_SEQ)


def _full(shape):
    nd = len(shape)
    return pl.BlockSpec(shape, lambda *_: (0,) * nd)


MOD_COL_TILE = 1536


def _mod_kernel(c_ref, w_ref, b_ref, o_ref):
    c = c_ref[...]
    o_ref[0] = _dot3(c * _sigmoid(c), w_ref[0]) + b_ref[0]


def modulation_table(cond8, w_mod, b_mod):
    depth, d, six_d = w_mod.shape
    return pl.pallas_call(
        _mod_kernel,
        grid=(depth, six_d // MOD_COL_TILE),
        in_specs=[pl.BlockSpec((8, d), lambda l, j: (0, 0)),
                  pl.BlockSpec((1, d, MOD_COL_TILE), lambda l, j: (l, 0, j)),
                  pl.BlockSpec((1, 1, MOD_COL_TILE), lambda l, j: (l, 0, j))],
        out_specs=pl.BlockSpec((1, 8, MOD_COL_TILE), lambda l, j: (l, 0, j)),
        out_shape=jax.ShapeDtypeStruct((depth, 8, six_d), F32),
        compiler_params=pltpu.CompilerParams(dimension_semantics=("parallel", "parallel")),
        name="modulation",
    )(cond8, w_mod, b_mod.reshape(depth, 1, six_d))


WPREP_ROWS = 128


def _wprep_kernel(w_ref, small_ref, gate_ref):
    for name in _ORDER:
        lo, hi = _ORIG[name]
        small_ref[0, :, _cs(name)] = w_ref[0, :, lo:hi].astype(BF16)
    gate_ref[0] = w_ref[0, :, G_START:].astype(BF16)


def prepare_in_weights(w_in):
    depth, d, cols = w_in.shape
    return pl.pallas_call(
        _wprep_kernel,
        grid=(depth, d // WPREP_ROWS),
        in_specs=[pl.BlockSpec((1, WPREP_ROWS, cols), lambda l, r: (l, r, 0))],
        out_specs=[pl.BlockSpec((1, WPREP_ROWS, SMALL_COLS), lambda l, r: (l, r, 0)),
                   pl.BlockSpec((1, WPREP_ROWS, cols - G_START), lambda l, r: (l, r, 0))],
        out_shape=[jax.ShapeDtypeStruct((depth, d, SMALL_COLS), BF16),
                   jax.ShapeDtypeStruct((depth, d, cols - G_START), BF16)],
        compiler_params=pltpu.CompilerParams(dimension_semantics=("parallel", "parallel")),
        name="prepare_in_weights",
    )(w_in)


def _rot_pairs(x, half, lane_mod_base=0):
    w = x.shape[-1]
    lane = lax.broadcasted_iota(jnp.int32, (1, w), 1) - lane_mod_base
    first = (lane % (2 * half)) < half
    return jnp.where(first, -pltpu.roll(x, w - half, axis=1), pltpu.roll(x, half, axis=1))


def _pre_kernel(x_ref, xp_ref, xn_ref, mod_ref, w_ref, ca_ref, sa_ref, cb_ref, sb_ref, ck_ref, sk_ref,
                qnorm_ref, kvnorm_ref, wuq_ref, cwg_ref, cbg_ref, rep_ref, mu_ref, dw0_ref, dw2_ref, da0_ref,
                da2_ref, dg2_ref, dkk_ref, dka_ref, drk_ref, bd_ref,
                aq_o, ak_o, av_o, bq_o, bckv_o, bkpe_o, cq4_o, ck4_o, claf_o, clab_o, cv_o, cgate_o,
                r_o, v_o, kk_o, lwf_o, lwb_o, kf_o, kb_o, af_o, ab_o, bonus_o, dgate_o):
    t = pl.program_id(0)
    tm = x_ref.shape[0]
    sh1 = mod_ref[0, 0:1, :]
    sc1 = mod_ref[0, 1:2, :]

    def modulate(xv):
        return (xv * (1.0 + sc1) + sh1).astype(BF16)

    h_all = jnp.concatenate([modulate(x_ref[...]), modulate(xp_ref[...]), modulate(xn_ref[...])], axis=0)
    z_all = jnp.dot(h_all, w_ref[0], preferred_element_type=F32)
    z = z_all[:tm]

    aq = z[:, _cs("aq")]
    ak = z[:, _cs("ak")]
    aq_o[...] = aq * ca_ref[...] + _rot_pairs(aq, A_HD // 4) * sa_ref[...]
    ak_o[...] = ak * ca_ref[:, :ak.shape[1]] + _rot_pairs(ak, A_HD // 4) * sa_ref[:, :ak.shape[1]]
    av_o[...] = z[:, _cs("av")]

    bcq = z[:, _cs("bcq")]
    qn = bcq * lax.rsqrt(jnp.mean(bcq * bcq, axis=-1, keepdims=True) + RMS_EPS) * qnorm_ref[...]
    bq = _dot1(qn, wuq_ref[...])
    bq_o[...] = bq * cb_ref[...] + _rot_pairs(bq, B_ROPE // 4, lane_mod_base=B_NOPE) * sb_ref[...]
    bckv = z[:, _cs("bckv")]
    bckv_o[...] = bckv * lax.rsqrt(jnp.mean(bckv * bckv, axis=-1, keepdims=True) + RMS_EPS) * kvnorm_ref[...]
    kpe_lo = COL["bkpe"][0] // 128 * 128
    kblk = z[:, kpe_lo:kpe_lo + 128]
    kblk = kblk * ck_ref[...] + _rot_pairs(kblk, B_ROPE // 4) * sk_ref[...]
    bkpe_o[...] = kblk[:, COL["bkpe"][0] - kpe_lo:COL["bkpe"][1] - kpe_lo]

    rep = rep_ref[...]
    cq4_o[...] = _dot1(z[:, _cs("cq")] * (C_DK ** -0.5), rep)
    ck4_o[...] = _dot1(z[:, _cs("ck")], rep)
    cv_o[...] = z[:, _cs("cv")]
    cog = z[:, _cs("cog")]
    cgate_o[...] = cog * _sigmoid(cog)
    for direction, (name, out) in enumerate((("caf", claf_o), ("cab", clab_o))):
        pre = _dot3(z[:, _cs(name)], cwg_ref[direction]) + cbg_ref[direction]
        la_hi, la_lo = _split2(-_softplus(-pre) * (1.0 / C_GATE_TEMP))
        out[...] = _bdot(la_hi, rep, _D2) + _bdot(la_lo, rep, _D2)

    zd_cols = _cs("zd")
    zd = z[:, zd_cols]
    j = (t - CTX_TILES) % LAT_TILES_PER_SEQ
    latent = t >= CTX_TILES
    has_prev = jnp.logical_and(latent, j != 0)
    has_next = jnp.logical_and(latent, j != LAT_TILES_PER_SEQ - 1)
    prev_row = jnp.where(has_prev, z_all[tm + 7:tm + 8, zd_cols], 0.0)
    next_row = jnp.where(has_next, z_all[tm + 8:tm + 9, zd_cols], 0.0)
    row = lax.broadcasted_iota(jnp.int32, (tm, 1), 0)
    up = jnp.where(row == 0, prev_row, pltpu.roll(zd, 1, axis=0))
    dn = jnp.where(row == tm - 1, next_row, pltpu.roll(zd, tm - 1, axis=0))
    zd = zd + (0.5 * (up + dn) - zd) * mu_ref[...]

    hn = D_HEADS * D_N
    d_r, d_k, d_v = zd[:, :hn], zd[:, hn:2 * hn], zd[:, 2 * hn:3 * hn]
    o = 3 * hn
    d_w = (zd[:, o:o + D_DECAY_RANK], zd[:, o + D_DECAY_RANK:o + 2 * D_DECAY_RANK])
    o += 2 * D_DECAY_RANK
    d_a = (zd[:, o:o + D_AAA_RANK], zd[:, o + D_AAA_RANK:o + 2 * D_AAA_RANK])
    o += 2 * D_AAA_RANK
    d_g = zd[:, o:o + D_GATE_RANK]
    bd = bd_ref[...]
    kk = d_k * dkk_ref[...]
    kk = kk / jnp.maximum(jnp.sqrt(_dot_exact_rhs(kk * kk, bd)), 1e-12)
    r_o[...] = d_r
    v_o[...] = d_v
    kk_o[...] = kk
    k_sum = None
    for direction, (lw_o, k_o, a_o) in enumerate(((lwf_o, kf_o, af_o), (lwb_o, kb_o, ab_o))):
        w_log = -_softplus(-(dw0_ref[direction] + _dot3(jnp.tanh(d_w[direction]), dw2_ref[direction]))) - 0.5
        lw_o[...] = -jnp.exp(w_log)
        a = _sigmoid(da0_ref[direction] + _dot1(d_a[direction], da2_ref[direction]))
        k_dir = d_k * (1.0 + (a - 1.0) * dka_ref[...])
        k_o[...] = k_dir
        a_o[...] = a
        k_sum = k_dir if k_sum is None else k_sum + k_dir
    bonus_o[...] = d_v * _dot_exact_rhs(d_r * drk_ref[...] * k_sum, bd)
    dgate_o[...] = _dot1(_sigmoid(d_g), dg2_ref[...])


def _rope_tables():
    pos = np.arange(LAT_LEN)
    rowp, colp = (pos // GRID_W).astype(np.float32), (pos % GRID_W).astype(np.float32)

    f32 = np.float32

    def head_tables(rot_dim):
        quarter = rot_dim // 4
        inv = (f32(ROPE_BASE) ** (-np.arange(quarter, dtype=f32) / f32(quarter))).astype(f32)
        ar = (rowp[:, None] * inv).astype(f32)
        ac = (colp[:, None] * inv).astype(f32)
        cos = np.concatenate([np.cos(ar), np.cos(ar), np.cos(ac), np.cos(ac)], axis=-1).astype(f32)
        sin = np.concatenate([np.sin(ar), np.sin(ar), np.sin(ac), np.sin(ac)], axis=-1).astype(f32)
        return cos, sin

    def with_identity(c, s):
        w = c.shape[1]
        return (jnp.asarray(np.concatenate([np.ones((ROW_TILE, w), f32), c], axis=0)),
                jnp.asarray(np.concatenate([np.zeros((ROW_TILE, w), f32), s], axis=0)))

    ca, sa = head_tables(A_HD)
    ca, sa = with_identity(np.tile(ca, (1, A_HEADS)), np.tile(sa, (1, A_HEADS)))
    cbh, sbh = head_tables(B_ROPE)
    ones, zeros = np.ones((LAT_LEN, B_NOPE), f32), np.zeros((LAT_LEN, B_NOPE), f32)
    qpad = MLA_HEAD_LANES - B_NOPE - B_ROPE
    cb, sb = with_identity(
        np.tile(np.concatenate([ones, cbh, np.ones((LAT_LEN, qpad), f32)], axis=1), (1, B_HEADS)),
        np.tile(np.concatenate([zeros, sbh, np.zeros((LAT_LEN, qpad), f32)], axis=1), (1, B_HEADS)))
    pad = 128 - B_ROPE
    ck, sk = with_identity(np.concatenate([np.ones((LAT_LEN, pad), f32), cbh], axis=1),
                           np.concatenate([np.zeros((LAT_LEN, pad), f32), sbh], axis=1))
    return ca, sa, cb, sb, ck, sk


def _lane_repeat_matrix():
    m = np.zeros((C_HEADS * C_DK, C_HEADS * 128), np.float32)
    for h in range(C_HEADS):
        for g in range(128 // C_DK):
            for d in range(C_DK):
                m[h * C_DK + d, h * 128 + g * C_DK + d] = 1.0
    return jnp.asarray(m, BF16)


def _head_block_diag():
    m = np.kron(np.eye(D_HEADS, dtype=np.float32), np.ones((D_N, D_N), np.float32))
    return jnp.asarray(m, BF16)


PRE_OUT_WIDTHS = (256, 128, 128, B_HEADS * MLA_HEAD_LANES, 128, 32, 512, 512, 512, 512, 256, 256) + (256,) * 11


def _layer_block(arr, layer):
    nd = arr.ndim
    return pl.BlockSpec((1,) + arr.shape[1:], lambda *_: (layer,) + (0,) * (nd - 1))


def _layer_item(arr, index):
    nd = arr.ndim
    return pl.BlockSpec((None,) + arr.shape[1:], lambda *_: (index,) + (0,) * (nd - 1))


def _mod_spec(layer):
    return pl.BlockSpec((1, 6, D_MODEL), lambda t: (layer * (1 + N_LAT_SEQ) + _mod_row(t), 0, 0))


def stacked_params(p):
    depth = p['d_mu'].shape[0]
    hn = D_HEADS * D_N
    w_uq = p['b_w_uq'].reshape(depth, B_Q_LORA, B_HEADS, B_NOPE + B_ROPE)
    w_uq = jnp.pad(w_uq, ((0, 0), (0, 0), (0, 0), (0, MLA_HEAD_LANES - B_NOPE - B_ROPE)))
    row = lambda a: a.reshape(depth, 1, -1)
    return dict(
        b_q_norm=row(p['b_q_norm']), b_kv_norm=row(p['b_kv_norm']), w_uq=w_uq.reshape(depth, B_Q_LORA, -1),
        c_w_gate=p['c_w_gate'], c_b_gate=p['c_b_gate'].reshape(depth, 2, 1, -1), d_mu=row(p['d_mu']),
        d_w0=p['d_w0'].reshape(depth, 2, 1, hn), d_w2=p['d_w2'], d_a0=p['d_a0'].reshape(depth, 2, 1, hn),
        d_a2=p['d_a2'], d_g2=p['d_g2'], d_k_k=row(p['d_k_k']), d_k_a=row(p['d_k_a']), d_r_k=row(p['d_r_k']),
        c_norm=row(jnp.tile(p['c_norm'], (1, C_HEADS))), d_ln_g=row(p['d_ln_g']), d_ln_b=row(p['d_ln_b']),
        ln_g=p['ln_g'].reshape(depth * 2, 1, -1), ln_b=p['ln_b'].reshape(depth * 2, 1, -1),
        w_router=p['w_router'], b_router=row(p['b_router']), a_sink=p['a_sink'].reshape(-1), b_w_ukv=p['b_w_ukv'])


def mixer_prelude(x, mod, w_small, layer, tables, sp):
    tm = ROW_TILE
    tab_idx = lambda t: (jnp.where(t < CTX_TILES, 0, 1 + (t - CTX_TILES) % LAT_TILES_PER_SEQ), 0)
    small = [sp[k] for k in ('b_q_norm', 'b_kv_norm', 'w_uq', 'c_w_gate', 'c_b_gate')] + [_lane_repeat_matrix()]
    small += [sp[k] for k in ('d_mu', 'd_w0', 'd_w2', 'd_a0', 'd_a2', 'd_g2', 'd_k_k', 'd_k_a', 'd_r_k')]
    small += [_head_block_diag()]
    const = lambda a: _full(a.shape) if a.dtype == BF16 else _layer_item(a, layer)
    in_specs = ([pl.BlockSpec((tm, D_MODEL), lambda t: (t, 0)),
                 pl.BlockSpec((8, D_MODEL), lambda t: (jnp.maximum(t * (tm // 8) - 1, 0), 0)),
                 pl.BlockSpec((8, D_MODEL), lambda t: (jnp.minimum((t + 1) * (tm // 8), N_TOK // 8 - 1), 0)),
                 _mod_spec(layer),
                 _layer_block(w_small, layer)]
                + [pl.BlockSpec((tm, tab.shape[1]), tab_idx) for tab in tables]
                + [const(a) for a in small])
    return pl.pallas_call(
        _pre_kernel,
        grid=(N_TILES,),
        in_specs=in_specs,
        out_specs=[pl.BlockSpec((tm, w), lambda t: (t, 0)) for w in PRE_OUT_WIDTHS],
        out_shape=[jax.ShapeDtypeStruct((N_TOK, w), F32) for w in PRE_OUT_WIDTHS],
        compiler_params=pltpu.CompilerParams(dimension_semantics=("parallel",), vmem_limit_bytes=VMEM_LIMIT),
        name="mixer_prelude",
    )(x, x, x, mod, w_small, *tables, *small)


ATT_Q_BLOCK = 128
MLA_Q_BLOCK = 256
ATT_WINDOW = 128
ATT_NEG_INF = -1e30
CACHE_LEN = 512


def _softmax_pv(s, v, sink):
    dv = v.shape[1] // 2
    m = jnp.max(s, axis=-1, keepdims=True)
    if sink is not None:
        m = jnp.maximum(m, sink)
    e = jnp.exp((s - m).astype(BF16))
    o = jnp.dot(e, v, preferred_element_type=F32)
    den = o[:, dv:dv + 1]
    if sink is not None:
        den = den + jnp.exp(sink - m)
    return o[:, :dv] / den


def _with_ones(v):
    return jnp.concatenate([v.astype(BF16), jnp.ones(v.shape, BF16)], axis=1)


def _gqa_kernel(sink_ref, q_ref, k_ref, v_ref, *rest, hd, group, scale, windowed, sink_base):
    if windowed:
        kp_ref, kn_ref, vp_ref, vn_ref, kc_ref, vc_ref, _, o_ref = rest
    else:
        (o_ref,) = rest
    i = pl.program_id(1)
    tq = q_ref.shape[0]
    n_kv = k_ref.shape[1] // hd
    if windowed:
        qpos = i * tq + lax.broadcasted_iota(jnp.int32, (tq, 3 * tq), 0)
        kpos = (i - 1) * tq + lax.broadcasted_iota(jnp.int32, (tq, 3 * tq), 1)
        n_tok = pl.num_programs(1) * tq
        mask = (jnp.abs(qpos - kpos) <= ATT_WINDOW) & (kpos >= 0) & (kpos < n_tok)
        mask = jnp.concatenate([mask] * group, axis=0)
    for kvh in range(n_kv):
        ks = slice(kvh * hd, (kvh + 1) * hd)
        qs = [q_ref[:, (kvh * group + g) * hd:(kvh * group + g + 1) * hd] for g in range(group)]
        q = (jnp.concatenate(qs, axis=0) * scale).astype(BF16)
        sink = jnp.concatenate(
            [jnp.full((tq, 1), sink_ref[sink_base + kvh * group + g], F32) for g in range(group)], axis=0)
        if windowed:
            k_win = jnp.concatenate([kp_ref[:, ks], k_ref[:, ks], kn_ref[:, ks]], axis=0)
            v_win = jnp.concatenate([vp_ref[:, ks], v_ref[:, ks], vn_ref[:, ks]], axis=0)
            s_win = _bdot(q, k_win.astype(BF16), _D2T)
            s_win = jnp.where(mask, s_win, ATT_NEG_INF)
            s_ctx = _bdot(q, kc_ref[0, 0, kvh].astype(BF16), _D2T)
            s = jnp.concatenate([s_win, s_ctx], axis=1)
            v = jnp.concatenate([v_win, vc_ref[0, 0, kvh]], axis=0)
        else:
            s = _bdot(q, k_ref[:, ks].astype(BF16), _D2T)
            v = v_ref[:, ks]
        o = _softmax_pv(s, _with_ones(v), sink)
        for g in range(group):
            h = kvh * group + g
            o_ref[:, h * hd:(h + 1) * hd] = o[g * tq:(g + 1) * tq]


def gqa_attention(q, k, v, sink, cache_k, cache_v, layer):
    qw, kw = q.shape[1], k.shape[1]
    group = qw // kw
    scale = A_HD ** -0.5
    params = pltpu.CompilerParams(dimension_semantics=("parallel", "parallel"))
    out_shape = jax.ShapeDtypeStruct((N_TOK, qw), F32)
    ctx_spec = lambda w: pl.BlockSpec((CTX_LEN, w), lambda s, i, sk: (s, 0))
    o = pl.pallas_call(
        functools.partial(_gqa_kernel, hd=A_HD, group=group, scale=scale, windowed=False, sink_base=layer * A_HEADS),
        grid_spec=pltpu.PrefetchScalarGridSpec(
            num_scalar_prefetch=1, grid=(N_CTX_SEQ, 1), in_specs=[ctx_spec(qw), ctx_spec(kw), ctx_spec(kw)],
            out_specs=ctx_spec(qw)),
        out_shape=out_shape, compiler_params=params, name="gqa_full",
    )(sink, q, k, v)
    tq = ATT_Q_BLOCK
    nb = LAT_LEN // tq
    base = N_CTX // tq
    blk = lambda w, f: pl.BlockSpec((tq, w), lambda b, i, sk: (base + nb * b + f(i), 0))
    same = lambda i: i
    prev = lambda i: jnp.maximum(i - 1, 0)
    nxt = lambda i: jnp.minimum(i + 1, nb - 1)
    cspec = pl.BlockSpec((1, 1) + cache_k.shape[2:], lambda b, i, sk: (b, layer, 0, 0, 0))
    return pl.pallas_call(
        functools.partial(_gqa_kernel, hd=A_HD, group=group, scale=scale, windowed=True, sink_base=layer * A_HEADS),
        grid_spec=pltpu.PrefetchScalarGridSpec(
            num_scalar_prefetch=1, grid=(N_LAT_SEQ, nb),
            in_specs=[blk(qw, same), blk(kw, same), blk(kw, same), blk(kw, prev), blk(kw, nxt), blk(kw, prev),
                      blk(kw, nxt), cspec, cspec, pl.BlockSpec(memory_space=pl.ANY)],
            out_specs=blk(qw, same)),
        out_shape=out_shape, input_output_aliases={10: 0}, compiler_params=params, name="gqa_windowed",
    )(sink, q, k, v, k, k, v, v, cache_k, cache_v, o)


def _mla_kernel(q_ref, ckv_ref, kpe_ref, wukv_ref, *rest, n_heads, nope, rope, vd, scale, cached):
    if cached:
        cckv_ref, ckpe_ref, _, o_ref, k_scr, vext_scr = rest
    else:
        o_ref, k_scr, vext_scr = rest
    i = pl.program_id(1)
    n_cache = k_scr.shape[0] - ckv_ref.shape[0]
    hw = nope + vd
    hl = MLA_HEAD_LANES

    @pl.when(i == 0)
    def _():
        w = wukv_ref[...].astype(BF16)

        def expand(rows, kpe_rows, lo, hi):
            kv = jnp.dot(rows.astype(BF16), w, preferred_element_type=F32).astype(BF16)
            n = hi - lo
            kpe = kpe_rows.astype(BF16)
            for h in range(n_heads):
                k_scr[lo:hi, hl * h:hl * (h + 1)] = jnp.concatenate(
                    [kv[:, h * hw:h * hw + nope], kpe, jnp.zeros((n, hl - nope - rope), BF16)], axis=1)
                vext_scr[lo:hi, 2 * vd * h:2 * vd * (h + 1)] = jnp.concatenate(
                    [kv[:, h * hw + nope:(h + 1) * hw], jnp.ones((n, vd), BF16)], axis=1)

        if cached:
            expand(cckv_ref[0, 0], ckpe_ref[0, 0], 0, n_cache)
        expand(ckv_ref[...], kpe_ref[...], n_cache, k_scr.shape[0])

    for h in range(n_heads):
        qh = (q_ref[:, hl * h:hl * (h + 1)] * scale).astype(BF16)
        s = _bdot(qh, k_scr[:, hl * h:hl * (h + 1)], _D2T)
        o_ref[:, h * vd:(h + 1) * vd] = _softmax_pv(s, vext_scr[:, 2 * vd * h:2 * vd * (h + 1)], None)


def mla_attention(q, ckv, kpe, w_ukv, cache_ckv, cache_kpe, layer):
    qw = q.shape[1]
    tq = MLA_Q_BLOCK
    kw = dict(n_heads=B_HEADS, nope=B_NOPE, rope=B_ROPE, vd=B_VD, scale=(B_NOPE + B_ROPE) ** -0.5)
    params = pltpu.CompilerParams(dimension_semantics=("parallel", "arbitrary"))
    out_shape = jax.ShapeDtypeStruct((N_TOK, B_HEADS * B_VD), F32)
    scratch = lambda rows: [pltpu.VMEM((rows, B_HEADS * MLA_HEAD_LANES), BF16),
                            pltpu.VMEM((rows, 2 * B_HEADS * B_VD), BF16)]
    nbc = CTX_LEN // tq
    o = pl.pallas_call(
        functools.partial(_mla_kernel, cached=False, **kw),
        grid=(N_CTX_SEQ, nbc),
        in_specs=[pl.BlockSpec((tq, qw), lambda s, i: (s * nbc + i, 0)),
                  pl.BlockSpec((CTX_LEN, B_KV_LORA), lambda s, i: (s, 0)),
                  pl.BlockSpec((CTX_LEN, B_ROPE), lambda s, i: (s, 0)),
                  _layer_item(w_ukv, layer)],
        out_specs=pl.BlockSpec((tq, B_HEADS * B_VD), lambda s, i: (s * nbc + i, 0)),
        out_shape=out_shape,
        scratch_shapes=scratch(CTX_LEN),
        compiler_params=params, name="mla_context",
    )(q, ckv, kpe, w_ukv)
    nb = LAT_LEN // tq
    base = N_CTX // tq
    lat0 = N_CTX // LAT_LEN
    s_len = CACHE_LEN + LAT_LEN
    return pl.pallas_call(
        functools.partial(_mla_kernel, cached=True, **kw),
        grid=(N_LAT_SEQ, nb),
        in_specs=[pl.BlockSpec((tq, qw), lambda b, i: (base + nb * b + i, 0)),
                  pl.BlockSpec((LAT_LEN, B_KV_LORA), lambda b, i: (lat0 + b, 0)),
                  pl.BlockSpec((LAT_LEN, B_ROPE), lambda b, i: (lat0 + b, 0)),
                  _layer_item(w_ukv, layer),
                  pl.BlockSpec((1, 1, CACHE_LEN, B_KV_LORA), lambda b, i: (b, layer, 0, 0)),
                  pl.BlockSpec((1, 1, CACHE_LEN, B_ROPE), lambda b, i: (b, layer, 0, 0)),
                  pl.BlockSpec(memory_space=pl.ANY)],
        out_specs=pl.BlockSpec((tq, B_HEADS * B_VD), lambda b, i: (base + nb * b + i, 0)),
        out_shape=out_shape, input_output_aliases={6: 0},
        scratch_shapes=scratch(s_len),
        compiler_params=params, name="mla_latent",
    )(q, ckv, kpe, w_ukv, cache_ckv, cache_kpe, o)


CHUNK = 64
GLA_SUB = 16
PAIR = 2
N_CHAIN = PAIR * 2 * 4


def _is_back(shape):
    return (lax.broadcasted_iota(jnp.int32, shape, 0) // 4) % 2 == 1


def _chains(ref_f, ref_b, width):
    return jnp.stack([ref[0, s, 0, :, h * width:(h + 1) * width]
                      for s in range(PAIR) for ref in (ref_f, ref_b) for h in range(4)], axis=0)


def _unchain(y, o_f, o_b):
    for s in range(PAIR):
        o_f[0, s, 0] = jnp.concatenate([y[s * 8 + h] for h in range(4)], axis=-1)
        o_b[0, s, 0] = jnp.concatenate([y[s * 8 + 4 + h] for h in range(4)], axis=-1)


def _dir_masks(L):
    shape = (N_CHAIN, L, L)
    back = _is_back(shape)
    row = lax.broadcasted_iota(jnp.int32, shape, 1)
    col = lax.broadcasted_iota(jnp.int32, shape, 2)
    ahead = jnp.where(back, col - row, row - col)
    return ahead >= 0, ahead > 0, row == col


def _chunk_end(ci):
    L = ci.shape[1]
    return jnp.where(_is_back((N_CHAIN, 1, 1)), ci[:, 0:1], ci[:, L - 1:L])


def _split_refs(refs, n_in, has_s0, has_sfin):
    ins = refs[:n_in]
    pos = n_in
    s0_ref = None
    if has_s0:
        s0_ref = refs[pos]
        pos += 3
    of_ref, ob_ref = refs[pos], refs[pos + 1]
    pos += 2
    sfin_ref = refs[pos] if has_sfin else None
    return ins, s0_ref, of_ref, ob_ref, sfin_ref, refs[-1]


def _init_state(s_scr, s0_ref):
    @pl.when(pl.program_id(1) == 0)
    def _():
        if s0_ref is None:
            s_scr[...] = jnp.zeros_like(s_scr)
        else:
            s_scr[...] = s0_ref[0]


def _emit_state(sfin_ref, s_new):
    if sfin_ref is None:
        return

    @pl.when(pl.program_id(1) == pl.num_programs(1) - 1)
    def _():
        sfin_ref[0] = s_new


def _rwkv_kernel(*refs, dot, has_s0, has_sfin):
    (rf, rb, vf, vb, kkf, kkb, lwf, lwb, kf, kb, af, ab), s0_ref, yf_ref, yb_ref, sfin_ref, s_scr = _split_refs(
        refs, 12, has_s0, has_sfin)
    _init_state(s_scr, s0_ref)
    n = D_N
    r = _chains(rf, rb, n)
    v = _chains(vf, vb, n)
    kk = _chains(kkf, kkb, n)
    lw = _chains(lwf, lwb, n)
    k = _chains(kf, kb, n)
    a = _chains(af, ab, n)
    L = r.shape[1]
    S = s_scr[...]
    incl, strict, diag = _dir_masks(L)
    ci = _dot_exact_lhs(jnp.where(incl, 1.0, 0.0), lw, _NN)
    ce = ci - lw
    cl = _chunk_end(ci)
    e_neg = jnp.exp(-ci)
    b = a * kk
    alpha = kk * jnp.exp(ce)
    rho = r * jnp.exp(ci)
    beta = b * e_neg
    kappa = k * e_neg
    e_end = jnp.exp(cl - ci)
    ar = jnp.concatenate([alpha, rho], axis=1)
    bk = jnp.concatenate([beta, kappa], axis=1)
    w = dot(ar, bk, _NT)
    nmat = jnp.where(strict, w[:, :L, :L], 0.0)
    mmat = jnp.where(strict, w[:, :L, L:], 0.0)
    p1 = jnp.where(incl, w[:, L:, :L], 0.0)
    p2 = jnp.where(incl, w[:, L:, L:], 0.0)
    x = jnp.where(diag, 1.0, 0.0) - nmat
    p = dot(nmat, nmat, _NN)
    span = 2
    while True:
        x = x + dot(x, p, _NN)
        span *= 2
        if span >= L:
            break
        p = dot(p, p, _NN)
    us = dot(ar, S, _NT)
    rhs = us[:, :L] + dot(mmat, v, _NN)
    d = -dot(x, rhs, _NN)
    dv = jnp.concatenate([d, v], axis=1)
    pp = jnp.concatenate([p1, p2], axis=2)
    _unchain(us[:, L:] + dot(pp, dv, _NN), yf_ref, yb_ref)
    bk_end = jnp.concatenate([b * e_end, k * e_end], axis=1)
    s_new = S * jnp.exp(cl) + dot(dv, bk_end, _TN)
    s_scr[...] = s_new
    _emit_state(sfin_ref, s_new)


def _gla_kernel(*refs, dot, has_s0, has_sfin):
    (qf, qb, kf, kb, vf, vb, laf, lab), s0_ref, of_ref, ob_ref, sfin_ref, s_scr = _split_refs(
        refs, 8, has_s0, has_sfin)
    _init_state(s_scr, s0_ref)
    q4 = _chains(qf, qb, 128)
    k4 = _chains(kf, kb, 128)
    la4 = _chains(laf, lab, 128)
    v = _chains(vf, vb, C_DV)
    g, L, lanes = q4.shape
    dk = C_DK
    n_sub = L // GLA_SUB
    st = s_scr[...]
    incl, _, _ = _dir_masks(L)
    c = _dot_exact_lhs(jnp.where(incl, 1.0, 0.0), la4, _NN)
    shape = (g, L, lanes)
    back = _is_back(shape)
    lane_blk = lax.broadcasted_iota(jnp.int32, shape, 2) // dk
    row_blk = lax.broadcasted_iota(jnp.int32, shape, 1) // GLA_SUB
    cref_f = jnp.zeros(shape, F32)
    cref_b = jnp.zeros(shape, F32)
    for j in range(1, n_sub):
        cref_f = jnp.where(lane_blk == j, c[:, j * GLA_SUB - 1:j * GLA_SUB], cref_f)
        cref_b = jnp.where(lane_blk == j - 1, c[:, j * GLA_SUB:j * GLA_SUB + 1], cref_b)
    cref = jnp.where(back, cref_b, cref_f)
    q_on = row_blk == lane_blk
    k_on = jnp.where(back, row_blk - lane_blk, lane_blk - row_blk) >= 0
    qh = jnp.where(q_on, q4 * jnp.exp(jnp.where(q_on, c - cref, 0.0)), 0.0)
    kh = jnp.where(k_on, k4 * jnp.exp(jnp.where(k_on, cref - c, 0.0)), 0.0)
    att = jnp.where(incl, dot(qh, kh, _NT), 0.0)
    cl = _chunk_end(c)
    qe = (q4 * jnp.exp(c))[:, :, :dk]
    ke = (k4 * jnp.exp(cl - c))[:, :, :dk]
    _unchain(dot(qe, st, _NT) + dot(att, v, _NN), of_ref, ob_ref)
    s_new = st * jnp.exp(cl[:, :, :dk]) + dot(v, ke, _TN)
    s_scr[...] = s_new
    _emit_state(sfin_ref, s_new)


def _recurrence_calls(kernel_fn, name, pairs, singles_f, singles_b, s0_lat, state_dims, out_width):
    def run(view, grid, group, s0, prev_out):
        nc = view[2]
        fwd_map = lambda p, c: (group(p), 0, c, 0, 0)
        bwd_map = lambda p, c: (group(p), 0, nc - 1 - c, 0, 0)
        blk = lambda w: (1, PAIR, 1, CHUNK, w)
        args, in_specs = [], []
        for af, ab in [(a, a) for a in pairs] + list(zip(singles_f, singles_b)):
            w = af.shape[-1]
            args += [af.reshape(view + (w,)), ab.reshape(view + (w,))]
            in_specs += [pl.BlockSpec(blk(w), fwd_map), pl.BlockSpec(blk(w), bwd_map)]
        out_specs = [pl.BlockSpec(blk(out_width), fwd_map), pl.BlockSpec(blk(out_width), bwd_map)]
        out_shape = [jax.ShapeDtypeStruct(view + (out_width,), F32)] * 2
        aliases = {}
        if s0 is not None:
            args += [s0] + [o.reshape(view + (out_width,)) for o in prev_out]
            in_specs += [_full(s0.shape), pl.BlockSpec(memory_space=pl.ANY), pl.BlockSpec(memory_space=pl.ANY)]
            aliases = {len(args) - 2: 0, len(args) - 1: 1}
        else:
            out_specs.append(pl.BlockSpec((1, N_CHAIN) + state_dims, lambda p, c: (p, 0, 0, 0)))
            out_shape.append(jax.ShapeDtypeStruct((grid[0], N_CHAIN) + state_dims, F32))
        return pl.pallas_call(
            functools.partial(kernel_fn, has_s0=s0 is not None, has_sfin=s0 is None),
            grid=grid, in_specs=in_specs, out_specs=out_specs, out_shape=out_shape,
            input_output_aliases=aliases, scratch_shapes=[pltpu.VMEM((N_CHAIN,) + state_dims, F32)],
            compiler_params=pltpu.CompilerParams(dimension_semantics=("parallel", "arbitrary")),
            name=name + ("_latent" if s0 is not None else "_context"),
        )(*args)

    ctx_nc = CTX_LEN // CHUNK
    ctx_view = (N_TOK // (PAIR * CTX_LEN), PAIR, ctx_nc, CHUNK)
    o_f, o_b, s_fin = run(ctx_view, (N_CTX_SEQ // PAIR, ctx_nc), lambda p: p, None, None)
    lat_nc = LAT_LEN // CHUNK
    lat_view = (N_TOK // (PAIR * LAT_LEN), PAIR, lat_nc, CHUNK)
    o_f, o_b = run(lat_view, (1, lat_nc), lambda p: N_CTX // (PAIR * LAT_LEN), s0_lat, (o_f, o_b))
    return o_f.reshape(N_TOK, out_width), o_b.reshape(N_TOK, out_width), s_fin


def _layer_norm(x, g, b):
    mu = jnp.mean(x, axis=-1, keepdims=True)
    xc = x - mu
    var = jnp.mean(xc * xc, axis=-1, keepdims=True)
    return xc * lax.rsqrt(var + LN_EPS) * g + b


def _merge_kernel(x_ref, mod_ref, oa_ref, ob_ref, cof_ref, cob_ref, cgate_ref, yf_ref, yb_ref, bonus_ref, dgate_ref,
                  wg_ref, wbr_ref, wout_ref, cnorm_ref, dlng_ref, dlnb_ref, lng_ref, lnb_ref, wr_ref, br_ref, bd_ref,
                  x1_o, h2_o, topi_o, topw_o):
    x = x_ref[...]
    m = mod_ref[0]
    sh1, sc1, g1, sh2, sc2 = m[0:1], m[1:2], m[2:3], m[3:4], m[4:5]
    bd = bd_ref[...]
    inv_n = 1.0 / D_N
    co = cof_ref[...] + cob_ref[...]
    o_c = co * lax.rsqrt(_dot_exact_rhs(co * co, bd) * inv_n + RMS_EPS) * cnorm_ref[...] * cgate_ref[...]
    y = yf_ref[...] + yb_ref[...]
    yc = y - _dot_exact_rhs(y, bd) * inv_n
    var = _dot_exact_rhs(yc * yc, bd) * inv_n
    o_d = (yc * lax.rsqrt(var + D_GN_EPS) * dlng_ref[...] + dlnb_ref[...] + bonus_ref[...]) * dgate_ref[...]
    branches = [b.astype(BF16) for b in (oa_ref[...], ob_ref[...], o_c, o_d)]
    h = (x * (1.0 + sc1) + sh1).astype(BF16)
    blocks = []
    for cb in range(D_MODEL // MXU_WIDTH):
        merged = None
        for n in range(N_BRANCH):
            cols = slice(n * D_MODEL + cb * MXU_WIDTH, n * D_MODEL + (cb + 1) * MXU_WIDTH)
            gate = _sigmoid(jnp.dot(h, wg_ref[0, :, cols], preferred_element_type=F32))
            term = gate * jnp.dot(branches[n], wbr_ref[0, n, :, cb * MXU_WIDTH:(cb + 1) * MXU_WIDTH],
                                  preferred_element_type=F32)
            merged = term if merged is None else merged + term
        blocks.append(merged.astype(BF16))
    mix = jnp.dot(jnp.concatenate(blocks, axis=1), wout_ref[0], preferred_element_type=F32)
    x1 = _layer_norm(ALPHA * x + g1 * mix, lng_ref[...], lnb_ref[...])
    x1_o[...] = x1
    h2 = x1 * (1.0 + sc2) + sh2
    h2_o[...] = h2.astype(BF16)
    logits = _dot3(h2, wr_ref[...]) + br_ref[...]
    tm, n_e = logits.shape
    lane_e = lax.broadcasted_iota(jnp.int32, (tm, n_e), 1)
    lane_o = lax.broadcasted_iota(jnp.int32, (tm, topi_o.shape[1]), 1)
    top_i = jnp.zeros((tm, topi_o.shape[1]), jnp.int32)
    top_v = jnp.zeros((tm, topw_o.shape[1]), F32)
    vals = []
    for kth in range(TOP_K):
        mx = jnp.max(logits, axis=-1, keepdims=True)
        idx = jnp.min(jnp.where(logits == mx, lane_e, n_e), axis=-1, keepdims=True)
        vals.append(mx)
        top_i = jnp.where(lane_o == kth, idx, top_i)
        logits = jnp.where(lane_e == idx, -jnp.inf, logits)
    es = [jnp.exp(vk - vals[0]) for vk in vals]
    den = es[0] + es[1] + es[2] + es[3]
    for kth in range(TOP_K):
        top_v = jnp.where(lane_o == kth, es[kth] / den, top_v)
    topi_o[...] = top_i
    topw_o[...] = top_v


def merge_and_route(x, mod, o_a, o_b, co_f, co_b, cgate, y_f, y_b, bonus, dgate, w_g, w_br, w_out, layer, sp):
    tm = ROW_TILE
    hn = D_HEADS * D_N
    row = lambda w: pl.BlockSpec((tm, w), lambda t: (t, 0))
    small = [sp[k] for k in ('c_norm', 'd_ln_g', 'd_ln_b', 'ln_g', 'ln_b', 'w_router', 'b_router')]
    index = [layer, layer, layer, 2 * layer, 2 * layer, layer, layer]
    bd = _head_block_diag()
    return pl.pallas_call(
        _merge_kernel,
        grid=(N_TILES,),
        in_specs=([row(D_MODEL), _mod_spec(layer)]
                  + [row(hn)] * 9 + [_layer_block(w, layer) for w in (w_g, w_br, w_out)]
                  + [_layer_item(a, i) for a, i in zip(small, index)] + [_full(bd.shape)]),
        out_specs=[row(D_MODEL), row(D_MODEL), row(128), row(128)],
        out_shape=[jax.ShapeDtypeStruct((N_TOK, D_MODEL), F32), jax.ShapeDtypeStruct((MOE_ROWS, D_MODEL), BF16),
                   jax.ShapeDtypeStruct((N_TOK, 128), jnp.int32), jax.ShapeDtypeStruct((N_TOK, 128), F32)],
        compiler_params=pltpu.CompilerParams(dimension_semantics=("parallel",), vmem_limit_bytes=VMEM_LIMIT),
        name="merge_and_route",
    )(x, mod, o_a, o_b, co_f, co_b, cgate, y_f, y_b, bonus, dgate, w_g, w_br, w_out, *small, bd)


def _moe_kernel(te_ref, tv_ref, first_ref, slot_ref, next_ref, x_ref, w1_hbm, b1_ref, w2_hbm, b2_ref, perm_ref, *rest,
                layer):
    y_ref, w1buf, w2buf, sem, w1s, w2s, hs = rest[-7:]
    t = pl.program_id(0)
    valid = tv_ref[t] != 0
    d_model, two_f = w1s.shape
    n_blk = two_f // MXU_WIDTH
    half = MXU_WIDTH // 2

    def fetch(expert, slot):
        return (pltpu.make_async_copy(w1_hbm.at[layer, expert], w1buf.at[slot], sem.at[0, slot]),
                pltpu.make_async_copy(w2_hbm.at[layer, expert], w2buf.at[slot], sem.at[1, slot]))

    @pl.when(t == 0)
    def _():
        for cp in fetch(te_ref[0], 0):
            cp.start()

    @pl.when(first_ref[t] == 1)
    def _():
        slot = slot_ref[t]
        for cp in fetch(te_ref[t], slot):
            cp.wait()

        @pl.when(next_ref[t] >= 0)
        def _():
            for cp in fetch(next_ref[t], 1 - slot):
                cp.start()

        for blk in range(n_blk):
            sl = slice(blk * MXU_WIDTH, (blk + 1) * MXU_WIDTH)
            wb = w1buf[slot, :, sl].astype(BF16)
            w1s[:, sl] = jnp.dot(wb, perm_ref[...], preferred_element_type=F32).astype(BF16)
        w2s[...] = w2buf[slot].astype(BF16)

    @pl.when(valid)
    def _():
        x = x_ref[...]
        for blk in range(n_blk):
            sl = slice(blk * MXU_WIDTH, (blk + 1) * MXU_WIDTH)
            u = jnp.dot(x, w1s[:, sl], preferred_element_type=F32) + b1_ref[0, 0, :, sl]
            glu = jnp.minimum(u[:, :half], SWIGLU_LIMIT)
            lin = jnp.clip(u[:, half:], -SWIGLU_LIMIT, SWIGLU_LIMIT)
            hs[:, blk * half:(blk + 1) * half] = (glu * _sigmoid(SWIGLU_ALPHA * glu) * (lin + 1.0)).astype(BF16)
        y = jnp.dot(hs[...], w2s[...], preferred_element_type=F32) + b2_ref[0, 0]
        y_ref[...] = y.astype(y_ref.dtype)

    @pl.when(jnp.logical_not(valid))
    def _():
        y_ref[...] = jnp.zeros_like(y_ref)


def _deinterleave_perm():
    half = MXU_WIDTH // 2
    src = np.arange(MXU_WIDTH)
    dst = np.where(src % 2 == 0, src // 2, half + src // 2)
    p = np.zeros((MXU_WIDTH, MXU_WIDTH), np.float32)
    p[src, dst] = 1.0
    return jnp.asarray(p, BF16)


def _moe_dispatch(top_i):
    n, k = top_i.shape
    tm = MOE_ROW_TILE
    p_rows = n * k + N_EXPERTS * tm
    experts = jnp.arange(N_EXPERTS, dtype=jnp.int32)
    onehot = top_i[:, :, None] == experts
    sel = jnp.sum(onehot.astype(jnp.int32), axis=1)
    before = jnp.cumsum(sel, axis=0) - sel
    counts = jnp.sum(sel, axis=0)
    padded = ((counts + tm - 1) // tm) * tm
    ends = jnp.cumsum(padded)
    starts = ends - padded
    pos = jnp.sum(jnp.where(onehot, (before + starts)[:, None, :], 0), axis=-1)
    n_tiles = p_rows // tm
    tile_start = jnp.arange(n_tiles, dtype=jnp.int32) * tm
    tile_valid = (tile_start < ends[-1]).astype(jnp.int32)
    last_tile = ends[-1] // tm - 1
    tile_expert = jnp.sum(ends[None, :] <= jnp.minimum(tile_start, last_tile * tm)[:, None], axis=1).astype(jnp.int32)
    keys = jnp.sort((top_i * n + jnp.arange(n, dtype=jnp.int32)[:, None]).reshape(-1))
    tile_onehot = tile_expert[:, None] == experts[None, :]
    lookup = lambda table: jnp.sum(jnp.where(tile_onehot, table[None, :], 0), axis=1)
    tile_rank0 = tile_start - lookup(starts)
    rank = tile_rank0[:, None] + jnp.arange(tm, dtype=jnp.int32)[None, :]
    sorted_at = jnp.clip(lookup(jnp.cumsum(counts) - counts)[:, None] + rank, 0, n * k - 1)
    tile_keys = keys[sorted_at.reshape(-1)].reshape(n_tiles, tm)
    filler = (tile_start[:, None] + jnp.arange(tm, dtype=jnp.int32)[None, :]) % n
    src_tok = jnp.where(rank < lookup(counts)[:, None], tile_keys % n, filler)
    nst = n_tiles // MOE_SPLITS
    idx = jnp.arange(nst, dtype=jnp.int32)
    tables = []
    for h in range(MOE_SPLITS):
        te_h = tile_expert[h * nst:(h + 1) * nst]
        is_first = jnp.concatenate([jnp.ones((1,), jnp.int32), (te_h[1:] != te_h[:-1]).astype(jnp.int32)])
        slot = (jnp.cumsum(is_first) - 1) % 2
        nxt = jnp.min(jnp.where(jnp.logical_and(idx[None, :] > idx[:, None], is_first[None, :] == 1),
                                idx[None, :], nst), axis=1)
        next_expert = jnp.sum(jnp.where(idx[None, :] == nxt[:, None], te_h[None, :] + 1, 0), axis=1) - 1
        tables.append((te_h, tile_valid[h * nst:(h + 1) * nst], is_first, slot.astype(jnp.int32),
                       next_expert.astype(jnp.int32)))
    return pos, src_tok.reshape(-1), tables, p_rows


def moe_experts(h2, top_i, layer, w1, b1, w2, b2):
    n = top_i.shape[0]
    d = h2.shape[1]
    depth, e, _, two_f = w1.shape
    f = two_f // 2
    tm = MOE_ROW_TILE
    pos, src_tok, tables, p_rows = _moe_dispatch(top_i)
    assert h2.shape[0] == p_rows
    src_tok = lax.optimization_barrier(src_tok)
    b1p = b1.reshape(depth, e, two_f // MXU_WIDTH, MXU_WIDTH // 2, 2).swapaxes(3, 4).reshape(depth, e, 1, two_f)
    b2r = b2.reshape(depth, e, 1, d)
    expert_vec = lambda w: pl.BlockSpec((1, 1, 1, w), lambda t, te, *_: (layer, te[t], 0, 0))
    rows = p_rows // MOE_SPLITS
    nst = rows // tm
    xs = [h2.at[src_tok[h * rows:(h + 1) * rows]].get(mode="promise_in_bounds") for h in range(MOE_SPLITS)]
    ys = None
    for h in range(MOE_SPLITS):
        in_specs = [
            pl.BlockSpec((tm, d), lambda t, *_: (t, 0)),
            pl.BlockSpec(memory_space=pl.ANY),
            expert_vec(two_f),
            pl.BlockSpec(memory_space=pl.ANY),
            expert_vec(d),
            pl.BlockSpec((MXU_WIDTH, MXU_WIDTH), lambda t, *_: (0, 0)),
        ]
        args = [*tables[h], xs[h], w1, b1p, w2, b2r, _deinterleave_perm()]
        aliases = {}
        if ys is not None:
            in_specs.append(pl.BlockSpec(memory_space=pl.ANY))
            aliases = {len(args): 0}
            args.append(ys)
        ys = pl.pallas_call(
            functools.partial(_moe_kernel, layer=layer),
            grid_spec=pltpu.PrefetchScalarGridSpec(
                num_scalar_prefetch=len(tables[h]), grid=(nst,), in_specs=in_specs,
                out_specs=pl.BlockSpec((tm, d), lambda t, *_, h=h: (h * nst + t, 0)),
                scratch_shapes=[pltpu.VMEM((2, d, two_f), F32), pltpu.VMEM((2, f, d), F32),
                                pltpu.SemaphoreType.DMA((2, 2)),
                                pltpu.VMEM((d, two_f), BF16), pltpu.VMEM((f, d), BF16), pltpu.VMEM((tm, f), BF16)]),
            out_shape=jax.ShapeDtypeStruct((p_rows, d), BF16),
            input_output_aliases=aliases,
            compiler_params=pltpu.CompilerParams(dimension_semantics=("arbitrary",),
                                                 vmem_limit_bytes=48 * 1024 * 1024),
            name="moe_experts",
        )(*args)
    return ys.at[lax.optimization_barrier(pos.T.reshape(-1))].get(mode="promise_in_bounds").reshape(TOP_K, n, d)


def _final_kernel(x1_ref, mod_ref, ys_ref, topw_ref, lng_ref, lnb_ref, o_ref):
    g2 = mod_ref[0, 5:6]
    moe = None
    for kth in range(TOP_K):
        term = ys_ref[kth].astype(F32) * topw_ref[:, kth:kth + 1]
        moe = term if moe is None else moe + term
    o_ref[...] = _layer_norm(ALPHA * x1_ref[...] + g2 * moe, lng_ref[...], lnb_ref[...])


def combine_and_norm(x1, mod, ys, top_w, layer, sp):
    tm = ROW_TILE
    ln_g, ln_b = sp['ln_g'], sp['ln_b']
    return pl.pallas_call(
        _final_kernel,
        grid=(N_TILES,),
        in_specs=[pl.BlockSpec((tm, D_MODEL), lambda t: (t, 0)),
                  _mod_spec(layer),
                  pl.BlockSpec((TOP_K, tm, D_MODEL), lambda t: (0, t, 0)),
                  pl.BlockSpec((tm, 128), lambda t: (t, 0)),
                  _layer_item(ln_g, 2 * layer + 1), _layer_item(ln_b, 2 * layer + 1)],
        out_specs=pl.BlockSpec((tm, D_MODEL), lambda t: (t, 0)),
        out_shape=jax.ShapeDtypeStruct((N_TOK, D_MODEL), F32),
        compiler_params=pltpu.CompilerParams(dimension_semantics=("parallel",)),
        name="combine_and_norm",
    )(x1, mod, ys, top_w, ln_g, ln_b)


def kernel(x_prompt, x_sample, cache_a_k, cache_a_v, cache_b_ckv, cache_b_kpe, state_c, state_d, c,
           c_ctx, w_mod, b_mod, w_in, a_sink, b_q_norm, b_w_uq, b_kv_norm, b_w_ukv, c_w_gate, c_b_gate,
           c_norm, d_mu, d_w0, d_w2, d_a0, d_a2, d_g2, d_k_k, d_k_a, d_r_k, d_ln_g, d_ln_b, w_br, w_out,
           ln_g, ln_b, w_router, b_router, w_mlp1, b_mlp1, w_mlp2, b_mlp2):
    sp = stacked_params(dict(
        a_sink=a_sink, b_q_norm=b_q_norm, b_w_uq=b_w_uq, b_kv_norm=b_kv_norm, b_w_ukv=b_w_ukv, c_w_gate=c_w_gate,
        c_b_gate=c_b_gate, c_norm=c_norm, d_mu=d_mu, d_w0=d_w0, d_w2=d_w2, d_a0=d_a0, d_a2=d_a2, d_g2=d_g2,
        d_k_k=d_k_k, d_k_a=d_k_a, d_r_k=d_r_k, d_ln_g=d_ln_g, d_ln_b=d_ln_b, ln_g=ln_g, ln_b=ln_b,
        w_router=w_router, b_router=b_router))
    assert x_prompt.shape == (N_CTX_SEQ, CTX_LEN, D_MODEL) and x_sample.shape == (N_LAT_SEQ, LAT_LEN, D_MODEL)
    x = jnp.concatenate([x_prompt.reshape(N_CTX, D_MODEL), x_sample.reshape(-1, D_MODEL)], axis=0)
    cond8 = jnp.concatenate([c_ctx[None], c, jnp.zeros((8 - 1 - N_LAT_SEQ, D_MODEL), F32)], axis=0)
    mod = modulation_table(cond8, w_mod, b_mod)[:, :1 + N_LAT_SEQ].reshape(DEPTH * (1 + N_LAT_SEQ), 6, D_MODEL)
    tables = _rope_tables()
    w_small, w_g = prepare_in_weights(w_in)
    w_br_bf, w_out_bf = w_br.astype(BF16), w_out.astype(BF16)
    new = {name: [] for name in ("a_k", "a_v", "b_ckv", "b_kpe", "c", "d")}
    for l in range(DEPTH):
        (aq, ak, av, bq, bckv, bkpe, cq4, ck4, cla_f, cla_b, cv, cgate,
         r, v, kk, lw_f, lw_b, k_f, k_b, a_f, a_b, bonus, dgate) = mixer_prelude(x, mod, w_small, l, tables, sp)

        o_a = gqa_attention(aq, ak, av, sp['a_sink'], cache_a_k, cache_a_v, l)
        o_b = mla_attention(bq, bckv, bkpe, sp['b_w_ukv'], cache_b_ckv, cache_b_kpe, l)

        c_s0 = jnp.swapaxes(state_c[:, l], 3, 4).reshape(1, N_CHAIN, C_DV, C_DK)
        co_f, co_b, c_fin = _recurrence_calls(functools.partial(_gla_kernel, dot=_dot1), "gla", [cq4, ck4, cv],
                                              [cla_f], [cla_b], c_s0, (C_DV, C_DK), C_HEADS * C_DV)
        d_s0 = state_d[:, l].reshape(1, N_CHAIN, D_N, D_N)
        y_f, y_b, d_fin = _recurrence_calls(functools.partial(_rwkv_kernel, dot=_dot1), "rwkv7", [r, v, kk],
                                            [lw_f, k_f, a_f], [lw_b, k_b, a_b], d_s0, (D_N, D_N), D_HEADS * D_N)

        x1, h2, top_i, top_w = merge_and_route(x, mod, o_a, o_b, co_f, co_b, cgate, y_f, y_b, bonus, dgate,
                                               w_g, w_br_bf, w_out_bf, l, sp)
        ys = moe_experts(h2, top_i[:, :TOP_K], l, w_mlp1, b_mlp1, w_mlp2, b_mlp2)
        x = combine_and_norm(x1, mod, ys, top_w, l, sp)

        new["a_k"].append(ak[:N_CTX].reshape(N_CTX_SEQ, CTX_LEN, A_KV_HEADS, A_HD).transpose(0, 2, 1, 3))
        new["a_v"].append(av[:N_CTX].reshape(N_CTX_SEQ, CTX_LEN, A_KV_HEADS, A_HD).transpose(0, 2, 1, 3))
        new["b_ckv"].append(bckv[:N_CTX].reshape(N_CTX_SEQ, CTX_LEN, B_KV_LORA))
        new["b_kpe"].append(bkpe[:N_CTX].reshape(N_CTX_SEQ, CTX_LEN, B_ROPE))
        new["c"].append(jnp.swapaxes(c_fin.reshape(N_CTX_SEQ, 2, C_HEADS, C_DV, C_DK), 3, 4))
        new["d"].append(d_fin.reshape(N_CTX_SEQ, 2, D_HEADS, D_N, D_N))
    y_prompt = x[:N_CTX].reshape(x_prompt.shape)
    y_sample = x[N_CTX:].reshape(x_sample.shape)
    return (y_prompt, y_sample, *(jnp.stack(new[name], axis=1) for name in ("a_k", "a_v", "b_ckv", "b_kpe", "c", "d")))
```

```python
import functools

import jax
import jax.numpy as jnp
import numpy as np
from jax import lax
from jax.experimental import pallas as pl
from jax.experimental.pallas import tpu as pltpu

F32 = jnp.float32
BF16 = jnp.bfloat16

MXU_WIDTH = 256
VMEM_LIMIT = 56 * 1024 * 1024

D_MODEL = 1024
DEPTH = 2
GRID_W = 64
ROPE_BASE = 10000.0
A_HEADS, A_KV_HEADS, A_HD = 4, 2, 64
B_HEADS, B_NOPE, B_ROPE, B_VD, B_Q_LORA, B_KV_LORA = 4, 64, 32, 64, 192, 128
C_HEADS, C_DK, C_DV, C_GATE_RANK, C_GATE_TEMP = 4, 32, 64, 16, 16.0
D_HEADS, D_N, D_DECAY_RANK, D_AAA_RANK, D_GATE_RANK, D_GN_EPS = 4, 64, 64, 64, 128, 64e-5
BRANCH_W = 256
N_BRANCH = 4
N_EXPERTS = 32
TOP_K = 4
SWIGLU_LIMIT = 7.0
SWIGLU_ALPHA = 1.702
ALPHA = (2 * DEPTH) ** 0.25
LN_EPS = 1e-5
RMS_EPS = 1e-6

N_CTX_SEQ, CTX_LEN = 16, 256
N_LAT_SEQ, LAT_LEN = 2, 2048
N_CTX = N_CTX_SEQ * CTX_LEN
N_TOK = N_CTX + N_LAT_SEQ * LAT_LEN
ROW_TILE = 256
N_TILES = N_TOK // ROW_TILE
CTX_TILES = N_CTX // ROW_TILE
LAT_TILES_PER_SEQ = LAT_LEN // ROW_TILE
N_SEQ = N_CTX_SEQ + N_LAT_SEQ
MLA_HEAD_LANES = 128
MOE_ROW_TILE = 256
MOE_SPLITS = 2
MOE_ROWS = N_TOK * TOP_K + N_EXPERTS * MOE_ROW_TILE

_ORIG = dict(aq=(0, 256), ak=(256, 384), av=(384, 512), bcq=(512, 704), bckv=(704, 832), bkpe=(832, 864),
             cq=(864, 992), ck=(992, 1120), cv=(1120, 1376), cog=(1376, 1632), caf=(1632, 1648), cab=(1648, 1664),
             zd=(1664, 2816))
_ORDER = ("aq", "ak", "av", "cq", "ck", "cv", "cog", "zd", "bcq", "caf", "cab", "bkpe", "bckv")
COL = {}
_off = 0
for _name in _ORDER:
    _w = _ORIG[_name][1] - _ORIG[_name][0]
    COL[_name] = (_off, _off + _w)
    _off += _w
SMALL_COLS = _off
G_START = 2816


def _cs(name):
    return slice(*COL[name])


def _split3(x):
    hi = x.astype(BF16)
    r1 = x - hi.astype(F32)
    mid = r1.astype(BF16)
    lo = (r1 - mid.astype(F32)).astype(BF16)
    return hi, mid, lo


def _split2(x):
    hi = x.astype(BF16)
    lo = (x - hi.astype(F32)).astype(BF16)
    return hi, lo


def _bdot(a, b, dims):
    return lax.dot_general(a, b, dims, preferred_element_type=F32)


_D2 = (((1,), (0,)), ((), ()))
_D2T = (((1,), (1,)), ((), ()))
_NN = (((2,), (1,)), ((0,), (0,)))
_NT = (((2,), (2,)), ((0,), (0,)))
_TN = (((1,), (1,)), ((0,), (0,)))


def _dot1(a, b, dims=_D2):
    return _bdot(a.astype(BF16), b.astype(BF16), dims)


def _dot3(a, b, dims=_D2):
    ah, al = _split2(a)
    bh, bl = _split2(b)
    return _bdot(ah, bh, dims) + (_bdot(ah, bl, dims) + _bdot(al, bh, dims))


def _dot_exact_lhs(a01, b, dims=_D2):
    a = a01.astype(BF16)
    h, m, l = _split3(b)
    return _bdot(a, h, dims) + (_bdot(a, m, dims) + _bdot(a, l, dims))


def _dot_exact_rhs(a, b01, dims=_D2):
    b = b01.astype(BF16)
    h, l = _split2(a)
    return _bdot(h, b, dims) + _bdot(l, b, dims)


def _sigmoid(x):
    return 0.5 * jnp.tanh(0.5 * x) + 0.5


def _softplus(x):
    return jnp.maximum(x, 0.0) + jnp.log(1.0 + jnp.exp(-jnp.abs(x)))


def _mod_row(t):
    return jnp.where(t < CTX_TILES, 0, 1 + (t - CTX_TILES) // LAT_TILES_PER_SEQ)


def _full(shape):
    nd = len(shape)
    return pl.BlockSpec(shape, lambda *_: (0,) * nd)


def _ctx_rows_spec(width):
    return pl.BlockSpec((ROW_TILE, width), lambda t: (jnp.minimum(t, CTX_TILES - 1), 0))


def _lat_rows_spec(width):
    return pl.BlockSpec((ROW_TILE, width), lambda t: (jnp.maximum(t - CTX_TILES, 0), 0))


def _tile_rows(t, ctx_ref, lat_ref):
    return jnp.where(t < CTX_TILES, ctx_ref[...], lat_ref[...])


MOD_COL_TILE = 1536


def _mod_kernel(c_ref, w_ref, b_ref, o_ref):
    c = c_ref[...]
    o_ref[0] = _dot3(c * _sigmoid(c), w_ref[0]) + b_ref[0]


def modulation_table(cond8, w_mod, b_mod):
    depth, d, six_d = w_mod.shape
    return pl.pallas_call(
        _mod_kernel,
        grid=(depth, six_d // MOD_COL_TILE),
        in_specs=[pl.BlockSpec((8, d), lambda l, j: (0, 0)),
                  pl.BlockSpec((1, d, MOD_COL_TILE), lambda l, j: (l, 0, j)),
                  pl.BlockSpec((1, 1, MOD_COL_TILE), lambda l, j: (l, 0, j))],
        out_specs=pl.BlockSpec((1, 8, MOD_COL_TILE), lambda l, j: (l, 0, j)),
        out_shape=jax.ShapeDtypeStruct((depth, 8, six_d), F32),
        compiler_params=pltpu.CompilerParams(dimension_semantics=("parallel", "parallel")),
        name="modulation",
    )(cond8, w_mod, b_mod.reshape(depth, 1, six_d))


WPREP_ROWS = 128


def _wprep_kernel(w_ref, small_ref, gate_ref):
    for name in _ORDER:
        lo, hi = _ORIG[name]
        small_ref[0, :, _cs(name)] = w_ref[0, :, lo:hi].astype(BF16)
    gate_ref[0] = w_ref[0, :, G_START:].astype(BF16)


def prepare_in_weights(w_in):
    depth, d, cols = w_in.shape
    return pl.pallas_call(
        _wprep_kernel,
        grid=(depth, d // WPREP_ROWS),
        in_specs=[pl.BlockSpec((1, WPREP_ROWS, cols), lambda l, r: (l, r, 0))],
        out_specs=[pl.BlockSpec((1, WPREP_ROWS, SMALL_COLS), lambda l, r: (l, r, 0)),
                   pl.BlockSpec((1, WPREP_ROWS, cols - G_START), lambda l, r: (l, r, 0))],
        out_shape=[jax.ShapeDtypeStruct((depth, d, SMALL_COLS), BF16),
                   jax.ShapeDtypeStruct((depth, d, cols - G_START), BF16)],
        compiler_params=pltpu.CompilerParams(dimension_semantics=("parallel", "parallel")),
        name="prepare_in_weights",
    )(w_in)


def _rot_pairs(x, half, lane_mod_base=0):
    w = x.shape[-1]
    lane = lax.broadcasted_iota(jnp.int32, (1, w), 1) - lane_mod_base
    first = (lane % (2 * half)) < half
    return jnp.where(first, -pltpu.roll(x, w - half, axis=1), pltpu.roll(x, half, axis=1))


def _pre_kernel(xc_ref, xl_ref, xp_ref, xn_ref, mod_ref, w_ref, ca_ref, sa_ref, cb_ref, sb_ref, ck_ref, sk_ref,
                qnorm_ref, kvnorm_ref, wuq_ref, cwg_ref, cbg_ref, rep_ref, mu_ref, dw0_ref, dw2_ref, da0_ref,
                da2_ref, dg2_ref, dkk_ref, dka_ref, drk_ref, bd_ref,
                aq_o, ak_o, av_o, bq_o, bckv_o, bkpe_o, cq4_o, ck4_o, claf_o, clab_o, cv_o, cgate_o,
                r_o, v_o, kk_o, lwf_o, lwb_o, kf_o, kb_o, af_o, ab_o, bonus_o, dgate_o):
    t = pl.program_id(0)
    tm = xc_ref.shape[0]
    sh1 = mod_ref[0, 0:1, :]
    sc1 = mod_ref[0, 1:2, :]

    def modulate(xv):
        return (xv * (1.0 + sc1) + sh1).astype(BF16)

    x_tile = _tile_rows(t, xc_ref, xl_ref)
    h_all = jnp.concatenate([modulate(x_tile), modulate(xp_ref[...]), modulate(xn_ref[...])], axis=0)
    z_all = jnp.dot(h_all, w_ref[0], preferred_element_type=F32)
    z = z_all[:tm]

    aq = z[:, _cs("aq")]
    ak = z[:, _cs("ak")]
    aq_o[...] = aq * ca_ref[...] + _rot_pairs(aq, A_HD // 4) * sa_ref[...]
    ak_o[...] = ak * ca_ref[:, :ak.shape[1]] + _rot_pairs(ak, A_HD // 4) * sa_ref[:, :ak.shape[1]]
    av_o[...] = z[:, _cs("av")]

    bcq = z[:, _cs("bcq")]
    qn = bcq * lax.rsqrt(jnp.mean(bcq * bcq, axis=-1, keepdims=True) + RMS_EPS) * qnorm_ref[...]
    bq = _dot1(qn, wuq_ref[...])
    bq_o[...] = bq * cb_ref[...] + _rot_pairs(bq, B_ROPE // 4, lane_mod_base=B_NOPE) * sb_ref[...]
    bckv = z[:, _cs("bckv")]
    bckv_o[...] = bckv * lax.rsqrt(jnp.mean(bckv * bckv, axis=-1, keepdims=True) + RMS_EPS) * kvnorm_ref[...]
    kpe_lo = COL["bkpe"][0] // 128 * 128
    kblk = z[:, kpe_lo:kpe_lo + 128]
    kblk = kblk * ck_ref[...] + _rot_pairs(kblk, B_ROPE // 4) * sk_ref[...]
    bkpe_o[...] = kblk[:, COL["bkpe"][0] - kpe_lo:COL["bkpe"][1] - kpe_lo]

    rep = rep_ref[...]
    cq4_o[...] = _dot1(z[:, _cs("cq")] * (C_DK ** -0.5), rep)
    ck4_o[...] = _dot1(z[:, _cs("ck")], rep)
    cv_o[...] = z[:, _cs("cv")]
    cog = z[:, _cs("cog")]
    cgate_o[...] = cog * _sigmoid(cog)
    for direction, (name, out) in enumerate((("caf", claf_o), ("cab", clab_o))):
        pre = _dot3(z[:, _cs(name)], cwg_ref[direction]) + cbg_ref[direction]
        la_hi, la_lo = _split2(-_softplus(-pre) * (1.0 / C_GATE_TEMP))
        out[...] = _bdot(la_hi, rep, _D2) + _bdot(la_lo, rep, _D2)

    zd_cols = _cs("zd")
    zd = z[:, zd_cols]
    j = (t - CTX_TILES) % LAT_TILES_PER_SEQ
    latent = t >= CTX_TILES
    has_prev = jnp.logical_and(latent, j != 0)
    has_next = jnp.logical_and(latent, j != LAT_TILES_PER_SEQ - 1)
    prev_row = jnp.where(has_prev, z_all[tm + 7:tm + 8, zd_cols], 0.0)
    next_row = jnp.where(has_next, z_all[tm + 8:tm + 9, zd_cols], 0.0)
    row = lax.broadcasted_iota(jnp.int32, (tm, 1), 0)
    up = jnp.where(row == 0, prev_row, pltpu.roll(zd, 1, axis=0))
    dn = jnp.where(row == tm - 1, next_row, pltpu.roll(zd, tm - 1, axis=0))
    zd = zd + (0.5 * (up + dn) - zd) * mu_ref[...]

    hn = D_HEADS * D_N
    d_r, d_k, d_v = zd[:, :hn], zd[:, hn:2 * hn], zd[:, 2 * hn:3 * hn]
    o = 3 * hn
    d_w = (zd[:, o:o + D_DECAY_RANK], zd[:, o + D_DECAY_RANK:o + 2 * D_DECAY_RANK])
    o += 2 * D_DECAY_RANK
    d_a = (zd[:, o:o + D_AAA_RANK], zd[:, o + D_AAA_RANK:o + 2 * D_AAA_RANK])
    o += 2 * D_AAA_RANK
    d_g = zd[:, o:o + D_GATE_RANK]
    bd = bd_ref[...]
    kk = d_k * dkk_ref[...]
    kk = kk / jnp.maximum(jnp.sqrt(_dot_exact_rhs(kk * kk, bd)), 1e-12)
    r_o[...] = d_r
    v_o[...] = d_v
    kk_o[...] = kk
    k_sum = None
    for direction, (lw_o, k_o, a_o) in enumerate(((lwf_o, kf_o, af_o), (lwb_o, kb_o, ab_o))):
        w_log = -_softplus(-(dw0_ref[direction] + _dot3(jnp.tanh(d_w[direction]), dw2_ref[direction]))) - 0.5
        lw_o[...] = -jnp.exp(w_log)
        a = _sigmoid(da0_ref[direction] + _dot1(d_a[direction], da2_ref[direction]))
        k_dir = d_k * (1.0 + (a - 1.0) * dka_ref[...])
        k_o[...] = k_dir
        a_o[...] = a
        k_sum = k_dir if k_sum is None else k_sum + k_dir
    bonus_o[...] = d_v * _dot_exact_rhs(d_r * drk_ref[...] * k_sum, bd)
    dgate_o[...] = _dot1(_sigmoid(d_g), dg2_ref[...])


def _rope_tables():
    pos = np.arange(LAT_LEN)
    rowp, colp = (pos // GRID_W).astype(np.float32), (pos % GRID_W).astype(np.float32)

    f32 = np.float32

    def head_tables(rot_dim):
        quarter = rot_dim // 4
        inv = (f32(ROPE_BASE) ** (-np.arange(quarter, dtype=f32) / f32(quarter))).astype(f32)
        ar = (rowp[:, None] * inv).astype(f32)
        ac = (colp[:, None] * inv).astype(f32)
        cos = np.concatenate([np.cos(ar), np.cos(ar), np.cos(ac), np.cos(ac)], axis=-1).astype(f32)
        sin = np.concatenate([np.sin(ar), np.sin(ar), np.sin(ac), np.sin(ac)], axis=-1).astype(f32)
        return cos, sin

    def with_identity(c, s):
        w = c.shape[1]
        return (jnp.asarray(np.concatenate([np.ones((ROW_TILE, w), f32), c], axis=0)),
                jnp.asarray(np.concatenate([np.zeros((ROW_TILE, w), f32), s], axis=0)))

    ca, sa = head_tables(A_HD)
    ca, sa = with_identity(np.tile(ca, (1, A_HEADS)), np.tile(sa, (1, A_HEADS)))
    cbh, sbh = head_tables(B_ROPE)
    ones, zeros = np.ones((LAT_LEN, B_NOPE), f32), np.zeros((LAT_LEN, B_NOPE), f32)
    qpad = MLA_HEAD_LANES - B_NOPE - B_ROPE
    cb, sb = with_identity(
        np.tile(np.concatenate([ones, cbh, np.ones((LAT_LEN, qpad), f32)], axis=1), (1, B_HEADS)),
        np.tile(np.concatenate([zeros, sbh, np.zeros((LAT_LEN, qpad), f32)], axis=1), (1, B_HEADS)))
    pad = 128 - B_ROPE
    ck, sk = with_identity(np.concatenate([np.ones((LAT_LEN, pad), f32), cbh], axis=1),
                           np.concatenate([np.zeros((LAT_LEN, pad), f32), sbh], axis=1))
    return ca, sa, cb, sb, ck, sk


def _lane_repeat_matrix():
    m = np.zeros((C_HEADS * C_DK, C_HEADS * 128), np.float32)
    for h in range(C_HEADS):
        for g in range(128 // C_DK):
            for d in range(C_DK):
                m[h * C_DK + d, h * 128 + g * C_DK + d] = 1.0
    return jnp.asarray(m, BF16)


def _head_block_diag():
    m = np.kron(np.eye(D_HEADS, dtype=np.float32), np.ones((D_N, D_N), np.float32))
    return jnp.asarray(m, BF16)


PRE_OUT_WIDTHS = (256, 128, 128, B_HEADS * MLA_HEAD_LANES, 128, 32, 512, 512, 512, 512, 256, 256) + (256,) * 11


def _layer_block(arr, layer):
    nd = arr.ndim
    return pl.BlockSpec((1,) + arr.shape[1:], lambda *_: (layer,) + (0,) * (nd - 1))


def _layer_item(arr, index):
    nd = arr.ndim
    return pl.BlockSpec((None,) + arr.shape[1:], lambda *_: (index,) + (0,) * (nd - 1))


def _mod_spec(layer):
    return pl.BlockSpec((1, 6, D_MODEL), lambda t: (layer * (1 + N_LAT_SEQ) + _mod_row(t), 0, 0))


def stacked_params(p):
    depth = p['d_mu'].shape[0]
    hn = D_HEADS * D_N
    w_uq = p['b_w_uq'].reshape(depth, B_Q_LORA, B_HEADS, B_NOPE + B_ROPE)
    w_uq = jnp.pad(w_uq, ((0, 0), (0, 0), (0, 0), (0, MLA_HEAD_LANES - B_NOPE - B_ROPE)))
    row = lambda a: a.reshape(depth, 1, -1)
    return dict(
        b_q_norm=row(p['b_q_norm']), b_kv_norm=row(p['b_kv_norm']), w_uq=w_uq.reshape(depth, B_Q_LORA, -1),
        c_w_gate=p['c_w_gate'], c_b_gate=p['c_b_gate'].reshape(depth, 2, 1, -1), d_mu=row(p['d_mu']),
        d_w0=p['d_w0'].reshape(depth, 2, 1, hn), d_w2=p['d_w2'], d_a0=p['d_a0'].reshape(depth, 2, 1, hn),
        d_a2=p['d_a2'], d_g2=p['d_g2'], d_k_k=row(p['d_k_k']), d_k_a=row(p['d_k_a']), d_r_k=row(p['d_r_k']),
        c_norm=row(jnp.tile(p['c_norm'], (1, C_HEADS))), d_ln_g=row(p['d_ln_g']), d_ln_b=row(p['d_ln_b']),
        ln_g=p['ln_g'].reshape(depth * 2, 1, -1), ln_b=p['ln_b'].reshape(depth * 2, 1, -1),
        w_router=p['w_router'], b_router=row(p['b_router']), a_sink=p['a_sink'].reshape(-1), b_w_ukv=p['b_w_ukv'])


def mixer_prelude(x_ctx, x_lat, mod, w_small, layer, tables, sp):
    tm = ROW_TILE
    lat8 = lambda t: (t - CTX_TILES) * (tm // 8)
    last8 = x_lat.shape[0] // 8 - 1
    tab_idx = lambda t: (jnp.where(t < CTX_TILES, 0, 1 + (t - CTX_TILES) % LAT_TILES_PER_SEQ), 0)
    small = [sp[k] for k in ('b_q_norm', 'b_kv_norm', 'w_uq', 'c_w_gate', 'c_b_gate')] + [_lane_repeat_matrix()]
    small += [sp[k] for k in ('d_mu', 'd_w0', 'd_w2', 'd_a0', 'd_a2', 'd_g2', 'd_k_k', 'd_k_a', 'd_r_k')]
    small += [_head_block_diag()]
    const = lambda a: _full(a.shape) if a.dtype == BF16 else _layer_item(a, layer)
    in_specs = ([_ctx_rows_spec(D_MODEL), _lat_rows_spec(D_MODEL),
                 pl.BlockSpec((8, D_MODEL), lambda t: (jnp.clip(lat8(t) - 1, 0, last8), 0)),
                 pl.BlockSpec((8, D_MODEL), lambda t: (jnp.clip(lat8(t + 1), 0, last8), 0)),
                 _mod_spec(layer),
                 _layer_block(w_small, layer)]
                + [pl.BlockSpec((tm, tab.shape[1]), tab_idx) for tab in tables]
                + [const(a) for a in small])
    return pl.pallas_call(
        _pre_kernel,
        grid=(N_TILES,),
        in_specs=in_specs,
        out_specs=[pl.BlockSpec((tm, w), lambda t: (t, 0)) for w in PRE_OUT_WIDTHS],
        out_shape=[jax.ShapeDtypeStruct((N_TOK, w), F32) for w in PRE_OUT_WIDTHS],
        compiler_params=pltpu.CompilerParams(dimension_semantics=("parallel",), vmem_limit_bytes=VMEM_LIMIT),
        name="mixer_prelude",
    )(x_ctx, x_lat, x_lat, x_lat, mod, w_small, *tables, *small)


ATT_Q_BLOCK = 128
MLA_Q_BLOCK = 256
ATT_WINDOW = 128
ATT_NEG_INF = -1e30
CACHE_LEN = 512


def _softmax_pv(s, v, sink):
    dv = v.shape[1] // 2
    m = jnp.max(s, axis=-1, keepdims=True)
    if sink is not None:
        m = jnp.maximum(m, sink)
    e = jnp.exp((s - m).astype(BF16))
    o = jnp.dot(e, v, preferred_element_type=F32)
    den = o[:, dv:dv + 1]
    if sink is not None:
        den = den + jnp.exp(sink - m)
    return o[:, :dv] / den


def _with_ones(v):
    return jnp.concatenate([v.astype(BF16), jnp.ones(v.shape, BF16)], axis=1)


def _gqa_kernel(sink_ref, q_ref, k_ref, v_ref, *rest, hd, group, scale, windowed, sink_base):
    if windowed:
        kp_ref, kn_ref, vp_ref, vn_ref, kc_ref, vc_ref, _, o_ref = rest
    else:
        (o_ref,) = rest
    i = pl.program_id(1)
    tq = q_ref.shape[0]
    n_kv = k_ref.shape[1] // hd
    if windowed:
        qpos = i * tq + lax.broadcasted_iota(jnp.int32, (tq, 3 * tq), 0)
        kpos = (i - 1) * tq + lax.broadcasted_iota(jnp.int32, (tq, 3 * tq), 1)
        n_tok = pl.num_programs(1) * tq
        mask = (jnp.abs(qpos - kpos) <= ATT_WINDOW) & (kpos >= 0) & (kpos < n_tok)
        mask = jnp.concatenate([mask] * group, axis=0)
    for kvh in range(n_kv):
        ks = slice(kvh * hd, (kvh + 1) * hd)
        qs = [q_ref[:, (kvh * group + g) * hd:(kvh * group + g + 1) * hd] for g in range(group)]
        q = (jnp.concatenate(qs, axis=0) * scale).astype(BF16)
        sink = jnp.concatenate(
            [jnp.full((tq, 1), sink_ref[sink_base + kvh * group + g], F32) for g in range(group)], axis=0)
        if windowed:
            k_win = jnp.concatenate([kp_ref[:, ks], k_ref[:, ks], kn_ref[:, ks]], axis=0)
            v_win = jnp.concatenate([vp_ref[:, ks], v_ref[:, ks], vn_ref[:, ks]], axis=0)
            s_win = _bdot(q, k_win.astype(BF16), _D2T)
            s_win = jnp.where(mask, s_win, ATT_NEG_INF)
            s_ctx = _bdot(q, kc_ref[0, 0, kvh].astype(BF16), _D2T)
            s = jnp.concatenate([s_win, s_ctx], axis=1)
            v = jnp.concatenate([v_win, vc_ref[0, 0, kvh]], axis=0)
        else:
            s = _bdot(q, k_ref[:, ks].astype(BF16), _D2T)
            v = v_ref[:, ks]
        o = _softmax_pv(s, _with_ones(v), sink)
        for g in range(group):
            h = kvh * group + g
            o_ref[:, h * hd:(h + 1) * hd] = o[g * tq:(g + 1) * tq]


def gqa_attention(q, k, v, sink, cache_k, cache_v, layer):
    qw, kw = q.shape[1], k.shape[1]
    group = qw // kw
    scale = A_HD ** -0.5
    params = pltpu.CompilerParams(dimension_semantics=("parallel", "parallel"))
    out_shape = jax.ShapeDtypeStruct((N_TOK, qw), F32)
    ctx_spec = lambda w: pl.BlockSpec((CTX_LEN, w), lambda s, i, sk: (s, 0))
    o = pl.pallas_call(
        functools.partial(_gqa_kernel, hd=A_HD, group=group, scale=scale, windowed=False, sink_base=layer * A_HEADS),
        grid_spec=pltpu.PrefetchScalarGridSpec(
            num_scalar_prefetch=1, grid=(N_CTX_SEQ, 1), in_specs=[ctx_spec(qw), ctx_spec(kw), ctx_spec(kw)],
            out_specs=ctx_spec(qw)),
        out_shape=out_shape, compiler_params=params, name="gqa_full",
    )(sink, q, k, v)
    tq = ATT_Q_BLOCK
    nb = LAT_LEN // tq
    base = N_CTX // tq
    blk = lambda w, f: pl.BlockSpec((tq, w), lambda b, i, sk: (base + nb * b + f(i), 0))
    same = lambda i: i
    prev = lambda i: jnp.maximum(i - 1, 0)
    nxt = lambda i: jnp.minimum(i + 1, nb - 1)
    cspec = pl.BlockSpec((1, 1) + cache_k.shape[2:], lambda b, i, sk: (b, layer, 0, 0, 0))
    return pl.pallas_call(
        functools.partial(_gqa_kernel, hd=A_HD, group=group, scale=scale, windowed=True, sink_base=layer * A_HEADS),
        grid_spec=pltpu.PrefetchScalarGridSpec(
            num_scalar_prefetch=1, grid=(N_LAT_SEQ, nb),
            in_specs=[blk(qw, same), blk(kw, same), blk(kw, same), blk(kw, prev), blk(kw, nxt), blk(kw, prev),
                      blk(kw, nxt), cspec, cspec, pl.BlockSpec(memory_space=pl.ANY)],
            out_specs=blk(qw, same)),
        out_shape=out_shape, input_output_aliases={10: 0}, compiler_params=params, name="gqa_windowed",
    )(sink, q, k, v, k, k, v, v, cache_k, cache_v, o)


def _mla_kernel(q_ref, ckv_ref, kpe_ref, wukv_ref, *rest, n_heads, nope, rope, vd, scale, cached):
    if cached:
        cckv_ref, ckpe_ref, _, o_ref, k_scr, vext_scr = rest
    else:
        o_ref, k_scr, vext_scr = rest
    i = pl.program_id(1)
    n_cache = k_scr.shape[0] - ckv_ref.shape[0]
    hw = nope + vd
    hl = MLA_HEAD_LANES

    @pl.when(i == 0)
    def _():
        w = wukv_ref[...].astype(BF16)

        def expand(rows, kpe_rows, lo, hi):
            kv = jnp.dot(rows.astype(BF16), w, preferred_element_type=F32).astype(BF16)
            n = hi - lo
            kpe = kpe_rows.astype(BF16)
            for h in range(n_heads):
                k_scr[lo:hi, hl * h:hl * (h + 1)] = jnp.concatenate(
                    [kv[:, h * hw:h * hw + nope], kpe, jnp.zeros((n, hl - nope - rope), BF16)], axis=1)
                vext_scr[lo:hi, 2 * vd * h:2 * vd * (h + 1)] = jnp.concatenate(
                    [kv[:, h * hw + nope:(h + 1) * hw], jnp.ones((n, vd), BF16)], axis=1)

        if cached:
            expand(cckv_ref[0, 0], ckpe_ref[0, 0], 0, n_cache)
        expand(ckv_ref[...], kpe_ref[...], n_cache, k_scr.shape[0])

    for h in range(n_heads):
        qh = (q_ref[:, hl * h:hl * (h + 1)] * scale).astype(BF16)
        s = _bdot(qh, k_scr[:, hl * h:hl * (h + 1)], _D2T)
        o_ref[:, h * vd:(h + 1) * vd] = _softmax_pv(s, vext_scr[:, 2 * vd * h:2 * vd * (h + 1)], None)


def mla_attention(q, ckv, kpe, w_ukv, cache_ckv, cache_kpe, layer):
    qw = q.shape[1]
    tq = MLA_Q_BLOCK
    kw = dict(n_heads=B_HEADS, nope=B_NOPE, rope=B_ROPE, vd=B_VD, scale=(B_NOPE + B_ROPE) ** -0.5)
    params = pltpu.CompilerParams(dimension_semantics=("parallel", "arbitrary"))
    out_shape = jax.ShapeDtypeStruct((N_TOK, B_HEADS * B_VD), F32)
    scratch = lambda rows: [pltpu.VMEM((rows, B_HEADS * MLA_HEAD_LANES), BF16),
                            pltpu.VMEM((rows, 2 * B_HEADS * B_VD), BF16)]
    nbc = CTX_LEN // tq
    o = pl.pallas_call(
        functools.partial(_mla_kernel, cached=False, **kw),
        grid=(N_CTX_SEQ, nbc),
        in_specs=[pl.BlockSpec((tq, qw), lambda s, i: (s * nbc + i, 0)),
                  pl.BlockSpec((CTX_LEN, B_KV_LORA), lambda s, i: (s, 0)),
                  pl.BlockSpec((CTX_LEN, B_ROPE), lambda s, i: (s, 0)),
                  _layer_item(w_ukv, layer)],
        out_specs=pl.BlockSpec((tq, B_HEADS * B_VD), lambda s, i: (s * nbc + i, 0)),
        out_shape=out_shape,
        scratch_shapes=scratch(CTX_LEN),
        compiler_params=params, name="mla_context",
    )(q, ckv, kpe, w_ukv)
    nb = LAT_LEN // tq
    base = N_CTX // tq
    lat0 = N_CTX // LAT_LEN
    s_len = CACHE_LEN + LAT_LEN
    return pl.pallas_call(
        functools.partial(_mla_kernel, cached=True, **kw),
        grid=(N_LAT_SEQ, nb),
        in_specs=[pl.BlockSpec((tq, qw), lambda b, i: (base + nb * b + i, 0)),
                  pl.BlockSpec((LAT_LEN, B_KV_LORA), lambda b, i: (lat0 + b, 0)),
                  pl.BlockSpec((LAT_LEN, B_ROPE), lambda b, i: (lat0 + b, 0)),
                  _layer_item(w_ukv, layer),
                  pl.BlockSpec((1, 1, CACHE_LEN, B_KV_LORA), lambda b, i: (b, layer, 0, 0)),
                  pl.BlockSpec((1, 1, CACHE_LEN, B_ROPE), lambda b, i: (b, layer, 0, 0)),
                  pl.BlockSpec(memory_space=pl.ANY)],
        out_specs=pl.BlockSpec((tq, B_HEADS * B_VD), lambda b, i: (base + nb * b + i, 0)),
        out_shape=out_shape, input_output_aliases={6: 0},
        scratch_shapes=scratch(s_len),
        compiler_params=params, name="mla_latent",
    )(q, ckv, kpe, w_ukv, cache_ckv, cache_kpe, o)


CHUNK = 64
GLA_SUB = 16
PAIR = 2
N_CHAIN = PAIR * 2 * 4


def _is_back(shape):
    return (lax.broadcasted_iota(jnp.int32, shape, 0) // 4) % 2 == 1


def _chains(ref_f, ref_b, width):
    return jnp.stack([ref[0, s, 0, :, h * width:(h + 1) * width]
                      for s in range(PAIR) for ref in (ref_f, ref_b) for h in range(4)], axis=0)


def _unchain(y, o_f, o_b):
    for s in range(PAIR):
        o_f[0, s, 0] = jnp.concatenate([y[s * 8 + h] for h in range(4)], axis=-1)
        o_b[0, s, 0] = jnp.concatenate([y[s * 8 + 4 + h] for h in range(4)], axis=-1)


def _dir_masks(L):
    shape = (N_CHAIN, L, L)
    back = _is_back(shape)
    row = lax.broadcasted_iota(jnp.int32, shape, 1)
    col = lax.broadcasted_iota(jnp.int32, shape, 2)
    ahead = jnp.where(back, col - row, row - col)
    return ahead >= 0, ahead > 0, row == col


def _chunk_end(ci):
    L = ci.shape[1]
    return jnp.where(_is_back((N_CHAIN, 1, 1)), ci[:, 0:1], ci[:, L - 1:L])


def _split_refs(refs, n_in, has_s0, has_sfin):
    ins = refs[:n_in]
    pos = n_in
    s0_ref = None
    if has_s0:
        s0_ref = refs[pos]
        pos += 3
    of_ref, ob_ref = refs[pos], refs[pos + 1]
    pos += 2
    sfin_ref = refs[pos] if has_sfin else None
    return ins, s0_ref, of_ref, ob_ref, sfin_ref, refs[-1]


def _init_state(s_scr, s0_ref):
    @pl.when(pl.program_id(1) == 0)
    def _():
        if s0_ref is None:
            s_scr[...] = jnp.zeros_like(s_scr)
        else:
            s_scr[...] = s0_ref[0]


def _emit_state(sfin_ref, s_new):
    if sfin_ref is None:
        return

    @pl.when(pl.program_id(1) == pl.num_programs(1) - 1)
    def _():
        sfin_ref[0] = s_new


def _rwkv_kernel(*refs, dot, has_s0, has_sfin):
    (rf, rb, vf, vb, kkf, kkb, lwf, lwb, kf, kb, af, ab), s0_ref, yf_ref, yb_ref, sfin_ref, s_scr = _split_refs(
        refs, 12, has_s0, has_sfin)
    _init_state(s_scr, s0_ref)
    n = D_N
    r = _chains(rf, rb, n)
    v = _chains(vf, vb, n)
    kk = _chains(kkf, kkb, n)
    lw = _chains(lwf, lwb, n)
    k = _chains(kf, kb, n)
    a = _chains(af, ab, n)
    L = r.shape[1]
    S = s_scr[...]
    incl, strict, diag = _dir_masks(L)
    ci = _dot_exact_lhs(jnp.where(incl, 1.0, 0.0), lw, _NN)
    ce = ci - lw
    cl = _chunk_end(ci)
    e_neg = jnp.exp(-ci)
    b = a * kk
    alpha = kk * jnp.exp(ce)
    rho = r * jnp.exp(ci)
    beta = b * e_neg
    kappa = k * e_neg
    e_end = jnp.exp(cl - ci)
    ar = jnp.concatenate([alpha, rho], axis=1)
    bk = jnp.concatenate([beta, kappa], axis=1)
    w = dot(ar, bk, _NT)
    nmat = jnp.where(strict, w[:, :L, :L], 0.0)
    mmat = jnp.where(strict, w[:, :L, L:], 0.0)
    p1 = jnp.where(incl, w[:, L:, :L], 0.0)
    p2 = jnp.where(incl, w[:, L:, L:], 0.0)
    x = jnp.where(diag, 1.0, 0.0) - nmat
    p = dot(nmat, nmat, _NN)
    span = 2
    while True:
        x = x + dot(x, p, _NN)
        span *= 2
        if span >= L:
            break
        p = dot(p, p, _NN)
    us = dot(ar, S, _NT)
    rhs = us[:, :L] + dot(mmat, v, _NN)
    d = -dot(x, rhs, _NN)
    dv = jnp.concatenate([d, v], axis=1)
    pp = jnp.concatenate([p1, p2], axis=2)
    _unchain(us[:, L:] + dot(pp, dv, _NN), yf_ref, yb_ref)
    bk_end = jnp.concatenate([b * e_end, k * e_end], axis=1)
    s_new = S * jnp.exp(cl) + dot(dv, bk_end, _TN)
    s_scr[...] = s_new
    _emit_state(sfin_ref, s_new)


def _gla_kernel(*refs, dot, has_s0, has_sfin):
    (qf, qb, kf, kb, vf, vb, laf, lab), s0_ref, of_ref, ob_ref, sfin_ref, s_scr = _split_refs(
        refs, 8, has_s0, has_sfin)
    _init_state(s_scr, s0_ref)
    q4 = _chains(qf, qb, 128)
    k4 = _chains(kf, kb, 128)
    la4 = _chains(laf, lab, 128)
    v = _chains(vf, vb, C_DV)
    g, L, lanes = q4.shape
    dk = C_DK
    n_sub = L // GLA_SUB
    st = s_scr[...]
    incl, _, _ = _dir_masks(L)
    c = _dot_exact_lhs(jnp.where(incl, 1.0, 0.0), la4, _NN)
    shape = (g, L, lanes)
    back = _is_back(shape)
    lane_blk = lax.broadcasted_iota(jnp.int32, shape, 2) // dk
    row_blk = lax.broadcasted_iota(jnp.int32, shape, 1) // GLA_SUB
    cref_f = jnp.zeros(shape, F32)
    cref_b = jnp.zeros(shape, F32)
    for j in range(1, n_sub):
        cref_f = jnp.where(lane_blk == j, c[:, j * GLA_SUB - 1:j * GLA_SUB], cref_f)
        cref_b = jnp.where(lane_blk == j - 1, c[:, j * GLA_SUB:j * GLA_SUB + 1], cref_b)
    cref = jnp.where(back, cref_b, cref_f)
    q_on = row_blk == lane_blk
    k_on = jnp.where(back, row_blk - lane_blk, lane_blk - row_blk) >= 0
    qh = jnp.where(q_on, q4 * jnp.exp(jnp.where(q_on, c - cref, 0.0)), 0.0)
    kh = jnp.where(k_on, k4 * jnp.exp(jnp.where(k_on, cref - c, 0.0)), 0.0)
    att = jnp.where(incl, dot(qh, kh, _NT), 0.0)
    cl = _chunk_end(c)
    qe = (q4 * jnp.exp(c))[:, :, :dk]
    ke = (k4 * jnp.exp(cl - c))[:, :, :dk]
    _unchain(dot(qe, st, _NT) + dot(att, v, _NN), of_ref, ob_ref)
    s_new = st * jnp.exp(cl[:, :, :dk]) + dot(v, ke, _TN)
    s_scr[...] = s_new
    _emit_state(sfin_ref, s_new)


def _recurrence_calls(kernel_fn, name, pairs, singles_f, singles_b, s0_lat, state_dims, out_width):
    def run(view, grid, group, s0, prev_out):
        nc = view[2]
        fwd_map = lambda p, c: (group(p), 0, c, 0, 0)
        bwd_map = lambda p, c: (group(p), 0, nc - 1 - c, 0, 0)
        blk = lambda w: (1, PAIR, 1, CHUNK, w)
        args, in_specs = [], []
        for af, ab in [(a, a) for a in pairs] + list(zip(singles_f, singles_b)):
            w = af.shape[-1]
            args += [af.reshape(view + (w,)), ab.reshape(view + (w,))]
            in_specs += [pl.BlockSpec(blk(w), fwd_map), pl.BlockSpec(blk(w), bwd_map)]
        out_specs = [pl.BlockSpec(blk(out_width), fwd_map), pl.BlockSpec(blk(out_width), bwd_map)]
        out_shape = [jax.ShapeDtypeStruct(view + (out_width,), F32)] * 2
        aliases = {}
        if s0 is not None:
            args += [s0] + [o.reshape(view + (out_width,)) for o in prev_out]
            in_specs += [_full(s0.shape), pl.BlockSpec(memory_space=pl.ANY), pl.BlockSpec(memory_space=pl.ANY)]
            aliases = {len(args) - 2: 0, len(args) - 1: 1}
        else:
            out_specs.append(pl.BlockSpec((1, N_CHAIN) + state_dims, lambda p, c: (p, 0, 0, 0)))
            out_shape.append(jax.ShapeDtypeStruct((grid[0], N_CHAIN) + state_dims, F32))
        return pl.pallas_call(
            functools.partial(kernel_fn, has_s0=s0 is not None, has_sfin=s0 is None),
            grid=grid, in_specs=in_specs, out_specs=out_specs, out_shape=out_shape,
            input_output_aliases=aliases, scratch_shapes=[pltpu.VMEM((N_CHAIN,) + state_dims, F32)],
            compiler_params=pltpu.CompilerParams(dimension_semantics=("parallel", "arbitrary")),
            name=name + ("_latent" if s0 is not None else "_context"),
        )(*args)

    ctx_nc = CTX_LEN // CHUNK
    ctx_view = (N_TOK // (PAIR * CTX_LEN), PAIR, ctx_nc, CHUNK)
    o_f, o_b, s_fin = run(ctx_view, (N_CTX_SEQ // PAIR, ctx_nc), lambda p: p, None, None)
    lat_nc = LAT_LEN // CHUNK
    lat_view = (N_TOK // (PAIR * LAT_LEN), PAIR, lat_nc, CHUNK)
    o_f, o_b = run(lat_view, (1, lat_nc), lambda p: N_CTX // (PAIR * LAT_LEN), s0_lat, (o_f, o_b))
    return o_f.reshape(N_TOK, out_width), o_b.reshape(N_TOK, out_width), s_fin


def _layer_norm(x, g, b):
    mu = jnp.mean(x, axis=-1, keepdims=True)
    xc = x - mu
    var = jnp.mean(xc * xc, axis=-1, keepdims=True)
    return xc * lax.rsqrt(var + LN_EPS) * g + b


def _merge_kernel(xc_ref, xl_ref, mod_ref, oa_ref, ob_ref, cof_ref, cob_ref, cgate_ref, yf_ref, yb_ref, bonus_ref,
                  dgate_ref, wg_ref, wbr_ref, wout_ref, cnorm_ref, dlng_ref, dlnb_ref, lng_ref, lnb_ref, wr_ref, br_ref,
                  bd_ref, x1_o, h2_o, topi_o, topw_o):
    x = _tile_rows(pl.program_id(0), xc_ref, xl_ref)
    m = mod_ref[0]
    sh1, sc1, g1, sh2, sc2 = m[0:1], m[1:2], m[2:3], m[3:4], m[4:5]
    bd = bd_ref[...]
    inv_n = 1.0 / D_N
    co = cof_ref[...] + cob_ref[...]
    o_c = co * lax.rsqrt(_dot_exact_rhs(co * co, bd) * inv_n + RMS_EPS) * cnorm_ref[...] * cgate_ref[...]
    y = yf_ref[...] + yb_ref[...]
    yc = y - _dot_exact_rhs(y, bd) * inv_n
    var = _dot_exact_rhs(yc * yc, bd) * inv_n
    o_d = (yc * lax.rsqrt(var + D_GN_EPS) * dlng_ref[...] + dlnb_ref[...] + bonus_ref[...]) * dgate_ref[...]
    branches = [b.astype(BF16) for b in (oa_ref[...], ob_ref[...], o_c, o_d)]
    h = (x * (1.0 + sc1) + sh1).astype(BF16)
    blocks = []
    for cb in range(D_MODEL // MXU_WIDTH):
        merged = None
        for n in range(N_BRANCH):
            cols = slice(n * D_MODEL + cb * MXU_WIDTH, n * D_MODEL + (cb + 1) * MXU_WIDTH)
            gate = _sigmoid(jnp.dot(h, wg_ref[0, :, cols], preferred_element_type=F32))
            term = gate * jnp.dot(branches[n], wbr_ref[0, n, :, cb * MXU_WIDTH:(cb + 1) * MXU_WIDTH],
                                  preferred_element_type=F32)
            merged = term if merged is None else merged + term
        blocks.append(merged.astype(BF16))
    mix = jnp.dot(jnp.concatenate(blocks, axis=1), wout_ref[0], preferred_element_type=F32)
    x1 = _layer_norm(ALPHA * x + g1 * mix, lng_ref[...], lnb_ref[...])
    x1_o[...] = x1
    h2 = x1 * (1.0 + sc2) + sh2
    h2_o[...] = h2.astype(BF16)
    logits = _dot3(h2, wr_ref[...]) + br_ref[...]
    tm, n_e = logits.shape
    lane_e = lax.broadcasted_iota(jnp.int32, (tm, n_e), 1)
    lane_o = lax.broadcasted_iota(jnp.int32, (tm, topi_o.shape[1]), 1)
    top_i = jnp.zeros((tm, topi_o.shape[1]), jnp.int32)
    top_v = jnp.zeros((tm, topw_o.shape[1]), F32)
    vals = []
    for kth in range(TOP_K):
        mx = jnp.max(logits, axis=-1, keepdims=True)
        idx = jnp.min(jnp.where(logits == mx, lane_e, n_e), axis=-1, keepdims=True)
        vals.append(mx)
        top_i = jnp.where(lane_o == kth, idx, top_i)
        logits = jnp.where(lane_e == idx, -jnp.inf, logits)
    es = [jnp.exp(vk - vals[0]) for vk in vals]
    den = es[0] + es[1] + es[2] + es[3]
    for kth in range(TOP_K):
        top_v = jnp.where(lane_o == kth, es[kth] / den, top_v)
    topi_o[...] = top_i
    topw_o[...] = top_v


def merge_and_route(x_ctx, x_lat, mod, o_a, o_b, co_f, co_b, cgate, y_f, y_b, bonus, dgate, w_g, w_br, w_out, layer, sp):
    tm = ROW_TILE
    hn = D_HEADS * D_N
    row = lambda w: pl.BlockSpec((tm, w), lambda t: (t, 0))
    small = [sp[k] for k in ('c_norm', 'd_ln_g', 'd_ln_b', 'ln_g', 'ln_b', 'w_router', 'b_router')]
    index = [layer, layer, layer, 2 * layer, 2 * layer, layer, layer]
    bd = _head_block_diag()
    return pl.pallas_call(
        _merge_kernel,
        grid=(N_TILES,),
        in_specs=([_ctx_rows_spec(D_MODEL), _lat_rows_spec(D_MODEL), _mod_spec(layer)]
                  + [row(hn)] * 9 + [_layer_block(w, layer) for w in (w_g, w_br, w_out)]
                  + [_layer_item(a, i) for a, i in zip(small, index)] + [_full(bd.shape)]),
        out_specs=[row(D_MODEL), row(D_MODEL), row(128), row(128)],
        out_shape=[jax.ShapeDtypeStruct((N_TOK, D_MODEL), F32), jax.ShapeDtypeStruct((MOE_ROWS, D_MODEL), BF16),
                   jax.ShapeDtypeStruct((N_TOK, 128), jnp.int32), jax.ShapeDtypeStruct((N_TOK, 128), F32)],
        compiler_params=pltpu.CompilerParams(dimension_semantics=("parallel",), vmem_limit_bytes=VMEM_LIMIT),
        name="merge_and_route",
    )(x_ctx, x_lat, mod, o_a, o_b, co_f, co_b, cgate, y_f, y_b, bonus, dgate, w_g, w_br, w_out, *small, bd)


def _moe_kernel(te_ref, tv_ref, first_ref, slot_ref, next_ref, x_ref, w1_hbm, b1_ref, w2_hbm, b2_ref, perm_ref, *rest,
                layer):
    y_ref, w1buf, w2buf, sem, w1s, w2s, hs = rest[-7:]
    t = pl.program_id(0)
    valid = tv_ref[t] != 0
    d_model, two_f = w1s.shape
    n_blk = two_f // MXU_WIDTH
    half = MXU_WIDTH // 2

    def fetch(expert, slot):
        return (pltpu.make_async_copy(w1_hbm.at[layer, expert], w1buf.at[slot], sem.at[0, slot]),
                pltpu.make_async_copy(w2_hbm.at[layer, expert], w2buf.at[slot], sem.at[1, slot]))

    @pl.when(t == 0)
    def _():
        for cp in fetch(te_ref[0], 0):
            cp.start()

    @pl.when(first_ref[t] == 1)
    def _():
        slot = slot_ref[t]
        for cp in fetch(te_ref[t], slot):
            cp.wait()

        @pl.when(next_ref[t] >= 0)
        def _():
            for cp in fetch(next_ref[t], 1 - slot):
                cp.start()

        for blk in range(n_blk):
            sl = slice(blk * MXU_WIDTH, (blk + 1) * MXU_WIDTH)
            wb = w1buf[slot, :, sl].astype(BF16)
            w1s[:, sl] = jnp.dot(wb, perm_ref[...], preferred_element_type=F32).astype(BF16)
        w2s[...] = w2buf[slot].astype(BF16)

    @pl.when(valid)
    def _():
        x = x_ref[...]
        for blk in range(n_blk):
            sl = slice(blk * MXU_WIDTH, (blk + 1) * MXU_WIDTH)
            u = jnp.dot(x, w1s[:, sl], preferred_element_type=F32) + b1_ref[0, 0, :, sl]
            glu = jnp.minimum(u[:, :half], SWIGLU_LIMIT)
            lin = jnp.clip(u[:, half:], -SWIGLU_LIMIT, SWIGLU_LIMIT)
            hs[:, blk * half:(blk + 1) * half] = (glu * _sigmoid(SWIGLU_ALPHA * glu) * (lin + 1.0)).astype(BF16)
        y = jnp.dot(hs[...], w2s[...], preferred_element_type=F32) + b2_ref[0, 0]
        y_ref[...] = y.astype(y_ref.dtype)

    @pl.when(jnp.logical_not(valid))
    def _():
        y_ref[...] = jnp.zeros_like(y_ref)


def _deinterleave_perm():
    half = MXU_WIDTH // 2
    src = np.arange(MXU_WIDTH)
    dst = np.where(src % 2 == 0, src // 2, half + src // 2)
    p = np.zeros((MXU_WIDTH, MXU_WIDTH), np.float32)
    p[src, dst] = 1.0
    return jnp.asarray(p, BF16)


def _moe_dispatch(top_i):
    n, k = top_i.shape
    tm = MOE_ROW_TILE
    p_rows = n * k + N_EXPERTS * tm
    experts = jnp.arange(N_EXPERTS, dtype=jnp.int32)
    onehot = top_i[:, :, None] == experts
    sel = jnp.sum(onehot.astype(jnp.int32), axis=1)
    before = jnp.cumsum(sel, axis=0) - sel
    counts = jnp.sum(sel, axis=0)
    padded = ((counts + tm - 1) // tm) * tm
    ends = jnp.cumsum(padded)
    starts = ends - padded
    pos = jnp.sum(jnp.where(onehot, (before + starts)[:, None, :], 0), axis=-1)
    n_tiles = p_rows // tm
    tile_start = jnp.arange(n_tiles, dtype=jnp.int32) * tm
    tile_valid = (tile_start < ends[-1]).astype(jnp.int32)
    last_tile = ends[-1] // tm - 1
    tile_expert = jnp.sum(ends[None, :] <= jnp.minimum(tile_start, last_tile * tm)[:, None], axis=1).astype(jnp.int32)
    keys = jnp.sort((top_i * n + jnp.arange(n, dtype=jnp.int32)[:, None]).reshape(-1))
    tile_onehot = tile_expert[:, None] == experts[None, :]
    lookup = lambda table: jnp.sum(jnp.where(tile_onehot, table[None, :], 0), axis=1)
    tile_rank0 = tile_start - lookup(starts)
    rank = tile_rank0[:, None] + jnp.arange(tm, dtype=jnp.int32)[None, :]
    sorted_at = jnp.clip(lookup(jnp.cumsum(counts) - counts)[:, None] + rank, 0, n * k - 1)
    tile_keys = keys[sorted_at.reshape(-1)].reshape(n_tiles, tm)
    filler = (tile_start[:, None] + jnp.arange(tm, dtype=jnp.int32)[None, :]) % n
    src_tok = jnp.where(rank < lookup(counts)[:, None], tile_keys % n, filler)
    nst = n_tiles // MOE_SPLITS
    idx = jnp.arange(nst, dtype=jnp.int32)
    tables = []
    for h in range(MOE_SPLITS):
        te_h = tile_expert[h * nst:(h + 1) * nst]
        is_first = jnp.concatenate([jnp.ones((1,), jnp.int32), (te_h[1:] != te_h[:-1]).astype(jnp.int32)])
        slot = (jnp.cumsum(is_first) - 1) % 2
        nxt = jnp.min(jnp.where(jnp.logical_and(idx[None, :] > idx[:, None], is_first[None, :] == 1),
                                idx[None, :], nst), axis=1)
        next_expert = jnp.sum(jnp.where(idx[None, :] == nxt[:, None], te_h[None, :] + 1, 0), axis=1) - 1
        tables.append((te_h, tile_valid[h * nst:(h + 1) * nst], is_first, slot.astype(jnp.int32),
                       next_expert.astype(jnp.int32)))
    return pos, src_tok.reshape(-1), tables, p_rows


def moe_experts(h2, top_i, layer, w1, b1, w2, b2):
    n = top_i.shape[0]
    d = h2.shape[1]
    depth, e, _, two_f = w1.shape
    f = two_f // 2
    tm = MOE_ROW_TILE
    pos, src_tok, tables, p_rows = _moe_dispatch(top_i)
    assert h2.shape[0] == p_rows
    src_tok = lax.optimization_barrier(src_tok)
    b1p = b1.reshape(depth, e, two_f // MXU_WIDTH, MXU_WIDTH // 2, 2).swapaxes(3, 4).reshape(depth, e, 1, two_f)
    b2r = b2.reshape(depth, e, 1, d)
    expert_vec = lambda w: pl.BlockSpec((1, 1, 1, w), lambda t, te, *_: (layer, te[t], 0, 0))
    rows = p_rows // MOE_SPLITS
    nst = rows // tm
    xs = [h2.at[src_tok[h * rows:(h + 1) * rows]].get(mode="promise_in_bounds") for h in range(MOE_SPLITS)]
    ys = None
    for h in range(MOE_SPLITS):
        in_specs = [
            pl.BlockSpec((tm, d), lambda t, *_: (t, 0)),
            pl.BlockSpec(memory_space=pl.ANY),
            expert_vec(two_f),
            pl.BlockSpec(memory_space=pl.ANY),
            expert_vec(d),
            pl.BlockSpec((MXU_WIDTH, MXU_WIDTH), lambda t, *_: (0, 0)),
        ]
        args = [*tables[h], xs[h], w1, b1p, w2, b2r, _deinterleave_perm()]
        aliases = {}
        if ys is not None:
            in_specs.append(pl.BlockSpec(memory_space=pl.ANY))
            aliases = {len(args): 0}
            args.append(ys)
        ys = pl.pallas_call(
            functools.partial(_moe_kernel, layer=layer),
            grid_spec=pltpu.PrefetchScalarGridSpec(
                num_scalar_prefetch=len(tables[h]), grid=(nst,), in_specs=in_specs,
                out_specs=pl.BlockSpec((tm, d), lambda t, *_, h=h: (h * nst + t, 0)),
                scratch_shapes=[pltpu.VMEM((2, d, two_f), F32), pltpu.VMEM((2, f, d), F32),
                                pltpu.SemaphoreType.DMA((2, 2)),
                                pltpu.VMEM((d, two_f), BF16), pltpu.VMEM((f, d), BF16), pltpu.VMEM((tm, f), BF16)]),
            out_shape=jax.ShapeDtypeStruct((p_rows, d), BF16),
            input_output_aliases=aliases,
            compiler_params=pltpu.CompilerParams(dimension_semantics=("arbitrary",),
                                                 vmem_limit_bytes=48 * 1024 * 1024),
            name="moe_experts",
        )(*args)
    return ys, pos


def _final_kernel(x1_ref, mod_ref, ys_ref, topw_ref, lng_ref, lnb_ref, o_ref):
    g2 = mod_ref[0, 5:6]
    moe = None
    for kth in range(TOP_K):
        term = ys_ref[kth].astype(F32) * topw_ref[:, kth:kth + 1]
        moe = term if moe is None else moe + term
    o_ref[...] = _layer_norm(ALPHA * x1_ref[...] + g2 * moe, lng_ref[...], lnb_ref[...])


def combine_and_norm(x1, mod, ys, pos, top_w, layer, sp):
    tm = ROW_TILE
    ln_g, ln_b = sp['ln_g'], sp['ln_b']
    outs = []
    for t0, n_rows in ((0, N_CTX), (CTX_TILES, N_TOK - N_CTX)):
        idx = lax.optimization_barrier(pos[t0 * tm:t0 * tm + n_rows].T.reshape(-1))
        rows = ys.at[idx].get(mode="promise_in_bounds").reshape(TOP_K, n_rows, D_MODEL)
        outs.append(pl.pallas_call(
            _final_kernel,
            grid=(n_rows // tm,),
            in_specs=[pl.BlockSpec((tm, D_MODEL), lambda t, t0=t0: (t0 + t, 0)),
                      pl.BlockSpec((1, 6, D_MODEL),
                                   lambda t, t0=t0: (layer * (1 + N_LAT_SEQ) + _mod_row(t0 + t), 0, 0)),
                      pl.BlockSpec((TOP_K, tm, D_MODEL), lambda t: (0, t, 0)),
                      pl.BlockSpec((tm, 128), lambda t, t0=t0: (t0 + t, 0)),
                      _layer_item(ln_g, 2 * layer + 1), _layer_item(ln_b, 2 * layer + 1)],
            out_specs=pl.BlockSpec((tm, D_MODEL), lambda t: (t, 0)),
            out_shape=jax.ShapeDtypeStruct((n_rows, D_MODEL), F32),
            compiler_params=pltpu.CompilerParams(dimension_semantics=("parallel",)),
            name="combine_and_norm",
        )(x1, mod, rows, top_w, ln_g, ln_b))
    return outs


def kernel(x_prompt, x_sample, cache_a_k, cache_a_v, cache_b_ckv, cache_b_kpe, state_c, state_d, c,
           c_ctx, w_mod, b_mod, w_in, a_sink, b_q_norm, b_w_uq, b_kv_norm, b_w_ukv, c_w_gate, c_b_gate,
           c_norm, d_mu, d_w0, d_w2, d_a0, d_a2, d_g2, d_k_k, d_k_a, d_r_k, d_ln_g, d_ln_b, w_br, w_out,
           ln_g, ln_b, w_router, b_router, w_mlp1, b_mlp1, w_mlp2, b_mlp2):
    sp = stacked_params(dict(
        a_sink=a_sink, b_q_norm=b_q_norm, b_w_uq=b_w_uq, b_kv_norm=b_kv_norm, b_w_ukv=b_w_ukv, c_w_gate=c_w_gate,
        c_b_gate=c_b_gate, c_norm=c_norm, d_mu=d_mu, d_w0=d_w0, d_w2=d_w2, d_a0=d_a0, d_a2=d_a2, d_g2=d_g2,
        d_k_k=d_k_k, d_k_a=d_k_a, d_r_k=d_r_k, d_ln_g=d_ln_g, d_ln_b=d_ln_b, ln_g=ln_g, ln_b=ln_b,
        w_router=w_router, b_router=b_router))
    assert x_prompt.shape == (N_CTX_SEQ, CTX_LEN, D_MODEL) and x_sample.shape == (N_LAT_SEQ, LAT_LEN, D_MODEL)
    x_ctx, x_lat = x_prompt.reshape(N_CTX, D_MODEL), x_sample.reshape(-1, D_MODEL)
    cond8 = jnp.concatenate([c_ctx[None], c, jnp.zeros((8 - 1 - N_LAT_SEQ, D_MODEL), F32)], axis=0)
    mod = modulation_table(cond8, w_mod, b_mod)[:, :1 + N_LAT_SEQ].reshape(DEPTH * (1 + N_LAT_SEQ), 6, D_MODEL)
    tables = _rope_tables()
    w_small, w_g = prepare_in_weights(w_in)
    w_br_bf, w_out_bf = w_br.astype(BF16), w_out.astype(BF16)
    new = {name: [] for name in ("a_k", "a_v", "b_ckv", "b_kpe", "c", "d")}
    for l in range(DEPTH):
        (aq, ak, av, bq, bckv, bkpe, cq4, ck4, cla_f, cla_b, cv, cgate,
         r, v, kk, lw_f, lw_b, k_f, k_b, a_f, a_b, bonus, dgate) = mixer_prelude(
             x_ctx, x_lat, mod, w_small, l, tables, sp)

        o_a = gqa_attention(aq, ak, av, sp['a_sink'], cache_a_k, cache_a_v, l)
        o_b = mla_attention(bq, bckv, bkpe, sp['b_w_ukv'], cache_b_ckv, cache_b_kpe, l)

        c_s0 = jnp.swapaxes(state_c[:, l], 3, 4).reshape(1, N_CHAIN, C_DV, C_DK)
        co_f, co_b, c_fin = _recurrence_calls(functools.partial(_gla_kernel, dot=_dot1), "gla", [cq4, ck4, cv],
                                              [cla_f], [cla_b], c_s0, (C_DV, C_DK), C_HEADS * C_DV)
        d_s0 = state_d[:, l].reshape(1, N_CHAIN, D_N, D_N)
        y_f, y_b, d_fin = _recurrence_calls(functools.partial(_rwkv_kernel, dot=_dot1), "rwkv7", [r, v, kk],
                                            [lw_f, k_f, a_f], [lw_b, k_b, a_b], d_s0, (D_N, D_N), D_HEADS * D_N)

        x1, h2, top_i, top_w = merge_and_route(x_ctx, x_lat, mod, o_a, o_b, co_f, co_b, cgate, y_f, y_b, bonus, dgate,
                                               w_g, w_br_bf, w_out_bf, l, sp)
        ys, pos = moe_experts(h2, top_i[:, :TOP_K], l, w_mlp1, b_mlp1, w_mlp2, b_mlp2)
        x_ctx, x_lat = combine_and_norm(x1, mod, ys, pos, top_w, l, sp)

        new["a_k"].append(ak[:N_CTX].reshape(N_CTX_SEQ, CTX_LEN, A_KV_HEADS, A_HD).transpose(0, 2, 1, 3))
        new["a_v"].append(av[:N_CTX].reshape(N_CTX_SEQ, CTX_LEN, A_KV_HEADS, A_HD).transpose(0, 2, 1, 3))
        new["b_ckv"].append(bckv[:N_CTX].reshape(N_CTX_SEQ, CTX_LEN, B_KV_LORA))
        new["b_kpe"].append(bkpe[:N_CTX].reshape(N_CTX_SEQ, CTX_LEN, B_ROPE))
        new["c"].append(jnp.swapaxes(c_fin.reshape(N_CTX_SEQ, 2, C_HEADS, C_DV, C_DK), 3, 4))
        new["d"].append(d_fin.reshape(N_CTX_SEQ, 2, D_HEADS, D_N, D_N))
    y_prompt = x_ctx.reshape(x_prompt.shape)
    y_sample = x_lat.reshape(x_sample.shape)
    return (y_prompt, y_sample, *(jnp.stack(new[name], axis=1) for name in ("a_k", "a_v", "b_ckv", "b_kpe", "c", "d")))
```

```python
import functools

import jax
import jax.numpy as jnp
import numpy as np
from jax import lax
from jax.experimental import pallas as pl
from jax.experimental.pallas import tpu as pltpu

F32 = jnp.float32
BF16 = jnp.bfloat16

MXU_WIDTH = 256
VMEM_LIMIT = 56 * 1024 * 1024

D_MODEL = 1024
DEPTH = 2
GRID_W = 64
ROPE_BASE = 10000.0
A_HEADS, A_KV_HEADS, A_HD = 4, 2, 64
B_HEADS, B_NOPE, B_ROPE, B_VD, B_Q_LORA, B_KV_LORA = 4, 64, 32, 64, 192, 128
C_HEADS, C_DK, C_DV, C_GATE_RANK, C_GATE_TEMP = 4, 32, 64, 16, 16.0
D_HEADS, D_N, D_DECAY_RANK, D_AAA_RANK, D_GATE_RANK, D_GN_EPS = 4, 64, 64, 64, 128, 64e-5
BRANCH_W = 256
N_BRANCH = 4
N_EXPERTS = 32
TOP_K = 4
SWIGLU_LIMIT = 7.0
SWIGLU_ALPHA = 1.702
ALPHA = (2 * DEPTH) ** 0.25
LN_EPS = 1e-5
RMS_EPS = 1e-6

N_CTX_SEQ, CTX_LEN = 16, 256
N_LAT_SEQ, LAT_LEN = 2, 2048
N_CTX = N_CTX_SEQ * CTX_LEN
N_TOK = N_CTX + N_LAT_SEQ * LAT_LEN
ROW_TILE = 256
N_TILES = N_TOK // ROW_TILE
CTX_TILES = N_CTX // ROW_TILE
LAT_TILES_PER_SEQ = LAT_LEN // ROW_TILE
N_SEQ = N_CTX_SEQ + N_LAT_SEQ
MLA_HEAD_LANES = 128
MERGE_ROW_TILE = 512
MOE_ROW_TILE = 256
MOE_SPLITS = 2
MOE_ROWS = N_TOK * TOP_K + N_EXPERTS * MOE_ROW_TILE

_ORIG = dict(aq=(0, 256), ak=(256, 384), av=(384, 512), bcq=(512, 704), bckv=(704, 832), bkpe=(832, 864),
             cq=(864, 992), ck=(992, 1120), cv=(1120, 1376), cog=(1376, 1632), caf=(1632, 1648), cab=(1648, 1664),
             zd=(1664, 2816))
_ORDER = ("aq", "ak", "av", "cq", "ck", "cv", "cog", "zd", "bcq", "caf", "cab", "bkpe", "bckv")
COL = {}
_off = 0
for _name in _ORDER:
    _w = _ORIG[_name][1] - _ORIG[_name][0]
    COL[_name] = (_off, _off + _w)
    _off += _w
SMALL_COLS = _off
G_START = 2816


def _cs(name):
    return slice(*COL[name])


def _split3(x):
    hi = x.astype(BF16)
    r1 = x - hi.astype(F32)
    mid = r1.astype(BF16)
    lo = (r1 - mid.astype(F32)).astype(BF16)
    return hi, mid, lo


def _split2(x):
    hi = x.astype(BF16)
    lo = (x - hi.astype(F32)).astype(BF16)
    return hi, lo


def _bdot(a, b, dims):
    return lax.dot_general(a, b, dims, preferred_element_type=F32)


_D2 = (((1,), (0,)), ((), ()))
_D2T = (((1,), (1,)), ((), ()))
_NN = (((2,), (1,)), ((0,), (0,)))
_NT = (((2,), (2,)), ((0,), (0,)))
_TN = (((1,), (1,)), ((0,), (0,)))


def _dot1(a, b, dims=_D2):
    return _bdot(a.astype(BF16), b.astype(BF16), dims)


def _dot3(a, b, dims=_D2):
    ah, al = _split2(a)
    bh, bl = _split2(b)
    return _bdot(ah, bh, dims) + (_bdot(ah, bl, dims) + _bdot(al, bh, dims))


def _dot_exact_lhs(a01, b, dims=_D2):
    a = a01.astype(BF16)
    h, l = _split2(b)
    return _bdot(a, h, dims) + _bdot(a, l, dims)


def _dot_exact_rhs(a, b01, dims=_D2):
    b = b01.astype(BF16)
    h, l = _split2(a)
    return _bdot(h, b, dims) + _bdot(l, b, dims)


def _sigmoid(x):
    return 0.5 * jnp.tanh(0.5 * x) + 0.5


def _softplus(x):
    return jnp.maximum(x, 0.0) + jnp.log(1.0 + jnp.exp(-jnp.abs(x)))


def _mod_row(t, tile=ROW_TILE):
    return jnp.where(t < N_CTX // tile, 0, 1 + (t - N_CTX // tile) // (LAT_LEN // tile))


def _full(shape):
    nd = len(shape)
    return pl.BlockSpec(shape, lambda *_: (0,) * nd)


def _ctx_rows_spec(width, tile=ROW_TILE):
    return pl.BlockSpec((tile, width), lambda t: (jnp.minimum(t, N_CTX // tile - 1), 0))


def _lat_rows_spec(width, tile=ROW_TILE):
    return pl.BlockSpec((tile, width), lambda t: (jnp.maximum(t - N_CTX // tile, 0), 0))


def _tile_rows(t, ctx_ref, lat_ref):
    return jnp.where(t < N_CTX // ctx_ref.shape[0], ctx_ref[...], lat_ref[...])


MOD_COL_TILE = 1536


def _mod_kernel(c_ref, w_ref, b_ref, o_ref):
    c = c_ref[...]
    o_ref[0] = _dot3(c * _sigmoid(c), w_ref[0]) + b_ref[0]


def modulation_table(cond8, w_mod, b_mod):
    depth, d, six_d = w_mod.shape
    return pl.pallas_call(
        _mod_kernel,
        grid=(depth, six_d // MOD_COL_TILE),
        in_specs=[pl.BlockSpec((8, d), lambda l, j: (0, 0)),
                  pl.BlockSpec((1, d, MOD_COL_TILE), lambda l, j: (l, 0, j)),
                  pl.BlockSpec((1, 1, MOD_COL_TILE), lambda l, j: (l, 0, j))],
        out_specs=pl.BlockSpec((1, 8, MOD_COL_TILE), lambda l, j: (l, 0, j)),
        out_shape=jax.ShapeDtypeStruct((depth, 8, six_d), F32),
        compiler_params=pltpu.CompilerParams(dimension_semantics=("parallel", "parallel")),
        name="modulation",
    )(cond8, w_mod, b_mod.reshape(depth, 1, six_d))


WPREP_ROWS = 128


def _wprep_kernel(w_ref, small_ref, gate_ref):
    for name in _ORDER:
        lo, hi = _ORIG[name]
        small_ref[0, :, _cs(name)] = w_ref[0, :, lo:hi].astype(BF16)
    gate_ref[0] = w_ref[0, :, G_START:].astype(BF16)


def prepare_in_weights(w_in):
    depth, d, cols = w_in.shape
    return pl.pallas_call(
        _wprep_kernel,
        grid=(depth, d // WPREP_ROWS),
        in_specs=[pl.BlockSpec((1, WPREP_ROWS, cols), lambda l, r: (l, r, 0))],
        out_specs=[pl.BlockSpec((1, WPREP_ROWS, SMALL_COLS), lambda l, r: (l, r, 0)),
                   pl.BlockSpec((1, WPREP_ROWS, cols - G_START), lambda l, r: (l, r, 0))],
        out_shape=[jax.ShapeDtypeStruct((depth, d, SMALL_COLS), BF16),
                   jax.ShapeDtypeStruct((depth, d, cols - G_START), BF16)],
        compiler_params=pltpu.CompilerParams(dimension_semantics=("parallel", "parallel")),
        name="prepare_in_weights",
    )(w_in)


def _rot_pairs(x, half, lane_mod_base=0):
    w = x.shape[-1]
    lane = lax.broadcasted_iota(jnp.int32, (1, w), 1) - lane_mod_base
    first = (lane % (2 * half)) < half
    return jnp.where(first, -pltpu.roll(x, w - half, axis=1), pltpu.roll(x, half, axis=1))


def _pre_kernel(xc_ref, xl_ref, xp_ref, xn_ref, mod_ref, w_ref, ca_ref, sa_ref, cb_ref, sb_ref, ck_ref, sk_ref,
                qnorm_ref, kvnorm_ref, wuq_ref, cwg_ref, cbg_ref, rep_ref, mu_ref, dw0_ref, dw2_ref, da0_ref,
                da2_ref, dg2_ref, dkk_ref, dka_ref, drk_ref, bd_ref,
                aq_o, ak_o, av_o, bq_o, bckv_o, bkpe_o, cq4_o, ck4_o, claf_o, clab_o, cv_o, cgate_o,
                r_o, v_o, kk_o, lwf_o, lwb_o, kf_o, kb_o, af_o, ab_o, bonus_o, dgate_o):
    t = pl.program_id(0)
    tm = xc_ref.shape[0]
    sh1 = mod_ref[0, 0:1, :]
    sc1 = mod_ref[0, 1:2, :]

    def modulate(xv):
        return (xv * (1.0 + sc1) + sh1).astype(BF16)

    x_tile = _tile_rows(t, xc_ref, xl_ref)
    h_all = jnp.concatenate([modulate(x_tile), modulate(xp_ref[...]), modulate(xn_ref[...])], axis=0)
    z_all = jnp.dot(h_all, w_ref[0], preferred_element_type=F32)
    z = z_all[:tm]

    aq = z[:, _cs("aq")]
    ak = z[:, _cs("ak")]
    aq_o[...] = aq * ca_ref[...] + _rot_pairs(aq, A_HD // 4) * sa_ref[...]
    ak_o[...] = ak * ca_ref[:, :ak.shape[1]] + _rot_pairs(ak, A_HD // 4) * sa_ref[:, :ak.shape[1]]
    av_o[...] = z[:, _cs("av")]

    bcq = z[:, _cs("bcq")]
    qn = bcq * lax.rsqrt(jnp.mean(bcq * bcq, axis=-1, keepdims=True) + RMS_EPS) * qnorm_ref[...]
    bq = _dot1(qn, wuq_ref[...])
    bq_o[...] = bq * cb_ref[...] + _rot_pairs(bq, B_ROPE // 4, lane_mod_base=B_NOPE) * sb_ref[...]
    bckv = z[:, _cs("bckv")]
    bckv_o[...] = bckv * lax.rsqrt(jnp.mean(bckv * bckv, axis=-1, keepdims=True) + RMS_EPS) * kvnorm_ref[...]
    kpe_lo = COL["bkpe"][0] // 128 * 128
    kblk = z[:, kpe_lo:kpe_lo + 128]
    kblk = kblk * ck_ref[...] + _rot_pairs(kblk, B_ROPE // 4) * sk_ref[...]
    bkpe_o[...] = kblk[:, COL["bkpe"][0] - kpe_lo:COL["bkpe"][1] - kpe_lo]

    rep = rep_ref[...]
    cq4_o[...] = _dot1(z[:, _cs("cq")] * (C_DK ** -0.5), rep)
    ck4_o[...] = _dot1(z[:, _cs("ck")], rep)
    cv_o[...] = z[:, _cs("cv")]
    cog = z[:, _cs("cog")]
    cgate_o[...] = cog * _sigmoid(cog)
    for direction, (name, out) in enumerate((("caf", claf_o), ("cab", clab_o))):
        pre = _dot1(z[:, _cs(name)], cwg_ref[direction]) + cbg_ref[direction]
        la_hi, la_lo = _split2(-_softplus(-pre) * (1.0 / C_GATE_TEMP))
        out[...] = _bdot(la_hi, rep, _D2) + _bdot(la_lo, rep, _D2)

    zd_cols = _cs("zd")
    zd = z[:, zd_cols]
    j = (t - CTX_TILES) % LAT_TILES_PER_SEQ
    latent = t >= CTX_TILES
    has_prev = jnp.logical_and(latent, j != 0)
    has_next = jnp.logical_and(latent, j != LAT_TILES_PER_SEQ - 1)
    prev_row = jnp.where(has_prev, z_all[tm + 7:tm + 8, zd_cols], 0.0)
    next_row = jnp.where(has_next, z_all[tm + 8:tm + 9, zd_cols], 0.0)
    row = lax.broadcasted_iota(jnp.int32, (tm, 1), 0)
    up = jnp.where(row == 0, prev_row, pltpu.roll(zd, 1, axis=0))
    dn = jnp.where(row == tm - 1, next_row, pltpu.roll(zd, tm - 1, axis=0))
    zd = zd + (0.5 * (up + dn) - zd) * mu_ref[...]

    hn = D_HEADS * D_N
    d_r, d_k, d_v = zd[:, :hn], zd[:, hn:2 * hn], zd[:, 2 * hn:3 * hn]
    o = 3 * hn
    d_w = (zd[:, o:o + D_DECAY_RANK], zd[:, o + D_DECAY_RANK:o + 2 * D_DECAY_RANK])
    o += 2 * D_DECAY_RANK
    d_a = (zd[:, o:o + D_AAA_RANK], zd[:, o + D_AAA_RANK:o + 2 * D_AAA_RANK])
    o += 2 * D_AAA_RANK
    d_g = zd[:, o:o + D_GATE_RANK]
    bd = bd_ref[...]
    kk = d_k * dkk_ref[...]
    kk = kk / jnp.maximum(jnp.sqrt(_dot_exact_rhs(kk * kk, bd)), 1e-12)
    r_o[...] = d_r
    v_o[...] = d_v
    kk_o[...] = kk
    k_sum = None
    for direction, (lw_o, k_o, a_o) in enumerate(((lwf_o, kf_o, af_o), (lwb_o, kb_o, ab_o))):
        w_log = -_softplus(-(dw0_ref[direction] + _dot1(jnp.tanh(d_w[direction]), dw2_ref[direction]))) - 0.5
        lw_o[...] = -jnp.exp(w_log)
        a = _sigmoid(da0_ref[direction] + _dot1(d_a[direction], da2_ref[direction]))
        k_dir = d_k * (1.0 + (a - 1.0) * dka_ref[...])
        k_o[...] = k_dir
        a_o[...] = a
        k_sum = k_dir if k_sum is None else k_sum + k_dir
    bonus_o[...] = d_v * _dot_exact_rhs(d_r * drk_ref[...] * k_sum, bd)
    dgate_o[...] = _dot1(_sigmoid(d_g), dg2_ref[...])


def _rope_tables():
    pos = np.arange(LAT_LEN)
    rowp, colp = (pos // GRID_W).astype(np.float32), (pos % GRID_W).astype(np.float32)

    f32 = np.float32

    def head_tables(rot_dim):
        quarter = rot_dim // 4
        inv = (f32(ROPE_BASE) ** (-np.arange(quarter, dtype=f32) / f32(quarter))).astype(f32)
        ar = (rowp[:, None] * inv).astype(f32)
        ac = (colp[:, None] * inv).astype(f32)
        cos = np.concatenate([np.cos(ar), np.cos(ar), np.cos(ac), np.cos(ac)], axis=-1).astype(f32)
        sin = np.concatenate([np.sin(ar), np.sin(ar), np.sin(ac), np.sin(ac)], axis=-1).astype(f32)
        return cos, sin

    def with_identity(c, s):
        w = c.shape[1]
        return (jnp.asarray(np.concatenate([np.ones((ROW_TILE, w), f32), c], axis=0)),
                jnp.asarray(np.concatenate([np.zeros((ROW_TILE, w), f32), s], axis=0)))

    ca, sa = head_tables(A_HD)
    ca, sa = with_identity(np.tile(ca, (1, A_HEADS)), np.tile(sa, (1, A_HEADS)))
    cbh, sbh = head_tables(B_ROPE)
    ones, zeros = np.ones((LAT_LEN, B_NOPE), f32), np.zeros((LAT_LEN, B_NOPE), f32)
    qpad = MLA_HEAD_LANES - B_NOPE - B_ROPE
    cb, sb = with_identity(
        np.tile(np.concatenate([ones, cbh, np.ones((LAT_LEN, qpad), f32)], axis=1), (1, B_HEADS)),
        np.tile(np.concatenate([zeros, sbh, np.zeros((LAT_LEN, qpad), f32)], axis=1), (1, B_HEADS)))
    pad = 128 - B_ROPE
    ck, sk = with_identity(np.concatenate([np.ones((LAT_LEN, pad), f32), cbh], axis=1),
                           np.concatenate([np.zeros((LAT_LEN, pad), f32), sbh], axis=1))
    return ca, sa, cb, sb, ck, sk


def _lane_repeat_matrix():
    m = np.zeros((C_HEADS * C_DK, C_HEADS * 128), np.float32)
    for h in range(C_HEADS):
        for g in range(128 // C_DK):
            for d in range(C_DK):
                m[h * C_DK + d, h * 128 + g * C_DK + d] = 1.0
    return jnp.asarray(m, BF16)


def _head_block_diag():
    m = np.kron(np.eye(D_HEADS, dtype=np.float32), np.ones((D_N, D_N), np.float32))
    return jnp.asarray(m, BF16)


PRE_OUT_WIDTHS = (256, 128, 128, B_HEADS * MLA_HEAD_LANES, 128, 32, 512, 512, 512, 512, 256, 256) + (256,) * 11


def _layer_block(arr, layer):
    nd = arr.ndim
    return pl.BlockSpec((1,) + arr.shape[1:], lambda *_: (layer,) + (0,) * (nd - 1))


def _layer_item(arr, index):
    nd = arr.ndim
    return pl.BlockSpec((None,) + arr.shape[1:], lambda *_: (index,) + (0,) * (nd - 1))


def _mod_spec(layer, tile=ROW_TILE):
    return pl.BlockSpec((1, 6, D_MODEL), lambda t: (layer * (1 + N_LAT_SEQ) + _mod_row(t, tile), 0, 0))


def stacked_params(p):
    depth = p['d_mu'].shape[0]
    hn = D_HEADS * D_N
    w_uq = p['b_w_uq'].reshape(depth, B_Q_LORA, B_HEADS, B_NOPE + B_ROPE)
    w_uq = jnp.pad(w_uq, ((0, 0), (0, 0), (0, 0), (0, MLA_HEAD_LANES - B_NOPE - B_ROPE)))
    row = lambda a: a.reshape(depth, 1, -1)
    return dict(
        b_q_norm=row(p['b_q_norm']), b_kv_norm=row(p['b_kv_norm']), w_uq=w_uq.reshape(depth, B_Q_LORA, -1),
        c_w_gate=p['c_w_gate'], c_b_gate=p['c_b_gate'].reshape(depth, 2, 1, -1), d_mu=row(p['d_mu']),
        d_w0=p['d_w0'].reshape(depth, 2, 1, hn), d_w2=p['d_w2'], d_a0=p['d_a0'].reshape(depth, 2, 1, hn),
        d_a2=p['d_a2'], d_g2=p['d_g2'], d_k_k=row(p['d_k_k']), d_k_a=row(p['d_k_a']), d_r_k=row(p['d_r_k']),
        c_norm=row(jnp.tile(p['c_norm'], (1, C_HEADS))), d_ln_g=row(p['d_ln_g']), d_ln_b=row(p['d_ln_b']),
        ln_g=p['ln_g'].reshape(depth * 2, 1, -1), ln_b=p['ln_b'].reshape(depth * 2, 1, -1),
        w_router=p['w_router'], b_router=row(p['b_router']), a_sink=p['a_sink'].reshape(-1), b_w_ukv=p['b_w_ukv'])


def mixer_prelude(x_ctx, x_lat, mod, w_small, layer, tables, sp):
    tm = ROW_TILE
    lat8 = lambda t: (t - CTX_TILES) * (tm // 8)
    last8 = x_lat.shape[0] // 8 - 1
    tab_idx = lambda t: (jnp.where(t < CTX_TILES, 0, 1 + (t - CTX_TILES) % LAT_TILES_PER_SEQ), 0)
    small = [sp[k] for k in ('b_q_norm', 'b_kv_norm', 'w_uq', 'c_w_gate', 'c_b_gate')] + [_lane_repeat_matrix()]
    small += [sp[k] for k in ('d_mu', 'd_w0', 'd_w2', 'd_a0', 'd_a2', 'd_g2', 'd_k_k', 'd_k_a', 'd_r_k')]
    small += [_head_block_diag()]
    const = lambda a: _full(a.shape) if a.dtype == BF16 else _layer_item(a, layer)
    in_specs = ([_ctx_rows_spec(D_MODEL), _lat_rows_spec(D_MODEL),
                 pl.BlockSpec((8, D_MODEL), lambda t: (jnp.clip(lat8(t) - 1, 0, last8), 0)),
                 pl.BlockSpec((8, D_MODEL), lambda t: (jnp.clip(lat8(t + 1), 0, last8), 0)),
                 _mod_spec(layer),
                 _layer_block(w_small, layer)]
                + [pl.BlockSpec((tm, tab.shape[1]), tab_idx) for tab in tables]
                + [const(a) for a in small])
    return pl.pallas_call(
        _pre_kernel,
        grid=(N_TILES,),
        in_specs=in_specs,
        out_specs=[pl.BlockSpec((tm, w), lambda t: (t, 0)) for w in PRE_OUT_WIDTHS],
        out_shape=[jax.ShapeDtypeStruct((N_TOK, w), F32) for w in PRE_OUT_WIDTHS],
        compiler_params=pltpu.CompilerParams(dimension_semantics=("parallel",), vmem_limit_bytes=VMEM_LIMIT),
        name="mixer_prelude",
    )(x_ctx, x_lat, x_lat, x_lat, mod, w_small, *tables, *small)


ATT_Q_BLOCK = 128
MLA_Q_BLOCK = 256
ATT_WINDOW = 128
ATT_NEG_INF = -1e30
CACHE_LEN = 512


def _softmax_pv(s, v, sink):
    dv = v.shape[1] // 2
    m = jnp.max(s, axis=-1, keepdims=True)
    if sink is not None:
        m = jnp.maximum(m, sink)
    e = jnp.exp((s - m).astype(BF16))
    o = jnp.dot(e, v, preferred_element_type=F32)
    den = o[:, dv:dv + 1]
    if sink is not None:
        den = den + jnp.exp(sink - m)
    return o[:, :dv] / den


def _with_ones(v):
    return jnp.concatenate([v.astype(BF16), jnp.ones(v.shape, BF16)], axis=1)


def _gqa_kernel(sink_ref, q_ref, k_ref, v_ref, *rest, hd, group, scale, windowed, sink_base):
    if windowed:
        kp_ref, kn_ref, vp_ref, vn_ref, kc_ref, vc_ref, _, o_ref = rest
    else:
        (o_ref,) = rest
    i = pl.program_id(1)
    tq = q_ref.shape[0]
    n_kv = k_ref.shape[1] // hd
    if windowed:
        qpos = i * tq + lax.broadcasted_iota(jnp.int32, (tq, 3 * tq), 0)
        kpos = (i - 1) * tq + lax.broadcasted_iota(jnp.int32, (tq, 3 * tq), 1)
        n_tok = pl.num_programs(1) * tq
        mask = (jnp.abs(qpos - kpos) <= ATT_WINDOW) & (kpos >= 0) & (kpos < n_tok)
        mask = jnp.concatenate([mask] * group, axis=0)
    for kvh in range(n_kv):
        ks = slice(kvh * hd, (kvh + 1) * hd)
        qs = [q_ref[:, (kvh * group + g) * hd:(kvh * group + g + 1) * hd] for g in range(group)]
        q = (jnp.concatenate(qs, axis=0) * scale).astype(BF16)
        sink = jnp.concatenate(
            [jnp.full((tq, 1), sink_ref[sink_base + kvh * group + g], F32) for g in range(group)], axis=0)
        if windowed:
            k_win = jnp.concatenate([kp_ref[:, ks], k_ref[:, ks], kn_ref[:, ks]], axis=0)
            v_win = jnp.concatenate([vp_ref[:, ks], v_ref[:, ks], vn_ref[:, ks]], axis=0)
            s_win = _bdot(q, k_win.astype(BF16), _D2T)
            s_win = jnp.where(mask, s_win, ATT_NEG_INF)
            s_ctx = _bdot(q, kc_ref[0, 0, kvh].astype(BF16), _D2T)
            s = jnp.concatenate([s_win, s_ctx], axis=1)
            v = jnp.concatenate([v_win, vc_ref[0, 0, kvh]], axis=0)
        else:
            s = _bdot(q, k_ref[:, ks].astype(BF16), _D2T)
            v = v_ref[:, ks]
        o = _softmax_pv(s, _with_ones(v), sink)
        for g in range(group):
            h = kvh * group + g
            o_ref[:, h * hd:(h + 1) * hd] = o[g * tq:(g + 1) * tq]


def gqa_attention(q, k, v, sink, cache_k, cache_v, layer):
    qw, kw = q.shape[1], k.shape[1]
    group = qw // kw
    scale = A_HD ** -0.5
    params = pltpu.CompilerParams(dimension_semantics=("parallel", "parallel"))
    out_shape = jax.ShapeDtypeStruct((N_TOK, qw), F32)
    ctx_spec = lambda w: pl.BlockSpec((CTX_LEN, w), lambda s, i, sk: (s, 0))
    o = pl.pallas_call(
        functools.partial(_gqa_kernel, hd=A_HD, group=group, scale=scale, windowed=False, sink_base=layer * A_HEADS),
        grid_spec=pltpu.PrefetchScalarGridSpec(
            num_scalar_prefetch=1, grid=(N_CTX_SEQ, 1), in_specs=[ctx_spec(qw), ctx_spec(kw), ctx_spec(kw)],
            out_specs=ctx_spec(qw)),
        out_shape=out_shape, compiler_params=params, name="gqa_full",
    )(sink, q, k, v)
    tq = ATT_Q_BLOCK
    nb = LAT_LEN // tq
    base = N_CTX // tq
    blk = lambda w, f: pl.BlockSpec((tq, w), lambda b, i, sk: (base + nb * b + f(i), 0))
    same = lambda i: i
    prev = lambda i: jnp.maximum(i - 1, 0)
    nxt = lambda i: jnp.minimum(i + 1, nb - 1)
    cspec = pl.BlockSpec((1, 1) + cache_k.shape[2:], lambda b, i, sk: (b, layer, 0, 0, 0))
    return pl.pallas_call(
        functools.partial(_gqa_kernel, hd=A_HD, group=group, scale=scale, windowed=True, sink_base=layer * A_HEADS),
        grid_spec=pltpu.PrefetchScalarGridSpec(
            num_scalar_prefetch=1, grid=(N_LAT_SEQ, nb),
            in_specs=[blk(qw, same), blk(kw, same), blk(kw, same), blk(kw, prev), blk(kw, nxt), blk(kw, prev),
                      blk(kw, nxt), cspec, cspec, pl.BlockSpec(memory_space=pl.ANY)],
            out_specs=blk(qw, same)),
        out_shape=out_shape, input_output_aliases={10: 0}, compiler_params=params, name="gqa_windowed",
    )(sink, q, k, v, k, k, v, v, cache_k, cache_v, o)


def _mla_kernel(q_ref, ckv_ref, kpe_ref, wukv_ref, *rest, n_heads, nope, rope, vd, scale, cached):
    if cached:
        cckv_ref, ckpe_ref, _, o_ref, k_scr, vext_scr = rest
    else:
        o_ref, k_scr, vext_scr = rest
    i = pl.program_id(1)
    n_cache = k_scr.shape[0] - ckv_ref.shape[0]
    hw = nope + vd
    hl = MLA_HEAD_LANES

    @pl.when(i == 0)
    def _():
        w = wukv_ref[...].astype(BF16)

        def expand(rows, kpe_rows, lo, hi):
            kv = jnp.dot(rows.astype(BF16), w, preferred_element_type=F32).astype(BF16)
            n = hi - lo
            kpe = kpe_rows.astype(BF16)
            for h in range(n_heads):
                k_scr[lo:hi, hl * h:hl * (h + 1)] = jnp.concatenate(
                    [kv[:, h * hw:h * hw + nope], kpe, jnp.zeros((n, hl - nope - rope), BF16)], axis=1)
                vext_scr[lo:hi, 2 * vd * h:2 * vd * (h + 1)] = jnp.concatenate(
                    [kv[:, h * hw + nope:(h + 1) * hw], jnp.ones((n, vd), BF16)], axis=1)

        if cached:
            expand(cckv_ref[0, 0], ckpe_ref[0, 0], 0, n_cache)
        expand(ckv_ref[...], kpe_ref[...], n_cache, k_scr.shape[0])

    for h in range(n_heads):
        qh = (q_ref[:, hl * h:hl * (h + 1)] * scale).astype(BF16)
        s = _bdot(qh, k_scr[:, hl * h:hl * (h + 1)], _D2T)
        o_ref[:, h * vd:(h + 1) * vd] = _softmax_pv(s, vext_scr[:, 2 * vd * h:2 * vd * (h + 1)], None)


def mla_attention(q, ckv, kpe, w_ukv, cache_ckv, cache_kpe, layer):
    qw = q.shape[1]
    tq = MLA_Q_BLOCK
    kw = dict(n_heads=B_HEADS, nope=B_NOPE, rope=B_ROPE, vd=B_VD, scale=(B_NOPE + B_ROPE) ** -0.5)
    params = pltpu.CompilerParams(dimension_semantics=("parallel", "arbitrary"))
    out_shape = jax.ShapeDtypeStruct((N_TOK, B_HEADS * B_VD), F32)
    scratch = lambda rows: [pltpu.VMEM((rows, B_HEADS * MLA_HEAD_LANES), BF16),
                            pltpu.VMEM((rows, 2 * B_HEADS * B_VD), BF16)]
    nbc = CTX_LEN // tq
    o = pl.pallas_call(
        functools.partial(_mla_kernel, cached=False, **kw),
        grid=(N_CTX_SEQ, nbc),
        in_specs=[pl.BlockSpec((tq, qw), lambda s, i: (s * nbc + i, 0)),
                  pl.BlockSpec((CTX_LEN, B_KV_LORA), lambda s, i: (s, 0)),
                  pl.BlockSpec((CTX_LEN, B_ROPE), lambda s, i: (s, 0)),
                  _layer_item(w_ukv, layer)],
        out_specs=pl.BlockSpec((tq, B_HEADS * B_VD), lambda s, i: (s * nbc + i, 0)),
        out_shape=out_shape,
        scratch_shapes=scratch(CTX_LEN),
        compiler_params=params, name="mla_context",
    )(q, ckv, kpe, w_ukv)
    nb = LAT_LEN // tq
    base = N_CTX // tq
    lat0 = N_CTX // LAT_LEN
    s_len = CACHE_LEN + LAT_LEN
    return pl.pallas_call(
        functools.partial(_mla_kernel, cached=True, **kw),
        grid=(N_LAT_SEQ, nb),
        in_specs=[pl.BlockSpec((tq, qw), lambda b, i: (base + nb * b + i, 0)),
                  pl.BlockSpec((LAT_LEN, B_KV_LORA), lambda b, i: (lat0 + b, 0)),
                  pl.BlockSpec((LAT_LEN, B_ROPE), lambda b, i: (lat0 + b, 0)),
                  _layer_item(w_ukv, layer),
                  pl.BlockSpec((1, 1, CACHE_LEN, B_KV_LORA), lambda b, i: (b, layer, 0, 0)),
                  pl.BlockSpec((1, 1, CACHE_LEN, B_ROPE), lambda b, i: (b, layer, 0, 0)),
                  pl.BlockSpec(memory_space=pl.ANY)],
        out_specs=pl.BlockSpec((tq, B_HEADS * B_VD), lambda b, i: (base + nb * b + i, 0)),
        out_shape=out_shape, input_output_aliases={6: 0},
        scratch_shapes=scratch(s_len),
        compiler_params=params, name="mla_latent",
    )(q, ckv, kpe, w_ukv, cache_ckv, cache_kpe, o)


CHUNK = 64
GLA_SUB = 16
PAIR = 2
N_CHAIN = PAIR * 2 * 4


def _is_back(shape):
    return (lax.broadcasted_iota(jnp.int32, shape, 0) // 4) % 2 == 1


def _chains(ref_f, ref_b, width):
    return jnp.stack([ref[0, s, 0, :, h * width:(h + 1) * width]
                      for s in range(PAIR) for ref in (ref_f, ref_b) for h in range(4)], axis=0)


def _unchain(y, o_f, o_b):
    for s in range(PAIR):
        o_f[0, s, 0] = jnp.concatenate([y[s * 8 + h] for h in range(4)], axis=-1)
        o_b[0, s, 0] = jnp.concatenate([y[s * 8 + 4 + h] for h in range(4)], axis=-1)


def _dir_masks(L):
    shape = (N_CHAIN, L, L)
    back = _is_back(shape)
    row = lax.broadcasted_iota(jnp.int32, shape, 1)
    col = lax.broadcasted_iota(jnp.int32, shape, 2)
    ahead = jnp.where(back, col - row, row - col)
    return ahead >= 0, ahead > 0, row == col


def _chunk_end(ci):
    L = ci.shape[1]
    return jnp.where(_is_back((N_CHAIN, 1, 1)), ci[:, 0:1], ci[:, L - 1:L])


def _split_refs(refs, n_in, has_s0, has_sfin):
    ins = refs[:n_in]
    pos = n_in
    s0_ref = None
    if has_s0:
        s0_ref = refs[pos]
        pos += 3
    of_ref, ob_ref = refs[pos], refs[pos + 1]
    pos += 2
    sfin_ref = refs[pos] if has_sfin else None
    return ins, s0_ref, of_ref, ob_ref, sfin_ref, refs[-1]


def _init_state(s_scr, s0_ref):
    @pl.when(pl.program_id(1) == 0)
    def _():
        if s0_ref is None:
            s_scr[...] = jnp.zeros_like(s_scr)
        else:
            s_scr[...] = s0_ref[0]


def _emit_state(sfin_ref, s_new):
    if sfin_ref is None:
        return

    @pl.when(pl.program_id(1) == pl.num_programs(1) - 1)
    def _():
        sfin_ref[0] = s_new


def _rwkv_kernel(*refs, dot, has_s0, has_sfin):
    (rf, rb, vf, vb, kkf, kkb, lwf, lwb, kf, kb, af, ab), s0_ref, yf_ref, yb_ref, sfin_ref, s_scr = _split_refs(
        refs, 12, has_s0, has_sfin)
    _init_state(s_scr, s0_ref)
    n = D_N
    r = _chains(rf, rb, n)
    v = _chains(vf, vb, n)
    kk = _chains(kkf, kkb, n)
    lw = _chains(lwf, lwb, n)
    k = _chains(kf, kb, n)
    a = _chains(af, ab, n)
    L = r.shape[1]
    S = s_scr[...]
    incl, strict, diag = _dir_masks(L)
    ci = _dot_exact_lhs(jnp.where(incl, 1.0, 0.0), lw, _NN)
    ce = ci - lw
    cl = _chunk_end(ci)
    e_neg = jnp.exp(-ci)
    b = a * kk
    alpha = kk * jnp.exp(ce)
    rho = r * jnp.exp(ci)
    beta = b * e_neg
    kappa = k * e_neg
    e_end = jnp.exp(cl - ci)
    ar = jnp.concatenate([alpha, rho], axis=1)
    bk = jnp.concatenate([beta, kappa], axis=1)
    w = dot(ar, bk, _NT)
    nmat = jnp.where(strict, w[:, :L, :L], 0.0)
    mmat = jnp.where(strict, w[:, :L, L:], 0.0)
    p1 = jnp.where(incl, w[:, L:, :L], 0.0)
    p2 = jnp.where(incl, w[:, L:, L:], 0.0)
    x = jnp.where(diag, 1.0, 0.0) - nmat
    p = dot(nmat, nmat, _NN)
    span = 2
    while True:
        x = x + dot(x, p, _NN)
        span *= 2
        if span >= L:
            break
        p = dot(p, p, _NN)
    us = dot(ar, S, _NT)
    rhs = us[:, :L] + dot(mmat, v, _NN)
    d = -dot(x, rhs, _NN)
    dv = jnp.concatenate([d, v], axis=1)
    pp = jnp.concatenate([p1, p2], axis=2)
    _unchain(us[:, L:] + dot(pp, dv, _NN), yf_ref, yb_ref)
    bk_end = jnp.concatenate([b * e_end, k * e_end], axis=1)
    s_new = S * jnp.exp(cl) + dot(dv, bk_end, _TN)
    s_scr[...] = s_new
    _emit_state(sfin_ref, s_new)


def _gla_kernel(*refs, dot, has_s0, has_sfin):
    (qf, qb, kf, kb, vf, vb, laf, lab), s0_ref, of_ref, ob_ref, sfin_ref, s_scr = _split_refs(
        refs, 8, has_s0, has_sfin)
    _init_state(s_scr, s0_ref)
    q4 = _chains(qf, qb, 128)
    k4 = _chains(kf, kb, 128)
    la4 = _chains(laf, lab, 128)
    v = _chains(vf, vb, C_DV)
    g, L, lanes = q4.shape
    dk = C_DK
    n_sub = L // GLA_SUB
    st = s_scr[...]
    incl, _, _ = _dir_masks(L)
    c = _dot_exact_lhs(jnp.where(incl, 1.0, 0.0), la4, _NN)
    shape = (g, L, lanes)
    back = _is_back(shape)
    lane_blk = lax.broadcasted_iota(jnp.int32, shape, 2) // dk
    row_blk = lax.broadcasted_iota(jnp.int32, shape, 1) // GLA_SUB
    cref_f = jnp.zeros(shape, F32)
    cref_b = jnp.zeros(shape, F32)
    for j in range(1, n_sub):
        cref_f = jnp.where(lane_blk == j, c[:, j * GLA_SUB - 1:j * GLA_SUB], cref_f)
        cref_b = jnp.where(lane_blk == j - 1, c[:, j * GLA_SUB:j * GLA_SUB + 1], cref_b)
    cref = jnp.where(back, cref_b, cref_f)
    q_on = row_blk == lane_blk
    k_on = jnp.where(back, row_blk - lane_blk, lane_blk - row_blk) >= 0
    qh = jnp.where(q_on, q4 * jnp.exp(jnp.where(q_on, c - cref, 0.0)), 0.0)
    kh = jnp.where(k_on, k4 * jnp.exp(jnp.where(k_on, cref - c, 0.0)), 0.0)
    att = jnp.where(incl, dot(qh, kh, _NT), 0.0)
    cl = _chunk_end(c)
    qe = (q4 * jnp.exp(c))[:, :, :dk]
    ke = (k4 * jnp.exp(cl - c))[:, :, :dk]
    _unchain(dot(qe, st, _NT) + dot(att, v, _NN), of_ref, ob_ref)
    s_new = st * jnp.exp(cl[:, :, :dk]) + dot(v, ke, _TN)
    s_scr[...] = s_new
    _emit_state(sfin_ref, s_new)


def _recurrence_calls(kernel_fn, name, pairs, singles_f, singles_b, s0_lat, state_dims, out_width):
    def run(view, grid, group, s0, prev_out):
        nc = view[2]
        fwd_map = lambda p, c: (group(p), 0, c, 0, 0)
        bwd_map = lambda p, c: (group(p), 0, nc - 1 - c, 0, 0)
        blk = lambda w: (1, PAIR, 1, CHUNK, w)
        args, in_specs = [], []
        for af, ab in [(a, a) for a in pairs] + list(zip(singles_f, singles_b)):
            w = af.shape[-1]
            args += [af.reshape(view + (w,)), ab.reshape(view + (w,))]
            in_specs += [pl.BlockSpec(blk(w), fwd_map), pl.BlockSpec(blk(w), bwd_map)]
        out_specs = [pl.BlockSpec(blk(out_width), fwd_map), pl.BlockSpec(blk(out_width), bwd_map)]
        out_shape = [jax.ShapeDtypeStruct(view + (out_width,), F32)] * 2
        aliases = {}
        if s0 is not None:
            args += [s0] + [o.reshape(view + (out_width,)) for o in prev_out]
            in_specs += [_full(s0.shape), pl.BlockSpec(memory_space=pl.ANY), pl.BlockSpec(memory_space=pl.ANY)]
            aliases = {len(args) - 2: 0, len(args) - 1: 1}
        else:
            out_specs.append(pl.BlockSpec((1, N_CHAIN) + state_dims, lambda p, c: (p, 0, 0, 0)))
            out_shape.append(jax.ShapeDtypeStruct((grid[0], N_CHAIN) + state_dims, F32))
        return pl.pallas_call(
            functools.partial(kernel_fn, has_s0=s0 is not None, has_sfin=s0 is None),
            grid=grid, in_specs=in_specs, out_specs=out_specs, out_shape=out_shape,
            input_output_aliases=aliases, scratch_shapes=[pltpu.VMEM((N_CHAIN,) + state_dims, F32)],
            compiler_params=pltpu.CompilerParams(dimension_semantics=("parallel", "arbitrary")),
            name=name + ("_latent" if s0 is not None else "_context"),
        )(*args)

    ctx_nc = CTX_LEN // CHUNK
    ctx_view = (N_TOK // (PAIR * CTX_LEN), PAIR, ctx_nc, CHUNK)
    o_f, o_b, s_fin = run(ctx_view, (N_CTX_SEQ // PAIR, ctx_nc), lambda p: p, None, None)
    lat_nc = LAT_LEN // CHUNK
    lat_view = (N_TOK // (PAIR * LAT_LEN), PAIR, lat_nc, CHUNK)
    o_f, o_b = run(lat_view, (1, lat_nc), lambda p: N_CTX // (PAIR * LAT_LEN), s0_lat, (o_f, o_b))
    return o_f.reshape(N_TOK, out_width), o_b.reshape(N_TOK, out_width), s_fin


def _layer_norm(x, g, b):
    mu = jnp.mean(x, axis=-1, keepdims=True)
    xc = x - mu
    var = jnp.mean(xc * xc, axis=-1, keepdims=True)
    return xc * lax.rsqrt(var + LN_EPS) * g + b


def _merge_kernel(xc_ref, xl_ref, mod_ref, oa_ref, ob_ref, cof_ref, cob_ref, cgate_ref, yf_ref, yb_ref, bonus_ref,
                  dgate_ref, wg_ref, wbr_ref, wout_ref, cnorm_ref, dlng_ref, dlnb_ref, lng_ref, lnb_ref, wr_ref, br_ref,
                  bd_ref, x1_o, h2_o, topi_o, topw_o):
    x = _tile_rows(pl.program_id(0), xc_ref, xl_ref)
    m = mod_ref[0]
    sh1, sc1, g1, sh2, sc2 = m[0:1], m[1:2], m[2:3], m[3:4], m[4:5]
    bd = bd_ref[...]
    inv_n = 1.0 / D_N
    co = cof_ref[...] + cob_ref[...]
    o_c = co * lax.rsqrt(_dot_exact_rhs(co * co, bd) * inv_n + RMS_EPS) * cnorm_ref[...] * cgate_ref[...]
    y = yf_ref[...] + yb_ref[...]
    yc = y - _dot_exact_rhs(y, bd) * inv_n
    var = _dot_exact_rhs(yc * yc, bd) * inv_n
    o_d = (yc * lax.rsqrt(var + D_GN_EPS) * dlng_ref[...] + dlnb_ref[...] + bonus_ref[...]) * dgate_ref[...]
    branches = [b.astype(BF16) for b in (oa_ref[...], ob_ref[...], o_c, o_d)]
    h = (x * (1.0 + sc1) + sh1).astype(BF16)
    blocks = []
    for cb in range(D_MODEL // MXU_WIDTH):
        merged = None
        for n in range(N_BRANCH):
            cols = slice(n * D_MODEL + cb * MXU_WIDTH, n * D_MODEL + (cb + 1) * MXU_WIDTH)
            gate = _sigmoid(jnp.dot(h, wg_ref[0, :, cols], preferred_element_type=F32))
            term = gate * jnp.dot(branches[n], wbr_ref[0, n, :, cb * MXU_WIDTH:(cb + 1) * MXU_WIDTH],
                                  preferred_element_type=F32)
            merged = term if merged is None else merged + term
        blocks.append(merged.astype(BF16))
    mix = jnp.dot(jnp.concatenate(blocks, axis=1), wout_ref[0], preferred_element_type=F32)
    x1 = _layer_norm(ALPHA * x + g1 * mix, lng_ref[...], lnb_ref[...])
    x1_o[...] = x1
    h2 = x1 * (1.0 + sc2) + sh2
    h2_o[...] = h2.astype(BF16)
    logits = _dot3(h2, wr_ref[...]) + br_ref[...]
    tm, n_e = logits.shape
    lane_e = lax.broadcasted_iota(jnp.int32, (tm, n_e), 1)
    lane_o = lax.broadcasted_iota(jnp.int32, (tm, topi_o.shape[1]), 1)
    top_i = jnp.zeros((tm, topi_o.shape[1]), jnp.int32)
    top_v = jnp.zeros((tm, topw_o.shape[1]), F32)
    vals = []
    for kth in range(TOP_K):
        mx = jnp.max(logits, axis=-1, keepdims=True)
        idx = jnp.min(jnp.where(logits == mx, lane_e, n_e), axis=-1, keepdims=True)
        vals.append(mx)
        top_i = jnp.where(lane_o == kth, idx, top_i)
        logits = jnp.where(lane_e == idx, -jnp.inf, logits)
    es = [jnp.exp(vk - vals[0]) for vk in vals]
    den = es[0] + es[1] + es[2] + es[3]
    for kth in range(TOP_K):
        top_v = jnp.where(lane_o == kth, es[kth] / den, top_v)
    topi_o[...] = top_i
    topw_o[...] = top_v


def merge_and_route(x_ctx, x_lat, mod, o_a, o_b, co_f, co_b, cgate, y_f, y_b, bonus, dgate, w_g, w_br, w_out, layer, sp):
    tm = MERGE_ROW_TILE
    hn = D_HEADS * D_N
    row = lambda w: pl.BlockSpec((tm, w), lambda t: (t, 0))
    small = [sp[k] for k in ('c_norm', 'd_ln_g', 'd_ln_b', 'ln_g', 'ln_b', 'w_router', 'b_router')]
    index = [layer, layer, layer, 2 * layer, 2 * layer, layer, layer]
    bd = _head_block_diag()
    return pl.pallas_call(
        _merge_kernel,
        grid=(N_TOK // tm,),
        in_specs=([_ctx_rows_spec(D_MODEL, tm), _lat_rows_spec(D_MODEL, tm), _mod_spec(layer, tm)]
                  + [row(hn)] * 9 + [_layer_block(w, layer) for w in (w_g, w_br, w_out)]
                  + [_layer_item(a, i) for a, i in zip(small, index)] + [_full(bd.shape)]),
        out_specs=[row(D_MODEL), row(D_MODEL), row(128), row(128)],
        out_shape=[jax.ShapeDtypeStruct((N_TOK, D_MODEL), F32), jax.ShapeDtypeStruct((MOE_ROWS, D_MODEL), BF16),
                   jax.ShapeDtypeStruct((N_TOK, 128), jnp.int32), jax.ShapeDtypeStruct((N_TOK, 128), F32)],
        compiler_params=pltpu.CompilerParams(dimension_semantics=("parallel",), vmem_limit_bytes=VMEM_LIMIT),
        name="merge_and_route",
    )(x_ctx, x_lat, mod, o_a, o_b, co_f, co_b, cgate, y_f, y_b, bonus, dgate, w_g, w_br, w_out, *small, bd)


def _moe_kernel(te_ref, tv_ref, first_ref, slot_ref, next_ref, x_ref, w1_hbm, b1_ref, w2_hbm, b2_ref, perm_ref, *rest,
                layer):
    y_ref, w1buf, w2buf, sem, w1s, w2s, hs = rest[-7:]
    t = pl.program_id(0)
    valid = tv_ref[t] != 0
    d_model, two_f = w1s.shape
    n_blk = two_f // MXU_WIDTH
    half = MXU_WIDTH // 2

    def fetch(expert, slot):
        return (pltpu.make_async_copy(w1_hbm.at[layer, expert], w1buf.at[slot], sem.at[0, slot]),
                pltpu.make_async_copy(w2_hbm.at[layer, expert], w2buf.at[slot], sem.at[1, slot]))

    @pl.when(t == 0)
    def _():
        for cp in fetch(te_ref[0], 0):
            cp.start()

    @pl.when(first_ref[t] == 1)
    def _():
        slot = slot_ref[t]
        for cp in fetch(te_ref[t], slot):
            cp.wait()

        @pl.when(next_ref[t] >= 0)
        def _():
            for cp in fetch(next_ref[t], 1 - slot):
                cp.start()

        for blk in range(n_blk):
            sl = slice(blk * MXU_WIDTH, (blk + 1) * MXU_WIDTH)
            wb = w1buf[slot, :, sl].astype(BF16)
            w1s[:, sl] = jnp.dot(wb, perm_ref[...], preferred_element_type=F32).astype(BF16)
        w2s[...] = w2buf[slot].astype(BF16)

    @pl.when(valid)
    def _():
        x = x_ref[...]
        for blk in range(n_blk):
            sl = slice(blk * MXU_WIDTH, (blk + 1) * MXU_WIDTH)
            u = jnp.dot(x, w1s[:, sl], preferred_element_type=F32) + b1_ref[0, 0, :, sl]
            glu = jnp.minimum(u[:, :half], SWIGLU_LIMIT)
            lin = jnp.clip(u[:, half:], -SWIGLU_LIMIT, SWIGLU_LIMIT)
            hs[:, blk * half:(blk + 1) * half] = (glu * _sigmoid(SWIGLU_ALPHA * glu) * (lin + 1.0)).astype(BF16)
        y = jnp.dot(hs[...], w2s[...], preferred_element_type=F32) + b2_ref[0, 0]
        y_ref[...] = y.astype(y_ref.dtype)

    @pl.when(jnp.logical_not(valid))
    def _():
        y_ref[...] = jnp.zeros_like(y_ref)


def _deinterleave_perm():
    half = MXU_WIDTH // 2
    src = np.arange(MXU_WIDTH)
    dst = np.where(src % 2 == 0, src // 2, half + src // 2)
    p = np.zeros((MXU_WIDTH, MXU_WIDTH), np.float32)
    p[src, dst] = 1.0
    return jnp.asarray(p, BF16)


def _moe_dispatch(top_i):
    n, k = top_i.shape
    tm = MOE_ROW_TILE
    p_rows = n * k + N_EXPERTS * tm
    experts = jnp.arange(N_EXPERTS, dtype=jnp.int32)
    onehot = top_i[:, :, None] == experts
    sel = jnp.sum(onehot.astype(jnp.int32), axis=1)
    before = jnp.cumsum(sel, axis=0) - sel
    counts = jnp.sum(sel, axis=0)
    padded = ((counts + tm - 1) // tm) * tm
    ends = jnp.cumsum(padded)
    starts = ends - padded
    pos = jnp.sum(jnp.where(onehot, (before + starts)[:, None, :], 0), axis=-1)
    n_tiles = p_rows // tm
    tile_start = jnp.arange(n_tiles, dtype=jnp.int32) * tm
    tile_valid = (tile_start < ends[-1]).astype(jnp.int32)
    last_tile = ends[-1] // tm - 1
    tile_expert = jnp.sum(ends[None, :] <= jnp.minimum(tile_start, last_tile * tm)[:, None], axis=1).astype(jnp.int32)
    keys = jnp.sort((top_i * n + jnp.arange(n, dtype=jnp.int32)[:, None]).reshape(-1))
    tile_onehot = tile_expert[:, None] == experts[None, :]
    lookup = lambda table: jnp.sum(jnp.where(tile_onehot, table[None, :], 0), axis=1)
    tile_rank0 = tile_start - lookup(starts)
    rank = tile_rank0[:, None] + jnp.arange(tm, dtype=jnp.int32)[None, :]
    sorted_at = jnp.clip(lookup(jnp.cumsum(counts) - counts)[:, None] + rank, 0, n * k - 1)
    tile_keys = keys[sorted_at.reshape(-1)].reshape(n_tiles, tm)
    filler = (tile_start[:, None] + jnp.arange(tm, dtype=jnp.int32)[None, :]) % n
    src_tok = jnp.where(rank < lookup(counts)[:, None], tile_keys % n, filler)
    nst = n_tiles // MOE_SPLITS
    idx = jnp.arange(nst, dtype=jnp.int32)
    tables = []
    for h in range(MOE_SPLITS):
        te_h = tile_expert[h * nst:(h + 1) * nst]
        is_first = jnp.concatenate([jnp.ones((1,), jnp.int32), (te_h[1:] != te_h[:-1]).astype(jnp.int32)])
        slot = (jnp.cumsum(is_first) - 1) % 2
        nxt = jnp.min(jnp.where(jnp.logical_and(idx[None, :] > idx[:, None], is_first[None, :] == 1),
                                idx[None, :], nst), axis=1)
        next_expert = jnp.sum(jnp.where(idx[None, :] == nxt[:, None], te_h[None, :] + 1, 0), axis=1) - 1
        tables.append((te_h, tile_valid[h * nst:(h + 1) * nst], is_first, slot.astype(jnp.int32),
                       next_expert.astype(jnp.int32)))
    return pos, src_tok.reshape(-1), tables, p_rows


def moe_experts(h2, top_i, layer, w1, b1, w2, b2):
    n = top_i.shape[0]
    d = h2.shape[1]
    depth, e, _, two_f = w1.shape
    f = two_f // 2
    tm = MOE_ROW_TILE
    pos, src_tok, tables, p_rows = _moe_dispatch(top_i)
    assert h2.shape[0] == p_rows
    src_tok = lax.optimization_barrier(src_tok)
    b1p = b1.reshape(depth, e, two_f // MXU_WIDTH, MXU_WIDTH // 2, 2).swapaxes(3, 4).reshape(depth, e, 1, two_f)
    b2r = b2.reshape(depth, e, 1, d)
    expert_vec = lambda w: pl.BlockSpec((1, 1, 1, w), lambda t, te, *_: (layer, te[t], 0, 0))
    rows = p_rows // MOE_SPLITS
    nst = rows // tm
    xs = [h2.at[src_tok[h * rows:(h + 1) * rows]].get(mode="promise_in_bounds") for h in range(MOE_SPLITS)]
    ys = None
    for h in range(MOE_SPLITS):
        in_specs = [
            pl.BlockSpec((tm, d), lambda t, *_: (t, 0)),
            pl.BlockSpec(memory_space=pl.ANY),
            expert_vec(two_f),
            pl.BlockSpec(memory_space=pl.ANY),
            expert_vec(d),
            pl.BlockSpec((MXU_WIDTH, MXU_WIDTH), lambda t, *_: (0, 0)),
        ]
        args = [*tables[h], xs[h], w1, b1p, w2, b2r, _deinterleave_perm()]
        aliases = {}
        if ys is not None:
            in_specs.append(pl.BlockSpec(memory_space=pl.ANY))
            aliases = {len(args): 0}
            args.append(ys)
        ys = pl.pallas_call(
            functools.partial(_moe_kernel, layer=layer),
            grid_spec=pltpu.PrefetchScalarGridSpec(
                num_scalar_prefetch=len(tables[h]), grid=(nst,), in_specs=in_specs,
                out_specs=pl.BlockSpec((tm, d), lambda t, *_, h=h: (h * nst + t, 0)),
                scratch_shapes=[pltpu.VMEM((2, d, two_f), F32), pltpu.VMEM((2, f, d), F32),
                                pltpu.SemaphoreType.DMA((2, 2)),
                                pltpu.VMEM((d, two_f), BF16), pltpu.VMEM((f, d), BF16), pltpu.VMEM((tm, f), BF16)]),
            out_shape=jax.ShapeDtypeStruct((p_rows, d), BF16),
            input_output_aliases=aliases,
            compiler_params=pltpu.CompilerParams(dimension_semantics=("arbitrary",),
                                                 vmem_limit_bytes=48 * 1024 * 1024),
            name="moe_experts",
        )(*args)
    return ys, pos


def _final_kernel(x1_ref, mod_ref, ys_ref, topw_ref, lng_ref, lnb_ref, o_ref):
    g2 = mod_ref[0, 5:6]
    moe = None
    for kth in range(TOP_K):
        term = ys_ref[kth].astype(F32) * topw_ref[:, kth:kth + 1]
        moe = term if moe is None else moe + term
    o_ref[...] = _layer_norm(ALPHA * x1_ref[...] + g2 * moe, lng_ref[...], lnb_ref[...])


def combine_and_norm(x1, mod, ys, pos, top_w, layer, sp):
    tm = ROW_TILE
    ln_g, ln_b = sp['ln_g'], sp['ln_b']
    outs = []
    for t0, n_rows in ((0, N_CTX), (CTX_TILES, N_TOK - N_CTX)):
        idx = lax.optimization_barrier(pos[t0 * tm:t0 * tm + n_rows].T.reshape(-1))
        rows = ys.at[idx].get(mode="promise_in_bounds").reshape(TOP_K, n_rows, D_MODEL)
        outs.append(pl.pallas_call(
            _final_kernel,
            grid=(n_rows // tm,),
            in_specs=[pl.BlockSpec((tm, D_MODEL), lambda t, t0=t0: (t0 + t, 0)),
                      pl.BlockSpec((1, 6, D_MODEL),
                                   lambda t, t0=t0: (layer * (1 + N_LAT_SEQ) + _mod_row(t0 + t), 0, 0)),
                      pl.BlockSpec((TOP_K, tm, D_MODEL), lambda t: (0, t, 0)),
                      pl.BlockSpec((tm, 128), lambda t, t0=t0: (t0 + t, 0)),
                      _layer_item(ln_g, 2 * layer + 1), _layer_item(ln_b, 2 * layer + 1)],
            out_specs=pl.BlockSpec((tm, D_MODEL), lambda t: (t, 0)),
            out_shape=jax.ShapeDtypeStruct((n_rows, D_MODEL), F32),
            compiler_params=pltpu.CompilerParams(dimension_semantics=("parallel",)),
            name="combine_and_norm",
        )(x1, mod, rows, top_w, ln_g, ln_b))
    return outs


def kernel(x_prompt, x_sample, cache_a_k, cache_a_v, cache_b_ckv, cache_b_kpe, state_c, state_d, c,
           c_ctx, w_mod, b_mod, w_in, a_sink, b_q_norm, b_w_uq, b_kv_norm, b_w_ukv, c_w_gate, c_b_gate,
           c_norm, d_mu, d_w0, d_w2, d_a0, d_a2, d_g2, d_k_k, d_k_a, d_r_k, d_ln_g, d_ln_b, w_br, w_out,
           ln_g, ln_b, w_router, b_router, w_mlp1, b_mlp1, w_mlp2, b_mlp2):
    sp = stacked_params(dict(
        a_sink=a_sink, b_q_norm=b_q_norm, b_w_uq=b_w_uq, b_kv_norm=b_kv_norm, b_w_ukv=b_w_ukv, c_w_gate=c_w_gate,
        c_b_gate=c_b_gate, c_norm=c_norm, d_mu=d_mu, d_w0=d_w0, d_w2=d_w2, d_a0=d_a0, d_a2=d_a2, d_g2=d_g2,
        d_k_k=d_k_k, d_k_a=d_k_a, d_r_k=d_r_k, d_ln_g=d_ln_g, d_ln_b=d_ln_b, ln_g=ln_g, ln_b=ln_b,
        w_router=w_router, b_router=b_router))
    assert x_prompt.shape == (N_CTX_SEQ, CTX_LEN, D_MODEL) and x_sample.shape == (N_LAT_SEQ, LAT_LEN, D_MODEL)
    x_ctx, x_lat = x_prompt.reshape(N_CTX, D_MODEL), x_sample.reshape(-1, D_MODEL)
    cond8 = jnp.concatenate([c_ctx[None], c, jnp.zeros((8 - 1 - N_LAT_SEQ, D_MODEL), F32)], axis=0)
    mod = modulation_table(cond8, w_mod, b_mod)[:, :1 + N_LAT_SEQ].reshape(DEPTH * (1 + N_LAT_SEQ), 6, D_MODEL)
    tables = _rope_tables()
    w_small, w_g = prepare_in_weights(w_in)
    w_br_bf, w_out_bf = w_br.astype(BF16), w_out.astype(BF16)
    new = {name: [] for name in ("a_k", "a_v", "b_ckv", "b_kpe", "c", "d")}
    for l in range(DEPTH):
        (aq, ak, av, bq, bckv, bkpe, cq4, ck4, cla_f, cla_b, cv, cgate,
         r, v, kk, lw_f, lw_b, k_f, k_b, a_f, a_b, bonus, dgate) = mixer_prelude(
             x_ctx, x_lat, mod, w_small, l, tables, sp)

        o_a = gqa_attention(aq, ak, av, sp['a_sink'], cache_a_k, cache_a_v, l)
        o_b = mla_attention(bq, bckv, bkpe, sp['b_w_ukv'], cache_b_ckv, cache_b_kpe, l)

        c_s0 = jnp.swapaxes(state_c[:, l], 3, 4).reshape(1, N_CHAIN, C_DV, C_DK)
        co_f, co_b, c_fin = _recurrence_calls(functools.partial(_gla_kernel, dot=_dot1), "gla", [cq4, ck4, cv],
                                              [cla_f], [cla_b], c_s0, (C_DV, C_DK), C_HEADS * C_DV)
        d_s0 = state_d[:, l].reshape(1, N_CHAIN, D_N, D_N)
        y_f, y_b, d_fin = _recurrence_calls(functools.partial(_rwkv_kernel, dot=_dot1), "rwkv7", [r, v, kk],
                                            [lw_f, k_f, a_f], [lw_b, k_b, a_b], d_s0, (D_N, D_N), D_HEADS * D_N)

        x1, h2, top_i, top_w = merge_and_route(x_ctx, x_lat, mod, o_a, o_b, co_f, co_b, cgate, y_f, y_b, bonus, dgate,
                                               w_g, w_br_bf, w_out_bf, l, sp)
        ys, pos = moe_experts(h2, top_i[:, :TOP_K], l, w_mlp1, b_mlp1, w_mlp2, b_mlp2)
        x_ctx, x_lat = combine_and_norm(x1, mod, ys, pos, top_w, l, sp)

        new["a_k"].append(ak[:N_CTX].reshape(N_CTX_SEQ, CTX_LEN, A_KV_HEADS, A_HD).transpose(0, 2, 1, 3))
        new["a_v"].append(av[:N_CTX].reshape(N_CTX_SEQ, CTX_LEN, A_KV_HEADS, A_HD).transpose(0, 2, 1, 3))
        new["b_ckv"].append(bckv[:N_CTX].reshape(N_CTX_SEQ, CTX_LEN, B_KV_LORA))
        new["b_kpe"].append(bkpe[:N_CTX].reshape(N_CTX_SEQ, CTX_LEN, B_ROPE))
        new["c"].append(jnp.swapaxes(c_fin.reshape(N_CTX_SEQ, 2, C_HEADS, C_DV, C_DK), 3, 4))
        new["d"].append(d_fin.reshape(N_CTX_SEQ, 2, D_HEADS, D_N, D_N))
    y_prompt = x_ctx.reshape(x_prompt.shape)
    y_sample = x_lat.reshape(x_sample.shape)
    return (y_prompt, y_sample, *(jnp.stack(new[name], axis=1) for name in ("a_k", "a_v", "b_ckv", "b_kpe", "c", "d")))
```

```python
import functools

import jax
import jax.numpy as jnp
import numpy as np
from jax import lax
from jax.experimental import pallas as pl
from jax.experimental.pallas import tpu as pltpu

F32 = jnp.float32
BF16 = jnp.bfloat16

MXU_WIDTH = 256
VMEM_LIMIT = 56 * 1024 * 1024

D_MODEL = 1024
DEPTH = 2
GRID_W = 64
ROPE_BASE = 10000.0
A_HEADS, A_KV_HEADS, A_HD = 4, 2, 64
B_HEADS, B_NOPE, B_ROPE, B_VD, B_Q_LORA, B_KV_LORA = 4, 64, 32, 64, 192, 128
C_HEADS, C_DK, C_DV, C_GATE_RANK, C_GATE_TEMP = 4, 32, 64, 16, 16.0
D_HEADS, D_N, D_DECAY_RANK, D_AAA_RANK, D_GATE_RANK, D_GN_EPS = 4, 64, 64, 64, 128, 64e-5
BRANCH_W = 256
N_BRANCH = 4
N_EXPERTS = 32
TOP_K = 4
SWIGLU_LIMIT = 7.0
SWIGLU_ALPHA = 1.702
ALPHA = (2 * DEPTH) ** 0.25
LN_EPS = 1e-5
RMS_EPS = 1e-6

N_CTX_SEQ, CTX_LEN = 16, 256
N_LAT_SEQ, LAT_LEN = 2, 2048
N_CTX = N_CTX_SEQ * CTX_LEN
N_TOK = N_CTX + N_LAT_SEQ * LAT_LEN
ROW_TILE = 256
N_TILES = N_TOK // ROW_TILE
CTX_TILES = N_CTX // ROW_TILE
LAT_TILES_PER_SEQ = LAT_LEN // ROW_TILE
N_SEQ = N_CTX_SEQ + N_LAT_SEQ
MLA_HEAD_LANES = 128
MERGE_ROW_TILE = 512
MOE_ROW_TILE = 512
MOE_SPLITS = 2
MOE_ROWS = N_TOK * TOP_K + N_EXPERTS * MOE_ROW_TILE

_ORIG = dict(aq=(0, 256), ak=(256, 384), av=(384, 512), bcq=(512, 704), bckv=(704, 832), bkpe=(832, 864),
             cq=(864, 992), ck=(992, 1120), cv=(1120, 1376), cog=(1376, 1632), caf=(1632, 1648), cab=(1648, 1664),
             zd=(1664, 2816))
_ORDER = ("aq", "ak", "av", "cq", "ck", "cv", "cog", "zd", "bcq", "caf", "cab", "bkpe", "bckv")
COL = {}
_off = 0
for _name in _ORDER:
    _w = _ORIG[_name][1] - _ORIG[_name][0]
    COL[_name] = (_off, _off + _w)
    _off += _w
SMALL_COLS = _off
G_START = 2816


def _cs(name):
    return slice(*COL[name])


def _split3(x):
    hi = x.astype(BF16)
    r1 = x - hi.astype(F32)
    mid = r1.astype(BF16)
    lo = (r1 - mid.astype(F32)).astype(BF16)
    return hi, mid, lo


def _split2(x):
    hi = x.astype(BF16)
    lo = (x - hi.astype(F32)).astype(BF16)
    return hi, lo


def _bdot(a, b, dims):
    return lax.dot_general(a, b, dims, preferred_element_type=F32)


_D2 = (((1,), (0,)), ((), ()))
_D2T = (((1,), (1,)), ((), ()))
_NN = (((2,), (1,)), ((0,), (0,)))
_NT = (((2,), (2,)), ((0,), (0,)))
_TN = (((1,), (1,)), ((0,), (0,)))


def _dot1(a, b, dims=_D2):
    return _bdot(a.astype(BF16), b.astype(BF16), dims)


def _dot3(a, b, dims=_D2):
    ah, al = _split2(a)
    bh, bl = _split2(b)
    return _bdot(ah, bh, dims) + (_bdot(ah, bl, dims) + _bdot(al, bh, dims))


def _dot_exact_lhs(a01, b, dims=_D2):
    a = a01.astype(BF16)
    h, l = _split2(b)
    return _bdot(a, h, dims) + _bdot(a, l, dims)


def _dot_exact_rhs(a, b01, dims=_D2):
    b = b01.astype(BF16)
    h, l = _split2(a)
    return _bdot(h, b, dims) + _bdot(l, b, dims)


def _sigmoid(x):
    return 0.5 * jnp.tanh(0.5 * x) + 0.5


def _softplus(x):
    return jnp.maximum(x, 0.0) + jnp.log(1.0 + jnp.exp(-jnp.abs(x)))


def _mod_row(t, tile=ROW_TILE):
    return jnp.where(t < N_CTX // tile, 0, 1 + (t - N_CTX // tile) // (LAT_LEN // tile))


def _full(shape):
    nd = len(shape)
    return pl.BlockSpec(shape, lambda *_: (0,) * nd)


def _ctx_rows_spec(width, tile=ROW_TILE):
    return pl.BlockSpec((tile, width), lambda t: (jnp.minimum(t, N_CTX // tile - 1), 0))


def _lat_rows_spec(width, tile=ROW_TILE):
    return pl.BlockSpec((tile, width), lambda t: (jnp.maximum(t - N_CTX // tile, 0), 0))


def _tile_rows(t, ctx_ref, lat_ref):
    return jnp.where(t < N_CTX // ctx_ref.shape[0], ctx_ref[...], lat_ref[...])


MOD_COL_TILE = 1536


def _mod_kernel(c_ref, w_ref, b_ref, o_ref):
    c = c_ref[...]
    o_ref[0] = _dot3(c * _sigmoid(c), w_ref[0]) + b_ref[0]


def modulation_table(cond8, w_mod, b_mod):
    depth, d, six_d = w_mod.shape
    return pl.pallas_call(
        _mod_kernel,
        grid=(depth, six_d // MOD_COL_TILE),
        in_specs=[pl.BlockSpec((8, d), lambda l, j: (0, 0)),
                  pl.BlockSpec((1, d, MOD_COL_TILE), lambda l, j: (l, 0, j)),
                  pl.BlockSpec((1, 1, MOD_COL_TILE), lambda l, j: (l, 0, j))],
        out_specs=pl.BlockSpec((1, 8, MOD_COL_TILE), lambda l, j: (l, 0, j)),
        out_shape=jax.ShapeDtypeStruct((depth, 8, six_d), F32),
        compiler_params=pltpu.CompilerParams(dimension_semantics=("parallel", "parallel")),
        name="modulation",
    )(cond8, w_mod, b_mod.reshape(depth, 1, six_d))


WPREP_ROWS = 128


def _wprep_kernel(w_ref, small_ref, gate_ref):
    for name in _ORDER:
        lo, hi = _ORIG[name]
        small_ref[0, :, _cs(name)] = w_ref[0, :, lo:hi].astype(BF16)
    gate_ref[0] = w_ref[0, :, G_START:].astype(BF16)


def prepare_in_weights(w_in):
    depth, d, cols = w_in.shape
    return pl.pallas_call(
        _wprep_kernel,
        grid=(depth, d // WPREP_ROWS),
        in_specs=[pl.BlockSpec((1, WPREP_ROWS, cols), lambda l, r: (l, r, 0))],
        out_specs=[pl.BlockSpec((1, WPREP_ROWS, SMALL_COLS), lambda l, r: (l, r, 0)),
                   pl.BlockSpec((1, WPREP_ROWS, cols - G_START), lambda l, r: (l, r, 0))],
        out_shape=[jax.ShapeDtypeStruct((depth, d, SMALL_COLS), BF16),
                   jax.ShapeDtypeStruct((depth, d, cols - G_START), BF16)],
        compiler_params=pltpu.CompilerParams(dimension_semantics=("parallel", "parallel")),
        name="prepare_in_weights",
    )(w_in)


def _rot_pairs(x, half, lane_mod_base=0):
    w = x.shape[-1]
    lane = lax.broadcasted_iota(jnp.int32, (1, w), 1) - lane_mod_base
    first = (lane % (2 * half)) < half
    return jnp.where(first, -pltpu.roll(x, w - half, axis=1), pltpu.roll(x, half, axis=1))


def _pre_kernel(xc_ref, xl_ref, xp_ref, xn_ref, mod_ref, w_ref, ca_ref, sa_ref, cb_ref, sb_ref, ck_ref, sk_ref,
                qnorm_ref, kvnorm_ref, wuq_ref, cwg_ref, cbg_ref, rep_ref, mu_ref, dw0_ref, dw2_ref, da0_ref,
                da2_ref, dg2_ref, dkk_ref, dka_ref, drk_ref, bd_ref,
                aq_o, ak_o, av_o, bq_o, bckv_o, bkpe_o, cq4_o, ck4_o, claf_o, clab_o, cv_o, cgate_o,
                r_o, v_o, kk_o, lwf_o, lwb_o, kf_o, kb_o, af_o, ab_o, bonus_o, dgate_o):
    t = pl.program_id(0)
    tm = xc_ref.shape[0]
    sh1 = mod_ref[0, 0:1, :]
    sc1 = mod_ref[0, 1:2, :]

    def modulate(xv):
        return (xv * (1.0 + sc1) + sh1).astype(BF16)

    x_tile = _tile_rows(t, xc_ref, xl_ref)
    h_all = jnp.concatenate([modulate(x_tile), modulate(xp_ref[...]), modulate(xn_ref[...])], axis=0)
    z_all = jnp.dot(h_all, w_ref[0], preferred_element_type=F32)
    z = z_all[:tm]

    aq = z[:, _cs("aq")]
    ak = z[:, _cs("ak")]
    aq_o[...] = aq * ca_ref[...] + _rot_pairs(aq, A_HD // 4) * sa_ref[...]
    ak_o[...] = ak * ca_ref[:, :ak.shape[1]] + _rot_pairs(ak, A_HD // 4) * sa_ref[:, :ak.shape[1]]
    av_o[...] = z[:, _cs("av")]

    bcq = z[:, _cs("bcq")]
    qn = bcq * lax.rsqrt(jnp.mean(bcq * bcq, axis=-1, keepdims=True) + RMS_EPS) * qnorm_ref[...]
    bq = _dot1(qn, wuq_ref[...])
    bq_o[...] = bq * cb_ref[...] + _rot_pairs(bq, B_ROPE // 4, lane_mod_base=B_NOPE) * sb_ref[...]
    bckv = z[:, _cs("bckv")]
    bckv_o[...] = bckv * lax.rsqrt(jnp.mean(bckv * bckv, axis=-1, keepdims=True) + RMS_EPS) * kvnorm_ref[...]
    kpe_lo = COL["bkpe"][0] // 128 * 128
    kblk = z[:, kpe_lo:kpe_lo + 128]
    kblk = kblk * ck_ref[...] + _rot_pairs(kblk, B_ROPE // 4) * sk_ref[...]
    bkpe_o[...] = kblk[:, COL["bkpe"][0] - kpe_lo:COL["bkpe"][1] - kpe_lo]

    rep = rep_ref[...]
    cq4_o[...] = _dot1(z[:, _cs("cq")] * (C_DK ** -0.5), rep)
    ck4_o[...] = _dot1(z[:, _cs("ck")], rep)
    cv_o[...] = z[:, _cs("cv")]
    cog = z[:, _cs("cog")]
    cgate_o[...] = cog * _sigmoid(cog)
    for direction, (name, out) in enumerate((("caf", claf_o), ("cab", clab_o))):
        pre = _dot1(z[:, _cs(name)], cwg_ref[direction]) + cbg_ref[direction]
        la_hi, la_lo = _split2(-_softplus(-pre) * (1.0 / C_GATE_TEMP))
        out[...] = _bdot(la_hi, rep, _D2) + _bdot(la_lo, rep, _D2)

    zd_cols = _cs("zd")
    zd = z[:, zd_cols]
    j = (t - CTX_TILES) % LAT_TILES_PER_SEQ
    latent = t >= CTX_TILES
    has_prev = jnp.logical_and(latent, j != 0)
    has_next = jnp.logical_and(latent, j != LAT_TILES_PER_SEQ - 1)
    prev_row = jnp.where(has_prev, z_all[tm + 7:tm + 8, zd_cols], 0.0)
    next_row = jnp.where(has_next, z_all[tm + 8:tm + 9, zd_cols], 0.0)
    row = lax.broadcasted_iota(jnp.int32, (tm, 1), 0)
    up = jnp.where(row == 0, prev_row, pltpu.roll(zd, 1, axis=0))
    dn = jnp.where(row == tm - 1, next_row, pltpu.roll(zd, tm - 1, axis=0))
    zd = zd + (0.5 * (up + dn) - zd) * mu_ref[...]

    hn = D_HEADS * D_N
    d_r, d_k, d_v = zd[:, :hn], zd[:, hn:2 * hn], zd[:, 2 * hn:3 * hn]
    o = 3 * hn
    d_w = (zd[:, o:o + D_DECAY_RANK], zd[:, o + D_DECAY_RANK:o + 2 * D_DECAY_RANK])
    o += 2 * D_DECAY_RANK
    d_a = (zd[:, o:o + D_AAA_RANK], zd[:, o + D_AAA_RANK:o + 2 * D_AAA_RANK])
    o += 2 * D_AAA_RANK
    d_g = zd[:, o:o + D_GATE_RANK]
    bd = bd_ref[...]
    kk = d_k * dkk_ref[...]
    kk = kk / jnp.maximum(jnp.sqrt(_dot_exact_rhs(kk * kk, bd)), 1e-12)
    r_o[...] = d_r
    v_o[...] = d_v
    kk_o[...] = kk
    k_sum = None
    for direction, (lw_o, k_o, a_o) in enumerate(((lwf_o, kf_o, af_o), (lwb_o, kb_o, ab_o))):
        w_log = -_softplus(-(dw0_ref[direction] + _dot1(jnp.tanh(d_w[direction]), dw2_ref[direction]))) - 0.5
        lw_o[...] = -jnp.exp(w_log)
        a = _sigmoid(da0_ref[direction] + _dot1(d_a[direction], da2_ref[direction]))
        k_dir = d_k * (1.0 + (a - 1.0) * dka_ref[...])
        k_o[...] = k_dir
        a_o[...] = a
        k_sum = k_dir if k_sum is None else k_sum + k_dir
    bonus_o[...] = d_v * _dot_exact_rhs(d_r * drk_ref[...] * k_sum, bd)
    dgate_o[...] = _dot1(_sigmoid(d_g), dg2_ref[...])


def _rope_tables():
    pos = np.arange(LAT_LEN)
    rowp, colp = (pos // GRID_W).astype(np.float32), (pos % GRID_W).astype(np.float32)

    f32 = np.float32

    def head_tables(rot_dim):
        quarter = rot_dim // 4
        inv = (f32(ROPE_BASE) ** (-np.arange(quarter, dtype=f32) / f32(quarter))).astype(f32)
        ar = (rowp[:, None] * inv).astype(f32)
        ac = (colp[:, None] * inv).astype(f32)
        cos = np.concatenate([np.cos(ar), np.cos(ar), np.cos(ac), np.cos(ac)], axis=-1).astype(f32)
        sin = np.concatenate([np.sin(ar), np.sin(ar), np.sin(ac), np.sin(ac)], axis=-1).astype(f32)
        return cos, sin

    def with_identity(c, s):
        w = c.shape[1]
        return (jnp.asarray(np.concatenate([np.ones((ROW_TILE, w), f32), c], axis=0)),
                jnp.asarray(np.concatenate([np.zeros((ROW_TILE, w), f32), s], axis=0)))

    ca, sa = head_tables(A_HD)
    ca, sa = with_identity(np.tile(ca, (1, A_HEADS)), np.tile(sa, (1, A_HEADS)))
    cbh, sbh = head_tables(B_ROPE)
    ones, zeros = np.ones((LAT_LEN, B_NOPE), f32), np.zeros((LAT_LEN, B_NOPE), f32)
    qpad = MLA_HEAD_LANES - B_NOPE - B_ROPE
    cb, sb = with_identity(
        np.tile(np.concatenate([ones, cbh, np.ones((LAT_LEN, qpad), f32)], axis=1), (1, B_HEADS)),
        np.tile(np.concatenate([zeros, sbh, np.zeros((LAT_LEN, qpad), f32)], axis=1), (1, B_HEADS)))
    pad = 128 - B_ROPE
    ck, sk = with_identity(np.concatenate([np.ones((LAT_LEN, pad), f32), cbh], axis=1),
                           np.concatenate([np.zeros((LAT_LEN, pad), f32), sbh], axis=1))
    return ca, sa, cb, sb, ck, sk


def _lane_repeat_matrix():
    m = np.zeros((C_HEADS * C_DK, C_HEADS * 128), np.float32)
    for h in range(C_HEADS):
        for g in range(128 // C_DK):
            for d in range(C_DK):
                m[h * C_DK + d, h * 128 + g * C_DK + d] = 1.0
    return jnp.asarray(m, BF16)


def _head_block_diag():
    m = np.kron(np.eye(D_HEADS, dtype=np.float32), np.ones((D_N, D_N), np.float32))
    return jnp.asarray(m, BF16)


PRE_OUT_WIDTHS = (256, 128, 128, B_HEADS * MLA_HEAD_LANES, 128, 32, 512, 512, 512, 512, 256, 256) + (256,) * 11


def _layer_block(arr, layer):
    nd = arr.ndim
    return pl.BlockSpec((1,) + arr.shape[1:], lambda *_: (layer,) + (0,) * (nd - 1))


def _layer_item(arr, index):
    nd = arr.ndim
    return pl.BlockSpec((None,) + arr.shape[1:], lambda *_: (index,) + (0,) * (nd - 1))


def _mod_spec(layer, tile=ROW_TILE):
    return pl.BlockSpec((1, 6, D_MODEL), lambda t: (layer * (1 + N_LAT_SEQ) + _mod_row(t, tile), 0, 0))


def stacked_params(p):
    depth = p['d_mu'].shape[0]
    hn = D_HEADS * D_N
    w_uq = p['b_w_uq'].reshape(depth, B_Q_LORA, B_HEADS, B_NOPE + B_ROPE)
    w_uq = jnp.pad(w_uq, ((0, 0), (0, 0), (0, 0), (0, MLA_HEAD_LANES - B_NOPE - B_ROPE)))
    row = lambda a: a.reshape(depth, 1, -1)
    return dict(
        b_q_norm=row(p['b_q_norm']), b_kv_norm=row(p['b_kv_norm']), w_uq=w_uq.reshape(depth, B_Q_LORA, -1),
        c_w_gate=p['c_w_gate'], c_b_gate=p['c_b_gate'].reshape(depth, 2, 1, -1), d_mu=row(p['d_mu']),
        d_w0=p['d_w0'].reshape(depth, 2, 1, hn), d_w2=p['d_w2'], d_a0=p['d_a0'].reshape(depth, 2, 1, hn),
        d_a2=p['d_a2'], d_g2=p['d_g2'], d_k_k=row(p['d_k_k']), d_k_a=row(p['d_k_a']), d_r_k=row(p['d_r_k']),
        c_norm=row(jnp.tile(p['c_norm'], (1, C_HEADS))), d_ln_g=row(p['d_ln_g']), d_ln_b=row(p['d_ln_b']),
        ln_g=p['ln_g'].reshape(depth * 2, 1, -1), ln_b=p['ln_b'].reshape(depth * 2, 1, -1),
        w_router=p['w_router'], b_router=row(p['b_router']), a_sink=p['a_sink'].reshape(-1), b_w_ukv=p['b_w_ukv'])


def mixer_prelude(x_ctx, x_lat, mod, w_small, layer, tables, sp):
    tm = ROW_TILE
    lat8 = lambda t: (t - CTX_TILES) * (tm // 8)
    last8 = x_lat.shape[0] // 8 - 1
    tab_idx = lambda t: (jnp.where(t < CTX_TILES, 0, 1 + (t - CTX_TILES) % LAT_TILES_PER_SEQ), 0)
    small = [sp[k] for k in ('b_q_norm', 'b_kv_norm', 'w_uq', 'c_w_gate', 'c_b_gate')] + [_lane_repeat_matrix()]
    small += [sp[k] for k in ('d_mu', 'd_w0', 'd_w2', 'd_a0', 'd_a2', 'd_g2', 'd_k_k', 'd_k_a', 'd_r_k')]
    small += [_head_block_diag()]
    const = lambda a: _full(a.shape) if a.dtype == BF16 else _layer_item(a, layer)
    in_specs = ([_ctx_rows_spec(D_MODEL), _lat_rows_spec(D_MODEL),
                 pl.BlockSpec((8, D_MODEL), lambda t: (jnp.clip(lat8(t) - 1, 0, last8), 0)),
                 pl.BlockSpec((8, D_MODEL), lambda t: (jnp.clip(lat8(t + 1), 0, last8), 0)),
                 _mod_spec(layer),
                 _layer_block(w_small, layer)]
                + [pl.BlockSpec((tm, tab.shape[1]), tab_idx) for tab in tables]
                + [const(a) for a in small])
    return pl.pallas_call(
        _pre_kernel,
        grid=(N_TILES,),
        in_specs=in_specs,
        out_specs=[pl.BlockSpec((tm, w), lambda t: (t, 0)) for w in PRE_OUT_WIDTHS],
        out_shape=[jax.ShapeDtypeStruct((N_TOK, w), F32) for w in PRE_OUT_WIDTHS],
        compiler_params=pltpu.CompilerParams(dimension_semantics=("parallel",), vmem_limit_bytes=VMEM_LIMIT),
        name="mixer_prelude",
    )(x_ctx, x_lat, x_lat, x_lat, mod, w_small, *tables, *small)


ATT_Q_BLOCK = 128
MLA_Q_BLOCK = 256
ATT_WINDOW = 128
ATT_NEG_INF = -1e30
CACHE_LEN = 512


def _softmax_pv(s, v, sink):
    dv = v.shape[1] // 2
    m = jnp.max(s, axis=-1, keepdims=True)
    if sink is not None:
        m = jnp.maximum(m, sink)
    e = jnp.exp((s - m).astype(BF16))
    o = jnp.dot(e, v, preferred_element_type=F32)
    den = o[:, dv:dv + 1]
    if sink is not None:
        den = den + jnp.exp(sink - m)
    return o[:, :dv] / den


def _with_ones(v):
    return jnp.concatenate([v.astype(BF16), jnp.ones(v.shape, BF16)], axis=1)


def _gqa_kernel(sink_ref, q_ref, k_ref, v_ref, *rest, hd, group, scale, windowed, sink_base):
    if windowed:
        kp_ref, kn_ref, vp_ref, vn_ref, kc_ref, vc_ref, _, o_ref = rest
    else:
        (o_ref,) = rest
    i = pl.program_id(1)
    tq = q_ref.shape[0]
    n_kv = k_ref.shape[1] // hd
    if windowed:
        qpos = i * tq + lax.broadcasted_iota(jnp.int32, (tq, 3 * tq), 0)
        kpos = (i - 1) * tq + lax.broadcasted_iota(jnp.int32, (tq, 3 * tq), 1)
        n_tok = pl.num_programs(1) * tq
        mask = (jnp.abs(qpos - kpos) <= ATT_WINDOW) & (kpos >= 0) & (kpos < n_tok)
        mask = jnp.concatenate([mask] * group, axis=0)
    for kvh in range(n_kv):
        ks = slice(kvh * hd, (kvh + 1) * hd)
        qs = [q_ref[:, (kvh * group + g) * hd:(kvh * group + g + 1) * hd] for g in range(group)]
        q = (jnp.concatenate(qs, axis=0) * scale).astype(BF16)
        sink = jnp.concatenate(
            [jnp.full((tq, 1), sink_ref[sink_base + kvh * group + g], F32) for g in range(group)], axis=0)
        if windowed:
            k_win = jnp.concatenate([kp_ref[:, ks], k_ref[:, ks], kn_ref[:, ks]], axis=0)
            v_win = jnp.concatenate([vp_ref[:, ks], v_ref[:, ks], vn_ref[:, ks]], axis=0)
            s_win = _bdot(q, k_win.astype(BF16), _D2T)
            s_win = jnp.where(mask, s_win, ATT_NEG_INF)
            s_ctx = _bdot(q, kc_ref[0, 0, kvh].astype(BF16), _D2T)
            s = jnp.concatenate([s_win, s_ctx], axis=1)
            v = jnp.concatenate([v_win, vc_ref[0, 0, kvh]], axis=0)
        else:
            s = _bdot(q, k_ref[:, ks].astype(BF16), _D2T)
            v = v_ref[:, ks]
        o = _softmax_pv(s, _with_ones(v), sink)
        for g in range(group):
            h = kvh * group + g
            o_ref[:, h * hd:(h + 1) * hd] = o[g * tq:(g + 1) * tq]


def gqa_attention(q, k, v, sink, cache_k, cache_v, layer):
    qw, kw = q.shape[1], k.shape[1]
    group = qw // kw
    scale = A_HD ** -0.5
    params = pltpu.CompilerParams(dimension_semantics=("parallel", "parallel"))
    out_shape = jax.ShapeDtypeStruct((N_TOK, qw), F32)
    ctx_spec = lambda w: pl.BlockSpec((CTX_LEN, w), lambda s, i, sk: (s, 0))
    o = pl.pallas_call(
        functools.partial(_gqa_kernel, hd=A_HD, group=group, scale=scale, windowed=False, sink_base=layer * A_HEADS),
        grid_spec=pltpu.PrefetchScalarGridSpec(
            num_scalar_prefetch=1, grid=(N_CTX_SEQ, 1), in_specs=[ctx_spec(qw), ctx_spec(kw), ctx_spec(kw)],
            out_specs=ctx_spec(qw)),
        out_shape=out_shape, compiler_params=params, name="gqa_full",
    )(sink, q, k, v)
    tq = ATT_Q_BLOCK
    nb = LAT_LEN // tq
    base = N_CTX // tq
    blk = lambda w, f: pl.BlockSpec((tq, w), lambda b, i, sk: (base + nb * b + f(i), 0))
    same = lambda i: i
    prev = lambda i: jnp.maximum(i - 1, 0)
    nxt = lambda i: jnp.minimum(i + 1, nb - 1)
    cspec = pl.BlockSpec((1, 1) + cache_k.shape[2:], lambda b, i, sk: (b, layer, 0, 0, 0))
    return pl.pallas_call(
        functools.partial(_gqa_kernel, hd=A_HD, group=group, scale=scale, windowed=True, sink_base=layer * A_HEADS),
        grid_spec=pltpu.PrefetchScalarGridSpec(
            num_scalar_prefetch=1, grid=(N_LAT_SEQ, nb),
            in_specs=[blk(qw, same), blk(kw, same), blk(kw, same), blk(kw, prev), blk(kw, nxt), blk(kw, prev),
                      blk(kw, nxt), cspec, cspec, pl.BlockSpec(memory_space=pl.ANY)],
            out_specs=blk(qw, same)),
        out_shape=out_shape, input_output_aliases={10: 0}, compiler_params=params, name="gqa_windowed",
    )(sink, q, k, v, k, k, v, v, cache_k, cache_v, o)


def _mla_kernel(q_ref, ckv_ref, kpe_ref, wukv_ref, *rest, n_heads, nope, rope, vd, scale, cached):
    if cached:
        cckv_ref, ckpe_ref, _, o_ref, k_scr, vext_scr = rest
    else:
        o_ref, k_scr, vext_scr = rest
    i = pl.program_id(1)
    n_cache = k_scr.shape[0] - ckv_ref.shape[0]
    hw = nope + vd
    hl = MLA_HEAD_LANES

    @pl.when(i == 0)
    def _():
        w = wukv_ref[...].astype(BF16)

        def expand(rows, kpe_rows, lo, hi):
            kv = jnp.dot(rows.astype(BF16), w, preferred_element_type=F32).astype(BF16)
            n = hi - lo
            kpe = kpe_rows.astype(BF16)
            for h in range(n_heads):
                k_scr[lo:hi, hl * h:hl * (h + 1)] = jnp.concatenate(
                    [kv[:, h * hw:h * hw + nope], kpe, jnp.zeros((n, hl - nope - rope), BF16)], axis=1)
                vext_scr[lo:hi, 2 * vd * h:2 * vd * (h + 1)] = jnp.concatenate(
                    [kv[:, h * hw + nope:(h + 1) * hw], jnp.ones((n, vd), BF16)], axis=1)

        if cached:
            expand(cckv_ref[0, 0], ckpe_ref[0, 0], 0, n_cache)
        expand(ckv_ref[...], kpe_ref[...], n_cache, k_scr.shape[0])

    for h in range(n_heads):
        qh = (q_ref[:, hl * h:hl * (h + 1)] * scale).astype(BF16)
        s = _bdot(qh, k_scr[:, hl * h:hl * (h + 1)], _D2T)
        o_ref[:, h * vd:(h + 1) * vd] = _softmax_pv(s, vext_scr[:, 2 * vd * h:2 * vd * (h + 1)], None)


def mla_attention(q, ckv, kpe, w_ukv, cache_ckv, cache_kpe, layer):
    qw = q.shape[1]
    tq = MLA_Q_BLOCK
    kw = dict(n_heads=B_HEADS, nope=B_NOPE, rope=B_ROPE, vd=B_VD, scale=(B_NOPE + B_ROPE) ** -0.5)
    params = pltpu.CompilerParams(dimension_semantics=("parallel", "arbitrary"))
    out_shape = jax.ShapeDtypeStruct((N_TOK, B_HEADS * B_VD), F32)
    scratch = lambda rows: [pltpu.VMEM((rows, B_HEADS * MLA_HEAD_LANES), BF16),
                            pltpu.VMEM((rows, 2 * B_HEADS * B_VD), BF16)]
    nbc = CTX_LEN // tq
    o = pl.pallas_call(
        functools.partial(_mla_kernel, cached=False, **kw),
        grid=(N_CTX_SEQ, nbc),
        in_specs=[pl.BlockSpec((tq, qw), lambda s, i: (s * nbc + i, 0)),
                  pl.BlockSpec((CTX_LEN, B_KV_LORA), lambda s, i: (s, 0)),
                  pl.BlockSpec((CTX_LEN, B_ROPE), lambda s, i: (s, 0)),
                  _layer_item(w_ukv, layer)],
        out_specs=pl.BlockSpec((tq, B_HEADS * B_VD), lambda s, i: (s * nbc + i, 0)),
        out_shape=out_shape,
        scratch_shapes=scratch(CTX_LEN),
        compiler_params=params, name="mla_context",
    )(q, ckv, kpe, w_ukv)
    nb = LAT_LEN // tq
    base = N_CTX // tq
    lat0 = N_CTX // LAT_LEN
    s_len = CACHE_LEN + LAT_LEN
    return pl.pallas_call(
        functools.partial(_mla_kernel, cached=True, **kw),
        grid=(N_LAT_SEQ, nb),
        in_specs=[pl.BlockSpec((tq, qw), lambda b, i: (base + nb * b + i, 0)),
                  pl.BlockSpec((LAT_LEN, B_KV_LORA), lambda b, i: (lat0 + b, 0)),
                  pl.BlockSpec((LAT_LEN, B_ROPE), lambda b, i: (lat0 + b, 0)),
                  _layer_item(w_ukv, layer),
                  pl.BlockSpec((1, 1, CACHE_LEN, B_KV_LORA), lambda b, i: (b, layer, 0, 0)),
                  pl.BlockSpec((1, 1, CACHE_LEN, B_ROPE), lambda b, i: (b, layer, 0, 0)),
                  pl.BlockSpec(memory_space=pl.ANY)],
        out_specs=pl.BlockSpec((tq, B_HEADS * B_VD), lambda b, i: (base + nb * b + i, 0)),
        out_shape=out_shape, input_output_aliases={6: 0},
        scratch_shapes=scratch(s_len),
        compiler_params=params, name="mla_latent",
    )(q, ckv, kpe, w_ukv, cache_ckv, cache_kpe, o)


CHUNK = 64
GLA_SUB = 16
PAIR = 2
N_CHAIN = PAIR * 2 * 4


def _is_back(shape):
    return (lax.broadcasted_iota(jnp.int32, shape, 0) // 4) % 2 == 1


def _chains(ref_f, ref_b, width):
    return jnp.stack([ref[0, s, 0, :, h * width:(h + 1) * width]
                      for s in range(PAIR) for ref in (ref_f, ref_b) for h in range(4)], axis=0)


def _unchain(y, o_f, o_b):
    for s in range(PAIR):
        o_f[0, s, 0] = jnp.concatenate([y[s * 8 + h] for h in range(4)], axis=-1)
        o_b[0, s, 0] = jnp.concatenate([y[s * 8 + 4 + h] for h in range(4)], axis=-1)


def _dir_masks(L):
    shape = (N_CHAIN, L, L)
    back = _is_back(shape)
    row = lax.broadcasted_iota(jnp.int32, shape, 1)
    col = lax.broadcasted_iota(jnp.int32, shape, 2)
    ahead = jnp.where(back, col - row, row - col)
    return ahead >= 0, ahead > 0, row == col


def _chunk_end(ci):
    L = ci.shape[1]
    return jnp.where(_is_back((N_CHAIN, 1, 1)), ci[:, 0:1], ci[:, L - 1:L])


def _split_refs(refs, n_in, has_s0, has_sfin):
    ins = refs[:n_in]
    pos = n_in
    s0_ref = None
    if has_s0:
        s0_ref = refs[pos]
        pos += 3
    of_ref, ob_ref = refs[pos], refs[pos + 1]
    pos += 2
    sfin_ref = refs[pos] if has_sfin else None
    return ins, s0_ref, of_ref, ob_ref, sfin_ref, refs[-1]


def _init_state(s_scr, s0_ref):
    @pl.when(pl.program_id(1) == 0)
    def _():
        if s0_ref is None:
            s_scr[...] = jnp.zeros_like(s_scr)
        else:
            s_scr[...] = s0_ref[0]


def _emit_state(sfin_ref, s_new):
    if sfin_ref is None:
        return

    @pl.when(pl.program_id(1) == pl.num_programs(1) - 1)
    def _():
        sfin_ref[0] = s_new


def _rwkv_kernel(*refs, dot, has_s0, has_sfin):
    (rf, rb, vf, vb, kkf, kkb, lwf, lwb, kf, kb, af, ab), s0_ref, yf_ref, yb_ref, sfin_ref, s_scr = _split_refs(
        refs, 12, has_s0, has_sfin)
    _init_state(s_scr, s0_ref)
    n = D_N
    r = _chains(rf, rb, n)
    v = _chains(vf, vb, n)
    kk = _chains(kkf, kkb, n)
    lw = _chains(lwf, lwb, n)
    k = _chains(kf, kb, n)
    a = _chains(af, ab, n)
    L = r.shape[1]
    S = s_scr[...]
    incl, strict, diag = _dir_masks(L)
    ci = _dot_exact_lhs(jnp.where(incl, 1.0, 0.0), lw, _NN)
    ce = ci - lw
    cl = _chunk_end(ci)
    e_neg = jnp.exp(-ci)
    b = a * kk
    alpha = kk * jnp.exp(ce)
    rho = r * jnp.exp(ci)
    beta = b * e_neg
    kappa = k * e_neg
    e_end = jnp.exp(cl - ci)
    ar = jnp.concatenate([alpha, rho], axis=1)
    bk = jnp.concatenate([beta, kappa], axis=1)
    w = dot(ar, bk, _NT)
    nmat = jnp.where(strict, w[:, :L, :L], 0.0)
    mmat = jnp.where(strict, w[:, :L, L:], 0.0)
    p1 = jnp.where(incl, w[:, L:, :L], 0.0)
    p2 = jnp.where(incl, w[:, L:, L:], 0.0)
    x = jnp.where(diag, 1.0, 0.0) - nmat
    p = dot(nmat, nmat, _NN)
    span = 2
    while True:
        x = x + dot(x, p, _NN)
        span *= 2
        if span >= L:
            break
        p = dot(p, p, _NN)
    us = dot(ar, S, _NT)
    rhs = us[:, :L] + dot(mmat, v, _NN)
    d = -dot(x, rhs, _NN)
    dv = jnp.concatenate([d, v], axis=1)
    pp = jnp.concatenate([p1, p2], axis=2)
    _unchain(us[:, L:] + dot(pp, dv, _NN), yf_ref, yb_ref)
    bk_end = jnp.concatenate([b * e_end, k * e_end], axis=1)
    s_new = S * jnp.exp(cl) + dot(dv, bk_end, _TN)
    s_scr[...] = s_new
    _emit_state(sfin_ref, s_new)


def _gla_kernel(*refs, dot, has_s0, has_sfin):
    (qf, qb, kf, kb, vf, vb, laf, lab), s0_ref, of_ref, ob_ref, sfin_ref, s_scr = _split_refs(
        refs, 8, has_s0, has_sfin)
    _init_state(s_scr, s0_ref)
    q4 = _chains(qf, qb, 128)
    k4 = _chains(kf, kb, 128)
    la4 = _chains(laf, lab, 128)
    v = _chains(vf, vb, C_DV)
    g, L, lanes = q4.shape
    dk = C_DK
    n_sub = L // GLA_SUB
    st = s_scr[...]
    incl, _, _ = _dir_masks(L)
    c = _dot_exact_lhs(jnp.where(incl, 1.0, 0.0), la4, _NN)
    shape = (g, L, lanes)
    back = _is_back(shape)
    lane_blk = lax.broadcasted_iota(jnp.int32, shape, 2) // dk
    row_blk = lax.broadcasted_iota(jnp.int32, shape, 1) // GLA_SUB
    cref_f = jnp.zeros(shape, F32)
    cref_b = jnp.zeros(shape, F32)
    for j in range(1, n_sub):
        cref_f = jnp.where(lane_blk == j, c[:, j * GLA_SUB - 1:j * GLA_SUB], cref_f)
        cref_b = jnp.where(lane_blk == j - 1, c[:, j * GLA_SUB:j * GLA_SUB + 1], cref_b)
    cref = jnp.where(back, cref_b, cref_f)
    q_on = row_blk == lane_blk
    k_on = jnp.where(back, row_blk - lane_blk, lane_blk - row_blk) >= 0
    qh = jnp.where(q_on, q4 * jnp.exp(jnp.where(q_on, c - cref, 0.0)), 0.0)
    kh = jnp.where(k_on, k4 * jnp.exp(jnp.where(k_on, cref - c, 0.0)), 0.0)
    att = jnp.where(incl, dot(qh, kh, _NT), 0.0)
    cl = _chunk_end(c)
    qe = (q4 * jnp.exp(c))[:, :, :dk]
    ke = (k4 * jnp.exp(cl - c))[:, :, :dk]
    _unchain(dot(qe, st, _NT) + dot(att, v, _NN), of_ref, ob_ref)
    s_new = st * jnp.exp(cl[:, :, :dk]) + dot(v, ke, _TN)
    s_scr[...] = s_new
    _emit_state(sfin_ref, s_new)


def _recurrence_calls(kernel_fn, name, pairs, singles_f, singles_b, s0_lat, state_dims, out_width):
    def run(view, grid, group, s0, prev_out):
        nc = view[2]
        fwd_map = lambda p, c: (group(p), 0, c, 0, 0)
        bwd_map = lambda p, c: (group(p), 0, nc - 1 - c, 0, 0)
        blk = lambda w: (1, PAIR, 1, CHUNK, w)
        args, in_specs = [], []
        for af, ab in [(a, a) for a in pairs] + list(zip(singles_f, singles_b)):
            w = af.shape[-1]
            args += [af.reshape(view + (w,)), ab.reshape(view + (w,))]
            in_specs += [pl.BlockSpec(blk(w), fwd_map), pl.BlockSpec(blk(w), bwd_map)]
        out_specs = [pl.BlockSpec(blk(out_width), fwd_map), pl.BlockSpec(blk(out_width), bwd_map)]
        out_shape = [jax.ShapeDtypeStruct(view + (out_width,), F32)] * 2
        aliases = {}
        if s0 is not None:
            args += [s0] + [o.reshape(view + (out_width,)) for o in prev_out]
            in_specs += [_full(s0.shape), pl.BlockSpec(memory_space=pl.ANY), pl.BlockSpec(memory_space=pl.ANY)]
            aliases = {len(args) - 2: 0, len(args) - 1: 1}
        else:
            out_specs.append(pl.BlockSpec((1, N_CHAIN) + state_dims, lambda p, c: (p, 0, 0, 0)))
            out_shape.append(jax.ShapeDtypeStruct((grid[0], N_CHAIN) + state_dims, F32))
        return pl.pallas_call(
            functools.partial(kernel_fn, has_s0=s0 is not None, has_sfin=s0 is None),
            grid=grid, in_specs=in_specs, out_specs=out_specs, out_shape=out_shape,
            input_output_aliases=aliases, scratch_shapes=[pltpu.VMEM((N_CHAIN,) + state_dims, F32)],
            compiler_params=pltpu.CompilerParams(dimension_semantics=("parallel", "arbitrary")),
            name=name + ("_latent" if s0 is not None else "_context"),
        )(*args)

    ctx_nc = CTX_LEN // CHUNK
    ctx_view = (N_TOK // (PAIR * CTX_LEN), PAIR, ctx_nc, CHUNK)
    o_f, o_b, s_fin = run(ctx_view, (N_CTX_SEQ // PAIR, ctx_nc), lambda p: p, None, None)
    lat_nc = LAT_LEN // CHUNK
    lat_view = (N_TOK // (PAIR * LAT_LEN), PAIR, lat_nc, CHUNK)
    o_f, o_b = run(lat_view, (1, lat_nc), lambda p: N_CTX // (PAIR * LAT_LEN), s0_lat, (o_f, o_b))
    return o_f.reshape(N_TOK, out_width), o_b.reshape(N_TOK, out_width), s_fin


def _layer_norm(x, g, b):
    mu = jnp.mean(x, axis=-1, keepdims=True)
    xc = x - mu
    var = jnp.mean(xc * xc, axis=-1, keepdims=True)
    return xc * lax.rsqrt(var + LN_EPS) * g + b


def _merge_kernel(xc_ref, xl_ref, mod_ref, oa_ref, ob_ref, cof_ref, cob_ref, cgate_ref, yf_ref, yb_ref, bonus_ref,
                  dgate_ref, wg_ref, wbr_ref, wout_ref, cnorm_ref, dlng_ref, dlnb_ref, lng_ref, lnb_ref, wr_ref, br_ref,
                  bd_ref, x1_o, h2_o, topi_o, topw_o):
    x = _tile_rows(pl.program_id(0), xc_ref, xl_ref)
    m = mod_ref[0]
    sh1, sc1, g1, sh2, sc2 = m[0:1], m[1:2], m[2:3], m[3:4], m[4:5]
    bd = bd_ref[...]
    inv_n = 1.0 / D_N
    co = cof_ref[...] + cob_ref[...]
    o_c = co * lax.rsqrt(_dot_exact_rhs(co * co, bd) * inv_n + RMS_EPS) * cnorm_ref[...] * cgate_ref[...]
    y = yf_ref[...] + yb_ref[...]
    yc = y - _dot_exact_rhs(y, bd) * inv_n
    var = _dot_exact_rhs(yc * yc, bd) * inv_n
    o_d = (yc * lax.rsqrt(var + D_GN_EPS) * dlng_ref[...] + dlnb_ref[...] + bonus_ref[...]) * dgate_ref[...]
    branches = [b.astype(BF16) for b in (oa_ref[...], ob_ref[...], o_c, o_d)]
    h = (x * (1.0 + sc1) + sh1).astype(BF16)
    blocks = []
    for cb in range(D_MODEL // MXU_WIDTH):
        merged = None
        for n in range(N_BRANCH):
            cols = slice(n * D_MODEL + cb * MXU_WIDTH, n * D_MODEL + (cb + 1) * MXU_WIDTH)
            gate = _sigmoid(jnp.dot(h, wg_ref[0, :, cols], preferred_element_type=F32))
            term = gate * jnp.dot(branches[n], wbr_ref[0, n, :, cb * MXU_WIDTH:(cb + 1) * MXU_WIDTH],
                                  preferred_element_type=F32)
            merged = term if merged is None else merged + term
        blocks.append(merged.astype(BF16))
    mix = jnp.dot(jnp.concatenate(blocks, axis=1), wout_ref[0], preferred_element_type=F32)
    x1 = _layer_norm(ALPHA * x + g1 * mix, lng_ref[...], lnb_ref[...])
    x1_o[...] = x1
    h2 = x1 * (1.0 + sc2) + sh2
    h2_o[...] = h2.astype(BF16)
    logits = _dot3(h2, wr_ref[...]) + br_ref[...]
    tm, n_e = logits.shape
    lane_e = lax.broadcasted_iota(jnp.int32, (tm, n_e), 1)
    lane_o = lax.broadcasted_iota(jnp.int32, (tm, topi_o.shape[1]), 1)
    top_i = jnp.zeros((tm, topi_o.shape[1]), jnp.int32)
    top_v = jnp.zeros((tm, topw_o.shape[1]), F32)
    vals = []
    for kth in range(TOP_K):
        mx = jnp.max(logits, axis=-1, keepdims=True)
        idx = jnp.min(jnp.where(logits == mx, lane_e, n_e), axis=-1, keepdims=True)
        vals.append(mx)
        top_i = jnp.where(lane_o == kth, idx, top_i)
        logits = jnp.where(lane_e == idx, -jnp.inf, logits)
    es = [jnp.exp(vk - vals[0]) for vk in vals]
    den = es[0] + es[1] + es[2] + es[3]
    for kth in range(TOP_K):
        top_v = jnp.where(lane_o == kth, es[kth] / den, top_v)
    topi_o[...] = top_i
    topw_o[...] = top_v


def merge_and_route(x_ctx, x_lat, mod, o_a, o_b, co_f, co_b, cgate, y_f, y_b, bonus, dgate, w_g, w_br, w_out, layer, sp):
    tm = MERGE_ROW_TILE
    hn = D_HEADS * D_N
    row = lambda w: pl.BlockSpec((tm, w), lambda t: (t, 0))
    small = [sp[k] for k in ('c_norm', 'd_ln_g', 'd_ln_b', 'ln_g', 'ln_b', 'w_router', 'b_router')]
    index = [layer, layer, layer, 2 * layer, 2 * layer, layer, layer]
    bd = _head_block_diag()
    return pl.pallas_call(
        _merge_kernel,
        grid=(N_TOK // tm,),
        in_specs=([_ctx_rows_spec(D_MODEL, tm), _lat_rows_spec(D_MODEL, tm), _mod_spec(layer, tm)]
                  + [row(hn)] * 9 + [_layer_block(w, layer) for w in (w_g, w_br, w_out)]
                  + [_layer_item(a, i) for a, i in zip(small, index)] + [_full(bd.shape)]),
        out_specs=[row(D_MODEL), row(D_MODEL), row(128), row(128)],
        out_shape=[jax.ShapeDtypeStruct((N_TOK, D_MODEL), F32), jax.ShapeDtypeStruct((MOE_ROWS, D_MODEL), BF16),
                   jax.ShapeDtypeStruct((N_TOK, 128), jnp.int32), jax.ShapeDtypeStruct((N_TOK, 128), F32)],
        compiler_params=pltpu.CompilerParams(dimension_semantics=("parallel",), vmem_limit_bytes=VMEM_LIMIT),
        name="merge_and_route",
    )(x_ctx, x_lat, mod, o_a, o_b, co_f, co_b, cgate, y_f, y_b, bonus, dgate, w_g, w_br, w_out, *small, bd)


def _moe_kernel(te_ref, tv_ref, first_ref, slot_ref, next_ref, x_ref, w1_hbm, b1_ref, w2_hbm, b2_ref, perm_ref, *rest,
                layer):
    y_ref, w1buf, w2buf, sem, w1s, w2s, hs = rest[-7:]
    t = pl.program_id(0)
    valid = tv_ref[t] != 0
    d_model, two_f = w1s.shape
    n_blk = two_f // MXU_WIDTH
    half = MXU_WIDTH // 2

    def fetch(expert, slot):
        return (pltpu.make_async_copy(w1_hbm.at[layer, expert], w1buf.at[slot], sem.at[0, slot]),
                pltpu.make_async_copy(w2_hbm.at[layer, expert], w2buf.at[slot], sem.at[1, slot]))

    @pl.when(t == 0)
    def _():
        for cp in fetch(te_ref[0], 0):
            cp.start()

    @pl.when(first_ref[t] == 1)
    def _():
        slot = slot_ref[t]
        for cp in fetch(te_ref[t], slot):
            cp.wait()

        @pl.when(next_ref[t] >= 0)
        def _():
            for cp in fetch(next_ref[t], 1 - slot):
                cp.start()

        for blk in range(n_blk):
            sl = slice(blk * MXU_WIDTH, (blk + 1) * MXU_WIDTH)
            wb = w1buf[slot, :, sl].astype(BF16)
            w1s[:, sl] = jnp.dot(wb, perm_ref[...], preferred_element_type=F32).astype(BF16)
        w2s[...] = w2buf[slot].astype(BF16)

    @pl.when(valid)
    def _():
        x = x_ref[...]
        for blk in range(n_blk):
            sl = slice(blk * MXU_WIDTH, (blk + 1) * MXU_WIDTH)
            u = jnp.dot(x, w1s[:, sl], preferred_element_type=F32) + b1_ref[0, 0, :, sl]
            glu = jnp.minimum(u[:, :half], SWIGLU_LIMIT)
            lin = jnp.clip(u[:, half:], -SWIGLU_LIMIT, SWIGLU_LIMIT)
            hs[:, blk * half:(blk + 1) * half] = (glu * _sigmoid(SWIGLU_ALPHA * glu) * (lin + 1.0)).astype(BF16)
        y = jnp.dot(hs[...], w2s[...], preferred_element_type=F32) + b2_ref[0, 0]
        y_ref[...] = y.astype(y_ref.dtype)

    @pl.when(jnp.logical_not(valid))
    def _():
        y_ref[...] = jnp.zeros_like(y_ref)


def _deinterleave_perm():
    half = MXU_WIDTH // 2
    src = np.arange(MXU_WIDTH)
    dst = np.where(src % 2 == 0, src // 2, half + src // 2)
    p = np.zeros((MXU_WIDTH, MXU_WIDTH), np.float32)
    p[src, dst] = 1.0
    return jnp.asarray(p, BF16)


def _moe_dispatch(top_i):
    n, k = top_i.shape
    tm = MOE_ROW_TILE
    p_rows = n * k + N_EXPERTS * tm
    experts = jnp.arange(N_EXPERTS, dtype=jnp.int32)
    onehot = top_i[:, :, None] == experts
    sel = jnp.sum(onehot.astype(jnp.int32), axis=1)
    before = jnp.cumsum(sel, axis=0) - sel
    counts = jnp.sum(sel, axis=0)
    padded = ((counts + tm - 1) // tm) * tm
    ends = jnp.cumsum(padded)
    starts = ends - padded
    pos = jnp.sum(jnp.where(onehot, (before + starts)[:, None, :], 0), axis=-1)
    n_tiles = p_rows // tm
    tile_start = jnp.arange(n_tiles, dtype=jnp.int32) * tm
    tile_valid = (tile_start < ends[-1]).astype(jnp.int32)
    last_tile = ends[-1] // tm - 1
    tile_expert = jnp.sum(ends[None, :] <= jnp.minimum(tile_start, last_tile * tm)[:, None], axis=1).astype(jnp.int32)
    keys = jnp.sort((top_i * n + jnp.arange(n, dtype=jnp.int32)[:, None]).reshape(-1))
    tile_onehot = tile_expert[:, None] == experts[None, :]
    lookup = lambda table: jnp.sum(jnp.where(tile_onehot, table[None, :], 0), axis=1)
    tile_rank0 = tile_start - lookup(starts)
    rank = tile_rank0[:, None] + jnp.arange(tm, dtype=jnp.int32)[None, :]
    sorted_at = jnp.clip(lookup(jnp.cumsum(counts) - counts)[:, None] + rank, 0, n * k - 1)
    tile_keys = keys[sorted_at.reshape(-1)].reshape(n_tiles, tm)
    filler = (tile_start[:, None] + jnp.arange(tm, dtype=jnp.int32)[None, :]) % n
    src_tok = jnp.where(rank < lookup(counts)[:, None], tile_keys % n, filler)
    nst = n_tiles // MOE_SPLITS
    idx = jnp.arange(nst, dtype=jnp.int32)
    tables = []
    for h in range(MOE_SPLITS):
        te_h = tile_expert[h * nst:(h + 1) * nst]
        is_first = jnp.concatenate([jnp.ones((1,), jnp.int32), (te_h[1:] != te_h[:-1]).astype(jnp.int32)])
        slot = (jnp.cumsum(is_first) - 1) % 2
        nxt = jnp.min(jnp.where(jnp.logical_and(idx[None, :] > idx[:, None], is_first[None, :] == 1),
                                idx[None, :], nst), axis=1)
        next_expert = jnp.sum(jnp.where(idx[None, :] == nxt[:, None], te_h[None, :] + 1, 0), axis=1) - 1
        tables.append((te_h, tile_valid[h * nst:(h + 1) * nst], is_first, slot.astype(jnp.int32),
                       next_expert.astype(jnp.int32)))
    return pos, src_tok.reshape(-1), tables, p_rows


def moe_experts(h2, top_i, layer, w1, b1, w2, b2):
    n = top_i.shape[0]
    d = h2.shape[1]
    depth, e, _, two_f = w1.shape
    f = two_f // 2
    tm = MOE_ROW_TILE
    pos, src_tok, tables, p_rows = _moe_dispatch(top_i)
    assert h2.shape[0] == p_rows
    src_tok = lax.optimization_barrier(src_tok)
    b1p = b1.reshape(depth, e, two_f // MXU_WIDTH, MXU_WIDTH // 2, 2).swapaxes(3, 4).reshape(depth, e, 1, two_f)
    b2r = b2.reshape(depth, e, 1, d)
    expert_vec = lambda w: pl.BlockSpec((1, 1, 1, w), lambda t, te, *_: (layer, te[t], 0, 0))
    rows = p_rows // MOE_SPLITS
    nst = rows // tm
    xs = [h2.at[src_tok[h * rows:(h + 1) * rows]].get(mode="promise_in_bounds") for h in range(MOE_SPLITS)]
    ys = None
    for h in range(MOE_SPLITS):
        in_specs = [
            pl.BlockSpec((tm, d), lambda t, *_: (t, 0)),
            pl.BlockSpec(memory_space=pl.ANY),
            expert_vec(two_f),
            pl.BlockSpec(memory_space=pl.ANY),
            expert_vec(d),
            pl.BlockSpec((MXU_WIDTH, MXU_WIDTH), lambda t, *_: (0, 0)),
        ]
        args = [*tables[h], xs[h], w1, b1p, w2, b2r, _deinterleave_perm()]
        aliases = {}
        if ys is not None:
            in_specs.append(pl.BlockSpec(memory_space=pl.ANY))
            aliases = {len(args): 0}
            args.append(ys)
        ys = pl.pallas_call(
            functools.partial(_moe_kernel, layer=layer),
            grid_spec=pltpu.PrefetchScalarGridSpec(
                num_scalar_prefetch=len(tables[h]), grid=(nst,), in_specs=in_specs,
                out_specs=pl.BlockSpec((tm, d), lambda t, *_, h=h: (h * nst + t, 0)),
                scratch_shapes=[pltpu.VMEM((2, d, two_f), F32), pltpu.VMEM((2, f, d), F32),
                                pltpu.SemaphoreType.DMA((2, 2)),
                                pltpu.VMEM((d, two_f), BF16), pltpu.VMEM((f, d), BF16), pltpu.VMEM((tm, f), BF16)]),
            out_shape=jax.ShapeDtypeStruct((p_rows, d), BF16),
            input_output_aliases=aliases,
            compiler_params=pltpu.CompilerParams(dimension_semantics=("arbitrary",),
                                                 vmem_limit_bytes=48 * 1024 * 1024),
            name="moe_experts",
        )(*args)
    return ys, pos


def _final_kernel(x1_ref, mod_ref, ys_ref, topw_ref, lng_ref, lnb_ref, o_ref):
    g2 = mod_ref[0, 5:6]
    moe = None
    for kth in range(TOP_K):
        term = ys_ref[kth].astype(F32) * topw_ref[:, kth:kth + 1]
        moe = term if moe is None else moe + term
    o_ref[...] = _layer_norm(ALPHA * x1_ref[...] + g2 * moe, lng_ref[...], lnb_ref[...])


def combine_and_norm(x1, mod, ys, pos, top_w, layer, sp):
    tm = ROW_TILE
    ln_g, ln_b = sp['ln_g'], sp['ln_b']
    outs = []
    for t0, n_rows in ((0, N_CTX), (CTX_TILES, N_TOK - N_CTX)):
        idx = lax.optimization_barrier(pos[t0 * tm:t0 * tm + n_rows].T.reshape(-1))
        rows = ys.at[idx].get(mode="promise_in_bounds").reshape(TOP_K, n_rows, D_MODEL)
        outs.append(pl.pallas_call(
            _final_kernel,
            grid=(n_rows // tm,),
            in_specs=[pl.BlockSpec((tm, D_MODEL), lambda t, t0=t0: (t0 + t, 0)),
                      pl.BlockSpec((1, 6, D_MODEL),
                                   lambda t, t0=t0: (layer * (1 + N_LAT_SEQ) + _mod_row(t0 + t), 0, 0)),
                      pl.BlockSpec((TOP_K, tm, D_MODEL), lambda t: (0, t, 0)),
                      pl.BlockSpec((tm, 128), lambda t, t0=t0: (t0 + t, 0)),
                      _layer_item(ln_g, 2 * layer + 1), _layer_item(ln_b, 2 * layer + 1)],
            out_specs=pl.BlockSpec((tm, D_MODEL), lambda t: (t, 0)),
            out_shape=jax.ShapeDtypeStruct((n_rows, D_MODEL), F32),
            compiler_params=pltpu.CompilerParams(dimension_semantics=("parallel",)),
            name="combine_and_norm",
        )(x1, mod, rows, top_w, ln_g, ln_b))
    return outs


def kernel(x_prompt, x_sample, cache_a_k, cache_a_v, cache_b_ckv, cache_b_kpe, state_c, state_d, c,
           c_ctx, w_mod, b_mod, w_in, a_sink, b_q_norm, b_w_uq, b_kv_norm, b_w_ukv, c_w_gate, c_b_gate,
           c_norm, d_mu, d_w0, d_w2, d_a0, d_a2, d_g2, d_k_k, d_k_a, d_r_k, d_ln_g, d_ln_b, w_br, w_out,
           ln_g, ln_b, w_router, b_router, w_mlp1, b_mlp1, w_mlp2, b_mlp2):
    sp = stacked_params(dict(
        a_sink=a_sink, b_q_norm=b_q_norm, b_w_uq=b_w_uq, b_kv_norm=b_kv_norm, b_w_ukv=b_w_ukv, c_w_gate=c_w_gate,
        c_b_gate=c_b_gate, c_norm=c_norm, d_mu=d_mu, d_w0=d_w0, d_w2=d_w2, d_a0=d_a0, d_a2=d_a2, d_g2=d_g2,
        d_k_k=d_k_k, d_k_a=d_k_a, d_r_k=d_r_k, d_ln_g=d_ln_g, d_ln_b=d_ln_b, ln_g=ln_g, ln_b=ln_b,
        w_router=w_router, b_router=b_router))
    assert x_prompt.shape == (N_CTX_SEQ, CTX_LEN, D_MODEL) and x_sample.shape == (N_LAT_SEQ, LAT_LEN, D_MODEL)
    x_ctx, x_lat = x_prompt.reshape(N_CTX, D_MODEL), x_sample.reshape(-1, D_MODEL)
    cond8 = jnp.concatenate([c_ctx[None], c, jnp.zeros((8 - 1 - N_LAT_SEQ, D_MODEL), F32)], axis=0)
    mod = modulation_table(cond8, w_mod, b_mod)[:, :1 + N_LAT_SEQ].reshape(DEPTH * (1 + N_LAT_SEQ), 6, D_MODEL)
    tables = _rope_tables()
    w_small, w_g = prepare_in_weights(w_in)
    w_br_bf, w_out_bf = w_br.astype(BF16), w_out.astype(BF16)
    new = {name: [] for name in ("a_k", "a_v", "b_ckv", "b_kpe", "c", "d")}
    for l in range(DEPTH):
        (aq, ak, av, bq, bckv, bkpe, cq4, ck4, cla_f, cla_b, cv, cgate,
         r, v, kk, lw_f, lw_b, k_f, k_b, a_f, a_b, bonus, dgate) = mixer_prelude(
             x_ctx, x_lat, mod, w_small, l, tables, sp)

        o_a = gqa_attention(aq, ak, av, sp['a_sink'], cache_a_k, cache_a_v, l)
        o_b = mla_attention(bq, bckv, bkpe, sp['b_w_ukv'], cache_b_ckv, cache_b_kpe, l)

        c_s0 = jnp.swapaxes(state_c[:, l], 3, 4).reshape(1, N_CHAIN, C_DV, C_DK)
        co_f, co_b, c_fin = _recurrence_calls(functools.partial(_gla_kernel, dot=_dot1), "gla", [cq4, ck4, cv],
                                              [cla_f], [cla_b], c_s0, (C_DV, C_DK), C_HEADS * C_DV)
        d_s0 = state_d[:, l].reshape(1, N_CHAIN, D_N, D_N)
        y_f, y_b, d_fin = _recurrence_calls(functools.partial(_rwkv_kernel, dot=_dot1), "rwkv7", [r, v, kk],
                                            [lw_f, k_f, a_f], [lw_b, k_b, a_b], d_s0, (D_N, D_N), D_HEADS * D_N)

        x1, h2, top_i, top_w = merge_and_route(x_ctx, x_lat, mod, o_a, o_b, co_f, co_b, cgate, y_f, y_b, bonus, dgate,
                                               w_g, w_br_bf, w_out_bf, l, sp)
        ys, pos = moe_experts(h2, top_i[:, :TOP_K], l, w_mlp1, b_mlp1, w_mlp2, b_mlp2)
        x_ctx, x_lat = combine_and_norm(x1, mod, ys, pos, top_w, l, sp)

        new["a_k"].append(ak[:N_CTX].reshape(N_CTX_SEQ, CTX_LEN, A_KV_HEADS, A_HD).transpose(0, 2, 1, 3))
        new["a_v"].append(av[:N_CTX].reshape(N_CTX_SEQ, CTX_LEN, A_KV_HEADS, A_HD).transpose(0, 2, 1, 3))
        new["b_ckv"].append(bckv[:N_CTX].reshape(N_CTX_SEQ, CTX_LEN, B_KV_LORA))
        new["b_kpe"].append(bkpe[:N_CTX].reshape(N_CTX_SEQ, CTX_LEN, B_ROPE))
        new["c"].append(jnp.swapaxes(c_fin.reshape(N_CTX_SEQ, 2, C_HEADS, C_DV, C_DK), 3, 4))
        new["d"].append(d_fin.reshape(N_CTX_SEQ, 2, D_HEADS, D_N, D_N))
    y_prompt = x_ctx.reshape(x_prompt.shape)
    y_sample = x_lat.reshape(x_sample.shape)
    return (y_prompt, y_sample, *(jnp.stack(new[name], axis=1) for name in ("a_k", "a_v", "b_ckv", "b_kpe", "c", "d")))
```

```python
import functools

import jax
import jax.numpy as jnp
import numpy as np
from jax import lax
from jax.experimental import pallas as pl
from jax.experimental.pallas import tpu as pltpu

F32 = jnp.float32
BF16 = jnp.bfloat16

MXU_WIDTH = 256
VMEM_LIMIT = 56 * 1024 * 1024

D_MODEL = 1024
DEPTH = 2
GRID_W = 64
ROPE_BASE = 10000.0
A_HEADS, A_KV_HEADS, A_HD = 4, 2, 64
B_HEADS, B_NOPE, B_ROPE, B_VD, B_Q_LORA, B_KV_LORA = 4, 64, 32, 64, 192, 128
C_HEADS, C_DK, C_DV, C_GATE_RANK, C_GATE_TEMP = 4, 32, 64, 16, 16.0
D_HEADS, D_N, D_DECAY_RANK, D_AAA_RANK, D_GATE_RANK, D_GN_EPS = 4, 64, 64, 64, 128, 64e-5
BRANCH_W = 256
N_BRANCH = 4
N_EXPERTS = 32
TOP_K = 4
SWIGLU_LIMIT = 7.0
SWIGLU_ALPHA = 1.702
ALPHA = (2 * DEPTH) ** 0.25
LN_EPS = 1e-5
RMS_EPS = 1e-6

N_CTX_SEQ, CTX_LEN = 16, 256
N_LAT_SEQ, LAT_LEN = 2, 2048
N_CTX = N_CTX_SEQ * CTX_LEN
N_TOK = N_CTX + N_LAT_SEQ * LAT_LEN
ROW_TILE = 256
N_TILES = N_TOK // ROW_TILE
CTX_TILES = N_CTX // ROW_TILE
LAT_TILES_PER_SEQ = LAT_LEN // ROW_TILE
N_SEQ = N_CTX_SEQ + N_LAT_SEQ
MLA_HEAD_LANES = 128
MERGE_ROW_TILE = 512
MOE_ROW_TILE = 256
MOE_SPLITS = 2
MOE_ROWS = N_TOK * TOP_K + N_EXPERTS * MOE_ROW_TILE

_ORIG = dict(aq=(0, 256), ak=(256, 384), av=(384, 512), bcq=(512, 704), bckv=(704, 832), bkpe=(832, 864),
             cq=(864, 992), ck=(992, 1120), cv=(1120, 1376), cog=(1376, 1632), caf=(1632, 1648), cab=(1648, 1664),
             zd=(1664, 2816))
_ORDER = ("aq", "ak", "av", "cq", "ck", "cv", "cog", "zd", "bcq", "caf", "cab", "bkpe", "bckv")
COL = {}
_off = 0
for _name in _ORDER:
    _w = _ORIG[_name][1] - _ORIG[_name][0]
    COL[_name] = (_off, _off + _w)
    _off += _w
SMALL_COLS = _off
G_START = 2816


def _cs(name):
    return slice(*COL[name])


def _split3(x):
    hi = x.astype(BF16)
    r1 = x - hi.astype(F32)
    mid = r1.astype(BF16)
    lo = (r1 - mid.astype(F32)).astype(BF16)
    return hi, mid, lo


def _split2(x):
    hi = x.astype(BF16)
    lo = (x - hi.astype(F32)).astype(BF16)
    return hi, lo


def _bdot(a, b, dims):
    return lax.dot_general(a, b, dims, preferred_element_type=F32)


_D2 = (((1,), (0,)), ((), ()))
_D2T = (((1,), (1,)), ((), ()))
_NN = (((2,), (1,)), ((0,), (0,)))
_NT = (((2,), (2,)), ((0,), (0,)))
_TN = (((1,), (1,)), ((0,), (0,)))


def _dot1(a, b, dims=_D2):
    return _bdot(a.astype(BF16), b.astype(BF16), dims)


def _dot3(a, b, dims=_D2):
    ah, al = _split2(a)
    bh, bl = _split2(b)
    return _bdot(ah, bh, dims) + (_bdot(ah, bl, dims) + _bdot(al, bh, dims))


def _dot_exact_lhs(a01, b, dims=_D2):
    a = a01.astype(BF16)
    h, l = _split2(b)
    return _bdot(a, h, dims) + _bdot(a, l, dims)


def _dot_exact_rhs(a, b01, dims=_D2):
    b = b01.astype(BF16)
    h, l = _split2(a)
    return _bdot(h, b, dims) + _bdot(l, b, dims)


def _sigmoid(x):
    return 0.5 * jnp.tanh(0.5 * x) + 0.5


def _softplus(x):
    return jnp.maximum(x, 0.0) + jnp.log(1.0 + jnp.exp(-jnp.abs(x)))


def _mod_row(t, tile=ROW_TILE):
    return jnp.where(t < N_CTX // tile, 0, 1 + (t - N_CTX // tile) // (LAT_LEN // tile))


def _full(shape):
    nd = len(shape)
    return pl.BlockSpec(shape, lambda *_: (0,) * nd)


def _ctx_rows_spec(width, tile=ROW_TILE):
    return pl.BlockSpec((tile, width), lambda t: (jnp.minimum(t, N_CTX // tile - 1), 0))


def _lat_rows_spec(width, tile=ROW_TILE):
    return pl.BlockSpec((tile, width), lambda t: (jnp.maximum(t - N_CTX // tile, 0), 0))


def _tile_rows(t, ctx_ref, lat_ref):
    return jnp.where(t < N_CTX // ctx_ref.shape[0], ctx_ref[...], lat_ref[...])


MOD_COL_TILE = 1536


def _mod_kernel(c_ref, w_ref, b_ref, o_ref):
    c = c_ref[...]
    o_ref[0] = _dot3(c * _sigmoid(c), w_ref[0]) + b_ref[0]


def modulation_table(cond8, w_mod, b_mod):
    depth, d, six_d = w_mod.shape
    return pl.pallas_call(
        _mod_kernel,
        grid=(depth, six_d // MOD_COL_TILE),
        in_specs=[pl.BlockSpec((8, d), lambda l, j: (0, 0)),
                  pl.BlockSpec((1, d, MOD_COL_TILE), lambda l, j: (l, 0, j)),
                  pl.BlockSpec((1, 1, MOD_COL_TILE), lambda l, j: (l, 0, j))],
        out_specs=pl.BlockSpec((1, 8, MOD_COL_TILE), lambda l, j: (l, 0, j)),
        out_shape=jax.ShapeDtypeStruct((depth, 8, six_d), F32),
        compiler_params=pltpu.CompilerParams(dimension_semantics=("parallel", "parallel")),
        name="modulation",
    )(cond8, w_mod, b_mod.reshape(depth, 1, six_d))


WPREP_ROWS = 128


def _wprep_kernel(w_ref, small_ref, gate_ref):
    for name in _ORDER:
        lo, hi = _ORIG[name]
        small_ref[0, :, _cs(name)] = w_ref[0, :, lo:hi].astype(BF16)
    gate_ref[0] = w_ref[0, :, G_START:].astype(BF16)


def prepare_in_weights(w_in):
    depth, d, cols = w_in.shape
    return pl.pallas_call(
        _wprep_kernel,
        grid=(depth, d // WPREP_ROWS),
        in_specs=[pl.BlockSpec((1, WPREP_ROWS, cols), lambda l, r: (l, r, 0))],
        out_specs=[pl.BlockSpec((1, WPREP_ROWS, SMALL_COLS), lambda l, r: (l, r, 0)),
                   pl.BlockSpec((1, WPREP_ROWS, cols - G_START), lambda l, r: (l, r, 0))],
        out_shape=[jax.ShapeDtypeStruct((depth, d, SMALL_COLS), BF16),
                   jax.ShapeDtypeStruct((depth, d, cols - G_START), BF16)],
        compiler_params=pltpu.CompilerParams(dimension_semantics=("parallel", "parallel")),
        name="prepare_in_weights",
    )(w_in)


def _rot_pairs(x, half, lane_mod_base=0):
    w = x.shape[-1]
    lane = lax.broadcasted_iota(jnp.int32, (1, w), 1) - lane_mod_base
    first = (lane % (2 * half)) < half
    return jnp.where(first, -pltpu.roll(x, w - half, axis=1), pltpu.roll(x, half, axis=1))


def _pre_kernel(xc_ref, xl_ref, xp_ref, xn_ref, mod_ref, w_ref, ca_ref, sa_ref, cb_ref, sb_ref, ck_ref, sk_ref,
                qnorm_ref, kvnorm_ref, wuq_ref, cwg_ref, cbg_ref, rep_ref, mu_ref, dw0_ref, dw2_ref, da0_ref,
                da2_ref, dg2_ref, dkk_ref, dka_ref, drk_ref, bd_ref,
                aq_o, ak_o, av_o, bq_o, bckv_o, bkpe_o, cq4_o, ck4_o, claf_o, clab_o, cv_o, cgate_o,
                r_o, v_o, kk_o, lwf_o, lwb_o, kf_o, kb_o, af_o, ab_o, bonus_o, dgate_o):
    t = pl.program_id(0)
    tm = xc_ref.shape[0]
    sh1 = mod_ref[0, 0:1, :]
    sc1 = mod_ref[0, 1:2, :]

    def modulate(xv):
        return (xv * (1.0 + sc1) + sh1).astype(BF16)

    x_tile = _tile_rows(t, xc_ref, xl_ref)
    h_all = jnp.concatenate([modulate(x_tile), modulate(xp_ref[...]), modulate(xn_ref[...])], axis=0)
    z_all = jnp.dot(h_all, w_ref[0], preferred_element_type=F32)
    z = z_all[:tm]

    aq = z[:, _cs("aq")]
    ak = z[:, _cs("ak")]
    aq_o[...] = aq * ca_ref[...] + _rot_pairs(aq, A_HD // 4) * sa_ref[...]
    ak_o[...] = ak * ca_ref[:, :ak.shape[1]] + _rot_pairs(ak, A_HD // 4) * sa_ref[:, :ak.shape[1]]
    av_o[...] = z[:, _cs("av")]

    bcq = z[:, _cs("bcq")]
    qn = bcq * lax.rsqrt(jnp.mean(bcq * bcq, axis=-1, keepdims=True) + RMS_EPS) * qnorm_ref[...]
    bq = _dot1(qn, wuq_ref[...])
    bq_o[...] = bq * cb_ref[...] + _rot_pairs(bq, B_ROPE // 4, lane_mod_base=B_NOPE) * sb_ref[...]
    bckv = z[:, _cs("bckv")]
    bckv_o[...] = bckv * lax.rsqrt(jnp.mean(bckv * bckv, axis=-1, keepdims=True) + RMS_EPS) * kvnorm_ref[...]
    kpe_lo = COL["bkpe"][0] // 128 * 128
    kblk = z[:, kpe_lo:kpe_lo + 128]
    kblk = kblk * ck_ref[...] + _rot_pairs(kblk, B_ROPE // 4) * sk_ref[...]
    bkpe_o[...] = kblk[:, COL["bkpe"][0] - kpe_lo:COL["bkpe"][1] - kpe_lo]

    rep = rep_ref[...]
    cq4_o[...] = _dot1(z[:, _cs("cq")] * (C_DK ** -0.5), rep)
    ck4_o[...] = _dot1(z[:, _cs("ck")], rep)
    cv_o[...] = z[:, _cs("cv")]
    cog = z[:, _cs("cog")]
    cgate_o[...] = cog * _sigmoid(cog)
    for direction, (name, out) in enumerate((("caf", claf_o), ("cab", clab_o))):
        pre = _dot1(z[:, _cs(name)], cwg_ref[direction]) + cbg_ref[direction]
        la_hi, la_lo = _split2(-_softplus(-pre) * (1.0 / C_GATE_TEMP))
        out[...] = _bdot(la_hi, rep, _D2) + _bdot(la_lo, rep, _D2)

    zd_cols = _cs("zd")
    zd = z[:, zd_cols]
    j = (t - CTX_TILES) % LAT_TILES_PER_SEQ
    latent = t >= CTX_TILES
    has_prev = jnp.logical_and(latent, j != 0)
    has_next = jnp.logical_and(latent, j != LAT_TILES_PER_SEQ - 1)
    prev_row = jnp.where(has_prev, z_all[tm + 7:tm + 8, zd_cols], 0.0)
    next_row = jnp.where(has_next, z_all[tm + 8:tm + 9, zd_cols], 0.0)
    row = lax.broadcasted_iota(jnp.int32, (tm, 1), 0)
    up = jnp.where(row == 0, prev_row, pltpu.roll(zd, 1, axis=0))
    dn = jnp.where(row == tm - 1, next_row, pltpu.roll(zd, tm - 1, axis=0))
    zd = zd + (0.5 * (up + dn) - zd) * mu_ref[...]

    hn = D_HEADS * D_N
    d_r, d_k, d_v = zd[:, :hn], zd[:, hn:2 * hn], zd[:, 2 * hn:3 * hn]
    o = 3 * hn
    d_w = (zd[:, o:o + D_DECAY_RANK], zd[:, o + D_DECAY_RANK:o + 2 * D_DECAY_RANK])
    o += 2 * D_DECAY_RANK
    d_a = (zd[:, o:o + D_AAA_RANK], zd[:, o + D_AAA_RANK:o + 2 * D_AAA_RANK])
    o += 2 * D_AAA_RANK
    d_g = zd[:, o:o + D_GATE_RANK]
    bd = bd_ref[...]
    kk = d_k * dkk_ref[...]
    kk = kk / jnp.maximum(jnp.sqrt(_dot_exact_rhs(kk * kk, bd)), 1e-12)
    r_o[...] = d_r
    v_o[...] = d_v
    kk_o[...] = kk
    k_sum = None
    for direction, (lw_o, k_o, a_o) in enumerate(((lwf_o, kf_o, af_o), (lwb_o, kb_o, ab_o))):
        w_log = -_softplus(-(dw0_ref[direction] + _dot1(jnp.tanh(d_w[direction]), dw2_ref[direction]))) - 0.5
        lw_o[...] = -jnp.exp(w_log)
        a = _sigmoid(da0_ref[direction] + _dot1(d_a[direction], da2_ref[direction]))
        k_dir = d_k * (1.0 + (a - 1.0) * dka_ref[...])
        k_o[...] = k_dir
        a_o[...] = a
        k_sum = k_dir if k_sum is None else k_sum + k_dir
    bonus_o[...] = d_v * _dot_exact_rhs(d_r * drk_ref[...] * k_sum, bd)
    dgate_o[...] = _dot1(_sigmoid(d_g), dg2_ref[...])


def _rope_tables():
    pos = np.arange(LAT_LEN)
    rowp, colp = (pos // GRID_W).astype(np.float32), (pos % GRID_W).astype(np.float32)

    f32 = np.float32

    def head_tables(rot_dim):
        quarter = rot_dim // 4
        inv = (f32(ROPE_BASE) ** (-np.arange(quarter, dtype=f32) / f32(quarter))).astype(f32)
        ar = (rowp[:, None] * inv).astype(f32)
        ac = (colp[:, None] * inv).astype(f32)
        cos = np.concatenate([np.cos(ar), np.cos(ar), np.cos(ac), np.cos(ac)], axis=-1).astype(f32)
        sin = np.concatenate([np.sin(ar), np.sin(ar), np.sin(ac), np.sin(ac)], axis=-1).astype(f32)
        return cos, sin

    def with_identity(c, s):
        w = c.shape[1]
        return (jnp.asarray(np.concatenate([np.ones((ROW_TILE, w), f32), c], axis=0)),
                jnp.asarray(np.concatenate([np.zeros((ROW_TILE, w), f32), s], axis=0)))

    ca, sa = head_tables(A_HD)
    ca, sa = with_identity(np.tile(ca, (1, A_HEADS)), np.tile(sa, (1, A_HEADS)))
    cbh, sbh = head_tables(B_ROPE)
    ones, zeros = np.ones((LAT_LEN, B_NOPE), f32), np.zeros((LAT_LEN, B_NOPE), f32)
    qpad = MLA_HEAD_LANES - B_NOPE - B_ROPE
    cb, sb = with_identity(
        np.tile(np.concatenate([ones, cbh, np.ones((LAT_LEN, qpad), f32)], axis=1), (1, B_HEADS)),
        np.tile(np.concatenate([zeros, sbh, np.zeros((LAT_LEN, qpad), f32)], axis=1), (1, B_HEADS)))
    pad = 128 - B_ROPE
    ck, sk = with_identity(np.concatenate([np.ones((LAT_LEN, pad), f32), cbh], axis=1),
                           np.concatenate([np.zeros((LAT_LEN, pad), f32), sbh], axis=1))
    return ca, sa, cb, sb, ck, sk


def _lane_repeat_matrix():
    m = np.zeros((C_HEADS * C_DK, C_HEADS * 128), np.float32)
    for h in range(C_HEADS):
        for g in range(128 // C_DK):
            for d in range(C_DK):
                m[h * C_DK + d, h * 128 + g * C_DK + d] = 1.0
    return jnp.asarray(m, BF16)


def _head_block_diag():
    m = np.kron(np.eye(D_HEADS, dtype=np.float32), np.ones((D_N, D_N), np.float32))
    return jnp.asarray(m, BF16)


PRE_OUT_WIDTHS = (256, 128, 128, B_HEADS * MLA_HEAD_LANES, 128, 32, 512, 512, 512, 512, 256, 256) + (256,) * 11


def _layer_block(arr, layer):
    nd = arr.ndim
    return pl.BlockSpec((1,) + arr.shape[1:], lambda *_: (layer,) + (0,) * (nd - 1))


def _layer_item(arr, index):
    nd = arr.ndim
    return pl.BlockSpec((None,) + arr.shape[1:], lambda *_: (index,) + (0,) * (nd - 1))


def _mod_spec(layer, tile=ROW_TILE):
    return pl.BlockSpec((1, 6, D_MODEL), lambda t: (layer * (1 + N_LAT_SEQ) + _mod_row(t, tile), 0, 0))


def stacked_params(p):
    depth = p['d_mu'].shape[0]
    hn = D_HEADS * D_N
    w_uq = p['b_w_uq'].reshape(depth, B_Q_LORA, B_HEADS, B_NOPE + B_ROPE)
    w_uq = jnp.pad(w_uq, ((0, 0), (0, 0), (0, 0), (0, MLA_HEAD_LANES - B_NOPE - B_ROPE)))
    row = lambda a: a.reshape(depth, 1, -1)
    return dict(
        b_q_norm=row(p['b_q_norm']), b_kv_norm=row(p['b_kv_norm']), w_uq=w_uq.reshape(depth, B_Q_LORA, -1),
        c_w_gate=p['c_w_gate'], c_b_gate=p['c_b_gate'].reshape(depth, 2, 1, -1), d_mu=row(p['d_mu']),
        d_w0=p['d_w0'].reshape(depth, 2, 1, hn), d_w2=p['d_w2'], d_a0=p['d_a0'].reshape(depth, 2, 1, hn),
        d_a2=p['d_a2'], d_g2=p['d_g2'], d_k_k=row(p['d_k_k']), d_k_a=row(p['d_k_a']), d_r_k=row(p['d_r_k']),
        c_norm=row(jnp.tile(p['c_norm'], (1, C_HEADS))), d_ln_g=row(p['d_ln_g']), d_ln_b=row(p['d_ln_b']),
        ln_g=p['ln_g'].reshape(depth * 2, 1, -1), ln_b=p['ln_b'].reshape(depth * 2, 1, -1),
        w_router=p['w_router'], b_router=row(p['b_router']), a_sink=p['a_sink'].reshape(-1), b_w_ukv=p['b_w_ukv'])


def mixer_prelude(x_ctx, x_lat, mod, w_small, layer, tables, sp):
    tm = ROW_TILE
    lat8 = lambda t: (t - CTX_TILES) * (tm // 8)
    last8 = x_lat.shape[0] // 8 - 1
    tab_idx = lambda t: (jnp.where(t < CTX_TILES, 0, 1 + (t - CTX_TILES) % LAT_TILES_PER_SEQ), 0)
    small = [sp[k] for k in ('b_q_norm', 'b_kv_norm', 'w_uq', 'c_w_gate', 'c_b_gate')] + [_lane_repeat_matrix()]
    small += [sp[k] for k in ('d_mu', 'd_w0', 'd_w2', 'd_a0', 'd_a2', 'd_g2', 'd_k_k', 'd_k_a', 'd_r_k')]
    small += [_head_block_diag()]
    const = lambda a: _full(a.shape) if a.dtype == BF16 else _layer_item(a, layer)
    in_specs = ([_ctx_rows_spec(D_MODEL), _lat_rows_spec(D_MODEL),
                 pl.BlockSpec((8, D_MODEL), lambda t: (jnp.clip(lat8(t) - 1, 0, last8), 0)),
                 pl.BlockSpec((8, D_MODEL), lambda t: (jnp.clip(lat8(t + 1), 0, last8), 0)),
                 _mod_spec(layer),
                 _layer_block(w_small, layer)]
                + [pl.BlockSpec((tm, tab.shape[1]), tab_idx) for tab in tables]
                + [const(a) for a in small])
    return pl.pallas_call(
        _pre_kernel,
        grid=(N_TILES,),
        in_specs=in_specs,
        out_specs=[pl.BlockSpec((tm, w), lambda t: (t, 0)) for w in PRE_OUT_WIDTHS],
        out_shape=[jax.ShapeDtypeStruct((N_TOK, w), F32) for w in PRE_OUT_WIDTHS],
        compiler_params=pltpu.CompilerParams(dimension_semantics=("parallel",), vmem_limit_bytes=VMEM_LIMIT),
        name="mixer_prelude",
    )(x_ctx, x_lat, x_lat, x_lat, mod, w_small, *tables, *small)


ATT_Q_BLOCK = 128
MLA_Q_BLOCK = 256
MLA_LATENT_Q_BLOCK = 512
ATT_WINDOW = 128
ATT_NEG_INF = -1e30
CACHE_LEN = 512


def _softmax_pv(s, v, sink):
    dv = v.shape[1] // 2
    m = jnp.max(s, axis=-1, keepdims=True)
    if sink is not None:
        m = jnp.maximum(m, sink)
    e = jnp.exp((s - m).astype(BF16))
    o = jnp.dot(e, v, preferred_element_type=F32)
    den = o[:, dv:dv + 1]
    if sink is not None:
        den = den + jnp.exp(sink - m)
    return o[:, :dv] / den


def _with_ones(v):
    return jnp.concatenate([v.astype(BF16), jnp.ones(v.shape, BF16)], axis=1)


def _gqa_kernel(sink_ref, q_ref, k_ref, v_ref, *rest, hd, group, scale, windowed, sink_base):
    if windowed:
        kp_ref, kn_ref, vp_ref, vn_ref, kc_ref, vc_ref, _, o_ref = rest
    else:
        (o_ref,) = rest
    i = pl.program_id(1)
    tq = q_ref.shape[0]
    n_kv = k_ref.shape[1] // hd
    if windowed:
        qpos = i * tq + lax.broadcasted_iota(jnp.int32, (tq, 3 * tq), 0)
        kpos = (i - 1) * tq + lax.broadcasted_iota(jnp.int32, (tq, 3 * tq), 1)
        n_tok = pl.num_programs(1) * tq
        mask = (jnp.abs(qpos - kpos) <= ATT_WINDOW) & (kpos >= 0) & (kpos < n_tok)
        mask = jnp.concatenate([mask] * group, axis=0)
    for kvh in range(n_kv):
        ks = slice(kvh * hd, (kvh + 1) * hd)
        qs = [q_ref[:, (kvh * group + g) * hd:(kvh * group + g + 1) * hd] for g in range(group)]
        q = (jnp.concatenate(qs, axis=0) * scale).astype(BF16)
        sink = jnp.concatenate(
            [jnp.full((tq, 1), sink_ref[sink_base + kvh * group + g], F32) for g in range(group)], axis=0)
        if windowed:
            k_win = jnp.concatenate([kp_ref[:, ks], k_ref[:, ks], kn_ref[:, ks]], axis=0)
            v_win = jnp.concatenate([vp_ref[:, ks], v_ref[:, ks], vn_ref[:, ks]], axis=0)
            s_win = _bdot(q, k_win.astype(BF16), _D2T)
            s_win = jnp.where(mask, s_win, ATT_NEG_INF)
            s_ctx = _bdot(q, kc_ref[0, 0, kvh].astype(BF16), _D2T)
            s = jnp.concatenate([s_win, s_ctx], axis=1)
            v = jnp.concatenate([v_win, vc_ref[0, 0, kvh]], axis=0)
        else:
            s = _bdot(q, k_ref[:, ks].astype(BF16), _D2T)
            v = v_ref[:, ks]
        o = _softmax_pv(s, _with_ones(v), sink)
        for g in range(group):
            h = kvh * group + g
            o_ref[:, h * hd:(h + 1) * hd] = o[g * tq:(g + 1) * tq]


def gqa_attention(q, k, v, sink, cache_k, cache_v, layer):
    qw, kw = q.shape[1], k.shape[1]
    group = qw // kw
    scale = A_HD ** -0.5
    params = pltpu.CompilerParams(dimension_semantics=("parallel", "parallel"))
    out_shape = jax.ShapeDtypeStruct((N_TOK, qw), F32)
    ctx_spec = lambda w: pl.BlockSpec((CTX_LEN, w), lambda s, i, sk: (s, 0))
    o = pl.pallas_call(
        functools.partial(_gqa_kernel, hd=A_HD, group=group, scale=scale, windowed=False, sink_base=layer * A_HEADS),
        grid_spec=pltpu.PrefetchScalarGridSpec(
            num_scalar_prefetch=1, grid=(N_CTX_SEQ, 1), in_specs=[ctx_spec(qw), ctx_spec(kw), ctx_spec(kw)],
            out_specs=ctx_spec(qw)),
        out_shape=out_shape, compiler_params=params, name="gqa_full",
    )(sink, q, k, v)
    tq = ATT_Q_BLOCK
    nb = LAT_LEN // tq
    base = N_CTX // tq
    blk = lambda w, f: pl.BlockSpec((tq, w), lambda b, i, sk: (base + nb * b + f(i), 0))
    same = lambda i: i
    prev = lambda i: jnp.maximum(i - 1, 0)
    nxt = lambda i: jnp.minimum(i + 1, nb - 1)
    cspec = pl.BlockSpec((1, 1) + cache_k.shape[2:], lambda b, i, sk: (b, layer, 0, 0, 0))
    return pl.pallas_call(
        functools.partial(_gqa_kernel, hd=A_HD, group=group, scale=scale, windowed=True, sink_base=layer * A_HEADS),
        grid_spec=pltpu.PrefetchScalarGridSpec(
            num_scalar_prefetch=1, grid=(N_LAT_SEQ, nb),
            in_specs=[blk(qw, same), blk(kw, same), blk(kw, same), blk(kw, prev), blk(kw, nxt), blk(kw, prev),
                      blk(kw, nxt), cspec, cspec, pl.BlockSpec(memory_space=pl.ANY)],
            out_specs=blk(qw, same)),
        out_shape=out_shape, input_output_aliases={10: 0}, compiler_params=params, name="gqa_windowed",
    )(sink, q, k, v, k, k, v, v, cache_k, cache_v, o)


def _mla_kernel(q_ref, ckv_ref, kpe_ref, wukv_ref, *rest, n_heads, nope, rope, vd, scale, cached):
    if cached:
        cckv_ref, ckpe_ref, _, o_ref, k_scr, vext_scr = rest
    else:
        o_ref, k_scr, vext_scr = rest
    i = pl.program_id(1)
    n_cache = k_scr.shape[0] - ckv_ref.shape[0]
    hw = nope + vd
    hl = MLA_HEAD_LANES

    @pl.when(i == 0)
    def _():
        w = wukv_ref[...].astype(BF16)

        def expand(rows, kpe_rows, lo, hi):
            kv = jnp.dot(rows.astype(BF16), w, preferred_element_type=F32).astype(BF16)
            n = hi - lo
            kpe = kpe_rows.astype(BF16)
            for h in range(n_heads):
                k_scr[lo:hi, hl * h:hl * (h + 1)] = jnp.concatenate(
                    [kv[:, h * hw:h * hw + nope], kpe, jnp.zeros((n, hl - nope - rope), BF16)], axis=1)
                vext_scr[lo:hi, 2 * vd * h:2 * vd * (h + 1)] = jnp.concatenate(
                    [kv[:, h * hw + nope:(h + 1) * hw], jnp.ones((n, vd), BF16)], axis=1)

        if cached:
            expand(cckv_ref[0, 0], ckpe_ref[0, 0], 0, n_cache)
        expand(ckv_ref[...], kpe_ref[...], n_cache, k_scr.shape[0])

    for h in range(n_heads):
        qh = (q_ref[:, hl * h:hl * (h + 1)] * scale).astype(BF16)
        s = _bdot(qh, k_scr[:, hl * h:hl * (h + 1)], _D2T)
        o_ref[:, h * vd:(h + 1) * vd] = _softmax_pv(s, vext_scr[:, 2 * vd * h:2 * vd * (h + 1)], None)


def mla_attention(q, ckv, kpe, w_ukv, cache_ckv, cache_kpe, layer):
    qw = q.shape[1]
    tq = MLA_Q_BLOCK
    kw = dict(n_heads=B_HEADS, nope=B_NOPE, rope=B_ROPE, vd=B_VD, scale=(B_NOPE + B_ROPE) ** -0.5)
    params = pltpu.CompilerParams(dimension_semantics=("parallel", "arbitrary"))
    out_shape = jax.ShapeDtypeStruct((N_TOK, B_HEADS * B_VD), F32)
    scratch = lambda rows: [pltpu.VMEM((rows, B_HEADS * MLA_HEAD_LANES), BF16),
                            pltpu.VMEM((rows, 2 * B_HEADS * B_VD), BF16)]
    nbc = CTX_LEN // tq
    o = pl.pallas_call(
        functools.partial(_mla_kernel, cached=False, **kw),
        grid=(N_CTX_SEQ, nbc),
        in_specs=[pl.BlockSpec((tq, qw), lambda s, i: (s * nbc + i, 0)),
                  pl.BlockSpec((CTX_LEN, B_KV_LORA), lambda s, i: (s, 0)),
                  pl.BlockSpec((CTX_LEN, B_ROPE), lambda s, i: (s, 0)),
                  _layer_item(w_ukv, layer)],
        out_specs=pl.BlockSpec((tq, B_HEADS * B_VD), lambda s, i: (s * nbc + i, 0)),
        out_shape=out_shape,
        scratch_shapes=scratch(CTX_LEN),
        compiler_params=params, name="mla_context",
    )(q, ckv, kpe, w_ukv)
    tq = MLA_LATENT_Q_BLOCK
    nb = LAT_LEN // tq
    base = N_CTX // tq
    lat0 = N_CTX // LAT_LEN
    s_len = CACHE_LEN + LAT_LEN
    return pl.pallas_call(
        functools.partial(_mla_kernel, cached=True, **kw),
        grid=(N_LAT_SEQ, nb),
        in_specs=[pl.BlockSpec((tq, qw), lambda b, i: (base + nb * b + i, 0)),
                  pl.BlockSpec((LAT_LEN, B_KV_LORA), lambda b, i: (lat0 + b, 0)),
                  pl.BlockSpec((LAT_LEN, B_ROPE), lambda b, i: (lat0 + b, 0)),
                  _layer_item(w_ukv, layer),
                  pl.BlockSpec((1, 1, CACHE_LEN, B_KV_LORA), lambda b, i: (b, layer, 0, 0)),
                  pl.BlockSpec((1, 1, CACHE_LEN, B_ROPE), lambda b, i: (b, layer, 0, 0)),
                  pl.BlockSpec(memory_space=pl.ANY)],
        out_specs=pl.BlockSpec((tq, B_HEADS * B_VD), lambda b, i: (base + nb * b + i, 0)),
        out_shape=out_shape, input_output_aliases={6: 0},
        scratch_shapes=scratch(s_len),
        compiler_params=params, name="mla_latent",
    )(q, ckv, kpe, w_ukv, cache_ckv, cache_kpe, o)


CHUNK = 64
GLA_SUB = 16
PAIR = 2
N_CHAIN = PAIR * 2 * 4


def _is_back(shape):
    return (lax.broadcasted_iota(jnp.int32, shape, 0) // 4) % 2 == 1


def _chains(ref_f, ref_b, width):
    return jnp.stack([ref[0, s, 0, :, h * width:(h + 1) * width]
                      for s in range(PAIR) for ref in (ref_f, ref_b) for h in range(4)], axis=0)


def _unchain(y, o_f, o_b):
    for s in range(PAIR):
        o_f[0, s, 0] = jnp.concatenate([y[s * 8 + h] for h in range(4)], axis=-1)
        o_b[0, s, 0] = jnp.concatenate([y[s * 8 + 4 + h] for h in range(4)], axis=-1)


def _dir_masks(L):
    shape = (N_CHAIN, L, L)
    back = _is_back(shape)
    row = lax.broadcasted_iota(jnp.int32, shape, 1)
    col = lax.broadcasted_iota(jnp.int32, shape, 2)
    ahead = jnp.where(back, col - row, row - col)
    return ahead >= 0, ahead > 0, row == col


def _chunk_end(ci):
    L = ci.shape[1]
    return jnp.where(_is_back((N_CHAIN, 1, 1)), ci[:, 0:1], ci[:, L - 1:L])


def _split_refs(refs, n_in, has_s0, has_sfin):
    ins = refs[:n_in]
    pos = n_in
    s0_ref = None
    if has_s0:
        s0_ref = refs[pos]
        pos += 3
    of_ref, ob_ref = refs[pos], refs[pos + 1]
    pos += 2
    sfin_ref = refs[pos] if has_sfin else None
    return ins, s0_ref, of_ref, ob_ref, sfin_ref, refs[-1]


def _init_state(s_scr, s0_ref):
    @pl.when(pl.program_id(1) == 0)
    def _():
        if s0_ref is None:
            s_scr[...] = jnp.zeros_like(s_scr)
        else:
            s_scr[...] = s0_ref[0]


def _emit_state(sfin_ref, s_new):
    if sfin_ref is None:
        return

    @pl.when(pl.program_id(1) == pl.num_programs(1) - 1)
    def _():
        sfin_ref[0] = s_new


def _rwkv_kernel(*refs, dot, has_s0, has_sfin):
    (rf, rb, vf, vb, kkf, kkb, lwf, lwb, kf, kb, af, ab), s0_ref, yf_ref, yb_ref, sfin_ref, s_scr = _split_refs(
        refs, 12, has_s0, has_sfin)
    _init_state(s_scr, s0_ref)
    n = D_N
    r = _chains(rf, rb, n)
    v = _chains(vf, vb, n)
    kk = _chains(kkf, kkb, n)
    lw = _chains(lwf, lwb, n)
    k = _chains(kf, kb, n)
    a = _chains(af, ab, n)
    L = r.shape[1]
    S = s_scr[...]
    incl, strict, diag = _dir_masks(L)
    ci = _dot_exact_lhs(jnp.where(incl, 1.0, 0.0), lw, _NN)
    ce = ci - lw
    cl = _chunk_end(ci)
    e_neg = jnp.exp(-ci)
    b = a * kk
    alpha = kk * jnp.exp(ce)
    rho = r * jnp.exp(ci)
    beta = b * e_neg
    kappa = k * e_neg
    e_end = jnp.exp(cl - ci)
    ar = jnp.concatenate([alpha, rho], axis=1)
    bk = jnp.concatenate([beta, kappa], axis=1)
    w = dot(ar, bk, _NT)
    nmat = jnp.where(strict, w[:, :L, :L], 0.0)
    mmat = jnp.where(strict, w[:, :L, L:], 0.0)
    p1 = jnp.where(incl, w[:, L:, :L], 0.0)
    p2 = jnp.where(incl, w[:, L:, L:], 0.0)
    x = jnp.where(diag, 1.0, 0.0) - nmat
    p = dot(nmat, nmat, _NN)
    span = 2
    while True:
        x = x + dot(x, p, _NN)
        span *= 2
        if span >= L:
            break
        p = dot(p, p, _NN)
    us = dot(ar, S, _NT)
    rhs = us[:, :L] + dot(mmat, v, _NN)
    d = -dot(x, rhs, _NN)
    dv = jnp.concatenate([d, v], axis=1)
    pp = jnp.concatenate([p1, p2], axis=2)
    _unchain(us[:, L:] + dot(pp, dv, _NN), yf_ref, yb_ref)
    bk_end = jnp.concatenate([b * e_end, k * e_end], axis=1)
    s_new = S * jnp.exp(cl) + dot(dv, bk_end, _TN)
    s_scr[...] = s_new
    _emit_state(sfin_ref, s_new)


def _gla_kernel(*refs, dot, has_s0, has_sfin):
    (qf, qb, kf, kb, vf, vb, laf, lab), s0_ref, of_ref, ob_ref, sfin_ref, s_scr = _split_refs(
        refs, 8, has_s0, has_sfin)
    _init_state(s_scr, s0_ref)
    q4 = _chains(qf, qb, 128)
    k4 = _chains(kf, kb, 128)
    la4 = _chains(laf, lab, 128)
    v = _chains(vf, vb, C_DV)
    g, L, lanes = q4.shape
    dk = C_DK
    n_sub = L // GLA_SUB
    st = s_scr[...]
    incl, _, _ = _dir_masks(L)
    c = _dot_exact_lhs(jnp.where(incl, 1.0, 0.0), la4, _NN)
    shape = (g, L, lanes)
    back = _is_back(shape)
    lane_blk = lax.broadcasted_iota(jnp.int32, shape, 2) // dk
    row_blk = lax.broadcasted_iota(jnp.int32, shape, 1) // GLA_SUB
    cref_f = jnp.zeros(shape, F32)
    cref_b = jnp.zeros(shape, F32)
    for j in range(1, n_sub):
        cref_f = jnp.where(lane_blk == j, c[:, j * GLA_SUB - 1:j * GLA_SUB], cref_f)
        cref_b = jnp.where(lane_blk == j - 1, c[:, j * GLA_SUB:j * GLA_SUB + 1], cref_b)
    cref = jnp.where(back, cref_b, cref_f)
    q_on = row_blk == lane_blk
    k_on = jnp.where(back, row_blk - lane_blk, lane_blk - row_blk) >= 0
    qh = jnp.where(q_on, q4 * jnp.exp(jnp.where(q_on, c - cref, 0.0)), 0.0)
    kh = jnp.where(k_on, k4 * jnp.exp(jnp.where(k_on, cref - c, 0.0)), 0.0)
    att = jnp.where(incl, dot(qh, kh, _NT), 0.0)
    cl = _chunk_end(c)
    qe = (q4 * jnp.exp(c))[:, :, :dk]
    ke = (k4 * jnp.exp(cl - c))[:, :, :dk]
    _unchain(dot(qe, st, _NT) + dot(att, v, _NN), of_ref, ob_ref)
    s_new = st * jnp.exp(cl[:, :, :dk]) + dot(v, ke, _TN)
    s_scr[...] = s_new
    _emit_state(sfin_ref, s_new)


def _recurrence_calls(kernel_fn, name, pairs, singles_f, singles_b, s0_lat, state_dims, out_width):
    def run(view, grid, group, s0, prev_out):
        nc = view[2]
        fwd_map = lambda p, c: (group(p), 0, c, 0, 0)
        bwd_map = lambda p, c: (group(p), 0, nc - 1 - c, 0, 0)
        blk = lambda w: (1, PAIR, 1, CHUNK, w)
        args, in_specs = [], []
        for af, ab in [(a, a) for a in pairs] + list(zip(singles_f, singles_b)):
            w = af.shape[-1]
            args += [af.reshape(view + (w,)), ab.reshape(view + (w,))]
            in_specs += [pl.BlockSpec(blk(w), fwd_map), pl.BlockSpec(blk(w), bwd_map)]
        out_specs = [pl.BlockSpec(blk(out_width), fwd_map), pl.BlockSpec(blk(out_width), bwd_map)]
        out_shape = [jax.ShapeDtypeStruct(view + (out_width,), F32)] * 2
        aliases = {}
        if s0 is not None:
            args += [s0] + [o.reshape(view + (out_width,)) for o in prev_out]
            in_specs += [_full(s0.shape), pl.BlockSpec(memory_space=pl.ANY), pl.BlockSpec(memory_space=pl.ANY)]
            aliases = {len(args) - 2: 0, len(args) - 1: 1}
        else:
            out_specs.append(pl.BlockSpec((1, N_CHAIN) + state_dims, lambda p, c: (p, 0, 0, 0)))
            out_shape.append(jax.ShapeDtypeStruct((grid[0], N_CHAIN) + state_dims, F32))
        return pl.pallas_call(
            functools.partial(kernel_fn, has_s0=s0 is not None, has_sfin=s0 is None),
            grid=grid, in_specs=in_specs, out_specs=out_specs, out_shape=out_shape,
            input_output_aliases=aliases, scratch_shapes=[pltpu.VMEM((N_CHAIN,) + state_dims, F32)],
            compiler_params=pltpu.CompilerParams(dimension_semantics=("parallel", "arbitrary")),
            name=name + ("_latent" if s0 is not None else "_context"),
        )(*args)

    ctx_nc = CTX_LEN // CHUNK
    ctx_view = (N_TOK // (PAIR * CTX_LEN), PAIR, ctx_nc, CHUNK)
    o_f, o_b, s_fin = run(ctx_view, (N_CTX_SEQ // PAIR, ctx_nc), lambda p: p, None, None)
    lat_nc = LAT_LEN // CHUNK
    lat_view = (N_TOK // (PAIR * LAT_LEN), PAIR, lat_nc, CHUNK)
    o_f, o_b = run(lat_view, (1, lat_nc), lambda p: N_CTX // (PAIR * LAT_LEN), s0_lat, (o_f, o_b))
    return o_f.reshape(N_TOK, out_width), o_b.reshape(N_TOK, out_width), s_fin


def _layer_norm(x, g, b):
    mu = jnp.mean(x, axis=-1, keepdims=True)
    xc = x - mu
    var = jnp.mean(xc * xc, axis=-1, keepdims=True)
    return xc * lax.rsqrt(var + LN_EPS) * g + b


def _merge_kernel(xc_ref, xl_ref, mod_ref, oa_ref, ob_ref, cof_ref, cob_ref, cgate_ref, yf_ref, yb_ref, bonus_ref,
                  dgate_ref, wg_ref, wbr_ref, wout_ref, cnorm_ref, dlng_ref, dlnb_ref, lng_ref, lnb_ref, wr_ref, br_ref,
                  bd_ref, x1_o, h2_o, topi_o, topw_o):
    x = _tile_rows(pl.program_id(0), xc_ref, xl_ref)
    m = mod_ref[0]
    sh1, sc1, g1, sh2, sc2 = m[0:1], m[1:2], m[2:3], m[3:4], m[4:5]
    bd = bd_ref[...]
    inv_n = 1.0 / D_N
    co = cof_ref[...] + cob_ref[...]
    o_c = co * lax.rsqrt(_dot_exact_rhs(co * co, bd) * inv_n + RMS_EPS) * cnorm_ref[...] * cgate_ref[...]
    y = yf_ref[...] + yb_ref[...]
    yc = y - _dot_exact_rhs(y, bd) * inv_n
    var = _dot_exact_rhs(yc * yc, bd) * inv_n
    o_d = (yc * lax.rsqrt(var + D_GN_EPS) * dlng_ref[...] + dlnb_ref[...] + bonus_ref[...]) * dgate_ref[...]
    branches = [b.astype(BF16) for b in (oa_ref[...], ob_ref[...], o_c, o_d)]
    h = (x * (1.0 + sc1) + sh1).astype(BF16)
    blocks = []
    for cb in range(D_MODEL // MXU_WIDTH):
        merged = None
        for n in range(N_BRANCH):
            cols = slice(n * D_MODEL + cb * MXU_WIDTH, n * D_MODEL + (cb + 1) * MXU_WIDTH)
            gate = _sigmoid(jnp.dot(h, wg_ref[0, :, cols], preferred_element_type=F32))
            term = gate * jnp.dot(branches[n], wbr_ref[0, n, :, cb * MXU_WIDTH:(cb + 1) * MXU_WIDTH],
                                  preferred_element_type=F32)
            merged = term if merged is None else merged + term
        blocks.append(merged.astype(BF16))
    mix = jnp.dot(jnp.concatenate(blocks, axis=1), wout_ref[0], preferred_element_type=F32)
    x1 = _layer_norm(ALPHA * x + g1 * mix, lng_ref[...], lnb_ref[...])
    x1_o[...] = x1
    h2 = x1 * (1.0 + sc2) + sh2
    h2_o[...] = h2.astype(BF16)
    logits = _dot3(h2, wr_ref[...]) + br_ref[...]
    tm, n_e = logits.shape
    lane_e = lax.broadcasted_iota(jnp.int32, (tm, n_e), 1)
    lane_o = lax.broadcasted_iota(jnp.int32, (tm, topi_o.shape[1]), 1)
    top_i = jnp.zeros((tm, topi_o.shape[1]), jnp.int32)
    top_v = jnp.zeros((tm, topw_o.shape[1]), F32)
    vals = []
    for kth in range(TOP_K):
        mx = jnp.max(logits, axis=-1, keepdims=True)
        idx = jnp.min(jnp.where(logits == mx, lane_e, n_e), axis=-1, keepdims=True)
        vals.append(mx)
        top_i = jnp.where(lane_o == kth, idx, top_i)
        logits = jnp.where(lane_e == idx, -jnp.inf, logits)
    es = [jnp.exp(vk - vals[0]) for vk in vals]
    den = es[0] + es[1] + es[2] + es[3]
    for kth in range(TOP_K):
        top_v = jnp.where(lane_o == kth, es[kth] / den, top_v)
    topi_o[...] = top_i
    topw_o[...] = top_v


def merge_and_route(x_ctx, x_lat, mod, o_a, o_b, co_f, co_b, cgate, y_f, y_b, bonus, dgate, w_g, w_br, w_out, layer, sp):
    tm = MERGE_ROW_TILE
    hn = D_HEADS * D_N
    row = lambda w: pl.BlockSpec((tm, w), lambda t: (t, 0))
    small = [sp[k] for k in ('c_norm', 'd_ln_g', 'd_ln_b', 'ln_g', 'ln_b', 'w_router', 'b_router')]
    index = [layer, layer, layer, 2 * layer, 2 * layer, layer, layer]
    bd = _head_block_diag()
    return pl.pallas_call(
        _merge_kernel,
        grid=(N_TOK // tm,),
        in_specs=([_ctx_rows_spec(D_MODEL, tm), _lat_rows_spec(D_MODEL, tm), _mod_spec(layer, tm)]
                  + [row(hn)] * 9 + [_layer_block(w, layer) for w in (w_g, w_br, w_out)]
                  + [_layer_item(a, i) for a, i in zip(small, index)] + [_full(bd.shape)]),
        out_specs=[row(D_MODEL), row(D_MODEL), row(128), row(128)],
        out_shape=[jax.ShapeDtypeStruct((N_TOK, D_MODEL), F32), jax.ShapeDtypeStruct((MOE_ROWS, D_MODEL), BF16),
                   jax.ShapeDtypeStruct((N_TOK, 128), jnp.int32), jax.ShapeDtypeStruct((N_TOK, 128), F32)],
        compiler_params=pltpu.CompilerParams(dimension_semantics=("parallel",), vmem_limit_bytes=VMEM_LIMIT),
        name="merge_and_route",
    )(x_ctx, x_lat, mod, o_a, o_b, co_f, co_b, cgate, y_f, y_b, bonus, dgate, w_g, w_br, w_out, *small, bd)


def _moe_kernel(te_ref, tv_ref, first_ref, slot_ref, next_ref, x_ref, w1_hbm, b1_ref, w2_hbm, b2_ref, perm_ref, *rest,
                layer):
    y_ref, w1buf, w2buf, sem, w1s, w2s, hs = rest[-7:]
    t = pl.program_id(0)
    valid = tv_ref[t] != 0
    d_model, two_f = w1s.shape
    n_blk = two_f // MXU_WIDTH
    half = MXU_WIDTH // 2

    def fetch(expert, slot):
        return (pltpu.make_async_copy(w1_hbm.at[layer, expert], w1buf.at[slot], sem.at[0, slot]),
                pltpu.make_async_copy(w2_hbm.at[layer, expert], w2buf.at[slot], sem.at[1, slot]))

    @pl.when(t == 0)
    def _():
        for cp in fetch(te_ref[0], 0):
            cp.start()

    @pl.when(first_ref[t] == 1)
    def _():
        slot = slot_ref[t]
        for cp in fetch(te_ref[t], slot):
            cp.wait()

        @pl.when(next_ref[t] >= 0)
        def _():
            for cp in fetch(next_ref[t], 1 - slot):
                cp.start()

        for blk in range(n_blk):
            sl = slice(blk * MXU_WIDTH, (blk + 1) * MXU_WIDTH)
            wb = w1buf[slot, :, sl].astype(BF16)
            w1s[:, sl] = jnp.dot(wb, perm_ref[...], preferred_element_type=F32).astype(BF16)
        w2s[...] = w2buf[slot].astype(BF16)

    @pl.when(valid)
    def _():
        x = x_ref[...]
        for blk in range(n_blk):
            sl = slice(blk * MXU_WIDTH, (blk + 1) * MXU_WIDTH)
            u = jnp.dot(x, w1s[:, sl], preferred_element_type=F32) + b1_ref[0, 0, :, sl]
            glu = jnp.minimum(u[:, :half], SWIGLU_LIMIT)
            lin = jnp.clip(u[:, half:], -SWIGLU_LIMIT, SWIGLU_LIMIT)
            hs[:, blk * half:(blk + 1) * half] = (glu * _sigmoid(SWIGLU_ALPHA * glu) * (lin + 1.0)).astype(BF16)
        y = jnp.dot(hs[...], w2s[...], preferred_element_type=F32) + b2_ref[0, 0]
        y_ref[...] = y.astype(y_ref.dtype)

    @pl.when(jnp.logical_not(valid))
    def _():
        y_ref[...] = jnp.zeros_like(y_ref)


def _deinterleave_perm():
    half = MXU_WIDTH // 2
    src = np.arange(MXU_WIDTH)
    dst = np.where(src % 2 == 0, src // 2, half + src // 2)
    p = np.zeros((MXU_WIDTH, MXU_WIDTH), np.float32)
    p[src, dst] = 1.0
    return jnp.asarray(p, BF16)


def _moe_dispatch(top_i):
    n, k = top_i.shape
    tm = MOE_ROW_TILE
    p_rows = n * k + N_EXPERTS * tm
    experts = jnp.arange(N_EXPERTS, dtype=jnp.int32)
    onehot = top_i[:, :, None] == experts
    sel = jnp.sum(onehot.astype(jnp.int32), axis=1)
    before = jnp.cumsum(sel, axis=0) - sel
    counts = jnp.sum(sel, axis=0)
    padded = ((counts + tm - 1) // tm) * tm
    ends = jnp.cumsum(padded)
    starts = ends - padded
    pos = jnp.sum(jnp.where(onehot, (before + starts)[:, None, :], 0), axis=-1)
    n_tiles = p_rows // tm
    tile_start = jnp.arange(n_tiles, dtype=jnp.int32) * tm
    tile_valid = (tile_start < ends[-1]).astype(jnp.int32)
    last_tile = ends[-1] // tm - 1
    tile_expert = jnp.sum(ends[None, :] <= jnp.minimum(tile_start, last_tile * tm)[:, None], axis=1).astype(jnp.int32)
    keys = jnp.sort((top_i * n + jnp.arange(n, dtype=jnp.int32)[:, None]).reshape(-1))
    tile_onehot = tile_expert[:, None] == experts[None, :]
    lookup = lambda table: jnp.sum(jnp.where(tile_onehot, table[None, :], 0), axis=1)
    tile_rank0 = tile_start - lookup(starts)
    rank = tile_rank0[:, None] + jnp.arange(tm, dtype=jnp.int32)[None, :]
    sorted_at = jnp.clip(lookup(jnp.cumsum(counts) - counts)[:, None] + rank, 0, n * k - 1)
    tile_keys = keys[sorted_at.reshape(-1)].reshape(n_tiles, tm)
    filler = (tile_start[:, None] + jnp.arange(tm, dtype=jnp.int32)[None, :]) % n
    src_tok = jnp.where(rank < lookup(counts)[:, None], tile_keys % n, filler)
    nst = n_tiles // MOE_SPLITS
    idx = jnp.arange(nst, dtype=jnp.int32)
    tables = []
    for h in range(MOE_SPLITS):
        te_h = tile_expert[h * nst:(h + 1) * nst]
        is_first = jnp.concatenate([jnp.ones((1,), jnp.int32), (te_h[1:] != te_h[:-1]).astype(jnp.int32)])
        slot = (jnp.cumsum(is_first) - 1) % 2
        nxt = jnp.min(jnp.where(jnp.logical_and(idx[None, :] > idx[:, None], is_first[None, :] == 1),
                                idx[None, :], nst), axis=1)
        next_expert = jnp.sum(jnp.where(idx[None, :] == nxt[:, None], te_h[None, :] + 1, 0), axis=1) - 1
        tables.append((te_h, tile_valid[h * nst:(h + 1) * nst], is_first, slot.astype(jnp.int32),
                       next_expert.astype(jnp.int32)))
    return pos, src_tok.reshape(-1), tables, p_rows


def moe_experts(h2, top_i, layer, w1, b1, w2, b2):
    n = top_i.shape[0]
    d = h2.shape[1]
    depth, e, _, two_f = w1.shape
    f = two_f // 2
    tm = MOE_ROW_TILE
    pos, src_tok, tables, p_rows = _moe_dispatch(top_i)
    assert h2.shape[0] == p_rows
    src_tok = lax.optimization_barrier(src_tok)
    b1p = b1.reshape(depth, e, two_f // MXU_WIDTH, MXU_WIDTH // 2, 2).swapaxes(3, 4).reshape(depth, e, 1, two_f)
    b2r = b2.reshape(depth, e, 1, d)
    expert_vec = lambda w: pl.BlockSpec((1, 1, 1, w), lambda t, te, *_: (layer, te[t], 0, 0))
    rows = p_rows // MOE_SPLITS
    nst = rows // tm
    xs = [h2.at[src_tok[h * rows:(h + 1) * rows]].get(mode="promise_in_bounds") for h in range(MOE_SPLITS)]
    ys = None
    for h in range(MOE_SPLITS):
        in_specs = [
            pl.BlockSpec((tm, d), lambda t, *_: (t, 0)),
            pl.BlockSpec(memory_space=pl.ANY),
            expert_vec(two_f),
            pl.BlockSpec(memory_space=pl.ANY),
            expert_vec(d),
            pl.BlockSpec((MXU_WIDTH, MXU_WIDTH), lambda t, *_: (0, 0)),
        ]
        args = [*tables[h], xs[h], w1, b1p, w2, b2r, _deinterleave_perm()]
        aliases = {}
        if ys is not None:
            in_specs.append(pl.BlockSpec(memory_space=pl.ANY))
            aliases = {len(args): 0}
            args.append(ys)
        ys = pl.pallas_call(
            functools.partial(_moe_kernel, layer=layer),
            grid_spec=pltpu.PrefetchScalarGridSpec(
                num_scalar_prefetch=len(tables[h]), grid=(nst,), in_specs=in_specs,
                out_specs=pl.BlockSpec((tm, d), lambda t, *_, h=h: (h * nst + t, 0)),
                scratch_shapes=[pltpu.VMEM((2, d, two_f), F32), pltpu.VMEM((2, f, d), F32),
                                pltpu.SemaphoreType.DMA((2, 2)),
                                pltpu.VMEM((d, two_f), BF16), pltpu.VMEM((f, d), BF16), pltpu.VMEM((tm, f), BF16)]),
            out_shape=jax.ShapeDtypeStruct((p_rows, d), BF16),
            input_output_aliases=aliases,
            compiler_params=pltpu.CompilerParams(dimension_semantics=("arbitrary",),
                                                 vmem_limit_bytes=48 * 1024 * 1024),
            name="moe_experts",
        )(*args)
    return ys, pos


def _final_kernel(x1_ref, mod_ref, ys_ref, topw_ref, lng_ref, lnb_ref, o_ref):
    g2 = mod_ref[0, 5:6]
    moe = None
    for kth in range(TOP_K):
        term = ys_ref[kth].astype(F32) * topw_ref[:, kth:kth + 1]
        moe = term if moe is None else moe + term
    o_ref[...] = _layer_norm(ALPHA * x1_ref[...] + g2 * moe, lng_ref[...], lnb_ref[...])


def combine_and_norm(x1, mod, ys, pos, top_w, layer, sp):
    tm = ROW_TILE
    ln_g, ln_b = sp['ln_g'], sp['ln_b']
    outs = []
    for t0, n_rows in ((0, N_CTX), (CTX_TILES, N_TOK - N_CTX)):
        idx = lax.optimization_barrier(pos[t0 * tm:t0 * tm + n_rows].T.reshape(-1))
        rows = ys.at[idx].get(mode="promise_in_bounds").reshape(TOP_K, n_rows, D_MODEL)
        outs.append(pl.pallas_call(
            _final_kernel,
            grid=(n_rows // tm,),
            in_specs=[pl.BlockSpec((tm, D_MODEL), lambda t, t0=t0: (t0 + t, 0)),
                      pl.BlockSpec((1, 6, D_MODEL),
                                   lambda t, t0=t0: (layer * (1 + N_LAT_SEQ) + _mod_row(t0 + t), 0, 0)),
                      pl.BlockSpec((TOP_K, tm, D_MODEL), lambda t: (0, t, 0)),
                      pl.BlockSpec((tm, 128), lambda t, t0=t0: (t0 + t, 0)),
                      _layer_item(ln_g, 2 * layer + 1), _layer_item(ln_b, 2 * layer + 1)],
            out_specs=pl.BlockSpec((tm, D_MODEL), lambda t: (t, 0)),
            out_shape=jax.ShapeDtypeStruct((n_rows, D_MODEL), F32),
            compiler_params=pltpu.CompilerParams(dimension_semantics=("parallel",)),
            name="combine_and_norm",
        )(x1, mod, rows, top_w, ln_g, ln_b))
    return outs


def kernel(x_prompt, x_sample, cache_a_k, cache_a_v, cache_b_ckv, cache_b_kpe, state_c, state_d, c,
           c_ctx, w_mod, b_mod, w_in, a_sink, b_q_norm, b_w_uq, b_kv_norm, b_w_ukv, c_w_gate, c_b_gate,
           c_norm, d_mu, d_w0, d_w2, d_a0, d_a2, d_g2, d_k_k, d_k_a, d_r_k, d_ln_g, d_ln_b, w_br, w_out,
           ln_g, ln_b, w_router, b_router, w_mlp1, b_mlp1, w_mlp2, b_mlp2):
    sp = stacked_params(dict(
        a_sink=a_sink, b_q_norm=b_q_norm, b_w_uq=b_w_uq, b_kv_norm=b_kv_norm, b_w_ukv=b_w_ukv, c_w_gate=c_w_gate,
        c_b_gate=c_b_gate, c_norm=c_norm, d_mu=d_mu, d_w0=d_w0, d_w2=d_w2, d_a0=d_a0, d_a2=d_a2, d_g2=d_g2,
        d_k_k=d_k_k, d_k_a=d_k_a, d_r_k=d_r_k, d_ln_g=d_ln_g, d_ln_b=d_ln_b, ln_g=ln_g, ln_b=ln_b,
        w_router=w_router, b_router=b_router))
    assert x_prompt.shape == (N_CTX_SEQ, CTX_LEN, D_MODEL) and x_sample.shape == (N_LAT_SEQ, LAT_LEN, D_MODEL)
    x_ctx, x_lat = x_prompt.reshape(N_CTX, D_MODEL), x_sample.reshape(-1, D_MODEL)
    cond8 = jnp.concatenate([c_ctx[None], c, jnp.zeros((8 - 1 - N_LAT_SEQ, D_MODEL), F32)], axis=0)
    mod = modulation_table(cond8, w_mod, b_mod)[:, :1 + N_LAT_SEQ].reshape(DEPTH * (1 + N_LAT_SEQ), 6, D_MODEL)
    tables = _rope_tables()
    w_small, w_g = prepare_in_weights(w_in)
    w_br_bf, w_out_bf = w_br.astype(BF16), w_out.astype(BF16)
    new = {name: [] for name in ("a_k", "a_v", "b_ckv", "b_kpe", "c", "d")}
    for l in range(DEPTH):
        (aq, ak, av, bq, bckv, bkpe, cq4, ck4, cla_f, cla_b, cv, cgate,
         r, v, kk, lw_f, lw_b, k_f, k_b, a_f, a_b, bonus, dgate) = mixer_prelude(
             x_ctx, x_lat, mod, w_small, l, tables, sp)

        o_a = gqa_attention(aq, ak, av, sp['a_sink'], cache_a_k, cache_a_v, l)
        o_b = mla_attention(bq, bckv, bkpe, sp['b_w_ukv'], cache_b_ckv, cache_b_kpe, l)

        c_s0 = jnp.swapaxes(state_c[:, l], 3, 4).reshape(1, N_CHAIN, C_DV, C_DK)
        co_f, co_b, c_fin = _recurrence_calls(functools.partial(_gla_kernel, dot=_dot1), "gla", [cq4, ck4, cv],
                                              [cla_f], [cla_b], c_s0, (C_DV, C_DK), C_HEADS * C_DV)
        d_s0 = state_d[:, l].reshape(1, N_CHAIN, D_N, D_N)
        y_f, y_b, d_fin = _recurrence_calls(functools.partial(_rwkv_kernel, dot=_dot1), "rwkv7", [r, v, kk],
                                            [lw_f, k_f, a_f], [lw_b, k_b, a_b], d_s0, (D_N, D_N), D_HEADS * D_N)

        x1, h2, top_i, top_w = merge_and_route(x_ctx, x_lat, mod, o_a, o_b, co_f, co_b, cgate, y_f, y_b, bonus, dgate,
                                               w_g, w_br_bf, w_out_bf, l, sp)
        ys, pos = moe_experts(h2, top_i[:, :TOP_K], l, w_mlp1, b_mlp1, w_mlp2, b_mlp2)
        x_ctx, x_lat = combine_and_norm(x1, mod, ys, pos, top_w, l, sp)

        new["a_k"].append(ak[:N_CTX].reshape(N_CTX_SEQ, CTX_LEN, A_KV_HEADS, A_HD).transpose(0, 2, 1, 3))
        new["a_v"].append(av[:N_CTX].reshape(N_CTX_SEQ, CTX_LEN, A_KV_HEADS, A_HD).transpose(0, 2, 1, 3))
        new["b_ckv"].append(bckv[:N_CTX].reshape(N_CTX_SEQ, CTX_LEN, B_KV_LORA))
        new["b_kpe"].append(bkpe[:N_CTX].reshape(N_CTX_SEQ, CTX_LEN, B_ROPE))
        new["c"].append(jnp.swapaxes(c_fin.reshape(N_CTX_SEQ, 2, C_HEADS, C_DV, C_DK), 3, 4))
        new["d"].append(d_fin.reshape(N_CTX_SEQ, 2, D_HEADS, D_N, D_N))
    y_prompt = x_ctx.reshape(x_prompt.shape)
    y_sample = x_lat.reshape(x_sample.shape)
    return (y_prompt, y_sample, *(jnp.stack(new[name], axis=1) for name in ("a_k", "a_v", "b_ckv", "b_kpe", "c", "d")))
```

```python
import functools

import jax
import jax.numpy as jnp
import numpy as np
from jax import lax
from jax.experimental import pallas as pl
from jax.experimental.pallas import tpu as pltpu

F32 = jnp.float32
BF16 = jnp.bfloat16

MXU_WIDTH = 256
VMEM_LIMIT = 56 * 1024 * 1024

D_MODEL = 1024
DEPTH = 2
GRID_W = 64
ROPE_BASE = 10000.0
A_HEADS, A_KV_HEADS, A_HD = 4, 2, 64
B_HEADS, B_NOPE, B_ROPE, B_VD, B_Q_LORA, B_KV_LORA = 4, 64, 32, 64, 192, 128
C_HEADS, C_DK, C_DV, C_GATE_RANK, C_GATE_TEMP = 4, 32, 64, 16, 16.0
D_HEADS, D_N, D_DECAY_RANK, D_AAA_RANK, D_GATE_RANK, D_GN_EPS = 4, 64, 64, 64, 128, 64e-5
BRANCH_W = 256
N_BRANCH = 4
N_EXPERTS = 32
TOP_K = 4
SWIGLU_LIMIT = 7.0
SWIGLU_ALPHA = 1.702
ALPHA = (2 * DEPTH) ** 0.25
LN_EPS = 1e-5
RMS_EPS = 1e-6

N_CTX_SEQ, CTX_LEN = 16, 256
N_LAT_SEQ, LAT_LEN = 2, 2048
N_CTX = N_CTX_SEQ * CTX_LEN
N_TOK = N_CTX + N_LAT_SEQ * LAT_LEN
ROW_TILE = 256
N_TILES = N_TOK // ROW_TILE
CTX_TILES = N_CTX // ROW_TILE
LAT_TILES_PER_SEQ = LAT_LEN // ROW_TILE
MLA_HEAD_LANES = 128
MERGE_ROW_TILE = 512
MOE_ROW_TILE = 256
MOE_ROWS = N_TOK * TOP_K + N_EXPERTS * MOE_ROW_TILE
MOE_SPLIT_TILES = (0, 48, MOE_ROWS // MOE_ROW_TILE)

_ORIG = dict(aq=(0, 256), ak=(256, 384), av=(384, 512), bcq=(512, 704), bckv=(704, 832), bkpe=(832, 864),
             cq=(864, 992), ck=(992, 1120), cv=(1120, 1376), cog=(1376, 1632), caf=(1632, 1648), cab=(1648, 1664),
             zd=(1664, 2816))
_ORDER = ("aq", "ak", "av", "cq", "ck", "cv", "cog", "zd", "bcq", "caf", "cab", "bkpe", "bckv")
COL = {}
_off = 0
for _name in _ORDER:
    _w = _ORIG[_name][1] - _ORIG[_name][0]
    COL[_name] = (_off, _off + _w)
    _off += _w
SMALL_COLS = _off
G_START = SMALL_COLS


def _cs(name):
    return slice(*COL[name])


def _split2(x):
    hi = x.astype(BF16)
    lo = (x - hi.astype(F32)).astype(BF16)
    return hi, lo


def _bdot(a, b, dims):
    return lax.dot_general(a, b, dims, preferred_element_type=F32)


_D2 = (((1,), (0,)), ((), ()))
_D2T = (((1,), (1,)), ((), ()))
_NN = (((2,), (1,)), ((0,), (0,)))
_NT = (((2,), (2,)), ((0,), (0,)))
_TN = (((1,), (1,)), ((0,), (0,)))


def _dot1(a, b, dims=_D2):
    return _bdot(a.astype(BF16), b.astype(BF16), dims)


def _dot3(a, b, dims=_D2):
    ah, al = _split2(a)
    bh, bl = _split2(b)
    return _bdot(ah, bh, dims) + (_bdot(ah, bl, dims) + _bdot(al, bh, dims))


def _dot_exact_lhs(a01, b, dims=_D2):
    a = a01.astype(BF16)
    h, l = _split2(b)
    return _bdot(a, h, dims) + _bdot(a, l, dims)


def _dot_exact_rhs(a, b01, dims=_D2):
    b = b01.astype(BF16)
    h, l = _split2(a)
    return _bdot(h, b, dims) + _bdot(l, b, dims)


def _sigmoid(x):
    return 0.5 * jnp.tanh(0.5 * x) + 0.5


def _softplus(x):
    return jnp.maximum(x, 0.0) + jnp.log(1.0 + jnp.exp(-jnp.abs(x)))


def _mod_row(t, tile=ROW_TILE):
    return jnp.where(t < N_CTX // tile, 0, 1 + (t - N_CTX // tile) // (LAT_LEN // tile))


def _full(shape):
    nd = len(shape)
    return pl.BlockSpec(shape, lambda *_: (0,) * nd)


def _ctx_rows_spec(width, tile=ROW_TILE):
    return pl.BlockSpec((tile, width), lambda t: (jnp.minimum(t, N_CTX // tile - 1), 0))


def _lat_rows_spec(width, tile=ROW_TILE):
    return pl.BlockSpec((tile, width), lambda t: (jnp.maximum(t - N_CTX // tile, 0), 0))


def _tile_rows(t, ctx_ref, lat_ref):
    return jnp.where(t < N_CTX // ctx_ref.shape[0], ctx_ref[...], lat_ref[...])


MOD_COL_TILE = 1536


def _mod_kernel(c_ref, w_ref, b_ref, o_ref):
    c = c_ref[...]
    o_ref[0] = _dot3(c * _sigmoid(c), w_ref[0]) + b_ref[0]


def modulation_table(cond8, w_mod, b_mod):
    depth, d, six_d = w_mod.shape
    return pl.pallas_call(
        _mod_kernel,
        grid=(depth, six_d // MOD_COL_TILE),
        in_specs=[pl.BlockSpec((8, d), lambda l, j: (0, 0)),
                  pl.BlockSpec((1, d, MOD_COL_TILE), lambda l, j: (l, 0, j)),
                  pl.BlockSpec((1, 1, MOD_COL_TILE), lambda l, j: (l, 0, j))],
        out_specs=pl.BlockSpec((1, 8, MOD_COL_TILE), lambda l, j: (l, 0, j)),
        out_shape=jax.ShapeDtypeStruct((depth, 8, six_d), F32),
        compiler_params=pltpu.CompilerParams(dimension_semantics=("parallel", "parallel")),
        name="modulation",
    )(cond8, w_mod, b_mod.reshape(depth, 1, six_d))


WPREP_ROWS = 128


def _wprep_kernel(w_ref, small_ref, gate_ref):
    for name in _ORDER:
        lo, hi = _ORIG[name]
        small_ref[0, :, _cs(name)] = w_ref[0, :, lo:hi].astype(BF16)
    gate_ref[0] = w_ref[0, :, G_START:].astype(BF16)


def prepare_in_weights(w_in):
    depth, d, cols = w_in.shape
    return pl.pallas_call(
        _wprep_kernel,
        grid=(depth, d // WPREP_ROWS),
        in_specs=[pl.BlockSpec((1, WPREP_ROWS, cols), lambda l, r: (l, r, 0))],
        out_specs=[pl.BlockSpec((1, WPREP_ROWS, SMALL_COLS), lambda l, r: (l, r, 0)),
                   pl.BlockSpec((1, WPREP_ROWS, cols - G_START), lambda l, r: (l, r, 0))],
        out_shape=[jax.ShapeDtypeStruct((depth, d, SMALL_COLS), BF16),
                   jax.ShapeDtypeStruct((depth, d, cols - G_START), BF16)],
        compiler_params=pltpu.CompilerParams(dimension_semantics=("parallel", "parallel")),
        name="prepare_in_weights",
    )(w_in)


def _rot_pairs(x, half, lane_mod_base=0):
    w = x.shape[-1]
    lane = lax.broadcasted_iota(jnp.int32, (1, w), 1) - lane_mod_base
    first = (lane % (2 * half)) < half
    return jnp.where(first, -pltpu.roll(x, w - half, axis=1), pltpu.roll(x, half, axis=1))


def _pre_kernel(xc_ref, xl_ref, xp_ref, xn_ref, mod_ref, w_ref, ca_ref, sa_ref, cb_ref, sb_ref, ck_ref, sk_ref,
                qnorm_ref, kvnorm_ref, wuq_ref, cwg_ref, cbg_ref, rep_ref, mu_ref, dw0_ref, dw2_ref, da0_ref,
                da2_ref, dg2_ref, dkk_ref, dka_ref, drk_ref, bd_ref,
                aq_o, ak_o, av_o, bq_o, bckv_o, bkpe_o, cq4_o, ck4_o, claf_o, clab_o, cv_o, cgate_o,
                r_o, v_o, kk_o, lwf_o, lwb_o, kf_o, kb_o, af_o, ab_o, bonus_o, dgate_o):
    t = pl.program_id(0)
    tm = xc_ref.shape[0]
    sh1 = mod_ref[0, 0:1, :]
    sc1 = mod_ref[0, 1:2, :]

    def modulate(xv):
        return (xv * (1.0 + sc1) + sh1).astype(BF16)

    x_tile = _tile_rows(t, xc_ref, xl_ref)
    h_all = jnp.concatenate([modulate(x_tile), modulate(xp_ref[...]), modulate(xn_ref[...])], axis=0)
    z_all = jnp.dot(h_all, w_ref[0], preferred_element_type=F32)
    z = z_all[:tm]

    aq = z[:, _cs("aq")]
    ak = z[:, _cs("ak")]
    aq_o[...] = aq * ca_ref[...] + _rot_pairs(aq, A_HD // 4) * sa_ref[...]
    ak_o[...] = ak * ca_ref[:, :ak.shape[1]] + _rot_pairs(ak, A_HD // 4) * sa_ref[:, :ak.shape[1]]
    av_o[...] = z[:, _cs("av")]

    bcq = z[:, _cs("bcq")]
    qn = bcq * lax.rsqrt(jnp.mean(bcq * bcq, axis=-1, keepdims=True) + RMS_EPS) * qnorm_ref[...]
    bq = _dot1(qn, wuq_ref[...])
    bq_o[...] = bq * cb_ref[...] + _rot_pairs(bq, B_ROPE // 4, lane_mod_base=B_NOPE) * sb_ref[...]
    bckv = z[:, _cs("bckv")]
    bckv_o[...] = bckv * lax.rsqrt(jnp.mean(bckv * bckv, axis=-1, keepdims=True) + RMS_EPS) * kvnorm_ref[...]
    kpe_lo = COL["bkpe"][0] // 128 * 128
    kblk = z[:, kpe_lo:kpe_lo + 128]
    kblk = kblk * ck_ref[...] + _rot_pairs(kblk, B_ROPE // 4) * sk_ref[...]
    bkpe_o[...] = kblk[:, COL["bkpe"][0] - kpe_lo:COL["bkpe"][1] - kpe_lo]

    rep = rep_ref[...]
    cq4_o[...] = _dot1(z[:, _cs("cq")] * (C_DK ** -0.5), rep)
    ck4_o[...] = _dot1(z[:, _cs("ck")], rep)
    cv_o[...] = z[:, _cs("cv")]
    cog = z[:, _cs("cog")]
    cgate_o[...] = cog * _sigmoid(cog)
    for direction, (name, out) in enumerate((("caf", claf_o), ("cab", clab_o))):
        pre = _dot1(z[:, _cs(name)], cwg_ref[direction]) + cbg_ref[direction]
        la_hi, la_lo = _split2(-_softplus(-pre) * (1.0 / C_GATE_TEMP))
        out[...] = _bdot(la_hi, rep, _D2) + _bdot(la_lo, rep, _D2)

    zd_cols = _cs("zd")
    zd = z[:, zd_cols]
    j = (t - CTX_TILES) % LAT_TILES_PER_SEQ
    latent = t >= CTX_TILES
    has_prev = jnp.logical_and(latent, j != 0)
    has_next = jnp.logical_and(latent, j != LAT_TILES_PER_SEQ - 1)
    prev_row = jnp.where(has_prev, z_all[tm + 7:tm + 8, zd_cols], 0.0)
    next_row = jnp.where(has_next, z_all[tm + 8:tm + 9, zd_cols], 0.0)
    row = lax.broadcasted_iota(jnp.int32, (tm, 1), 0)
    up = jnp.where(row == 0, prev_row, pltpu.roll(zd, 1, axis=0))
    dn = jnp.where(row == tm - 1, next_row, pltpu.roll(zd, tm - 1, axis=0))
    zd = zd + (0.5 * (up + dn) - zd) * mu_ref[...]

    hn = D_HEADS * D_N
    d_r, d_k, d_v = zd[:, :hn], zd[:, hn:2 * hn], zd[:, 2 * hn:3 * hn]
    o = 3 * hn
    d_w = (zd[:, o:o + D_DECAY_RANK], zd[:, o + D_DECAY_RANK:o + 2 * D_DECAY_RANK])
    o += 2 * D_DECAY_RANK
    d_a = (zd[:, o:o + D_AAA_RANK], zd[:, o + D_AAA_RANK:o + 2 * D_AAA_RANK])
    o += 2 * D_AAA_RANK
    d_g = zd[:, o:o + D_GATE_RANK]
    bd = bd_ref[...]
    kk = d_k * dkk_ref[...]
    kk = kk / jnp.maximum(jnp.sqrt(_dot_exact_rhs(kk * kk, bd)), 1e-12)
    r_o[...] = d_r
    v_o[...] = d_v
    kk_o[...] = kk
    k_sum = None
    for direction, (lw_o, k_o, a_o) in enumerate(((lwf_o, kf_o, af_o), (lwb_o, kb_o, ab_o))):
        w_log = -_softplus(-(dw0_ref[direction] + _dot1(jnp.tanh(d_w[direction]), dw2_ref[direction]))) - 0.5
        lw_o[...] = -jnp.exp(w_log)
        a = _sigmoid(da0_ref[direction] + _dot1(d_a[direction], da2_ref[direction]))
        k_dir = d_k * (1.0 + (a - 1.0) * dka_ref[...])
        k_o[...] = k_dir
        a_o[...] = a
        k_sum = k_dir if k_sum is None else k_sum + k_dir
    bonus_o[...] = d_v * _dot_exact_rhs(d_r * drk_ref[...] * k_sum, bd)
    dgate_o[...] = _dot1(_sigmoid(d_g), dg2_ref[...])


def _rope_tables():
    pos = np.arange(LAT_LEN)
    rowp, colp = (pos // GRID_W).astype(np.float32), (pos % GRID_W).astype(np.float32)

    f32 = np.float32

    def head_tables(rot_dim):
        quarter = rot_dim // 4
        inv = (f32(ROPE_BASE) ** (-np.arange(quarter, dtype=f32) / f32(quarter))).astype(f32)
        ar = (rowp[:, None] * inv).astype(f32)
        ac = (colp[:, None] * inv).astype(f32)
        cos = np.concatenate([np.cos(ar), np.cos(ar), np.cos(ac), np.cos(ac)], axis=-1).astype(f32)
        sin = np.concatenate([np.sin(ar), np.sin(ar), np.sin(ac), np.sin(ac)], axis=-1).astype(f32)
        return cos, sin

    def with_identity(c, s):
        w = c.shape[1]
        return (jnp.asarray(np.concatenate([np.ones((ROW_TILE, w), f32), c], axis=0)),
                jnp.asarray(np.concatenate([np.zeros((ROW_TILE, w), f32), s], axis=0)))

    ca, sa = head_tables(A_HD)
    ca, sa = with_identity(np.tile(ca, (1, A_HEADS)), np.tile(sa, (1, A_HEADS)))
    cbh, sbh = head_tables(B_ROPE)
    ones, zeros = np.ones((LAT_LEN, B_NOPE), f32), np.zeros((LAT_LEN, B_NOPE), f32)
    qpad = MLA_HEAD_LANES - B_NOPE - B_ROPE
    cb, sb = with_identity(
        np.tile(np.concatenate([ones, cbh, np.ones((LAT_LEN, qpad), f32)], axis=1), (1, B_HEADS)),
        np.tile(np.concatenate([zeros, sbh, np.zeros((LAT_LEN, qpad), f32)], axis=1), (1, B_HEADS)))
    pad = 128 - B_ROPE
    ck, sk = with_identity(np.concatenate([np.ones((LAT_LEN, pad), f32), cbh], axis=1),
                           np.concatenate([np.zeros((LAT_LEN, pad), f32), sbh], axis=1))
    return ca, sa, cb, sb, ck, sk


def _lane_repeat_matrix():
    m = np.zeros((C_HEADS * C_DK, C_HEADS * 128), np.float32)
    for h in range(C_HEADS):
        for g in range(128 // C_DK):
            for d in range(C_DK):
                m[h * C_DK + d, h * 128 + g * C_DK + d] = 1.0
    return jnp.asarray(m, BF16)


def _head_block_diag():
    m = np.kron(np.eye(D_HEADS, dtype=np.float32), np.ones((D_N, D_N), np.float32))
    return jnp.asarray(m, BF16)


PRE_OUT_WIDTHS = (256, 128, 128, B_HEADS * MLA_HEAD_LANES, 128, 32, 512, 512, 512, 512, 256, 256) + (256,) * 11


def _layer_block(arr, layer):
    nd = arr.ndim
    return pl.BlockSpec((1,) + arr.shape[1:], lambda *_: (layer,) + (0,) * (nd - 1))


def _layer_item(arr, index):
    nd = arr.ndim
    return pl.BlockSpec((None,) + arr.shape[1:], lambda *_: (index,) + (0,) * (nd - 1))


def _mod_spec(layer, tile=ROW_TILE):
    return pl.BlockSpec((1, 6, D_MODEL), lambda t: (layer * (1 + N_LAT_SEQ) + _mod_row(t, tile), 0, 0))


def stacked_params(p):
    depth = p['d_mu'].shape[0]
    hn = D_HEADS * D_N
    w_uq = p['b_w_uq'].reshape(depth, B_Q_LORA, B_HEADS, B_NOPE + B_ROPE)
    w_uq = jnp.pad(w_uq, ((0, 0), (0, 0), (0, 0), (0, MLA_HEAD_LANES - B_NOPE - B_ROPE)))
    row = lambda a: a.reshape(depth, 1, -1)
    return dict(
        b_q_norm=row(p['b_q_norm']), b_kv_norm=row(p['b_kv_norm']), w_uq=w_uq.reshape(depth, B_Q_LORA, -1),
        c_w_gate=p['c_w_gate'], c_b_gate=p['c_b_gate'].reshape(depth, 2, 1, -1), d_mu=row(p['d_mu']),
        d_w0=p['d_w0'].reshape(depth, 2, 1, hn), d_w2=p['d_w2'], d_a0=p['d_a0'].reshape(depth, 2, 1, hn),
        d_a2=p['d_a2'], d_g2=p['d_g2'], d_k_k=row(p['d_k_k']), d_k_a=row(p['d_k_a']), d_r_k=row(p['d_r_k']),
        c_norm=row(jnp.tile(p['c_norm'], (1, C_HEADS))), d_ln_g=row(p['d_ln_g']), d_ln_b=row(p['d_ln_b']),
        ln_g=p['ln_g'].reshape(depth * 2, 1, -1), ln_b=p['ln_b'].reshape(depth * 2, 1, -1),
        w_router=p['w_router'], b_router=row(p['b_router']), a_sink=p['a_sink'].reshape(-1), b_w_ukv=p['b_w_ukv'])


def mixer_prelude(x_ctx, x_lat, mod, w_small, layer, tables, sp):
    tm = ROW_TILE
    lat8 = lambda t: (t - CTX_TILES) * (tm // 8)
    last8 = x_lat.shape[0] // 8 - 1
    tab_idx = lambda t: (jnp.where(t < CTX_TILES, 0, 1 + (t - CTX_TILES) % LAT_TILES_PER_SEQ), 0)
    small = [sp[k] for k in ('b_q_norm', 'b_kv_norm', 'w_uq', 'c_w_gate', 'c_b_gate')] + [_lane_repeat_matrix()]
    small += [sp[k] for k in ('d_mu', 'd_w0', 'd_w2', 'd_a0', 'd_a2', 'd_g2', 'd_k_k', 'd_k_a', 'd_r_k')]
    small += [_head_block_diag()]
    const = lambda a: _full(a.shape) if a.dtype == BF16 else _layer_item(a, layer)
    in_specs = ([_ctx_rows_spec(D_MODEL), _lat_rows_spec(D_MODEL),
                 pl.BlockSpec((8, D_MODEL), lambda t: (jnp.clip(lat8(t) - 1, 0, last8), 0)),
                 pl.BlockSpec((8, D_MODEL), lambda t: (jnp.clip(lat8(t + 1), 0, last8), 0)),
                 _mod_spec(layer),
                 _layer_block(w_small, layer)]
                + [pl.BlockSpec((tm, tab.shape[1]), tab_idx) for tab in tables]
                + [const(a) for a in small])
    return pl.pallas_call(
        _pre_kernel,
        grid=(N_TILES,),
        in_specs=in_specs,
        out_specs=[pl.BlockSpec((tm, w), lambda t: (t, 0)) for w in PRE_OUT_WIDTHS],
        out_shape=[jax.ShapeDtypeStruct((N_TOK, w), F32) for w in PRE_OUT_WIDTHS],
        compiler_params=pltpu.CompilerParams(dimension_semantics=("parallel",), vmem_limit_bytes=VMEM_LIMIT),
        name="mixer_prelude",
    )(x_ctx, x_lat, x_lat, x_lat, mod, w_small, *tables, *small)


ATT_Q_BLOCK = 128
MLA_Q_BLOCK = 256
MLA_LATENT_Q_BLOCK = 512
ATT_WINDOW = 128
ATT_NEG_INF = -1e30
CACHE_LEN = 512


def _softmax_pv(s, v, sink):
    dv = v.shape[1] // 2
    m = jnp.max(s, axis=-1, keepdims=True)
    if sink is not None:
        m = jnp.maximum(m, sink)
    e = jnp.exp((s - m).astype(BF16))
    o = jnp.dot(e, v, preferred_element_type=F32)
    den = o[:, dv:dv + 1]
    if sink is not None:
        den = den + jnp.exp(sink - m)
    return o[:, :dv] / den


def _with_ones(v):
    return jnp.concatenate([v.astype(BF16), jnp.ones(v.shape, BF16)], axis=1)


def _gqa_kernel(sink_ref, q_ref, k_ref, v_ref, *rest, hd, group, scale, windowed, sink_base):
    if windowed:
        kp_ref, kn_ref, vp_ref, vn_ref, kc_ref, vc_ref, _, o_ref = rest
    else:
        (o_ref,) = rest
    i = pl.program_id(1)
    tq = q_ref.shape[0]
    n_kv = k_ref.shape[1] // hd
    if windowed:
        qpos = i * tq + lax.broadcasted_iota(jnp.int32, (tq, 3 * tq), 0)
        kpos = (i - 1) * tq + lax.broadcasted_iota(jnp.int32, (tq, 3 * tq), 1)
        n_tok = pl.num_programs(1) * tq
        mask = (jnp.abs(qpos - kpos) <= ATT_WINDOW) & (kpos >= 0) & (kpos < n_tok)
        mask = jnp.concatenate([mask] * group, axis=0)
    for kvh in range(n_kv):
        ks = slice(kvh * hd, (kvh + 1) * hd)
        qs = [q_ref[:, (kvh * group + g) * hd:(kvh * group + g + 1) * hd] for g in range(group)]
        q = (jnp.concatenate(qs, axis=0) * scale).astype(BF16)
        sink = jnp.concatenate(
            [jnp.full((tq, 1), sink_ref[sink_base + kvh * group + g], F32) for g in range(group)], axis=0)
        if windowed:
            k_win = jnp.concatenate([kp_ref[:, ks], k_ref[:, ks], kn_ref[:, ks]], axis=0)
            v_win = jnp.concatenate([vp_ref[:, ks], v_ref[:, ks], vn_ref[:, ks]], axis=0)
            s_win = _bdot(q, k_win.astype(BF16), _D2T)
            s_win = jnp.where(mask, s_win, ATT_NEG_INF)
            s_ctx = _bdot(q, kc_ref[0, 0, kvh].astype(BF16), _D2T)
            s = jnp.concatenate([s_win, s_ctx], axis=1)
            v = jnp.concatenate([v_win, vc_ref[0, 0, kvh]], axis=0)
        else:
            s = _bdot(q, k_ref[:, ks].astype(BF16), _D2T)
            v = v_ref[:, ks]
        o = _softmax_pv(s, _with_ones(v), sink)
        for g in range(group):
            h = kvh * group + g
            o_ref[:, h * hd:(h + 1) * hd] = o[g * tq:(g + 1) * tq]


def gqa_attention(q, k, v, sink, cache_k, cache_v, layer):
    qw, kw = q.shape[1], k.shape[1]
    group = qw // kw
    scale = A_HD ** -0.5
    params = pltpu.CompilerParams(dimension_semantics=("parallel", "parallel"))
    out_shape = jax.ShapeDtypeStruct((N_TOK, qw), F32)
    ctx_spec = lambda w: pl.BlockSpec((CTX_LEN, w), lambda s, i, sk: (s, 0))
    o = pl.pallas_call(
        functools.partial(_gqa_kernel, hd=A_HD, group=group, scale=scale, windowed=False, sink_base=layer * A_HEADS),
        grid_spec=pltpu.PrefetchScalarGridSpec(
            num_scalar_prefetch=1, grid=(N_CTX_SEQ, 1), in_specs=[ctx_spec(qw), ctx_spec(kw), ctx_spec(kw)],
            out_specs=ctx_spec(qw)),
        out_shape=out_shape, compiler_params=params, name="gqa_full",
    )(sink, q, k, v)
    tq = ATT_Q_BLOCK
    nb = LAT_LEN // tq
    base = N_CTX // tq
    blk = lambda w, f: pl.BlockSpec((tq, w), lambda b, i, sk: (base + nb * b + f(i), 0))
    same = lambda i: i
    prev = lambda i: jnp.maximum(i - 1, 0)
    nxt = lambda i: jnp.minimum(i + 1, nb - 1)
    cspec = pl.BlockSpec((1, 1) + cache_k.shape[2:], lambda b, i, sk: (b, layer, 0, 0, 0))
    return pl.pallas_call(
        functools.partial(_gqa_kernel, hd=A_HD, group=group, scale=scale, windowed=True, sink_base=layer * A_HEADS),
        grid_spec=pltpu.PrefetchScalarGridSpec(
            num_scalar_prefetch=1, grid=(N_LAT_SEQ, nb),
            in_specs=[blk(qw, same), blk(kw, same), blk(kw, same), blk(kw, prev), blk(kw, nxt), blk(kw, prev),
                      blk(kw, nxt), cspec, cspec, pl.BlockSpec(memory_space=pl.ANY)],
            out_specs=blk(qw, same)),
        out_shape=out_shape, input_output_aliases={10: 0}, compiler_params=params, name="gqa_windowed",
    )(sink, q, k, v, k, k, v, v, cache_k, cache_v, o)


def _mla_kernel(q_ref, ckv_ref, kpe_ref, wukv_ref, *rest, n_heads, nope, rope, vd, scale, cached):
    if cached:
        cckv_ref, ckpe_ref, _, o_ref, k_scr, vext_scr = rest
    else:
        o_ref, k_scr, vext_scr = rest
    i = pl.program_id(1)
    n_cache = k_scr.shape[0] - ckv_ref.shape[0]
    hw = nope + vd
    hl = MLA_HEAD_LANES

    @pl.when(i == 0)
    def _():
        w = wukv_ref[...].astype(BF16)

        def expand(rows, kpe_rows, lo, hi):
            kv = jnp.dot(rows.astype(BF16), w, preferred_element_type=F32).astype(BF16)
            n = hi - lo
            kpe = kpe_rows.astype(BF16)
            for h in range(n_heads):
                k_scr[lo:hi, hl * h:hl * (h + 1)] = jnp.concatenate(
                    [kv[:, h * hw:h * hw + nope], kpe, jnp.zeros((n, hl - nope - rope), BF16)], axis=1)
                vext_scr[lo:hi, 2 * vd * h:2 * vd * (h + 1)] = jnp.concatenate(
                    [kv[:, h * hw + nope:(h + 1) * hw], jnp.ones((n, vd), BF16)], axis=1)

        if cached:
            expand(cckv_ref[0, 0], ckpe_ref[0, 0], 0, n_cache)
        expand(ckv_ref[...], kpe_ref[...], n_cache, k_scr.shape[0])

    for h in range(n_heads):
        qh = (q_ref[:, hl * h:hl * (h + 1)] * scale).astype(BF16)
        s = _bdot(qh, k_scr[:, hl * h:hl * (h + 1)], _D2T)
        o_ref[:, h * vd:(h + 1) * vd] = _softmax_pv(s, vext_scr[:, 2 * vd * h:2 * vd * (h + 1)], None)


def mla_attention(q, ckv, kpe, w_ukv, cache_ckv, cache_kpe, layer):
    qw = q.shape[1]
    tq = MLA_Q_BLOCK
    kw = dict(n_heads=B_HEADS, nope=B_NOPE, rope=B_ROPE, vd=B_VD, scale=(B_NOPE + B_ROPE) ** -0.5)
    params = pltpu.CompilerParams(dimension_semantics=("parallel", "arbitrary"))
    out_shape = jax.ShapeDtypeStruct((N_TOK, B_HEADS * B_VD), F32)
    scratch = lambda rows: [pltpu.VMEM((rows, B_HEADS * MLA_HEAD_LANES), BF16),
                            pltpu.VMEM((rows, 2 * B_HEADS * B_VD), BF16)]
    nbc = CTX_LEN // tq
    o = pl.pallas_call(
        functools.partial(_mla_kernel, cached=False, **kw),
        grid=(N_CTX_SEQ, nbc),
        in_specs=[pl.BlockSpec((tq, qw), lambda s, i: (s * nbc + i, 0)),
                  pl.BlockSpec((CTX_LEN, B_KV_LORA), lambda s, i: (s, 0)),
                  pl.BlockSpec((CTX_LEN, B_ROPE), lambda s, i: (s, 0)),
                  _layer_item(w_ukv, layer)],
        out_specs=pl.BlockSpec((tq, B_HEADS * B_VD), lambda s, i: (s * nbc + i, 0)),
        out_shape=out_shape,
        scratch_shapes=scratch(CTX_LEN),
        compiler_params=params, name="mla_context",
    )(q, ckv, kpe, w_ukv)
    tq = MLA_LATENT_Q_BLOCK
    nb = LAT_LEN // tq
    base = N_CTX // tq
    lat0 = N_CTX // LAT_LEN
    s_len = CACHE_LEN + LAT_LEN
    return pl.pallas_call(
        functools.partial(_mla_kernel, cached=True, **kw),
        grid=(N_LAT_SEQ, nb),
        in_specs=[pl.BlockSpec((tq, qw), lambda b, i: (base + nb * b + i, 0)),
                  pl.BlockSpec((LAT_LEN, B_KV_LORA), lambda b, i: (lat0 + b, 0)),
                  pl.BlockSpec((LAT_LEN, B_ROPE), lambda b, i: (lat0 + b, 0)),
                  _layer_item(w_ukv, layer),
                  pl.BlockSpec((1, 1, CACHE_LEN, B_KV_LORA), lambda b, i: (b, layer, 0, 0)),
                  pl.BlockSpec((1, 1, CACHE_LEN, B_ROPE), lambda b, i: (b, layer, 0, 0)),
                  pl.BlockSpec(memory_space=pl.ANY)],
        out_specs=pl.BlockSpec((tq, B_HEADS * B_VD), lambda b, i: (base + nb * b + i, 0)),
        out_shape=out_shape, input_output_aliases={6: 0},
        scratch_shapes=scratch(s_len),
        compiler_params=params, name="mla_latent",
    )(q, ckv, kpe, w_ukv, cache_ckv, cache_kpe, o)


CHUNK = 64
GLA_SUB = 16
PAIR = 2
N_CHAIN = PAIR * 2 * 4


def _is_back(shape):
    return (lax.broadcasted_iota(jnp.int32, shape, 0) // 4) % 2 == 1


def _chains(ref_f, ref_b, width):
    return jnp.stack([ref[0, s, 0, :, h * width:(h + 1) * width]
                      for s in range(PAIR) for ref in (ref_f, ref_b) for h in range(4)], axis=0)


def _unchain(y, o_f, o_b):
    for s in range(PAIR):
        o_f[0, s, 0] = jnp.concatenate([y[s * 8 + h] for h in range(4)], axis=-1)
        o_b[0, s, 0] = jnp.concatenate([y[s * 8 + 4 + h] for h in range(4)], axis=-1)


def _dir_masks(L):
    shape = (N_CHAIN, L, L)
    back = _is_back(shape)
    row = lax.broadcasted_iota(jnp.int32, shape, 1)
    col = lax.broadcasted_iota(jnp.int32, shape, 2)
    ahead = jnp.where(back, col - row, row - col)
    return ahead >= 0, ahead > 0, row == col


def _chunk_end(ci):
    L = ci.shape[1]
    return jnp.where(_is_back((N_CHAIN, 1, 1)), ci[:, 0:1], ci[:, L - 1:L])


def _split_refs(refs, n_in, has_s0, has_sfin):
    ins = refs[:n_in]
    pos = n_in
    s0_ref = None
    if has_s0:
        s0_ref = refs[pos]
        pos += 3
    of_ref, ob_ref = refs[pos], refs[pos + 1]
    pos += 2
    sfin_ref = refs[pos] if has_sfin else None
    return ins, s0_ref, of_ref, ob_ref, sfin_ref, refs[-1]


def _init_state(s_scr, s0_ref):
    @pl.when(pl.program_id(1) == 0)
    def _():
        if s0_ref is None:
            s_scr[...] = jnp.zeros_like(s_scr)
        else:
            s_scr[...] = s0_ref[0]


def _emit_state(sfin_ref, s_new):
    if sfin_ref is None:
        return

    @pl.when(pl.program_id(1) == pl.num_programs(1) - 1)
    def _():
        sfin_ref[0] = s_new


def _rwkv_kernel(*refs, dot, has_s0, has_sfin):
    (rf, rb, vf, vb, kkf, kkb, lwf, lwb, kf, kb, af, ab), s0_ref, yf_ref, yb_ref, sfin_ref, s_scr = _split_refs(
        refs, 12, has_s0, has_sfin)
    _init_state(s_scr, s0_ref)
    n = D_N
    r = _chains(rf, rb, n)
    v = _chains(vf, vb, n)
    kk = _chains(kkf, kkb, n)
    lw = _chains(lwf, lwb, n)
    k = _chains(kf, kb, n)
    a = _chains(af, ab, n)
    L = r.shape[1]
    S = s_scr[...]
    incl, strict, diag = _dir_masks(L)
    ci = _dot_exact_lhs(jnp.where(incl, 1.0, 0.0), lw, _NN)
    ce = ci - lw
    cl = _chunk_end(ci)
    e_neg = jnp.exp(-ci)
    b = a * kk
    alpha = kk * jnp.exp(ce)
    rho = r * jnp.exp(ci)
    beta = b * e_neg
    kappa = k * e_neg
    e_end = jnp.exp(cl - ci)
    ar = jnp.concatenate([alpha, rho], axis=1)
    bk = jnp.concatenate([beta, kappa], axis=1)
    w = dot(ar, bk, _NT)
    nmat = jnp.where(strict, w[:, :L, :L], 0.0)
    mmat = jnp.where(strict, w[:, :L, L:], 0.0)
    p1 = jnp.where(incl, w[:, L:, :L], 0.0)
    p2 = jnp.where(incl, w[:, L:, L:], 0.0)
    x = jnp.where(diag, 1.0, 0.0) - nmat
    p = dot(nmat, nmat, _NN)
    span = 2
    while True:
        x = x + dot(x, p, _NN)
        span *= 2
        if span >= L:
            break
        p = dot(p, p, _NN)
    us = dot(ar, S, _NT)
    rhs = us[:, :L] + dot(mmat, v, _NN)
    d = -dot(x, rhs, _NN)
    dv = jnp.concatenate([d, v], axis=1)
    pp = jnp.concatenate([p1, p2], axis=2)
    _unchain(us[:, L:] + dot(pp, dv, _NN), yf_ref, yb_ref)
    bk_end = jnp.concatenate([b * e_end, k * e_end], axis=1)
    s_new = S * jnp.exp(cl) + dot(dv, bk_end, _TN)
    s_scr[...] = s_new
    _emit_state(sfin_ref, s_new)


def _gla_kernel(*refs, dot, has_s0, has_sfin):
    (qf, qb, kf, kb, vf, vb, laf, lab), s0_ref, of_ref, ob_ref, sfin_ref, s_scr = _split_refs(
        refs, 8, has_s0, has_sfin)
    _init_state(s_scr, s0_ref)
    q4 = _chains(qf, qb, 128)
    k4 = _chains(kf, kb, 128)
    la4 = _chains(laf, lab, 128)
    v = _chains(vf, vb, C_DV)
    g, L, lanes = q4.shape
    dk = C_DK
    n_sub = L // GLA_SUB
    st = s_scr[...]
    incl, _, _ = _dir_masks(L)
    c = _dot_exact_lhs(jnp.where(incl, 1.0, 0.0), la4, _NN)
    shape = (g, L, lanes)
    back = _is_back(shape)
    lane_blk = lax.broadcasted_iota(jnp.int32, shape, 2) // dk
    row_blk = lax.broadcasted_iota(jnp.int32, shape, 1) // GLA_SUB
    cref_f = jnp.zeros(shape, F32)
    cref_b = jnp.zeros(shape, F32)
    for j in range(1, n_sub):
        cref_f = jnp.where(lane_blk == j, c[:, j * GLA_SUB - 1:j * GLA_SUB], cref_f)
        cref_b = jnp.where(lane_blk == j - 1, c[:, j * GLA_SUB:j * GLA_SUB + 1], cref_b)
    cref = jnp.where(back, cref_b, cref_f)
    q_on = row_blk == lane_blk
    k_on = jnp.where(back, row_blk - lane_blk, lane_blk - row_blk) >= 0
    qh = jnp.where(q_on, q4 * jnp.exp(jnp.where(q_on, c - cref, 0.0)), 0.0)
    kh = jnp.where(k_on, k4 * jnp.exp(jnp.where(k_on, cref - c, 0.0)), 0.0)
    att = jnp.where(incl, dot(qh, kh, _NT), 0.0)
    cl = _chunk_end(c)
    qe = (q4 * jnp.exp(c))[:, :, :dk]
    ke = (k4 * jnp.exp(cl - c))[:, :, :dk]
    _unchain(dot(qe, st, _NT) + dot(att, v, _NN), of_ref, ob_ref)
    s_new = st * jnp.exp(cl[:, :, :dk]) + dot(v, ke, _TN)
    s_scr[...] = s_new
    _emit_state(sfin_ref, s_new)


def _recurrence_calls(kernel_fn, name, pairs, singles_f, singles_b, s0_lat, state_dims, out_width):
    def run(view, grid, group, s0, prev_out):
        nc = view[2]
        fwd_map = lambda p, c: (group(p), 0, c, 0, 0)
        bwd_map = lambda p, c: (group(p), 0, nc - 1 - c, 0, 0)
        blk = lambda w: (1, PAIR, 1, CHUNK, w)
        args, in_specs = [], []
        for af, ab in [(a, a) for a in pairs] + list(zip(singles_f, singles_b)):
            w = af.shape[-1]
            args += [af.reshape(view + (w,)), ab.reshape(view + (w,))]
            in_specs += [pl.BlockSpec(blk(w), fwd_map), pl.BlockSpec(blk(w), bwd_map)]
        out_specs = [pl.BlockSpec(blk(out_width), fwd_map), pl.BlockSpec(blk(out_width), bwd_map)]
        out_shape = [jax.ShapeDtypeStruct(view + (out_width,), F32)] * 2
        aliases = {}
        if s0 is not None:
            args += [s0] + [o.reshape(view + (out_width,)) for o in prev_out]
            in_specs += [_full(s0.shape), pl.BlockSpec(memory_space=pl.ANY), pl.BlockSpec(memory_space=pl.ANY)]
            aliases = {len(args) - 2: 0, len(args) - 1: 1}
        else:
            out_specs.append(pl.BlockSpec((1, N_CHAIN) + state_dims, lambda p, c: (p, 0, 0, 0)))
            out_shape.append(jax.ShapeDtypeStruct((grid[0], N_CHAIN) + state_dims, F32))
        return pl.pallas_call(
            functools.partial(kernel_fn, has_s0=s0 is not None, has_sfin=s0 is None),
            grid=grid, in_specs=in_specs, out_specs=out_specs, out_shape=out_shape,
            input_output_aliases=aliases, scratch_shapes=[pltpu.VMEM((N_CHAIN,) + state_dims, F32)],
            compiler_params=pltpu.CompilerParams(dimension_semantics=("parallel", "arbitrary")),
            name=name + ("_latent" if s0 is not None else "_context"),
        )(*args)

    ctx_nc = CTX_LEN // CHUNK
    ctx_view = (N_TOK // (PAIR * CTX_LEN), PAIR, ctx_nc, CHUNK)
    o_f, o_b, s_fin = run(ctx_view, (N_CTX_SEQ // PAIR, ctx_nc), lambda p: p, None, None)
    lat_nc = LAT_LEN // CHUNK
    lat_view = (N_TOK // (PAIR * LAT_LEN), PAIR, lat_nc, CHUNK)
    o_f, o_b = run(lat_view, (1, lat_nc), lambda p: N_CTX // (PAIR * LAT_LEN), s0_lat, (o_f, o_b))
    return o_f.reshape(N_TOK, out_width), o_b.reshape(N_TOK, out_width), s_fin


def _layer_norm(x, g, b):
    mu = jnp.mean(x, axis=-1, keepdims=True)
    xc = x - mu
    var = jnp.mean(xc * xc, axis=-1, keepdims=True)
    return xc * lax.rsqrt(var + LN_EPS) * g + b


def _merge_kernel(xc_ref, xl_ref, mod_ref, oa_ref, ob_ref, cof_ref, cob_ref, cgate_ref, yf_ref, yb_ref, bonus_ref,
                  dgate_ref, wg_ref, wbr_ref, wout_ref, cnorm_ref, dlng_ref, dlnb_ref, lng_ref, lnb_ref, wr_ref, br_ref,
                  bd_ref, x1_o, h2_o, topi_o, topw_o):
    x = _tile_rows(pl.program_id(0), xc_ref, xl_ref)
    m = mod_ref[0]
    sh1, sc1, g1, sh2, sc2 = m[0:1], m[1:2], m[2:3], m[3:4], m[4:5]
    bd = bd_ref[...]
    inv_n = 1.0 / D_N
    co = cof_ref[...] + cob_ref[...]
    o_c = co * lax.rsqrt(_dot_exact_rhs(co * co, bd) * inv_n + RMS_EPS) * cnorm_ref[...] * cgate_ref[...]
    y = yf_ref[...] + yb_ref[...]
    yc = y - _dot_exact_rhs(y, bd) * inv_n
    var = _dot_exact_rhs(yc * yc, bd) * inv_n
    o_d = (yc * lax.rsqrt(var + D_GN_EPS) * dlng_ref[...] + dlnb_ref[...] + bonus_ref[...]) * dgate_ref[...]
    branches = [b.astype(BF16) for b in (oa_ref[...], ob_ref[...], o_c, o_d)]
    h = (x * (1.0 + sc1) + sh1).astype(BF16)
    blocks = []
    for cb in range(D_MODEL // MXU_WIDTH):
        merged = None
        for n in range(N_BRANCH):
            cols = slice(n * D_MODEL + cb * MXU_WIDTH, n * D_MODEL + (cb + 1) * MXU_WIDTH)
            gate = _sigmoid(jnp.dot(h, wg_ref[0, :, cols], preferred_element_type=F32))
            term = gate * jnp.dot(branches[n], wbr_ref[0, n, :, cb * MXU_WIDTH:(cb + 1) * MXU_WIDTH],
                                  preferred_element_type=F32)
            merged = term if merged is None else merged + term
        blocks.append(merged.astype(BF16))
    mix = jnp.dot(jnp.concatenate(blocks, axis=1), wout_ref[0], preferred_element_type=F32)
    x1 = _layer_norm(ALPHA * x + g1 * mix, lng_ref[...], lnb_ref[...])
    x1_o[...] = x1
    h2 = x1 * (1.0 + sc2) + sh2
    h2_o[...] = h2.astype(BF16)
    logits = _dot3(h2, wr_ref[...]) + br_ref[...]
    tm, n_e = logits.shape
    lane_e = lax.broadcasted_iota(jnp.int32, (tm, n_e), 1)
    lane_o = lax.broadcasted_iota(jnp.int32, (tm, topi_o.shape[1]), 1)
    top_i = jnp.zeros((tm, topi_o.shape[1]), jnp.int32)
    top_v = jnp.zeros((tm, topw_o.shape[1]), F32)
    vals = []
    for kth in range(TOP_K):
        mx = jnp.max(logits, axis=-1, keepdims=True)
        idx = jnp.min(jnp.where(logits == mx, lane_e, n_e), axis=-1, keepdims=True)
        vals.append(mx)
        top_i = jnp.where(lane_o == kth, idx, top_i)
        logits = jnp.where(lane_e == idx, -jnp.inf, logits)
    es = [jnp.exp(vk - vals[0]) for vk in vals]
    den = es[0] + es[1] + es[2] + es[3]
    for kth in range(TOP_K):
        top_v = jnp.where(lane_o == kth, es[kth] / den, top_v)
    topi_o[...] = top_i
    topw_o[...] = top_v


def merge_and_route(x_ctx, x_lat, mod, o_a, o_b, co_f, co_b, cgate, y_f, y_b, bonus, dgate, w_g, w_br, w_out, layer, sp):
    tm = MERGE_ROW_TILE
    hn = D_HEADS * D_N
    row = lambda w: pl.BlockSpec((tm, w), lambda t: (t, 0))
    small = [sp[k] for k in ('c_norm', 'd_ln_g', 'd_ln_b', 'ln_g', 'ln_b', 'w_router', 'b_router')]
    index = [layer, layer, layer, 2 * layer, 2 * layer, layer, layer]
    bd = _head_block_diag()
    return pl.pallas_call(
        _merge_kernel,
        grid=(N_TOK // tm,),
        in_specs=([_ctx_rows_spec(D_MODEL, tm), _lat_rows_spec(D_MODEL, tm), _mod_spec(layer, tm)]
                  + [row(hn)] * 9 + [_layer_block(w, layer) for w in (w_g, w_br, w_out)]
                  + [_layer_item(a, i) for a, i in zip(small, index)] + [_full(bd.shape)]),
        out_specs=[row(D_MODEL), row(D_MODEL), row(128), row(128)],
        out_shape=[jax.ShapeDtypeStruct((N_TOK, D_MODEL), F32), jax.ShapeDtypeStruct((MOE_ROWS, D_MODEL), BF16),
                   jax.ShapeDtypeStruct((N_TOK, 128), jnp.int32), jax.ShapeDtypeStruct((N_TOK, 128), F32)],
        compiler_params=pltpu.CompilerParams(dimension_semantics=("parallel",), vmem_limit_bytes=VMEM_LIMIT),
        name="merge_and_route",
    )(x_ctx, x_lat, mod, o_a, o_b, co_f, co_b, cgate, y_f, y_b, bonus, dgate, w_g, w_br, w_out, *small, bd)


def _moe_kernel(te_ref, tv_ref, first_ref, slot_ref, next_ref, x_ref, w1_hbm, b1_ref, w2_hbm, b2_ref, perm_ref, *rest,
                layer):
    y_ref, w1buf, w2buf, sem, w1s, w2s, hs = rest[-7:]
    t = pl.program_id(0)
    valid = tv_ref[t] != 0
    d_model, two_f = w1s.shape
    n_blk = two_f // MXU_WIDTH
    half = MXU_WIDTH // 2

    def fetch(expert, slot):
        return (pltpu.make_async_copy(w1_hbm.at[layer, expert], w1buf.at[slot], sem.at[0, slot]),
                pltpu.make_async_copy(w2_hbm.at[layer, expert], w2buf.at[slot], sem.at[1, slot]))

    @pl.when(t == 0)
    def _():
        for cp in fetch(te_ref[0], 0):
            cp.start()

    @pl.when(first_ref[t] == 1)
    def _():
        slot = slot_ref[t]
        for cp in fetch(te_ref[t], slot):
            cp.wait()

        @pl.when(next_ref[t] >= 0)
        def _():
            for cp in fetch(next_ref[t], 1 - slot):
                cp.start()

        for blk in range(n_blk):
            sl = slice(blk * MXU_WIDTH, (blk + 1) * MXU_WIDTH)
            wb = w1buf[slot, :, sl].astype(BF16)
            w1s[:, sl] = jnp.dot(wb, perm_ref[...], preferred_element_type=F32).astype(BF16)
        w2s[...] = w2buf[slot].astype(BF16)

    @pl.when(valid)
    def _():
        x = x_ref[...]
        for blk in range(n_blk):
            sl = slice(blk * MXU_WIDTH, (blk + 1) * MXU_WIDTH)
            u = jnp.dot(x, w1s[:, sl], preferred_element_type=F32) + b1_ref[0, 0, :, sl]
            glu = jnp.minimum(u[:, :half], SWIGLU_LIMIT)
            lin = jnp.clip(u[:, half:], -SWIGLU_LIMIT, SWIGLU_LIMIT)
            hs[:, blk * half:(blk + 1) * half] = (glu * _sigmoid(SWIGLU_ALPHA * glu) * (lin + 1.0)).astype(BF16)
        y = jnp.dot(hs[...], w2s[...], preferred_element_type=F32) + b2_ref[0, 0]
        y_ref[...] = y.astype(y_ref.dtype)

    @pl.when(jnp.logical_not(valid))
    def _():
        y_ref[...] = jnp.zeros_like(y_ref)


def _deinterleave_perm():
    half = MXU_WIDTH // 2
    src = np.arange(MXU_WIDTH)
    dst = np.where(src % 2 == 0, src // 2, half + src // 2)
    p = np.zeros((MXU_WIDTH, MXU_WIDTH), np.float32)
    p[src, dst] = 1.0
    return jnp.asarray(p, BF16)


def _moe_dispatch(top_i):
    n, k = top_i.shape
    tm = MOE_ROW_TILE
    p_rows = n * k + N_EXPERTS * tm
    experts = jnp.arange(N_EXPERTS, dtype=jnp.int32)
    onehot = top_i[:, :, None] == experts
    sel = jnp.sum(onehot.astype(jnp.int32), axis=1)
    before = jnp.cumsum(sel, axis=0) - sel
    counts = jnp.sum(sel, axis=0)
    padded = ((counts + tm - 1) // tm) * tm
    ends = jnp.cumsum(padded)
    starts = ends - padded
    pos = jnp.sum(jnp.where(onehot, (before + starts)[:, None, :], 0), axis=-1)
    n_tiles = p_rows // tm
    tile_start = jnp.arange(n_tiles, dtype=jnp.int32) * tm
    tile_valid = (tile_start < ends[-1]).astype(jnp.int32)
    last_tile = ends[-1] // tm - 1
    tile_expert = jnp.sum(ends[None, :] <= jnp.minimum(tile_start, last_tile * tm)[:, None], axis=1).astype(jnp.int32)
    keys = jnp.sort((top_i * n + jnp.arange(n, dtype=jnp.int32)[:, None]).reshape(-1))
    tile_onehot = tile_expert[:, None] == experts[None, :]
    lookup = lambda table: jnp.sum(jnp.where(tile_onehot, table[None, :], 0), axis=1)
    tile_rank0 = tile_start - lookup(starts)
    rank = tile_rank0[:, None] + jnp.arange(tm, dtype=jnp.int32)[None, :]
    sorted_at = jnp.clip(lookup(jnp.cumsum(counts) - counts)[:, None] + rank, 0, n * k - 1)
    tile_keys = keys[sorted_at.reshape(-1)].reshape(n_tiles, tm)
    filler = (tile_start[:, None] + jnp.arange(tm, dtype=jnp.int32)[None, :]) % n
    src_tok = jnp.where(rank < lookup(counts)[:, None], tile_keys % n, filler)
    tables = []
    for lo, hi in zip(MOE_SPLIT_TILES[:-1], MOE_SPLIT_TILES[1:]):
        nst = hi - lo
        idx = jnp.arange(nst, dtype=jnp.int32)
        te_h = tile_expert[lo:hi]
        is_first = jnp.concatenate([jnp.ones((1,), jnp.int32), (te_h[1:] != te_h[:-1]).astype(jnp.int32)])
        slot = (jnp.cumsum(is_first) - 1) % 2
        nxt = jnp.min(jnp.where(jnp.logical_and(idx[None, :] > idx[:, None], is_first[None, :] == 1),
                                idx[None, :], nst), axis=1)
        next_expert = jnp.sum(jnp.where(idx[None, :] == nxt[:, None], te_h[None, :] + 1, 0), axis=1) - 1
        tables.append((te_h, tile_valid[lo:hi], is_first, slot.astype(jnp.int32), next_expert.astype(jnp.int32)))
    return pos, src_tok.reshape(-1), tables, p_rows


def moe_experts(h2, top_i, layer, w1, b1, w2, b2):
    n = top_i.shape[0]
    d = h2.shape[1]
    depth, e, _, two_f = w1.shape
    f = two_f // 2
    tm = MOE_ROW_TILE
    pos, src_tok, tables, p_rows = _moe_dispatch(top_i)
    assert h2.shape[0] == p_rows
    src_tok = lax.optimization_barrier(src_tok)
    b1p = b1.reshape(depth, e, two_f // MXU_WIDTH, MXU_WIDTH // 2, 2).swapaxes(3, 4).reshape(depth, e, 1, two_f)
    b2r = b2.reshape(depth, e, 1, d)
    expert_vec = lambda w: pl.BlockSpec((1, 1, 1, w), lambda t, te, *_: (layer, te[t], 0, 0))
    spans = list(zip(MOE_SPLIT_TILES[:-1], MOE_SPLIT_TILES[1:]))
    assert MOE_SPLIT_TILES[0] == 0 and MOE_SPLIT_TILES[-1] * tm == p_rows
    xs = [h2.at[src_tok[lo * tm:hi * tm]].get(mode="promise_in_bounds") for lo, hi in spans]
    ys = None
    for h, (lo, hi) in enumerate(spans):
        in_specs = [
            pl.BlockSpec((tm, d), lambda t, *_: (t, 0)),
            pl.BlockSpec(memory_space=pl.ANY),
            expert_vec(two_f),
            pl.BlockSpec(memory_space=pl.ANY),
            expert_vec(d),
            pl.BlockSpec((MXU_WIDTH, MXU_WIDTH), lambda t, *_: (0, 0)),
        ]
        args = [*tables[h], xs[h], w1, b1p, w2, b2r, _deinterleave_perm()]
        aliases = {}
        if ys is not None:
            in_specs.append(pl.BlockSpec(memory_space=pl.ANY))
            aliases = {len(args): 0}
            args.append(ys)
        ys = pl.pallas_call(
            functools.partial(_moe_kernel, layer=layer),
            grid_spec=pltpu.PrefetchScalarGridSpec(
                num_scalar_prefetch=len(tables[h]), grid=(hi - lo,), in_specs=in_specs,
                out_specs=pl.BlockSpec((tm, d), lambda t, *_, lo=lo: (lo + t, 0)),
                scratch_shapes=[pltpu.VMEM((2, d, two_f), F32), pltpu.VMEM((2, f, d), F32),
                                pltpu.SemaphoreType.DMA((2, 2)),
                                pltpu.VMEM((d, two_f), BF16), pltpu.VMEM((f, d), BF16), pltpu.VMEM((tm, f), BF16)]),
            out_shape=jax.ShapeDtypeStruct((p_rows, d), BF16),
            input_output_aliases=aliases,
            compiler_params=pltpu.CompilerParams(dimension_semantics=("arbitrary",),
                                                 vmem_limit_bytes=48 * 1024 * 1024),
            name="moe_experts",
        )(*args)
    return ys, pos


def _final_kernel(x1_ref, mod_ref, ys_ref, topw_ref, lng_ref, lnb_ref, o_ref):
    g2 = mod_ref[0, 5:6]
    moe = None
    for kth in range(TOP_K):
        term = ys_ref[kth].astype(F32) * topw_ref[:, kth:kth + 1]
        moe = term if moe is None else moe + term
    o_ref[...] = _layer_norm(ALPHA * x1_ref[...] + g2 * moe, lng_ref[...], lnb_ref[...])


def combine_and_norm(x1, mod, ys, pos, top_w, layer, sp):
    tm = ROW_TILE
    ln_g, ln_b = sp['ln_g'], sp['ln_b']
    outs = []
    for t0, n_rows in ((0, N_CTX), (CTX_TILES, N_TOK - N_CTX)):
        idx = lax.optimization_barrier(pos[t0 * tm:t0 * tm + n_rows].T.reshape(-1))
        rows = ys.at[idx].get(mode="promise_in_bounds").reshape(TOP_K, n_rows, D_MODEL)
        outs.append(pl.pallas_call(
            _final_kernel,
            grid=(n_rows // tm,),
            in_specs=[pl.BlockSpec((tm, D_MODEL), lambda t, t0=t0: (t0 + t, 0)),
                      pl.BlockSpec((1, 6, D_MODEL),
                                   lambda t, t0=t0: (layer * (1 + N_LAT_SEQ) + _mod_row(t0 + t), 0, 0)),
                      pl.BlockSpec((TOP_K, tm, D_MODEL), lambda t: (0, t, 0)),
                      pl.BlockSpec((tm, 128), lambda t, t0=t0: (t0 + t, 0)),
                      _layer_item(ln_g, 2 * layer + 1), _layer_item(ln_b, 2 * layer + 1)],
            out_specs=pl.BlockSpec((tm, D_MODEL), lambda t: (t, 0)),
            out_shape=jax.ShapeDtypeStruct((n_rows, D_MODEL), F32),
            compiler_params=pltpu.CompilerParams(dimension_semantics=("parallel",)),
            name="combine_and_norm",
        )(x1, mod, rows, top_w, ln_g, ln_b))
    return outs


def kernel(x_prompt, x_sample, cache_a_k, cache_a_v, cache_b_ckv, cache_b_kpe, state_c, state_d, c,
           c_ctx, w_mod, b_mod, w_in, a_sink, b_q_norm, b_w_uq, b_kv_norm, b_w_ukv, c_w_gate, c_b_gate,
           c_norm, d_mu, d_w0, d_w2, d_a0, d_a2, d_g2, d_k_k, d_k_a, d_r_k, d_ln_g, d_ln_b, w_br, w_out,
           ln_g, ln_b, w_router, b_router, w_mlp1, b_mlp1, w_mlp2, b_mlp2):
    sp = stacked_params(dict(
        a_sink=a_sink, b_q_norm=b_q_norm, b_w_uq=b_w_uq, b_kv_norm=b_kv_norm, b_w_ukv=b_w_ukv, c_w_gate=c_w_gate,
        c_b_gate=c_b_gate, c_norm=c_norm, d_mu=d_mu, d_w0=d_w0, d_w2=d_w2, d_a0=d_a0, d_a2=d_a2, d_g2=d_g2,
        d_k_k=d_k_k, d_k_a=d_k_a, d_r_k=d_r_k, d_ln_g=d_ln_g, d_ln_b=d_ln_b, ln_g=ln_g, ln_b=ln_b,
        w_router=w_router, b_router=b_router))
    assert x_prompt.shape == (N_CTX_SEQ, CTX_LEN, D_MODEL) and x_sample.shape == (N_LAT_SEQ, LAT_LEN, D_MODEL)
    x_ctx, x_lat = x_prompt.reshape(N_CTX, D_MODEL), x_sample.reshape(-1, D_MODEL)
    cond8 = jnp.concatenate([c_ctx[None], c, jnp.zeros((8 - 1 - N_LAT_SEQ, D_MODEL), F32)], axis=0)
    mod = modulation_table(cond8, w_mod, b_mod)[:, :1 + N_LAT_SEQ].reshape(DEPTH * (1 + N_LAT_SEQ), 6, D_MODEL)
    tables = _rope_tables()
    w_small, w_g = prepare_in_weights(w_in)
    w_br_bf, w_out_bf = w_br.astype(BF16), w_out.astype(BF16)
    new = {name: [] for name in ("a_k", "a_v", "b_ckv", "b_kpe", "c", "d")}
    for l in range(DEPTH):
        (aq, ak, av, bq, bckv, bkpe, cq4, ck4, cla_f, cla_b, cv, cgate,
         r, v, kk, lw_f, lw_b, k_f, k_b, a_f, a_b, bonus, dgate) = mixer_prelude(
             x_ctx, x_lat, mod, w_small, l, tables, sp)

        o_a = gqa_attention(aq, ak, av, sp['a_sink'], cache_a_k, cache_a_v, l)
        o_b = mla_attention(bq, bckv, bkpe, sp['b_w_ukv'], cache_b_ckv, cache_b_kpe, l)

        c_s0 = jnp.swapaxes(state_c[:, l], 3, 4).reshape(1, N_CHAIN, C_DV, C_DK)
        co_f, co_b, c_fin = _recurrence_calls(functools.partial(_gla_kernel, dot=_dot1), "gla", [cq4, ck4, cv],
                                              [cla_f], [cla_b], c_s0, (C_DV, C_DK), C_HEADS * C_DV)
        d_s0 = state_d[:, l].reshape(1, N_CHAIN, D_N, D_N)
        y_f, y_b, d_fin = _recurrence_calls(functools.partial(_rwkv_kernel, dot=_dot1), "rwkv7", [r, v, kk],
                                            [lw_f, k_f, a_f], [lw_b, k_b, a_b], d_s0, (D_N, D_N), D_HEADS * D_N)

        x1, h2, top_i, top_w = merge_and_route(x_ctx, x_lat, mod, o_a, o_b, co_f, co_b, cgate, y_f, y_b, bonus, dgate,
                                               w_g, w_br_bf, w_out_bf, l, sp)
        ys, pos = moe_experts(h2, top_i[:, :TOP_K], l, w_mlp1, b_mlp1, w_mlp2, b_mlp2)
        x_ctx, x_lat = combine_and_norm(x1, mod, ys, pos, top_w, l, sp)

        new["a_k"].append(ak[:N_CTX].reshape(N_CTX_SEQ, CTX_LEN, A_KV_HEADS, A_HD).transpose(0, 2, 1, 3))
        new["a_v"].append(av[:N_CTX].reshape(N_CTX_SEQ, CTX_LEN, A_KV_HEADS, A_HD).transpose(0, 2, 1, 3))
        new["b_ckv"].append(bckv[:N_CTX].reshape(N_CTX_SEQ, CTX_LEN, B_KV_LORA))
        new["b_kpe"].append(bkpe[:N_CTX].reshape(N_CTX_SEQ, CTX_LEN, B_ROPE))
        new["c"].append(jnp.swapaxes(c_fin.reshape(N_CTX_SEQ, 2, C_HEADS, C_DV, C_DK), 3, 4))
        new["d"].append(d_fin.reshape(N_CTX_SEQ, 2, D_HEADS, D_N, D_N))
    y_prompt = x_ctx.reshape(x_prompt.shape)
    y_sample = x_lat.reshape(x_sample.shape)
    return (y_prompt, y_sample, *(jnp.stack(new[name], axis=1) for name in ("a_k", "a_v", "b_ckv", "b_kpe", "c", "d")))
```

```python
import functools

import jax
import jax.numpy as jnp
import numpy as np
from jax import lax
from jax.experimental import pallas as pl
from jax.experimental.pallas import tpu as pltpu

F32 = jnp.float32
BF16 = jnp.bfloat16

MXU_WIDTH = 256
VMEM_LIMIT = 56 * 1024 * 1024

D_MODEL = 1024
DEPTH = 2
GRID_W = 64
ROPE_BASE = 10000.0
A_HEADS, A_KV_HEADS, A_HD = 4, 2, 64
B_HEADS, B_NOPE, B_ROPE, B_VD, B_Q_LORA, B_KV_LORA = 4, 64, 32, 64, 192, 128
C_HEADS, C_DK, C_DV, C_GATE_RANK, C_GATE_TEMP = 4, 32, 64, 16, 16.0
D_HEADS, D_N, D_DECAY_RANK, D_AAA_RANK, D_GATE_RANK, D_GN_EPS = 4, 64, 64, 64, 128, 64e-5
BRANCH_W = 256
N_BRANCH = 4
N_EXPERTS = 32
TOP_K = 4
SWIGLU_LIMIT = 7.0
SWIGLU_ALPHA = 1.702
ALPHA = (2 * DEPTH) ** 0.25
LN_EPS = 1e-5
RMS_EPS = 1e-6

N_CTX_SEQ, CTX_LEN = 16, 256
N_LAT_SEQ, LAT_LEN = 2, 2048
N_CTX = N_CTX_SEQ * CTX_LEN
N_TOK = N_CTX + N_LAT_SEQ * LAT_LEN
ROW_TILE = 256
N_TILES = N_TOK // ROW_TILE
CTX_TILES = N_CTX // ROW_TILE
LAT_TILES_PER_SEQ = LAT_LEN // ROW_TILE
MLA_HEAD_LANES = 128
MERGE_ROW_TILE = 512
MOE_ROW_TILE = 256
MOE_ROWS = N_TOK * TOP_K + N_EXPERTS * MOE_ROW_TILE
MOE_SPLIT_TILES = (0, 48, MOE_ROWS // MOE_ROW_TILE)

_ORIG = dict(aq=(0, 256), ak=(256, 384), av=(384, 512), bcq=(512, 704), bckv=(704, 832), bkpe=(832, 864),
             cq=(864, 992), ck=(992, 1120), cv=(1120, 1376), cog=(1376, 1632), caf=(1632, 1648), cab=(1648, 1664),
             zd=(1664, 2816))
_ORDER = ("aq", "ak", "av", "cq", "ck", "cv", "cog", "zd", "bcq", "caf", "cab", "bkpe", "bckv")
COL = {}
_off = 0
for _name in _ORDER:
    _w = _ORIG[_name][1] - _ORIG[_name][0]
    COL[_name] = (_off, _off + _w)
    _off += _w
SMALL_COLS = _off
G_START = SMALL_COLS


def _cs(name):
    return slice(*COL[name])


def _split2(x):
    hi = x.astype(BF16)
    lo = (x - hi.astype(F32)).astype(BF16)
    return hi, lo


def _bdot(a, b, dims):
    return lax.dot_general(a, b, dims, preferred_element_type=F32)


_D2 = (((1,), (0,)), ((), ()))
_D2T = (((1,), (1,)), ((), ()))
_NN = (((2,), (1,)), ((0,), (0,)))
_NT = (((2,), (2,)), ((0,), (0,)))
_TN = (((1,), (1,)), ((0,), (0,)))


def _dot1(a, b, dims=_D2):
    return _bdot(a.astype(BF16), b.astype(BF16), dims)


def _dot3(a, b, dims=_D2):
    ah, al = _split2(a)
    bh, bl = _split2(b)
    return _bdot(ah, bh, dims) + (_bdot(ah, bl, dims) + _bdot(al, bh, dims))


def _dot_exact_lhs(a01, b, dims=_D2):
    a = a01.astype(BF16)
    h, l = _split2(b)
    return _bdot(a, h, dims) + _bdot(a, l, dims)


def _dot_exact_rhs(a, b01, dims=_D2):
    b = b01.astype(BF16)
    h, l = _split2(a)
    return _bdot(h, b, dims) + _bdot(l, b, dims)


def _sigmoid(x):
    return 0.5 * jnp.tanh(0.5 * x) + 0.5


def _softplus(x):
    return jnp.maximum(x, 0.0) + jnp.log(1.0 + jnp.exp(-jnp.abs(x)))


def _mod_row(t, tile=ROW_TILE):
    return jnp.where(t < N_CTX // tile, 0, 1 + (t - N_CTX // tile) // (LAT_LEN // tile))


def _full(shape):
    nd = len(shape)
    return pl.BlockSpec(shape, lambda *_: (0,) * nd)


def _ctx_rows_spec(width, tile=ROW_TILE):
    return pl.BlockSpec((tile, width), lambda t: (jnp.minimum(t, N_CTX // tile - 1), 0))


def _lat_rows_spec(width, tile=ROW_TILE):
    return pl.BlockSpec((tile, width), lambda t: (jnp.maximum(t - N_CTX // tile, 0), 0))


def _tile_rows(t, ctx_ref, lat_ref):
    return jnp.where(t < N_CTX // ctx_ref.shape[0], ctx_ref[...], lat_ref[...])


MOD_COL_TILE = 1536


def _mod_kernel(c_ref, w_ref, b_ref, o_ref):
    c = c_ref[...]
    o_ref[0] = _dot3(c * _sigmoid(c), w_ref[0]) + b_ref[0]


def modulation_table(cond8, w_mod, b_mod):
    depth, d, six_d = w_mod.shape
    return pl.pallas_call(
        _mod_kernel,
        grid=(depth, six_d // MOD_COL_TILE),
        in_specs=[pl.BlockSpec((8, d), lambda l, j: (0, 0)),
                  pl.BlockSpec((1, d, MOD_COL_TILE), lambda l, j: (l, 0, j)),
                  pl.BlockSpec((1, 1, MOD_COL_TILE), lambda l, j: (l, 0, j))],
        out_specs=pl.BlockSpec((1, 8, MOD_COL_TILE), lambda l, j: (l, 0, j)),
        out_shape=jax.ShapeDtypeStruct((depth, 8, six_d), F32),
        compiler_params=pltpu.CompilerParams(dimension_semantics=("parallel", "parallel")),
        name="modulation",
    )(cond8, w_mod, b_mod.reshape(depth, 1, six_d))


WPREP_ROWS = 128


def _wprep_kernel(w_ref, small_ref, gate_ref):
    for name in _ORDER:
        lo, hi = _ORIG[name]
        small_ref[0, :, _cs(name)] = w_ref[0, :, lo:hi].astype(BF16)
    gate_ref[0] = w_ref[0, :, G_START:].astype(BF16)


def prepare_in_weights(w_in):
    depth, d, cols = w_in.shape
    return pl.pallas_call(
        _wprep_kernel,
        grid=(depth, d // WPREP_ROWS),
        in_specs=[pl.BlockSpec((1, WPREP_ROWS, cols), lambda l, r: (l, r, 0))],
        out_specs=[pl.BlockSpec((1, WPREP_ROWS, SMALL_COLS), lambda l, r: (l, r, 0)),
                   pl.BlockSpec((1, WPREP_ROWS, cols - G_START), lambda l, r: (l, r, 0))],
        out_shape=[jax.ShapeDtypeStruct((depth, d, SMALL_COLS), BF16),
                   jax.ShapeDtypeStruct((depth, d, cols - G_START), BF16)],
        compiler_params=pltpu.CompilerParams(dimension_semantics=("parallel", "parallel")),
        name="prepare_in_weights",
    )(w_in)


def _rot_pairs(x, half, lane_mod_base=0):
    w = x.shape[-1]
    lane = lax.broadcasted_iota(jnp.int32, (1, w), 1) - lane_mod_base
    first = (lane % (2 * half)) < half
    return jnp.where(first, -pltpu.roll(x, w - half, axis=1), pltpu.roll(x, half, axis=1))


def _pre_kernel(xc_ref, xl_ref, xp_ref, xn_ref, mod_ref, w_ref, ca_ref, sa_ref, cb_ref, sb_ref, ck_ref, sk_ref,
                qnorm_ref, kvnorm_ref, wuq_ref, cwg_ref, cbg_ref, rep_ref, mu_ref, dw0_ref, dw2_ref, da0_ref,
                da2_ref, dg2_ref, dkk_ref, dka_ref, drk_ref, bd_ref,
                aq_o, ak_o, av_o, bq_o, bckv_o, bkpe_o, cq4_o, ck4_o, claf_o, clab_o, cv_o, cgate_o,
                r_o, v_o, kk_o, lwf_o, lwb_o, kf_o, kb_o, af_o, ab_o, bonus_o, dgate_o):
    t = pl.program_id(0)
    tm = xc_ref.shape[0]
    sh1 = mod_ref[0, 0:1, :]
    sc1 = mod_ref[0, 1:2, :]

    def modulate(xv):
        return (xv * (1.0 + sc1) + sh1).astype(BF16)

    x_tile = _tile_rows(t, xc_ref, xl_ref)
    h_all = jnp.concatenate([modulate(x_tile), modulate(xp_ref[...]), modulate(xn_ref[...])], axis=0)
    z_all = jnp.dot(h_all, w_ref[0], preferred_element_type=F32)
    z = z_all[:tm]

    aq = z[:, _cs("aq")]
    ak = z[:, _cs("ak")]
    aq_o[...] = aq * ca_ref[...] + _rot_pairs(aq, A_HD // 4) * sa_ref[...]
    ak_o[...] = ak * ca_ref[:, :ak.shape[1]] + _rot_pairs(ak, A_HD // 4) * sa_ref[:, :ak.shape[1]]
    av_o[...] = z[:, _cs("av")]

    bcq = z[:, _cs("bcq")]
    qn = bcq * lax.rsqrt(jnp.mean(bcq * bcq, axis=-1, keepdims=True) + RMS_EPS) * qnorm_ref[...]
    bq = _dot1(qn, wuq_ref[...])
    bq_o[...] = bq * cb_ref[...] + _rot_pairs(bq, B_ROPE // 4, lane_mod_base=B_NOPE) * sb_ref[...]
    bckv = z[:, _cs("bckv")]
    bckv_o[...] = bckv * lax.rsqrt(jnp.mean(bckv * bckv, axis=-1, keepdims=True) + RMS_EPS) * kvnorm_ref[...]
    kpe_lo = COL["bkpe"][0] // 128 * 128
    kblk = z[:, kpe_lo:kpe_lo + 128]
    kblk = kblk * ck_ref[...] + _rot_pairs(kblk, B_ROPE // 4) * sk_ref[...]
    bkpe_o[...] = kblk[:, COL["bkpe"][0] - kpe_lo:COL["bkpe"][1] - kpe_lo]

    rep = rep_ref[...]
    cq4_o[...] = _dot1(z[:, _cs("cq")] * (C_DK ** -0.5), rep)
    ck4_o[...] = _dot1(z[:, _cs("ck")], rep)
    cv_o[...] = z[:, _cs("cv")]
    cog = z[:, _cs("cog")]
    cgate_o[...] = cog * _sigmoid(cog)
    for direction, (name, out) in enumerate((("caf", claf_o), ("cab", clab_o))):
        pre = _dot1(z[:, _cs(name)], cwg_ref[direction]) + cbg_ref[direction]
        la_hi, la_lo = _split2(-_softplus(-pre) * (1.0 / C_GATE_TEMP))
        out[...] = _bdot(la_hi, rep, _D2) + _bdot(la_lo, rep, _D2)

    zd_cols = _cs("zd")
    zd = z[:, zd_cols]
    j = (t - CTX_TILES) % LAT_TILES_PER_SEQ
    latent = t >= CTX_TILES
    has_prev = jnp.logical_and(latent, j != 0)
    has_next = jnp.logical_and(latent, j != LAT_TILES_PER_SEQ - 1)
    prev_row = jnp.where(has_prev, z_all[tm + 7:tm + 8, zd_cols], 0.0)
    next_row = jnp.where(has_next, z_all[tm + 8:tm + 9, zd_cols], 0.0)
    row = lax.broadcasted_iota(jnp.int32, (tm, 1), 0)
    up = jnp.where(row == 0, prev_row, pltpu.roll(zd, 1, axis=0))
    dn = jnp.where(row == tm - 1, next_row, pltpu.roll(zd, tm - 1, axis=0))
    zd = zd + (0.5 * (up + dn) - zd) * mu_ref[...]

    hn = D_HEADS * D_N
    d_r, d_k, d_v = zd[:, :hn], zd[:, hn:2 * hn], zd[:, 2 * hn:3 * hn]
    o = 3 * hn
    d_w = (zd[:, o:o + D_DECAY_RANK], zd[:, o + D_DECAY_RANK:o + 2 * D_DECAY_RANK])
    o += 2 * D_DECAY_RANK
    d_a = (zd[:, o:o + D_AAA_RANK], zd[:, o + D_AAA_RANK:o + 2 * D_AAA_RANK])
    o += 2 * D_AAA_RANK
    d_g = zd[:, o:o + D_GATE_RANK]
    bd = bd_ref[...]
    kk = d_k * dkk_ref[...]
    kk = kk / jnp.maximum(jnp.sqrt(_dot_exact_rhs(kk * kk, bd)), 1e-12)
    r_o[...] = d_r
    v_o[...] = d_v
    kk_o[...] = kk
    k_sum = None
    for direction, (lw_o, k_o, a_o) in enumerate(((lwf_o, kf_o, af_o), (lwb_o, kb_o, ab_o))):
        w_log = -_softplus(-(dw0_ref[direction] + _dot1(jnp.tanh(d_w[direction]), dw2_ref[direction]))) - 0.5
        lw_o[...] = -jnp.exp(w_log)
        a = _sigmoid(da0_ref[direction] + _dot1(d_a[direction], da2_ref[direction]))
        k_dir = d_k * (1.0 + (a - 1.0) * dka_ref[...])
        k_o[...] = k_dir
        a_o[...] = a
        k_sum = k_dir if k_sum is None else k_sum + k_dir
    bonus_o[...] = d_v * _dot_exact_rhs(d_r * drk_ref[...] * k_sum, bd)
    dgate_o[...] = _dot1(_sigmoid(d_g), dg2_ref[...])


def _rope_tables():
    pos = np.arange(LAT_LEN)
    rowp, colp = (pos // GRID_W).astype(np.float32), (pos % GRID_W).astype(np.float32)

    f32 = np.float32

    def head_tables(rot_dim):
        quarter = rot_dim // 4
        inv = (f32(ROPE_BASE) ** (-np.arange(quarter, dtype=f32) / f32(quarter))).astype(f32)
        ar = (rowp[:, None] * inv).astype(f32)
        ac = (colp[:, None] * inv).astype(f32)
        cos = np.concatenate([np.cos(ar), np.cos(ar), np.cos(ac), np.cos(ac)], axis=-1).astype(f32)
        sin = np.concatenate([np.sin(ar), np.sin(ar), np.sin(ac), np.sin(ac)], axis=-1).astype(f32)
        return cos, sin

    def with_identity(c, s):
        w = c.shape[1]
        return (jnp.asarray(np.concatenate([np.ones((ROW_TILE, w), f32), c], axis=0)),
                jnp.asarray(np.concatenate([np.zeros((ROW_TILE, w), f32), s], axis=0)))

    ca, sa = head_tables(A_HD)
    ca, sa = with_identity(np.tile(ca, (1, A_HEADS)), np.tile(sa, (1, A_HEADS)))
    cbh, sbh = head_tables(B_ROPE)
    ones, zeros = np.ones((LAT_LEN, B_NOPE), f32), np.zeros((LAT_LEN, B_NOPE), f32)
    qpad = MLA_HEAD_LANES - B_NOPE - B_ROPE
    cb, sb = with_identity(
        np.tile(np.concatenate([ones, cbh, np.ones((LAT_LEN, qpad), f32)], axis=1), (1, B_HEADS)),
        np.tile(np.concatenate([zeros, sbh, np.zeros((LAT_LEN, qpad), f32)], axis=1), (1, B_HEADS)))
    pad = 128 - B_ROPE
    ck, sk = with_identity(np.concatenate([np.ones((LAT_LEN, pad), f32), cbh], axis=1),
                           np.concatenate([np.zeros((LAT_LEN, pad), f32), sbh], axis=1))
    return ca, sa, cb, sb, ck, sk


def _lane_repeat_matrix():
    m = np.zeros((C_HEADS * C_DK, C_HEADS * 128), np.float32)
    for h in range(C_HEADS):
        for g in range(128 // C_DK):
            for d in range(C_DK):
                m[h * C_DK + d, h * 128 + g * C_DK + d] = 1.0
    return jnp.asarray(m, BF16)


def _head_block_diag():
    m = np.kron(np.eye(D_HEADS, dtype=np.float32), np.ones((D_N, D_N), np.float32))
    return jnp.asarray(m, BF16)


PRE_OUT_WIDTHS = (256, 128, 128, B_HEADS * MLA_HEAD_LANES, 128, 32, 512, 512, 512, 512, 256, 256) + (256,) * 11


def _layer_block(arr, layer):
    nd = arr.ndim
    return pl.BlockSpec((1,) + arr.shape[1:], lambda *_: (layer,) + (0,) * (nd - 1))


def _layer_item(arr, index):
    nd = arr.ndim
    return pl.BlockSpec((None,) + arr.shape[1:], lambda *_: (index,) + (0,) * (nd - 1))


def _mod_spec(layer, tile=ROW_TILE):
    return pl.BlockSpec((1, 6, D_MODEL), lambda t: (layer * (1 + N_LAT_SEQ) + _mod_row(t, tile), 0, 0))


def stacked_params(p):
    depth = p['d_mu'].shape[0]
    hn = D_HEADS * D_N
    w_uq = p['b_w_uq'].reshape(depth, B_Q_LORA, B_HEADS, B_NOPE + B_ROPE)
    w_uq = jnp.pad(w_uq, ((0, 0), (0, 0), (0, 0), (0, MLA_HEAD_LANES - B_NOPE - B_ROPE)))
    row = lambda a: a.reshape(depth, 1, -1)
    return dict(
        b_q_norm=row(p['b_q_norm']), b_kv_norm=row(p['b_kv_norm']), w_uq=w_uq.reshape(depth, B_Q_LORA, -1),
        c_w_gate=p['c_w_gate'], c_b_gate=p['c_b_gate'].reshape(depth, 2, 1, -1), d_mu=row(p['d_mu']),
        d_w0=p['d_w0'].reshape(depth, 2, 1, hn), d_w2=p['d_w2'], d_a0=p['d_a0'].reshape(depth, 2, 1, hn),
        d_a2=p['d_a2'], d_g2=p['d_g2'], d_k_k=row(p['d_k_k']), d_k_a=row(p['d_k_a']), d_r_k=row(p['d_r_k']),
        c_norm=row(jnp.tile(p['c_norm'], (1, C_HEADS))), d_ln_g=row(p['d_ln_g']), d_ln_b=row(p['d_ln_b']),
        ln_g=p['ln_g'].reshape(depth * 2, 1, -1), ln_b=p['ln_b'].reshape(depth * 2, 1, -1),
        w_router=p['w_router'], b_router=row(p['b_router']), a_sink=p['a_sink'].reshape(-1), b_w_ukv=p['b_w_ukv'])


def mixer_prelude(x_ctx, x_lat, mod, w_small, layer, tables, sp):
    tm = ROW_TILE
    lat8 = lambda t: (t - CTX_TILES) * (tm // 8)
    last8 = x_lat.shape[0] // 8 - 1
    tab_idx = lambda t: (jnp.where(t < CTX_TILES, 0, 1 + (t - CTX_TILES) % LAT_TILES_PER_SEQ), 0)
    small = [sp[k] for k in ('b_q_norm', 'b_kv_norm', 'w_uq', 'c_w_gate', 'c_b_gate')] + [_lane_repeat_matrix()]
    small += [sp[k] for k in ('d_mu', 'd_w0', 'd_w2', 'd_a0', 'd_a2', 'd_g2', 'd_k_k', 'd_k_a', 'd_r_k')]
    small += [_head_block_diag()]
    const = lambda a: _full(a.shape) if a.dtype == BF16 else _layer_item(a, layer)
    in_specs = ([_ctx_rows_spec(D_MODEL), _lat_rows_spec(D_MODEL),
                 pl.BlockSpec((8, D_MODEL), lambda t: (jnp.clip(lat8(t) - 1, 0, last8), 0)),
                 pl.BlockSpec((8, D_MODEL), lambda t: (jnp.clip(lat8(t + 1), 0, last8), 0)),
                 _mod_spec(layer),
                 _layer_block(w_small, layer)]
                + [pl.BlockSpec((tm, tab.shape[1]), tab_idx) for tab in tables]
                + [const(a) for a in small])
    return pl.pallas_call(
        _pre_kernel,
        grid=(N_TILES,),
        in_specs=in_specs,
        out_specs=[pl.BlockSpec((tm, w), lambda t: (t, 0)) for w in PRE_OUT_WIDTHS],
        out_shape=[jax.ShapeDtypeStruct((N_TOK, w), F32) for w in PRE_OUT_WIDTHS],
        compiler_params=pltpu.CompilerParams(dimension_semantics=("parallel",), vmem_limit_bytes=VMEM_LIMIT),
        name="mixer_prelude",
    )(x_ctx, x_lat, x_lat, x_lat, mod, w_small, *tables, *small)


ATT_Q_BLOCK = 128
MLA_Q_BLOCK = 256
MLA_LATENT_Q_BLOCK = 512
ATT_WINDOW = 128
ATT_NEG_INF = -1e30
CACHE_LEN = 512


def _softmax_pv(s, v, sink):
    dv = v.shape[1] // 2
    m = jnp.max(s, axis=-1, keepdims=True)
    if sink is not None:
        m = jnp.maximum(m, sink)
    e = jnp.exp((s - m).astype(BF16))
    o = jnp.dot(e, v, preferred_element_type=F32)
    den = o[:, dv:dv + 1]
    if sink is not None:
        den = den + jnp.exp(sink - m)
    return o[:, :dv] / den


def _with_ones(v):
    return jnp.concatenate([v.astype(BF16), jnp.ones(v.shape, BF16)], axis=1)


def _gqa_kernel(sink_ref, q_ref, k_ref, v_ref, *rest, hd, group, scale, windowed, sink_base):
    if windowed:
        kp_ref, kn_ref, vp_ref, vn_ref, kc_ref, vc_ref, _, o_ref = rest
    else:
        (o_ref,) = rest
    i = pl.program_id(1)
    tq = q_ref.shape[0]
    n_kv = k_ref.shape[1] // hd
    if windowed:
        qpos = i * tq + lax.broadcasted_iota(jnp.int32, (tq, 3 * tq), 0)
        kpos = (i - 1) * tq + lax.broadcasted_iota(jnp.int32, (tq, 3 * tq), 1)
        n_tok = pl.num_programs(1) * tq
        mask = (jnp.abs(qpos - kpos) <= ATT_WINDOW) & (kpos >= 0) & (kpos < n_tok)
        mask = jnp.concatenate([mask] * group, axis=0)
    for kvh in range(n_kv):
        ks = slice(kvh * hd, (kvh + 1) * hd)
        qs = [q_ref[:, (kvh * group + g) * hd:(kvh * group + g + 1) * hd] for g in range(group)]
        q = (jnp.concatenate(qs, axis=0) * scale).astype(BF16)
        sink = jnp.concatenate(
            [jnp.full((tq, 1), sink_ref[sink_base + kvh * group + g], F32) for g in range(group)], axis=0)
        if windowed:
            k_win = jnp.concatenate([kp_ref[:, ks], k_ref[:, ks], kn_ref[:, ks]], axis=0)
            v_win = jnp.concatenate([vp_ref[:, ks], v_ref[:, ks], vn_ref[:, ks]], axis=0)
            s_win = _bdot(q, k_win.astype(BF16), _D2T)
            s_win = jnp.where(mask, s_win, ATT_NEG_INF)
            s_ctx = _bdot(q, kc_ref[0, 0, kvh].astype(BF16), _D2T)
            s = jnp.concatenate([s_win, s_ctx], axis=1)
            v = jnp.concatenate([v_win, vc_ref[0, 0, kvh]], axis=0)
        else:
            s = _bdot(q, k_ref[:, ks].astype(BF16), _D2T)
            v = v_ref[:, ks]
        o = _softmax_pv(s, _with_ones(v), sink)
        for g in range(group):
            h = kvh * group + g
            o_ref[:, h * hd:(h + 1) * hd] = o[g * tq:(g + 1) * tq]


def gqa_attention(q, k, v, sink, cache_k, cache_v, layer):
    qw, kw = q.shape[1], k.shape[1]
    group = qw // kw
    scale = A_HD ** -0.5
    params = pltpu.CompilerParams(dimension_semantics=("parallel", "parallel"))
    out_shape = jax.ShapeDtypeStruct((N_TOK, qw), F32)
    ctx_spec = lambda w: pl.BlockSpec((CTX_LEN, w), lambda s, i, sk: (s, 0))
    o = pl.pallas_call(
        functools.partial(_gqa_kernel, hd=A_HD, group=group, scale=scale, windowed=False, sink_base=layer * A_HEADS),
        grid_spec=pltpu.PrefetchScalarGridSpec(
            num_scalar_prefetch=1, grid=(N_CTX_SEQ, 1), in_specs=[ctx_spec(qw), ctx_spec(kw), ctx_spec(kw)],
            out_specs=ctx_spec(qw)),
        out_shape=out_shape, compiler_params=params, name="gqa_full",
    )(sink, q, k, v)
    tq = ATT_Q_BLOCK
    nb = LAT_LEN // tq
    base = N_CTX // tq
    blk = lambda w, f: pl.BlockSpec((tq, w), lambda b, i, sk: (base + nb * b + f(i), 0))
    same = lambda i: i
    prev = lambda i: jnp.maximum(i - 1, 0)
    nxt = lambda i: jnp.minimum(i + 1, nb - 1)
    cspec = pl.BlockSpec((1, 1) + cache_k.shape[2:], lambda b, i, sk: (b, layer, 0, 0, 0))
    return pl.pallas_call(
        functools.partial(_gqa_kernel, hd=A_HD, group=group, scale=scale, windowed=True, sink_base=layer * A_HEADS),
        grid_spec=pltpu.PrefetchScalarGridSpec(
            num_scalar_prefetch=1, grid=(N_LAT_SEQ, nb),
            in_specs=[blk(qw, same), blk(kw, same), blk(kw, same), blk(kw, prev), blk(kw, nxt), blk(kw, prev),
                      blk(kw, nxt), cspec, cspec, pl.BlockSpec(memory_space=pl.ANY)],
            out_specs=blk(qw, same)),
        out_shape=out_shape, input_output_aliases={10: 0}, compiler_params=params, name="gqa_windowed",
    )(sink, q, k, v, k, k, v, v, cache_k, cache_v, o)


def _mla_kernel(q_ref, ckv_ref, kpe_ref, wukv_ref, *rest, n_heads, nope, rope, vd, scale, cached):
    if cached:
        cckv_ref, ckpe_ref, _, o_ref, k_scr, vext_scr = rest
    else:
        o_ref, k_scr, vext_scr = rest
    i = pl.program_id(1)
    n_cache = k_scr.shape[0] - ckv_ref.shape[0]
    hw = nope + vd
    hl = MLA_HEAD_LANES

    @pl.when(i == 0)
    def _():
        w = wukv_ref[...].astype(BF16)

        def expand(rows, kpe_rows, lo, hi):
            kv = jnp.dot(rows.astype(BF16), w, preferred_element_type=F32).astype(BF16)
            n = hi - lo
            kpe = kpe_rows.astype(BF16)
            for h in range(n_heads):
                k_scr[lo:hi, hl * h:hl * (h + 1)] = jnp.concatenate(
                    [kv[:, h * hw:h * hw + nope], kpe, jnp.zeros((n, hl - nope - rope), BF16)], axis=1)
                vext_scr[lo:hi, 2 * vd * h:2 * vd * (h + 1)] = jnp.concatenate(
                    [kv[:, h * hw + nope:(h + 1) * hw], jnp.ones((n, vd), BF16)], axis=1)

        if cached:
            expand(cckv_ref[0, 0], ckpe_ref[0, 0], 0, n_cache)
        expand(ckv_ref[...], kpe_ref[...], n_cache, k_scr.shape[0])

    for h in range(n_heads):
        qh = (q_ref[:, hl * h:hl * (h + 1)] * scale).astype(BF16)
        s = _bdot(qh, k_scr[:, hl * h:hl * (h + 1)], _D2T)
        o_ref[:, h * vd:(h + 1) * vd] = _softmax_pv(s, vext_scr[:, 2 * vd * h:2 * vd * (h + 1)], None)


def mla_attention(q, ckv, kpe, w_ukv, cache_ckv, cache_kpe, layer):
    qw = q.shape[1]
    tq = MLA_Q_BLOCK
    kw = dict(n_heads=B_HEADS, nope=B_NOPE, rope=B_ROPE, vd=B_VD, scale=(B_NOPE + B_ROPE) ** -0.5)
    params = pltpu.CompilerParams(dimension_semantics=("parallel", "arbitrary"))
    out_shape = jax.ShapeDtypeStruct((N_TOK, B_HEADS * B_VD), F32)
    scratch = lambda rows: [pltpu.VMEM((rows, B_HEADS * MLA_HEAD_LANES), BF16),
                            pltpu.VMEM((rows, 2 * B_HEADS * B_VD), BF16)]
    nbc = CTX_LEN // tq
    o = pl.pallas_call(
        functools.partial(_mla_kernel, cached=False, **kw),
        grid=(N_CTX_SEQ, nbc),
        in_specs=[pl.BlockSpec((tq, qw), lambda s, i: (s * nbc + i, 0)),
                  pl.BlockSpec((CTX_LEN, B_KV_LORA), lambda s, i: (s, 0)),
                  pl.BlockSpec((CTX_LEN, B_ROPE), lambda s, i: (s, 0)),
                  _layer_item(w_ukv, layer)],
        out_specs=pl.BlockSpec((tq, B_HEADS * B_VD), lambda s, i: (s * nbc + i, 0)),
        out_shape=out_shape,
        scratch_shapes=scratch(CTX_LEN),
        compiler_params=params, name="mla_context",
    )(q, ckv, kpe, w_ukv)
    tq = MLA_LATENT_Q_BLOCK
    nb = LAT_LEN // tq
    base = N_CTX // tq
    lat0 = N_CTX // LAT_LEN
    s_len = CACHE_LEN + LAT_LEN
    return pl.pallas_call(
        functools.partial(_mla_kernel, cached=True, **kw),
        grid=(N_LAT_SEQ, nb),
        in_specs=[pl.BlockSpec((tq, qw), lambda b, i: (base + nb * b + i, 0)),
                  pl.BlockSpec((LAT_LEN, B_KV_LORA), lambda b, i: (lat0 + b, 0)),
                  pl.BlockSpec((LAT_LEN, B_ROPE), lambda b, i: (lat0 + b, 0)),
                  _layer_item(w_ukv, layer),
                  pl.BlockSpec((1, 1, CACHE_LEN, B_KV_LORA), lambda b, i: (b, layer, 0, 0)),
                  pl.BlockSpec((1, 1, CACHE_LEN, B_ROPE), lambda b, i: (b, layer, 0, 0)),
                  pl.BlockSpec(memory_space=pl.ANY)],
        out_specs=pl.BlockSpec((tq, B_HEADS * B_VD), lambda b, i: (base + nb * b + i, 0)),
        out_shape=out_shape, input_output_aliases={6: 0},
        scratch_shapes=scratch(s_len),
        compiler_params=params, name="mla_latent",
    )(q, ckv, kpe, w_ukv, cache_ckv, cache_kpe, o)


CHUNK = 64
GLA_SUB = 16
PAIR = 2
N_CHAIN = PAIR * 2 * 4


def _is_back(shape):
    return (lax.broadcasted_iota(jnp.int32, shape, 0) // 4) % 2 == 1


def _chains(ref_f, ref_b, width):
    return jnp.stack([ref[0, s, 0, :, h * width:(h + 1) * width]
                      for s in range(PAIR) for ref in (ref_f, ref_b) for h in range(4)], axis=0)


def _unchain(y, o_f, o_b):
    for s in range(PAIR):
        o_f[0, s, 0] = jnp.concatenate([y[s * 8 + h] for h in range(4)], axis=-1)
        o_b[0, s, 0] = jnp.concatenate([y[s * 8 + 4 + h] for h in range(4)], axis=-1)


def _dir_masks(L):
    shape = (N_CHAIN, L, L)
    back = _is_back(shape)
    row = lax.broadcasted_iota(jnp.int32, shape, 1)
    col = lax.broadcasted_iota(jnp.int32, shape, 2)
    ahead = jnp.where(back, col - row, row - col)
    return ahead >= 0, ahead > 0, row == col


def _chunk_end(ci):
    L = ci.shape[1]
    return jnp.where(_is_back((N_CHAIN, 1, 1)), ci[:, 0:1], ci[:, L - 1:L])


def _split_refs(refs, n_in, has_s0, has_sfin):
    ins = refs[:n_in]
    pos = n_in
    s0_ref = None
    if has_s0:
        s0_ref = refs[pos]
        pos += 3
    of_ref, ob_ref = refs[pos], refs[pos + 1]
    pos += 2
    sfin_ref = refs[pos] if has_sfin else None
    return ins, s0_ref, of_ref, ob_ref, sfin_ref, refs[-1]


def _init_state(s_scr, s0_ref):
    @pl.when(pl.program_id(1) == 0)
    def _():
        if s0_ref is None:
            s_scr[...] = jnp.zeros_like(s_scr)
        else:
            s_scr[...] = s0_ref[0]


def _emit_state(sfin_ref, s_new):
    if sfin_ref is None:
        return

    @pl.when(pl.program_id(1) == pl.num_programs(1) - 1)
    def _():
        sfin_ref[0] = s_new


def _rwkv_kernel(*refs, dot, has_s0, has_sfin):
    (rf, rb, vf, vb, kkf, kkb, lwf, lwb, kf, kb, af, ab), s0_ref, yf_ref, yb_ref, sfin_ref, s_scr = _split_refs(
        refs, 12, has_s0, has_sfin)
    _init_state(s_scr, s0_ref)
    n = D_N
    r = _chains(rf, rb, n)
    v = _chains(vf, vb, n)
    kk = _chains(kkf, kkb, n)
    lw = _chains(lwf, lwb, n)
    k = _chains(kf, kb, n)
    a = _chains(af, ab, n)
    L = r.shape[1]
    S = s_scr[...]
    incl, strict, diag = _dir_masks(L)
    ci = _dot_exact_lhs(jnp.where(incl, 1.0, 0.0), lw, _NN)
    ce = ci - lw
    cl = _chunk_end(ci)
    e_neg = jnp.exp(-ci)
    b = a * kk
    alpha = kk * jnp.exp(ce)
    rho = r * jnp.exp(ci)
    beta = b * e_neg
    kappa = k * e_neg
    e_end = jnp.exp(cl - ci)
    ar = jnp.concatenate([alpha, rho], axis=1)
    bk = jnp.concatenate([beta, kappa], axis=1)
    w = dot(ar, bk, _NT)
    nmat = jnp.where(strict, w[:, :L, :L], 0.0)
    mmat = jnp.where(strict, w[:, :L, L:], 0.0)
    p1 = jnp.where(incl, w[:, L:, :L], 0.0)
    p2 = jnp.where(incl, w[:, L:, L:], 0.0)
    x = jnp.where(diag, 1.0, 0.0) - nmat
    p = dot(nmat, nmat, _NN)
    span = 2
    while True:
        x = x + dot(x, p, _NN)
        span *= 2
        if span >= L:
            break
        p = dot(p, p, _NN)
    us = dot(ar, S, _NT)
    rhs = us[:, :L] + dot(mmat, v, _NN)
    d = -dot(x, rhs, _NN)
    dv = jnp.concatenate([d, v], axis=1)
    pp = jnp.concatenate([p1, p2], axis=2)
    _unchain(us[:, L:] + dot(pp, dv, _NN), yf_ref, yb_ref)
    bk_end = jnp.concatenate([b * e_end, k * e_end], axis=1)
    s_new = S * jnp.exp(cl) + dot(dv, bk_end, _TN)
    s_scr[...] = s_new
    _emit_state(sfin_ref, s_new)


def _gla_kernel(*refs, dot, has_s0, has_sfin):
    (qf, qb, kf, kb, vf, vb, laf, lab), s0_ref, of_ref, ob_ref, sfin_ref, s_scr = _split_refs(
        refs, 8, has_s0, has_sfin)
    _init_state(s_scr, s0_ref)
    q4 = _chains(qf, qb, 128)
    k4 = _chains(kf, kb, 128)
    la4 = _chains(laf, lab, 128)
    v = _chains(vf, vb, C_DV)
    g, L, lanes = q4.shape
    dk = C_DK
    n_sub = L // GLA_SUB
    st = s_scr[...]
    incl, _, _ = _dir_masks(L)
    c = _dot_exact_lhs(jnp.where(incl, 1.0, 0.0), la4, _NN)
    shape = (g, L, lanes)
    back = _is_back(shape)
    lane_blk = lax.broadcasted_iota(jnp.int32, shape, 2) // dk
    row_blk = lax.broadcasted_iota(jnp.int32, shape, 1) // GLA_SUB
    cref_f = jnp.zeros(shape, F32)
    cref_b = jnp.zeros(shape, F32)
    for j in range(1, n_sub):
        cref_f = jnp.where(lane_blk == j, c[:, j * GLA_SUB - 1:j * GLA_SUB], cref_f)
        cref_b = jnp.where(lane_blk == j - 1, c[:, j * GLA_SUB:j * GLA_SUB + 1], cref_b)
    cref = jnp.where(back, cref_b, cref_f)
    q_on = row_blk == lane_blk
    k_on = jnp.where(back, row_blk - lane_blk, lane_blk - row_blk) >= 0
    qh = jnp.where(q_on, q4 * jnp.exp(jnp.where(q_on, c - cref, 0.0)), 0.0)
    kh = jnp.where(k_on, k4 * jnp.exp(jnp.where(k_on, cref - c, 0.0)), 0.0)
    att = jnp.where(incl, dot(qh, kh, _NT), 0.0)
    cl = _chunk_end(c)
    qe = (q4 * jnp.exp(c))[:, :, :dk]
    ke = (k4 * jnp.exp(cl - c))[:, :, :dk]
    _unchain(dot(qe, st, _NT) + dot(att, v, _NN), of_ref, ob_ref)
    s_new = st * jnp.exp(cl[:, :, :dk]) + dot(v, ke, _TN)
    s_scr[...] = s_new
    _emit_state(sfin_ref, s_new)


def _recurrence_calls(kernel_fn, name, pairs, singles_f, singles_b, s0_lat, state_dims, out_width):
    def run(view, grid, group, s0, prev_out):
        nc = view[2]
        fwd_map = lambda p, c: (group(p), 0, c, 0, 0)
        bwd_map = lambda p, c: (group(p), 0, nc - 1 - c, 0, 0)
        blk = lambda w: (1, PAIR, 1, CHUNK, w)
        args, in_specs = [], []
        for af, ab in [(a, a) for a in pairs] + list(zip(singles_f, singles_b)):
            w = af.shape[-1]
            args += [af.reshape(view + (w,)), ab.reshape(view + (w,))]
            in_specs += [pl.BlockSpec(blk(w), fwd_map), pl.BlockSpec(blk(w), bwd_map)]
        out_specs = [pl.BlockSpec(blk(out_width), fwd_map), pl.BlockSpec(blk(out_width), bwd_map)]
        out_shape = [jax.ShapeDtypeStruct(view + (out_width,), F32)] * 2
        aliases = {}
        if s0 is not None:
            args += [s0] + [o.reshape(view + (out_width,)) for o in prev_out]
            in_specs += [_full(s0.shape), pl.BlockSpec(memory_space=pl.ANY), pl.BlockSpec(memory_space=pl.ANY)]
            aliases = {len(args) - 2: 0, len(args) - 1: 1}
        else:
            out_specs.append(pl.BlockSpec((1, N_CHAIN) + state_dims, lambda p, c: (p, 0, 0, 0)))
            out_shape.append(jax.ShapeDtypeStruct((grid[0], N_CHAIN) + state_dims, F32))
        return pl.pallas_call(
            functools.partial(kernel_fn, has_s0=s0 is not None, has_sfin=s0 is None),
            grid=grid, in_specs=in_specs, out_specs=out_specs, out_shape=out_shape,
            input_output_aliases=aliases, scratch_shapes=[pltpu.VMEM((N_CHAIN,) + state_dims, F32)],
            compiler_params=pltpu.CompilerParams(dimension_semantics=("parallel", "arbitrary")),
            name=name + ("_latent" if s0 is not None else "_context"),
        )(*args)

    ctx_nc = CTX_LEN // CHUNK
    ctx_view = (N_TOK // (PAIR * CTX_LEN), PAIR, ctx_nc, CHUNK)
    o_f, o_b, s_fin = run(ctx_view, (N_CTX_SEQ // PAIR, ctx_nc), lambda p: p, None, None)
    lat_nc = LAT_LEN // CHUNK
    lat_view = (N_TOK // (PAIR * LAT_LEN), PAIR, lat_nc, CHUNK)
    o_f, o_b = run(lat_view, (1, lat_nc), lambda p: N_CTX // (PAIR * LAT_LEN), s0_lat, (o_f, o_b))
    return o_f.reshape(N_TOK, out_width), o_b.reshape(N_TOK, out_width), s_fin


def _layer_norm(x, g, b):
    mu = jnp.mean(x, axis=-1, keepdims=True)
    xc = x - mu
    var = jnp.mean(xc * xc, axis=-1, keepdims=True)
    return xc * lax.rsqrt(var + LN_EPS) * g + b


def _merge_kernel(xc_ref, xl_ref, mod_ref, oa_ref, ob_ref, cof_ref, cob_ref, cgate_ref, yf_ref, yb_ref, bonus_ref,
                  dgate_ref, wg_ref, wbr_ref, wout_ref, cnorm_ref, dlng_ref, dlnb_ref, lng_ref, lnb_ref, wr_ref, br_ref,
                  bd_ref, x1_o, h2_o, topi_o, topw_o):
    x = _tile_rows(pl.program_id(0), xc_ref, xl_ref)
    m = mod_ref[0]
    sh1, sc1, g1, sh2, sc2 = m[0:1], m[1:2], m[2:3], m[3:4], m[4:5]
    bd = bd_ref[...]
    inv_n = 1.0 / D_N
    co = cof_ref[...] + cob_ref[...]
    o_c = co * lax.rsqrt(_dot_exact_rhs(co * co, bd) * inv_n + RMS_EPS) * cnorm_ref[...] * cgate_ref[...]
    y = yf_ref[...] + yb_ref[...]
    yc = y - _dot_exact_rhs(y, bd) * inv_n
    var = _dot_exact_rhs(yc * yc, bd) * inv_n
    o_d = (yc * lax.rsqrt(var + D_GN_EPS) * dlng_ref[...] + dlnb_ref[...] + bonus_ref[...]) * dgate_ref[...]
    branches = [b.astype(BF16) for b in (oa_ref[...], ob_ref[...], o_c, o_d)]
    h = (x * (1.0 + sc1) + sh1).astype(BF16)
    blocks = []
    for cb in range(D_MODEL // MXU_WIDTH):
        merged = None
        for n in range(N_BRANCH):
            cols = slice(n * D_MODEL + cb * MXU_WIDTH, n * D_MODEL + (cb + 1) * MXU_WIDTH)
            gate = _sigmoid(jnp.dot(h, wg_ref[0, :, cols], preferred_element_type=F32))
            term = gate * jnp.dot(branches[n], wbr_ref[0, n, :, cb * MXU_WIDTH:(cb + 1) * MXU_WIDTH],
                                  preferred_element_type=F32)
            merged = term if merged is None else merged + term
        blocks.append(merged.astype(BF16))
    mix = jnp.dot(jnp.concatenate(blocks, axis=1), wout_ref[0], preferred_element_type=F32)
    x1 = _layer_norm(ALPHA * x + g1 * mix, lng_ref[...], lnb_ref[...])
    x1_o[...] = x1
    h2 = x1 * (1.0 + sc2) + sh2
    h2_o[...] = h2.astype(BF16)
    logits = _dot3(h2, wr_ref[...]) + br_ref[...]
    tm, n_e = logits.shape
    lane_e = lax.broadcasted_iota(jnp.int32, (tm, n_e), 1)
    lane_o = lax.broadcasted_iota(jnp.int32, (tm, topi_o.shape[1]), 1)
    top_i = jnp.zeros((tm, topi_o.shape[1]), jnp.int32)
    top_v = jnp.zeros((tm, topw_o.shape[1]), F32)
    vals = []
    for kth in range(TOP_K):
        mx = jnp.max(logits, axis=-1, keepdims=True)
        idx = jnp.min(jnp.where(logits == mx, lane_e, n_e), axis=-1, keepdims=True)
        vals.append(mx)
        top_i = jnp.where(lane_o == kth, idx, top_i)
        logits = jnp.where(lane_e == idx, -jnp.inf, logits)
    es = [jnp.exp(vk - vals[0]) for vk in vals]
    den = es[0] + es[1] + es[2] + es[3]
    for kth in range(TOP_K):
        top_v = jnp.where(lane_o == kth, es[kth] / den, top_v)
    topi_o[...] = top_i
    topw_o[...] = top_v


def merge_and_route(x_ctx, x_lat, mod, o_a, o_b, co_f, co_b, cgate, y_f, y_b, bonus, dgate, w_g, w_br, w_out, layer, sp):
    tm = MERGE_ROW_TILE
    hn = D_HEADS * D_N
    row = lambda w: pl.BlockSpec((tm, w), lambda t: (t, 0))
    small = [sp[k] for k in ('c_norm', 'd_ln_g', 'd_ln_b', 'ln_g', 'ln_b', 'w_router', 'b_router')]
    index = [layer, layer, layer, 2 * layer, 2 * layer, layer, layer]
    bd = _head_block_diag()
    return pl.pallas_call(
        _merge_kernel,
        grid=(N_TOK // tm,),
        in_specs=([_ctx_rows_spec(D_MODEL, tm), _lat_rows_spec(D_MODEL, tm), _mod_spec(layer, tm)]
                  + [row(hn)] * 9 + [_layer_block(w, layer) for w in (w_g, w_br, w_out)]
                  + [_layer_item(a, i) for a, i in zip(small, index)] + [_full(bd.shape)]),
        out_specs=[row(D_MODEL), row(D_MODEL), row(128), row(128)],
        out_shape=[jax.ShapeDtypeStruct((N_TOK, D_MODEL), F32), jax.ShapeDtypeStruct((MOE_ROWS, D_MODEL), BF16),
                   jax.ShapeDtypeStruct((N_TOK, 128), jnp.int32), jax.ShapeDtypeStruct((N_TOK, 128), F32)],
        compiler_params=pltpu.CompilerParams(dimension_semantics=("parallel",), vmem_limit_bytes=VMEM_LIMIT),
        name="merge_and_route",
    )(x_ctx, x_lat, mod, o_a, o_b, co_f, co_b, cgate, y_f, y_b, bonus, dgate, w_g, w_br, w_out, *small, bd)


def _moe_kernel(te_ref, tv_ref, first_ref, slot_ref, next_ref, x_ref, w1_hbm, b1_ref, w2_hbm, b2_ref, perm_ref, *rest,
                layer):
    y_ref, w1buf, w2buf, sem, w1s, w2s, hs = rest[-7:]
    t = pl.program_id(0)
    valid = tv_ref[t] != 0
    d_model, two_f = w1s.shape
    n_blk = two_f // MXU_WIDTH
    half = MXU_WIDTH // 2

    def fetch(expert, slot):
        return (pltpu.make_async_copy(w1_hbm.at[layer, expert], w1buf.at[slot], sem.at[0, slot]),
                pltpu.make_async_copy(w2_hbm.at[layer, expert], w2buf.at[slot], sem.at[1, slot]))

    @pl.when(t == 0)
    def _():
        for cp in fetch(te_ref[0], 0):
            cp.start()

    @pl.when(first_ref[t] == 1)
    def _():
        slot = slot_ref[t]
        for cp in fetch(te_ref[t], slot):
            cp.wait()

        @pl.when(next_ref[t] >= 0)
        def _():
            for cp in fetch(next_ref[t], 1 - slot):
                cp.start()

        for blk in range(n_blk):
            sl = slice(blk * MXU_WIDTH, (blk + 1) * MXU_WIDTH)
            wb = w1buf[slot, :, sl].astype(BF16)
            w1s[:, sl] = jnp.dot(wb, perm_ref[...], preferred_element_type=F32).astype(BF16)
        w2s[...] = w2buf[slot].astype(BF16)

    @pl.when(valid)
    def _():
        x = x_ref[...]
        for blk in range(n_blk):
            sl = slice(blk * MXU_WIDTH, (blk + 1) * MXU_WIDTH)
            u = jnp.dot(x, w1s[:, sl], preferred_element_type=F32) + b1_ref[0, 0, :, sl]
            glu = jnp.minimum(u[:, :half], SWIGLU_LIMIT)
            lin = jnp.clip(u[:, half:], -SWIGLU_LIMIT, SWIGLU_LIMIT)
            hs[:, blk * half:(blk + 1) * half] = (glu * _sigmoid(SWIGLU_ALPHA * glu) * (lin + 1.0)).astype(BF16)
        y = jnp.dot(hs[...], w2s[...], preferred_element_type=F32) + b2_ref[0, 0]
        y_ref[...] = y.astype(y_ref.dtype)

    @pl.when(jnp.logical_not(valid))
    def _():
        y_ref[...] = jnp.zeros_like(y_ref)


def _deinterleave_perm():
    half = MXU_WIDTH // 2
    src = np.arange(MXU_WIDTH)
    dst = np.where(src % 2 == 0, src // 2, half + src // 2)
    p = np.zeros((MXU_WIDTH, MXU_WIDTH), np.float32)
    p[src, dst] = 1.0
    return jnp.asarray(p, BF16)


def _moe_dispatch(top_i):
    n, k = top_i.shape
    tm = MOE_ROW_TILE
    p_rows = n * k + N_EXPERTS * tm
    experts = jnp.arange(N_EXPERTS, dtype=jnp.int32)
    onehot = top_i[:, :, None] == experts
    sel = jnp.sum(onehot.astype(jnp.int32), axis=1)
    before = jnp.cumsum(sel, axis=0) - sel
    counts = jnp.sum(sel, axis=0)
    padded = ((counts + tm - 1) // tm) * tm
    ends = jnp.cumsum(padded)
    starts = ends - padded
    pos = jnp.sum(jnp.where(onehot, (before + starts)[:, None, :], 0), axis=-1)
    n_tiles = p_rows // tm
    tile_start = jnp.arange(n_tiles, dtype=jnp.int32) * tm
    tile_valid = (tile_start < ends[-1]).astype(jnp.int32)
    last_tile = ends[-1] // tm - 1
    tile_expert = jnp.sum(ends[None, :] <= jnp.minimum(tile_start, last_tile * tm)[:, None], axis=1).astype(jnp.int32)
    keys = jnp.sort((top_i * n + jnp.arange(n, dtype=jnp.int32)[:, None]).reshape(-1))
    tile_onehot = tile_expert[:, None] == experts[None, :]
    lookup = lambda table: jnp.sum(jnp.where(tile_onehot, table[None, :], 0), axis=1)
    tile_rank0 = tile_start - lookup(starts)
    rank = tile_rank0[:, None] + jnp.arange(tm, dtype=jnp.int32)[None, :]
    sorted_at = jnp.clip(lookup(jnp.cumsum(counts) - counts)[:, None] + rank, 0, n * k - 1)
    filler = (tile_start[:, None] + jnp.arange(tm, dtype=jnp.int32)[None, :]) % n
    in_group = rank < lookup(counts)[:, None]
    src_tok = []
    for lo, hi in zip(MOE_SPLIT_TILES[:-1], MOE_SPLIT_TILES[1:]):
        tile_keys = keys[sorted_at[lo:hi].reshape(-1)].reshape(hi - lo, tm)
        src_tok.append(jnp.where(in_group[lo:hi], tile_keys % n, filler[lo:hi]).reshape(-1))
    tables = []
    for lo, hi in zip(MOE_SPLIT_TILES[:-1], MOE_SPLIT_TILES[1:]):
        nst = hi - lo
        idx = jnp.arange(nst, dtype=jnp.int32)
        te_h = tile_expert[lo:hi]
        is_first = jnp.concatenate([jnp.ones((1,), jnp.int32), (te_h[1:] != te_h[:-1]).astype(jnp.int32)])
        slot = (jnp.cumsum(is_first) - 1) % 2
        nxt = jnp.min(jnp.where(jnp.logical_and(idx[None, :] > idx[:, None], is_first[None, :] == 1),
                                idx[None, :], nst), axis=1)
        next_expert = jnp.sum(jnp.where(idx[None, :] == nxt[:, None], te_h[None, :] + 1, 0), axis=1) - 1
        tables.append((te_h, tile_valid[lo:hi], is_first, slot.astype(jnp.int32), next_expert.astype(jnp.int32)))
    return pos, src_tok, tables, p_rows


def moe_experts(h2, top_i, layer, w1, b1, w2, b2):
    n = top_i.shape[0]
    d = h2.shape[1]
    depth, e, _, two_f = w1.shape
    f = two_f // 2
    tm = MOE_ROW_TILE
    pos, src_tok, tables, p_rows = _moe_dispatch(top_i)
    assert h2.shape[0] == p_rows
    src_tok = [lax.optimization_barrier(s) for s in src_tok]
    b1p = b1.reshape(depth, e, two_f // MXU_WIDTH, MXU_WIDTH // 2, 2).swapaxes(3, 4).reshape(depth, e, 1, two_f)
    b2r = b2.reshape(depth, e, 1, d)
    expert_vec = lambda w: pl.BlockSpec((1, 1, 1, w), lambda t, te, *_: (layer, te[t], 0, 0))
    spans = list(zip(MOE_SPLIT_TILES[:-1], MOE_SPLIT_TILES[1:]))
    assert MOE_SPLIT_TILES[0] == 0 and MOE_SPLIT_TILES[-1] * tm == p_rows
    xs = [h2.at[s].get(mode="promise_in_bounds") for s in src_tok]
    ys = None
    for h, (lo, hi) in enumerate(spans):
        in_specs = [
            pl.BlockSpec((tm, d), lambda t, *_: (t, 0)),
            pl.BlockSpec(memory_space=pl.ANY),
            expert_vec(two_f),
            pl.BlockSpec(memory_space=pl.ANY),
            expert_vec(d),
            pl.BlockSpec((MXU_WIDTH, MXU_WIDTH), lambda t, *_: (0, 0)),
        ]
        args = [*tables[h], xs[h], w1, b1p, w2, b2r, _deinterleave_perm()]
        aliases = {}
        if ys is not None:
            in_specs.append(pl.BlockSpec(memory_space=pl.ANY))
            aliases = {len(args): 0}
            args.append(ys)
        ys = pl.pallas_call(
            functools.partial(_moe_kernel, layer=layer),
            grid_spec=pltpu.PrefetchScalarGridSpec(
                num_scalar_prefetch=len(tables[h]), grid=(hi - lo,), in_specs=in_specs,
                out_specs=pl.BlockSpec((tm, d), lambda t, *_, lo=lo: (lo + t, 0)),
                scratch_shapes=[pltpu.VMEM((2, d, two_f), F32), pltpu.VMEM((2, f, d), F32),
                                pltpu.SemaphoreType.DMA((2, 2)),
                                pltpu.VMEM((d, two_f), BF16), pltpu.VMEM((f, d), BF16), pltpu.VMEM((tm, f), BF16)]),
            out_shape=jax.ShapeDtypeStruct((p_rows, d), BF16),
            input_output_aliases=aliases,
            compiler_params=pltpu.CompilerParams(dimension_semantics=("arbitrary",),
                                                 vmem_limit_bytes=48 * 1024 * 1024),
            name="moe_experts",
        )(*args)
    return ys, pos


def _final_kernel(x1_ref, mod_ref, ys_ref, topw_ref, lng_ref, lnb_ref, o_ref):
    g2 = mod_ref[0, 5:6]
    moe = None
    for kth in range(TOP_K):
        term = ys_ref[kth].astype(F32) * topw_ref[:, kth:kth + 1]
        moe = term if moe is None else moe + term
    o_ref[...] = _layer_norm(ALPHA * x1_ref[...] + g2 * moe, lng_ref[...], lnb_ref[...])


def combine_and_norm(x1, mod, ys, pos, top_w, layer, sp):
    tm = ROW_TILE
    ln_g, ln_b = sp['ln_g'], sp['ln_b']
    outs = []
    for t0, n_rows in ((0, N_CTX), (CTX_TILES, N_TOK - N_CTX)):
        idx = lax.optimization_barrier(pos[t0 * tm:t0 * tm + n_rows].T.reshape(-1))
        rows = ys.at[idx].get(mode="promise_in_bounds").reshape(TOP_K, n_rows, D_MODEL)
        outs.append(pl.pallas_call(
            _final_kernel,
            grid=(n_rows // tm,),
            in_specs=[pl.BlockSpec((tm, D_MODEL), lambda t, t0=t0: (t0 + t, 0)),
                      pl.BlockSpec((1, 6, D_MODEL),
                                   lambda t, t0=t0: (layer * (1 + N_LAT_SEQ) + _mod_row(t0 + t), 0, 0)),
                      pl.BlockSpec((TOP_K, tm, D_MODEL), lambda t: (0, t, 0)),
                      pl.BlockSpec((tm, 128), lambda t, t0=t0: (t0 + t, 0)),
                      _layer_item(ln_g, 2 * layer + 1), _layer_item(ln_b, 2 * layer + 1)],
            out_specs=pl.BlockSpec((tm, D_MODEL), lambda t: (t, 0)),
            out_shape=jax.ShapeDtypeStruct((n_rows, D_MODEL), F32),
            compiler_params=pltpu.CompilerParams(dimension_semantics=("parallel",)),
            name="combine_and_norm",
        )(x1, mod, rows, top_w, ln_g, ln_b))
    return outs


def kernel(x_prompt, x_sample, cache_a_k, cache_a_v, cache_b_ckv, cache_b_kpe, state_c, state_d, c,
           c_ctx, w_mod, b_mod, w_in, a_sink, b_q_norm, b_w_uq, b_kv_norm, b_w_ukv, c_w_gate, c_b_gate,
           c_norm, d_mu, d_w0, d_w2, d_a0, d_a2, d_g2, d_k_k, d_k_a, d_r_k, d_ln_g, d_ln_b, w_br, w_out,
           ln_g, ln_b, w_router, b_router, w_mlp1, b_mlp1, w_mlp2, b_mlp2):
    sp = stacked_params(dict(
        a_sink=a_sink, b_q_norm=b_q_norm, b_w_uq=b_w_uq, b_kv_norm=b_kv_norm, b_w_ukv=b_w_ukv, c_w_gate=c_w_gate,
        c_b_gate=c_b_gate, c_norm=c_norm, d_mu=d_mu, d_w0=d_w0, d_w2=d_w2, d_a0=d_a0, d_a2=d_a2, d_g2=d_g2,
        d_k_k=d_k_k, d_k_a=d_k_a, d_r_k=d_r_k, d_ln_g=d_ln_g, d_ln_b=d_ln_b, ln_g=ln_g, ln_b=ln_b,
        w_router=w_router, b_router=b_router))
    assert x_prompt.shape == (N_CTX_SEQ, CTX_LEN, D_MODEL) and x_sample.shape == (N_LAT_SEQ, LAT_LEN, D_MODEL)
    x_ctx, x_lat = x_prompt.reshape(N_CTX, D_MODEL), x_sample.reshape(-1, D_MODEL)
    cond8 = jnp.concatenate([c_ctx[None], c, jnp.zeros((8 - 1 - N_LAT_SEQ, D_MODEL), F32)], axis=0)
    mod = modulation_table(cond8, w_mod, b_mod)[:, :1 + N_LAT_SEQ].reshape(DEPTH * (1 + N_LAT_SEQ), 6, D_MODEL)
    tables = _rope_tables()
    w_small, w_g = prepare_in_weights(w_in)
    w_br_bf, w_out_bf = w_br.astype(BF16), w_out.astype(BF16)
    new = {name: [] for name in ("a_k", "a_v", "b_ckv", "b_kpe", "c", "d")}
    for l in range(DEPTH):
        (aq, ak, av, bq, bckv, bkpe, cq4, ck4, cla_f, cla_b, cv, cgate,
         r, v, kk, lw_f, lw_b, k_f, k_b, a_f, a_b, bonus, dgate) = mixer_prelude(
             x_ctx, x_lat, mod, w_small, l, tables, sp)

        o_a = gqa_attention(aq, ak, av, sp['a_sink'], cache_a_k, cache_a_v, l)
        o_b = mla_attention(bq, bckv, bkpe, sp['b_w_ukv'], cache_b_ckv, cache_b_kpe, l)

        c_s0 = jnp.swapaxes(state_c[:, l], 3, 4).reshape(1, N_CHAIN, C_DV, C_DK)
        co_f, co_b, c_fin = _recurrence_calls(functools.partial(_gla_kernel, dot=_dot1), "gla", [cq4, ck4, cv],
                                              [cla_f], [cla_b], c_s0, (C_DV, C_DK), C_HEADS * C_DV)
        d_s0 = state_d[:, l].reshape(1, N_CHAIN, D_N, D_N)
        y_f, y_b, d_fin = _recurrence_calls(functools.partial(_rwkv_kernel, dot=_dot1), "rwkv7", [r, v, kk],
                                            [lw_f, k_f, a_f], [lw_b, k_b, a_b], d_s0, (D_N, D_N), D_HEADS * D_N)

        x1, h2, top_i, top_w = merge_and_route(x_ctx, x_lat, mod, o_a, o_b, co_f, co_b, cgate, y_f, y_b, bonus, dgate,
                                               w_g, w_br_bf, w_out_bf, l, sp)
        ys, pos = moe_experts(h2, top_i[:, :TOP_K], l, w_mlp1, b_mlp1, w_mlp2, b_mlp2)
        x_ctx, x_lat = combine_and_norm(x1, mod, ys, pos, top_w, l, sp)

        new["a_k"].append(ak[:N_CTX].reshape(N_CTX_SEQ, CTX_LEN, A_KV_HEADS, A_HD).transpose(0, 2, 1, 3))
        new["a_v"].append(av[:N_CTX].reshape(N_CTX_SEQ, CTX_LEN, A_KV_HEADS, A_HD).transpose(0, 2, 1, 3))
        new["b_ckv"].append(bckv[:N_CTX].reshape(N_CTX_SEQ, CTX_LEN, B_KV_LORA))
        new["b_kpe"].append(bkpe[:N_CTX].reshape(N_CTX_SEQ, CTX_LEN, B_ROPE))
        new["c"].append(jnp.swapaxes(c_fin.reshape(N_CTX_SEQ, 2, C_HEADS, C_DV, C_DK), 3, 4))
        new["d"].append(d_fin.reshape(N_CTX_SEQ, 2, D_HEADS, D_N, D_N))
    y_prompt = x_ctx.reshape(x_prompt.shape)
    y_sample = x_lat.reshape(x_sample.shape)
    return (y_prompt, y_sample, *(jnp.stack(new[name], axis=1) for name in ("a_k", "a_v", "b_ckv", "b_kpe", "c", "d")))
```

```python
import functools

import jax
import jax.numpy as jnp
import numpy as np
from jax import lax
from jax.experimental import pallas as pl
from jax.experimental.pallas import tpu as pltpu

F32 = jnp.float32
BF16 = jnp.bfloat16

MXU_WIDTH = 256
VMEM_LIMIT = 56 * 1024 * 1024

D_MODEL = 1024
DEPTH = 2
GRID_W = 64
ROPE_BASE = 10000.0
A_HEADS, A_KV_HEADS, A_HD = 4, 2, 64
B_HEADS, B_NOPE, B_ROPE, B_VD, B_Q_LORA, B_KV_LORA = 4, 64, 32, 64, 192, 128
C_HEADS, C_DK, C_DV, C_GATE_RANK, C_GATE_TEMP = 4, 32, 64, 16, 16.0
D_HEADS, D_N, D_DECAY_RANK, D_AAA_RANK, D_GATE_RANK, D_GN_EPS = 4, 64, 64, 64, 128, 64e-5
BRANCH_W = 256
N_BRANCH = 4
N_EXPERTS = 32
TOP_K = 4
SWIGLU_LIMIT = 7.0
SWIGLU_ALPHA = 1.702
ALPHA = (2 * DEPTH) ** 0.25
LN_EPS = 1e-5
RMS_EPS = 1e-6

N_CTX_SEQ, CTX_LEN = 16, 256
N_LAT_SEQ, LAT_LEN = 2, 2048
N_CTX = N_CTX_SEQ * CTX_LEN
N_TOK = N_CTX + N_LAT_SEQ * LAT_LEN
ROW_TILE = 256
N_TILES = N_TOK // ROW_TILE
CTX_TILES = N_CTX // ROW_TILE
LAT_TILES_PER_SEQ = LAT_LEN // ROW_TILE
MLA_HEAD_LANES = 128
MERGE_ROW_TILE = 512
MOE_ROW_TILE = 256
MOE_ROWS = N_TOK * TOP_K + N_EXPERTS * MOE_ROW_TILE
MOE_SPLIT_TILES = (0, 48, MOE_ROWS // MOE_ROW_TILE)

_ORIG = dict(aq=(0, 256), ak=(256, 384), av=(384, 512), bcq=(512, 704), bckv=(704, 832), bkpe=(832, 864),
             cq=(864, 992), ck=(992, 1120), cv=(1120, 1376), cog=(1376, 1632), caf=(1632, 1648), cab=(1648, 1664),
             zd=(1664, 2816))
_ORDER = ("aq", "ak", "av", "cq", "ck", "cv", "cog", "zd", "bcq", "caf", "cab", "bkpe", "bckv")
COL = {}
_off = 0
for _name in _ORDER:
    _w = _ORIG[_name][1] - _ORIG[_name][0]
    COL[_name] = (_off, _off + _w)
    _off += _w
SMALL_COLS = _off
G_START = SMALL_COLS


def _cs(name):
    return slice(*COL[name])


def _split2(x):
    hi = x.astype(BF16)
    lo = (x - hi.astype(F32)).astype(BF16)
    return hi, lo


def _bdot(a, b, dims):
    return lax.dot_general(a, b, dims, preferred_element_type=F32)


_D2 = (((1,), (0,)), ((), ()))
_D2T = (((1,), (1,)), ((), ()))
_NN = (((2,), (1,)), ((0,), (0,)))
_NT = (((2,), (2,)), ((0,), (0,)))
_TN = (((1,), (1,)), ((0,), (0,)))


def _dot1(a, b, dims=_D2):
    return _bdot(a.astype(BF16), b.astype(BF16), dims)


def _dot3(a, b, dims=_D2):
    ah, al = _split2(a)
    bh, bl = _split2(b)
    return _bdot(ah, bh, dims) + (_bdot(ah, bl, dims) + _bdot(al, bh, dims))


def _dot_exact_lhs(a01, b, dims=_D2):
    a = a01.astype(BF16)
    h, l = _split2(b)
    return _bdot(a, h, dims) + _bdot(a, l, dims)


def _dot_exact_rhs(a, b01, dims=_D2):
    b = b01.astype(BF16)
    h, l = _split2(a)
    return _bdot(h, b, dims) + _bdot(l, b, dims)


def _sigmoid(x):
    return 0.5 * jnp.tanh(0.5 * x) + 0.5


def _softplus(x):
    return jnp.maximum(x, 0.0) + jnp.log(1.0 + jnp.exp(-jnp.abs(x)))


def _mod_row(t, tile=ROW_TILE):
    return jnp.where(t < N_CTX // tile, 0, 1 + (t - N_CTX // tile) // (LAT_LEN // tile))


def _full(shape):
    nd = len(shape)
    return pl.BlockSpec(shape, lambda *_: (0,) * nd)


def _ctx_rows_spec(width, tile=ROW_TILE):
    return pl.BlockSpec((tile, width), lambda t: (jnp.minimum(t, N_CTX // tile - 1), 0))


def _lat_rows_spec(width, tile=ROW_TILE):
    return pl.BlockSpec((tile, width), lambda t: (jnp.maximum(t - N_CTX // tile, 0), 0))


def _tile_rows(t, ctx_ref, lat_ref):
    return jnp.where(t < N_CTX // ctx_ref.shape[0], ctx_ref[...], lat_ref[...])


MOD_COL_TILE = 1536


def _mod_kernel(c_ref, w_ref, b_ref, o_ref):
    c = c_ref[...]
    o_ref[0] = _dot3(c * _sigmoid(c), w_ref[0]) + b_ref[0]


def modulation_table(cond8, w_mod, b_mod):
    depth, d, six_d = w_mod.shape
    return pl.pallas_call(
        _mod_kernel,
        grid=(depth, six_d // MOD_COL_TILE),
        in_specs=[pl.BlockSpec((8, d), lambda l, j: (0, 0)),
                  pl.BlockSpec((1, d, MOD_COL_TILE), lambda l, j: (l, 0, j)),
                  pl.BlockSpec((1, 1, MOD_COL_TILE), lambda l, j: (l, 0, j))],
        out_specs=pl.BlockSpec((1, 8, MOD_COL_TILE), lambda l, j: (l, 0, j)),
        out_shape=jax.ShapeDtypeStruct((depth, 8, six_d), F32),
        compiler_params=pltpu.CompilerParams(dimension_semantics=("parallel", "parallel")),
        name="modulation",
    )(cond8, w_mod, b_mod.reshape(depth, 1, six_d))


WPREP_ROWS = 128


def _wprep_kernel(w_ref, small_ref, gate_ref):
    for name in _ORDER:
        lo, hi = _ORIG[name]
        small_ref[0, :, _cs(name)] = w_ref[0, :, lo:hi].astype(BF16)
    gate_ref[0] = w_ref[0, :, G_START:].astype(BF16)


def prepare_in_weights(w_in):
    depth, d, cols = w_in.shape
    return pl.pallas_call(
        _wprep_kernel,
        grid=(depth, d // WPREP_ROWS),
        in_specs=[pl.BlockSpec((1, WPREP_ROWS, cols), lambda l, r: (l, r, 0))],
        out_specs=[pl.BlockSpec((1, WPREP_ROWS, SMALL_COLS), lambda l, r: (l, r, 0)),
                   pl.BlockSpec((1, WPREP_ROWS, cols - G_START), lambda l, r: (l, r, 0))],
        out_shape=[jax.ShapeDtypeStruct((depth, d, SMALL_COLS), BF16),
                   jax.ShapeDtypeStruct((depth, d, cols - G_START), BF16)],
        compiler_params=pltpu.CompilerParams(dimension_semantics=("parallel", "parallel")),
        name="prepare_in_weights",
    )(w_in)


def _rot_pairs(x, half, lane_mod_base=0):
    w = x.shape[-1]
    lane = lax.broadcasted_iota(jnp.int32, (1, w), 1) - lane_mod_base
    first = (lane % (2 * half)) < half
    return jnp.where(first, -pltpu.roll(x, w - half, axis=1), pltpu.roll(x, half, axis=1))


def _pre_kernel(xc_ref, xl_ref, xp_ref, xn_ref, mod_ref, w_ref, ca_ref, sa_ref, cb_ref, sb_ref, ck_ref, sk_ref,
                qnorm_ref, kvnorm_ref, wuq_ref, cwg_ref, cbg_ref, rep_ref, mu_ref, dw0_ref, dw2_ref, da0_ref,
                da2_ref, dg2_ref, dkk_ref, dka_ref, drk_ref, bd_ref,
                aq_o, ak_o, av_o, bq_o, bckv_o, bkpe_o, cq4_o, ck4_o, claf_o, clab_o, cv_o, cgate_o,
                r_o, v_o, kk_o, lwf_o, lwb_o, kf_o, kb_o, af_o, ab_o, bonus_o, dgate_o):
    t = pl.program_id(0)
    tm = xc_ref.shape[0]
    sh1 = mod_ref[0, 0:1, :]
    sc1 = mod_ref[0, 1:2, :]

    def modulate(xv):
        return (xv * (1.0 + sc1) + sh1).astype(BF16)

    x_tile = _tile_rows(t, xc_ref, xl_ref)
    h_all = jnp.concatenate([modulate(x_tile), modulate(xp_ref[...]), modulate(xn_ref[...])], axis=0)
    z_all = jnp.dot(h_all, w_ref[0], preferred_element_type=F32)
    z = z_all[:tm]

    aq = z[:, _cs("aq")]
    ak = z[:, _cs("ak")]
    aq_o[...] = aq * ca_ref[...] + _rot_pairs(aq, A_HD // 4) * sa_ref[...]
    ak_o[...] = ak * ca_ref[:, :ak.shape[1]] + _rot_pairs(ak, A_HD // 4) * sa_ref[:, :ak.shape[1]]
    av_o[...] = z[:, _cs("av")]

    bcq = z[:, _cs("bcq")]
    qn = bcq * lax.rsqrt(jnp.mean(bcq * bcq, axis=-1, keepdims=True) + RMS_EPS) * qnorm_ref[...]
    bq = _dot1(qn, wuq_ref[...])
    bq_o[...] = bq * cb_ref[...] + _rot_pairs(bq, B_ROPE // 4, lane_mod_base=B_NOPE) * sb_ref[...]
    bckv = z[:, _cs("bckv")]
    bckv_o[...] = bckv * lax.rsqrt(jnp.mean(bckv * bckv, axis=-1, keepdims=True) + RMS_EPS) * kvnorm_ref[...]
    kpe_lo = COL["bkpe"][0] // 128 * 128
    kblk = z[:, kpe_lo:kpe_lo + 128]
    kblk = kblk * ck_ref[...] + _rot_pairs(kblk, B_ROPE // 4) * sk_ref[...]
    bkpe_o[...] = kblk[:, COL["bkpe"][0] - kpe_lo:COL["bkpe"][1] - kpe_lo]

    rep = rep_ref[...]
    cq4_o[...] = _dot1(z[:, _cs("cq")] * (C_DK ** -0.5), rep)
    ck4_o[...] = _dot1(z[:, _cs("ck")], rep)
    cv_o[...] = z[:, _cs("cv")]
    cog = z[:, _cs("cog")]
    cgate_o[...] = cog * _sigmoid(cog)
    for direction, (name, out) in enumerate((("caf", claf_o), ("cab", clab_o))):
        pre = _dot1(z[:, _cs(name)], cwg_ref[direction]) + cbg_ref[direction]
        la_hi, la_lo = _split2(-_softplus(-pre) * (1.0 / C_GATE_TEMP))
        out[...] = _bdot(la_hi, rep, _D2) + _bdot(la_lo, rep, _D2)

    zd_cols = _cs("zd")
    zd = z[:, zd_cols]
    j = (t - CTX_TILES) % LAT_TILES_PER_SEQ
    latent = t >= CTX_TILES
    has_prev = jnp.logical_and(latent, j != 0)
    has_next = jnp.logical_and(latent, j != LAT_TILES_PER_SEQ - 1)
    prev_row = jnp.where(has_prev, z_all[tm + 7:tm + 8, zd_cols], 0.0)
    next_row = jnp.where(has_next, z_all[tm + 8:tm + 9, zd_cols], 0.0)
    row = lax.broadcasted_iota(jnp.int32, (tm, 1), 0)
    up = jnp.where(row == 0, prev_row, pltpu.roll(zd, 1, axis=0))
    dn = jnp.where(row == tm - 1, next_row, pltpu.roll(zd, tm - 1, axis=0))
    zd = zd + (0.5 * (up + dn) - zd) * mu_ref[...]

    hn = D_HEADS * D_N
    d_r, d_k, d_v = zd[:, :hn], zd[:, hn:2 * hn], zd[:, 2 * hn:3 * hn]
    o = 3 * hn
    d_w = (zd[:, o:o + D_DECAY_RANK], zd[:, o + D_DECAY_RANK:o + 2 * D_DECAY_RANK])
    o += 2 * D_DECAY_RANK
    d_a = (zd[:, o:o + D_AAA_RANK], zd[:, o + D_AAA_RANK:o + 2 * D_AAA_RANK])
    o += 2 * D_AAA_RANK
    d_g = zd[:, o:o + D_GATE_RANK]
    bd = bd_ref[...]
    kk = d_k * dkk_ref[...]
    kk = kk / jnp.maximum(jnp.sqrt(_dot_exact_rhs(kk * kk, bd)), 1e-12)
    r_o[...] = d_r
    v_o[...] = d_v
    kk_o[...] = kk
    k_sum = None
    for direction, (lw_o, k_o, a_o) in enumerate(((lwf_o, kf_o, af_o), (lwb_o, kb_o, ab_o))):
        w_log = -_softplus(-(dw0_ref[direction] + _dot1(jnp.tanh(d_w[direction]), dw2_ref[direction]))) - 0.5
        lw_o[...] = -jnp.exp(w_log)
        a = _sigmoid(da0_ref[direction] + _dot1(d_a[direction], da2_ref[direction]))
        k_dir = d_k * (1.0 + (a - 1.0) * dka_ref[...])
        k_o[...] = k_dir
        a_o[...] = a
        k_sum = k_dir if k_sum is None else k_sum + k_dir
    bonus_o[...] = d_v * _dot_exact_rhs(d_r * drk_ref[...] * k_sum, bd)
    dgate_o[...] = _dot1(_sigmoid(d_g), dg2_ref[...])


def _rope_tables():
    pos = np.arange(LAT_LEN)
    rowp, colp = (pos // GRID_W).astype(np.float32), (pos % GRID_W).astype(np.float32)

    f32 = np.float32

    def head_tables(rot_dim):
        quarter = rot_dim // 4
        inv = (f32(ROPE_BASE) ** (-np.arange(quarter, dtype=f32) / f32(quarter))).astype(f32)
        ar = (rowp[:, None] * inv).astype(f32)
        ac = (colp[:, None] * inv).astype(f32)
        cos = np.concatenate([np.cos(ar), np.cos(ar), np.cos(ac), np.cos(ac)], axis=-1).astype(f32)
        sin = np.concatenate([np.sin(ar), np.sin(ar), np.sin(ac), np.sin(ac)], axis=-1).astype(f32)
        return cos, sin

    def with_identity(c, s):
        w = c.shape[1]
        return (jnp.asarray(np.concatenate([np.ones((ROW_TILE, w), f32), c], axis=0)),
                jnp.asarray(np.concatenate([np.zeros((ROW_TILE, w), f32), s], axis=0)))

    ca, sa = head_tables(A_HD)
    ca, sa = with_identity(np.tile(ca, (1, A_HEADS)), np.tile(sa, (1, A_HEADS)))
    cbh, sbh = head_tables(B_ROPE)
    ones, zeros = np.ones((LAT_LEN, B_NOPE), f32), np.zeros((LAT_LEN, B_NOPE), f32)
    qpad = MLA_HEAD_LANES - B_NOPE - B_ROPE
    cb, sb = with_identity(
        np.tile(np.concatenate([ones, cbh, np.ones((LAT_LEN, qpad), f32)], axis=1), (1, B_HEADS)),
        np.tile(np.concatenate([zeros, sbh, np.zeros((LAT_LEN, qpad), f32)], axis=1), (1, B_HEADS)))
    pad = 128 - B_ROPE
    ck, sk = with_identity(np.concatenate([np.ones((LAT_LEN, pad), f32), cbh], axis=1),
                           np.concatenate([np.zeros((LAT_LEN, pad), f32), sbh], axis=1))
    return ca, sa, cb, sb, ck, sk


def _lane_repeat_matrix():
    m = np.zeros((C_HEADS * C_DK, C_HEADS * 128), np.float32)
    for h in range(C_HEADS):
        for g in range(128 // C_DK):
            for d in range(C_DK):
                m[h * C_DK + d, h * 128 + g * C_DK + d] = 1.0
    return jnp.asarray(m, BF16)


def _head_block_diag():
    m = np.kron(np.eye(D_HEADS, dtype=np.float32), np.ones((D_N, D_N), np.float32))
    return jnp.asarray(m, BF16)


PRE_OUT_WIDTHS = (256, 128, 128, B_HEADS * MLA_HEAD_LANES, 128, 32, 512, 512, 512, 512, 256, 256) + (256,) * 11


def _layer_block(arr, layer):
    nd = arr.ndim
    return pl.BlockSpec((1,) + arr.shape[1:], lambda *_: (layer,) + (0,) * (nd - 1))


def _layer_item(arr, index):
    nd = arr.ndim
    return pl.BlockSpec((None,) + arr.shape[1:], lambda *_: (index,) + (0,) * (nd - 1))


def _mod_spec(layer, tile=ROW_TILE):
    return pl.BlockSpec((1, 6, D_MODEL), lambda t: (layer * (1 + N_LAT_SEQ) + _mod_row(t, tile), 0, 0))


def stacked_params(p):
    depth = p['d_mu'].shape[0]
    hn = D_HEADS * D_N
    w_uq = p['b_w_uq'].reshape(depth, B_Q_LORA, B_HEADS, B_NOPE + B_ROPE)
    w_uq = jnp.pad(w_uq, ((0, 0), (0, 0), (0, 0), (0, MLA_HEAD_LANES - B_NOPE - B_ROPE)))
    row = lambda a: a.reshape(depth, 1, -1)
    return dict(
        b_q_norm=row(p['b_q_norm']), b_kv_norm=row(p['b_kv_norm']), w_uq=w_uq.reshape(depth, B_Q_LORA, -1),
        c_w_gate=p['c_w_gate'], c_b_gate=p['c_b_gate'].reshape(depth, 2, 1, -1), d_mu=row(p['d_mu']),
        d_w0=p['d_w0'].reshape(depth, 2, 1, hn), d_w2=p['d_w2'], d_a0=p['d_a0'].reshape(depth, 2, 1, hn),
        d_a2=p['d_a2'], d_g2=p['d_g2'], d_k_k=row(p['d_k_k']), d_k_a=row(p['d_k_a']), d_r_k=row(p['d_r_k']),
        c_norm=row(jnp.tile(p['c_norm'], (1, C_HEADS))), d_ln_g=row(p['d_ln_g']), d_ln_b=row(p['d_ln_b']),
        ln_g=p['ln_g'].reshape(depth * 2, 1, -1), ln_b=p['ln_b'].reshape(depth * 2, 1, -1),
        w_router=p['w_router'], b_router=row(p['b_router']), a_sink=p['a_sink'].reshape(-1), b_w_ukv=p['b_w_ukv'])


def mixer_prelude(x_ctx, x_lat, mod, w_small, layer, tables, sp):
    tm = ROW_TILE
    lat8 = lambda t: (t - CTX_TILES) * (tm // 8)
    last8 = x_lat.shape[0] // 8 - 1
    tab_idx = lambda t: (jnp.where(t < CTX_TILES, 0, 1 + (t - CTX_TILES) % LAT_TILES_PER_SEQ), 0)
    small = [sp[k] for k in ('b_q_norm', 'b_kv_norm', 'w_uq', 'c_w_gate', 'c_b_gate')] + [_lane_repeat_matrix()]
    small += [sp[k] for k in ('d_mu', 'd_w0', 'd_w2', 'd_a0', 'd_a2', 'd_g2', 'd_k_k', 'd_k_a', 'd_r_k')]
    small += [_head_block_diag()]
    const = lambda a: _full(a.shape) if a.dtype == BF16 else _layer_item(a, layer)
    in_specs = ([_ctx_rows_spec(D_MODEL), _lat_rows_spec(D_MODEL),
                 pl.BlockSpec((8, D_MODEL), lambda t: (jnp.clip(lat8(t) - 1, 0, last8), 0)),
                 pl.BlockSpec((8, D_MODEL), lambda t: (jnp.clip(lat8(t + 1), 0, last8), 0)),
                 _mod_spec(layer),
                 _layer_block(w_small, layer)]
                + [pl.BlockSpec((tm, tab.shape[1]), tab_idx) for tab in tables]
                + [const(a) for a in small])
    return pl.pallas_call(
        _pre_kernel,
        grid=(N_TILES,),
        in_specs=in_specs,
        out_specs=[pl.BlockSpec((tm, w), lambda t: (t, 0)) for w in PRE_OUT_WIDTHS],
        out_shape=[jax.ShapeDtypeStruct((N_TOK, w), F32) for w in PRE_OUT_WIDTHS],
        compiler_params=pltpu.CompilerParams(dimension_semantics=("parallel",), vmem_limit_bytes=VMEM_LIMIT),
        name="mixer_prelude",
    )(x_ctx, x_lat, x_lat, x_lat, mod, w_small, *tables, *small)


ATT_Q_BLOCK = 128
MLA_Q_BLOCK = 256
MLA_LATENT_Q_BLOCK = 512
ATT_WINDOW = 128
ATT_NEG_INF = -1e30
CACHE_LEN = 512


def _softmax_pv(s, v, sink):
    dv = v.shape[1] // 2
    m = jnp.max(s, axis=-1, keepdims=True)
    if sink is not None:
        m = jnp.maximum(m, sink)
    e = jnp.exp((s - m).astype(BF16))
    o = jnp.dot(e, v, preferred_element_type=F32)
    den = o[:, dv:dv + 1]
    if sink is not None:
        den = den + jnp.exp(sink - m)
    return o[:, :dv] / den


def _with_ones(v):
    return jnp.concatenate([v.astype(BF16), jnp.ones(v.shape, BF16)], axis=1)


def _gqa_kernel(sink_ref, q_ref, k_ref, v_ref, *rest, hd, group, scale, windowed, sink_base):
    if windowed:
        kp_ref, kn_ref, vp_ref, vn_ref, kc_ref, vc_ref, _, o_ref = rest
    else:
        (o_ref,) = rest
    i = pl.program_id(1)
    tq = q_ref.shape[0]
    n_kv = k_ref.shape[1] // hd
    if windowed:
        qpos = i * tq + lax.broadcasted_iota(jnp.int32, (tq, 3 * tq), 0)
        kpos = (i - 1) * tq + lax.broadcasted_iota(jnp.int32, (tq, 3 * tq), 1)
        n_tok = pl.num_programs(1) * tq
        mask = (jnp.abs(qpos - kpos) <= ATT_WINDOW) & (kpos >= 0) & (kpos < n_tok)
        mask = jnp.concatenate([mask] * group, axis=0)
    for kvh in range(n_kv):
        ks = slice(kvh * hd, (kvh + 1) * hd)
        qs = [q_ref[:, (kvh * group + g) * hd:(kvh * group + g + 1) * hd] for g in range(group)]
        q = (jnp.concatenate(qs, axis=0) * scale).astype(BF16)
        sink = jnp.concatenate(
            [jnp.full((tq, 1), sink_ref[sink_base + kvh * group + g], F32) for g in range(group)], axis=0)
        if windowed:
            k_win = jnp.concatenate([kp_ref[:, ks], k_ref[:, ks], kn_ref[:, ks]], axis=0)
            v_win = jnp.concatenate([vp_ref[:, ks], v_ref[:, ks], vn_ref[:, ks]], axis=0)
            s_win = _bdot(q, k_win.astype(BF16), _D2T)
            s_win = jnp.where(mask, s_win, ATT_NEG_INF)
            s_ctx = _bdot(q, kc_ref[0, 0, kvh].astype(BF16), _D2T)
            s = jnp.concatenate([s_win, s_ctx], axis=1)
            v = jnp.concatenate([v_win, vc_ref[0, 0, kvh]], axis=0)
        else:
            s = _bdot(q, k_ref[:, ks].astype(BF16), _D2T)
            v = v_ref[:, ks]
        o = _softmax_pv(s, _with_ones(v), sink)
        for g in range(group):
            h = kvh * group + g
            o_ref[:, h * hd:(h + 1) * hd] = o[g * tq:(g + 1) * tq]


def gqa_attention(q, k, v, sink, cache_k, cache_v, layer):
    qw, kw = q.shape[1], k.shape[1]
    group = qw // kw
    scale = A_HD ** -0.5
    params = pltpu.CompilerParams(dimension_semantics=("parallel", "parallel"))
    out_shape = jax.ShapeDtypeStruct((N_TOK, qw), F32)
    ctx_spec = lambda w: pl.BlockSpec((CTX_LEN, w), lambda s, i, sk: (s, 0))
    o = pl.pallas_call(
        functools.partial(_gqa_kernel, hd=A_HD, group=group, scale=scale, windowed=False, sink_base=layer * A_HEADS),
        grid_spec=pltpu.PrefetchScalarGridSpec(
            num_scalar_prefetch=1, grid=(N_CTX_SEQ, 1), in_specs=[ctx_spec(qw), ctx_spec(kw), ctx_spec(kw)],
            out_specs=ctx_spec(qw)),
        out_shape=out_shape, compiler_params=params, name="gqa_full",
    )(sink, q, k, v)
    tq = ATT_Q_BLOCK
    nb = LAT_LEN // tq
    base = N_CTX // tq
    blk = lambda w, f: pl.BlockSpec((tq, w), lambda b, i, sk: (base + nb * b + f(i), 0))
    same = lambda i: i
    prev = lambda i: jnp.maximum(i - 1, 0)
    nxt = lambda i: jnp.minimum(i + 1, nb - 1)
    cspec = pl.BlockSpec((1, 1) + cache_k.shape[2:], lambda b, i, sk: (b, layer, 0, 0, 0))
    return pl.pallas_call(
        functools.partial(_gqa_kernel, hd=A_HD, group=group, scale=scale, windowed=True, sink_base=layer * A_HEADS),
        grid_spec=pltpu.PrefetchScalarGridSpec(
            num_scalar_prefetch=1, grid=(N_LAT_SEQ, nb),
            in_specs=[blk(qw, same), blk(kw, same), blk(kw, same), blk(kw, prev), blk(kw, nxt), blk(kw, prev),
                      blk(kw, nxt), cspec, cspec, pl.BlockSpec(memory_space=pl.ANY)],
            out_specs=blk(qw, same)),
        out_shape=out_shape, input_output_aliases={10: 0}, compiler_params=params, name="gqa_windowed",
    )(sink, q, k, v, k, k, v, v, cache_k, cache_v, o)


def _mla_kernel(q_ref, ckv_ref, kpe_ref, wukv_ref, *rest, n_heads, nope, rope, vd, scale, cached):
    if cached:
        cckv_ref, ckpe_ref, _, o_ref, k_scr, vext_scr = rest
    else:
        o_ref, k_scr, vext_scr = rest
    i = pl.program_id(1)
    n_cache = k_scr.shape[0] - ckv_ref.shape[0]
    hw = nope + vd
    hl = MLA_HEAD_LANES

    @pl.when(i == 0)
    def _():
        w = wukv_ref[...].astype(BF16)

        def expand(rows, kpe_rows, lo, hi):
            kv = jnp.dot(rows.astype(BF16), w, preferred_element_type=F32).astype(BF16)
            n = hi - lo
            kpe = kpe_rows.astype(BF16)
            for h in range(n_heads):
                k_scr[lo:hi, hl * h:hl * (h + 1)] = jnp.concatenate(
                    [kv[:, h * hw:h * hw + nope], kpe, jnp.zeros((n, hl - nope - rope), BF16)], axis=1)
                vext_scr[lo:hi, 2 * vd * h:2 * vd * (h + 1)] = jnp.concatenate(
                    [kv[:, h * hw + nope:(h + 1) * hw], jnp.ones((n, vd), BF16)], axis=1)

        if cached:
            expand(cckv_ref[0, 0], ckpe_ref[0, 0], 0, n_cache)
        expand(ckv_ref[...], kpe_ref[...], n_cache, k_scr.shape[0])

    for h in range(n_heads):
        qh = (q_ref[:, hl * h:hl * (h + 1)] * scale).astype(BF16)
        s = _bdot(qh, k_scr[:, hl * h:hl * (h + 1)], _D2T)
        o_ref[:, h * vd:(h + 1) * vd] = _softmax_pv(s, vext_scr[:, 2 * vd * h:2 * vd * (h + 1)], None)


def mla_attention(q, ckv, kpe, w_ukv, cache_ckv, cache_kpe, layer):
    qw = q.shape[1]
    tq = MLA_Q_BLOCK
    kw = dict(n_heads=B_HEADS, nope=B_NOPE, rope=B_ROPE, vd=B_VD, scale=(B_NOPE + B_ROPE) ** -0.5)
    params = pltpu.CompilerParams(dimension_semantics=("parallel", "arbitrary"))
    out_shape = jax.ShapeDtypeStruct((N_TOK, B_HEADS * B_VD), F32)
    scratch = lambda rows: [pltpu.VMEM((rows, B_HEADS * MLA_HEAD_LANES), BF16),
                            pltpu.VMEM((rows, 2 * B_HEADS * B_VD), BF16)]
    nbc = CTX_LEN // tq
    o = pl.pallas_call(
        functools.partial(_mla_kernel, cached=False, **kw),
        grid=(N_CTX_SEQ, nbc),
        in_specs=[pl.BlockSpec((tq, qw), lambda s, i: (s * nbc + i, 0)),
                  pl.BlockSpec((CTX_LEN, B_KV_LORA), lambda s, i: (s, 0)),
                  pl.BlockSpec((CTX_LEN, B_ROPE), lambda s, i: (s, 0)),
                  _layer_item(w_ukv, layer)],
        out_specs=pl.BlockSpec((tq, B_HEADS * B_VD), lambda s, i: (s * nbc + i, 0)),
        out_shape=out_shape,
        scratch_shapes=scratch(CTX_LEN),
        compiler_params=params, name="mla_context",
    )(q, ckv, kpe, w_ukv)
    tq = MLA_LATENT_Q_BLOCK
    nb = LAT_LEN // tq
    base = N_CTX // tq
    lat0 = N_CTX // LAT_LEN
    s_len = CACHE_LEN + LAT_LEN
    return pl.pallas_call(
        functools.partial(_mla_kernel, cached=True, **kw),
        grid=(N_LAT_SEQ, nb),
        in_specs=[pl.BlockSpec((tq, qw), lambda b, i: (base + nb * b + i, 0)),
                  pl.BlockSpec((LAT_LEN, B_KV_LORA), lambda b, i: (lat0 + b, 0)),
                  pl.BlockSpec((LAT_LEN, B_ROPE), lambda b, i: (lat0 + b, 0)),
                  _layer_item(w_ukv, layer),
                  pl.BlockSpec((1, 1, CACHE_LEN, B_KV_LORA), lambda b, i: (b, layer, 0, 0)),
                  pl.BlockSpec((1, 1, CACHE_LEN, B_ROPE), lambda b, i: (b, layer, 0, 0)),
                  pl.BlockSpec(memory_space=pl.ANY)],
        out_specs=pl.BlockSpec((tq, B_HEADS * B_VD), lambda b, i: (base + nb * b + i, 0)),
        out_shape=out_shape, input_output_aliases={6: 0},
        scratch_shapes=scratch(s_len),
        compiler_params=params, name="mla_latent",
    )(q, ckv, kpe, w_ukv, cache_ckv, cache_kpe, o)


CHUNK = 64
GLA_SUB = 16
CTX_SEQS_PER_STEP = 4
LAT_SEQS_PER_STEP = N_LAT_SEQ
CHAINS_PER_SEQ = 2 * 4


def _is_back(shape):
    return (lax.broadcasted_iota(jnp.int32, shape, 0) // 4) % 2 == 1


def _chains(ref_f, ref_b, width):
    return jnp.stack([ref[0, s, 0, :, h * width:(h + 1) * width]
                      for s in range(ref_f.shape[1]) for ref in (ref_f, ref_b) for h in range(4)], axis=0)


def _unchain(y, o_f, o_b):
    for s in range(o_f.shape[1]):
        o_f[0, s, 0] = jnp.concatenate([y[s * 8 + h] for h in range(4)], axis=-1)
        o_b[0, s, 0] = jnp.concatenate([y[s * 8 + 4 + h] for h in range(4)], axis=-1)


def _dir_masks(n_chain, L):
    shape = (n_chain, L, L)
    back = _is_back(shape)
    row = lax.broadcasted_iota(jnp.int32, shape, 1)
    col = lax.broadcasted_iota(jnp.int32, shape, 2)
    ahead = jnp.where(back, col - row, row - col)
    return ahead >= 0, ahead > 0, row == col


def _chunk_end(ci):
    L = ci.shape[1]
    return jnp.where(_is_back((ci.shape[0], 1, 1)), ci[:, 0:1], ci[:, L - 1:L])


def _split_refs(refs, n_in, has_s0, has_sfin):
    ins = refs[:n_in]
    pos = n_in
    s0_ref = None
    if has_s0:
        s0_ref = refs[pos]
        pos += 3
    of_ref, ob_ref = refs[pos], refs[pos + 1]
    pos += 2
    sfin_ref = refs[pos] if has_sfin else None
    return ins, s0_ref, of_ref, ob_ref, sfin_ref, refs[-1]


def _init_state(s_scr, s0_ref):
    @pl.when(pl.program_id(1) == 0)
    def _():
        if s0_ref is None:
            s_scr[...] = jnp.zeros_like(s_scr)
        else:
            s_scr[...] = s0_ref[0]


def _emit_state(sfin_ref, s_new):
    if sfin_ref is None:
        return

    @pl.when(pl.program_id(1) == pl.num_programs(1) - 1)
    def _():
        sfin_ref[0] = s_new


def _rwkv_kernel(*refs, dot, has_s0, has_sfin):
    (rf, rb, vf, vb, kkf, kkb, lwf, lwb, kf, kb, af, ab), s0_ref, yf_ref, yb_ref, sfin_ref, s_scr = _split_refs(
        refs, 12, has_s0, has_sfin)
    _init_state(s_scr, s0_ref)
    n = D_N
    r = _chains(rf, rb, n)
    v = _chains(vf, vb, n)
    kk = _chains(kkf, kkb, n)
    lw = _chains(lwf, lwb, n)
    k = _chains(kf, kb, n)
    a = _chains(af, ab, n)
    L = r.shape[1]
    S = s_scr[...]
    incl, strict, diag = _dir_masks(r.shape[0], L)
    ci = _dot_exact_lhs(jnp.where(incl, 1.0, 0.0), lw, _NN)
    ce = ci - lw
    cl = _chunk_end(ci)
    e_neg = jnp.exp(-ci)
    b = a * kk
    alpha = kk * jnp.exp(ce)
    rho = r * jnp.exp(ci)
    beta = b * e_neg
    kappa = k * e_neg
    e_end = jnp.exp(cl - ci)
    ar = jnp.concatenate([alpha, rho], axis=1)
    bk = jnp.concatenate([beta, kappa], axis=1)
    w = dot(ar, bk, _NT)
    nmat = jnp.where(strict, w[:, :L, :L], 0.0)
    mmat = jnp.where(strict, w[:, :L, L:], 0.0)
    p1 = jnp.where(incl, w[:, L:, :L], 0.0)
    p2 = jnp.where(incl, w[:, L:, L:], 0.0)
    x = jnp.where(diag, 1.0, 0.0) - nmat
    p = dot(nmat, nmat, _NN)
    span = 2
    while True:
        x = x + dot(x, p, _NN)
        span *= 2
        if span >= L:
            break
        p = dot(p, p, _NN)
    us = dot(ar, S, _NT)
    rhs = us[:, :L] + dot(mmat, v, _NN)
    d = -dot(x, rhs, _NN)
    dv = jnp.concatenate([d, v], axis=1)
    pp = jnp.concatenate([p1, p2], axis=2)
    _unchain(us[:, L:] + dot(pp, dv, _NN), yf_ref, yb_ref)
    bk_end = jnp.concatenate([b * e_end, k * e_end], axis=1)
    s_new = S * jnp.exp(cl) + dot(dv, bk_end, _TN)
    s_scr[...] = s_new
    _emit_state(sfin_ref, s_new)


def _gla_kernel(*refs, dot, has_s0, has_sfin):
    (qf, qb, kf, kb, vf, vb, laf, lab), s0_ref, of_ref, ob_ref, sfin_ref, s_scr = _split_refs(
        refs, 8, has_s0, has_sfin)
    _init_state(s_scr, s0_ref)
    q4 = _chains(qf, qb, 128)
    k4 = _chains(kf, kb, 128)
    la4 = _chains(laf, lab, 128)
    v = _chains(vf, vb, C_DV)
    g, L, lanes = q4.shape
    dk = C_DK
    n_sub = L // GLA_SUB
    st = s_scr[...]
    incl, _, _ = _dir_masks(g, L)
    c = _dot_exact_lhs(jnp.where(incl, 1.0, 0.0), la4, _NN)
    shape = (g, L, lanes)
    back = _is_back(shape)
    lane_blk = lax.broadcasted_iota(jnp.int32, shape, 2) // dk
    row_blk = lax.broadcasted_iota(jnp.int32, shape, 1) // GLA_SUB
    cref_f = jnp.zeros(shape, F32)
    cref_b = jnp.zeros(shape, F32)
    for j in range(1, n_sub):
        cref_f = jnp.where(lane_blk == j, c[:, j * GLA_SUB - 1:j * GLA_SUB], cref_f)
        cref_b = jnp.where(lane_blk == j - 1, c[:, j * GLA_SUB:j * GLA_SUB + 1], cref_b)
    cref = jnp.where(back, cref_b, cref_f)
    q_on = row_blk == lane_blk
    k_on = jnp.where(back, row_blk - lane_blk, lane_blk - row_blk) >= 0
    qh = jnp.where(q_on, q4 * jnp.exp(jnp.where(q_on, c - cref, 0.0)), 0.0)
    kh = jnp.where(k_on, k4 * jnp.exp(jnp.where(k_on, cref - c, 0.0)), 0.0)
    att = jnp.where(incl, dot(qh, kh, _NT), 0.0)
    cl = _chunk_end(c)
    qe = (q4 * jnp.exp(c))[:, :, :dk]
    ke = (k4 * jnp.exp(cl - c))[:, :, :dk]
    _unchain(dot(qe, st, _NT) + dot(att, v, _NN), of_ref, ob_ref)
    s_new = st * jnp.exp(cl[:, :, :dk]) + dot(v, ke, _TN)
    s_scr[...] = s_new
    _emit_state(sfin_ref, s_new)


def _recurrence_calls(kernel_fn, name, pairs, singles_f, singles_b, s0_lat, state_dims, out_width):
    def run(view, grid, group, s0, prev_out):
        seqs, nc = view[1], view[2]
        n_chain = seqs * CHAINS_PER_SEQ
        fwd_map = lambda p, c: (group(p), 0, c, 0, 0)
        bwd_map = lambda p, c: (group(p), 0, nc - 1 - c, 0, 0)
        blk = lambda w: (1, seqs, 1, CHUNK, w)
        args, in_specs = [], []
        for af, ab in [(a, a) for a in pairs] + list(zip(singles_f, singles_b)):
            w = af.shape[-1]
            args += [af.reshape(view + (w,)), ab.reshape(view + (w,))]
            in_specs += [pl.BlockSpec(blk(w), fwd_map), pl.BlockSpec(blk(w), bwd_map)]
        out_specs = [pl.BlockSpec(blk(out_width), fwd_map), pl.BlockSpec(blk(out_width), bwd_map)]
        out_shape = [jax.ShapeDtypeStruct(view + (out_width,), F32)] * 2
        aliases = {}
        if s0 is not None:
            args += [s0] + [o.reshape(view + (out_width,)) for o in prev_out]
            in_specs += [_full(s0.shape), pl.BlockSpec(memory_space=pl.ANY), pl.BlockSpec(memory_space=pl.ANY)]
            aliases = {len(args) - 2: 0, len(args) - 1: 1}
        else:
            out_specs.append(pl.BlockSpec((1, n_chain) + state_dims, lambda p, c: (p, 0, 0, 0)))
            out_shape.append(jax.ShapeDtypeStruct((grid[0], n_chain) + state_dims, F32))
        return pl.pallas_call(
            functools.partial(kernel_fn, has_s0=s0 is not None, has_sfin=s0 is None),
            grid=grid, in_specs=in_specs, out_specs=out_specs, out_shape=out_shape,
            input_output_aliases=aliases, scratch_shapes=[pltpu.VMEM((n_chain,) + state_dims, F32)],
            compiler_params=pltpu.CompilerParams(dimension_semantics=("parallel", "arbitrary")),
            name=name + ("_latent" if s0 is not None else "_context"),
        )(*args)

    ctx_nc = CTX_LEN // CHUNK
    cs, ls = CTX_SEQS_PER_STEP, LAT_SEQS_PER_STEP
    ctx_view = (N_TOK // (cs * CTX_LEN), cs, ctx_nc, CHUNK)
    o_f, o_b, s_fin = run(ctx_view, (N_CTX_SEQ // cs, ctx_nc), lambda p: p, None, None)
    lat_nc = LAT_LEN // CHUNK
    lat_view = (N_TOK // (ls * LAT_LEN), ls, lat_nc, CHUNK)
    o_f, o_b = run(lat_view, (1, lat_nc), lambda p: N_CTX // (ls * LAT_LEN), s0_lat, (o_f, o_b))
    return o_f.reshape(N_TOK, out_width), o_b.reshape(N_TOK, out_width), s_fin


def _layer_norm(x, g, b):
    mu = jnp.mean(x, axis=-1, keepdims=True)
    xc = x - mu
    var = jnp.mean(xc * xc, axis=-1, keepdims=True)
    return xc * lax.rsqrt(var + LN_EPS) * g + b


def _merge_kernel(xc_ref, xl_ref, mod_ref, oa_ref, ob_ref, cof_ref, cob_ref, cgate_ref, yf_ref, yb_ref, bonus_ref,
                  dgate_ref, wg_ref, wbr_ref, wout_ref, cnorm_ref, dlng_ref, dlnb_ref, lng_ref, lnb_ref, wr_ref, br_ref,
                  bd_ref, x1_o, h2_o, topi_o, topw_o):
    x = _tile_rows(pl.program_id(0), xc_ref, xl_ref)
    m = mod_ref[0]
    sh1, sc1, g1, sh2, sc2 = m[0:1], m[1:2], m[2:3], m[3:4], m[4:5]
    bd = bd_ref[...]
    inv_n = 1.0 / D_N
    co = cof_ref[...] + cob_ref[...]
    o_c = co * lax.rsqrt(_dot_exact_rhs(co * co, bd) * inv_n + RMS_EPS) * cnorm_ref[...] * cgate_ref[...]
    y = yf_ref[...] + yb_ref[...]
    yc = y - _dot_exact_rhs(y, bd) * inv_n
    var = _dot_exact_rhs(yc * yc, bd) * inv_n
    o_d = (yc * lax.rsqrt(var + D_GN_EPS) * dlng_ref[...] + dlnb_ref[...] + bonus_ref[...]) * dgate_ref[...]
    branches = [b.astype(BF16) for b in (oa_ref[...], ob_ref[...], o_c, o_d)]
    h = (x * (1.0 + sc1) + sh1).astype(BF16)
    blocks = []
    for cb in range(D_MODEL // MXU_WIDTH):
        merged = None
        for n in range(N_BRANCH):
            cols = slice(n * D_MODEL + cb * MXU_WIDTH, n * D_MODEL + (cb + 1) * MXU_WIDTH)
            gate = _sigmoid(jnp.dot(h, wg_ref[0, :, cols], preferred_element_type=F32))
            term = gate * jnp.dot(branches[n], wbr_ref[0, n, :, cb * MXU_WIDTH:(cb + 1) * MXU_WIDTH],
                                  preferred_element_type=F32)
            merged = term if merged is None else merged + term
        blocks.append(merged.astype(BF16))
    mix = jnp.dot(jnp.concatenate(blocks, axis=1), wout_ref[0], preferred_element_type=F32)
    x1 = _layer_norm(ALPHA * x + g1 * mix, lng_ref[...], lnb_ref[...])
    x1_o[...] = x1
    h2 = x1 * (1.0 + sc2) + sh2
    h2_o[...] = h2.astype(BF16)
    logits = _dot3(h2, wr_ref[...]) + br_ref[...]
    tm, n_e = logits.shape
    lane_e = lax.broadcasted_iota(jnp.int32, (tm, n_e), 1)
    lane_o = lax.broadcasted_iota(jnp.int32, (tm, topi_o.shape[1]), 1)
    top_i = jnp.zeros((tm, topi_o.shape[1]), jnp.int32)
    top_v = jnp.zeros((tm, topw_o.shape[1]), F32)
    vals = []
    for kth in range(TOP_K):
        mx = jnp.max(logits, axis=-1, keepdims=True)
        idx = jnp.min(jnp.where(logits == mx, lane_e, n_e), axis=-1, keepdims=True)
        vals.append(mx)
        top_i = jnp.where(lane_o == kth, idx, top_i)
        logits = jnp.where(lane_e == idx, -jnp.inf, logits)
    es = [jnp.exp(vk - vals[0]) for vk in vals]
    den = es[0] + es[1] + es[2] + es[3]
    for kth in range(TOP_K):
        top_v = jnp.where(lane_o == kth, es[kth] / den, top_v)
    topi_o[...] = top_i
    topw_o[...] = top_v


def merge_and_route(x_ctx, x_lat, mod, o_a, o_b, co_f, co_b, cgate, y_f, y_b, bonus, dgate, w_g, w_br, w_out, layer, sp):
    tm = MERGE_ROW_TILE
    hn = D_HEADS * D_N
    row = lambda w: pl.BlockSpec((tm, w), lambda t: (t, 0))
    small = [sp[k] for k in ('c_norm', 'd_ln_g', 'd_ln_b', 'ln_g', 'ln_b', 'w_router', 'b_router')]
    index = [layer, layer, layer, 2 * layer, 2 * layer, layer, layer]
    bd = _head_block_diag()
    return pl.pallas_call(
        _merge_kernel,
        grid=(N_TOK // tm,),
        in_specs=([_ctx_rows_spec(D_MODEL, tm), _lat_rows_spec(D_MODEL, tm), _mod_spec(layer, tm)]
                  + [row(hn)] * 9 + [_layer_block(w, layer) for w in (w_g, w_br, w_out)]
                  + [_layer_item(a, i) for a, i in zip(small, index)] + [_full(bd.shape)]),
        out_specs=[row(D_MODEL), row(D_MODEL), row(128), row(128)],
        out_shape=[jax.ShapeDtypeStruct((N_TOK, D_MODEL), F32), jax.ShapeDtypeStruct((MOE_ROWS, D_MODEL), BF16),
                   jax.ShapeDtypeStruct((N_TOK, 128), jnp.int32), jax.ShapeDtypeStruct((N_TOK, 128), F32)],
        compiler_params=pltpu.CompilerParams(dimension_semantics=("parallel",), vmem_limit_bytes=VMEM_LIMIT),
        name="merge_and_route",
    )(x_ctx, x_lat, mod, o_a, o_b, co_f, co_b, cgate, y_f, y_b, bonus, dgate, w_g, w_br, w_out, *small, bd)


def _moe_kernel(te_ref, tv_ref, first_ref, slot_ref, next_ref, x_ref, w1_hbm, b1_ref, w2_hbm, b2_ref, perm_ref, *rest,
                layer):
    y_ref, w1buf, w2buf, sem, w1s, w2s, hs = rest[-7:]
    t = pl.program_id(0)
    valid = tv_ref[t] != 0
    d_model, two_f = w1s.shape
    n_blk = two_f // MXU_WIDTH
    half = MXU_WIDTH // 2

    def fetch(expert, slot):
        return (pltpu.make_async_copy(w1_hbm.at[layer, expert], w1buf.at[slot], sem.at[0, slot]),
                pltpu.make_async_copy(w2_hbm.at[layer, expert], w2buf.at[slot], sem.at[1, slot]))

    @pl.when(t == 0)
    def _():
        for cp in fetch(te_ref[0], 0):
            cp.start()

    @pl.when(first_ref[t] == 1)
    def _():
        slot = slot_ref[t]
        for cp in fetch(te_ref[t], slot):
            cp.wait()

        @pl.when(next_ref[t] >= 0)
        def _():
            for cp in fetch(next_ref[t], 1 - slot):
                cp.start()

        for blk in range(n_blk):
            sl = slice(blk * MXU_WIDTH, (blk + 1) * MXU_WIDTH)
            wb = w1buf[slot, :, sl].astype(BF16)
            w1s[:, sl] = jnp.dot(wb, perm_ref[...], preferred_element_type=F32).astype(BF16)
        w2s[...] = w2buf[slot].astype(BF16)

    @pl.when(valid)
    def _():
        x = x_ref[...]
        for blk in range(n_blk):
            sl = slice(blk * MXU_WIDTH, (blk + 1) * MXU_WIDTH)
            u = jnp.dot(x, w1s[:, sl], preferred_element_type=F32) + b1_ref[0, 0, :, sl]
            glu = jnp.minimum(u[:, :half], SWIGLU_LIMIT)
            lin = jnp.clip(u[:, half:], -SWIGLU_LIMIT, SWIGLU_LIMIT)
            hs[:, blk * half:(blk + 1) * half] = (glu * _sigmoid(SWIGLU_ALPHA * glu) * (lin + 1.0)).astype(BF16)
        y = jnp.dot(hs[...], w2s[...], preferred_element_type=F32) + b2_ref[0, 0]
        y_ref[...] = y.astype(y_ref.dtype)

    @pl.when(jnp.logical_not(valid))
    def _():
        y_ref[...] = jnp.zeros_like(y_ref)


def _deinterleave_perm():
    half = MXU_WIDTH // 2
    src = np.arange(MXU_WIDTH)
    dst = np.where(src % 2 == 0, src // 2, half + src // 2)
    p = np.zeros((MXU_WIDTH, MXU_WIDTH), np.float32)
    p[src, dst] = 1.0
    return jnp.asarray(p, BF16)


def _moe_dispatch(top_i):
    n, k = top_i.shape
    tm = MOE_ROW_TILE
    p_rows = n * k + N_EXPERTS * tm
    experts = jnp.arange(N_EXPERTS, dtype=jnp.int32)
    onehot = top_i[:, :, None] == experts
    sel = jnp.sum(onehot.astype(jnp.int32), axis=1)
    before = jnp.cumsum(sel, axis=0) - sel
    counts = jnp.sum(sel, axis=0)
    padded = ((counts + tm - 1) // tm) * tm
    ends = jnp.cumsum(padded)
    starts = ends - padded
    pos = jnp.sum(jnp.where(onehot, (before + starts)[:, None, :], 0), axis=-1)
    n_tiles = p_rows // tm
    tile_start = jnp.arange(n_tiles, dtype=jnp.int32) * tm
    tile_valid = (tile_start < ends[-1]).astype(jnp.int32)
    last_tile = ends[-1] // tm - 1
    tile_expert = jnp.sum(ends[None, :] <= jnp.minimum(tile_start, last_tile * tm)[:, None], axis=1).astype(jnp.int32)
    keys = jnp.sort((top_i * n + jnp.arange(n, dtype=jnp.int32)[:, None]).reshape(-1))
    tile_onehot = tile_expert[:, None] == experts[None, :]
    lookup = lambda table: jnp.sum(jnp.where(tile_onehot, table[None, :], 0), axis=1)
    tile_rank0 = tile_start - lookup(starts)
    rank = tile_rank0[:, None] + jnp.arange(tm, dtype=jnp.int32)[None, :]
    sorted_at = jnp.clip(lookup(jnp.cumsum(counts) - counts)[:, None] + rank, 0, n * k - 1)
    tile_keys = keys[sorted_at.reshape(-1)].reshape(n_tiles, tm)
    filler = (tile_start[:, None] + jnp.arange(tm, dtype=jnp.int32)[None, :]) % n
    src_tok = jnp.where(rank < lookup(counts)[:, None], tile_keys % n, filler)
    tables = []
    for lo, hi in zip(MOE_SPLIT_TILES[:-1], MOE_SPLIT_TILES[1:]):
        nst = hi - lo
        idx = jnp.arange(nst, dtype=jnp.int32)
        te_h = tile_expert[lo:hi]
        is_first = jnp.concatenate([jnp.ones((1,), jnp.int32), (te_h[1:] != te_h[:-1]).astype(jnp.int32)])
        slot = (jnp.cumsum(is_first) - 1) % 2
        nxt = jnp.min(jnp.where(jnp.logical_and(idx[None, :] > idx[:, None], is_first[None, :] == 1),
                                idx[None, :], nst), axis=1)
        next_expert = jnp.sum(jnp.where(idx[None, :] == nxt[:, None], te_h[None, :] + 1, 0), axis=1) - 1
        tables.append((te_h, tile_valid[lo:hi], is_first, slot.astype(jnp.int32), next_expert.astype(jnp.int32)))
    return pos, src_tok.reshape(-1), tables, p_rows


def moe_experts(h2, top_i, layer, w1, b1, w2, b2):
    n = top_i.shape[0]
    d = h2.shape[1]
    depth, e, _, two_f = w1.shape
    f = two_f // 2
    tm = MOE_ROW_TILE
    pos, src_tok, tables, p_rows = _moe_dispatch(top_i)
    assert h2.shape[0] == p_rows
    src_tok = lax.optimization_barrier(src_tok)
    b1p = b1.reshape(depth, e, two_f // MXU_WIDTH, MXU_WIDTH // 2, 2).swapaxes(3, 4).reshape(depth, e, 1, two_f)
    b2r = b2.reshape(depth, e, 1, d)
    expert_vec = lambda w: pl.BlockSpec((1, 1, 1, w), lambda t, te, *_: (layer, te[t], 0, 0))
    spans = list(zip(MOE_SPLIT_TILES[:-1], MOE_SPLIT_TILES[1:]))
    assert MOE_SPLIT_TILES[0] == 0 and MOE_SPLIT_TILES[-1] * tm == p_rows
    xs = [h2.at[src_tok[lo * tm:hi * tm]].get(mode="promise_in_bounds") for lo, hi in spans]
    ys = None
    for h, (lo, hi) in enumerate(spans):
        in_specs = [
            pl.BlockSpec((tm, d), lambda t, *_: (t, 0)),
            pl.BlockSpec(memory_space=pl.ANY),
            expert_vec(two_f),
            pl.BlockSpec(memory_space=pl.ANY),
            expert_vec(d),
            pl.BlockSpec((MXU_WIDTH, MXU_WIDTH), lambda t, *_: (0, 0)),
        ]
        args = [*tables[h], xs[h], w1, b1p, w2, b2r, _deinterleave_perm()]
        aliases = {}
        if ys is not None:
            in_specs.append(pl.BlockSpec(memory_space=pl.ANY))
            aliases = {len(args): 0}
            args.append(ys)
        ys = pl.pallas_call(
            functools.partial(_moe_kernel, layer=layer),
            grid_spec=pltpu.PrefetchScalarGridSpec(
                num_scalar_prefetch=len(tables[h]), grid=(hi - lo,), in_specs=in_specs,
                out_specs=pl.BlockSpec((tm, d), lambda t, *_, lo=lo: (lo + t, 0)),
                scratch_shapes=[pltpu.VMEM((2, d, two_f), F32), pltpu.VMEM((2, f, d), F32),
                                pltpu.SemaphoreType.DMA((2, 2)),
                                pltpu.VMEM((d, two_f), BF16), pltpu.VMEM((f, d), BF16), pltpu.VMEM((tm, f), BF16)]),
            out_shape=jax.ShapeDtypeStruct((p_rows, d), BF16),
            input_output_aliases=aliases,
            compiler_params=pltpu.CompilerParams(dimension_semantics=("arbitrary",),
                                                 vmem_limit_bytes=48 * 1024 * 1024),
            name="moe_experts",
        )(*args)
    return ys, pos


def _final_kernel(x1_ref, mod_ref, ys_ref, topw_ref, lng_ref, lnb_ref, o_ref):
    g2 = mod_ref[0, 5:6]
    moe = None
    for kth in range(TOP_K):
        term = ys_ref[kth].astype(F32) * topw_ref[:, kth:kth + 1]
        moe = term if moe is None else moe + term
    o_ref[...] = _layer_norm(ALPHA * x1_ref[...] + g2 * moe, lng_ref[...], lnb_ref[...])


def combine_and_norm(x1, mod, ys, pos, top_w, layer, sp):
    tm = ROW_TILE
    ln_g, ln_b = sp['ln_g'], sp['ln_b']
    outs = []
    for t0, n_rows in ((0, N_CTX), (CTX_TILES, N_TOK - N_CTX)):
        idx = lax.optimization_barrier(pos[t0 * tm:t0 * tm + n_rows].T.reshape(-1))
        rows = ys.at[idx].get(mode="promise_in_bounds").reshape(TOP_K, n_rows, D_MODEL)
        outs.append(pl.pallas_call(
            _final_kernel,
            grid=(n_rows // tm,),
            in_specs=[pl.BlockSpec((tm, D_MODEL), lambda t, t0=t0: (t0 + t, 0)),
                      pl.BlockSpec((1, 6, D_MODEL),
                                   lambda t, t0=t0: (layer * (1 + N_LAT_SEQ) + _mod_row(t0 + t), 0, 0)),
                      pl.BlockSpec((TOP_K, tm, D_MODEL), lambda t: (0, t, 0)),
                      pl.BlockSpec((tm, 128), lambda t, t0=t0: (t0 + t, 0)),
                      _layer_item(ln_g, 2 * layer + 1), _layer_item(ln_b, 2 * layer + 1)],
            out_specs=pl.BlockSpec((tm, D_MODEL), lambda t: (t, 0)),
            out_shape=jax.ShapeDtypeStruct((n_rows, D_MODEL), F32),
            compiler_params=pltpu.CompilerParams(dimension_semantics=("parallel",)),
            name="combine_and_norm",
        )(x1, mod, rows, top_w, ln_g, ln_b))
    return outs


def kernel(x_prompt, x_sample, cache_a_k, cache_a_v, cache_b_ckv, cache_b_kpe, state_c, state_d, c,
           c_ctx, w_mod, b_mod, w_in, a_sink, b_q_norm, b_w_uq, b_kv_norm, b_w_ukv, c_w_gate, c_b_gate,
           c_norm, d_mu, d_w0, d_w2, d_a0, d_a2, d_g2, d_k_k, d_k_a, d_r_k, d_ln_g, d_ln_b, w_br, w_out,
           ln_g, ln_b, w_router, b_router, w_mlp1, b_mlp1, w_mlp2, b_mlp2):
    sp = stacked_params(dict(
        a_sink=a_sink, b_q_norm=b_q_norm, b_w_uq=b_w_uq, b_kv_norm=b_kv_norm, b_w_ukv=b_w_ukv, c_w_gate=c_w_gate,
        c_b_gate=c_b_gate, c_norm=c_norm, d_mu=d_mu, d_w0=d_w0, d_w2=d_w2, d_a0=d_a0, d_a2=d_a2, d_g2=d_g2,
        d_k_k=d_k_k, d_k_a=d_k_a, d_r_k=d_r_k, d_ln_g=d_ln_g, d_ln_b=d_ln_b, ln_g=ln_g, ln_b=ln_b,
        w_router=w_router, b_router=b_router))
    assert x_prompt.shape == (N_CTX_SEQ, CTX_LEN, D_MODEL) and x_sample.shape == (N_LAT_SEQ, LAT_LEN, D_MODEL)
    x_ctx, x_lat = x_prompt.reshape(N_CTX, D_MODEL), x_sample.reshape(-1, D_MODEL)
    cond8 = jnp.concatenate([c_ctx[None], c, jnp.zeros((8 - 1 - N_LAT_SEQ, D_MODEL), F32)], axis=0)
    mod = modulation_table(cond8, w_mod, b_mod)[:, :1 + N_LAT_SEQ].reshape(DEPTH * (1 + N_LAT_SEQ), 6, D_MODEL)
    tables = _rope_tables()
    w_small, w_g = prepare_in_weights(w_in)
    w_br_bf, w_out_bf = w_br.astype(BF16), w_out.astype(BF16)
    new = {name: [] for name in ("a_k", "a_v", "b_ckv", "b_kpe", "c", "d")}
    for l in range(DEPTH):
        (aq, ak, av, bq, bckv, bkpe, cq4, ck4, cla_f, cla_b, cv, cgate,
         r, v, kk, lw_f, lw_b, k_f, k_b, a_f, a_b, bonus, dgate) = mixer_prelude(
             x_ctx, x_lat, mod, w_small, l, tables, sp)

        o_a = gqa_attention(aq, ak, av, sp['a_sink'], cache_a_k, cache_a_v, l)
        o_b = mla_attention(bq, bckv, bkpe, sp['b_w_ukv'], cache_b_ckv, cache_b_kpe, l)

        lat_chains = LAT_SEQS_PER_STEP * CHAINS_PER_SEQ
        c_s0 = jnp.swapaxes(state_c[:, l], 3, 4).reshape(1, lat_chains, C_DV, C_DK)
        co_f, co_b, c_fin = _recurrence_calls(functools.partial(_gla_kernel, dot=_dot1), "gla", [cq4, ck4, cv],
                                              [cla_f], [cla_b], c_s0, (C_DV, C_DK), C_HEADS * C_DV)
        d_s0 = state_d[:, l].reshape(1, lat_chains, D_N, D_N)
        y_f, y_b, d_fin = _recurrence_calls(functools.partial(_rwkv_kernel, dot=_dot1), "rwkv7", [r, v, kk],
                                            [lw_f, k_f, a_f], [lw_b, k_b, a_b], d_s0, (D_N, D_N), D_HEADS * D_N)

        x1, h2, top_i, top_w = merge_and_route(x_ctx, x_lat, mod, o_a, o_b, co_f, co_b, cgate, y_f, y_b, bonus, dgate,
                                               w_g, w_br_bf, w_out_bf, l, sp)
        ys, pos = moe_experts(h2, top_i[:, :TOP_K], l, w_mlp1, b_mlp1, w_mlp2, b_mlp2)
        x_ctx, x_lat = combine_and_norm(x1, mod, ys, pos, top_w, l, sp)

        new["a_k"].append(ak[:N_CTX].reshape(N_CTX_SEQ, CTX_LEN, A_KV_HEADS, A_HD).transpose(0, 2, 1, 3))
        new["a_v"].append(av[:N_CTX].reshape(N_CTX_SEQ, CTX_LEN, A_KV_HEADS, A_HD).transpose(0, 2, 1, 3))
        new["b_ckv"].append(bckv[:N_CTX].reshape(N_CTX_SEQ, CTX_LEN, B_KV_LORA))
        new["b_kpe"].append(bkpe[:N_CTX].reshape(N_CTX_SEQ, CTX_LEN, B_ROPE))
        new["c"].append(jnp.swapaxes(c_fin.reshape(N_CTX_SEQ, 2, C_HEADS, C_DV, C_DK), 3, 4))
        new["d"].append(d_fin.reshape(N_CTX_SEQ, 2, D_HEADS, D_N, D_N))
    y_prompt = x_ctx.reshape(x_prompt.shape)
    y_sample = x_lat.reshape(x_sample.shape)
    return (y_prompt, y_sample, *(jnp.stack(new[name], axis=1) for name in ("a_k", "a_v", "b_ckv", "b_kpe", "c", "d")))
```

```python
import functools

import jax
import jax.numpy as jnp
import numpy as np
from jax import lax
from jax.experimental import pallas as pl
from jax.experimental.pallas import tpu as pltpu

F32 = jnp.float32
BF16 = jnp.bfloat16

MXU_WIDTH = 256
VMEM_LIMIT = 56 * 1024 * 1024

D_MODEL = 1024
DEPTH = 2
GRID_W = 64
ROPE_BASE = 10000.0
A_HEADS, A_KV_HEADS, A_HD = 4, 2, 64
B_HEADS, B_NOPE, B_ROPE, B_VD, B_Q_LORA, B_KV_LORA = 4, 64, 32, 64, 192, 128
C_HEADS, C_DK, C_DV, C_GATE_RANK, C_GATE_TEMP = 4, 32, 64, 16, 16.0
D_HEADS, D_N, D_DECAY_RANK, D_AAA_RANK, D_GATE_RANK, D_GN_EPS = 4, 64, 64, 64, 128, 64e-5
BRANCH_W = 256
N_BRANCH = 4
N_EXPERTS = 32
TOP_K = 4
SWIGLU_LIMIT = 7.0
SWIGLU_ALPHA = 1.702
ALPHA = (2 * DEPTH) ** 0.25
LN_EPS = 1e-5
RMS_EPS = 1e-6

N_CTX_SEQ, CTX_LEN = 16, 256
N_LAT_SEQ, LAT_LEN = 2, 2048
N_CTX = N_CTX_SEQ * CTX_LEN
N_TOK = N_CTX + N_LAT_SEQ * LAT_LEN
ROW_TILE = 256
N_TILES = N_TOK // ROW_TILE
CTX_TILES = N_CTX // ROW_TILE
LAT_TILES_PER_SEQ = LAT_LEN // ROW_TILE
MLA_HEAD_LANES = 128
MERGE_ROW_TILE = 512
MOE_ROW_TILE = 256
MOE_ROWS = N_TOK * TOP_K + N_EXPERTS * MOE_ROW_TILE
MOE_SPLIT_TILES = (0, 48, MOE_ROWS // MOE_ROW_TILE)

_ORIG = dict(aq=(0, 256), ak=(256, 384), av=(384, 512), bcq=(512, 704), bckv=(704, 832), bkpe=(832, 864),
             cq=(864, 992), ck=(992, 1120), cv=(1120, 1376), cog=(1376, 1632), caf=(1632, 1648), cab=(1648, 1664),
             zd=(1664, 2816))
_ORDER = ("aq", "ak", "av", "cq", "ck", "cv", "cog", "zd", "bcq", "caf", "cab", "bkpe", "bckv")
COL = {}
_off = 0
for _name in _ORDER:
    _w = _ORIG[_name][1] - _ORIG[_name][0]
    COL[_name] = (_off, _off + _w)
    _off += _w
SMALL_COLS = _off
G_START = SMALL_COLS


def _cs(name):
    return slice(*COL[name])


def _split2(x):
    hi = x.astype(BF16)
    lo = (x - hi.astype(F32)).astype(BF16)
    return hi, lo


def _bdot(a, b, dims):
    return lax.dot_general(a, b, dims, preferred_element_type=F32)


_D2 = (((1,), (0,)), ((), ()))
_D2T = (((1,), (1,)), ((), ()))
_NN = (((2,), (1,)), ((0,), (0,)))
_NT = (((2,), (2,)), ((0,), (0,)))
_TN = (((1,), (1,)), ((0,), (0,)))


def _dot1(a, b, dims=_D2):
    return _bdot(a.astype(BF16), b.astype(BF16), dims)


def _dot3(a, b, dims=_D2):
    ah, al = _split2(a)
    bh, bl = _split2(b)
    return _bdot(ah, bh, dims) + (_bdot(ah, bl, dims) + _bdot(al, bh, dims))


def _dot_exact_lhs(a01, b, dims=_D2):
    a = a01.astype(BF16)
    h, l = _split2(b)
    return _bdot(a, h, dims) + _bdot(a, l, dims)


def _dot_exact_rhs(a, b01, dims=_D2):
    b = b01.astype(BF16)
    h, l = _split2(a)
    return _bdot(h, b, dims) + _bdot(l, b, dims)


def _sigmoid(x):
    return 0.5 * jnp.tanh(0.5 * x) + 0.5


def _softplus(x):
    return jnp.maximum(x, 0.0) + jnp.log(1.0 + jnp.exp(-jnp.abs(x)))


def _mod_row(t, tile=ROW_TILE):
    return jnp.where(t < N_CTX // tile, 0, 1 + (t - N_CTX // tile) // (LAT_LEN // tile))


def _full(shape):
    nd = len(shape)
    return pl.BlockSpec(shape, lambda *_: (0,) * nd)


def _ctx_rows_spec(width, tile=ROW_TILE):
    return pl.BlockSpec((tile, width), lambda t: (jnp.minimum(t, N_CTX // tile - 1), 0))


def _lat_rows_spec(width, tile=ROW_TILE):
    return pl.BlockSpec((tile, width), lambda t: (jnp.maximum(t - N_CTX // tile, 0), 0))


def _tile_rows(t, ctx_ref, lat_ref):
    return jnp.where(t < N_CTX // ctx_ref.shape[0], ctx_ref[...], lat_ref[...])


MOD_COL_TILE = 1536


def _mod_kernel(c_ref, w_ref, b_ref, o_ref):
    c = c_ref[...]
    o_ref[0] = _dot3(c * _sigmoid(c), w_ref[0]) + b_ref[0]


def modulation_table(cond8, w_mod, b_mod):
    depth, d, six_d = w_mod.shape
    return pl.pallas_call(
        _mod_kernel,
        grid=(depth, six_d // MOD_COL_TILE),
        in_specs=[pl.BlockSpec((8, d), lambda l, j: (0, 0)),
                  pl.BlockSpec((1, d, MOD_COL_TILE), lambda l, j: (l, 0, j)),
                  pl.BlockSpec((1, 1, MOD_COL_TILE), lambda l, j: (l, 0, j))],
        out_specs=pl.BlockSpec((1, 8, MOD_COL_TILE), lambda l, j: (l, 0, j)),
        out_shape=jax.ShapeDtypeStruct((depth, 8, six_d), F32),
        compiler_params=pltpu.CompilerParams(dimension_semantics=("parallel", "parallel")),
        name="modulation",
    )(cond8, w_mod, b_mod.reshape(depth, 1, six_d))


WPREP_ROWS = 128


def _wprep_kernel(w_ref, small_ref, gate_ref):
    for name in _ORDER:
        lo, hi = _ORIG[name]
        small_ref[0, :, _cs(name)] = w_ref[0, :, lo:hi].astype(BF16)
    gate_ref[0] = w_ref[0, :, G_START:].astype(BF16)


def prepare_in_weights(w_in):
    depth, d, cols = w_in.shape
    return pl.pallas_call(
        _wprep_kernel,
        grid=(depth, d // WPREP_ROWS),
        in_specs=[pl.BlockSpec((1, WPREP_ROWS, cols), lambda l, r: (l, r, 0))],
        out_specs=[pl.BlockSpec((1, WPREP_ROWS, SMALL_COLS), lambda l, r: (l, r, 0)),
                   pl.BlockSpec((1, WPREP_ROWS, cols - G_START), lambda l, r: (l, r, 0))],
        out_shape=[jax.ShapeDtypeStruct((depth, d, SMALL_COLS), BF16),
                   jax.ShapeDtypeStruct((depth, d, cols - G_START), BF16)],
        compiler_params=pltpu.CompilerParams(dimension_semantics=("parallel", "parallel")),
        name="prepare_in_weights",
    )(w_in)


def _rot_pairs(x, half, lane_mod_base=0):
    w = x.shape[-1]
    lane = lax.broadcasted_iota(jnp.int32, (1, w), 1) - lane_mod_base
    first = (lane % (2 * half)) < half
    return jnp.where(first, -pltpu.roll(x, w - half, axis=1), pltpu.roll(x, half, axis=1))


def _pre_kernel(xc_ref, xl_ref, xp_ref, xn_ref, mod_ref, w_ref, ca_ref, sa_ref, cb_ref, sb_ref, ck_ref, sk_ref,
                qnorm_ref, kvnorm_ref, wuq_ref, cwg_ref, cbg_ref, rep_ref, mu_ref, dw0_ref, dw2_ref, da0_ref,
                da2_ref, dg2_ref, dkk_ref, dka_ref, drk_ref, bd_ref,
                aq_o, ak_o, av_o, bq_o, bckv_o, bkpe_o, cq4_o, ck4_o, claf_o, clab_o, cv_o, cgate_o,
                r_o, v_o, kk_o, lwf_o, lwb_o, kf_o, kb_o, af_o, ab_o, bonus_o, dgate_o):
    t = pl.program_id(0)
    tm = xc_ref.shape[0]
    sh1 = mod_ref[0, 0:1, :]
    sc1 = mod_ref[0, 1:2, :]

    def modulate(xv):
        return (xv * (1.0 + sc1) + sh1).astype(BF16)

    x_tile = _tile_rows(t, xc_ref, xl_ref)
    h_all = jnp.concatenate([modulate(x_tile), modulate(xp_ref[...]), modulate(xn_ref[...])], axis=0)
    z_all = jnp.dot(h_all, w_ref[0], preferred_element_type=F32)
    z = z_all[:tm]

    aq = z[:, _cs("aq")]
    ak = z[:, _cs("ak")]
    aq_o[...] = aq * ca_ref[...] + _rot_pairs(aq, A_HD // 4) * sa_ref[...]
    ak_o[...] = ak * ca_ref[:, :ak.shape[1]] + _rot_pairs(ak, A_HD // 4) * sa_ref[:, :ak.shape[1]]
    av_o[...] = z[:, _cs("av")]

    bcq = z[:, _cs("bcq")]
    qn = bcq * lax.rsqrt(jnp.mean(bcq * bcq, axis=-1, keepdims=True) + RMS_EPS) * qnorm_ref[...]
    bq = _dot1(qn, wuq_ref[...])
    bq_o[...] = bq * cb_ref[...] + _rot_pairs(bq, B_ROPE // 4, lane_mod_base=B_NOPE) * sb_ref[...]
    bckv = z[:, _cs("bckv")]
    bckv_o[...] = bckv * lax.rsqrt(jnp.mean(bckv * bckv, axis=-1, keepdims=True) + RMS_EPS) * kvnorm_ref[...]
    kpe_lo = COL["bkpe"][0] // 128 * 128
    kblk = z[:, kpe_lo:kpe_lo + 128]
    kblk = kblk * ck_ref[...] + _rot_pairs(kblk, B_ROPE // 4) * sk_ref[...]
    bkpe_o[...] = kblk[:, COL["bkpe"][0] - kpe_lo:COL["bkpe"][1] - kpe_lo]

    rep = rep_ref[...]
    cq4_o[...] = _dot1(z[:, _cs("cq")] * (C_DK ** -0.5), rep)
    ck4_o[...] = _dot1(z[:, _cs("ck")], rep)
    cv_o[...] = z[:, _cs("cv")]
    cog = z[:, _cs("cog")]
    cgate_o[...] = cog * _sigmoid(cog)
    for direction, (name, out) in enumerate((("caf", claf_o), ("cab", clab_o))):
        pre = _dot1(z[:, _cs(name)], cwg_ref[direction]) + cbg_ref[direction]
        la_hi, la_lo = _split2(-_softplus(-pre) * (1.0 / C_GATE_TEMP))
        out[...] = _bdot(la_hi, rep, _D2) + _bdot(la_lo, rep, _D2)

    zd_cols = _cs("zd")
    zd = z[:, zd_cols]
    j = (t - CTX_TILES) % LAT_TILES_PER_SEQ
    latent = t >= CTX_TILES
    has_prev = jnp.logical_and(latent, j != 0)
    has_next = jnp.logical_and(latent, j != LAT_TILES_PER_SEQ - 1)
    prev_row = jnp.where(has_prev, z_all[tm + 7:tm + 8, zd_cols], 0.0)
    next_row = jnp.where(has_next, z_all[tm + 8:tm + 9, zd_cols], 0.0)
    row = lax.broadcasted_iota(jnp.int32, (tm, 1), 0)
    up = jnp.where(row == 0, prev_row, pltpu.roll(zd, 1, axis=0))
    dn = jnp.where(row == tm - 1, next_row, pltpu.roll(zd, tm - 1, axis=0))
    zd = zd + (0.5 * (up + dn) - zd) * mu_ref[...]

    hn = D_HEADS * D_N
    d_r, d_k, d_v = zd[:, :hn], zd[:, hn:2 * hn], zd[:, 2 * hn:3 * hn]
    o = 3 * hn
    d_w = (zd[:, o:o + D_DECAY_RANK], zd[:, o + D_DECAY_RANK:o + 2 * D_DECAY_RANK])
    o += 2 * D_DECAY_RANK
    d_a = (zd[:, o:o + D_AAA_RANK], zd[:, o + D_AAA_RANK:o + 2 * D_AAA_RANK])
    o += 2 * D_AAA_RANK
    d_g = zd[:, o:o + D_GATE_RANK]
    bd = bd_ref[...]
    kk = d_k * dkk_ref[...]
    kk = kk / jnp.maximum(jnp.sqrt(_dot_exact_rhs(kk * kk, bd)), 1e-12)
    r_o[...] = d_r
    v_o[...] = d_v
    kk_o[...] = kk
    k_sum = None
    for direction, (lw_o, k_o, a_o) in enumerate(((lwf_o, kf_o, af_o), (lwb_o, kb_o, ab_o))):
        w_log = -_softplus(-(dw0_ref[direction] + _dot1(jnp.tanh(d_w[direction]), dw2_ref[direction]))) - 0.5
        lw_o[...] = -jnp.exp(w_log)
        a = _sigmoid(da0_ref[direction] + _dot1(d_a[direction], da2_ref[direction]))
        k_dir = d_k * (1.0 + (a - 1.0) * dka_ref[...])
        k_o[...] = k_dir
        a_o[...] = a
        k_sum = k_dir if k_sum is None else k_sum + k_dir
    bonus_o[...] = d_v * _dot_exact_rhs(d_r * drk_ref[...] * k_sum, bd)
    dgate_o[...] = _dot1(_sigmoid(d_g), dg2_ref[...])


def _rope_tables():
    pos = np.arange(LAT_LEN)
    rowp, colp = (pos // GRID_W).astype(np.float32), (pos % GRID_W).astype(np.float32)

    f32 = np.float32

    def head_tables(rot_dim):
        quarter = rot_dim // 4
        inv = (f32(ROPE_BASE) ** (-np.arange(quarter, dtype=f32) / f32(quarter))).astype(f32)
        ar = (rowp[:, None] * inv).astype(f32)
        ac = (colp[:, None] * inv).astype(f32)
        cos = np.concatenate([np.cos(ar), np.cos(ar), np.cos(ac), np.cos(ac)], axis=-1).astype(f32)
        sin = np.concatenate([np.sin(ar), np.sin(ar), np.sin(ac), np.sin(ac)], axis=-1).astype(f32)
        return cos, sin

    def with_identity(c, s):
        w = c.shape[1]
        return (jnp.asarray(np.concatenate([np.ones((ROW_TILE, w), f32), c], axis=0)),
                jnp.asarray(np.concatenate([np.zeros((ROW_TILE, w), f32), s], axis=0)))

    ca, sa = head_tables(A_HD)
    ca, sa = with_identity(np.tile(ca, (1, A_HEADS)), np.tile(sa, (1, A_HEADS)))
    cbh, sbh = head_tables(B_ROPE)
    ones, zeros = np.ones((LAT_LEN, B_NOPE), f32), np.zeros((LAT_LEN, B_NOPE), f32)
    qpad = MLA_HEAD_LANES - B_NOPE - B_ROPE
    cb, sb = with_identity(
        np.tile(np.concatenate([ones, cbh, np.ones((LAT_LEN, qpad), f32)], axis=1), (1, B_HEADS)),
        np.tile(np.concatenate([zeros, sbh, np.zeros((LAT_LEN, qpad), f32)], axis=1), (1, B_HEADS)))
    pad = 128 - B_ROPE
    ck, sk = with_identity(np.concatenate([np.ones((LAT_LEN, pad), f32), cbh], axis=1),
                           np.concatenate([np.zeros((LAT_LEN, pad), f32), sbh], axis=1))
    return ca, sa, cb, sb, ck, sk


def _lane_repeat_matrix():
    m = np.zeros((C_HEADS * C_DK, C_HEADS * 128), np.float32)
    for h in range(C_HEADS):
        for g in range(128 // C_DK):
            for d in range(C_DK):
                m[h * C_DK + d, h * 128 + g * C_DK + d] = 1.0
    return jnp.asarray(m, BF16)


def _head_block_diag():
    m = np.kron(np.eye(D_HEADS, dtype=np.float32), np.ones((D_N, D_N), np.float32))
    return jnp.asarray(m, BF16)


PRE_OUT_WIDTHS = (256, 128, 128, B_HEADS * MLA_HEAD_LANES, 128, 32, 512, 512, 512, 512, 256, 256) + (256,) * 11


def _layer_block(arr, layer):
    nd = arr.ndim
    return pl.BlockSpec((1,) + arr.shape[1:], lambda *_: (layer,) + (0,) * (nd - 1))


def _layer_item(arr, index):
    nd = arr.ndim
    return pl.BlockSpec((None,) + arr.shape[1:], lambda *_: (index,) + (0,) * (nd - 1))


def _mod_spec(layer, tile=ROW_TILE):
    return pl.BlockSpec((1, 6, D_MODEL), lambda t: (layer * (1 + N_LAT_SEQ) + _mod_row(t, tile), 0, 0))


def stacked_params(p):
    depth = p['d_mu'].shape[0]
    hn = D_HEADS * D_N
    w_uq = p['b_w_uq'].reshape(depth, B_Q_LORA, B_HEADS, B_NOPE + B_ROPE)
    w_uq = jnp.pad(w_uq, ((0, 0), (0, 0), (0, 0), (0, MLA_HEAD_LANES - B_NOPE - B_ROPE)))
    row = lambda a: a.reshape(depth, 1, -1)
    return dict(
        b_q_norm=row(p['b_q_norm']), b_kv_norm=row(p['b_kv_norm']), w_uq=w_uq.reshape(depth, B_Q_LORA, -1),
        c_w_gate=p['c_w_gate'], c_b_gate=p['c_b_gate'].reshape(depth, 2, 1, -1), d_mu=row(p['d_mu']),
        d_w0=p['d_w0'].reshape(depth, 2, 1, hn), d_w2=p['d_w2'], d_a0=p['d_a0'].reshape(depth, 2, 1, hn),
        d_a2=p['d_a2'], d_g2=p['d_g2'], d_k_k=row(p['d_k_k']), d_k_a=row(p['d_k_a']), d_r_k=row(p['d_r_k']),
        c_norm=row(jnp.tile(p['c_norm'], (1, C_HEADS))), d_ln_g=row(p['d_ln_g']), d_ln_b=row(p['d_ln_b']),
        ln_g=p['ln_g'].reshape(depth * 2, 1, -1), ln_b=p['ln_b'].reshape(depth * 2, 1, -1),
        w_router=p['w_router'], b_router=row(p['b_router']), a_sink=p['a_sink'].reshape(-1), b_w_ukv=p['b_w_ukv'])


def mixer_prelude(x_ctx, x_lat, mod, w_small, layer, tables, sp):
    tm = ROW_TILE
    lat8 = lambda t: (t - CTX_TILES) * (tm // 8)
    last8 = x_lat.shape[0] // 8 - 1
    tab_idx = lambda t: (jnp.where(t < CTX_TILES, 0, 1 + (t - CTX_TILES) % LAT_TILES_PER_SEQ), 0)
    small = [sp[k] for k in ('b_q_norm', 'b_kv_norm', 'w_uq', 'c_w_gate', 'c_b_gate')] + [_lane_repeat_matrix()]
    small += [sp[k] for k in ('d_mu', 'd_w0', 'd_w2', 'd_a0', 'd_a2', 'd_g2', 'd_k_k', 'd_k_a', 'd_r_k')]
    small += [_head_block_diag()]
    const = lambda a: _full(a.shape) if a.dtype == BF16 else _layer_item(a, layer)
    in_specs = ([_ctx_rows_spec(D_MODEL), _lat_rows_spec(D_MODEL),
                 pl.BlockSpec((8, D_MODEL), lambda t: (jnp.clip(lat8(t) - 1, 0, last8), 0)),
                 pl.BlockSpec((8, D_MODEL), lambda t: (jnp.clip(lat8(t + 1), 0, last8), 0)),
                 _mod_spec(layer),
                 _layer_block(w_small, layer)]
                + [pl.BlockSpec((tm, tab.shape[1]), tab_idx) for tab in tables]
                + [const(a) for a in small])
    return pl.pallas_call(
        _pre_kernel,
        grid=(N_TILES,),
        in_specs=in_specs,
        out_specs=[pl.BlockSpec((tm, w), lambda t: (t, 0)) for w in PRE_OUT_WIDTHS],
        out_shape=[jax.ShapeDtypeStruct((N_TOK, w), F32) for w in PRE_OUT_WIDTHS],
        compiler_params=pltpu.CompilerParams(dimension_semantics=("parallel",), vmem_limit_bytes=VMEM_LIMIT),
        name="mixer_prelude",
    )(x_ctx, x_lat, x_lat, x_lat, mod, w_small, *tables, *small)


ATT_Q_BLOCK = 128
MLA_Q_BLOCK = 256
MLA_LATENT_Q_BLOCK = 512
ATT_WINDOW = 128
ATT_NEG_INF = -1e30
CACHE_LEN = 512


def _softmax_pv(s, v, sink):
    dv = v.shape[1] // 2
    m = jnp.max(s, axis=-1, keepdims=True)
    if sink is not None:
        m = jnp.maximum(m, sink)
    e = jnp.exp((s - m).astype(BF16))
    o = jnp.dot(e, v, preferred_element_type=F32)
    den = o[:, dv:dv + 1]
    if sink is not None:
        den = den + jnp.exp(sink - m)
    return o[:, :dv] / den


def _with_ones(v):
    return jnp.concatenate([v.astype(BF16), jnp.ones(v.shape, BF16)], axis=1)


def _gqa_kernel(sink_ref, q_ref, k_ref, v_ref, *rest, hd, group, scale, windowed, sink_base):
    if windowed:
        kp_ref, kn_ref, vp_ref, vn_ref, kc_ref, vc_ref, _, o_ref = rest
    else:
        (o_ref,) = rest
    i = pl.program_id(1)
    tq = q_ref.shape[0]
    n_kv = k_ref.shape[1] // hd
    if windowed:
        qpos = i * tq + lax.broadcasted_iota(jnp.int32, (tq, 3 * tq), 0)
        kpos = (i - 1) * tq + lax.broadcasted_iota(jnp.int32, (tq, 3 * tq), 1)
        n_tok = pl.num_programs(1) * tq
        mask = (jnp.abs(qpos - kpos) <= ATT_WINDOW) & (kpos >= 0) & (kpos < n_tok)
        mask = jnp.concatenate([mask] * group, axis=0)
    for kvh in range(n_kv):
        ks = slice(kvh * hd, (kvh + 1) * hd)
        qs = [q_ref[:, (kvh * group + g) * hd:(kvh * group + g + 1) * hd] for g in range(group)]
        q = (jnp.concatenate(qs, axis=0) * scale).astype(BF16)
        sink = jnp.concatenate(
            [jnp.full((tq, 1), sink_ref[sink_base + kvh * group + g], F32) for g in range(group)], axis=0)
        if windowed:
            k_win = jnp.concatenate([kp_ref[:, ks], k_ref[:, ks], kn_ref[:, ks]], axis=0)
            v_win = jnp.concatenate([vp_ref[:, ks], v_ref[:, ks], vn_ref[:, ks]], axis=0)
            s_win = _bdot(q, k_win.astype(BF16), _D2T)
            s_win = jnp.where(mask, s_win, ATT_NEG_INF)
            s_ctx = _bdot(q, kc_ref[0, 0, kvh].astype(BF16), _D2T)
            s = jnp.concatenate([s_win, s_ctx], axis=1)
            v = jnp.concatenate([v_win, vc_ref[0, 0, kvh]], axis=0)
        else:
            s = _bdot(q, k_ref[:, ks].astype(BF16), _D2T)
            v = v_ref[:, ks]
        o = _softmax_pv(s, _with_ones(v), sink)
        for g in range(group):
            h = kvh * group + g
            o_ref[:, h * hd:(h + 1) * hd] = o[g * tq:(g + 1) * tq]


def gqa_attention(q, k, v, sink, cache_k, cache_v, layer):
    qw, kw = q.shape[1], k.shape[1]
    group = qw // kw
    scale = A_HD ** -0.5
    params = pltpu.CompilerParams(dimension_semantics=("parallel", "parallel"))
    out_shape = jax.ShapeDtypeStruct((N_TOK, qw), F32)
    ctx_spec = lambda w: pl.BlockSpec((CTX_LEN, w), lambda s, i, sk: (s, 0))
    o = pl.pallas_call(
        functools.partial(_gqa_kernel, hd=A_HD, group=group, scale=scale, windowed=False, sink_base=layer * A_HEADS),
        grid_spec=pltpu.PrefetchScalarGridSpec(
            num_scalar_prefetch=1, grid=(N_CTX_SEQ, 1), in_specs=[ctx_spec(qw), ctx_spec(kw), ctx_spec(kw)],
            out_specs=ctx_spec(qw)),
        out_shape=out_shape, compiler_params=params, name="gqa_full",
    )(sink, q, k, v)
    tq = ATT_Q_BLOCK
    nb = LAT_LEN // tq
    base = N_CTX // tq
    blk = lambda w, f: pl.BlockSpec((tq, w), lambda b, i, sk: (base + nb * b + f(i), 0))
    same = lambda i: i
    prev = lambda i: jnp.maximum(i - 1, 0)
    nxt = lambda i: jnp.minimum(i + 1, nb - 1)
    cspec = pl.BlockSpec((1, 1) + cache_k.shape[2:], lambda b, i, sk: (b, layer, 0, 0, 0))
    return pl.pallas_call(
        functools.partial(_gqa_kernel, hd=A_HD, group=group, scale=scale, windowed=True, sink_base=layer * A_HEADS),
        grid_spec=pltpu.PrefetchScalarGridSpec(
            num_scalar_prefetch=1, grid=(N_LAT_SEQ, nb),
            in_specs=[blk(qw, same), blk(kw, same), blk(kw, same), blk(kw, prev), blk(kw, nxt), blk(kw, prev),
                      blk(kw, nxt), cspec, cspec, pl.BlockSpec(memory_space=pl.ANY)],
            out_specs=blk(qw, same)),
        out_shape=out_shape, input_output_aliases={10: 0}, compiler_params=params, name="gqa_windowed",
    )(sink, q, k, v, k, k, v, v, cache_k, cache_v, o)


def _mla_kernel(q_ref, ckv_ref, kpe_ref, wukv_ref, *rest, n_heads, nope, rope, vd, scale, cached):
    if cached:
        cckv_ref, ckpe_ref, _, o_ref, k_scr, vext_scr = rest
    else:
        o_ref, k_scr, vext_scr = rest
    i = pl.program_id(1)
    n_cache = k_scr.shape[0] - ckv_ref.shape[0]
    hw = nope + vd
    hl = MLA_HEAD_LANES

    @pl.when(i == 0)
    def _():
        w = wukv_ref[...].astype(BF16)

        def expand(rows, kpe_rows, lo, hi):
            kv = jnp.dot(rows.astype(BF16), w, preferred_element_type=F32).astype(BF16)
            n = hi - lo
            kpe = kpe_rows.astype(BF16)
            for h in range(n_heads):
                k_scr[lo:hi, hl * h:hl * (h + 1)] = jnp.concatenate(
                    [kv[:, h * hw:h * hw + nope], kpe, jnp.zeros((n, hl - nope - rope), BF16)], axis=1)
                vext_scr[lo:hi, 2 * vd * h:2 * vd * (h + 1)] = jnp.concatenate(
                    [kv[:, h * hw + nope:(h + 1) * hw], jnp.ones((n, vd), BF16)], axis=1)

        if cached:
            expand(cckv_ref[0, 0], ckpe_ref[0, 0], 0, n_cache)
        expand(ckv_ref[...], kpe_ref[...], n_cache, k_scr.shape[0])

    for h in range(n_heads):
        qh = (q_ref[:, hl * h:hl * (h + 1)] * scale).astype(BF16)
        s = _bdot(qh, k_scr[:, hl * h:hl * (h + 1)], _D2T)
        o_ref[:, h * vd:(h + 1) * vd] = _softmax_pv(s, vext_scr[:, 2 * vd * h:2 * vd * (h + 1)], None)


def mla_attention(q, ckv, kpe, w_ukv, cache_ckv, cache_kpe, layer):
    qw = q.shape[1]
    tq = MLA_Q_BLOCK
    kw = dict(n_heads=B_HEADS, nope=B_NOPE, rope=B_ROPE, vd=B_VD, scale=(B_NOPE + B_ROPE) ** -0.5)
    params = pltpu.CompilerParams(dimension_semantics=("parallel", "arbitrary"))
    out_shape = jax.ShapeDtypeStruct((N_TOK, B_HEADS * B_VD), F32)
    scratch = lambda rows: [pltpu.VMEM((rows, B_HEADS * MLA_HEAD_LANES), BF16),
                            pltpu.VMEM((rows, 2 * B_HEADS * B_VD), BF16)]
    nbc = CTX_LEN // tq
    o = pl.pallas_call(
        functools.partial(_mla_kernel, cached=False, **kw),
        grid=(N_CTX_SEQ, nbc),
        in_specs=[pl.BlockSpec((tq, qw), lambda s, i: (s * nbc + i, 0)),
                  pl.BlockSpec((CTX_LEN, B_KV_LORA), lambda s, i: (s, 0)),
                  pl.BlockSpec((CTX_LEN, B_ROPE), lambda s, i: (s, 0)),
                  _layer_item(w_ukv, layer)],
        out_specs=pl.BlockSpec((tq, B_HEADS * B_VD), lambda s, i: (s * nbc + i, 0)),
        out_shape=out_shape,
        scratch_shapes=scratch(CTX_LEN),
        compiler_params=params, name="mla_context",
    )(q, ckv, kpe, w_ukv)
    tq = MLA_LATENT_Q_BLOCK
    nb = LAT_LEN // tq
    base = N_CTX // tq
    lat0 = N_CTX // LAT_LEN
    s_len = CACHE_LEN + LAT_LEN
    return pl.pallas_call(
        functools.partial(_mla_kernel, cached=True, **kw),
        grid=(N_LAT_SEQ, nb),
        in_specs=[pl.BlockSpec((tq, qw), lambda b, i: (base + nb * b + i, 0)),
                  pl.BlockSpec((LAT_LEN, B_KV_LORA), lambda b, i: (lat0 + b, 0)),
                  pl.BlockSpec((LAT_LEN, B_ROPE), lambda b, i: (lat0 + b, 0)),
                  _layer_item(w_ukv, layer),
                  pl.BlockSpec((1, 1, CACHE_LEN, B_KV_LORA), lambda b, i: (b, layer, 0, 0)),
                  pl.BlockSpec((1, 1, CACHE_LEN, B_ROPE), lambda b, i: (b, layer, 0, 0)),
                  pl.BlockSpec(memory_space=pl.ANY)],
        out_specs=pl.BlockSpec((tq, B_HEADS * B_VD), lambda b, i: (base + nb * b + i, 0)),
        out_shape=out_shape, input_output_aliases={6: 0},
        scratch_shapes=scratch(s_len),
        compiler_params=params, name="mla_latent",
    )(q, ckv, kpe, w_ukv, cache_ckv, cache_kpe, o)


CHUNK = 64
GLA_SUB = 16
CTX_SEQS_PER_STEP = 8
LAT_SEQS_PER_STEP = N_LAT_SEQ
CHAINS_PER_SEQ = 2 * 4


def _is_back(shape):
    return (lax.broadcasted_iota(jnp.int32, shape, 0) // 4) % 2 == 1


def _chains(ref_f, ref_b, width):
    return jnp.stack([ref[0, s, 0, :, h * width:(h + 1) * width]
                      for s in range(ref_f.shape[1]) for ref in (ref_f, ref_b) for h in range(4)], axis=0)


def _unchain(y, o_f, o_b):
    for s in range(o_f.shape[1]):
        o_f[0, s, 0] = jnp.concatenate([y[s * 8 + h] for h in range(4)], axis=-1)
        o_b[0, s, 0] = jnp.concatenate([y[s * 8 + 4 + h] for h in range(4)], axis=-1)


def _dir_masks(n_chain, L):
    shape = (n_chain, L, L)
    back = _is_back(shape)
    row = lax.broadcasted_iota(jnp.int32, shape, 1)
    col = lax.broadcasted_iota(jnp.int32, shape, 2)
    ahead = jnp.where(back, col - row, row - col)
    return ahead >= 0, ahead > 0, row == col


def _chunk_end(ci):
    L = ci.shape[1]
    return jnp.where(_is_back((ci.shape[0], 1, 1)), ci[:, 0:1], ci[:, L - 1:L])


def _split_refs(refs, n_in, has_s0, has_sfin):
    ins = refs[:n_in]
    pos = n_in
    s0_ref = None
    if has_s0:
        s0_ref = refs[pos]
        pos += 3
    of_ref, ob_ref = refs[pos], refs[pos + 1]
    pos += 2
    sfin_ref = refs[pos] if has_sfin else None
    return ins, s0_ref, of_ref, ob_ref, sfin_ref, refs[-1]


def _init_state(s_scr, s0_ref):
    @pl.when(pl.program_id(1) == 0)
    def _():
        if s0_ref is None:
            s_scr[...] = jnp.zeros_like(s_scr)
        else:
            s_scr[...] = s0_ref[0]


def _emit_state(sfin_ref, s_new):
    if sfin_ref is None:
        return

    @pl.when(pl.program_id(1) == pl.num_programs(1) - 1)
    def _():
        sfin_ref[0] = s_new


def _rwkv_kernel(*refs, dot, has_s0, has_sfin):
    (rf, rb, vf, vb, kkf, kkb, lwf, lwb, kf, kb, af, ab), s0_ref, yf_ref, yb_ref, sfin_ref, s_scr = _split_refs(
        refs, 12, has_s0, has_sfin)
    _init_state(s_scr, s0_ref)
    n = D_N
    r = _chains(rf, rb, n)
    v = _chains(vf, vb, n)
    kk = _chains(kkf, kkb, n)
    lw = _chains(lwf, lwb, n)
    k = _chains(kf, kb, n)
    a = _chains(af, ab, n)
    L = r.shape[1]
    S = s_scr[...]
    incl, strict, diag = _dir_masks(r.shape[0], L)
    ci = _dot_exact_lhs(jnp.where(incl, 1.0, 0.0), lw, _NN)
    ce = ci - lw
    cl = _chunk_end(ci)
    e_neg = jnp.exp(-ci)
    b = a * kk
    alpha = kk * jnp.exp(ce)
    rho = r * jnp.exp(ci)
    beta = b * e_neg
    kappa = k * e_neg
    e_end = jnp.exp(cl - ci)
    ar = jnp.concatenate([alpha, rho], axis=1)
    bk = jnp.concatenate([beta, kappa], axis=1)
    w = dot(ar, bk, _NT)
    nmat = jnp.where(strict, w[:, :L, :L], 0.0)
    mmat = jnp.where(strict, w[:, :L, L:], 0.0)
    p1 = jnp.where(incl, w[:, L:, :L], 0.0)
    p2 = jnp.where(incl, w[:, L:, L:], 0.0)
    x = jnp.where(diag, 1.0, 0.0) - nmat
    p = dot(nmat, nmat, _NN)
    span = 2
    while True:
        x = x + dot(x, p, _NN)
        span *= 2
        if span >= L:
            break
        p = dot(p, p, _NN)
    us = dot(ar, S, _NT)
    rhs = us[:, :L] + dot(mmat, v, _NN)
    d = -dot(x, rhs, _NN)
    dv = jnp.concatenate([d, v], axis=1)
    pp = jnp.concatenate([p1, p2], axis=2)
    _unchain(us[:, L:] + dot(pp, dv, _NN), yf_ref, yb_ref)
    bk_end = jnp.concatenate([b * e_end, k * e_end], axis=1)
    s_new = S * jnp.exp(cl) + dot(dv, bk_end, _TN)
    s_scr[...] = s_new
    _emit_state(sfin_ref, s_new)


def _gla_kernel(*refs, dot, has_s0, has_sfin):
    (qf, qb, kf, kb, vf, vb, laf, lab), s0_ref, of_ref, ob_ref, sfin_ref, s_scr = _split_refs(
        refs, 8, has_s0, has_sfin)
    _init_state(s_scr, s0_ref)
    q4 = _chains(qf, qb, 128)
    k4 = _chains(kf, kb, 128)
    la4 = _chains(laf, lab, 128)
    v = _chains(vf, vb, C_DV)
    g, L, lanes = q4.shape
    dk = C_DK
    n_sub = L // GLA_SUB
    st = s_scr[...]
    incl, _, _ = _dir_masks(g, L)
    c = _dot_exact_lhs(jnp.where(incl, 1.0, 0.0), la4, _NN)
    shape = (g, L, lanes)
    back = _is_back(shape)
    lane_blk = lax.broadcasted_iota(jnp.int32, shape, 2) // dk
    row_blk = lax.broadcasted_iota(jnp.int32, shape, 1) // GLA_SUB
    cref_f = jnp.zeros(shape, F32)
    cref_b = jnp.zeros(shape, F32)
    for j in range(1, n_sub):
        cref_f = jnp.where(lane_blk == j, c[:, j * GLA_SUB - 1:j * GLA_SUB], cref_f)
        cref_b = jnp.where(lane_blk == j - 1, c[:, j * GLA_SUB:j * GLA_SUB + 1], cref_b)
    cref = jnp.where(back, cref_b, cref_f)
    q_on = row_blk == lane_blk
    k_on = jnp.where(back, row_blk - lane_blk, lane_blk - row_blk) >= 0
    qh = jnp.where(q_on, q4 * jnp.exp(jnp.where(q_on, c - cref, 0.0)), 0.0)
    kh = jnp.where(k_on, k4 * jnp.exp(jnp.where(k_on, cref - c, 0.0)), 0.0)
    att = jnp.where(incl, dot(qh, kh, _NT), 0.0)
    cl = _chunk_end(c)
    qe = (q4 * jnp.exp(c))[:, :, :dk]
    ke = (k4 * jnp.exp(cl - c))[:, :, :dk]
    _unchain(dot(qe, st, _NT) + dot(att, v, _NN), of_ref, ob_ref)
    s_new = st * jnp.exp(cl[:, :, :dk]) + dot(v, ke, _TN)
    s_scr[...] = s_new
    _emit_state(sfin_ref, s_new)


def _recurrence_calls(kernel_fn, name, pairs, singles_f, singles_b, s0_lat, state_dims, out_width):
    def run(view, grid, group, s0, prev_out):
        seqs, nc = view[1], view[2]
        n_chain = seqs * CHAINS_PER_SEQ
        fwd_map = lambda p, c: (group(p), 0, c, 0, 0)
        bwd_map = lambda p, c: (group(p), 0, nc - 1 - c, 0, 0)
        blk = lambda w: (1, seqs, 1, CHUNK, w)
        args, in_specs = [], []
        for af, ab in [(a, a) for a in pairs] + list(zip(singles_f, singles_b)):
            w = af.shape[-1]
            args += [af.reshape(view + (w,)), ab.reshape(view + (w,))]
            in_specs += [pl.BlockSpec(blk(w), fwd_map), pl.BlockSpec(blk(w), bwd_map)]
        out_specs = [pl.BlockSpec(blk(out_width), fwd_map), pl.BlockSpec(blk(out_width), bwd_map)]
        out_shape = [jax.ShapeDtypeStruct(view + (out_width,), F32)] * 2
        aliases = {}
        if s0 is not None:
            args += [s0] + [o.reshape(view + (out_width,)) for o in prev_out]
            in_specs += [_full(s0.shape), pl.BlockSpec(memory_space=pl.ANY), pl.BlockSpec(memory_space=pl.ANY)]
            aliases = {len(args) - 2: 0, len(args) - 1: 1}
        else:
            out_specs.append(pl.BlockSpec((1, n_chain) + state_dims, lambda p, c: (p, 0, 0, 0)))
            out_shape.append(jax.ShapeDtypeStruct((grid[0], n_chain) + state_dims, F32))
        return pl.pallas_call(
            functools.partial(kernel_fn, has_s0=s0 is not None, has_sfin=s0 is None),
            grid=grid, in_specs=in_specs, out_specs=out_specs, out_shape=out_shape,
            input_output_aliases=aliases, scratch_shapes=[pltpu.VMEM((n_chain,) + state_dims, F32)],
            compiler_params=pltpu.CompilerParams(dimension_semantics=("parallel", "arbitrary")),
            name=name + ("_latent" if s0 is not None else "_context"),
        )(*args)

    ctx_nc = CTX_LEN // CHUNK
    cs, ls = CTX_SEQS_PER_STEP, LAT_SEQS_PER_STEP
    ctx_view = (N_TOK // (cs * CTX_LEN), cs, ctx_nc, CHUNK)
    o_f, o_b, s_fin = run(ctx_view, (N_CTX_SEQ // cs, ctx_nc), lambda p: p, None, None)
    lat_nc = LAT_LEN // CHUNK
    lat_view = (N_TOK // (ls * LAT_LEN), ls, lat_nc, CHUNK)
    o_f, o_b = run(lat_view, (1, lat_nc), lambda p: N_CTX // (ls * LAT_LEN), s0_lat, (o_f, o_b))
    return o_f.reshape(N_TOK, out_width), o_b.reshape(N_TOK, out_width), s_fin


def _layer_norm(x, g, b):
    mu = jnp.mean(x, axis=-1, keepdims=True)
    xc = x - mu
    var = jnp.mean(xc * xc, axis=-1, keepdims=True)
    return xc * lax.rsqrt(var + LN_EPS) * g + b


def _merge_kernel(xc_ref, xl_ref, mod_ref, oa_ref, ob_ref, cof_ref, cob_ref, cgate_ref, yf_ref, yb_ref, bonus_ref,
                  dgate_ref, wg_ref, wbr_ref, wout_ref, cnorm_ref, dlng_ref, dlnb_ref, lng_ref, lnb_ref, wr_ref, br_ref,
                  bd_ref, x1_o, h2_o, topi_o, topw_o):
    x = _tile_rows(pl.program_id(0), xc_ref, xl_ref)
    m = mod_ref[0]
    sh1, sc1, g1, sh2, sc2 = m[0:1], m[1:2], m[2:3], m[3:4], m[4:5]
    bd = bd_ref[...]
    inv_n = 1.0 / D_N
    co = cof_ref[...] + cob_ref[...]
    o_c = co * lax.rsqrt(_dot_exact_rhs(co * co, bd) * inv_n + RMS_EPS) * cnorm_ref[...] * cgate_ref[...]
    y = yf_ref[...] + yb_ref[...]
    yc = y - _dot_exact_rhs(y, bd) * inv_n
    var = _dot_exact_rhs(yc * yc, bd) * inv_n
    o_d = (yc * lax.rsqrt(var + D_GN_EPS) * dlng_ref[...] + dlnb_ref[...] + bonus_ref[...]) * dgate_ref[...]
    branches = [b.astype(BF16) for b in (oa_ref[...], ob_ref[...], o_c, o_d)]
    h = (x * (1.0 + sc1) + sh1).astype(BF16)
    blocks = []
    for cb in range(D_MODEL // MXU_WIDTH):
        merged = None
        for n in range(N_BRANCH):
            cols = slice(n * D_MODEL + cb * MXU_WIDTH, n * D_MODEL + (cb + 1) * MXU_WIDTH)
            gate = _sigmoid(jnp.dot(h, wg_ref[0, :, cols], preferred_element_type=F32))
            term = gate * jnp.dot(branches[n], wbr_ref[0, n, :, cb * MXU_WIDTH:(cb + 1) * MXU_WIDTH],
                                  preferred_element_type=F32)
            merged = term if merged is None else merged + term
        blocks.append(merged.astype(BF16))
    mix = jnp.dot(jnp.concatenate(blocks, axis=1), wout_ref[0], preferred_element_type=F32)
    x1 = _layer_norm(ALPHA * x + g1 * mix, lng_ref[...], lnb_ref[...])
    x1_o[...] = x1
    h2 = x1 * (1.0 + sc2) + sh2
    h2_o[...] = h2.astype(BF16)
    logits = _dot3(h2, wr_ref[...]) + br_ref[...]
    tm, n_e = logits.shape
    lane_e = lax.broadcasted_iota(jnp.int32, (tm, n_e), 1)
    lane_o = lax.broadcasted_iota(jnp.int32, (tm, topi_o.shape[1]), 1)
    top_i = jnp.zeros((tm, topi_o.shape[1]), jnp.int32)
    top_v = jnp.zeros((tm, topw_o.shape[1]), F32)
    vals = []
    for kth in range(TOP_K):
        mx = jnp.max(logits, axis=-1, keepdims=True)
        idx = jnp.min(jnp.where(logits == mx, lane_e, n_e), axis=-1, keepdims=True)
        vals.append(mx)
        top_i = jnp.where(lane_o == kth, idx, top_i)
        logits = jnp.where(lane_e == idx, -jnp.inf, logits)
    es = [jnp.exp(vk - vals[0]) for vk in vals]
    den = es[0] + es[1] + es[2] + es[3]
    for kth in range(TOP_K):
        top_v = jnp.where(lane_o == kth, es[kth] / den, top_v)
    topi_o[...] = top_i
    topw_o[...] = top_v


def merge_and_route(x_ctx, x_lat, mod, o_a, o_b, co_f, co_b, cgate, y_f, y_b, bonus, dgate, w_g, w_br, w_out, layer, sp):
    tm = MERGE_ROW_TILE
    hn = D_HEADS * D_N
    row = lambda w: pl.BlockSpec((tm, w), lambda t: (t, 0))
    small = [sp[k] for k in ('c_norm', 'd_ln_g', 'd_ln_b', 'ln_g', 'ln_b', 'w_router', 'b_router')]
    index = [layer, layer, layer, 2 * layer, 2 * layer, layer, layer]
    bd = _head_block_diag()
    return pl.pallas_call(
        _merge_kernel,
        grid=(N_TOK // tm,),
        in_specs=([_ctx_rows_spec(D_MODEL, tm), _lat_rows_spec(D_MODEL, tm), _mod_spec(layer, tm)]
                  + [row(hn)] * 9 + [_layer_block(w, layer) for w in (w_g, w_br, w_out)]
                  + [_layer_item(a, i) for a, i in zip(small, index)] + [_full(bd.shape)]),
        out_specs=[row(D_MODEL), row(D_MODEL), row(128), row(128)],
        out_shape=[jax.ShapeDtypeStruct((N_TOK, D_MODEL), F32), jax.ShapeDtypeStruct((MOE_ROWS, D_MODEL), BF16),
                   jax.ShapeDtypeStruct((N_TOK, 128), jnp.int32), jax.ShapeDtypeStruct((N_TOK, 128), F32)],
        compiler_params=pltpu.CompilerParams(dimension_semantics=("parallel",), vmem_limit_bytes=VMEM_LIMIT),
        name="merge_and_route",
    )(x_ctx, x_lat, mod, o_a, o_b, co_f, co_b, cgate, y_f, y_b, bonus, dgate, w_g, w_br, w_out, *small, bd)


def _moe_kernel(te_ref, tv_ref, first_ref, slot_ref, next_ref, x_ref, w1_hbm, b1_ref, w2_hbm, b2_ref, perm_ref, *rest,
                layer):
    y_ref, w1buf, w2buf, sem, w1s, w2s, hs = rest[-7:]
    t = pl.program_id(0)
    valid = tv_ref[t] != 0
    d_model, two_f = w1s.shape
    n_blk = two_f // MXU_WIDTH
    half = MXU_WIDTH // 2

    def fetch(expert, slot):
        return (pltpu.make_async_copy(w1_hbm.at[layer, expert], w1buf.at[slot], sem.at[0, slot]),
                pltpu.make_async_copy(w2_hbm.at[layer, expert], w2buf.at[slot], sem.at[1, slot]))

    @pl.when(t == 0)
    def _():
        for cp in fetch(te_ref[0], 0):
            cp.start()

    @pl.when(first_ref[t] == 1)
    def _():
        slot = slot_ref[t]
        for cp in fetch(te_ref[t], slot):
            cp.wait()

        @pl.when(next_ref[t] >= 0)
        def _():
            for cp in fetch(next_ref[t], 1 - slot):
                cp.start()

        for blk in range(n_blk):
            sl = slice(blk * MXU_WIDTH, (blk + 1) * MXU_WIDTH)
            wb = w1buf[slot, :, sl].astype(BF16)
            w1s[:, sl] = jnp.dot(wb, perm_ref[...], preferred_element_type=F32).astype(BF16)
        w2s[...] = w2buf[slot].astype(BF16)

    @pl.when(valid)
    def _():
        x = x_ref[...]
        for blk in range(n_blk):
            sl = slice(blk * MXU_WIDTH, (blk + 1) * MXU_WIDTH)
            u = jnp.dot(x, w1s[:, sl], preferred_element_type=F32) + b1_ref[0, 0, :, sl]
            glu = jnp.minimum(u[:, :half], SWIGLU_LIMIT)
            lin = jnp.clip(u[:, half:], -SWIGLU_LIMIT, SWIGLU_LIMIT)
            hs[:, blk * half:(blk + 1) * half] = (glu * _sigmoid(SWIGLU_ALPHA * glu) * (lin + 1.0)).astype(BF16)
        y = jnp.dot(hs[...], w2s[...], preferred_element_type=F32) + b2_ref[0, 0]
        y_ref[...] = y.astype(y_ref.dtype)

    @pl.when(jnp.logical_not(valid))
    def _():
        y_ref[...] = jnp.zeros_like(y_ref)


def _deinterleave_perm():
    half = MXU_WIDTH // 2
    src = np.arange(MXU_WIDTH)
    dst = np.where(src % 2 == 0, src // 2, half + src // 2)
    p = np.zeros((MXU_WIDTH, MXU_WIDTH), np.float32)
    p[src, dst] = 1.0
    return jnp.asarray(p, BF16)


def _moe_dispatch(top_i):
    n, k = top_i.shape
    tm = MOE_ROW_TILE
    p_rows = n * k + N_EXPERTS * tm
    experts = jnp.arange(N_EXPERTS, dtype=jnp.int32)
    onehot = top_i[:, :, None] == experts
    sel = jnp.sum(onehot.astype(jnp.int32), axis=1)
    before = jnp.cumsum(sel, axis=0) - sel
    counts = jnp.sum(sel, axis=0)
    padded = ((counts + tm - 1) // tm) * tm
    ends = jnp.cumsum(padded)
    starts = ends - padded
    pos = jnp.sum(jnp.where(onehot, (before + starts)[:, None, :], 0), axis=-1)
    n_tiles = p_rows // tm
    tile_start = jnp.arange(n_tiles, dtype=jnp.int32) * tm
    tile_valid = (tile_start < ends[-1]).astype(jnp.int32)
    last_tile = ends[-1] // tm - 1
    tile_expert = jnp.sum(ends[None, :] <= jnp.minimum(tile_start, last_tile * tm)[:, None], axis=1).astype(jnp.int32)
    keys = jnp.sort((top_i * n + jnp.arange(n, dtype=jnp.int32)[:, None]).reshape(-1))
    tile_onehot = tile_expert[:, None] == experts[None, :]
    lookup = lambda table: jnp.sum(jnp.where(tile_onehot, table[None, :], 0), axis=1)
    tile_rank0 = tile_start - lookup(starts)
    rank = tile_rank0[:, None] + jnp.arange(tm, dtype=jnp.int32)[None, :]
    sorted_at = jnp.clip(lookup(jnp.cumsum(counts) - counts)[:, None] + rank, 0, n * k - 1)
    tile_keys = keys[sorted_at.reshape(-1)].reshape(n_tiles, tm)
    filler = (tile_start[:, None] + jnp.arange(tm, dtype=jnp.int32)[None, :]) % n
    src_tok = jnp.where(rank < lookup(counts)[:, None], tile_keys % n, filler)
    tables = []
    for lo, hi in zip(MOE_SPLIT_TILES[:-1], MOE_SPLIT_TILES[1:]):
        nst = hi - lo
        idx = jnp.arange(nst, dtype=jnp.int32)
        te_h = tile_expert[lo:hi]
        is_first = jnp.concatenate([jnp.ones((1,), jnp.int32), (te_h[1:] != te_h[:-1]).astype(jnp.int32)])
        slot = (jnp.cumsum(is_first) - 1) % 2
        nxt = jnp.min(jnp.where(jnp.logical_and(idx[None, :] > idx[:, None], is_first[None, :] == 1),
                                idx[None, :], nst), axis=1)
        next_expert = jnp.sum(jnp.where(idx[None, :] == nxt[:, None], te_h[None, :] + 1, 0), axis=1) - 1
        tables.append((te_h, tile_valid[lo:hi], is_first, slot.astype(jnp.int32), next_expert.astype(jnp.int32)))
    return pos, src_tok.reshape(-1), tables, p_rows


def moe_experts(h2, top_i, layer, w1, b1, w2, b2):
    n = top_i.shape[0]
    d = h2.shape[1]
    depth, e, _, two_f = w1.shape
    f = two_f // 2
    tm = MOE_ROW_TILE
    pos, src_tok, tables, p_rows = _moe_dispatch(top_i)
    assert h2.shape[0] == p_rows
    src_tok = lax.optimization_barrier(src_tok)
    b1p = b1.reshape(depth, e, two_f // MXU_WIDTH, MXU_WIDTH // 2, 2).swapaxes(3, 4).reshape(depth, e, 1, two_f)
    b2r = b2.reshape(depth, e, 1, d)
    expert_vec = lambda w: pl.BlockSpec((1, 1, 1, w), lambda t, te, *_: (layer, te[t], 0, 0))
    spans = list(zip(MOE_SPLIT_TILES[:-1], MOE_SPLIT_TILES[1:]))
    assert MOE_SPLIT_TILES[0] == 0 and MOE_SPLIT_TILES[-1] * tm == p_rows
    xs = [h2.at[src_tok[lo * tm:hi * tm]].get(mode="promise_in_bounds") for lo, hi in spans]
    ys = None
    for h, (lo, hi) in enumerate(spans):
        in_specs = [
            pl.BlockSpec((tm, d), lambda t, *_: (t, 0)),
            pl.BlockSpec(memory_space=pl.ANY),
            expert_vec(two_f),
            pl.BlockSpec(memory_space=pl.ANY),
            expert_vec(d),
            pl.BlockSpec((MXU_WIDTH, MXU_WIDTH), lambda t, *_: (0, 0)),
        ]
        args = [*tables[h], xs[h], w1, b1p, w2, b2r, _deinterleave_perm()]
        aliases = {}
        if ys is not None:
            in_specs.append(pl.BlockSpec(memory_space=pl.ANY))
            aliases = {len(args): 0}
            args.append(ys)
        ys = pl.pallas_call(
            functools.partial(_moe_kernel, layer=layer),
            grid_spec=pltpu.PrefetchScalarGridSpec(
                num_scalar_prefetch=len(tables[h]), grid=(hi - lo,), in_specs=in_specs,
                out_specs=pl.BlockSpec((tm, d), lambda t, *_, lo=lo: (lo + t, 0)),
                scratch_shapes=[pltpu.VMEM((2, d, two_f), F32), pltpu.VMEM((2, f, d), F32),
                                pltpu.SemaphoreType.DMA((2, 2)),
                                pltpu.VMEM((d, two_f), BF16), pltpu.VMEM((f, d), BF16), pltpu.VMEM((tm, f), BF16)]),
            out_shape=jax.ShapeDtypeStruct((p_rows, d), BF16),
            input_output_aliases=aliases,
            compiler_params=pltpu.CompilerParams(dimension_semantics=("arbitrary",),
                                                 vmem_limit_bytes=48 * 1024 * 1024),
            name="moe_experts",
        )(*args)
    return ys, pos


def _final_kernel(x1_ref, mod_ref, ys_ref, topw_ref, lng_ref, lnb_ref, o_ref):
    g2 = mod_ref[0, 5:6]
    moe = None
    for kth in range(TOP_K):
        term = ys_ref[kth].astype(F32) * topw_ref[:, kth:kth + 1]
        moe = term if moe is None else moe + term
    o_ref[...] = _layer_norm(ALPHA * x1_ref[...] + g2 * moe, lng_ref[...], lnb_ref[...])


def combine_and_norm(x1, mod, ys, pos, top_w, layer, sp):
    tm = ROW_TILE
    ln_g, ln_b = sp['ln_g'], sp['ln_b']
    outs = []
    for t0, n_rows in ((0, N_CTX), (CTX_TILES, N_TOK - N_CTX)):
        idx = lax.optimization_barrier(pos[t0 * tm:t0 * tm + n_rows].T.reshape(-1))
        rows = ys.at[idx].get(mode="promise_in_bounds").reshape(TOP_K, n_rows, D_MODEL)
        outs.append(pl.pallas_call(
            _final_kernel,
            grid=(n_rows // tm,),
            in_specs=[pl.BlockSpec((tm, D_MODEL), lambda t, t0=t0: (t0 + t, 0)),
                      pl.BlockSpec((1, 6, D_MODEL),
                                   lambda t, t0=t0: (layer * (1 + N_LAT_SEQ) + _mod_row(t0 + t), 0, 0)),
                      pl.BlockSpec((TOP_K, tm, D_MODEL), lambda t: (0, t, 0)),
                      pl.BlockSpec((tm, 128), lambda t, t0=t0: (t0 + t, 0)),
                      _layer_item(ln_g, 2 * layer + 1), _layer_item(ln_b, 2 * layer + 1)],
            out_specs=pl.BlockSpec((tm, D_MODEL), lambda t: (t, 0)),
            out_shape=jax.ShapeDtypeStruct((n_rows, D_MODEL), F32),
            compiler_params=pltpu.CompilerParams(dimension_semantics=("parallel",)),
            name="combine_and_norm",
        )(x1, mod, rows, top_w, ln_g, ln_b))
    return outs


def kernel(x_prompt, x_sample, cache_a_k, cache_a_v, cache_b_ckv, cache_b_kpe, state_c, state_d, c,
           c_ctx, w_mod, b_mod, w_in, a_sink, b_q_norm, b_w_uq, b_kv_norm, b_w_ukv, c_w_gate, c_b_gate,
           c_norm, d_mu, d_w0, d_w2, d_a0, d_a2, d_g2, d_k_k, d_k_a, d_r_k, d_ln_g, d_ln_b, w_br, w_out,
           ln_g, ln_b, w_router, b_router, w_mlp1, b_mlp1, w_mlp2, b_mlp2):
    sp = stacked_params(dict(
        a_sink=a_sink, b_q_norm=b_q_norm, b_w_uq=b_w_uq, b_kv_norm=b_kv_norm, b_w_ukv=b_w_ukv, c_w_gate=c_w_gate,
        c_b_gate=c_b_gate, c_norm=c_norm, d_mu=d_mu, d_w0=d_w0, d_w2=d_w2, d_a0=d_a0, d_a2=d_a2, d_g2=d_g2,
        d_k_k=d_k_k, d_k_a=d_k_a, d_r_k=d_r_k, d_ln_g=d_ln_g, d_ln_b=d_ln_b, ln_g=ln_g, ln_b=ln_b,
        w_router=w_router, b_router=b_router))
    assert x_prompt.shape == (N_CTX_SEQ, CTX_LEN, D_MODEL) and x_sample.shape == (N_LAT_SEQ, LAT_LEN, D_MODEL)
    x_ctx, x_lat = x_prompt.reshape(N_CTX, D_MODEL), x_sample.reshape(-1, D_MODEL)
    cond8 = jnp.concatenate([c_ctx[None], c, jnp.zeros((8 - 1 - N_LAT_SEQ, D_MODEL), F32)], axis=0)
    mod = modulation_table(cond8, w_mod, b_mod)[:, :1 + N_LAT_SEQ].reshape(DEPTH * (1 + N_LAT_SEQ), 6, D_MODEL)
    tables = _rope_tables()
    w_small, w_g = prepare_in_weights(w_in)
    w_br_bf, w_out_bf = w_br.astype(BF16), w_out.astype(BF16)
    new = {name: [] for name in ("a_k", "a_v", "b_ckv", "b_kpe", "c", "d")}
    for l in range(DEPTH):
        (aq, ak, av, bq, bckv, bkpe, cq4, ck4, cla_f, cla_b, cv, cgate,
         r, v, kk, lw_f, lw_b, k_f, k_b, a_f, a_b, bonus, dgate) = mixer_prelude(
             x_ctx, x_lat, mod, w_small, l, tables, sp)

        o_a = gqa_attention(aq, ak, av, sp['a_sink'], cache_a_k, cache_a_v, l)
        o_b = mla_attention(bq, bckv, bkpe, sp['b_w_ukv'], cache_b_ckv, cache_b_kpe, l)

        lat_chains = LAT_SEQS_PER_STEP * CHAINS_PER_SEQ
        c_s0 = jnp.swapaxes(state_c[:, l], 3, 4).reshape(1, lat_chains, C_DV, C_DK)
        co_f, co_b, c_fin = _recurrence_calls(functools.partial(_gla_kernel, dot=_dot1), "gla", [cq4, ck4, cv],
                                              [cla_f], [cla_b], c_s0, (C_DV, C_DK), C_HEADS * C_DV)
        d_s0 = state_d[:, l].reshape(1, lat_chains, D_N, D_N)
        y_f, y_b, d_fin = _recurrence_calls(functools.partial(_rwkv_kernel, dot=_dot1), "rwkv7", [r, v, kk],
                                            [lw_f, k_f, a_f], [lw_b, k_b, a_b], d_s0, (D_N, D_N), D_HEADS * D_N)

        x1, h2, top_i, top_w = merge_and_route(x_ctx, x_lat, mod, o_a, o_b, co_f, co_b, cgate, y_f, y_b, bonus, dgate,
                                               w_g, w_br_bf, w_out_bf, l, sp)
        ys, pos = moe_experts(h2, top_i[:, :TOP_K], l, w_mlp1, b_mlp1, w_mlp2, b_mlp2)
        x_ctx, x_lat = combine_and_norm(x1, mod, ys, pos, top_w, l, sp)

        new["a_k"].append(ak[:N_CTX].reshape(N_CTX_SEQ, CTX_LEN, A_KV_HEADS, A_HD).transpose(0, 2, 1, 3))
        new["a_v"].append(av[:N_CTX].reshape(N_CTX_SEQ, CTX_LEN, A_KV_HEADS, A_HD).transpose(0, 2, 1, 3))
        new["b_ckv"].append(bckv[:N_CTX].reshape(N_CTX_SEQ, CTX_LEN, B_KV_LORA))
        new["b_kpe"].append(bkpe[:N_CTX].reshape(N_CTX_SEQ, CTX_LEN, B_ROPE))
        new["c"].append(jnp.swapaxes(c_fin.reshape(N_CTX_SEQ, 2, C_HEADS, C_DV, C_DK), 3, 4))
        new["d"].append(d_fin.reshape(N_CTX_SEQ, 2, D_HEADS, D_N, D_N))
    y_prompt = x_ctx.reshape(x_prompt.shape)
    y_sample = x_lat.reshape(x_sample.shape)
    return (y_prompt, y_sample, *(jnp.stack(new[name], axis=1) for name in ("a_k", "a_v", "b_ckv", "b_kpe", "c", "d")))
```
